```python
import jax, jax.numpy as jnp
from jax import lax
import numpy as np

D_MODEL = 1024
BATCH = 8
SEQ = 16384
DEPTH = 1

PLE_DIM = 256
GRID_W = 64
ATTN_HEAD_DIM = 64
ATTN_HEADS = (D_MODEL // 2) // ATTN_HEAD_DIM
ATTN_KV_HEADS = ATTN_HEADS // 4
RET_HEAD_DIM = 128
RET_HEADS = (D_MODEL // 2) // RET_HEAD_DIM
ATTN_Q_W = ATTN_HEADS * ATTN_HEAD_DIM
ATTN_KV_W = ATTN_KV_HEADS * ATTN_HEAD_DIM
RET_W = RET_HEADS * RET_HEAD_DIM
IN_SPLITS = (ATTN_Q_W, ATTN_KV_W, ATTN_KV_W, RET_W, RET_W, RET_W, RET_W, D_MODEL, D_MODEL)
IN_W = ATTN_Q_W + 2 * ATTN_KV_W + 4 * RET_W + 2 * D_MODEL
D_FF = 4 * D_MODEL
Q_BLOCK = 128
RET_CHUNK = 128
ROPE_THETA = 10000.0
NORM_EPS = 1e-6
GN_EPS = 1e-5

kernel_name = "hybrid_gqa_retention_gated_encoder"


def rms_norm(x, gain):
    x32 = x.astype(jnp.float32)
    y = x32 * lax.rsqrt(jnp.mean(x32 * x32, axis=-1, keepdims=True) + NORM_EPS)
    return (y * gain.astype(jnp.float32)).astype(x.dtype)


def axial_rope_tables(seq_len, head_dim):
    rows = seq_len // GRID_W
    n_axis = head_dim // 4
    freqs = ROPE_THETA ** (-jnp.arange(n_axis, dtype=jnp.float32) / n_axis)
    row = jnp.repeat(jnp.arange(rows, dtype=jnp.float32), GRID_W)
    col = jnp.tile(jnp.arange(GRID_W, dtype=jnp.float32), rows)
    ang = jnp.concatenate([row[:, None] * freqs, col[:, None] * freqs], axis=-1)
    return jnp.cos(ang), jnp.sin(ang)


def apply_rope(x, cos, sin):
    x32 = x.astype(jnp.float32)
    x1, x2 = jnp.split(x32, 2, axis=-1)
    c = cos[None, :, None, :]
    s = sin[None, :, None, :]
    return jnp.concatenate([x1 * c - x2 * s, x1 * s + x2 * c], axis=-1).astype(x.dtype)


def gqa_attention(q, k, v):
    b, s, _, hd = q.shape
    g = ATTN_HEADS // ATTN_KV_HEADS
    nb = s // Q_BLOCK
    qb = q.reshape(b, nb, Q_BLOCK, ATTN_KV_HEADS, g, hd).transpose(1, 0, 3, 4, 2, 5)
    kt = k.transpose(0, 2, 1, 3)
    vt = v.transpose(0, 2, 1, 3)
    scale = hd ** -0.5

    def block(qi):
        sc = jnp.einsum('bkgqd,bksd->bkgqs', qi, kt).astype(jnp.float32) * scale
        pr = jax.nn.softmax(sc, axis=-1).astype(vt.dtype)
        return jnp.einsum('bkgqs,bksd->bkgqd', pr, vt)

    o = lax.map(block, qb)
    return o.transpose(1, 0, 4, 2, 3, 5).reshape(b, s, ATTN_HEADS * hd)


def retention_direction(q, k, v, log_gamma, include_diag):
    b, h, s, dk = q.shape
    dv = v.shape[-1]
    c = RET_CHUNK
    nc = s // c
    qc = q.reshape(b, h, nc, c, dk)
    kc = k.reshape(b, h, nc, c, dk)
    vc = v.reshape(b, h, nc, c, dv)
    pos = jnp.arange(c, dtype=jnp.float32)
    lg = log_gamma[:, None]
    diff = pos[:, None] - pos[None, :]
    mask = (diff >= 0) if include_diag else (diff > 0)
    dmask = jnp.where(mask[None], jnp.exp(lg[:, :, None] * jnp.maximum(diff, 0.0)[None]), 0.0)
    scores = jnp.einsum('bhnid,bhnjd->bhnij', qc, kc) * dmask[None, :, None]
    y = jnp.einsum('bhnij,bhnje->bhnie', scores, vc)
    k_dec = jnp.exp(lg * (c - 1 - pos))
    kv = jnp.einsum('bhnjd,bhnje->nbhde', kc * k_dec[None, :, None, :, None], vc)
    chunk_decay = jnp.exp(log_gamma * c)[None, :, None, None]

    def step(state, kv_n):
        return state * chunk_decay + kv_n, state

    _, prev = lax.scan(step, jnp.zeros((b, h, dk, dv), jnp.float32), kv)
    q_dec = jnp.exp(lg * (pos + 1.0))
    y = y + jnp.einsum('bhnid,nbhde->bhnie', qc * q_dec[None, :, None, :, None], prev)
    return y.reshape(b, h, s, dv)


def bidirectional_retention(q, k, v, decay_logit):
    log_g = jax.nn.log_sigmoid(decay_logit.astype(jnp.float32))
    qt = q.astype(jnp.float32).transpose(0, 2, 1, 3) * (RET_HEAD_DIM ** -0.5)
    kt = k.astype(jnp.float32).transpose(0, 2, 1, 3)
    vt = v.astype(jnp.float32).transpose(0, 2, 1, 3)
    fwd = retention_direction(qt, kt, vt, log_g[0], True)
    flip = lambda a: jnp.flip(a, axis=2)
    bwd = flip(retention_direction(flip(qt), flip(kt), flip(vt), log_g[1], False))
    return fwd + bwd


def hybrid_layer(x, p_i, mix_norm, w_in, attn_q_norm, attn_k_norm, ret_decay_logit, ret_norm_gain,
                 w_attn_o, w_ret_o, w_out, mlp_norm, w_up, w_down, ple_norm, w_ple_gate, w_ple,
                 rope_a, rope_r):
    b, s, _ = x.shape
    dt = x.dtype
    h = rms_norm(x, mix_norm)
    proj = h @ w_in
    split_idx = [int(i) for i in np.cumsum(IN_SPLITS)[:-1]]
    aq, ak, av, rq, rk, rv, rg, gate_a, gate_r = jnp.split(proj, split_idx, axis=-1)

    aq = rms_norm(aq.reshape(b, s, ATTN_HEADS, ATTN_HEAD_DIM), attn_q_norm)
    ak = rms_norm(ak.reshape(b, s, ATTN_KV_HEADS, ATTN_HEAD_DIM), attn_k_norm)
    av = av.reshape(b, s, ATTN_KV_HEADS, ATTN_HEAD_DIM)
    aq = apply_rope(aq, *rope_a)
    ak = apply_rope(ak, *rope_a)
    attn_out = gqa_attention(aq, ak, av).astype(dt) @ w_attn_o

    rq = apply_rope(rq.reshape(b, s, RET_HEADS, RET_HEAD_DIM), *rope_r)
    rk = apply_rope(rk.reshape(b, s, RET_HEADS, RET_HEAD_DIM), *rope_r)
    rv = rv.reshape(b, s, RET_HEADS, RET_HEAD_DIM)
    ry = bidirectional_retention(rq, rk, rv, ret_decay_logit)
    mu = jnp.mean(ry, axis=-1, keepdims=True)
    var = jnp.mean(jnp.square(ry - mu), axis=-1, keepdims=True)
    ry = ((ry - mu) * lax.rsqrt(var + GN_EPS)).transpose(0, 2, 1, 3).reshape(b, s, RET_W)
    ry = ry * ret_norm_gain.astype(jnp.float32) * jax.nn.silu(rg.astype(jnp.float32))
    ret_out = ry.astype(dt) @ w_ret_o

    merged = jax.nn.sigmoid(gate_a) * attn_out + jax.nn.sigmoid(gate_r) * ret_out
    x = x + merged @ w_out

    hm = rms_norm(x, mlp_norm)
    x = x + jnp.square(jax.nn.relu(hm @ w_up)) @ w_down

    gate = jax.nn.sigmoid(rms_norm(x, ple_norm) @ w_ple_gate)
    x = x + gate * (p_i.astype(dt) @ w_ple)
    return x


def _fwd_setup_inputs(seed: int = 0) -> dict:
    key = jax.random.key(seed)
    ks = jax.random.split(key, 20)
    f32 = jnp.float32

    def w(k, shape, fan_in):
        return jax.random.normal(k, shape, f32) * (fan_in ** -0.5)

    def gain(k, shape):
        return 1.0 + 0.05 * jax.random.normal(k, shape, f32)

    gamma0 = 1.0 - 2.0 ** (-5.0 - jnp.arange(RET_HEADS, dtype=f32))
    logit0 = jnp.log(gamma0) - jnp.log1p(-gamma0)
    decay_logit = logit0[None, None, :] + 0.05 * jax.random.normal(ks[5], (DEPTH, 2, RET_HEADS), f32)

    return {
        "x": jax.random.normal(ks[0], (BATCH, SEQ, D_MODEL), f32),
        "p": jax.random.normal(ks[1], (DEPTH, BATCH, SEQ, PLE_DIM), f32),
        "mix_norm": gain(ks[2], (DEPTH, D_MODEL)),
        "w_in": w(ks[3], (DEPTH, D_MODEL, IN_W), D_MODEL),
        "attn_q_norm": gain(ks[4], (DEPTH, ATTN_HEAD_DIM)),
        "attn_k_norm": gain(ks[6], (DEPTH, ATTN_HEAD_DIM)),
        "ret_decay_logit": decay_logit,
        "ret_norm_gain": gain(ks[7], (DEPTH, RET_W)),
        "w_attn_o": w(ks[8], (DEPTH, ATTN_Q_W, D_MODEL), ATTN_Q_W),
        "w_ret_o": w(ks[9], (DEPTH, RET_W, D_MODEL), RET_W),
        "w_out": w(ks[10], (DEPTH, D_MODEL, D_MODEL), D_MODEL),
        "mlp_norm": gain(ks[11], (DEPTH, D_MODEL)),
        "w_up": w(ks[12], (DEPTH, D_MODEL, D_FF), D_MODEL),
        "w_down": w(ks[13], (DEPTH, D_FF, D_MODEL), D_FF),
        "ple_norm": gain(ks[14], (DEPTH, D_MODEL)),
        "w_ple_gate": w(ks[15], (DEPTH, D_MODEL, D_MODEL), D_MODEL),
        "w_ple": w(ks[16], (DEPTH, PLE_DIM, D_MODEL), PLE_DIM),
        "final_norm": gain(ks[17], (D_MODEL,)),
    }


def _fwd_reference(x, p, mix_norm, w_in, attn_q_norm, attn_k_norm, ret_decay_logit, ret_norm_gain,
              w_attn_o, w_ret_o, w_out, mlp_norm, w_up, w_down, ple_norm, w_ple_gate, w_ple,
              final_norm):
    seq_len = x.shape[1]
    rope_a = axial_rope_tables(seq_len, ATTN_HEAD_DIM)
    rope_r = axial_rope_tables(seq_len, RET_HEAD_DIM)
    for i in range(DEPTH):
        x = hybrid_layer(x, p[i], mix_norm[i], w_in[i], attn_q_norm[i], attn_k_norm[i],
                         ret_decay_logit[i], ret_norm_gain[i], w_attn_o[i], w_ret_o[i], w_out[i],
                         mlp_norm[i], w_up[i], w_down[i], ple_norm[i], w_ple_gate[i], w_ple[i],
                         rope_a, rope_r)
    return rms_norm(x, final_norm)


import jax as _jax
import jax.numpy as _jnp

TWIN_FORMAT = 'train_step'
FWD_PARAMS = ['x', 'p', 'mix_norm', 'w_in', 'attn_q_norm', 'attn_k_norm', 'ret_decay_logit', 'ret_norm_gain', 'w_attn_o', 'w_ret_o', 'w_out', 'mlp_norm', 'w_up', 'w_down', 'ple_norm', 'w_ple_gate', 'w_ple', 'final_norm']
TWIN_WEIGHTS = ['mix_norm', 'w_in', 'attn_q_norm', 'attn_k_norm', 'ret_decay_logit', 'ret_norm_gain', 'w_attn_o', 'w_ret_o', 'w_out', 'mlp_norm', 'w_up', 'w_down', 'ple_norm', 'w_ple_gate', 'w_ple', 'final_norm']
TWIN_DIFF_INPUT = 'x'
TWIN_INPUTS = ['x', 'p', 'mix_norm', 'w_in', 'attn_q_norm', 'attn_k_norm', 'ret_decay_logit', 'ret_norm_gain', 'w_attn_o', 'w_ret_o', 'w_out', 'mlp_norm', 'w_up', 'w_down', 'ple_norm', 'w_ple_gate', 'w_ple', 'final_norm', 'loss_target', 'm_mix_norm', 'm_w_in', 'm_attn_q_norm', 'm_attn_k_norm', 'm_ret_decay_logit', 'm_ret_norm_gain', 'm_w_attn_o', 'm_w_ret_o', 'm_w_out', 'm_mlp_norm', 'm_w_up', 'm_w_down', 'm_ple_norm', 'm_w_ple_gate', 'm_w_ple', 'm_final_norm', 'v_mix_norm', 'v_w_in', 'v_attn_q_norm', 'v_attn_k_norm', 'v_ret_decay_logit', 'v_ret_norm_gain', 'v_w_attn_o', 'v_w_ret_o', 'v_w_out', 'v_mlp_norm', 'v_w_up', 'v_w_down', 'v_ple_norm', 'v_w_ple_gate', 'v_w_ple', 'v_final_norm']
TWIN_OUTPUTS = ['loss', 'grad_x', 'grad_mix_norm', 'grad_w_in', 'grad_attn_q_norm', 'grad_attn_k_norm', 'grad_ret_decay_logit', 'grad_ret_norm_gain', 'grad_w_attn_o', 'grad_w_ret_o', 'grad_w_out', 'grad_mlp_norm', 'grad_w_up', 'grad_w_down', 'grad_ple_norm', 'grad_w_ple_gate', 'grad_w_ple', 'grad_final_norm', 'delta_mix_norm', 'delta_w_in', 'delta_attn_q_norm', 'delta_attn_k_norm', 'delta_ret_decay_logit', 'delta_ret_norm_gain', 'delta_w_attn_o', 'delta_w_ret_o', 'delta_w_out', 'delta_mlp_norm', 'delta_w_up', 'delta_w_down', 'delta_ple_norm', 'delta_w_ple_gate', 'delta_w_ple', 'delta_final_norm', 'new_m_mix_norm', 'new_m_w_in', 'new_m_attn_q_norm', 'new_m_attn_k_norm', 'new_m_ret_decay_logit', 'new_m_ret_norm_gain', 'new_m_w_attn_o', 'new_m_w_ret_o', 'new_m_w_out', 'new_m_mlp_norm', 'new_m_w_up', 'new_m_w_down', 'new_m_ple_norm', 'new_m_w_ple_gate', 'new_m_w_ple', 'new_m_final_norm', 'new_v_mix_norm', 'new_v_w_in', 'new_v_attn_q_norm', 'new_v_attn_k_norm', 'new_v_ret_decay_logit', 'new_v_ret_norm_gain', 'new_v_w_attn_o', 'new_v_w_ret_o', 'new_v_w_out', 'new_v_mlp_norm', 'new_v_w_up', 'new_v_w_down', 'new_v_ple_norm', 'new_v_w_ple_gate', 'new_v_w_ple', 'new_v_final_norm']
TWIN_LEAF_KINDS = {'loss': 'loss', 'grad_x': 'grad_x', 'grad_mix_norm': 'grad_w', 'grad_w_in': 'grad_w', 'grad_attn_q_norm': 'grad_w', 'grad_attn_k_norm': 'grad_w', 'grad_ret_decay_logit': 'grad_w', 'grad_ret_norm_gain': 'grad_w', 'grad_w_attn_o': 'grad_w', 'grad_w_ret_o': 'grad_w', 'grad_w_out': 'grad_w', 'grad_mlp_norm': 'grad_w', 'grad_w_up': 'grad_w', 'grad_w_down': 'grad_w', 'grad_ple_norm': 'grad_w', 'grad_w_ple_gate': 'grad_w', 'grad_w_ple': 'grad_w', 'grad_final_norm': 'grad_w', 'delta_mix_norm': 'delta_w', 'delta_w_in': 'delta_w', 'delta_attn_q_norm': 'delta_w', 'delta_attn_k_norm': 'delta_w', 'delta_ret_decay_logit': 'delta_w', 'delta_ret_norm_gain': 'delta_w', 'delta_w_attn_o': 'delta_w', 'delta_w_ret_o': 'delta_w', 'delta_w_out': 'delta_w', 'delta_mlp_norm': 'delta_w', 'delta_w_up': 'delta_w', 'delta_w_down': 'delta_w', 'delta_ple_norm': 'delta_w', 'delta_w_ple_gate': 'delta_w', 'delta_w_ple': 'delta_w', 'delta_final_norm': 'delta_w', 'new_m_mix_norm': 'new_m', 'new_m_w_in': 'new_m', 'new_m_attn_q_norm': 'new_m', 'new_m_attn_k_norm': 'new_m', 'new_m_ret_decay_logit': 'new_m', 'new_m_ret_norm_gain': 'new_m', 'new_m_w_attn_o': 'new_m', 'new_m_w_ret_o': 'new_m', 'new_m_w_out': 'new_m', 'new_m_mlp_norm': 'new_m', 'new_m_w_up': 'new_m', 'new_m_w_down': 'new_m', 'new_m_ple_norm': 'new_m', 'new_m_w_ple_gate': 'new_m', 'new_m_w_ple': 'new_m', 'new_m_final_norm': 'new_m', 'new_v_mix_norm': 'new_v', 'new_v_w_in': 'new_v', 'new_v_attn_q_norm': 'new_v', 'new_v_attn_k_norm': 'new_v', 'new_v_ret_decay_logit': 'new_v', 'new_v_ret_norm_gain': 'new_v', 'new_v_w_attn_o': 'new_v', 'new_v_w_ret_o': 'new_v', 'new_v_w_out': 'new_v', 'new_v_mlp_norm': 'new_v', 'new_v_w_up': 'new_v', 'new_v_w_down': 'new_v', 'new_v_ple_norm': 'new_v', 'new_v_w_ple_gate': 'new_v', 'new_v_w_ple': 'new_v', 'new_v_final_norm': 'new_v'}


def _forward(args):
    return _fwd_reference(*[args[k] for k in FWD_PARAMS])


def _output_shape():
    def fwd():
        inp = _fwd_setup_inputs(0)
        return _fwd_reference(*[inp[k] for k in FWD_PARAMS])
    out = _jax.eval_shape(fwd)
    return out.shape, out.dtype

N_MICROBATCH = 1
ADAM_LR = 0.001
ADAM_B1 = 0.9
ADAM_B2 = 0.999
ADAM_EPS = 1e-08
ADAM_WD = 0.01
ADAM_STEP = 10
PER_EXAMPLE_BATCH_AXIS = {'x': 0, 'p': 1, 'loss_target': 0}
SHARED_INPUTS = []
_WEIGHT_DTYPES = {'mix_norm': _jnp.float32, 'w_in': _jnp.float32, 'attn_q_norm': _jnp.float32, 'attn_k_norm': _jnp.float32, 'ret_decay_logit': _jnp.float32, 'ret_norm_gain': _jnp.float32, 'w_attn_o': _jnp.float32, 'w_ret_o': _jnp.float32, 'w_out': _jnp.float32, 'mlp_norm': _jnp.float32, 'w_up': _jnp.float32, 'w_down': _jnp.float32, 'ple_norm': _jnp.float32, 'w_ple_gate': _jnp.float32, 'w_ple': _jnp.float32, 'final_norm': _jnp.float32}
MOMENT_SCALE = {'mix_norm': 2.568519e-01, 'w_in': 1.097235e-01, 'attn_q_norm': 7.129062e-02, 'attn_k_norm': 7.744166e-02, 'ret_decay_logit': 8.543486e-01, 'ret_norm_gain': 1.651031e-01, 'w_attn_o': 2.014836e-02, 'w_ret_o': 1.161000e-01, 'w_out': 1.167559e-01, 'mlp_norm': 3.097892e-01, 'w_up': 1.537654e-01, 'w_down': 6.469278e-01, 'ple_norm': 6.801239e-02, 'w_ple_gate': 7.691502e-02, 'w_ple': 1.182855e-01, 'final_norm': 1.290963e+02}


def _to_microbatches(a, axis):
    t = _jnp.moveaxis(a, axis, 0)
    t = t.reshape((N_MICROBATCH, t.shape[0] // N_MICROBATCH) + t.shape[1:])
    return _jnp.moveaxis(t, 1, axis + 1)


def setup_inputs(seed: int = 0) -> dict:
    inp = _fwd_setup_inputs(seed)
    key = _jax.random.fold_in(_jax.random.key(seed), 7919)
    shape, _ = _output_shape()
    out = dict(inp)
    out["loss_target"] = _jax.random.normal(_jax.random.fold_in(key, 0), shape, _jnp.float32)
    for i, name in enumerate(TWIN_WEIGHTS):
        w = inp[name].astype(_jnp.float32)
        if MOMENT_SCALE is None:
            s = _jnp.sqrt(_jnp.mean(_jnp.square(w)) + 1e-30)
        else:
            s = MOMENT_SCALE[name]
        km, kv = _jax.random.split(_jax.random.fold_in(key, i + 1))
        out[name] = w
        out["m_" + name] = s * _jax.random.normal(km, w.shape, _jnp.float32)
        out["v_" + name] = (s * s) * _jax.random.uniform(kv, w.shape, _jnp.float32, 0.5, 1.5)
    if N_MICROBATCH > 1:
        for name, axis in PER_EXAMPLE_BATCH_AXIS.items():
            out[name] = _to_microbatches(out[name], axis)
    return {'x': out['x'], 'p': out['p'], 'mix_norm': out['mix_norm'], 'w_in': out['w_in'], 'attn_q_norm': out['attn_q_norm'], 'attn_k_norm': out['attn_k_norm'], 'ret_decay_logit': out['ret_decay_logit'], 'ret_norm_gain': out['ret_norm_gain'], 'w_attn_o': out['w_attn_o'], 'w_ret_o': out['w_ret_o'], 'w_out': out['w_out'], 'mlp_norm': out['mlp_norm'], 'w_up': out['w_up'], 'w_down': out['w_down'], 'ple_norm': out['ple_norm'], 'w_ple_gate': out['w_ple_gate'], 'w_ple': out['w_ple'], 'final_norm': out['final_norm'], 'loss_target': out['loss_target'], 'm_mix_norm': out['m_mix_norm'], 'm_w_in': out['m_w_in'], 'm_attn_q_norm': out['m_attn_q_norm'], 'm_attn_k_norm': out['m_attn_k_norm'], 'm_ret_decay_logit': out['m_ret_decay_logit'], 'm_ret_norm_gain': out['m_ret_norm_gain'], 'm_w_attn_o': out['m_w_attn_o'], 'm_w_ret_o': out['m_w_ret_o'], 'm_w_out': out['m_w_out'], 'm_mlp_norm': out['m_mlp_norm'], 'm_w_up': out['m_w_up'], 'm_w_down': out['m_w_down'], 'm_ple_norm': out['m_ple_norm'], 'm_w_ple_gate': out['m_w_ple_gate'], 'm_w_ple': out['m_w_ple'], 'm_final_norm': out['m_final_norm'], 'v_mix_norm': out['v_mix_norm'], 'v_w_in': out['v_w_in'], 'v_attn_q_norm': out['v_attn_q_norm'], 'v_attn_k_norm': out['v_attn_k_norm'], 'v_ret_decay_logit': out['v_ret_decay_logit'], 'v_ret_norm_gain': out['v_ret_norm_gain'], 'v_w_attn_o': out['v_w_attn_o'], 'v_w_ret_o': out['v_w_ret_o'], 'v_w_out': out['v_w_out'], 'v_mlp_norm': out['v_mlp_norm'], 'v_w_up': out['v_w_up'], 'v_w_down': out['v_w_down'], 'v_ple_norm': out['v_ple_norm'], 'v_w_ple_gate': out['v_w_ple_gate'], 'v_w_ple': out['v_w_ple'], 'v_final_norm': out['v_final_norm']}


def _loss(weights, diff, rest, loss_target):
    with _jax.named_scope("forward"):
        args = {**rest, TWIN_DIFF_INPUT: diff, **{k: w.astype(_WEIGHT_DTYPES[k]) for k, w in weights.items()}}
        y = _forward(args)
    with _jax.named_scope("loss_head"):
        err = _jnp.square(y.astype(_jnp.float32) - loss_target)
        return 0.5 * _jnp.sum(_jnp.mean(err, axis=-1)) if err.ndim else 0.5 * err


def _adamw(w, g, m, v):
    m = ADAM_B1 * m + (1.0 - ADAM_B1) * g
    v = ADAM_B2 * v + (1.0 - ADAM_B2) * _jnp.square(g)
    m_hat = m / (1.0 - ADAM_B1 ** ADAM_STEP)
    v_hat = v / (1.0 - ADAM_B2 ** ADAM_STEP)
    delta = -ADAM_LR * (m_hat / (_jnp.sqrt(v_hat) + ADAM_EPS) + ADAM_WD * w)
    return delta, m, v


def reference(x, p, mix_norm, w_in, attn_q_norm, attn_k_norm, ret_decay_logit, ret_norm_gain, w_attn_o, w_ret_o, w_out, mlp_norm, w_up, w_down, ple_norm, w_ple_gate, w_ple, final_norm, loss_target, m_mix_norm, m_w_in, m_attn_q_norm, m_attn_k_norm, m_ret_decay_logit, m_ret_norm_gain, m_w_attn_o, m_w_ret_o, m_w_out, m_mlp_norm, m_w_up, m_w_down, m_ple_norm, m_w_ple_gate, m_w_ple, m_final_norm, v_mix_norm, v_w_in, v_attn_q_norm, v_attn_k_norm, v_ret_decay_logit, v_ret_norm_gain, v_w_attn_o, v_w_ret_o, v_w_out, v_mlp_norm, v_w_up, v_w_down, v_ple_norm, v_w_ple_gate, v_w_ple, v_final_norm):
    given = dict(x=x, p=p, mix_norm=mix_norm, w_in=w_in, attn_q_norm=attn_q_norm, attn_k_norm=attn_k_norm, ret_decay_logit=ret_decay_logit, ret_norm_gain=ret_norm_gain, w_attn_o=w_attn_o, w_ret_o=w_ret_o, w_out=w_out, mlp_norm=mlp_norm, w_up=w_up, w_down=w_down, ple_norm=ple_norm, w_ple_gate=w_ple_gate, w_ple=w_ple, final_norm=final_norm, loss_target=loss_target, m_mix_norm=m_mix_norm, m_w_in=m_w_in, m_attn_q_norm=m_attn_q_norm, m_attn_k_norm=m_attn_k_norm, m_ret_decay_logit=m_ret_decay_logit, m_ret_norm_gain=m_ret_norm_gain, m_w_attn_o=m_w_attn_o, m_w_ret_o=m_w_ret_o, m_w_out=m_w_out, m_mlp_norm=m_mlp_norm, m_w_up=m_w_up, m_w_down=m_w_down, m_ple_norm=m_ple_norm, m_w_ple_gate=m_w_ple_gate, m_w_ple=m_w_ple, m_final_norm=m_final_norm, v_mix_norm=v_mix_norm, v_w_in=v_w_in, v_attn_q_norm=v_attn_q_norm, v_attn_k_norm=v_attn_k_norm, v_ret_decay_logit=v_ret_decay_logit, v_ret_norm_gain=v_ret_norm_gain, v_w_attn_o=v_w_attn_o, v_w_ret_o=v_w_ret_o, v_w_out=v_w_out, v_mlp_norm=v_mlp_norm, v_w_up=v_w_up, v_w_down=v_w_down, v_ple_norm=v_ple_norm, v_w_ple_gate=v_w_ple_gate, v_w_ple=v_w_ple, v_final_norm=v_final_norm)
    weights = {n: given[n] for n in TWIN_WEIGHTS}
    shared = {n: given[n] for n in SHARED_INPUTS}
    per_example = {n: given[n] for n in ['x', 'p']}
    grad_fn = _jax.value_and_grad(_loss, argnums=(0, 1))

    def one_microbatch(ex, loss_target):
        ex = dict(ex)
        diff = ex.pop(TWIN_DIFF_INPUT)
        return grad_fn(weights, diff, {**shared, **ex}, loss_target)

    if N_MICROBATCH == 1:
        loss, (grad_w, grad_x) = one_microbatch(per_example, given["loss_target"])
    else:
        def body(carry, xs):
            loss_sum, grad_sum = carry
            l_k, (gw_k, gx_k) = one_microbatch(xs[0], xs[1])
            with _jax.named_scope("update"):
                return (loss_sum + l_k, _jax.tree.map(_jnp.add, grad_sum, gw_k)), gx_k

        init = (_jnp.zeros((), _jnp.float32), _jax.tree.map(_jnp.zeros_like, weights))
        (loss, grad_w), grad_x = _jax.lax.scan(body, init, (per_example, given["loss_target"]))
    with _jax.named_scope("update"):
        delta_w, new_m, new_v = {}, {}, {}
        for n in TWIN_WEIGHTS:
            delta_w[n], new_m[n], new_v[n] = _adamw(weights[n], grad_w[n], given["m_" + n], given["v_" + n])
    return (loss, grad_x, *[grad_w[n] for n in TWIN_WEIGHTS], *[delta_w[n] for n in TWIN_WEIGHTS],
            *[new_m[n] for n in TWIN_WEIGHTS], *[new_v[n] for n in TWIN_WEIGHTS])
```

```python
import functools

import jax
import jax.numpy as jnp
from jax import lax
from jax.experimental import pallas as pl
from jax.experimental.pallas import tpu as pltpu

F32 = jnp.float32
_MXU = jnp.bfloat16

D = 1024
PLE = 256
GRID_W = 64
A_HD = 64
A_H = 8
A_KV = 2
A_G = A_H // A_KV
AQ_W = A_H * A_HD
AKV_W = A_KV * A_HD
R_HD = 128
R_H = 4
R_W = R_H * R_HD
IN_W = AQ_W + 2 * AKV_W + 4 * R_W + 2 * D
PA_W = AQ_W + 2 * AKV_W
PR_W = 4 * R_W
PG_W = 2 * D
FF = 4 * D
CHUNK = 128
ROPE_THETA = 10000.0
EPS = 1e-6
GN_EPS = 1e-5
N_DEV = 8

LR, B1, B2, ADAM_EPS, WD, STEP = 0.001, 0.9, 0.999, 1e-08, 0.01, 10

LANES = 128
PACK_COLS = 1024
SMALL_ROWS = 24
HIGHEST = lax.Precision.HIGHEST


def _tile(n, pref):
    t = min(n, pref)
    assert n % t == 0, (n, t)
    return t


def _bf(a):
    return a.astype(_MXU)


def _mm(a, b):
    return jnp.dot(_bf(a), _bf(b), preferred_element_type=F32)


def _mm_nt(a, b):
    return lax.dot_general(_bf(a), _bf(b), (((1,), (1,)), ((), ())), preferred_element_type=F32)


def _mm_tn(a, b):
    return lax.dot_general(_bf(a), _bf(b), (((0,), (0,)), ((), ())), preferred_element_type=F32)


def _sigmoid(z):
    return 1.0 / (1.0 + jnp.exp(-z))


def _rms(x):
    r = lax.rsqrt(jnp.mean(x * x, axis=-1, keepdims=True) + EPS)
    return x * r, r


def _rms_bwd(n, r, gain, dy):
    dn = dy * gain
    return r * (dn - n * jnp.mean(dn * n, axis=-1, keepdims=True))


def _swap_halves(x, half):
    n = x.shape[-1]
    lane = lax.broadcasted_iota(jnp.int32, x.shape, x.ndim - 1)
    first = (lane % (2 * half)) < half
    return jnp.where(first, pltpu.roll(x, n - half, axis=1), pltpu.roll(x, half, axis=1))


def _rope(x, cos, sin, half):
    return x * cos + _swap_halves(x, half) * sin


def _cat(t, reps):
    return jnp.concatenate([t] * reps, axis=1)


def _full(shape):
    nd = len(shape)
    return pl.BlockSpec(shape, lambda *_: (0,) * nd)


def _rope_tables(seq):
    def tab(head_dim):
        n_axis = head_dim // 4
        freqs = ROPE_THETA ** (-jnp.arange(n_axis, dtype=F32) / n_axis)
        rows = seq // GRID_W
        row = jnp.repeat(jnp.arange(rows, dtype=F32), GRID_W)
        col = jnp.tile(jnp.arange(GRID_W, dtype=F32), rows)
        ang = jnp.concatenate([row[:, None] * freqs, col[:, None] * freqs], axis=-1)
        c, s = jnp.cos(ang), jnp.sin(ang)
        return jnp.concatenate([c, c], axis=-1), jnp.concatenate([-s, s], axis=-1)
    ca, sa = tab(A_HD)
    cr, sr = tab(R_HD)
    return jnp.tile(ca, (1, 2)), jnp.tile(sa, (1, 2)), cr, sr


def _seg_mean_matrix():
    i = jnp.arange(AQ_W) // A_HD
    return (i[:, None] == i[None, :]).astype(F32) / A_HD


def _mesh_pos():
    return lax.axis_index("x"), lax.axis_index("y"), lax.axis_index("c")


def _all_gather(shard):
    rws, cols = shard.shape

    def body(x_ref, out_ref, send_sems, recv_sems, local_sem):
        x, y, c = _mesh_pos()
        me, sibling = (x, y, c), (x, y, 1 - c)
        chips = [(1 - x, y), (x, 1 - y), (1 - x, 1 - y)]

        def slot(px, py, pc):
            return out_ref.at[4 * px + 2 * py + pc]

        def copy(k, block, to, src=None):
            return pltpu.make_async_remote_copy(
                src_ref=slot(*block) if src is None else src, dst_ref=slot(*block),
                send_sem=send_sems.at[k], recv_sem=recv_sems.at[k],
                device_id=to, device_id_type=pl.DeviceIdType.MESH)

        mine = pltpu.make_async_copy(x_ref, slot(*me), local_sem)
        mine.start()
        first = [copy(0, me, sibling, src=x_ref)]
        first += [copy(1 + j, me, (*chip, c), src=x_ref) for j, chip in enumerate(chips)]
        for cp in first:
            cp.start()
        passed = [copy(4 + j, (*chip, c), sibling) for j, chip in enumerate(chips)]
        for j, chip in enumerate(chips):
            copy(1 + j, (*chip, c), me).wait_recv()
            passed[j].start()
        copy(0, sibling, me).wait_recv()
        for j, chip in enumerate(chips):
            copy(4 + j, (*chip, 1 - c), me).wait_recv()
        for cp in first + passed:
            cp.wait_send()
        mine.wait()

    return pl.pallas_call(
        body, name="all_gather_weights",
        out_shape=jax.ShapeDtypeStruct((N_DEV, rws, cols), shard.dtype),
        in_specs=[pl.BlockSpec(memory_space=pl.ANY)],
        out_specs=pl.BlockSpec(memory_space=pl.ANY),
        scratch_shapes=[pltpu.SemaphoreType.DMA((7,)), pltpu.SemaphoreType.DMA((7,)),
                        pltpu.SemaphoreType.DMA],
    )(shard)


def _exchange_grads(gpack, small):
    _, rws, cols = gpack.shape

    def body(g_ref, s_ref, land_ref, sland_ref, send_sems, recv_sems, local_sems):
        x, y, c = _mesh_pos()
        me = 4 * x + 2 * y + c
        own = pltpu.make_async_copy(g_ref.at[me], land_ref.at[me], local_sems.at[0])
        own_s = pltpu.make_async_copy(s_ref, sland_ref.at[me], local_sems.at[1])
        own.start()
        own_s.start()
        sends = []
        for k in range(1, N_DEV):
            peer = (x ^ ((k >> 2) & 1), y ^ ((k >> 1) & 1), c ^ (k & 1))
            pidx = 4 * peer[0] + 2 * peer[1] + peer[2]
            big = pltpu.make_async_remote_copy(
                src_ref=g_ref.at[pidx], dst_ref=land_ref.at[me],
                send_sem=send_sems.at[k - 1], recv_sem=recv_sems.at[k - 1],
                device_id=peer, device_id_type=pl.DeviceIdType.MESH)
            sm = pltpu.make_async_remote_copy(
                src_ref=s_ref, dst_ref=sland_ref.at[me],
                send_sem=send_sems.at[6 + k], recv_sem=recv_sems.at[6 + k],
                device_id=peer, device_id_type=pl.DeviceIdType.MESH)
            big.start()
            sm.start()
            sends += [big, sm]
        for k in range(1, N_DEV):
            peer = (x ^ ((k >> 2) & 1), y ^ ((k >> 1) & 1), c ^ (k & 1))
            pidx = 4 * peer[0] + 2 * peer[1] + peer[2]
            pltpu.make_async_remote_copy(
                src_ref=g_ref.at[me], dst_ref=land_ref.at[pidx],
                send_sem=send_sems.at[k - 1], recv_sem=recv_sems.at[k - 1],
                device_id=peer, device_id_type=pl.DeviceIdType.MESH).wait_recv()
            pltpu.make_async_remote_copy(
                src_ref=s_ref, dst_ref=sland_ref.at[pidx],
                send_sem=send_sems.at[6 + k], recv_sem=recv_sems.at[6 + k],
                device_id=peer, device_id_type=pl.DeviceIdType.MESH).wait_recv()
        for cp in sends:
            cp.wait_send()
        own.wait()
        own_s.wait()

    return pl.pallas_call(
        body, name="exchange_grads",
        out_shape=(jax.ShapeDtypeStruct((N_DEV, rws, cols), gpack.dtype),
                   jax.ShapeDtypeStruct((N_DEV,) + small.shape, small.dtype)),
        in_specs=[pl.BlockSpec(memory_space=pl.ANY), pl.BlockSpec(memory_space=pl.ANY)],
        out_specs=(pl.BlockSpec(memory_space=pl.ANY), pl.BlockSpec(memory_space=pl.ANY)),
        scratch_shapes=[pltpu.SemaphoreType.DMA((14,)), pltpu.SemaphoreType.DMA((14,)),
                        pltpu.SemaphoreType.DMA((2,))],
    )(gpack, small)


def _in_proj(x, gain, w):
    seq = x.shape[0]
    tm = _tile(seq, 256)

    def body(x_ref, g_ref, w_ref, pa_ref, pr_ref, pg_ref, h_ref):
        n, _ = _rms(x_ref[...])
        h = _bf(n * g_ref[...])
        h_ref[...] = h
        pa_ref[...] = _mm(h, w_ref[:, 0:PA_W])
        pr_ref[...] = _mm(h, w_ref[:, PA_W:PA_W + PR_W])
        pg_ref[...] = _mm(h, w_ref[:, PA_W + PR_W:IN_W])

    row = lambda w_: pl.BlockSpec((tm, w_), lambda i: (i, 0))
    return pl.pallas_call(
        body, name="in_proj", grid=(seq // tm,),
        in_specs=[row(D), _full((1, D)), _full((D, IN_W))],
        out_specs=(row(PA_W), row(PR_W), row(PG_W), row(D)),
        out_shape=(jax.ShapeDtypeStruct((seq, PA_W), F32), jax.ShapeDtypeStruct((seq, PR_W), F32),
                   jax.ShapeDtypeStruct((seq, PG_W), F32), jax.ShapeDtypeStruct((seq, D), _MXU)),
    )(x, gain, w)


def _qk_prep(pa, pr, gq, gk, seg, ca, sa, cr, sr):
    seq = pa.shape[0]
    tm = _tile(seq, 256)

    def body(pa_ref, pr_ref, gq_ref, gk_ref, seg_ref, ca_ref, sa_ref, cr_ref, sr_ref,
             qh_ref, kh_ref, v_ref, rq_ref, rk_ref):
        q = pa_ref[:, 0:AQ_W]
        k = pa_ref[:, AQ_W:AQ_W + AKV_W]
        v_ref[...] = _bf(pa_ref[:, AQ_W + AKV_W:PA_W])
        ca_, sa_ = ca_ref[...], sa_ref[...]
        msq = jnp.dot(q * q, seg_ref[...], precision=HIGHEST, preferred_element_type=F32)
        qn = q * lax.rsqrt(msq + EPS) * gq_ref[...]
        qh_ref[...] = _bf(_rope(qn, _cat(ca_, 4), _cat(sa_, 4), A_HD // 2) * (A_HD ** -0.5))
        msk = jnp.dot(k * k, seg_ref[0:AKV_W, 0:AKV_W], precision=HIGHEST, preferred_element_type=F32)
        kn = k * lax.rsqrt(msk + EPS) * gk_ref[...]
        kh_ref[...] = _bf(_rope(kn, ca_, sa_, A_HD // 2))
        cr_, sr_ = _cat(cr_ref[...], 4), _cat(sr_ref[...], 4)
        rq_ref[...] = _rope(pr_ref[:, 0:R_W], cr_, sr_, R_HD // 2) * (R_HD ** -0.5)
        rk_ref[...] = _rope(pr_ref[:, R_W:2 * R_W], cr_, sr_, R_HD // 2)

    row = lambda w_: pl.BlockSpec((tm, w_), lambda i: (i, 0))
    return pl.pallas_call(
        body, name="qk_prep", grid=(seq // tm,),
        in_specs=[row(PA_W), row(2 * R_W), _full((1, AQ_W)), _full((1, AKV_W)), _full((AQ_W, AQ_W)),
                  row(LANES), row(LANES), row(LANES), row(LANES)],
        out_specs=(row(AQ_W), row(AKV_W), row(AKV_W), row(R_W), row(R_W)),
        out_shape=(jax.ShapeDtypeStruct((seq, AQ_W), _MXU), jax.ShapeDtypeStruct((seq, AKV_W), _MXU),
                   jax.ShapeDtypeStruct((seq, AKV_W), _MXU), jax.ShapeDtypeStruct((seq, R_W), F32),
                   jax.ShapeDtypeStruct((seq, R_W), F32)),
    )(pa, pr, gq, gk, seg, ca, sa, cr, sr)


def _pad_heads(a):
    seq = a.shape[0]
    t = a.reshape(seq, A_H, A_HD).transpose(1, 0, 2)
    z = jnp.zeros((A_G, seq, A_HD), a.dtype)
    return jnp.concatenate([jnp.concatenate([t[:A_G], z], axis=-1),
                            jnp.concatenate([z, t[A_G:]], axis=-1)], axis=0)


def _chunk_t(a, tk):
    seq = a.shape[0]
    return a.reshape(seq // tk, tk, a.shape[1]).transpose(0, 2, 1)


def _heads_to_rows(t):
    return t.transpose(2, 0, 1).reshape(t.shape[2], AQ_W)


def _attn_fwd(qpad, k2, v2t, tq, tk):
    seq = k2.shape[0]
    nck = seq // tk
    rows = A_G * tq

    def body(q_ref, k_ref, vt_ref, o_ref, lse_ref, m_sc, l_sc, acc_sc):
        g = pl.program_id(0)
        q = q_ref[...].reshape(rows, LANES)
        m_sc[...] = jnp.full((1, rows), -jnp.inf, F32)
        l_sc[...] = jnp.zeros((1, rows), F32)
        acc_sc[...] = jnp.zeros((LANES, rows), F32)

        def step(c, carry):
            kc = k_ref[pl.ds(pl.multiple_of(c * tk, tk), tk), :]
            st = _mm_nt(kc, q)
            m_old = m_sc[...]
            m_new = jnp.maximum(m_old, jnp.max(st, axis=0, keepdims=True))
            p = jnp.exp(st - m_new)
            alpha = jnp.exp(m_old - m_new)
            l_sc[...] = alpha * l_sc[...] + jnp.sum(p, axis=0, keepdims=True)
            acc_sc[...] = alpha * acc_sc[...] + _mm(vt_ref[c], p)
            m_sc[...] = m_new
            return carry

        lax.fori_loop(0, nck, step, 0)
        l = l_sc[...]
        lse = m_sc[...] + jnp.log(l)
        acc_sc[...] = acc_sc[...] * (1.0 / l)
        for a in range(A_G):
            o_ref[a] = acc_sc[pl.ds(pl.multiple_of(g * A_HD, A_HD), A_HD), a * tq:(a + 1) * tq]
            lse_ref[a] = lse[:, a * tq:(a + 1) * tq]

    return pl.pallas_call(
        body, name="attn_fwd", grid=(A_KV, seq // tq),
        in_specs=[pl.BlockSpec((A_G, tq, LANES), lambda g, i: (g, i, 0)),
                  _full((seq, LANES)), _full((nck, LANES, tk))],
        out_specs=(pl.BlockSpec((A_G, A_HD, tq), lambda g, i: (g, 0, i)),
                   pl.BlockSpec((A_G, 1, tq), lambda g, i: (g, 0, i))),
        out_shape=(jax.ShapeDtypeStruct((A_H, A_HD, seq), F32), jax.ShapeDtypeStruct((A_H, 1, seq), F32)),
        scratch_shapes=[pltpu.VMEM((1, rows), F32), pltpu.VMEM((1, rows), F32),
                        pltpu.VMEM((LANES, rows), F32)],
    )(qpad, k2, v2t)


def _attn_bwd(qpad, dopad, ot, dot_, lse, k2, v2, k2t, tq, tk, ksplit):
    seq = k2.shape[0]
    sh = seq // ksplit
    nck = sh // tk
    rows = A_G * tq

    def body(q_ref, do_ref, ot_ref, dot_ref, lse_ref, k_ref, v_ref, kt_ref,
             dk_ref, dv_ref, dq_ref, dq_sc):
        g = pl.program_id(1)
        i = pl.program_id(2)

        @pl.when((g == 0) & (i == 0))
        def _():
            dk_ref[...] = jnp.zeros_like(dk_ref)
            dv_ref[...] = jnp.zeros_like(dv_ref)

        q = q_ref[...].reshape(rows, LANES)
        do = do_ref[...].reshape(rows, LANES)
        lse_row = jnp.concatenate([lse_ref[a] for a in range(A_G)], axis=1)
        dd = jnp.concatenate([jnp.sum(ot_ref[a] * dot_ref[a], axis=0, keepdims=True)
                              for a in range(A_G)], axis=1)
        dq_sc[...] = jnp.zeros((LANES, rows), F32)

        def step(c, carry):
            sl = pl.ds(pl.multiple_of(c * tk, tk), tk)
            kc = k_ref[sl, :]
            vc = v_ref[sl, :]
            p = jnp.exp(_mm_nt(kc, q) - lse_row)
            ds = p * (_mm_nt(vc, do) - dd)
            dv_ref[sl, :] += _mm(p, do)
            dk_ref[sl, :] += _mm(ds, q)
            dq_sc[...] += _mm(kt_ref[c], ds)
            return carry

        lax.fori_loop(0, nck, step, 0)
        for a in range(A_G):
            dq_ref[0, a] = dq_sc[pl.ds(pl.multiple_of(g * A_HD, A_HD), A_HD), a * tq:(a + 1) * tq]

    qspec = pl.BlockSpec((A_G, tq, LANES), lambda s, g, i: (g, i, 0))
    tspec = pl.BlockSpec((A_G, A_HD, tq), lambda s, g, i: (g, 0, i))
    kspec = pl.BlockSpec((sh, LANES), lambda s, g, i: (s, 0))
    return pl.pallas_call(
        body, name="attn_bwd", grid=(ksplit, A_KV, seq // tq),
        in_specs=[qspec, qspec, tspec, tspec, pl.BlockSpec((A_G, 1, tq), lambda s, g, i: (g, 0, i)),
                  kspec, kspec, pl.BlockSpec((nck, LANES, tk), lambda s, g, i: (s, 0, 0))],
        out_specs=(kspec, kspec, pl.BlockSpec((1, A_G, A_HD, tq), lambda s, g, i: (s, g, 0, i))),
        out_shape=(jax.ShapeDtypeStruct((seq, LANES), F32), jax.ShapeDtypeStruct((seq, LANES), F32),
                   jax.ShapeDtypeStruct((ksplit, A_H, A_HD, seq), F32)),
        scratch_shapes=[pltpu.VMEM((LANES, rows), F32)],
    )(qpad, dopad, ot, dot_, lse, k2, v2, k2t)


def _ret_tables(zb):
    c = CHUNK

    def body(z_ref, m_ref, mw_ref, qd_ref, qdw_ref, kd_ref, kdw_ref, g_ref, gw_ref):
        fwd = pl.program_id(0) < R_H
        z = z_ref[0]
        lam = jnp.minimum(z, 0.0) - jnp.log(1.0 + jnp.exp(-jnp.abs(z)))
        i = lax.broadcasted_iota(jnp.int32, (c, c), 0).astype(F32)
        j = lax.broadcasted_iota(jnp.int32, (c, c), 1).astype(F32)
        diff = jnp.where(fwd, i - j, j - i)
        keep = diff >= jnp.where(fwd, 0.0, 1.0)
        dist = jnp.maximum(diff, 0.0)
        m = jnp.where(keep, jnp.exp(lam * dist), 0.0)
        m_ref[0] = m
        mw_ref[0] = m * dist
        fq = jnp.where(fwd, i + 1.0, c - i)
        qd = jnp.exp(lam * fq)
        qd_ref[0] = qd
        qdw_ref[0] = qd * fq
        fk = jnp.where(fwd, c - 1.0 - i, i)
        kd = jnp.exp(lam * fk)
        kd_ref[0] = kd
        kdw_ref[0] = kd * fk
        gdec = jnp.exp(lam * c)
        g_ref[0] = gdec
        gw_ref[0] = gdec * c

    big = pl.BlockSpec((1, c, c), lambda t: (t, 0, 0))
    vec = pl.BlockSpec((1, 1, LANES), lambda t: (t, 0, 0))
    bshape = jax.ShapeDtypeStruct((2 * R_H, c, c), F32)
    vshape = jax.ShapeDtypeStruct((2 * R_H, 1, LANES), F32)
    return pl.pallas_call(
        body, name="ret_tables", grid=(2 * R_H,), in_specs=[vec],
        out_specs=(big, big, big, big, big, big, vec, vec),
        out_shape=(bshape,) * 6 + (vshape, vshape),
    )(zb)


def _ret_fwd(rq, rk, pr, m, qd, kd, gdec, cb):
    seq = rq.shape[0]
    c = CHUNK
    ns = seq // (cb * c)

    def body(q_ref, k_ref, v_ref, m_ref, qd_ref, kd_ref, g_ref, y_ref, pst_ref, p_sc):
        d = pl.program_id(1)

        @pl.when(pl.program_id(2) == 0)
        def _():
            p_sc[...] = jnp.zeros((R_HD, R_HD), F32)

        mm, qdd, kdd, gg = m_ref[0], qd_ref[0], kd_ref[0], g_ref[0]

        def chunk(j, carry):
            cc = jnp.where(d == 0, j, cb - 1 - j)
            sl = pl.ds(pl.multiple_of(cc * c, c), c)
            q, k, v = q_ref[sl, :], k_ref[sl, :], v_ref[sl, :]
            p = p_sc[...]
            pst_ref[0, cc] = p
            a = _mm_nt(q, k) * mm
            y_ref[0, sl, :] = _mm(a, v) + _mm(q * qdd, p)
            p_sc[...] = p * gg + _mm_tn(k * kdd, v)
            return carry

        lax.fori_loop(0, cb, chunk, 0)

    def step(d, n):
        return d * (ns - 1 - n) + (1 - d) * n

    blk = lambda off: pl.BlockSpec((cb * c, R_HD), lambda h, d, n: (step(d, n), off + h))
    big = pl.BlockSpec((1, c, c), lambda h, d, n: (d * R_H + h, 0, 0))
    vec = pl.BlockSpec((1, 1, LANES), lambda h, d, n: (d * R_H + h, 0, 0))
    return pl.pallas_call(
        body, name="ret_fwd", grid=(R_H, 2, ns),
        in_specs=[blk(0), blk(0), blk(2 * R_H), big, big, big, vec],
        out_specs=(pl.BlockSpec((1, cb * c, R_HD), lambda h, d, n: (d, step(d, n), h)),
                   pl.BlockSpec((1, cb, R_HD, R_HD), lambda h, d, n: (d * R_H + h, step(d, n), 0, 0))),
        out_shape=(jax.ShapeDtypeStruct((2, seq, R_W), F32),
                   jax.ShapeDtypeStruct((2 * R_H, seq // c, R_HD, R_HD), F32)),
        scratch_shapes=[pltpu.VMEM((R_HD, R_HD), F32)],
    )(rq, rk, pr, m, qd, kd, gdec)


def _ret_bwd(rq, rk, pr, dry, pst, m, mw, qd, qdw, kd, kdw, gdec, gw, cb):
    seq = rq.shape[0]
    c = CHUNK
    ns = seq // (cb * c)

    def body(q_ref, k_ref, v_ref, dy_ref, pst_ref, m_ref, mw_ref, qd_ref, qdw_ref, kd_ref, kdw_ref,
             g_ref, gw_ref, dq_ref, dk_ref, dv_ref, dlam_ref, r_sc, acc_sc, e_sc, g_sc):
        d = pl.program_id(1)
        n = pl.program_id(2)

        @pl.when(n == 0)
        def _():
            r_sc[...] = jnp.zeros_like(r_sc)
            acc_sc[...] = jnp.zeros_like(acc_sc)
            e_sc[...] = jnp.zeros_like(e_sc)
            g_sc[...] = jnp.zeros_like(g_sc)

        mm, mww, qdd, qdww, kdd, kdww, gg = (m_ref[0], mw_ref[0], qd_ref[0], qdw_ref[0],
                                             kd_ref[0], kdw_ref[0], g_ref[0])

        def chunk(j, carry):
            cc = jnp.where(d == 0, cb - 1 - j, j)
            sl = pl.ds(pl.multiple_of(cc * c, c), c)
            q, k, v, dy = q_ref[sl, :], k_ref[sl, :], v_ref[sl, :], dy_ref[sl, :]
            p = pst_ref[0, cc]
            r = r_sc[...]
            qk = _mm_nt(q, k)
            a = qk * mm
            ds = _mm_nt(dy, v)
            da = ds * mm
            dyp = _mm_nt(dy, p)
            vr = _mm_nt(v, r)
            dv_ref[0, sl, :] = _mm_tn(a, dy) + _mm(k * kdd, r)
            dq_ref[0, sl, :] = _mm(da, k) + dyp * qdd
            dk_ref[0, sl, :] = _mm_tn(da, q) + vr * kdd
            acc_sc[...] += dyp * q * qdww + vr * k * kdww
            e_sc[...] += ds * qk * mww
            g_sc[...] += r * p
            r_sc[...] = r * gg + _mm_tn(q * qdd, dy)
            return carry

        lax.fori_loop(0, cb, chunk, 0)

        @pl.when(n == ns - 1)
        def _():
            tot = (jnp.sum(jnp.sum(acc_sc[...] + e_sc[...] + g_sc[...] * gw_ref[0], axis=0, keepdims=True),
                           axis=1, keepdims=True))
            dlam_ref[0] = jnp.broadcast_to(tot, (1, LANES))

    def step(d, n):
        return d * n + (1 - d) * (ns - 1 - n)

    blk = lambda off: pl.BlockSpec((cb * c, R_HD), lambda h, d, n: (step(d, n), off + h))
    big = pl.BlockSpec((1, c, c), lambda h, d, n: (d * R_H + h, 0, 0))
    vec = pl.BlockSpec((1, 1, LANES), lambda h, d, n: (d * R_H + h, 0, 0))
    out = pl.BlockSpec((1, cb * c, R_HD), lambda h, d, n: (d, step(d, n), h))
    oshape = jax.ShapeDtypeStruct((2, seq, R_W), F32)
    return pl.pallas_call(
        body, name="ret_bwd", grid=(R_H, 2, ns),
        in_specs=[blk(0), blk(0), blk(2 * R_H), blk(0),
                  pl.BlockSpec((1, cb, R_HD, R_HD), lambda h, d, n: (d * R_H + h, step(d, n), 0, 0)),
                  big, big, big, big, big, big, vec, vec],
        out_specs=(out, out, out, vec),
        out_shape=(oshape, oshape, oshape, jax.ShapeDtypeStruct((2 * R_H, 1, LANES), F32)),
        scratch_shapes=[pltpu.VMEM((R_HD, R_HD), F32), pltpu.VMEM((c, R_HD), F32),
                        pltpu.VMEM((c, c), F32), pltpu.VMEM((R_HD, R_HD), F32)],
    )(rq, rk, pr, dry, pst, m, mw, qd, qdw, kd, kdw, gdec, gw)


def _group_norm(ry):
    yn, rs = [], []
    for h in range(R_H):
        s = ry[:, h * R_HD:(h + 1) * R_HD]
        mu = jnp.mean(s, axis=-1, keepdims=True)
        cen = s - mu
        r = lax.rsqrt(jnp.mean(cen * cen, axis=-1, keepdims=True) + GN_EPS)
        yn.append(cen * r)
        rs.append(r)
    return yn, rs


def _merge_fwd(x, o, y2, pr, pg, gain_r, wao, wro, wout):
    seq = x.shape[0]
    tm = _tile(seq, 256)

    def body(x_ref, o_ref, yf_ref, yb_ref, rg_ref, ga_ref, gr_ref, gn_ref, wao_ref, wro_ref, wout_ref,
             x1_ref, mg_ref, ri_ref):
        yn, _ = _group_norm(yf_ref[0] + yb_ref[0])
        rg = rg_ref[...]
        ret_in = jnp.concatenate(yn, axis=1) * gn_ref[...] * (rg * _sigmoid(rg))
        ri_ref[...] = _bf(ret_in)
        attn_out = _mm(o_ref[...], wao_ref[...])
        ret_out = _mm(ret_in, wro_ref[...])
        merged = _sigmoid(ga_ref[...]) * attn_out + _sigmoid(gr_ref[...]) * ret_out
        mg_ref[...] = _bf(merged)
        x1_ref[...] = x_ref[...] + _mm(merged, wout_ref[...])

    row = lambda w_, j=0: pl.BlockSpec((tm, w_), lambda i: (i, j))
    ydir = lambda d: pl.BlockSpec((1, tm, R_W), lambda i: (d, i, 0))
    return pl.pallas_call(
        body, name="merge_fwd", grid=(seq // tm,),
        in_specs=[row(D), row(AQ_W), ydir(0), ydir(1), row(R_W, 3), row(D, 0), row(D, 1),
                  _full((1, R_W)), _full((AQ_W, D)), _full((R_W, D)), _full((D, D))],
        out_specs=(row(D), row(D), row(R_W)),
        out_shape=(jax.ShapeDtypeStruct((seq, D), F32), jax.ShapeDtypeStruct((seq, D), _MXU),
                   jax.ShapeDtypeStruct((seq, R_W), _MXU)),
    )(x, o, y2, y2, pr, pg, pg, gain_r, wao, wro, wout)


def _mlp_fwd(x1, gain, wup, wdown):
    seq = x1.shape[0]
    tm = _tile(seq, 512)
    fc = 512
    nfc = FF // fc

    def body(x_ref, g_ref, wu_ref, wd_ref, x2_ref, hm_sc, acc_sc):
        c = pl.program_id(1)

        @pl.when(c == 0)
        def _():
            n, _ = _rms(x_ref[...])
            hm_sc[...] = _bf(n * g_ref[...])
            acc_sc[...] = jnp.zeros_like(acc_sc)

        u = jnp.maximum(_mm(hm_sc[...], wu_ref[...]), 0.0)
        acc_sc[...] += _mm(u * u, wd_ref[...])

        @pl.when(c == nfc - 1)
        def _():
            x2_ref[...] = x_ref[...] + acc_sc[...]

    return pl.pallas_call(
        body, name="mlp_fwd", grid=(seq // tm, nfc),
        in_specs=[pl.BlockSpec((tm, D), lambda i, c: (i, 0)), pl.BlockSpec((1, D), lambda i, c: (0, 0)),
                  pl.BlockSpec((D, fc), lambda i, c: (0, c)), pl.BlockSpec((fc, D), lambda i, c: (c, 0))],
        out_specs=pl.BlockSpec((tm, D), lambda i, c: (i, 0)),
        out_shape=jax.ShapeDtypeStruct((seq, D), F32),
        scratch_shapes=[pltpu.VMEM((tm, D), _MXU), pltpu.VMEM((tm, D), F32)],
    )(x1, gain, wup, wdown)


def _ple_loss(x2, p, tgt, g_ple, g_fin, wpg, wpgt, wple):
    seq = x2.shape[0]
    tm = _tile(seq, 256)

    def body(x2_ref, p_ref, t_ref, gp_ref, gf_ref, wpg_ref, wpgt_ref, wple_ref,
             dx2_ref, de_ref, dz_ref, hp_ref, loss_ref, dgf_ref, dgp_ref):
        @pl.when(pl.program_id(0) == 0)
        def _():
            loss_ref[...] = jnp.zeros_like(loss_ref)
            dgf_ref[...] = jnp.zeros_like(dgf_ref)
            dgp_ref[...] = jnp.zeros_like(dgp_ref)

        x2 = x2_ref[...]
        gp, gf = gp_ref[...], gf_ref[...]
        n2, r2 = _rms(x2)
        hp = _bf(n2 * gp)
        hp_ref[...] = hp
        gate = _sigmoid(_mm(hp, wpg_ref[...]))
        e = _mm(p_ref[...], wple_ref[...])
        x3 = x2 + gate * e
        n3, r3 = _rms(x3)
        diff = n3 * gf - t_ref[...]
        row_loss = jnp.mean(diff * diff, axis=-1, keepdims=True)
        loss_ref[...] += 0.5 * jnp.sum(row_loss, axis=0, keepdims=True)
        dy = diff * (1.0 / D)
        dgf_ref[...] += jnp.sum(dy * n3, axis=0, keepdims=True)
        dx3 = _rms_bwd(n3, r3, gf, dy)
        de_ref[...] = _bf(dx3 * gate)
        dz = dx3 * e * gate * (1.0 - gate)
        dz_ref[...] = _bf(dz)
        dhp = _mm(dz, wpgt_ref[...])
        dgp_ref[...] += jnp.sum(dhp * n2, axis=0, keepdims=True)
        dx2_ref[...] = dx3 + _rms_bwd(n2, r2, gp, dhp)

    row = lambda w_: pl.BlockSpec((tm, w_), lambda i: (i, 0))
    act = lambda dt: jax.ShapeDtypeStruct((seq, D), dt)
    return pl.pallas_call(
        body, name="ple_loss", grid=(seq // tm,),
        in_specs=[row(D), row(PLE), row(D), _full((1, D)), _full((1, D)),
                  _full((D, D)), _full((D, D)), _full((PLE, D))],
        out_specs=(row(D), row(D), row(D), row(D), _full((1, LANES)), _full((1, D)), _full((1, D))),
        out_shape=(act(F32), act(_MXU), act(_MXU), act(_MXU), jax.ShapeDtypeStruct((1, LANES), F32),
                   jax.ShapeDtypeStruct((1, D), F32), jax.ShapeDtypeStruct((1, D), F32)),
    )(x2, p, tgt, g_ple, g_fin, wpg, wpgt, wple)


def _mlp_bwd(x1, dx2, gain, wup, wdownt, wupt):
    seq = x1.shape[0]
    tm = _tile(seq, 512)
    fc = 512
    nfc = FF // fc

    def body(x_ref, dx2_ref, g_ref, wu_ref, wdt_ref, wut_ref,
             dx1_ref, a_ref, du_ref, hm_ref, dg_ref, dhm_sc):
        i = pl.program_id(0)
        c = pl.program_id(1)

        @pl.when((i == 0) & (c == 0))
        def _():
            dg_ref[...] = jnp.zeros_like(dg_ref)

        @pl.when(c == 0)
        def _():
            n, _ = _rms(x_ref[...])
            hm_ref[...] = _bf(n * g_ref[...])
            dhm_sc[...] = jnp.zeros_like(dhm_sc)

        u = jnp.maximum(_mm(hm_ref[...], wu_ref[...]), 0.0)
        a_ref[...] = _bf(u * u)
        du = _mm(dx2_ref[...], wdt_ref[...]) * (2.0 * u)
        du_ref[...] = _bf(du)
        dhm_sc[...] += _mm(du, wut_ref[...])

        @pl.when(c == nfc - 1)
        def _():
            n, r = _rms(x_ref[...])
            dhm = dhm_sc[...]
            dg_ref[...] += jnp.sum(dhm * n, axis=0, keepdims=True)
            dx1_ref[...] = dx2_ref[...] + _rms_bwd(n, r, g_ref[...], dhm)

    rowd = pl.BlockSpec((tm, D), lambda i, c: (i, 0))
    rowf = pl.BlockSpec((tm, fc), lambda i, c: (i, c))
    return pl.pallas_call(
        body, name="mlp_bwd", grid=(seq // tm, nfc),
        in_specs=[rowd, rowd, pl.BlockSpec((1, D), lambda i, c: (0, 0)),
                  pl.BlockSpec((D, fc), lambda i, c: (0, c)), pl.BlockSpec((D, fc), lambda i, c: (0, c)),
                  pl.BlockSpec((fc, D), lambda i, c: (c, 0))],
        out_specs=(rowd, rowf, rowf, rowd, pl.BlockSpec((1, D), lambda i, c: (0, 0))),
        out_shape=(jax.ShapeDtypeStruct((seq, D), F32), jax.ShapeDtypeStruct((seq, FF), _MXU),
                   jax.ShapeDtypeStruct((seq, FF), _MXU), jax.ShapeDtypeStruct((seq, D), _MXU),
                   jax.ShapeDtypeStruct((1, D), F32)),
        scratch_shapes=[pltpu.VMEM((tm, D), F32)],
    )(x1, dx2, gain, wup, wdownt, wupt)


def _merge_bwd(dx1, o, y2, pr, pg, gain_r, wao, wro, woutt, waot, wrot):
    seq = dx1.shape[0]
    tm = _tile(seq, 256)

    def body(dx1_ref, o_ref, yf_ref, yb_ref, rg_ref, ga_ref, gr_ref, gn_ref, wao_ref, wro_ref,
             woutt_ref, waot_ref, wrot_ref,
             dpg_ref, dao_ref, dro_ref, do_ref, dry_ref, drg_ref, dgn_ref):
        @pl.when(pl.program_id(0) == 0)
        def _():
            dgn_ref[...] = jnp.zeros_like(dgn_ref)

        yn_l, rs_l = _group_norm(yf_ref[0] + yb_ref[0])
        yn = jnp.concatenate(yn_l, axis=1)
        rg = rg_ref[...]
        gn = gn_ref[...]
        sg = _sigmoid(rg)
        sil = rg * sg
        ret_in = yn * gn * sil
        attn_out = _mm(o_ref[...], wao_ref[...])
        ret_out = _mm(ret_in, wro_ref[...])
        sa = _sigmoid(ga_ref[...])
        sr = _sigmoid(gr_ref[...])
        dm = _mm(dx1_ref[...], woutt_ref[...])
        dpg_ref[:, 0:D] = _bf(dm * attn_out * sa * (1.0 - sa))
        dpg_ref[:, D:2 * D] = _bf(dm * ret_out * sr * (1.0 - sr))
        dao = _bf(dm * sa)
        dro = _bf(dm * sr)
        dao_ref[...] = dao
        dro_ref[...] = dro
        do_ref[...] = _mm(dao, waot_ref[...])
        dri = _mm(dro, wrot_ref[...])
        dgn_ref[...] += jnp.sum(dri * yn * sil, axis=0, keepdims=True)
        drg_ref[...] = _bf(dri * yn * gn * (sg * (1.0 + rg * (1.0 - sg))))
        dyn = dri * gn * sil
        dry = []
        for h in range(R_H):
            dh = dyn[:, h * R_HD:(h + 1) * R_HD]
            dry.append(rs_l[h] * (dh - jnp.mean(dh, axis=-1, keepdims=True)
                                  - yn_l[h] * jnp.mean(dh * yn_l[h], axis=-1, keepdims=True)))
        dry_ref[...] = jnp.concatenate(dry, axis=1)

    row = lambda w_, j=0: pl.BlockSpec((tm, w_), lambda i: (i, j))
    ydir = lambda d: pl.BlockSpec((1, tm, R_W), lambda i: (d, i, 0))
    return pl.pallas_call(
        body, name="merge_bwd", grid=(seq // tm,),
        in_specs=[row(D), row(AQ_W), ydir(0), ydir(1), row(R_W, 3), row(D, 0), row(D, 1),
                  _full((1, R_W)), _full((AQ_W, D)), _full((R_W, D)), _full((D, D)),
                  _full((D, AQ_W)), _full((D, R_W))],
        out_specs=(row(PG_W), row(D), row(D), row(AQ_W), row(R_W), row(R_W), _full((1, R_W))),
        out_shape=(jax.ShapeDtypeStruct((seq, PG_W), _MXU), jax.ShapeDtypeStruct((seq, D), _MXU),
                   jax.ShapeDtypeStruct((seq, D), _MXU), jax.ShapeDtypeStruct((seq, AQ_W), F32),
                   jax.ShapeDtypeStruct((seq, R_W), F32), jax.ShapeDtypeStruct((seq, R_W), _MXU),
                   jax.ShapeDtypeStruct((1, R_W), F32)),
    )(dx1, o, y2, y2, pr, pg, pg, gain_r, wao, wro, woutt, waot, wrot)


def _qk_prep_bwd(pa, dqh, dk2, dv2, rdq, rdk, rdv, drg, gq, gk, seg, ca, sa, cr, sr):
    seq = pa.shape[0]
    tm = _tile(seq, 256)

    def body(pa_ref, dqh_ref, dk2_ref, dv2_ref, rdqf_ref, rdqb_ref, rdkf_ref, rdkb_ref, rdvf_ref, rdvb_ref,
             drg_ref, gq_ref, gk_ref, seg_ref, ca_ref, sa_ref, cr_ref, sr_ref,
             dpa_ref, dpr_ref, dgq_ref, dgk_ref):
        @pl.when(pl.program_id(0) == 0)
        def _():
            dgq_ref[...] = jnp.zeros_like(dgq_ref)
            dgk_ref[...] = jnp.zeros_like(dgk_ref)

        ca_, sa_ = ca_ref[...], sa_ref[...]

        def norm_bwd(raw, gain, dy, segm, dg_ref):
            msq = jnp.dot(raw * raw, segm, precision=HIGHEST, preferred_element_type=F32)
            r = lax.rsqrt(msq + EPS)
            n = raw * r
            dg_ref[...] += jnp.sum(dy * n, axis=0, keepdims=True)
            dn = dy * gain
            return r * (dn - n * jnp.dot(dn * n, segm, precision=HIGHEST, preferred_element_type=F32))

        dqn = _rope(dqh_ref[...] * (A_HD ** -0.5), _cat(ca_, 4), -_cat(sa_, 4), A_HD // 2)
        dpa_ref[:, 0:AQ_W] = _bf(norm_bwd(pa_ref[:, 0:AQ_W], gq_ref[...], dqn, seg_ref[...], dgq_ref))
        dkn = _rope(dk2_ref[...], ca_, -sa_, A_HD // 2)
        dpa_ref[:, AQ_W:AQ_W + AKV_W] = _bf(norm_bwd(pa_ref[:, AQ_W:AQ_W + AKV_W], gk_ref[...], dkn,
                                                     seg_ref[0:AKV_W, 0:AKV_W], dgk_ref))
        dpa_ref[:, AQ_W + AKV_W:PA_W] = _bf(dv2_ref[...])
        cr_, sr_ = _cat(cr_ref[...], 4), -_cat(sr_ref[...], 4)
        dpr_ref[:, 0:R_W] = _bf(_rope((rdqf_ref[0] + rdqb_ref[0]) * (R_HD ** -0.5), cr_, sr_, R_HD // 2))
        dpr_ref[:, R_W:2 * R_W] = _bf(_rope(rdkf_ref[0] + rdkb_ref[0], cr_, sr_, R_HD // 2))
        dpr_ref[:, 2 * R_W:3 * R_W] = _bf(rdvf_ref[0] + rdvb_ref[0])
        dpr_ref[:, 3 * R_W:4 * R_W] = drg_ref[...]

    row = lambda w_: pl.BlockSpec((tm, w_), lambda i: (i, 0))
    ydir = lambda d: pl.BlockSpec((1, tm, R_W), lambda i: (d, i, 0))
    return pl.pallas_call(
        body, name="qk_prep_bwd", grid=(seq // tm,),
        in_specs=[row(PA_W), row(AQ_W), row(AKV_W), row(AKV_W), ydir(0), ydir(1), ydir(0), ydir(1),
                  ydir(0), ydir(1), row(R_W), _full((1, AQ_W)), _full((1, AKV_W)), _full((AQ_W, AQ_W)),
                  row(LANES), row(LANES), row(LANES), row(LANES)],
        out_specs=(row(PA_W), row(PR_W), _full((1, AQ_W)), _full((1, AKV_W))),
        out_shape=(jax.ShapeDtypeStruct((seq, PA_W), _MXU), jax.ShapeDtypeStruct((seq, PR_W), _MXU),
                   jax.ShapeDtypeStruct((1, AQ_W), F32), jax.ShapeDtypeStruct((1, AKV_W), F32)),
    )(pa, dqh, dk2, dv2, rdq, rdq, rdk, rdk, rdv, rdv, drg, gq, gk, seg, ca, sa, cr, sr)


def _in_proj_bwd(x, dx1, gain, dpa, dpr, dpg, wint):
    seq = x.shape[0]
    tm = _tile(seq, 256)

    def body(x_ref, dx1_ref, g_ref, dpa_ref, dpr_ref, dpg_ref, wt_ref, dx_ref, dg_ref):
        @pl.when(pl.program_id(0) == 0)
        def _():
            dg_ref[...] = jnp.zeros_like(dg_ref)

        dh = (_mm(dpa_ref[...], wt_ref[0:PA_W, :]) + _mm(dpr_ref[...], wt_ref[PA_W:PA_W + PR_W, :])
              + _mm(dpg_ref[...], wt_ref[PA_W + PR_W:IN_W, :]))
        n, r = _rms(x_ref[...])
        dg_ref[...] += jnp.sum(dh * n, axis=0, keepdims=True)
        dx_ref[...] = dx1_ref[...] + _rms_bwd(n, r, g_ref[...], dh)

    row = lambda w_: pl.BlockSpec((tm, w_), lambda i: (i, 0))
    return pl.pallas_call(
        body, name="in_proj_bwd", grid=(seq // tm,),
        in_specs=[row(D), row(D), _full((1, D)), row(PA_W), row(PR_W), row(PG_W), _full((IN_W, D))],
        out_specs=(row(D), _full((1, D))),
        out_shape=(jax.ShapeDtypeStruct((seq, D), F32), jax.ShapeDtypeStruct((1, D), F32)),
    )(x, dx1, gain, dpa, dpr, dpg, wint)


def _wgrad(a, b, name):
    seq, m = a.shape
    n = b.shape[1]
    tm, tn, ts = _tile(m, 512), _tile(n, 1024), _tile(seq, 512)
    ns = seq // ts

    def body(a_ref, b_ref, o_ref):
        @pl.when(pl.program_id(2) == 0)
        def _():
            o_ref[...] = jnp.zeros_like(o_ref)

        o_ref[...] += _mm_tn(a_ref[...], b_ref[...])

    return pl.pallas_call(
        body, name=name, grid=(m // tm, n // tn, ns),
        in_specs=[pl.BlockSpec((ts, tm), lambda i, j, s: (s, i)), pl.BlockSpec((ts, tn), lambda i, j, s: (s, j))],
        out_specs=pl.BlockSpec((tm, tn), lambda i, j, s: (i, j)),
        out_shape=jax.ShapeDtypeStruct((m, n), F32),
    )(a, b)


def _adamw_math(w, g, m, v):
    m = B1 * m + (1.0 - B1) * g
    v = B2 * v + (1.0 - B2) * (g * g)
    m_hat = m / (1.0 - B1 ** STEP)
    v_hat = v / (1.0 - B2 ** STEP)
    delta = -LR * (m_hat / (jnp.sqrt(v_hat) + ADAM_EPS) + WD * w)
    return delta, m, v


def _adamw_big(land, w, m, v):
    rws, cols = w.shape
    tr = _tile(rws, 256)

    def body(l_ref, w_ref, m_ref, v_ref, g_ref, d_ref, nm_ref, nv_ref):
        g = l_ref[0]
        for j in range(1, N_DEV):
            g = g + l_ref[j]
        g_ref[...] = g
        d_ref[...], nm_ref[...], nv_ref[...] = _adamw_math(w_ref[...], g, m_ref[...], v_ref[...])

    row = pl.BlockSpec((tr, cols), lambda i: (i, 0))
    shp = jax.ShapeDtypeStruct((rws, cols), F32)
    return pl.pallas_call(
        body, name="adamw_shard", grid=(rws // tr,),
        in_specs=[pl.BlockSpec((N_DEV, tr, cols), lambda i: (0, i, 0)), row, row, row],
        out_specs=(row, row, row, row), out_shape=(shp, shp, shp, shp),
    )(land, w, m, v)


def _adamw_small(sland, w, m, v):
    def body(l_ref, w_ref, m_ref, v_ref, g_ref, d_ref, nm_ref, nv_ref, loss_ref):
        s = l_ref[0]
        for j in range(1, N_DEV):
            s = s + l_ref[j]
        w = w_ref[...]
        gq = s[8:9]
        for h in range(1, A_H):
            gq = gq + s[8 + h:9 + h]
        gk = s[16:17] + s[17:18]
        gdec = s[5:6] * _sigmoid(-w[5:6])
        g = jnp.concatenate([s[0:5], gdec, gq, gk], axis=0)
        g_ref[...] = g
        d_ref[...], nm_ref[...], nv_ref[...] = _adamw_math(w, g, m_ref[...], v_ref[...])
        loss_ref[...] = s[6:7, 0:LANES]

    shp = jax.ShapeDtypeStruct((8, PACK_COLS), F32)
    return pl.pallas_call(
        body, name="adamw_small",
        out_shape=(shp, shp, shp, shp, jax.ShapeDtypeStruct((1, LANES), F32)),
    )(sland, w, m, v)


_BIG = (("w_in", D, IN_W, 1), ("w_attn_o", AQ_W, D, 1), ("w_ret_o", R_W, D, 1), ("w_out", D, D, 0),
        ("w_up", D, FF, 1), ("w_down", FF, D, 0), ("w_ple_gate", D, D, 0), ("w_ple", PLE, D, 1))
_SMALL = ("mix_norm", "mlp_norm", "ple_norm", "final_norm", "ret_norm_gain", "ret_decay_logit",
          "attn_q_norm", "attn_k_norm")


def _shard_shape(rows, cols, axis):
    return (rows // N_DEV, cols) if axis == 0 else (rows, cols // N_DEV)


def _pack_shards(shards):
    flat = jnp.concatenate([s.reshape(-1) for s in shards])
    return flat.reshape(-1, PACK_COLS)


def _unpack_gathered(gathered):
    flat = gathered.reshape(N_DEV, -1)
    out, off = {}, 0
    for name, rows, cols, axis in _BIG:
        sr, sc = _shard_shape(rows, cols, axis)
        blk = flat[:, off:off + sr * sc].reshape(N_DEV, sr, sc)
        off += sr * sc
        out[name] = blk.reshape(rows, cols) if axis == 0 else blk.transpose(1, 0, 2).reshape(rows, cols)
    return out


def _pack_full_grads(grads):
    parts = []
    for name, rows, cols, axis in _BIG:
        sr, sc = _shard_shape(rows, cols, axis)
        g = grads[name]
        blk = g.reshape(N_DEV, sr, sc) if axis == 0 else g.reshape(rows, N_DEV, sc).transpose(1, 0, 2)
        parts.append(blk.reshape(N_DEV, -1))
    flat = jnp.concatenate(parts, axis=1)
    return flat.reshape(N_DEV, -1, PACK_COLS)


def _unpack_shard(packed):
    flat = packed.reshape(-1)
    out, off = {}, 0
    for name, rows, cols, axis in _BIG:
        sr, sc = _shard_shape(rows, cols, axis)
        out[name] = flat[off:off + sr * sc].reshape(1, sr, sc)
        off += sr * sc
    return out


def _pack_small(vals):
    rows = [jnp.pad(vals[n].reshape(-1), (0, PACK_COLS - vals[n].size)) for n in _SMALL]
    return jnp.stack(rows)


def _unpack_small(packed, like):
    return {n: packed[i, :like[n].size].reshape(like[n].shape) for i, n in enumerate(_SMALL)}


def _row(v):
    return jnp.pad(v.reshape(-1), (0, PACK_COLS - v.size))


def kernel(x, p, mix_norm, w_in, attn_q_norm, attn_k_norm, ret_decay_logit, ret_norm_gain, w_attn_o, w_ret_o, w_out, mlp_norm, w_up, w_down, ple_norm, w_ple_gate, w_ple, final_norm, loss_target, m_mix_norm, m_w_in, m_attn_q_norm, m_attn_k_norm, m_ret_decay_logit, m_ret_norm_gain, m_w_attn_o, m_w_ret_o, m_w_out, m_mlp_norm, m_w_up, m_w_down, m_ple_norm, m_w_ple_gate, m_w_ple, m_final_norm, v_mix_norm, v_w_in, v_attn_q_norm, v_attn_k_norm, v_ret_decay_logit, v_ret_norm_gain, v_w_attn_o, v_w_ret_o, v_w_out, v_mlp_norm, v_w_up, v_w_down, v_ple_norm, v_w_ple_gate, v_w_ple, v_final_norm):
    args = dict(locals())
    seq = x.shape[1]
    xs = x[0]
    ps = p[0, 0]
    tgt = loss_target[0]

    big_names = [b[0] for b in _BIG]
    wshard = _pack_shards([args[n] for n in big_names])
    wfull = _unpack_gathered(_all_gather(wshard.astype(_MXU)))
    win, wao, wro, wout = wfull["w_in"], wfull["w_attn_o"], wfull["w_ret_o"], wfull["w_out"]
    wup, wdown, wpg, wple = wfull["w_up"], wfull["w_down"], wfull["w_ple_gate"], wfull["w_ple"]

    g_mix, g_mlp, g_ple = mix_norm, mlp_norm, ple_norm
    g_fin = final_norm.reshape(1, D)
    gq = jnp.tile(attn_q_norm, (1, A_H))
    gk = jnp.tile(attn_k_norm, (1, A_KV))
    seg = _seg_mean_matrix()
    ca, sa, cr, sr = _rope_tables(seq)

    pa, pr, pg, h = _in_proj(xs, g_mix, win)
    qh, kh, vh, rqh, rkh = _qk_prep(pa, pr, gq, gk, seg, ca, sa, cr, sr)

    tq = _tile(seq, 128)
    tk = _tile(seq, 512)
    qpad = _pad_heads(qh)
    ot, lse = _attn_fwd(qpad, kh, _chunk_t(vh, tk), tq, tk)
    o = _heads_to_rows(ot)

    zb = jnp.broadcast_to(ret_decay_logit.reshape(2 * R_H, 1, 1), (2 * R_H, 1, LANES))
    tm_, tmw, tqd, tqdw, tkd, tkdw, tg, tgw = _ret_tables(zb)
    cb = _tile(seq // CHUNK, 8)
    y2, pst = _ret_fwd(rqh, rkh, pr, tm_, tqd, tkd, tg, cb)

    x1, merged, ret_in = _merge_fwd(xs, o, y2, pr, pg, ret_norm_gain, wao, wro, wout)
    x2 = _mlp_fwd(x1, g_mlp, wup, wdown)

    dx2, de, dz, hp, loss_p, dg_fin, dg_ple = _ple_loss(x2, ps, tgt, g_ple, g_fin, wpg, wpg.T, wple)
    dx1, act, du, hm, dg_mlp = _mlp_bwd(x1, dx2, g_mlp, wup, wdown.T, wup.T)
    dpg, dao, dro, do, dry, drg, dg_gn = _merge_bwd(dx1, o, y2, pr, pg, ret_norm_gain, wao, wro,
                                                    wout.T, wao.T, wro.T)
    rdq, rdk, rdv, dlam = _ret_bwd(rqh, rkh, pr, dry, pst, tm_, tmw, tqd, tqdw, tkd, tkdw, tg, tgw, cb)

    ksplit = 2 if seq // 2 >= tk else 1
    dot_ = do.reshape(seq, A_H, A_HD).transpose(1, 2, 0)
    dk2, dv2, dqt = _attn_bwd(qpad, _pad_heads(_bf(do)), ot, dot_, lse, kh, vh, _chunk_t(kh, tk),
                              tq, tk, ksplit)
    dqh = _heads_to_rows(jnp.sum(dqt, axis=0))
    dpa, dpr, dg_q, dg_k = _qk_prep_bwd(pa, dqh, dk2, dv2, rdq, rdk, rdv, drg, gq, gk, seg, ca, sa, cr, sr)
    grad_x, dg_mix = _in_proj_bwd(xs, dx1, g_mix, dpa, dpr, dpg, win.T)

    wg = {
        "w_in": jnp.concatenate([_wgrad(h, dpa, "wgrad_in_a"), _wgrad(h, dpr, "wgrad_in_r"),
                                 _wgrad(h, dpg, "wgrad_in_g")], axis=1),
        "w_attn_o": _wgrad(o, dao, "wgrad_attn_o"),
        "w_ret_o": _wgrad(ret_in, dro, "wgrad_ret_o"),
        "w_out": _wgrad(merged, dx1, "wgrad_out"),
        "w_up": _wgrad(hm, du, "wgrad_up"),
        "w_down": _wgrad(act, dx2, "wgrad_down"),
        "w_ple_gate": _wgrad(hp, dz, "wgrad_ple_gate"),
        "w_ple": _wgrad(ps, de, "wgrad_ple"),
    }
    gpack = _pack_full_grads(wg)
    small = jnp.stack(
        [_row(dg_mix), _row(dg_mlp), _row(dg_ple), _row(dg_fin), _row(dg_gn), _row(dlam[:, 0, 0]),
         _row(loss_p[0, 0:1]), jnp.zeros((PACK_COLS,), F32)]
        + [_row(dg_q[0, hh * A_HD:(hh + 1) * A_HD]) for hh in range(A_H)]
        + [_row(dg_k[0, hh * A_HD:(hh + 1) * A_HD]) for hh in range(A_KV)]
        + [jnp.zeros((PACK_COLS,), F32)] * (SMALL_ROWS - 18))

    land, sland = _exchange_grads(gpack, small)
    g_sh, d_sh, m_sh, v_sh = _adamw_big(land, wshard, _pack_shards([args["m_" + n] for n in big_names]),
                                        _pack_shards([args["v_" + n] for n in big_names]))
    g_sm, d_sm, m_sm, v_sm, loss_row = _adamw_small(
        sland, _pack_small({n: args[n] for n in _SMALL}), _pack_small({n: args["m_" + n] for n in _SMALL}),
        _pack_small({n: args["v_" + n] for n in _SMALL}))

    names = ["mix_norm", "w_in", "attn_q_norm", "attn_k_norm", "ret_decay_logit", "ret_norm_gain", "w_attn_o",
             "w_ret_o", "w_out", "mlp_norm", "w_up", "w_down", "ple_norm", "w_ple_gate", "w_ple", "final_norm"]
    like = {n: args[n] for n in _SMALL}
    outs = [loss_row[0, 0], grad_x[None]]
    for big, sm in ((g_sh, g_sm), (d_sh, d_sm), (m_sh, m_sm), (v_sh, v_sm)):
        table = {**_unpack_shard(big), **_unpack_small(sm, like)}
        outs += [table[n] for n in names]
    return tuple(outs)
```

```python
import functools

import jax
import jax.numpy as jnp
from jax import lax
from jax.experimental import pallas as pl
from jax.experimental.pallas import tpu as pltpu

F32 = jnp.float32
_MXU = jnp.bfloat16

D = 1024
PLE = 256
GRID_W = 64
A_HD = 64
A_H = 8
A_KV = 2
A_G = A_H // A_KV
AQ_W = A_H * A_HD
AKV_W = A_KV * A_HD
R_HD = 128
R_H = 4
R_W = R_H * R_HD
IN_W = AQ_W + 2 * AKV_W + 4 * R_W + 2 * D
PA_W = AQ_W + 2 * AKV_W
PR_W = 4 * R_W
PG_W = 2 * D
FF = 4 * D
CHUNK = 128
ROPE_THETA = 10000.0
EPS = 1e-6
GN_EPS = 1e-5
N_DEV = 8

LR, B1, B2, ADAM_EPS, WD, STEP = 0.001, 0.9, 0.999, 1e-08, 0.01, 10

LANES = 128
PACK_COLS = 1024
SMALL_ROWS = 24
HIGHEST = lax.Precision.HIGHEST


def _tile(n, pref):
    t = min(n, pref)
    assert n % t == 0, (n, t)
    return t


def _bf(a):
    return a.astype(_MXU)


def _mm(a, b):
    return jnp.dot(_bf(a), _bf(b), preferred_element_type=F32)


def _mm_nt(a, b):
    return lax.dot_general(_bf(a), _bf(b), (((1,), (1,)), ((), ())), preferred_element_type=F32)


def _mm_tn(a, b):
    return lax.dot_general(_bf(a), _bf(b), (((0,), (0,)), ((), ())), preferred_element_type=F32)


def _sigmoid(z):
    return 1.0 / (1.0 + jnp.exp(-z))


def _rms(x):
    r = lax.rsqrt(jnp.mean(x * x, axis=-1, keepdims=True) + EPS)
    return x * r, r


def _rms_bwd(n, r, gain, dy):
    dn = dy * gain
    return r * (dn - n * jnp.mean(dn * n, axis=-1, keepdims=True))


def _swap_halves(x, half):
    n = x.shape[-1]
    lane = lax.broadcasted_iota(jnp.int32, x.shape, x.ndim - 1)
    first = (lane % (2 * half)) < half
    return jnp.where(first, pltpu.roll(x, n - half, axis=1), pltpu.roll(x, half, axis=1))


def _rope(x, cos, sin, half):
    return x * cos + _swap_halves(x, half) * sin


def _cat(t, reps):
    return jnp.concatenate([t] * reps, axis=1)


def _full(shape):
    nd = len(shape)
    return pl.BlockSpec(shape, lambda *_: (0,) * nd)


def _rope_tables(seq):
    def tab(head_dim):
        n_axis = head_dim // 4
        freqs = ROPE_THETA ** (-jnp.arange(n_axis, dtype=F32) / n_axis)
        rows = seq // GRID_W
        row = jnp.repeat(jnp.arange(rows, dtype=F32), GRID_W)
        col = jnp.tile(jnp.arange(GRID_W, dtype=F32), rows)
        ang = jnp.concatenate([row[:, None] * freqs, col[:, None] * freqs], axis=-1)
        c, s = jnp.cos(ang), jnp.sin(ang)
        return jnp.concatenate([c, c], axis=-1), jnp.concatenate([-s, s], axis=-1)
    ca, sa = tab(A_HD)
    cr, sr = tab(R_HD)
    return jnp.tile(ca, (1, 2)), jnp.tile(sa, (1, 2)), cr, sr


def _seg_mean_matrix():
    i = jnp.arange(AQ_W) // A_HD
    return (i[:, None] == i[None, :]).astype(F32) / A_HD


def _mesh_pos():
    return lax.axis_index("x"), lax.axis_index("y"), lax.axis_index("c")


def _all_gather(shards):
    n = len(shards)

    def body(*refs):
        x_refs, out_refs = refs[:n], refs[n:2 * n]
        send_sems, recv_sems, local_sems = refs[2 * n:]
        x, y, c = _mesh_pos()
        me, sibling = (x, y, c), (x, y, 1 - c)
        chips = [(1 - x, y), (x, 1 - y), (1 - x, 1 - y)]

        def slot(t, px, py, pc):
            return out_refs[t].at[4 * px + 2 * py + pc]

        def copy(t, k, block, to, src=None):
            return pltpu.make_async_remote_copy(
                src_ref=slot(t, *block) if src is None else src, dst_ref=slot(t, *block),
                send_sem=send_sems.at[7 * t + k], recv_sem=recv_sems.at[7 * t + k],
                device_id=to, device_id_type=pl.DeviceIdType.MESH)

        mine = [pltpu.make_async_copy(x_refs[t], slot(t, *me), local_sems.at[t]) for t in range(n)]
        for cp in mine:
            cp.start()
        first = []
        for t in range(n):
            first.append(copy(t, 0, me, sibling, src=x_refs[t]))
            first += [copy(t, 1 + j, me, (*chip, c), src=x_refs[t]) for j, chip in enumerate(chips)]
        for cp in first:
            cp.start()
        passed = []
        for t in range(n):
            for j, chip in enumerate(chips):
                copy(t, 1 + j, (*chip, c), me).wait_recv()
                passed.append(copy(t, 4 + j, (*chip, c), sibling))
                passed[-1].start()
        for t in range(n):
            copy(t, 0, sibling, me).wait_recv()
            for j, chip in enumerate(chips):
                copy(t, 4 + j, (*chip, 1 - c), me).wait_recv()
        for cp in first + passed:
            cp.wait_send()
        for cp in mine:
            cp.wait()

    anyspec = pl.BlockSpec(memory_space=pl.ANY)
    return pl.pallas_call(
        body, name="all_gather_weights",
        out_shape=tuple(jax.ShapeDtypeStruct((N_DEV,) + s.shape, s.dtype) for s in shards),
        in_specs=[anyspec] * n, out_specs=(anyspec,) * n,
        scratch_shapes=[pltpu.SemaphoreType.DMA((7 * n,)), pltpu.SemaphoreType.DMA((7 * n,)),
                        pltpu.SemaphoreType.DMA((n,))],
    )(*shards)


def _exchange_grads(packs):
    n = len(packs)

    def body(*refs):
        g_refs, land_refs = refs[:n], refs[n:2 * n]
        send_sems, recv_sems, local_sems = refs[2 * n:]
        x, y, c = _mesh_pos()
        me = 4 * x + 2 * y + c

        def row(t, j):
            return g_refs[t].at[j if packs[t].shape[0] == N_DEV else 0]

        own = [pltpu.make_async_copy(row(t, me), land_refs[t].at[me], local_sems.at[t]) for t in range(n)]
        for cp in own:
            cp.start()
        sends = []
        for k in range(1, N_DEV):
            peer = (x ^ ((k >> 2) & 1), y ^ ((k >> 1) & 1), c ^ (k & 1))
            pidx = 4 * peer[0] + 2 * peer[1] + peer[2]
            for t in range(n):
                sends.append(pltpu.make_async_remote_copy(
                    src_ref=row(t, pidx), dst_ref=land_refs[t].at[me],
                    send_sem=send_sems.at[7 * t + k - 1], recv_sem=recv_sems.at[7 * t + k - 1],
                    device_id=peer, device_id_type=pl.DeviceIdType.MESH))
                sends[-1].start()
        for k in range(1, N_DEV):
            peer = (x ^ ((k >> 2) & 1), y ^ ((k >> 1) & 1), c ^ (k & 1))
            pidx = 4 * peer[0] + 2 * peer[1] + peer[2]
            for t in range(n):
                pltpu.make_async_remote_copy(
                    src_ref=row(t, me), dst_ref=land_refs[t].at[pidx],
                    send_sem=send_sems.at[7 * t + k - 1], recv_sem=recv_sems.at[7 * t + k - 1],
                    device_id=peer, device_id_type=pl.DeviceIdType.MESH).wait_recv()
        for cp in sends:
            cp.wait_send()
        for cp in own:
            cp.wait()

    anyspec = pl.BlockSpec(memory_space=pl.ANY)
    return pl.pallas_call(
        body, name="exchange_grads",
        out_shape=tuple(jax.ShapeDtypeStruct((N_DEV,) + g.shape[1:], g.dtype) for g in packs),
        in_specs=[anyspec] * n, out_specs=(anyspec,) * n,
        scratch_shapes=[pltpu.SemaphoreType.DMA((7 * n,)), pltpu.SemaphoreType.DMA((7 * n,)),
                        pltpu.SemaphoreType.DMA((n,))],
    )(*packs)


def _in_proj(x, gain, w):
    seq = x.shape[0]
    tm = _tile(seq, 256)

    def body(x_ref, g_ref, w_ref, pa_ref, pr_ref, pg_ref, h_ref):
        n, _ = _rms(x_ref[...])
        h = _bf(n * g_ref[...])
        h_ref[...] = h
        pa_ref[...] = _mm(h, w_ref[:, 0:PA_W])
        pr_ref[...] = _mm(h, w_ref[:, PA_W:PA_W + PR_W])
        pg_ref[...] = _mm(h, w_ref[:, PA_W + PR_W:IN_W])

    row = lambda w_: pl.BlockSpec((tm, w_), lambda i: (i, 0))
    return pl.pallas_call(
        body, name="in_proj", grid=(seq // tm,),
        in_specs=[row(D), _full((1, D)), _full((D, IN_W))],
        out_specs=(row(PA_W), row(PR_W), row(PG_W), row(D)),
        out_shape=(jax.ShapeDtypeStruct((seq, PA_W), F32), jax.ShapeDtypeStruct((seq, PR_W), F32),
                   jax.ShapeDtypeStruct((seq, PG_W), F32), jax.ShapeDtypeStruct((seq, D), _MXU)),
    )(x, gain, w)


def _qk_prep(pa, pr, gq, gk, seg, ca, sa, cr, sr):
    seq = pa.shape[0]
    tm = _tile(seq, 256)

    def body(pa_ref, pr_ref, gq_ref, gk_ref, seg_ref, ca_ref, sa_ref, cr_ref, sr_ref,
             qh_ref, kh_ref, v_ref, rq_ref, rk_ref):
        q = pa_ref[:, 0:AQ_W]
        k = pa_ref[:, AQ_W:AQ_W + AKV_W]
        v_ref[...] = _bf(pa_ref[:, AQ_W + AKV_W:PA_W])
        ca_, sa_ = ca_ref[...], sa_ref[...]
        msq = jnp.dot(q * q, seg_ref[...], precision=HIGHEST, preferred_element_type=F32)
        qn = q * lax.rsqrt(msq + EPS) * gq_ref[...]
        qh_ref[...] = _bf(_rope(qn, _cat(ca_, 4), _cat(sa_, 4), A_HD // 2) * (A_HD ** -0.5))
        msk = jnp.dot(k * k, seg_ref[0:AKV_W, 0:AKV_W], precision=HIGHEST, preferred_element_type=F32)
        kn = k * lax.rsqrt(msk + EPS) * gk_ref[...]
        kh_ref[...] = _bf(_rope(kn, ca_, sa_, A_HD // 2))
        cr_, sr_ = _cat(cr_ref[...], 4), _cat(sr_ref[...], 4)
        rq_ref[...] = _rope(pr_ref[:, 0:R_W], cr_, sr_, R_HD // 2) * (R_HD ** -0.5)
        rk_ref[...] = _rope(pr_ref[:, R_W:2 * R_W], cr_, sr_, R_HD // 2)

    row = lambda w_: pl.BlockSpec((tm, w_), lambda i: (i, 0))
    return pl.pallas_call(
        body, name="qk_prep", grid=(seq // tm,),
        in_specs=[row(PA_W), row(2 * R_W), _full((1, AQ_W)), _full((1, AKV_W)), _full((AQ_W, AQ_W)),
                  row(LANES), row(LANES), row(LANES), row(LANES)],
        out_specs=(row(AQ_W), row(AKV_W), row(AKV_W), row(R_W), row(R_W)),
        out_shape=(jax.ShapeDtypeStruct((seq, AQ_W), _MXU), jax.ShapeDtypeStruct((seq, AKV_W), _MXU),
                   jax.ShapeDtypeStruct((seq, AKV_W), _MXU), jax.ShapeDtypeStruct((seq, R_W), F32),
                   jax.ShapeDtypeStruct((seq, R_W), F32)),
    )(pa, pr, gq, gk, seg, ca, sa, cr, sr)


def _pad_heads(a):
    seq = a.shape[0]
    t = a.reshape(seq, A_H, A_HD).transpose(1, 0, 2)
    z = jnp.zeros((A_G, seq, A_HD), a.dtype)
    return jnp.concatenate([jnp.concatenate([t[:A_G], z], axis=-1),
                            jnp.concatenate([z, t[A_G:]], axis=-1)], axis=0)


def _chunk_t(a, tk):
    seq = a.shape[0]
    return a.reshape(seq // tk, tk, a.shape[1]).transpose(0, 2, 1)


def _heads_to_rows(t):
    return t.transpose(2, 0, 1).reshape(t.shape[2], AQ_W)


def _attn_fwd(qpad, k2, v2t, tq, tk):
    seq = k2.shape[0]
    nck = seq // tk
    rows = A_G * tq

    assert nck % 2 == 0, nck

    def body(q_ref, k_ref, vt_ref, o_ref, lse_ref, m_sc, l_sc, acc_sc, s_a, s_b, p_a, p_b, al_a, al_b):
        g = pl.program_id(0)
        hrows = pl.ds(pl.multiple_of(g * A_HD, A_HD), A_HD)
        m_sc[...] = jnp.full((1, rows), -jnp.inf, F32)
        l_sc[...] = jnp.zeros((1, rows), F32)
        acc_sc[...] = jnp.zeros((A_HD, rows), F32)
        p_b[...] = jnp.zeros_like(p_b)
        al_b[...] = jnp.ones_like(al_b)

        def scores(c):
            kc = k_ref[pl.ds(pl.multiple_of(c * tk, tk), tk), :]
            return _mm_nt(kc, q_ref[...].reshape(rows, LANES))

        def stage(c, s_cur, s_nxt, p_cur, p_prv, al_cur, al_prv):
            s_nxt[...] = scores(jnp.minimum(c + 1, nck - 1))
            acc_sc[...] = al_prv[...] * acc_sc[...] + _mm(vt_ref[jnp.maximum(c - 1, 0), hrows, :], p_prv[...])
            st = s_cur[...]
            m_old = m_sc[...]
            m_new = jnp.maximum(m_old, jnp.max(st, axis=0, keepdims=True))
            p = jnp.exp(st - m_new)
            alpha = jnp.exp(m_old - m_new)
            al_cur[...] = alpha
            l_sc[...] = alpha * l_sc[...] + jnp.sum(p, axis=0, keepdims=True)
            p_cur[...] = _bf(p)
            m_sc[...] = m_new

        s_a[...] = scores(0)

        def pair(j, carry):
            stage(2 * j, s_a, s_b, p_a, p_b, al_a, al_b)
            stage(2 * j + 1, s_b, s_a, p_b, p_a, al_b, al_a)
            return carry

        lax.fori_loop(0, nck // 2, pair, 0)
        acc = al_b[...] * acc_sc[...] + _mm(vt_ref[nck - 1, hrows, :], p_b[...])
        l = l_sc[...]
        lse = m_sc[...] + jnp.log(l)
        acc = acc * (1.0 / l)
        for a in range(A_G):
            o_ref[a] = acc[:, a * tq:(a + 1) * tq]
            lse_ref[a] = lse[:, a * tq:(a + 1) * tq]

    return pl.pallas_call(
        body, name="attn_fwd", grid=(A_KV, seq // tq),
        in_specs=[pl.BlockSpec((A_G, tq, LANES), lambda g, i: (g, i, 0)),
                  _full((seq, LANES)), _full((nck, LANES, tk))],
        out_specs=(pl.BlockSpec((A_G, A_HD, tq), lambda g, i: (g, 0, i)),
                   pl.BlockSpec((A_G, 1, tq), lambda g, i: (g, 0, i))),
        out_shape=(jax.ShapeDtypeStruct((A_H, A_HD, seq), F32), jax.ShapeDtypeStruct((A_H, 1, seq), F32)),
        scratch_shapes=[pltpu.VMEM((1, rows), F32), pltpu.VMEM((1, rows), F32), pltpu.VMEM((A_HD, rows), F32),
                        pltpu.VMEM((tk, rows), F32), pltpu.VMEM((tk, rows), F32),
                        pltpu.VMEM((tk, rows), _MXU), pltpu.VMEM((tk, rows), _MXU),
                        pltpu.VMEM((1, rows), F32), pltpu.VMEM((1, rows), F32)],
    )(qpad, k2, v2t)


def _attn_bwd(qpad, dopad, qt8, ot, dot_, lse, k2, v2, k2t, tq, tk, ksplit):
    seq = k2.shape[0]
    sh = seq // ksplit
    nck = sh // tk
    rows = A_G * tq
    assert nck % 2 == 0, nck

    def body(q_ref, do_ref, qt_ref, ot_ref, dot_ref, lse_ref, k_ref, v_ref, kt_ref,
             dk_ref, dv_ref, dq_ref, dq_sc, dot_sc, s_a, s_b, dp_a, dp_b, p_a, p_b, ds_a, ds_b):
        g = pl.program_id(1)
        hrows = pl.ds(pl.multiple_of(g * A_HD, A_HD), A_HD)

        @pl.when(pl.program_id(2) == 0)
        def _():
            dk_ref[...] = jnp.zeros_like(dk_ref)
            dv_ref[...] = jnp.zeros_like(dv_ref)

        lse_row = jnp.concatenate([lse_ref[a] for a in range(A_G)], axis=1)
        dd = jnp.concatenate([jnp.sum(ot_ref[a] * dot_ref[a], axis=0, keepdims=True)
                              for a in range(A_G)], axis=1)
        dot_sc[...] = _bf(jnp.concatenate([dot_ref[a] for a in range(A_G)], axis=1))
        dq_sc[...] = jnp.zeros_like(dq_sc)
        p_b[...] = jnp.zeros_like(p_b)
        ds_b[...] = jnp.zeros_like(ds_b)

        def products(c, s_ref, dp_ref):
            sl = pl.ds(pl.multiple_of(c * tk, tk), tk)
            s_ref[...] = _mm_nt(k_ref[sl, :], q_ref[...].reshape(rows, LANES))
            dp_ref[...] = _mm_nt(v_ref[sl, :], do_ref[...].reshape(rows, LANES))

        def accumulate(c, p_ref, ds_ref):
            qt = jnp.concatenate([qt_ref[a] for a in range(A_G)], axis=1)
            dv_ref[0, c] += _mm_nt(dot_sc[...], p_ref[...])
            dk_ref[0, c] += _mm_nt(qt, ds_ref[...])
            dq_sc[...] += _mm(kt_ref[c, hrows, :], ds_ref[...])

        def stage(c, s_cur, dp_cur, s_nxt, dp_nxt, p_cur, ds_cur, p_prv, ds_prv):
            products(jnp.minimum(c + 1, nck - 1), s_nxt, dp_nxt)
            accumulate(jnp.maximum(c - 1, 0), p_prv, ds_prv)
            p = jnp.exp(s_cur[...] - lse_row)
            p_cur[...] = _bf(p)
            ds_cur[...] = _bf(p * (dp_cur[...] - dd))

        products(0, s_a, dp_a)

        def pair(j, carry):
            stage(2 * j, s_a, dp_a, s_b, dp_b, p_a, ds_a, p_b, ds_b)
            stage(2 * j + 1, s_b, dp_b, s_a, dp_a, p_b, ds_b, p_a, ds_a)
            return carry

        lax.fori_loop(0, nck // 2, pair, 0)
        accumulate(nck - 1, p_b, ds_b)
        for a in range(A_G):
            dq_ref[0, a] = dq_sc[:, a * tq:(a + 1) * tq]

    qspec = pl.BlockSpec((A_G, tq, LANES), lambda s, g, i: (g, i, 0))
    tspec = pl.BlockSpec((A_G, A_HD, tq), lambda s, g, i: (g, 0, i))
    kspec = pl.BlockSpec((sh, LANES), lambda s, g, i: (s, 0))
    gspec = pl.BlockSpec((1, nck, A_HD, tk), lambda s, g, i: (g, s, 0, 0))
    gshape = jax.ShapeDtypeStruct((A_KV, seq // tk, A_HD, tk), F32)
    big = lambda dt: pltpu.VMEM((tk, rows), dt)
    return pl.pallas_call(
        body, name="attn_bwd", grid=(ksplit, A_KV, seq // tq),
        in_specs=[qspec, qspec, tspec, tspec, tspec, pl.BlockSpec((A_G, 1, tq), lambda s, g, i: (g, 0, i)),
                  kspec, kspec, pl.BlockSpec((nck, LANES, tk), lambda s, g, i: (s, 0, 0))],
        out_specs=(gspec, gspec, pl.BlockSpec((1, A_G, A_HD, tq), lambda s, g, i: (s, g, 0, i))),
        out_shape=(gshape, gshape, jax.ShapeDtypeStruct((ksplit, A_H, A_HD, seq), F32)),
        scratch_shapes=[pltpu.VMEM((A_HD, rows), F32), pltpu.VMEM((A_HD, rows), _MXU),
                        big(F32), big(F32), big(F32), big(F32), big(_MXU), big(_MXU), big(_MXU), big(_MXU)],
    )(qpad, dopad, qt8, ot, dot_, lse, k2, v2, k2t)


def _chunks_to_rows(t):
    return t.transpose(1, 3, 0, 2).reshape(t.shape[1] * t.shape[3], AKV_W)


def _ret_tables(zb):
    c = CHUNK

    def body(z_ref, m_ref, mw_ref, qd_ref, qdw_ref, kd_ref, kdw_ref, g_ref, gw_ref):
        fwd = pl.program_id(0) < R_H
        z = z_ref[0]
        lam = jnp.minimum(z, 0.0) - jnp.log(1.0 + jnp.exp(-jnp.abs(z)))
        i = lax.broadcasted_iota(jnp.int32, (c, c), 0).astype(F32)
        j = lax.broadcasted_iota(jnp.int32, (c, c), 1).astype(F32)
        diff = jnp.where(fwd, i - j, j - i)
        keep = diff >= jnp.where(fwd, 0.0, 1.0)
        dist = jnp.maximum(diff, 0.0)
        m = jnp.where(keep, jnp.exp(lam * dist), 0.0)
        m_ref[0] = m
        mw_ref[0] = m * dist
        fq = jnp.where(fwd, i + 1.0, c - i)
        qd = jnp.exp(lam * fq)
        qd_ref[0] = qd
        qdw_ref[0] = qd * fq
        fk = jnp.where(fwd, c - 1.0 - i, i)
        kd = jnp.exp(lam * fk)
        kd_ref[0] = kd
        kdw_ref[0] = kd * fk
        gdec = jnp.exp(lam * c)
        g_ref[0] = gdec
        gw_ref[0] = gdec * c

    big = pl.BlockSpec((1, c, c), lambda t: (t, 0, 0))
    vec = pl.BlockSpec((1, 1, LANES), lambda t: (t, 0, 0))
    bshape = jax.ShapeDtypeStruct((2 * R_H, c, c), F32)
    vshape = jax.ShapeDtypeStruct((2 * R_H, 1, LANES), F32)
    return pl.pallas_call(
        body, name="ret_tables", grid=(2 * R_H,), in_specs=[vec],
        out_specs=(big, big, big, big, big, big, vec, vec),
        out_shape=(bshape,) * 6 + (vshape, vshape),
    )(zb)


def _ret_fwd(rq, rk, pr, m, qd, kd, gdec, cb):
    seq = rq.shape[0]
    c = CHUNK
    ns = seq // (cb * c)

    def body(q_ref, k_ref, v_ref, m_ref, qd_ref, kd_ref, g_ref, y_ref, pst_ref, p_sc):
        d = pl.program_id(1)

        @pl.when(pl.program_id(2) == 0)
        def _():
            p_sc[...] = jnp.zeros((R_HD, R_HD), F32)

        mm, qdd, kdd, gg = m_ref[0], qd_ref[0], kd_ref[0], g_ref[0]

        def chunk(j, carry):
            cc = jnp.where(d == 0, j, cb - 1 - j)
            sl = pl.ds(pl.multiple_of(cc * c, c), c)
            q, k, v = q_ref[sl, :], k_ref[sl, :], v_ref[sl, :]
            p = p_sc[...]
            pst_ref[0, cc] = p
            a = _mm_nt(q, k) * mm
            y_ref[0, sl, :] = _mm(a, v) + _mm(q * qdd, p)
            p_sc[...] = p * gg + _mm_tn(k * kdd, v)
            return carry

        lax.fori_loop(0, cb, chunk, 0)

    def step(d, n):
        return d * (ns - 1 - n) + (1 - d) * n

    blk = lambda off: pl.BlockSpec((cb * c, R_HD), lambda h, d, n: (step(d, n), off + h))
    big = pl.BlockSpec((1, c, c), lambda h, d, n: (d * R_H + h, 0, 0))
    vec = pl.BlockSpec((1, 1, LANES), lambda h, d, n: (d * R_H + h, 0, 0))
    return pl.pallas_call(
        body, name="ret_fwd", grid=(R_H, 2, ns),
        in_specs=[blk(0), blk(0), blk(2 * R_H), big, big, big, vec],
        out_specs=(pl.BlockSpec((1, cb * c, R_HD), lambda h, d, n: (d, step(d, n), h)),
                   pl.BlockSpec((1, cb, R_HD, R_HD), lambda h, d, n: (d * R_H + h, step(d, n), 0, 0))),
        out_shape=(jax.ShapeDtypeStruct((2, seq, R_W), F32),
                   jax.ShapeDtypeStruct((2 * R_H, seq // c, R_HD, R_HD), F32)),
        scratch_shapes=[pltpu.VMEM((R_HD, R_HD), F32)],
    )(rq, rk, pr, m, qd, kd, gdec)


def _ret_bwd(rq, rk, pr, dry, pst, m, mw, qd, qdw, kd, kdw, gdec, gw, cb):
    seq = rq.shape[0]
    c = CHUNK
    ns = seq // (cb * c)

    def body(q_ref, k_ref, v_ref, dy_ref, pst_ref, m_ref, mw_ref, qd_ref, qdw_ref, kd_ref, kdw_ref,
             g_ref, gw_ref, dq_ref, dk_ref, dv_ref, dlam_ref, r_sc, acc_sc, e_sc, g_sc):
        d = pl.program_id(1)
        n = pl.program_id(2)

        @pl.when(n == 0)
        def _():
            r_sc[...] = jnp.zeros_like(r_sc)
            acc_sc[...] = jnp.zeros_like(acc_sc)
            e_sc[...] = jnp.zeros_like(e_sc)
            g_sc[...] = jnp.zeros_like(g_sc)

        mm, mww, qdd, qdww, kdd, kdww, gg = (m_ref[0], mw_ref[0], qd_ref[0], qdw_ref[0],
                                             kd_ref[0], kdw_ref[0], g_ref[0])

        def chunk(j, carry):
            cc = jnp.where(d == 0, cb - 1 - j, j)
            sl = pl.ds(pl.multiple_of(cc * c, c), c)
            q, k, v, dy = q_ref[sl, :], k_ref[sl, :], v_ref[sl, :], dy_ref[sl, :]
            p = pst_ref[0, cc]
            r = r_sc[...]
            qk = _mm_nt(q, k)
            a = qk * mm
            ds = _mm_nt(dy, v)
            da = ds * mm
            dyp = _mm_nt(dy, p)
            vr = _mm_nt(v, r)
            dv_ref[0, sl, :] = _mm_tn(a, dy) + _mm(k * kdd, r)
            dq_ref[0, sl, :] = _mm(da, k) + dyp * qdd
            dk_ref[0, sl, :] = _mm_tn(da, q) + vr * kdd
            acc_sc[...] += dyp * q * qdww + vr * k * kdww
            e_sc[...] += ds * qk * mww
            g_sc[...] += r * p
            r_sc[...] = r * gg + _mm_tn(q * qdd, dy)
            return carry

        lax.fori_loop(0, cb, chunk, 0)

        @pl.when(n == ns - 1)
        def _():
            tot = (jnp.sum(jnp.sum(acc_sc[...] + e_sc[...] + g_sc[...] * gw_ref[0], axis=0, keepdims=True),
                           axis=1, keepdims=True))
            dlam_ref[0] = jnp.broadcast_to(tot, (1, LANES))

    def step(d, n):
        return d * n + (1 - d) * (ns - 1 - n)

    blk = lambda off: pl.BlockSpec((cb * c, R_HD), lambda h, d, n: (step(d, n), off + h))
    big = pl.BlockSpec((1, c, c), lambda h, d, n: (d * R_H + h, 0, 0))
    vec = pl.BlockSpec((1, 1, LANES), lambda h, d, n: (d * R_H + h, 0, 0))
    out = pl.BlockSpec((1, cb * c, R_HD), lambda h, d, n: (d, step(d, n), h))
    oshape = jax.ShapeDtypeStruct((2, seq, R_W), F32)
    return pl.pallas_call(
        body, name="ret_bwd", grid=(R_H, 2, ns),
        in_specs=[blk(0), blk(0), blk(2 * R_H), blk(0),
                  pl.BlockSpec((1, cb, R_HD, R_HD), lambda h, d, n: (d * R_H + h, step(d, n), 0, 0)),
                  big, big, big, big, big, big, vec, vec],
        out_specs=(out, out, out, vec),
        out_shape=(oshape, oshape, oshape, jax.ShapeDtypeStruct((2 * R_H, 1, LANES), F32)),
        scratch_shapes=[pltpu.VMEM((R_HD, R_HD), F32), pltpu.VMEM((c, R_HD), F32),
                        pltpu.VMEM((c, c), F32), pltpu.VMEM((R_HD, R_HD), F32)],
    )(rq, rk, pr, dry, pst, m, mw, qd, qdw, kd, kdw, gdec, gw)


def _group_norm(ry):
    yn, rs = [], []
    for h in range(R_H):
        s = ry[:, h * R_HD:(h + 1) * R_HD]
        mu = jnp.mean(s, axis=-1, keepdims=True)
        cen = s - mu
        r = lax.rsqrt(jnp.mean(cen * cen, axis=-1, keepdims=True) + GN_EPS)
        yn.append(cen * r)
        rs.append(r)
    return yn, rs


def _merge_fwd(x, o, y2, pr, pg, gain_r, wao, wro, wout):
    seq = x.shape[0]
    tm = _tile(seq, 256)

    def body(x_ref, o_ref, yf_ref, yb_ref, rg_ref, ga_ref, gr_ref, gn_ref, wao_ref, wro_ref, wout_ref,
             x1_ref, mg_ref, ri_ref):
        yn, _ = _group_norm(yf_ref[0] + yb_ref[0])
        rg = rg_ref[...]
        ret_in = jnp.concatenate(yn, axis=1) * gn_ref[...] * (rg * _sigmoid(rg))
        ri_ref[...] = _bf(ret_in)
        attn_out = _mm(o_ref[...], wao_ref[...])
        ret_out = _mm(ret_in, wro_ref[...])
        merged = _sigmoid(ga_ref[...]) * attn_out + _sigmoid(gr_ref[...]) * ret_out
        mg_ref[...] = _bf(merged)
        x1_ref[...] = x_ref[...] + _mm(merged, wout_ref[...])

    row = lambda w_, j=0: pl.BlockSpec((tm, w_), lambda i: (i, j))
    ydir = lambda d: pl.BlockSpec((1, tm, R_W), lambda i: (d, i, 0))
    return pl.pallas_call(
        body, name="merge_fwd", grid=(seq // tm,),
        in_specs=[row(D), row(AQ_W), ydir(0), ydir(1), row(R_W, 3), row(D, 0), row(D, 1),
                  _full((1, R_W)), _full((AQ_W, D)), _full((R_W, D)), _full((D, D))],
        out_specs=(row(D), row(D), row(R_W)),
        out_shape=(jax.ShapeDtypeStruct((seq, D), F32), jax.ShapeDtypeStruct((seq, D), _MXU),
                   jax.ShapeDtypeStruct((seq, R_W), _MXU)),
    )(x, o, y2, y2, pr, pg, pg, gain_r, wao, wro, wout)


def _mlp_fwd(x1, gain, wup, wdown):
    seq = x1.shape[0]
    tm = _tile(seq, 512)
    fc = 512
    nfc = FF // fc

    def body(x_ref, g_ref, wu_ref, wd_ref, x2_ref, hm_sc, acc_sc):
        c = pl.program_id(1)

        @pl.when(c == 0)
        def _():
            n, _ = _rms(x_ref[...])
            hm_sc[...] = _bf(n * g_ref[...])
            acc_sc[...] = jnp.zeros_like(acc_sc)

        u = jnp.maximum(_mm(hm_sc[...], wu_ref[...]), 0.0)
        acc_sc[...] += _mm(u * u, wd_ref[...])

        @pl.when(c == nfc - 1)
        def _():
            x2_ref[...] = x_ref[...] + acc_sc[...]

    return pl.pallas_call(
        body, name="mlp_fwd", grid=(seq // tm, nfc),
        in_specs=[pl.BlockSpec((tm, D), lambda i, c: (i, 0)), pl.BlockSpec((1, D), lambda i, c: (0, 0)),
                  pl.BlockSpec((D, fc), lambda i, c: (0, c)), pl.BlockSpec((fc, D), lambda i, c: (c, 0))],
        out_specs=pl.BlockSpec((tm, D), lambda i, c: (i, 0)),
        out_shape=jax.ShapeDtypeStruct((seq, D), F32),
        scratch_shapes=[pltpu.VMEM((tm, D), _MXU), pltpu.VMEM((tm, D), F32)],
    )(x1, gain, wup, wdown)


def _ple_loss(x2, p, tgt, g_ple, g_fin, wpg, wpgt, wple):
    seq = x2.shape[0]
    tm = _tile(seq, 256)

    def body(x2_ref, p_ref, t_ref, gp_ref, gf_ref, wpg_ref, wpgt_ref, wple_ref,
             dx2_ref, de_ref, dz_ref, hp_ref, loss_ref, dgf_ref, dgp_ref):
        @pl.when(pl.program_id(0) == 0)
        def _():
            loss_ref[...] = jnp.zeros_like(loss_ref)
            dgf_ref[...] = jnp.zeros_like(dgf_ref)
            dgp_ref[...] = jnp.zeros_like(dgp_ref)

        x2 = x2_ref[...]
        gp, gf = gp_ref[...], gf_ref[...]
        n2, r2 = _rms(x2)
        hp = _bf(n2 * gp)
        hp_ref[...] = hp
        gate = _sigmoid(_mm(hp, wpg_ref[...]))
        e = _mm(p_ref[...], wple_ref[...])
        x3 = x2 + gate * e
        n3, r3 = _rms(x3)
        diff = n3 * gf - t_ref[...]
        row_loss = jnp.mean(diff * diff, axis=-1, keepdims=True)
        loss_ref[...] += 0.5 * jnp.sum(row_loss, axis=0, keepdims=True)
        dy = diff * (1.0 / D)
        dgf_ref[...] += jnp.sum(dy * n3, axis=0, keepdims=True)
        dx3 = _rms_bwd(n3, r3, gf, dy)
        de_ref[...] = _bf(dx3 * gate)
        dz = dx3 * e * gate * (1.0 - gate)
        dz_ref[...] = _bf(dz)
        dhp = _mm(dz, wpgt_ref[...])
        dgp_ref[...] += jnp.sum(dhp * n2, axis=0, keepdims=True)
        dx2_ref[...] = dx3 + _rms_bwd(n2, r2, gp, dhp)

    row = lambda w_: pl.BlockSpec((tm, w_), lambda i: (i, 0))
    act = lambda dt: jax.ShapeDtypeStruct((seq, D), dt)
    return pl.pallas_call(
        body, name="ple_loss", grid=(seq // tm,),
        in_specs=[row(D), row(PLE), row(D), _full((1, D)), _full((1, D)),
                  _full((D, D)), _full((D, D)), _full((PLE, D))],
        out_specs=(row(D), row(D), row(D), row(D), _full((1, LANES)), _full((1, D)), _full((1, D))),
        out_shape=(act(F32), act(_MXU), act(_MXU), act(_MXU), jax.ShapeDtypeStruct((1, LANES), F32),
                   jax.ShapeDtypeStruct((1, D), F32), jax.ShapeDtypeStruct((1, D), F32)),
    )(x2, p, tgt, g_ple, g_fin, wpg, wpgt, wple)


def _mlp_bwd(x1, dx2, gain, wup, wdownt, wupt):
    seq = x1.shape[0]
    tm = _tile(seq, 512)
    fc = 512
    nfc = FF // fc

    def body(x_ref, dx2_ref, g_ref, wu_ref, wdt_ref, wut_ref,
             dx1_ref, a_ref, du_ref, hm_ref, dg_ref, dhm_sc):
        i = pl.program_id(0)
        c = pl.program_id(1)

        @pl.when((i == 0) & (c == 0))
        def _():
            dg_ref[...] = jnp.zeros_like(dg_ref)

        @pl.when(c == 0)
        def _():
            n, _ = _rms(x_ref[...])
            hm_ref[...] = _bf(n * g_ref[...])
            dhm_sc[...] = jnp.zeros_like(dhm_sc)

        u = jnp.maximum(_mm(hm_ref[...], wu_ref[...]), 0.0)
        a_ref[...] = _bf(u * u)
        du = _mm(dx2_ref[...], wdt_ref[...]) * (2.0 * u)
        du_ref[...] = _bf(du)
        dhm_sc[...] += _mm(du, wut_ref[...])

        @pl.when(c == nfc - 1)
        def _():
            n, r = _rms(x_ref[...])
            dhm = dhm_sc[...]
            dg_ref[...] += jnp.sum(dhm * n, axis=0, keepdims=True)
            dx1_ref[...] = dx2_ref[...] + _rms_bwd(n, r, g_ref[...], dhm)

    rowd = pl.BlockSpec((tm, D), lambda i, c: (i, 0))
    rowf = pl.BlockSpec((tm, fc), lambda i, c: (i, c))
    return pl.pallas_call(
        body, name="mlp_bwd", grid=(seq // tm, nfc),
        in_specs=[rowd, rowd, pl.BlockSpec((1, D), lambda i, c: (0, 0)),
                  pl.BlockSpec((D, fc), lambda i, c: (0, c)), pl.BlockSpec((D, fc), lambda i, c: (0, c)),
                  pl.BlockSpec((fc, D), lambda i, c: (c, 0))],
        out_specs=(rowd, rowf, rowf, rowd, pl.BlockSpec((1, D), lambda i, c: (0, 0))),
        out_shape=(jax.ShapeDtypeStruct((seq, D), F32), jax.ShapeDtypeStruct((seq, FF), _MXU),
                   jax.ShapeDtypeStruct((seq, FF), _MXU), jax.ShapeDtypeStruct((seq, D), _MXU),
                   jax.ShapeDtypeStruct((1, D), F32)),
        scratch_shapes=[pltpu.VMEM((tm, D), F32)],
    )(x1, dx2, gain, wup, wdownt, wupt)


def _merge_bwd(dx1, o, y2, pr, pg, gain_r, wao, wro, woutt, waot, wrot):
    seq = dx1.shape[0]
    tm = _tile(seq, 256)

    def body(dx1_ref, o_ref, yf_ref, yb_ref, rg_ref, ga_ref, gr_ref, gn_ref, wao_ref, wro_ref,
             woutt_ref, waot_ref, wrot_ref,
             dpg_ref, dao_ref, dro_ref, do_ref, dry_ref, drg_ref, dgn_ref):
        @pl.when(pl.program_id(0) == 0)
        def _():
            dgn_ref[...] = jnp.zeros_like(dgn_ref)

        yn_l, rs_l = _group_norm(yf_ref[0] + yb_ref[0])
        yn = jnp.concatenate(yn_l, axis=1)
        rg = rg_ref[...]
        gn = gn_ref[...]
        sg = _sigmoid(rg)
        sil = rg * sg
        ret_in = yn * gn * sil
        attn_out = _mm(o_ref[...], wao_ref[...])
        ret_out = _mm(ret_in, wro_ref[...])
        sa = _sigmoid(ga_ref[...])
        sr = _sigmoid(gr_ref[...])
        dm = _mm(dx1_ref[...], woutt_ref[...])
        dpg_ref[:, 0:D] = _bf(dm * attn_out * sa * (1.0 - sa))
        dpg_ref[:, D:2 * D] = _bf(dm * ret_out * sr * (1.0 - sr))
        dao = _bf(dm * sa)
        dro = _bf(dm * sr)
        dao_ref[...] = dao
        dro_ref[...] = dro
        do_ref[...] = _mm(dao, waot_ref[...])
        dri = _mm(dro, wrot_ref[...])
        dgn_ref[...] += jnp.sum(dri * yn * sil, axis=0, keepdims=True)
        drg_ref[...] = _bf(dri * yn * gn * (sg * (1.0 + rg * (1.0 - sg))))
        dyn = dri * gn * sil
        dry = []
        for h in range(R_H):
            dh = dyn[:, h * R_HD:(h + 1) * R_HD]
            dry.append(rs_l[h] * (dh - jnp.mean(dh, axis=-1, keepdims=True)
                                  - yn_l[h] * jnp.mean(dh * yn_l[h], axis=-1, keepdims=True)))
        dry_ref[...] = jnp.concatenate(dry, axis=1)

    row = lambda w_, j=0: pl.BlockSpec((tm, w_), lambda i: (i, j))
    ydir = lambda d: pl.BlockSpec((1, tm, R_W), lambda i: (d, i, 0))
    return pl.pallas_call(
        body, name="merge_bwd", grid=(seq // tm,),
        in_specs=[row(D), row(AQ_W), ydir(0), ydir(1), row(R_W, 3), row(D, 0), row(D, 1),
                  _full((1, R_W)), _full((AQ_W, D)), _full((R_W, D)), _full((D, D)),
                  _full((D, AQ_W)), _full((D, R_W))],
        out_specs=(row(PG_W), row(D), row(D), row(AQ_W), row(R_W), row(R_W), _full((1, R_W))),
        out_shape=(jax.ShapeDtypeStruct((seq, PG_W), _MXU), jax.ShapeDtypeStruct((seq, D), _MXU),
                   jax.ShapeDtypeStruct((seq, D), _MXU), jax.ShapeDtypeStruct((seq, AQ_W), F32),
                   jax.ShapeDtypeStruct((seq, R_W), F32), jax.ShapeDtypeStruct((seq, R_W), _MXU),
                   jax.ShapeDtypeStruct((1, R_W), F32)),
    )(dx1, o, y2, y2, pr, pg, pg, gain_r, wao, wro, woutt, waot, wrot)


def _qk_prep_bwd(pa, dqh, dk2, dv2, rdq, rdk, rdv, drg, gq, gk, seg, ca, sa, cr, sr):
    seq = pa.shape[0]
    tm = _tile(seq, 256)

    def body(pa_ref, dqh_ref, dk2_ref, dv2_ref, rdqf_ref, rdqb_ref, rdkf_ref, rdkb_ref, rdvf_ref, rdvb_ref,
             drg_ref, gq_ref, gk_ref, seg_ref, ca_ref, sa_ref, cr_ref, sr_ref,
             dpa_ref, dpr_ref, dgq_ref, dgk_ref):
        @pl.when(pl.program_id(0) == 0)
        def _():
            dgq_ref[...] = jnp.zeros_like(dgq_ref)
            dgk_ref[...] = jnp.zeros_like(dgk_ref)

        ca_, sa_ = ca_ref[...], sa_ref[...]

        def norm_bwd(raw, gain, dy, segm, dg_ref):
            msq = jnp.dot(raw * raw, segm, precision=HIGHEST, preferred_element_type=F32)
            r = lax.rsqrt(msq + EPS)
            n = raw * r
            dg_ref[...] += jnp.sum(dy * n, axis=0, keepdims=True)
            dn = dy * gain
            return r * (dn - n * jnp.dot(dn * n, segm, precision=HIGHEST, preferred_element_type=F32))

        dqn = _rope(dqh_ref[...] * (A_HD ** -0.5), _cat(ca_, 4), -_cat(sa_, 4), A_HD // 2)
        dpa_ref[:, 0:AQ_W] = _bf(norm_bwd(pa_ref[:, 0:AQ_W], gq_ref[...], dqn, seg_ref[...], dgq_ref))
        dkn = _rope(dk2_ref[...], ca_, -sa_, A_HD // 2)
        dpa_ref[:, AQ_W:AQ_W + AKV_W] = _bf(norm_bwd(pa_ref[:, AQ_W:AQ_W + AKV_W], gk_ref[...], dkn,
                                                     seg_ref[0:AKV_W, 0:AKV_W], dgk_ref))
        dpa_ref[:, AQ_W + AKV_W:PA_W] = _bf(dv2_ref[...])
        cr_, sr_ = _cat(cr_ref[...], 4), -_cat(sr_ref[...], 4)
        dpr_ref[:, 0:R_W] = _bf(_rope((rdqf_ref[0] + rdqb_ref[0]) * (R_HD ** -0.5), cr_, sr_, R_HD // 2))
        dpr_ref[:, R_W:2 * R_W] = _bf(_rope(rdkf_ref[0] + rdkb_ref[0], cr_, sr_, R_HD // 2))
        dpr_ref[:, 2 * R_W:3 * R_W] = _bf(rdvf_ref[0] + rdvb_ref[0])
        dpr_ref[:, 3 * R_W:4 * R_W] = drg_ref[...]

    row = lambda w_: pl.BlockSpec((tm, w_), lambda i: (i, 0))
    ydir = lambda d: pl.BlockSpec((1, tm, R_W), lambda i: (d, i, 0))
    return pl.pallas_call(
        body, name="qk_prep_bwd", grid=(seq // tm,),
        in_specs=[row(PA_W), row(AQ_W), row(AKV_W), row(AKV_W), ydir(0), ydir(1), ydir(0), ydir(1),
                  ydir(0), ydir(1), row(R_W), _full((1, AQ_W)), _full((1, AKV_W)), _full((AQ_W, AQ_W)),
                  row(LANES), row(LANES), row(LANES), row(LANES)],
        out_specs=(row(PA_W), row(PR_W), _full((1, AQ_W)), _full((1, AKV_W))),
        out_shape=(jax.ShapeDtypeStruct((seq, PA_W), _MXU), jax.ShapeDtypeStruct((seq, PR_W), _MXU),
                   jax.ShapeDtypeStruct((1, AQ_W), F32), jax.ShapeDtypeStruct((1, AKV_W), F32)),
    )(pa, dqh, dk2, dv2, rdq, rdq, rdk, rdk, rdv, rdv, drg, gq, gk, seg, ca, sa, cr, sr)


def _in_proj_bwd(x, dx1, gain, dpa, dpr, dpg, wint):
    seq = x.shape[0]
    tm = _tile(seq, 256)

    def body(x_ref, dx1_ref, g_ref, dpa_ref, dpr_ref, dpg_ref, wt_ref, dx_ref, dg_ref):
        @pl.when(pl.program_id(0) == 0)
        def _():
            dg_ref[...] = jnp.zeros_like(dg_ref)

        dh = (_mm(dpa_ref[...], wt_ref[0:PA_W, :]) + _mm(dpr_ref[...], wt_ref[PA_W:PA_W + PR_W, :])
              + _mm(dpg_ref[...], wt_ref[PA_W + PR_W:IN_W, :]))
        n, r = _rms(x_ref[...])
        dg_ref[...] += jnp.sum(dh * n, axis=0, keepdims=True)
        dx_ref[...] = dx1_ref[...] + _rms_bwd(n, r, g_ref[...], dh)

    row = lambda w_: pl.BlockSpec((tm, w_), lambda i: (i, 0))
    return pl.pallas_call(
        body, name="in_proj_bwd", grid=(seq // tm,),
        in_specs=[row(D), row(D), _full((1, D)), row(PA_W), row(PR_W), row(PG_W), _full((IN_W, D))],
        out_specs=(row(D), _full((1, D))),
        out_shape=(jax.ShapeDtypeStruct((seq, D), F32), jax.ShapeDtypeStruct((1, D), F32)),
    )(x, dx1, gain, dpa, dpr, dpg, wint)


def _wgrad(a, b, name):
    seq, m = a.shape
    n = b.shape[1]
    tm, tn, ts = _tile(m, 512), _tile(n, 1024), _tile(seq, 512)
    ns = seq // ts

    def body(a_ref, b_ref, o_ref):
        @pl.when(pl.program_id(2) == 0)
        def _():
            o_ref[...] = jnp.zeros_like(o_ref)

        o_ref[...] += _mm_tn(a_ref[...], b_ref[...])

    return pl.pallas_call(
        body, name=name, grid=(m // tm, n // tn, ns),
        in_specs=[pl.BlockSpec((ts, tm), lambda i, j, s: (s, i)), pl.BlockSpec((ts, tn), lambda i, j, s: (s, j))],
        out_specs=pl.BlockSpec((tm, tn), lambda i, j, s: (i, j)),
        out_shape=jax.ShapeDtypeStruct((m, n), F32),
    )(a, b)


def _adamw_math(w, g, m, v):
    m = B1 * m + (1.0 - B1) * g
    v = B2 * v + (1.0 - B2) * (g * g)
    m_hat = m / (1.0 - B1 ** STEP)
    v_hat = v / (1.0 - B2 ** STEP)
    delta = -LR * (m_hat / (jnp.sqrt(v_hat) + ADAM_EPS) + WD * w)
    return delta, m, v


def _adamw_big(land, w, m, v, name):
    rws, cols = w.shape
    tr = next(t for t in range(min(rws, 288), 0, -8) if rws % t == 0)

    def body(l_ref, w_ref, m_ref, v_ref, g_ref, d_ref, nm_ref, nv_ref):
        g = l_ref[0]
        for j in range(1, N_DEV):
            g = g + l_ref[j]
        g_ref[...] = g
        d_ref[...], nm_ref[...], nv_ref[...] = _adamw_math(w_ref[...], g, m_ref[...], v_ref[...])

    row = pl.BlockSpec((tr, cols), lambda i: (i, 0))
    shp = jax.ShapeDtypeStruct((rws, cols), F32)
    return pl.pallas_call(
        body, name=name, grid=(rws // tr,),
        in_specs=[pl.BlockSpec((N_DEV, tr, cols), lambda i: (0, i, 0)), row, row, row],
        out_specs=(row, row, row, row), out_shape=(shp, shp, shp, shp),
    )(land, w, m, v)


def _adamw_small(sland, w, m, v):
    def body(l_ref, w_ref, m_ref, v_ref, g_ref, d_ref, nm_ref, nv_ref, loss_ref):
        s = l_ref[0]
        for j in range(1, N_DEV):
            s = s + l_ref[j]
        w = w_ref[...]
        gq = s[8:9]
        for h in range(1, A_H):
            gq = gq + s[8 + h:9 + h]
        gk = s[16:17] + s[17:18]
        gdec = s[5:6] * _sigmoid(-w[5:6])
        g = jnp.concatenate([s[0:5], gdec, gq, gk], axis=0)
        g_ref[...] = g
        d_ref[...], nm_ref[...], nv_ref[...] = _adamw_math(w, g, m_ref[...], v_ref[...])
        loss_ref[...] = s[6:7, 0:LANES]

    shp = jax.ShapeDtypeStruct((8, PACK_COLS), F32)
    return pl.pallas_call(
        body, name="adamw_small",
        out_shape=(shp, shp, shp, shp, jax.ShapeDtypeStruct((1, LANES), F32)),
    )(sland, w, m, v)


_BIG = (("w_attn_o", AQ_W, D, 1), ("w_ret_o", R_W, D, 1), ("w_out", D, D, 0),
        ("w_up", D, FF, 1), ("w_down", FF, D, 0), ("w_ple_gate", D, D, 0), ("w_ple", PLE, D, 1))
IN_SHARD = IN_W // N_DEV
_SMALL = ("mix_norm", "mlp_norm", "ple_norm", "final_norm", "ret_norm_gain", "ret_decay_logit",
          "attn_q_norm", "attn_k_norm")


def _shard_shape(rows, cols, axis):
    return (rows // N_DEV, cols) if axis == 0 else (rows, cols // N_DEV)


def _pack_shards(shards):
    flat = jnp.concatenate([s.reshape(-1) for s in shards])
    return flat.reshape(-1, PACK_COLS)


def _unpack_gathered(gathered):
    flat = gathered.reshape(N_DEV, -1)
    out, off = {}, 0
    for name, rows, cols, axis in _BIG:
        sr, sc = _shard_shape(rows, cols, axis)
        blk = flat[:, off:off + sr * sc].reshape(N_DEV, sr, sc)
        off += sr * sc
        out[name] = blk.reshape(rows, cols) if axis == 0 else blk.transpose(1, 0, 2).reshape(rows, cols)
    return out


def _pack_full_grads(grads):
    parts = []
    for name, rows, cols, axis in _BIG:
        sr, sc = _shard_shape(rows, cols, axis)
        g = grads[name]
        blk = g.reshape(N_DEV, sr, sc) if axis == 0 else g.reshape(rows, N_DEV, sc).transpose(1, 0, 2)
        parts.append(blk.reshape(N_DEV, -1))
    flat = jnp.concatenate(parts, axis=1)
    return flat.reshape(N_DEV, -1, PACK_COLS)


def _unpack_shard(packed):
    flat = packed.reshape(-1)
    out, off = {}, 0
    for name, rows, cols, axis in _BIG:
        sr, sc = _shard_shape(rows, cols, axis)
        out[name] = flat[off:off + sr * sc].reshape(1, sr, sc)
        off += sr * sc
    return out


def _pack_small(vals):
    rows = [jnp.pad(vals[n].reshape(-1), (0, PACK_COLS - vals[n].size)) for n in _SMALL]
    return jnp.stack(rows)


def _unpack_small(packed, like):
    return {n: packed[i, :like[n].size].reshape(like[n].shape) for i, n in enumerate(_SMALL)}


def _row(v):
    return jnp.pad(v.reshape(-1), (0, PACK_COLS - v.size))


def kernel(x, p, mix_norm, w_in, attn_q_norm, attn_k_norm, ret_decay_logit, ret_norm_gain, w_attn_o, w_ret_o, w_out, mlp_norm, w_up, w_down, ple_norm, w_ple_gate, w_ple, final_norm, loss_target, m_mix_norm, m_w_in, m_attn_q_norm, m_attn_k_norm, m_ret_decay_logit, m_ret_norm_gain, m_w_attn_o, m_w_ret_o, m_w_out, m_mlp_norm, m_w_up, m_w_down, m_ple_norm, m_w_ple_gate, m_w_ple, m_final_norm, v_mix_norm, v_w_in, v_attn_q_norm, v_attn_k_norm, v_ret_decay_logit, v_ret_norm_gain, v_w_attn_o, v_w_ret_o, v_w_out, v_mlp_norm, v_w_up, v_w_down, v_ple_norm, v_w_ple_gate, v_w_ple, v_final_norm):
    args = dict(locals())
    seq = x.shape[1]
    xs = x[0]
    ps = p[0, 0]
    tgt = loss_target[0]

    big_names = [b[0] for b in _BIG]
    wshard = _pack_shards([args[n] for n in big_names])
    win_g, rest_g = _all_gather([w_in[0].astype(_MXU), wshard.astype(_MXU)])
    win = win_g.transpose(1, 0, 2).reshape(D, IN_W)
    wfull = _unpack_gathered(rest_g)
    wao, wro, wout = wfull["w_attn_o"], wfull["w_ret_o"], wfull["w_out"]
    wup, wdown, wpg, wple = wfull["w_up"], wfull["w_down"], wfull["w_ple_gate"], wfull["w_ple"]

    g_mix, g_mlp, g_ple = mix_norm, mlp_norm, ple_norm
    g_fin = final_norm.reshape(1, D)
    gq = jnp.tile(attn_q_norm, (1, A_H))
    gk = jnp.tile(attn_k_norm, (1, A_KV))
    seg = _seg_mean_matrix()
    ca, sa, cr, sr = _rope_tables(seq)

    pa, pr, pg, h = _in_proj(xs, g_mix, win)
    qh, kh, vh, rqh, rkh = _qk_prep(pa, pr, gq, gk, seg, ca, sa, cr, sr)

    tq = _tile(seq, 128)
    tk = _tile(seq // 4, 512)
    qpad = _pad_heads(qh)
    ot, lse = _attn_fwd(qpad, kh, _chunk_t(vh, tk), tq, tk)
    o = _heads_to_rows(ot)

    zb = jnp.broadcast_to(ret_decay_logit.reshape(2 * R_H, 1, 1), (2 * R_H, 1, LANES))
    tm_, tmw, tqd, tqdw, tkd, tkdw, tg, tgw = _ret_tables(zb)
    cb = _tile(seq // CHUNK, 8)
    y2, pst = _ret_fwd(rqh, rkh, pr, tm_, tqd, tkd, tg, cb)

    x1, merged, ret_in = _merge_fwd(xs, o, y2, pr, pg, ret_norm_gain, wao, wro, wout)
    x2 = _mlp_fwd(x1, g_mlp, wup, wdown)

    dx2, de, dz, hp, loss_p, dg_fin, dg_ple = _ple_loss(x2, ps, tgt, g_ple, g_fin, wpg, wpg.T, wple)
    dx1, act, du, hm, dg_mlp = _mlp_bwd(x1, dx2, g_mlp, wup, wdown.T, wup.T)
    dpg, dao, dro, do, dry, drg, dg_gn = _merge_bwd(dx1, o, y2, pr, pg, ret_norm_gain, wao, wro,
                                                    wout.T, wao.T, wro.T)
    rdq, rdk, rdv, dlam = _ret_bwd(rqh, rkh, pr, dry, pst, tm_, tmw, tqd, tqdw, tkd, tkdw, tg, tgw, cb)

    ksplit = 2
    dot_ = do.reshape(seq, A_H, A_HD).transpose(1, 2, 0)
    qt8 = qh.reshape(seq, A_H, A_HD).transpose(1, 2, 0)
    dkt, dvt, dqt = _attn_bwd(qpad, _pad_heads(_bf(do)), qt8, ot, dot_, lse, kh, vh, _chunk_t(kh, tk),
                              tq, tk, ksplit)
    dqh = _heads_to_rows(jnp.sum(dqt, axis=0))
    dpa, dpr, dg_q, dg_k = _qk_prep_bwd(pa, dqh, _chunks_to_rows(dkt), _chunks_to_rows(dvt), rdq, rdk, rdv, drg, gq, gk, seg, ca, sa, cr, sr)
    grad_x, dg_mix = _in_proj_bwd(xs, dx1, g_mix, dpa, dpr, dpg, win.T)

    wg = {
        "w_in": jnp.concatenate([_wgrad(h, dpa, "wgrad_in_a"), _wgrad(h, dpr, "wgrad_in_r"),
                                 _wgrad(h, dpg, "wgrad_in_g")], axis=1),
        "w_attn_o": _wgrad(o, dao, "wgrad_attn_o"),
        "w_ret_o": _wgrad(ret_in, dro, "wgrad_ret_o"),
        "w_out": _wgrad(merged, dx1, "wgrad_out"),
        "w_up": _wgrad(hm, du, "wgrad_up"),
        "w_down": _wgrad(act, dx2, "wgrad_down"),
        "w_ple_gate": _wgrad(hp, dz, "wgrad_ple_gate"),
        "w_ple": _wgrad(ps, de, "wgrad_ple"),
    }
    gpack = _pack_full_grads(wg)
    gpack_in = wg["w_in"].reshape(D, N_DEV, IN_SHARD).transpose(1, 0, 2)
    small = jnp.stack(
        [_row(dg_mix), _row(dg_mlp), _row(dg_ple), _row(dg_fin), _row(dg_gn), _row(dlam[:, 0, 0]),
         _row(loss_p[0, 0:1]), jnp.zeros((PACK_COLS,), F32)]
        + [_row(dg_q[0, hh * A_HD:(hh + 1) * A_HD]) for hh in range(A_H)]
        + [_row(dg_k[0, hh * A_HD:(hh + 1) * A_HD]) for hh in range(A_KV)]
        + [jnp.zeros((PACK_COLS,), F32)] * (SMALL_ROWS - 18))

    land_in, land, sland = _exchange_grads([gpack_in, gpack, small[None]])
    in_sh = _adamw_big(land_in, w_in[0], m_w_in[0], v_w_in[0], "adamw_w_in")
    g_sh, d_sh, m_sh, v_sh = _adamw_big(land, wshard, _pack_shards([args["m_" + n] for n in big_names]),
                                        _pack_shards([args["v_" + n] for n in big_names]), "adamw_shard")
    g_sm, d_sm, m_sm, v_sm, loss_row = _adamw_small(
        sland, _pack_small({n: args[n] for n in _SMALL}), _pack_small({n: args["m_" + n] for n in _SMALL}),
        _pack_small({n: args["v_" + n] for n in _SMALL}))

    names = ["mix_norm", "w_in", "attn_q_norm", "attn_k_norm", "ret_decay_logit", "ret_norm_gain", "w_attn_o",
             "w_ret_o", "w_out", "mlp_norm", "w_up", "w_down", "ple_norm", "w_ple_gate", "w_ple", "final_norm"]
    like = {n: args[n] for n in _SMALL}
    outs = [loss_row[0, 0], grad_x[None]]
    for big, sm, w_in_part in ((g_sh, g_sm, in_sh[0]), (d_sh, d_sm, in_sh[1]), (m_sh, m_sm, in_sh[2]),
                               (v_sh, v_sm, in_sh[3])):
        table = {**_unpack_shard(big), **_unpack_small(sm, like), "w_in": w_in_part[None]}
        outs += [table[n] for n in names]
    return tuple(outs)
```

```python
import functools

import jax
import jax.numpy as jnp
from jax import lax
from jax.experimental import pallas as pl
from jax.experimental.pallas import tpu as pltpu

F32 = jnp.float32
_MXU = jnp.bfloat16

D = 1024
PLE = 256
GRID_W = 64
A_HD = 64
A_H = 8
A_KV = 2
A_G = A_H // A_KV
AQ_W = A_H * A_HD
AKV_W = A_KV * A_HD
R_HD = 128
R_H = 4
R_W = R_H * R_HD
IN_W = AQ_W + 2 * AKV_W + 4 * R_W + 2 * D
PA_W = AQ_W + 2 * AKV_W
PR_W = 4 * R_W
PG_W = 2 * D
FF = 4 * D
CHUNK = 128
ROPE_THETA = 10000.0
EPS = 1e-6
GN_EPS = 1e-5
N_DEV = 8

LR, B1, B2, ADAM_EPS, WD, STEP = 0.001, 0.9, 0.999, 1e-08, 0.01, 10

LANES = 128
PACK_COLS = 1024
SMALL_ROWS = 24
HIGHEST = lax.Precision.HIGHEST


def _tile(n, pref):
    t = min(n, pref)
    assert n % t == 0, (n, t)
    return t


def _bf(a):
    return a.astype(_MXU)


def _mm(a, b):
    return jnp.dot(_bf(a), _bf(b), preferred_element_type=F32)


def _mm_nt(a, b):
    return lax.dot_general(_bf(a), _bf(b), (((1,), (1,)), ((), ())), preferred_element_type=F32)


def _mm_tn(a, b):
    return lax.dot_general(_bf(a), _bf(b), (((0,), (0,)), ((), ())), preferred_element_type=F32)


def _sigmoid(z):
    return 1.0 / (1.0 + jnp.exp(-z))


def _rms(x):
    r = lax.rsqrt(jnp.mean(x * x, axis=-1, keepdims=True) + EPS)
    return x * r, r


def _rms_bwd(n, r, gain, dy):
    dn = dy * gain
    return r * (dn - n * jnp.mean(dn * n, axis=-1, keepdims=True))


def _swap_halves(x, half):
    n = x.shape[-1]
    lane = lax.broadcasted_iota(jnp.int32, x.shape, x.ndim - 1)
    first = (lane % (2 * half)) < half
    return jnp.where(first, pltpu.roll(x, n - half, axis=1), pltpu.roll(x, half, axis=1))


def _rope(x, cos, sin, half):
    return x * cos + _swap_halves(x, half) * sin


def _cat(t, reps):
    return jnp.concatenate([t] * reps, axis=1)


def _full(shape):
    nd = len(shape)
    return pl.BlockSpec(shape, lambda *_: (0,) * nd)


def _rope_tables(seq):
    def tab(head_dim):
        n_axis = head_dim // 4
        freqs = ROPE_THETA ** (-jnp.arange(n_axis, dtype=F32) / n_axis)
        rows = seq // GRID_W
        row = jnp.repeat(jnp.arange(rows, dtype=F32), GRID_W)
        col = jnp.tile(jnp.arange(GRID_W, dtype=F32), rows)
        ang = jnp.concatenate([row[:, None] * freqs, col[:, None] * freqs], axis=-1)
        c, s = jnp.cos(ang), jnp.sin(ang)
        return jnp.concatenate([c, c], axis=-1), jnp.concatenate([-s, s], axis=-1)
    ca, sa = tab(A_HD)
    cr, sr = tab(R_HD)
    return jnp.tile(ca, (1, 2)), jnp.tile(sa, (1, 2)), cr, sr


def _seg_mean_matrix():
    i = jnp.arange(AQ_W) // A_HD
    return (i[:, None] == i[None, :]).astype(F32) / A_HD


def _mesh_pos():
    return lax.axis_index("x"), lax.axis_index("y"), lax.axis_index("c")


def _all_gather(shards):
    n = len(shards)

    def body(*refs):
        x_refs, out_refs = refs[:n], refs[n:2 * n]
        send_sems, recv_sems, local_sems = refs[2 * n:]
        x, y, c = _mesh_pos()
        me, sibling = (x, y, c), (x, y, 1 - c)
        chips = [(1 - x, y), (x, 1 - y), (1 - x, 1 - y)]

        def slot(t, px, py, pc):
            return out_refs[t].at[4 * px + 2 * py + pc]

        def copy(t, k, block, to, src=None):
            return pltpu.make_async_remote_copy(
                src_ref=slot(t, *block) if src is None else src, dst_ref=slot(t, *block),
                send_sem=send_sems.at[7 * t + k], recv_sem=recv_sems.at[7 * t + k],
                device_id=to, device_id_type=pl.DeviceIdType.MESH)

        mine = [pltpu.make_async_copy(x_refs[t], slot(t, *me), local_sems.at[t]) for t in range(n)]
        for cp in mine:
            cp.start()
        first = []
        for t in range(n):
            first.append(copy(t, 0, me, sibling, src=x_refs[t]))
            first += [copy(t, 1 + j, me, (*chip, c), src=x_refs[t]) for j, chip in enumerate(chips)]
        for cp in first:
            cp.start()
        passed = []
        for t in range(n):
            for j, chip in enumerate(chips):
                copy(t, 1 + j, (*chip, c), me).wait_recv()
                passed.append(copy(t, 4 + j, (*chip, c), sibling))
                passed[-1].start()
        for t in range(n):
            copy(t, 0, sibling, me).wait_recv()
            for j, chip in enumerate(chips):
                copy(t, 4 + j, (*chip, 1 - c), me).wait_recv()
        for cp in first + passed:
            cp.wait_send()
        for cp in mine:
            cp.wait()

    anyspec = pl.BlockSpec(memory_space=pl.ANY)
    return pl.pallas_call(
        body, name="all_gather_weights",
        out_shape=tuple(jax.ShapeDtypeStruct((N_DEV,) + s.shape, s.dtype) for s in shards),
        in_specs=[anyspec] * n, out_specs=(anyspec,) * n,
        scratch_shapes=[pltpu.SemaphoreType.DMA((7 * n,)), pltpu.SemaphoreType.DMA((7 * n,)),
                        pltpu.SemaphoreType.DMA((n,))],
    )(*shards)


def _exchange_grads(packs):
    n = len(packs)

    def body(*refs):
        g_refs, land_refs = refs[:n], refs[n:2 * n]
        send_sems, recv_sems, local_sems = refs[2 * n:]
        x, y, c = _mesh_pos()
        me = 4 * x + 2 * y + c

        def row(t, j):
            return g_refs[t].at[j if packs[t].shape[0] == N_DEV else 0]

        own = [pltpu.make_async_copy(row(t, me), land_refs[t].at[me], local_sems.at[t]) for t in range(n)]
        for cp in own:
            cp.start()
        sends = []
        for k in range(1, N_DEV):
            peer = (x ^ ((k >> 2) & 1), y ^ ((k >> 1) & 1), c ^ (k & 1))
            pidx = 4 * peer[0] + 2 * peer[1] + peer[2]
            for t in range(n):
                sends.append(pltpu.make_async_remote_copy(
                    src_ref=row(t, pidx), dst_ref=land_refs[t].at[me],
                    send_sem=send_sems.at[7 * t + k - 1], recv_sem=recv_sems.at[7 * t + k - 1],
                    device_id=peer, device_id_type=pl.DeviceIdType.MESH))
                sends[-1].start()
        for k in range(1, N_DEV):
            peer = (x ^ ((k >> 2) & 1), y ^ ((k >> 1) & 1), c ^ (k & 1))
            pidx = 4 * peer[0] + 2 * peer[1] + peer[2]
            for t in range(n):
                pltpu.make_async_remote_copy(
                    src_ref=row(t, me), dst_ref=land_refs[t].at[pidx],
                    send_sem=send_sems.at[7 * t + k - 1], recv_sem=recv_sems.at[7 * t + k - 1],
                    device_id=peer, device_id_type=pl.DeviceIdType.MESH).wait_recv()
        for cp in sends:
            cp.wait_send()
        for cp in own:
            cp.wait()

    anyspec = pl.BlockSpec(memory_space=pl.ANY)
    return pl.pallas_call(
        body, name="exchange_grads",
        out_shape=tuple(jax.ShapeDtypeStruct((N_DEV,) + g.shape[1:], g.dtype) for g in packs),
        in_specs=[anyspec] * n, out_specs=(anyspec,) * n,
        scratch_shapes=[pltpu.SemaphoreType.DMA((7 * n,)), pltpu.SemaphoreType.DMA((7 * n,)),
                        pltpu.SemaphoreType.DMA((n,))],
    )(*packs)


def _in_proj(x, gain, w):
    seq = x.shape[0]
    tm = _tile(seq, 256)

    def body(x_ref, g_ref, w_ref, pa_ref, pr_ref, pg_ref, h_ref):
        n, _ = _rms(x_ref[...])
        h = _bf(n * g_ref[...])
        h_ref[...] = h
        pa_ref[...] = _mm(h, w_ref[:, 0:PA_W])
        pr_ref[...] = _mm(h, w_ref[:, PA_W:PA_W + PR_W])
        pg_ref[...] = _mm(h, w_ref[:, PA_W + PR_W:IN_W])

    row = lambda w_: pl.BlockSpec((tm, w_), lambda i: (i, 0))
    return pl.pallas_call(
        body, name="in_proj", grid=(seq // tm,),
        in_specs=[row(D), _full((1, D)), _full((D, IN_W))],
        out_specs=(row(PA_W), row(PR_W), row(PG_W), row(D)),
        out_shape=(jax.ShapeDtypeStruct((seq, PA_W), F32), jax.ShapeDtypeStruct((seq, PR_W), F32),
                   jax.ShapeDtypeStruct((seq, PG_W), F32), jax.ShapeDtypeStruct((seq, D), _MXU)),
    )(x, gain, w)


def _qk_prep(pa, pr, gq, gk, seg, ca, sa, cr, sr):
    seq = pa.shape[0]
    tm = _tile(seq, 256)

    def body(pa_ref, pr_ref, gq_ref, gk_ref, seg_ref, ca_ref, sa_ref, cr_ref, sr_ref,
             qh_ref, kh_ref, v_ref, rq_ref, rk_ref):
        q = pa_ref[:, 0:AQ_W]
        k = pa_ref[:, AQ_W:AQ_W + AKV_W]
        v_ref[...] = _bf(pa_ref[:, AQ_W + AKV_W:PA_W])
        ca_, sa_ = ca_ref[...], sa_ref[...]
        msq = jnp.dot(q * q, seg_ref[...], precision=HIGHEST, preferred_element_type=F32)
        qn = q * lax.rsqrt(msq + EPS) * gq_ref[...]
        qh_ref[...] = _bf(_rope(qn, _cat(ca_, 4), _cat(sa_, 4), A_HD // 2) * (A_HD ** -0.5))
        msk = jnp.dot(k * k, seg_ref[0:AKV_W, 0:AKV_W], precision=HIGHEST, preferred_element_type=F32)
        kn = k * lax.rsqrt(msk + EPS) * gk_ref[...]
        kh_ref[...] = _bf(_rope(kn, ca_, sa_, A_HD // 2))
        cr_, sr_ = _cat(cr_ref[...], 4), _cat(sr_ref[...], 4)
        rq_ref[...] = _rope(pr_ref[:, 0:R_W], cr_, sr_, R_HD // 2) * (R_HD ** -0.5)
        rk_ref[...] = _rope(pr_ref[:, R_W:2 * R_W], cr_, sr_, R_HD // 2)

    row = lambda w_: pl.BlockSpec((tm, w_), lambda i: (i, 0))
    return pl.pallas_call(
        body, name="qk_prep", grid=(seq // tm,),
        in_specs=[row(PA_W), row(2 * R_W), _full((1, AQ_W)), _full((1, AKV_W)), _full((AQ_W, AQ_W)),
                  row(LANES), row(LANES), row(LANES), row(LANES)],
        out_specs=(row(AQ_W), row(AKV_W), row(AKV_W), row(R_W), row(R_W)),
        out_shape=(jax.ShapeDtypeStruct((seq, AQ_W), _MXU), jax.ShapeDtypeStruct((seq, AKV_W), _MXU),
                   jax.ShapeDtypeStruct((seq, AKV_W), _MXU), jax.ShapeDtypeStruct((seq, R_W), F32),
                   jax.ShapeDtypeStruct((seq, R_W), F32)),
    )(pa, pr, gq, gk, seg, ca, sa, cr, sr)


def _pad_heads(a):
    seq = a.shape[0]
    t = a.reshape(seq, A_H, A_HD).transpose(1, 0, 2)
    z = jnp.zeros((A_G, seq, A_HD), a.dtype)
    return jnp.concatenate([jnp.concatenate([t[:A_G], z], axis=-1),
                            jnp.concatenate([z, t[A_G:]], axis=-1)], axis=0)


def _chunk_t(a, tk):
    seq = a.shape[0]
    return a.reshape(seq // tk, tk, a.shape[1]).transpose(0, 2, 1)


def _heads_to_rows(t):
    return t.transpose(2, 0, 1).reshape(t.shape[2], AQ_W)


def _attn_fwd(qt8, k2, vta, tq, tk):
    seq = k2.shape[0]
    nck = seq // tk
    rows = A_G * tq
    vrows = vta.shape[2]
    assert nck % 2 == 0, nck

    def body(qt_ref, k_ref, vt_ref, o_ref, lse_ref, m_sc, acc_sc, qtp_sc, s_a, s_b, p_a, p_b, al_a, al_b):
        g = pl.program_id(0)
        qtp_sc[...] = jnp.zeros_like(qtp_sc)
        qtp_sc[pl.ds(pl.multiple_of(g * A_HD, A_HD), A_HD), :] = jnp.concatenate(
            [qt_ref[a] for a in range(A_G)], axis=1)
        m_sc[...] = jnp.full((1, rows), -jnp.inf, F32)
        acc_sc[...] = jnp.zeros_like(acc_sc)
        p_b[...] = jnp.zeros_like(p_b)
        al_b[...] = jnp.ones_like(al_b)

        def scores(c):
            kc = k_ref[pl.ds(pl.multiple_of(c * tk, tk), tk), :]
            return _mm(kc, qtp_sc[...])

        def stage(c, s_cur, s_nxt, p_cur, p_prv, al_cur, al_prv):
            s_nxt[...] = scores(jnp.minimum(c + 1, nck - 1))
            acc_sc[...] = al_prv[...] * acc_sc[...] + _mm(vt_ref[0, jnp.maximum(c - 1, 0)], p_prv[...])
            for b in range(rows // LANES):
                cs = slice(b * LANES, (b + 1) * LANES)
                st = s_cur[:, cs]
                m_old = m_sc[:, cs]
                m_new = jnp.maximum(m_old, jnp.max(st, axis=0, keepdims=True))
                p_cur[:, cs] = _bf(jnp.exp(st - m_new))
                al_cur[:, cs] = jnp.exp(m_old - m_new)
                m_sc[:, cs] = m_new

        s_a[...] = scores(0)

        def pair(j, carry):
            stage(2 * j, s_a, s_b, p_a, p_b, al_a, al_b)
            stage(2 * j + 1, s_b, s_a, p_b, p_a, al_b, al_a)
            return carry

        lax.fori_loop(0, nck // 2, pair, 0)
        acc = al_b[...] * acc_sc[...] + _mm(vt_ref[0, nck - 1], p_b[...])
        l = acc[A_HD:A_HD + 1, :]
        lse = m_sc[...] + jnp.log(l)
        out = acc[0:A_HD, :] * (1.0 / l)
        for a in range(A_G):
            o_ref[a] = out[:, a * tq:(a + 1) * tq]
            lse_ref[a] = lse[:, a * tq:(a + 1) * tq]

    return pl.pallas_call(
        body, name="attn_fwd", grid=(A_KV, seq // tq),
        in_specs=[pl.BlockSpec((A_G, A_HD, tq), lambda g, i: (g, 0, i)),
                  _full((seq, LANES)), pl.BlockSpec((1, nck, vrows, tk), lambda g, i: (g, 0, 0, 0))],
        out_specs=(pl.BlockSpec((A_G, A_HD, tq), lambda g, i: (g, 0, i)),
                   pl.BlockSpec((A_G, 1, tq), lambda g, i: (g, 0, i))),
        out_shape=(jax.ShapeDtypeStruct((A_H, A_HD, seq), F32), jax.ShapeDtypeStruct((A_H, 1, seq), F32)),
        scratch_shapes=[pltpu.VMEM((1, rows), F32), pltpu.VMEM((vrows, rows), F32), pltpu.VMEM((LANES, rows), _MXU),
                        pltpu.VMEM((tk, rows), F32), pltpu.VMEM((tk, rows), F32),
                        pltpu.VMEM((tk, rows), _MXU), pltpu.VMEM((tk, rows), _MXU),
                        pltpu.VMEM((1, rows), F32), pltpu.VMEM((1, rows), F32)],
    )(qt8, k2, vta)


def _attn_bwd(qpad, dopad, qt8, ot, dot_, lse, k2, v2, k2t, tq, tk, ksplit):
    seq = k2.shape[0]
    sh = seq // ksplit
    nck = sh // tk
    rows = A_G * tq
    assert nck % 2 == 0, nck

    def body(q_ref, do_ref, qt_ref, ot_ref, dot_ref, lse_ref, k_ref, v_ref, kt_ref,
             dk_ref, dv_ref, dq_ref, dq_sc, qtp_sc, dotp_sc, s_a, s_b, dp_a, dp_b, p_a, p_b, ds_a, ds_b):
        g = pl.program_id(1)
        hrows = pl.ds(pl.multiple_of(g * A_HD, A_HD), A_HD)

        @pl.when((g == 0) & (pl.program_id(2) == 0))
        def _():
            dk_ref[...] = jnp.zeros_like(dk_ref)
            dv_ref[...] = jnp.zeros_like(dv_ref)

        lse_row = jnp.concatenate([lse_ref[a] for a in range(A_G)], axis=1)
        dd = jnp.concatenate([jnp.sum(ot_ref[a] * dot_ref[a], axis=0, keepdims=True)
                              for a in range(A_G)], axis=1)
        qtp_sc[...] = jnp.zeros_like(qtp_sc)
        dotp_sc[...] = jnp.zeros_like(dotp_sc)
        qtp_sc[hrows, :] = jnp.concatenate([qt_ref[a] for a in range(A_G)], axis=1)
        dotp_sc[hrows, :] = _bf(jnp.concatenate([dot_ref[a] for a in range(A_G)], axis=1))
        dq_sc[...] = jnp.zeros_like(dq_sc)
        p_b[...] = jnp.zeros_like(p_b)
        ds_b[...] = jnp.zeros_like(ds_b)

        def products(c, s_ref, dp_ref):
            sl = pl.ds(pl.multiple_of(c * tk, tk), tk)
            s_ref[...] = _mm(k_ref[sl, :], qtp_sc[...])
            dp_ref[...] = _mm(v_ref[sl, :], dotp_sc[...])

        def accumulate(c, p_ref, ds_ref):
            sl = pl.ds(pl.multiple_of(c * tk, tk), tk)
            dv_ref[sl, :] += _mm(p_ref[...], do_ref[...].reshape(rows, LANES))
            dk_ref[sl, :] += _mm(ds_ref[...], q_ref[...].reshape(rows, LANES))
            dq_sc[...] += _mm(kt_ref[c, hrows, :], ds_ref[...])

        def stage(c, s_cur, dp_cur, s_nxt, dp_nxt, p_cur, ds_cur, p_prv, ds_prv):
            products(jnp.minimum(c + 1, nck - 1), s_nxt, dp_nxt)
            accumulate(jnp.maximum(c - 1, 0), p_prv, ds_prv)
            p = jnp.exp(s_cur[...] - lse_row)
            p_cur[...] = _bf(p)
            ds_cur[...] = _bf(p * (dp_cur[...] - dd))

        products(0, s_a, dp_a)

        def pair(j, carry):
            stage(2 * j, s_a, dp_a, s_b, dp_b, p_a, ds_a, p_b, ds_b)
            stage(2 * j + 1, s_b, dp_b, s_a, dp_a, p_b, ds_b, p_a, ds_a)
            return carry

        lax.fori_loop(0, nck // 2, pair, 0)
        accumulate(nck - 1, p_b, ds_b)
        for a in range(A_G):
            dq_ref[0, a] = dq_sc[:, a * tq:(a + 1) * tq]

    qspec = pl.BlockSpec((A_G, tq, LANES), lambda s, g, i: (g, i, 0))
    tspec = pl.BlockSpec((A_G, A_HD, tq), lambda s, g, i: (g, 0, i))
    kspec = pl.BlockSpec((sh, LANES), lambda s, g, i: (s, 0))
    big = lambda dt: pltpu.VMEM((tk, rows), dt)
    return pl.pallas_call(
        body, name="attn_bwd", grid=(ksplit, A_KV, seq // tq),
        in_specs=[qspec, qspec, tspec, tspec, tspec, pl.BlockSpec((A_G, 1, tq), lambda s, g, i: (g, 0, i)),
                  kspec, kspec, pl.BlockSpec((nck, LANES, tk), lambda s, g, i: (s, 0, 0))],
        out_specs=(kspec, kspec, pl.BlockSpec((1, A_G, A_HD, tq), lambda s, g, i: (s, g, 0, i))),
        out_shape=(jax.ShapeDtypeStruct((seq, LANES), F32), jax.ShapeDtypeStruct((seq, LANES), F32),
                   jax.ShapeDtypeStruct((ksplit, A_H, A_HD, seq), F32)),
        scratch_shapes=[pltpu.VMEM((A_HD, rows), F32), pltpu.VMEM((LANES, rows), _MXU), pltpu.VMEM((LANES, rows), _MXU),
                        big(F32), big(F32), big(F32), big(F32), big(_MXU), big(_MXU), big(_MXU), big(_MXU)],
    )(qpad, dopad, qt8, ot, dot_, lse, k2, v2, k2t)


def _ret_tables(zb):
    c = CHUNK

    def body(z_ref, m_ref, mw_ref, qd_ref, qdw_ref, kd_ref, kdw_ref, g_ref, gw_ref):
        fwd = pl.program_id(0) < R_H
        z = z_ref[0]
        lam = jnp.minimum(z, 0.0) - jnp.log(1.0 + jnp.exp(-jnp.abs(z)))
        i = lax.broadcasted_iota(jnp.int32, (c, c), 0).astype(F32)
        j = lax.broadcasted_iota(jnp.int32, (c, c), 1).astype(F32)
        diff = jnp.where(fwd, i - j, j - i)
        keep = diff >= jnp.where(fwd, 0.0, 1.0)
        dist = jnp.maximum(diff, 0.0)
        m = jnp.where(keep, jnp.exp(lam * dist), 0.0)
        m_ref[0] = m
        mw_ref[0] = m * dist
        fq = jnp.where(fwd, i + 1.0, c - i)
        qd = jnp.exp(lam * fq)
        qd_ref[0] = qd
        qdw_ref[0] = qd * fq
        fk = jnp.where(fwd, c - 1.0 - i, i)
        kd = jnp.exp(lam * fk)
        kd_ref[0] = kd
        kdw_ref[0] = kd * fk
        gdec = jnp.exp(lam * c)
        g_ref[0] = gdec
        gw_ref[0] = gdec * c

    big = pl.BlockSpec((1, c, c), lambda t: (t, 0, 0))
    vec = pl.BlockSpec((1, 1, LANES), lambda t: (t, 0, 0))
    bshape = jax.ShapeDtypeStruct((2 * R_H, c, c), F32)
    vshape = jax.ShapeDtypeStruct((2 * R_H, 1, LANES), F32)
    return pl.pallas_call(
        body, name="ret_tables", grid=(2 * R_H,), in_specs=[vec],
        out_specs=(big, big, big, big, big, big, vec, vec),
        out_shape=(bshape,) * 6 + (vshape, vshape),
    )(zb)


def _ret_fwd(rq, rk, pr, m, qd, kd, gdec, cb):
    seq = rq.shape[0]
    c = CHUNK
    ns = seq // (cb * c)

    def body(q_ref, k_ref, v_ref, m_ref, qd_ref, kd_ref, g_ref, y_ref, pst_ref, p_sc):
        d = pl.program_id(1)

        @pl.when(pl.program_id(2) == 0)
        def _():
            p_sc[...] = jnp.zeros((R_HD, R_HD), F32)

        mm, qdd, kdd, gg = m_ref[0], qd_ref[0], kd_ref[0], g_ref[0]

        def chunk(j, carry):
            cc = jnp.where(d == 0, j, cb - 1 - j)
            sl = pl.ds(pl.multiple_of(cc * c, c), c)
            q, k, v = q_ref[sl, :], k_ref[sl, :], v_ref[sl, :]
            p = p_sc[...]
            pst_ref[0, cc] = p
            a = _mm_nt(q, k) * mm
            y_ref[0, sl, :] = _mm(a, v) + _mm(q * qdd, p)
            p_sc[...] = p * gg + _mm_tn(k * kdd, v)
            return carry

        lax.fori_loop(0, cb, chunk, 0)

    def step(d, n):
        return d * (ns - 1 - n) + (1 - d) * n

    blk = lambda off: pl.BlockSpec((cb * c, R_HD), lambda h, d, n: (step(d, n), off + h))
    big = pl.BlockSpec((1, c, c), lambda h, d, n: (d * R_H + h, 0, 0))
    vec = pl.BlockSpec((1, 1, LANES), lambda h, d, n: (d * R_H + h, 0, 0))
    return pl.pallas_call(
        body, name="ret_fwd", grid=(R_H, 2, ns),
        in_specs=[blk(0), blk(0), blk(2 * R_H), big, big, big, vec],
        out_specs=(pl.BlockSpec((1, cb * c, R_HD), lambda h, d, n: (d, step(d, n), h)),
                   pl.BlockSpec((1, cb, R_HD, R_HD), lambda h, d, n: (d * R_H + h, step(d, n), 0, 0))),
        out_shape=(jax.ShapeDtypeStruct((2, seq, R_W), F32),
                   jax.ShapeDtypeStruct((2 * R_H, seq // c, R_HD, R_HD), F32)),
        scratch_shapes=[pltpu.VMEM((R_HD, R_HD), F32)],
    )(rq, rk, pr, m, qd, kd, gdec)


def _ret_bwd(rq, rk, pr, dry, pst, m, mw, qd, qdw, kd, kdw, gdec, gw, cb):
    seq = rq.shape[0]
    c = CHUNK
    ns = seq // (cb * c)

    def body(q_ref, k_ref, v_ref, dy_ref, pst_ref, m_ref, mw_ref, qd_ref, qdw_ref, kd_ref, kdw_ref,
             g_ref, gw_ref, dq_ref, dk_ref, dv_ref, dlam_ref, r_sc, acc_sc, e_sc, g_sc):
        d = pl.program_id(1)
        n = pl.program_id(2)

        @pl.when(n == 0)
        def _():
            r_sc[...] = jnp.zeros_like(r_sc)
            acc_sc[...] = jnp.zeros_like(acc_sc)
            e_sc[...] = jnp.zeros_like(e_sc)
            g_sc[...] = jnp.zeros_like(g_sc)

        mm, mww, qdd, qdww, kdd, kdww, gg = (m_ref[0], mw_ref[0], qd_ref[0], qdw_ref[0],
                                             kd_ref[0], kdw_ref[0], g_ref[0])

        def chunk(j, carry):
            cc = jnp.where(d == 0, cb - 1 - j, j)
            sl = pl.ds(pl.multiple_of(cc * c, c), c)
            q, k, v, dy = q_ref[sl, :], k_ref[sl, :], v_ref[sl, :], dy_ref[sl, :]
            p = pst_ref[0, cc]
            r = r_sc[...]
            qk = _mm_nt(q, k)
            a = qk * mm
            ds = _mm_nt(dy, v)
            da = ds * mm
            dyp = _mm_nt(dy, p)
            vr = _mm_nt(v, r)
            dv_ref[0, sl, :] = _mm_tn(a, dy) + _mm(k * kdd, r)
            dq_ref[0, sl, :] = _mm(da, k) + dyp * qdd
            dk_ref[0, sl, :] = _mm_tn(da, q) + vr * kdd
            acc_sc[...] += dyp * q * qdww + vr * k * kdww
            e_sc[...] += ds * qk * mww
            g_sc[...] += r * p
            r_sc[...] = r * gg + _mm_tn(q * qdd, dy)
            return carry

        lax.fori_loop(0, cb, chunk, 0)

        @pl.when(n == ns - 1)
        def _():
            tot = (jnp.sum(jnp.sum(acc_sc[...] + e_sc[...] + g_sc[...] * gw_ref[0], axis=0, keepdims=True),
                           axis=1, keepdims=True))
            dlam_ref[0] = jnp.broadcast_to(tot, (1, LANES))

    def step(d, n):
        return d * n + (1 - d) * (ns - 1 - n)

    blk = lambda off: pl.BlockSpec((cb * c, R_HD), lambda h, d, n: (step(d, n), off + h))
    big = pl.BlockSpec((1, c, c), lambda h, d, n: (d * R_H + h, 0, 0))
    vec = pl.BlockSpec((1, 1, LANES), lambda h, d, n: (d * R_H + h, 0, 0))
    out = pl.BlockSpec((1, cb * c, R_HD), lambda h, d, n: (d, step(d, n), h))
    oshape = jax.ShapeDtypeStruct((2, seq, R_W), F32)
    return pl.pallas_call(
        body, name="ret_bwd", grid=(R_H, 2, ns),
        in_specs=[blk(0), blk(0), blk(2 * R_H), blk(0),
                  pl.BlockSpec((1, cb, R_HD, R_HD), lambda h, d, n: (d * R_H + h, step(d, n), 0, 0)),
                  big, big, big, big, big, big, vec, vec],
        out_specs=(out, out, out, vec),
        out_shape=(oshape, oshape, oshape, jax.ShapeDtypeStruct((2 * R_H, 1, LANES), F32)),
        scratch_shapes=[pltpu.VMEM((R_HD, R_HD), F32), pltpu.VMEM((c, R_HD), F32),
                        pltpu.VMEM((c, c), F32), pltpu.VMEM((R_HD, R_HD), F32)],
    )(rq, rk, pr, dry, pst, m, mw, qd, qdw, kd, kdw, gdec, gw)


def _group_norm(ry):
    yn, rs = [], []
    for h in range(R_H):
        s = ry[:, h * R_HD:(h + 1) * R_HD]
        mu = jnp.mean(s, axis=-1, keepdims=True)
        cen = s - mu
        r = lax.rsqrt(jnp.mean(cen * cen, axis=-1, keepdims=True) + GN_EPS)
        yn.append(cen * r)
        rs.append(r)
    return yn, rs


def _merge_fwd(x, o, y2, pr, pg, gain_r, wao, wro, wout):
    seq = x.shape[0]
    tm = _tile(seq, 256)

    def body(x_ref, o_ref, yf_ref, yb_ref, rg_ref, ga_ref, gr_ref, gn_ref, wao_ref, wro_ref, wout_ref,
             x1_ref, mg_ref, ri_ref):
        yn, _ = _group_norm(yf_ref[0] + yb_ref[0])
        rg = rg_ref[...]
        ret_in = jnp.concatenate(yn, axis=1) * gn_ref[...] * (rg * _sigmoid(rg))
        ri_ref[...] = _bf(ret_in)
        attn_out = _mm(o_ref[...], wao_ref[...])
        ret_out = _mm(ret_in, wro_ref[...])
        merged = _sigmoid(ga_ref[...]) * attn_out + _sigmoid(gr_ref[...]) * ret_out
        mg_ref[...] = _bf(merged)
        x1_ref[...] = x_ref[...] + _mm(merged, wout_ref[...])

    row = lambda w_, j=0: pl.BlockSpec((tm, w_), lambda i: (i, j))
    ydir = lambda d: pl.BlockSpec((1, tm, R_W), lambda i: (d, i, 0))
    return pl.pallas_call(
        body, name="merge_fwd", grid=(seq // tm,),
        in_specs=[row(D), row(AQ_W), ydir(0), ydir(1), row(R_W, 3), row(D, 0), row(D, 1),
                  _full((1, R_W)), _full((AQ_W, D)), _full((R_W, D)), _full((D, D))],
        out_specs=(row(D), row(D), row(R_W)),
        out_shape=(jax.ShapeDtypeStruct((seq, D), F32), jax.ShapeDtypeStruct((seq, D), _MXU),
                   jax.ShapeDtypeStruct((seq, R_W), _MXU)),
    )(x, o, y2, y2, pr, pg, pg, gain_r, wao, wro, wout)


def _mlp_fwd(x1, gain, wup, wdown):
    seq = x1.shape[0]
    tm = _tile(seq, 512)
    fc = 512
    nfc = FF // fc

    def body(x_ref, g_ref, wu_ref, wd_ref, x2_ref, hm_sc, acc_sc):
        c = pl.program_id(1)

        @pl.when(c == 0)
        def _():
            n, _ = _rms(x_ref[...])
            hm_sc[...] = _bf(n * g_ref[...])
            acc_sc[...] = jnp.zeros_like(acc_sc)

        u = jnp.maximum(_mm(hm_sc[...], wu_ref[...]), 0.0)
        acc_sc[...] += _mm(u * u, wd_ref[...])

        @pl.when(c == nfc - 1)
        def _():
            x2_ref[...] = x_ref[...] + acc_sc[...]

    return pl.pallas_call(
        body, name="mlp_fwd", grid=(seq // tm, nfc),
        in_specs=[pl.BlockSpec((tm, D), lambda i, c: (i, 0)), pl.BlockSpec((1, D), lambda i, c: (0, 0)),
                  pl.BlockSpec((D, fc), lambda i, c: (0, c)), pl.BlockSpec((fc, D), lambda i, c: (c, 0))],
        out_specs=pl.BlockSpec((tm, D), lambda i, c: (i, 0)),
        out_shape=jax.ShapeDtypeStruct((seq, D), F32),
        scratch_shapes=[pltpu.VMEM((tm, D), _MXU), pltpu.VMEM((tm, D), F32)],
    )(x1, gain, wup, wdown)


def _ple_loss(x2, p, tgt, g_ple, g_fin, wpg, wpgt, wple):
    seq = x2.shape[0]
    tm = _tile(seq, 256)

    def body(x2_ref, p_ref, t_ref, gp_ref, gf_ref, wpg_ref, wpgt_ref, wple_ref,
             dx2_ref, de_ref, dz_ref, hp_ref, loss_ref, dgf_ref, dgp_ref):
        @pl.when(pl.program_id(0) == 0)
        def _():
            loss_ref[...] = jnp.zeros_like(loss_ref)
            dgf_ref[...] = jnp.zeros_like(dgf_ref)
            dgp_ref[...] = jnp.zeros_like(dgp_ref)

        x2 = x2_ref[...]
        gp, gf = gp_ref[...], gf_ref[...]
        n2, r2 = _rms(x2)
        hp = _bf(n2 * gp)
        hp_ref[...] = hp
        gate = _sigmoid(_mm(hp, wpg_ref[...]))
        e = _mm(p_ref[...], wple_ref[...])
        x3 = x2 + gate * e
        n3, r3 = _rms(x3)
        diff = n3 * gf - t_ref[...]
        row_loss = jnp.mean(diff * diff, axis=-1, keepdims=True)
        loss_ref[...] += 0.5 * jnp.sum(row_loss, axis=0, keepdims=True)
        dy = diff * (1.0 / D)
        dgf_ref[...] += jnp.sum(dy * n3, axis=0, keepdims=True)
        dx3 = _rms_bwd(n3, r3, gf, dy)
        de_ref[...] = _bf(dx3 * gate)
        dz = dx3 * e * gate * (1.0 - gate)
        dz_ref[...] = _bf(dz)
        dhp = _mm(dz, wpgt_ref[...])
        dgp_ref[...] += jnp.sum(dhp * n2, axis=0, keepdims=True)
        dx2_ref[...] = dx3 + _rms_bwd(n2, r2, gp, dhp)

    row = lambda w_: pl.BlockSpec((tm, w_), lambda i: (i, 0))
    act = lambda dt: jax.ShapeDtypeStruct((seq, D), dt)
    return pl.pallas_call(
        body, name="ple_loss", grid=(seq // tm,),
        in_specs=[row(D), row(PLE), row(D), _full((1, D)), _full((1, D)),
                  _full((D, D)), _full((D, D)), _full((PLE, D))],
        out_specs=(row(D), row(D), row(D), row(D), _full((1, LANES)), _full((1, D)), _full((1, D))),
        out_shape=(act(F32), act(_MXU), act(_MXU), act(_MXU), jax.ShapeDtypeStruct((1, LANES), F32),
                   jax.ShapeDtypeStruct((1, D), F32), jax.ShapeDtypeStruct((1, D), F32)),
    )(x2, p, tgt, g_ple, g_fin, wpg, wpgt, wple)


def _mlp_bwd(x1, dx2, gain, wup, wdownt, wupt):
    seq = x1.shape[0]
    tm = _tile(seq, 512)
    fc = 512
    nfc = FF // fc

    def body(x_ref, dx2_ref, g_ref, wu_ref, wdt_ref, wut_ref,
             dx1_ref, a_ref, du_ref, hm_ref, dg_ref, dhm_sc):
        i = pl.program_id(0)
        c = pl.program_id(1)

        @pl.when((i == 0) & (c == 0))
        def _():
            dg_ref[...] = jnp.zeros_like(dg_ref)

        @pl.when(c == 0)
        def _():
            n, _ = _rms(x_ref[...])
            hm_ref[...] = _bf(n * g_ref[...])
            dhm_sc[...] = jnp.zeros_like(dhm_sc)

        u = jnp.maximum(_mm(hm_ref[...], wu_ref[...]), 0.0)
        a_ref[...] = _bf(u * u)
        du = _mm(dx2_ref[...], wdt_ref[...]) * (2.0 * u)
        du_ref[...] = _bf(du)
        dhm_sc[...] += _mm(du, wut_ref[...])

        @pl.when(c == nfc - 1)
        def _():
            n, r = _rms(x_ref[...])
            dhm = dhm_sc[...]
            dg_ref[...] += jnp.sum(dhm * n, axis=0, keepdims=True)
            dx1_ref[...] = dx2_ref[...] + _rms_bwd(n, r, g_ref[...], dhm)

    rowd = pl.BlockSpec((tm, D), lambda i, c: (i, 0))
    rowf = pl.BlockSpec((tm, fc), lambda i, c: (i, c))
    return pl.pallas_call(
        body, name="mlp_bwd", grid=(seq // tm, nfc),
        in_specs=[rowd, rowd, pl.BlockSpec((1, D), lambda i, c: (0, 0)),
                  pl.BlockSpec((D, fc), lambda i, c: (0, c)), pl.BlockSpec((D, fc), lambda i, c: (0, c)),
                  pl.BlockSpec((fc, D), lambda i, c: (c, 0))],
        out_specs=(rowd, rowf, rowf, rowd, pl.BlockSpec((1, D), lambda i, c: (0, 0))),
        out_shape=(jax.ShapeDtypeStruct((seq, D), F32), jax.ShapeDtypeStruct((seq, FF), _MXU),
                   jax.ShapeDtypeStruct((seq, FF), _MXU), jax.ShapeDtypeStruct((seq, D), _MXU),
                   jax.ShapeDtypeStruct((1, D), F32)),
        scratch_shapes=[pltpu.VMEM((tm, D), F32)],
    )(x1, dx2, gain, wup, wdownt, wupt)


def _merge_bwd(dx1, o, y2, pr, pg, gain_r, wao, wro, woutt, waot, wrot):
    seq = dx1.shape[0]
    tm = _tile(seq, 256)

    def body(dx1_ref, o_ref, yf_ref, yb_ref, rg_ref, ga_ref, gr_ref, gn_ref, wao_ref, wro_ref,
             woutt_ref, waot_ref, wrot_ref,
             dpg_ref, dao_ref, dro_ref, do_ref, dry_ref, drg_ref, dgn_ref):
        @pl.when(pl.program_id(0) == 0)
        def _():
            dgn_ref[...] = jnp.zeros_like(dgn_ref)

        yn_l, rs_l = _group_norm(yf_ref[0] + yb_ref[0])
        yn = jnp.concatenate(yn_l, axis=1)
        rg = rg_ref[...]
        gn = gn_ref[...]
        sg = _sigmoid(rg)
        sil = rg * sg
        ret_in = yn * gn * sil
        attn_out = _mm(o_ref[...], wao_ref[...])
        ret_out = _mm(ret_in, wro_ref[...])
        sa = _sigmoid(ga_ref[...])
        sr = _sigmoid(gr_ref[...])
        dm = _mm(dx1_ref[...], woutt_ref[...])
        dpg_ref[:, 0:D] = _bf(dm * attn_out * sa * (1.0 - sa))
        dpg_ref[:, D:2 * D] = _bf(dm * ret_out * sr * (1.0 - sr))
        dao = _bf(dm * sa)
        dro = _bf(dm * sr)
        dao_ref[...] = dao
        dro_ref[...] = dro
        do_ref[...] = _mm(dao, waot_ref[...])
        dri = _mm(dro, wrot_ref[...])
        dgn_ref[...] += jnp.sum(dri * yn * sil, axis=0, keepdims=True)
        drg_ref[...] = _bf(dri * yn * gn * (sg * (1.0 + rg * (1.0 - sg))))
        dyn = dri * gn * sil
        dry = []
        for h in range(R_H):
            dh = dyn[:, h * R_HD:(h + 1) * R_HD]
            dry.append(rs_l[h] * (dh - jnp.mean(dh, axis=-1, keepdims=True)
                                  - yn_l[h] * jnp.mean(dh * yn_l[h], axis=-1, keepdims=True)))
        dry_ref[...] = jnp.concatenate(dry, axis=1)

    row = lambda w_, j=0: pl.BlockSpec((tm, w_), lambda i: (i, j))
    ydir = lambda d: pl.BlockSpec((1, tm, R_W), lambda i: (d, i, 0))
    return pl.pallas_call(
        body, name="merge_bwd", grid=(seq // tm,),
        in_specs=[row(D), row(AQ_W), ydir(0), ydir(1), row(R_W, 3), row(D, 0), row(D, 1),
                  _full((1, R_W)), _full((AQ_W, D)), _full((R_W, D)), _full((D, D)),
                  _full((D, AQ_W)), _full((D, R_W))],
        out_specs=(row(PG_W), row(D), row(D), row(AQ_W), row(R_W), row(R_W), _full((1, R_W))),
        out_shape=(jax.ShapeDtypeStruct((seq, PG_W), _MXU), jax.ShapeDtypeStruct((seq, D), _MXU),
                   jax.ShapeDtypeStruct((seq, D), _MXU), jax.ShapeDtypeStruct((seq, AQ_W), F32),
                   jax.ShapeDtypeStruct((seq, R_W), F32), jax.ShapeDtypeStruct((seq, R_W), _MXU),
                   jax.ShapeDtypeStruct((1, R_W), F32)),
    )(dx1, o, y2, y2, pr, pg, pg, gain_r, wao, wro, woutt, waot, wrot)


def _qk_prep_bwd(pa, dqh, dk2, dv2, rdq, rdk, rdv, drg, gq, gk, seg, ca, sa, cr, sr):
    seq = pa.shape[0]
    tm = _tile(seq, 256)

    def body(pa_ref, dqh_ref, dk2_ref, dv2_ref, rdqf_ref, rdqb_ref, rdkf_ref, rdkb_ref, rdvf_ref, rdvb_ref,
             drg_ref, gq_ref, gk_ref, seg_ref, ca_ref, sa_ref, cr_ref, sr_ref,
             dpa_ref, dpr_ref, dgq_ref, dgk_ref):
        @pl.when(pl.program_id(0) == 0)
        def _():
            dgq_ref[...] = jnp.zeros_like(dgq_ref)
            dgk_ref[...] = jnp.zeros_like(dgk_ref)

        ca_, sa_ = ca_ref[...], sa_ref[...]

        def norm_bwd(raw, gain, dy, segm, dg_ref):
            msq = jnp.dot(raw * raw, segm, precision=HIGHEST, preferred_element_type=F32)
            r = lax.rsqrt(msq + EPS)
            n = raw * r
            dg_ref[...] += jnp.sum(dy * n, axis=0, keepdims=True)
            dn = dy * gain
            return r * (dn - n * jnp.dot(dn * n, segm, precision=HIGHEST, preferred_element_type=F32))

        dqn = _rope(dqh_ref[...] * (A_HD ** -0.5), _cat(ca_, 4), -_cat(sa_, 4), A_HD // 2)
        dpa_ref[:, 0:AQ_W] = _bf(norm_bwd(pa_ref[:, 0:AQ_W], gq_ref[...], dqn, seg_ref[...], dgq_ref))
        dkn = _rope(dk2_ref[...], ca_, -sa_, A_HD // 2)
        dpa_ref[:, AQ_W:AQ_W + AKV_W] = _bf(norm_bwd(pa_ref[:, AQ_W:AQ_W + AKV_W], gk_ref[...], dkn,
                                                     seg_ref[0:AKV_W, 0:AKV_W], dgk_ref))
        dpa_ref[:, AQ_W + AKV_W:PA_W] = _bf(dv2_ref[...])
        cr_, sr_ = _cat(cr_ref[...], 4), -_cat(sr_ref[...], 4)
        dpr_ref[:, 0:R_W] = _bf(_rope((rdqf_ref[0] + rdqb_ref[0]) * (R_HD ** -0.5), cr_, sr_, R_HD // 2))
        dpr_ref[:, R_W:2 * R_W] = _bf(_rope(rdkf_ref[0] + rdkb_ref[0], cr_, sr_, R_HD // 2))
        dpr_ref[:, 2 * R_W:3 * R_W] = _bf(rdvf_ref[0] + rdvb_ref[0])
        dpr_ref[:, 3 * R_W:4 * R_W] = drg_ref[...]

    row = lambda w_: pl.BlockSpec((tm, w_), lambda i: (i, 0))
    ydir = lambda d: pl.BlockSpec((1, tm, R_W), lambda i: (d, i, 0))
    return pl.pallas_call(
        body, name="qk_prep_bwd", grid=(seq // tm,),
        in_specs=[row(PA_W), row(AQ_W), row(AKV_W), row(AKV_W), ydir(0), ydir(1), ydir(0), ydir(1),
                  ydir(0), ydir(1), row(R_W), _full((1, AQ_W)), _full((1, AKV_W)), _full((AQ_W, AQ_W)),
                  row(LANES), row(LANES), row(LANES), row(LANES)],
        out_specs=(row(PA_W), row(PR_W), _full((1, AQ_W)), _full((1, AKV_W))),
        out_shape=(jax.ShapeDtypeStruct((seq, PA_W), _MXU), jax.ShapeDtypeStruct((seq, PR_W), _MXU),
                   jax.ShapeDtypeStruct((1, AQ_W), F32), jax.ShapeDtypeStruct((1, AKV_W), F32)),
    )(pa, dqh, dk2, dv2, rdq, rdq, rdk, rdk, rdv, rdv, drg, gq, gk, seg, ca, sa, cr, sr)


def _in_proj_bwd(x, dx1, gain, dpa, dpr, dpg, wint):
    seq = x.shape[0]
    tm = _tile(seq, 256)

    def body(x_ref, dx1_ref, g_ref, dpa_ref, dpr_ref, dpg_ref, wt_ref, dx_ref, dg_ref):
        @pl.when(pl.program_id(0) == 0)
        def _():
            dg_ref[...] = jnp.zeros_like(dg_ref)

        dh = (_mm(dpa_ref[...], wt_ref[0:PA_W, :]) + _mm(dpr_ref[...], wt_ref[PA_W:PA_W + PR_W, :])
              + _mm(dpg_ref[...], wt_ref[PA_W + PR_W:IN_W, :]))
        n, r = _rms(x_ref[...])
        dg_ref[...] += jnp.sum(dh * n, axis=0, keepdims=True)
        dx_ref[...] = dx1_ref[...] + _rms_bwd(n, r, g_ref[...], dh)

    row = lambda w_: pl.BlockSpec((tm, w_), lambda i: (i, 0))
    return pl.pallas_call(
        body, name="in_proj_bwd", grid=(seq // tm,),
        in_specs=[row(D), row(D), _full((1, D)), row(PA_W), row(PR_W), row(PG_W), _full((IN_W, D))],
        out_specs=(row(D), _full((1, D))),
        out_shape=(jax.ShapeDtypeStruct((seq, D), F32), jax.ShapeDtypeStruct((1, D), F32)),
    )(x, dx1, gain, dpa, dpr, dpg, wint)


def _wgrad(a, b, name):
    seq, m = a.shape
    n = b.shape[1]
    tm, tn, ts = _tile(m, 512), _tile(n, 1024), _tile(seq, 512)
    ns = seq // ts

    def body(a_ref, b_ref, o_ref):
        @pl.when(pl.program_id(2) == 0)
        def _():
            o_ref[...] = jnp.zeros_like(o_ref)

        o_ref[...] += _mm_tn(a_ref[...], b_ref[...])

    return pl.pallas_call(
        body, name=name, grid=(m // tm, n // tn, ns),
        in_specs=[pl.BlockSpec((ts, tm), lambda i, j, s: (s, i)), pl.BlockSpec((ts, tn), lambda i, j, s: (s, j))],
        out_specs=pl.BlockSpec((tm, tn), lambda i, j, s: (i, j)),
        out_shape=jax.ShapeDtypeStruct((m, n), F32),
    )(a, b)


def _adamw_math(w, g, m, v):
    m = B1 * m + (1.0 - B1) * g
    v = B2 * v + (1.0 - B2) * (g * g)
    m_hat = m / (1.0 - B1 ** STEP)
    v_hat = v / (1.0 - B2 ** STEP)
    delta = -LR * (m_hat / (jnp.sqrt(v_hat) + ADAM_EPS) + WD * w)
    return delta, m, v


def _adamw_big(land, own, w, m, v, name):
    rws, cols = w.shape
    tr = next(t for t in range(min(rws, 288), 0, -8) if rws % t == 0)

    def body(l_ref, o_ref, w_ref, m_ref, v_ref, g_ref, d_ref, nm_ref, nv_ref):
        x, y, c = _mesh_pos()
        me = 4 * x + 2 * y + c
        g = o_ref[...]
        for j in range(N_DEV):
            g = g + jnp.where(me == j, 0.0, l_ref[j].astype(F32))
        g_ref[...] = g
        d_ref[...], nm_ref[...], nv_ref[...] = _adamw_math(w_ref[...], g, m_ref[...], v_ref[...])

    row = pl.BlockSpec((tr, cols), lambda i: (i, 0))
    shp = jax.ShapeDtypeStruct((rws, cols), F32)
    return pl.pallas_call(
        body, name=name, grid=(rws // tr,),
        in_specs=[pl.BlockSpec((N_DEV, tr, cols), lambda i: (0, i, 0)), row, row, row, row],
        out_specs=(row, row, row, row), out_shape=(shp, shp, shp, shp),
    )(land, own, w, m, v)


def _adamw_small(sland, w, m, v):
    def body(l_ref, w_ref, m_ref, v_ref, g_ref, d_ref, nm_ref, nv_ref, loss_ref):
        s = l_ref[0]
        for j in range(1, N_DEV):
            s = s + l_ref[j]
        w = w_ref[...]
        gq = s[8:9]
        for h in range(1, A_H):
            gq = gq + s[8 + h:9 + h]
        gk = s[16:17] + s[17:18]
        gdec = s[5:6] * _sigmoid(-w[5:6])
        g = jnp.concatenate([s[0:5], gdec, gq, gk], axis=0)
        g_ref[...] = g
        d_ref[...], nm_ref[...], nv_ref[...] = _adamw_math(w, g, m_ref[...], v_ref[...])
        loss_ref[...] = s[6:7, 0:LANES]

    shp = jax.ShapeDtypeStruct((8, PACK_COLS), F32)
    return pl.pallas_call(
        body, name="adamw_small",
        out_shape=(shp, shp, shp, shp, jax.ShapeDtypeStruct((1, LANES), F32)),
    )(sland, w, m, v)


_BIG = (("w_attn_o", AQ_W, D, 1), ("w_ret_o", R_W, D, 1), ("w_out", D, D, 0),
        ("w_up", D, FF, 1), ("w_down", FF, D, 0), ("w_ple_gate", D, D, 0), ("w_ple", PLE, D, 1))
IN_SHARD = IN_W // N_DEV
_SMALL = ("mix_norm", "mlp_norm", "ple_norm", "final_norm", "ret_norm_gain", "ret_decay_logit",
          "attn_q_norm", "attn_k_norm")


def _shard_shape(rows, cols, axis):
    return (rows // N_DEV, cols) if axis == 0 else (rows, cols // N_DEV)


def _pack_shards(shards):
    flat = jnp.concatenate([s.reshape(-1) for s in shards])
    return flat.reshape(-1, PACK_COLS)


def _unpack_gathered(gathered):
    flat = gathered.reshape(N_DEV, -1)
    out, off = {}, 0
    for name, rows, cols, axis in _BIG:
        sr, sc = _shard_shape(rows, cols, axis)
        blk = flat[:, off:off + sr * sc].reshape(N_DEV, sr, sc)
        off += sr * sc
        out[name] = blk.reshape(rows, cols) if axis == 0 else blk.transpose(1, 0, 2).reshape(rows, cols)
    return out


def _pack_full_grads(grads):
    parts = []
    for name, rows, cols, axis in _BIG:
        sr, sc = _shard_shape(rows, cols, axis)
        g = grads[name]
        blk = g.reshape(N_DEV, sr, sc) if axis == 0 else g.reshape(rows, N_DEV, sc).transpose(1, 0, 2)
        parts.append(blk.reshape(N_DEV, -1))
    flat = jnp.concatenate(parts, axis=1)
    return flat.reshape(N_DEV, -1, PACK_COLS)


def _unpack_shard(packed):
    flat = packed.reshape(-1)
    out, off = {}, 0
    for name, rows, cols, axis in _BIG:
        sr, sc = _shard_shape(rows, cols, axis)
        out[name] = flat[off:off + sr * sc].reshape(1, sr, sc)
        off += sr * sc
    return out


def _pack_small(vals):
    rows = [jnp.pad(vals[n].reshape(-1), (0, PACK_COLS - vals[n].size)) for n in _SMALL]
    return jnp.stack(rows)


def _unpack_small(packed, like):
    return {n: packed[i, :like[n].size].reshape(like[n].shape) for i, n in enumerate(_SMALL)}


def _row(v):
    return jnp.pad(v.reshape(-1), (0, PACK_COLS - v.size))


def kernel(x, p, mix_norm, w_in, attn_q_norm, attn_k_norm, ret_decay_logit, ret_norm_gain, w_attn_o, w_ret_o, w_out, mlp_norm, w_up, w_down, ple_norm, w_ple_gate, w_ple, final_norm, loss_target, m_mix_norm, m_w_in, m_attn_q_norm, m_attn_k_norm, m_ret_decay_logit, m_ret_norm_gain, m_w_attn_o, m_w_ret_o, m_w_out, m_mlp_norm, m_w_up, m_w_down, m_ple_norm, m_w_ple_gate, m_w_ple, m_final_norm, v_mix_norm, v_w_in, v_attn_q_norm, v_attn_k_norm, v_ret_decay_logit, v_ret_norm_gain, v_w_attn_o, v_w_ret_o, v_w_out, v_mlp_norm, v_w_up, v_w_down, v_ple_norm, v_w_ple_gate, v_w_ple, v_final_norm):
    args = dict(locals())
    seq = x.shape[1]
    xs = x[0]
    ps = p[0, 0]
    tgt = loss_target[0]

    big_names = [b[0] for b in _BIG]
    wshard = _pack_shards([args[n] for n in big_names])
    win_g, rest_g = _all_gather([w_in[0].astype(_MXU), wshard.astype(_MXU)])
    win = win_g.transpose(1, 0, 2).reshape(D, IN_W)
    wfull = _unpack_gathered(rest_g)
    wao, wro, wout = wfull["w_attn_o"], wfull["w_ret_o"], wfull["w_out"]
    wup, wdown, wpg, wple = wfull["w_up"], wfull["w_down"], wfull["w_ple_gate"], wfull["w_ple"]

    g_mix, g_mlp, g_ple = mix_norm, mlp_norm, ple_norm
    g_fin = final_norm.reshape(1, D)
    gq = jnp.tile(attn_q_norm, (1, A_H))
    gk = jnp.tile(attn_k_norm, (1, A_KV))
    seg = _seg_mean_matrix()
    ca, sa, cr, sr = _rope_tables(seq)

    pa, pr, pg, h = _in_proj(xs, g_mix, win)
    qh, kh, vh, rqh, rkh = _qk_prep(pa, pr, gq, gk, seg, ca, sa, cr, sr)

    tq = _tile(seq, 128)
    tk = _tile(seq // 4, 1024)
    qpad = _pad_heads(qh)
    qt8 = qh.reshape(seq, A_H, A_HD).transpose(1, 2, 0)
    vta = jnp.stack([jnp.concatenate([_chunk_t(vh[:, g * A_HD:(g + 1) * A_HD], tk),
                                      jnp.ones((seq // tk, 16, tk), _MXU)], axis=1) for g in range(A_KV)])
    ot, lse = _attn_fwd(qt8, kh, vta, tq, tk)
    o = _heads_to_rows(ot)

    zb = jnp.broadcast_to(ret_decay_logit.reshape(2 * R_H, 1, 1), (2 * R_H, 1, LANES))
    tm_, tmw, tqd, tqdw, tkd, tkdw, tg, tgw = _ret_tables(zb)
    cb = _tile(seq // CHUNK, 8)
    y2, pst = _ret_fwd(rqh, rkh, pr, tm_, tqd, tkd, tg, cb)

    x1, merged, ret_in = _merge_fwd(xs, o, y2, pr, pg, ret_norm_gain, wao, wro, wout)
    x2 = _mlp_fwd(x1, g_mlp, wup, wdown)

    dx2, de, dz, hp, loss_p, dg_fin, dg_ple = _ple_loss(x2, ps, tgt, g_ple, g_fin, wpg, wpg.T, wple)
    dx1, act, du, hm, dg_mlp = _mlp_bwd(x1, dx2, g_mlp, wup, wdown.T, wup.T)
    dpg, dao, dro, do, dry, drg, dg_gn = _merge_bwd(dx1, o, y2, pr, pg, ret_norm_gain, wao, wro,
                                                    wout.T, wao.T, wro.T)
    rdq, rdk, rdv, dlam = _ret_bwd(rqh, rkh, pr, dry, pst, tm_, tmw, tqd, tqdw, tkd, tkdw, tg, tgw, cb)

    ksplit = 2
    tkb = _tile(seq // 4, 512)
    dot_ = do.reshape(seq, A_H, A_HD).transpose(1, 2, 0)
    dk2, dv2, dqt = _attn_bwd(qpad, _pad_heads(_bf(do)), qt8, ot, dot_, lse, kh, vh, _chunk_t(kh, tkb),
                              tq, tkb, ksplit)
    dqh = _heads_to_rows(jnp.sum(dqt, axis=0))
    dpa, dpr, dg_q, dg_k = _qk_prep_bwd(pa, dqh, dk2, dv2, rdq, rdk, rdv, drg, gq, gk, seg, ca, sa, cr, sr)
    grad_x, dg_mix = _in_proj_bwd(xs, dx1, g_mix, dpa, dpr, dpg, win.T)

    wg = {
        "w_in": jnp.concatenate([_wgrad(h, dpa, "wgrad_in_a"), _wgrad(h, dpr, "wgrad_in_r"),
                                 _wgrad(h, dpg, "wgrad_in_g")], axis=1),
        "w_attn_o": _wgrad(o, dao, "wgrad_attn_o"),
        "w_ret_o": _wgrad(ret_in, dro, "wgrad_ret_o"),
        "w_out": _wgrad(merged, dx1, "wgrad_out"),
        "w_up": _wgrad(hm, du, "wgrad_up"),
        "w_down": _wgrad(act, dx2, "wgrad_down"),
        "w_ple_gate": _wgrad(hp, dz, "wgrad_ple_gate"),
        "w_ple": _wgrad(ps, de, "wgrad_ple"),
    }
    gpack = _pack_full_grads(wg)
    gpack_in = wg["w_in"].reshape(D, N_DEV, IN_SHARD).transpose(1, 0, 2)
    small = jnp.stack(
        [_row(dg_mix), _row(dg_mlp), _row(dg_ple), _row(dg_fin), _row(dg_gn), _row(dlam[:, 0, 0]),
         _row(loss_p[0, 0:1]), jnp.zeros((PACK_COLS,), F32)]
        + [_row(dg_q[0, hh * A_HD:(hh + 1) * A_HD]) for hh in range(A_H)]
        + [_row(dg_k[0, hh * A_HD:(hh + 1) * A_HD]) for hh in range(A_KV)]
        + [jnp.zeros((PACK_COLS,), F32)] * (SMALL_ROWS - 18))

    me = 4 * lax.axis_index("x") + 2 * lax.axis_index("y") + lax.axis_index("c")
    own_in = lax.dynamic_index_in_dim(gpack_in, me, axis=0, keepdims=False)
    own = lax.dynamic_index_in_dim(gpack, me, axis=0, keepdims=False)
    land_in, land, sland = _exchange_grads([_bf(gpack_in), _bf(gpack), small[None]])
    in_sh = _adamw_big(land_in, own_in, w_in[0], m_w_in[0], v_w_in[0], "adamw_w_in")
    g_sh, d_sh, m_sh, v_sh = _adamw_big(land, own, wshard, _pack_shards([args["m_" + n] for n in big_names]),
                                        _pack_shards([args["v_" + n] for n in big_names]), "adamw_shard")
    g_sm, d_sm, m_sm, v_sm, loss_row = _adamw_small(
        sland, _pack_small({n: args[n] for n in _SMALL}), _pack_small({n: args["m_" + n] for n in _SMALL}),
        _pack_small({n: args["v_" + n] for n in _SMALL}))

    names = ["mix_norm", "w_in", "attn_q_norm", "attn_k_norm", "ret_decay_logit", "ret_norm_gain", "w_attn_o",
             "w_ret_o", "w_out", "mlp_norm", "w_up", "w_down", "ple_norm", "w_ple_gate", "w_ple", "final_norm"]
    like = {n: args[n] for n in _SMALL}
    outs = [loss_row[0, 0], grad_x[None]]
    for big, sm, w_in_part in ((g_sh, g_sm, in_sh[0]), (d_sh, d_sm, in_sh[1]), (m_sh, m_sm, in_sh[2]),
                               (v_sh, v_sm, in_sh[3])):
        table = {**_unpack_shard(big), **_unpack_small(sm, like), "w_in": w_in_part[None]}
        outs += [table[n] for n in names]
    return tuple(outs)
```

```python
import functools

import jax
import jax.numpy as jnp
from jax import lax
from jax.experimental import pallas as pl
from jax.experimental.pallas import tpu as pltpu

F32 = jnp.float32
_MXU = jnp.bfloat16

D = 1024
PLE = 256
GRID_W = 64
A_HD = 64
A_H = 8
A_KV = 2
A_G = A_H // A_KV
AQ_W = A_H * A_HD
AKV_W = A_KV * A_HD
R_HD = 128
R_H = 4
R_W = R_H * R_HD
IN_W = AQ_W + 2 * AKV_W + 4 * R_W + 2 * D
PA_W = AQ_W + 2 * AKV_W
PR_W = 4 * R_W
PG_W = 2 * D
FF = 4 * D
CHUNK = 128
ROPE_THETA = 10000.0
EPS = 1e-6
GN_EPS = 1e-5
N_DEV = 8

LR, B1, B2, ADAM_EPS, WD, STEP = 0.001, 0.9, 0.999, 1e-08, 0.01, 10

LANES = 128
PACK_COLS = 1024
SMALL_ROWS = 24
HIGHEST = lax.Precision.HIGHEST


def _tile(n, pref):
    t = min(n, pref)
    assert n % t == 0, (n, t)
    return t


def _bf(a):
    return a.astype(_MXU)


def _mm(a, b):
    return jnp.dot(_bf(a), _bf(b), preferred_element_type=F32)


def _mm_nt(a, b):
    return lax.dot_general(_bf(a), _bf(b), (((1,), (1,)), ((), ())), preferred_element_type=F32)


def _mm_tn(a, b):
    return lax.dot_general(_bf(a), _bf(b), (((0,), (0,)), ((), ())), preferred_element_type=F32)


def _sigmoid(z):
    return 1.0 / (1.0 + jnp.exp(-z))


def _rms(x):
    r = lax.rsqrt(jnp.mean(x * x, axis=-1, keepdims=True) + EPS)
    return x * r, r


def _rms_bwd(n, r, gain, dy):
    dn = dy * gain
    return r * (dn - n * jnp.mean(dn * n, axis=-1, keepdims=True))


def _swap_halves(x, half):
    n = x.shape[-1]
    lane = lax.broadcasted_iota(jnp.int32, x.shape, x.ndim - 1)
    first = (lane % (2 * half)) < half
    return jnp.where(first, pltpu.roll(x, n - half, axis=1), pltpu.roll(x, half, axis=1))


def _rope(x, cos, sin, half):
    return x * cos + _swap_halves(x, half) * sin


def _cat(t, reps):
    return jnp.concatenate([t] * reps, axis=1)


def _full(shape):
    nd = len(shape)
    return pl.BlockSpec(shape, lambda *_: (0,) * nd)


def _rope_tables(seq):
    def tab(head_dim):
        n_axis = head_dim // 4
        freqs = ROPE_THETA ** (-jnp.arange(n_axis, dtype=F32) / n_axis)
        rows = seq // GRID_W
        row = jnp.repeat(jnp.arange(rows, dtype=F32), GRID_W)
        col = jnp.tile(jnp.arange(GRID_W, dtype=F32), rows)
        ang = jnp.concatenate([row[:, None] * freqs, col[:, None] * freqs], axis=-1)
        c, s = jnp.cos(ang), jnp.sin(ang)
        return jnp.concatenate([c, c], axis=-1), jnp.concatenate([-s, s], axis=-1)
    ca, sa = tab(A_HD)
    cr, sr = tab(R_HD)
    return jnp.tile(ca, (1, 2)), jnp.tile(sa, (1, 2)), cr, sr


def _seg_mean_matrix():
    i = jnp.arange(AQ_W) // A_HD
    return (i[:, None] == i[None, :]).astype(F32) / A_HD


def _mesh_pos():
    return lax.axis_index("x"), lax.axis_index("y"), lax.axis_index("c")


def _all_gather(shards):
    n = len(shards)

    def body(*refs):
        x_refs, out_refs = refs[:n], refs[n:2 * n]
        send_sems, recv_sems, local_sems = refs[2 * n:]
        x, y, c = _mesh_pos()
        me, sibling = (x, y, c), (x, y, 1 - c)
        chips = [(1 - x, y), (x, 1 - y), (1 - x, 1 - y)]

        def slot(t, px, py, pc):
            return out_refs[t].at[4 * px + 2 * py + pc]

        def copy(t, k, block, to, src=None):
            return pltpu.make_async_remote_copy(
                src_ref=slot(t, *block) if src is None else src, dst_ref=slot(t, *block),
                send_sem=send_sems.at[7 * t + k], recv_sem=recv_sems.at[7 * t + k],
                device_id=to, device_id_type=pl.DeviceIdType.MESH)

        mine = [pltpu.make_async_copy(x_refs[t], slot(t, *me), local_sems.at[t]) for t in range(n)]
        for cp in mine:
            cp.start()
        first = []
        for t in range(n):
            first.append(copy(t, 0, me, sibling, src=x_refs[t]))
            first += [copy(t, 1 + j, me, (*chip, c), src=x_refs[t]) for j, chip in enumerate(chips)]
        for cp in first:
            cp.start()
        passed = []
        for t in range(n):
            for j, chip in enumerate(chips):
                copy(t, 1 + j, (*chip, c), me).wait_recv()
                passed.append(copy(t, 4 + j, (*chip, c), sibling))
                passed[-1].start()
        for t in range(n):
            copy(t, 0, sibling, me).wait_recv()
            for j, chip in enumerate(chips):
                copy(t, 4 + j, (*chip, 1 - c), me).wait_recv()
        for cp in first + passed:
            cp.wait_send()
        for cp in mine:
            cp.wait()

    anyspec = pl.BlockSpec(memory_space=pl.ANY)
    return pl.pallas_call(
        body, name="all_gather_weights",
        out_shape=tuple(jax.ShapeDtypeStruct((N_DEV,) + s.shape, s.dtype) for s in shards),
        in_specs=[anyspec] * n, out_specs=(anyspec,) * n,
        scratch_shapes=[pltpu.SemaphoreType.DMA((7 * n,)), pltpu.SemaphoreType.DMA((7 * n,)),
                        pltpu.SemaphoreType.DMA((n,))],
    )(*shards)


def _exchange_grads(packs):
    n = len(packs)

    def body(*refs):
        g_refs, land_refs = refs[:n], refs[n:2 * n]
        send_sems, recv_sems, local_sems = refs[2 * n:]
        x, y, c = _mesh_pos()
        me = 4 * x + 2 * y + c

        def row(t, j):
            return g_refs[t].at[j if packs[t].shape[0] == N_DEV else 0]

        own = [pltpu.make_async_copy(row(t, me), land_refs[t].at[me], local_sems.at[t]) for t in range(n)]
        for cp in own:
            cp.start()
        sends = []
        for k in range(1, N_DEV):
            peer = (x ^ ((k >> 2) & 1), y ^ ((k >> 1) & 1), c ^ (k & 1))
            pidx = 4 * peer[0] + 2 * peer[1] + peer[2]
            for t in range(n):
                sends.append(pltpu.make_async_remote_copy(
                    src_ref=row(t, pidx), dst_ref=land_refs[t].at[me],
                    send_sem=send_sems.at[7 * t + k - 1], recv_sem=recv_sems.at[7 * t + k - 1],
                    device_id=peer, device_id_type=pl.DeviceIdType.MESH))
                sends[-1].start()
        for k in range(1, N_DEV):
            peer = (x ^ ((k >> 2) & 1), y ^ ((k >> 1) & 1), c ^ (k & 1))
            pidx = 4 * peer[0] + 2 * peer[1] + peer[2]
            for t in range(n):
                pltpu.make_async_remote_copy(
                    src_ref=row(t, me), dst_ref=land_refs[t].at[pidx],
                    send_sem=send_sems.at[7 * t + k - 1], recv_sem=recv_sems.at[7 * t + k - 1],
                    device_id=peer, device_id_type=pl.DeviceIdType.MESH).wait_recv()
        for cp in sends:
            cp.wait_send()
        for cp in own:
            cp.wait()

    anyspec = pl.BlockSpec(memory_space=pl.ANY)
    return pl.pallas_call(
        body, name="exchange_grads",
        out_shape=tuple(jax.ShapeDtypeStruct((N_DEV,) + g.shape[1:], g.dtype) for g in packs),
        in_specs=[anyspec] * n, out_specs=(anyspec,) * n,
        scratch_shapes=[pltpu.SemaphoreType.DMA((7 * n,)), pltpu.SemaphoreType.DMA((7 * n,)),
                        pltpu.SemaphoreType.DMA((n,))],
    )(*packs)


def _in_proj(x, gain, w):
    seq = x.shape[0]
    tm = _tile(seq, 256)

    def body(x_ref, g_ref, w_ref, pa_ref, pr_ref, pg_ref, h_ref):
        n, _ = _rms(x_ref[...])
        h = _bf(n * g_ref[...])
        h_ref[...] = h
        pa_ref[...] = _mm(h, w_ref[:, 0:PA_W])
        pr_ref[...] = _mm(h, w_ref[:, PA_W:PA_W + PR_W])
        pg_ref[...] = _mm(h, w_ref[:, PA_W + PR_W:IN_W])

    row = lambda w_: pl.BlockSpec((tm, w_), lambda i: (i, 0))
    return pl.pallas_call(
        body, name="in_proj", grid=(seq // tm,),
        in_specs=[row(D), _full((1, D)), _full((D, IN_W))],
        out_specs=(row(PA_W), row(PR_W), row(PG_W), row(D)),
        out_shape=(jax.ShapeDtypeStruct((seq, PA_W), F32), jax.ShapeDtypeStruct((seq, PR_W), F32),
                   jax.ShapeDtypeStruct((seq, PG_W), F32), jax.ShapeDtypeStruct((seq, D), _MXU)),
    )(x, gain, w)


def _qk_prep(pa, pr, gq, gk, seg, ca, sa, cr, sr):
    seq = pa.shape[0]
    tm = _tile(seq, 256)

    def body(pa_ref, pr_ref, gq_ref, gk_ref, seg_ref, ca_ref, sa_ref, cr_ref, sr_ref,
             qh_ref, kh_ref, v_ref, rq_ref, rk_ref):
        q = pa_ref[:, 0:AQ_W]
        k = pa_ref[:, AQ_W:AQ_W + AKV_W]
        v_ref[...] = _bf(pa_ref[:, AQ_W + AKV_W:PA_W])
        ca_, sa_ = ca_ref[...], sa_ref[...]
        msq = jnp.dot(q * q, seg_ref[...], precision=HIGHEST, preferred_element_type=F32)
        qn = q * lax.rsqrt(msq + EPS) * gq_ref[...]
        qh_ref[...] = _bf(_rope(qn, _cat(ca_, 4), _cat(sa_, 4), A_HD // 2) * (A_HD ** -0.5))
        msk = jnp.dot(k * k, seg_ref[0:AKV_W, 0:AKV_W], precision=HIGHEST, preferred_element_type=F32)
        kn = k * lax.rsqrt(msk + EPS) * gk_ref[...]
        kh_ref[...] = _bf(_rope(kn, ca_, sa_, A_HD // 2))
        cr_, sr_ = _cat(cr_ref[...], 4), _cat(sr_ref[...], 4)
        rq_ref[...] = _rope(pr_ref[:, 0:R_W], cr_, sr_, R_HD // 2) * (R_HD ** -0.5)
        rk_ref[...] = _rope(pr_ref[:, R_W:2 * R_W], cr_, sr_, R_HD // 2)

    row = lambda w_: pl.BlockSpec((tm, w_), lambda i: (i, 0))
    return pl.pallas_call(
        body, name="qk_prep", grid=(seq // tm,),
        in_specs=[row(PA_W), row(2 * R_W), _full((1, AQ_W)), _full((1, AKV_W)), _full((AQ_W, AQ_W)),
                  row(LANES), row(LANES), row(LANES), row(LANES)],
        out_specs=(row(AQ_W), row(AKV_W), row(AKV_W), row(R_W), row(R_W)),
        out_shape=(jax.ShapeDtypeStruct((seq, AQ_W), _MXU), jax.ShapeDtypeStruct((seq, AKV_W), _MXU),
                   jax.ShapeDtypeStruct((seq, AKV_W), _MXU), jax.ShapeDtypeStruct((seq, R_W), F32),
                   jax.ShapeDtypeStruct((seq, R_W), F32)),
    )(pa, pr, gq, gk, seg, ca, sa, cr, sr)


def _pad_heads(a):
    seq = a.shape[0]
    t = a.reshape(seq, A_H, A_HD).transpose(1, 0, 2)
    z = jnp.zeros((A_G, seq, A_HD), a.dtype)
    return jnp.concatenate([jnp.concatenate([t[:A_G], z], axis=-1),
                            jnp.concatenate([z, t[A_G:]], axis=-1)], axis=0)


def _chunk_t(a, tk):
    seq = a.shape[0]
    return a.reshape(seq // tk, tk, a.shape[1]).transpose(0, 2, 1)


def _heads_to_rows(t):
    return t.transpose(2, 0, 1).reshape(t.shape[2], AQ_W)


def _attn_fwd(qt8, k2, vta, tq, tk):
    seq = k2.shape[0]
    nck = seq // tk
    rows = A_G * tq
    vrows = vta.shape[2]
    rb = _tile(tk, 256)
    assert nck % 2 == 0, nck

    def body(qt_ref, k_ref, vt_ref, o_ref, lse_ref, m_sc, acc_sc, qtp_sc, s_a, s_b, p_a, p_b, al_a, al_b):
        g = pl.program_id(0)
        qtp_sc[...] = jnp.zeros_like(qtp_sc)
        qtp_sc[pl.ds(pl.multiple_of(g * A_HD, A_HD), A_HD), :] = jnp.concatenate(
            [qt_ref[a] for a in range(A_G)], axis=1)
        m_sc[...] = jnp.full((1, rows), -jnp.inf, F32)
        acc_sc[...] = jnp.zeros_like(acc_sc)

        def scores(c):
            kc = k_ref[pl.ds(pl.multiple_of(c * tk, tk), tk), :]
            return _mm(kc, qtp_sc[...])

        def stage(c, s_cur, s_nxt, p_cur, p_prv, al_cur, al_prv, first=False, last=False):
            if not last:
                s_nxt[...] = scores(c + 1)
            if not first:
                acc_sc[...] = al_prv[...] * acc_sc[...] + _mm(vt_ref[0, c - 1], p_prv[...])
            for b in range(rows // LANES):
                cs = slice(b * LANES, (b + 1) * LANES)
                m_old = m_sc[:, cs]
                m_new = m_old
                for r in range(0, tk, rb):
                    m_new = jnp.maximum(m_new, jnp.max(s_cur[r:r + rb, cs], axis=0, keepdims=True))
                for r in range(0, tk, rb):
                    p_cur[r:r + rb, cs] = _bf(jnp.exp(s_cur[r:r + rb, cs] - m_new))
                al_cur[:, cs] = jnp.exp(m_old - m_new)
                m_sc[:, cs] = m_new

        s_a[...] = scores(0)
        stage(0, s_a, s_b, p_a, p_b, al_a, al_b, first=True)

        def pair(j, carry):
            stage(2 * j + 1, s_b, s_a, p_b, p_a, al_b, al_a)
            stage(2 * j + 2, s_a, s_b, p_a, p_b, al_a, al_b)
            return carry

        lax.fori_loop(0, nck // 2 - 1, pair, 0)
        stage(nck - 1, s_b, s_a, p_b, p_a, al_b, al_a, last=True)
        acc = al_b[...] * acc_sc[...] + _mm(vt_ref[0, nck - 1], p_b[...])
        l = acc[A_HD:A_HD + 1, :]
        lse = m_sc[...] + jnp.log(l)
        out = acc[0:A_HD, :] * (1.0 / l)
        for a in range(A_G):
            o_ref[a] = out[:, a * tq:(a + 1) * tq]
            lse_ref[a] = lse[:, a * tq:(a + 1) * tq]

    return pl.pallas_call(
        body, name="attn_fwd", grid=(A_KV, seq // tq),
        in_specs=[pl.BlockSpec((A_G, A_HD, tq), lambda g, i: (g, 0, i)),
                  _full((seq, LANES)), pl.BlockSpec((1, nck, vrows, tk), lambda g, i: (g, 0, 0, 0))],
        out_specs=(pl.BlockSpec((A_G, A_HD, tq), lambda g, i: (g, 0, i)),
                   pl.BlockSpec((A_G, 1, tq), lambda g, i: (g, 0, i))),
        out_shape=(jax.ShapeDtypeStruct((A_H, A_HD, seq), F32), jax.ShapeDtypeStruct((A_H, 1, seq), F32)),
        scratch_shapes=[pltpu.VMEM((1, rows), F32), pltpu.VMEM((vrows, rows), F32), pltpu.VMEM((LANES, rows), _MXU),
                        pltpu.VMEM((tk, rows), F32), pltpu.VMEM((tk, rows), F32),
                        pltpu.VMEM((tk, rows), _MXU), pltpu.VMEM((tk, rows), _MXU),
                        pltpu.VMEM((1, rows), F32), pltpu.VMEM((1, rows), F32)],
    )(qt8, k2, vta)


def _attn_bwd(qpad, dopad, qt8, ot, dot_, lse, k2, v2, k2t, tq, tk, ksplit):
    seq = k2.shape[0]
    sh = seq // ksplit
    nck = sh // tk
    rows = A_G * tq
    rb = _tile(tk, 64)
    assert nck % 2 == 0, nck

    def body(q_ref, do_ref, qt_ref, ot_ref, dot_ref, lse_ref, k_ref, v_ref, kt_ref,
             dk_ref, dv_ref, dq_ref, dq_sc, qtp_sc, dotp_sc, s_a, s_b, dp_a, dp_b, p_a, p_b, ds_a, ds_b):
        g = pl.program_id(1)
        hrows = pl.ds(pl.multiple_of(g * A_HD, A_HD), A_HD)

        @pl.when((g == 0) & (pl.program_id(2) == 0))
        def _():
            dk_ref[...] = jnp.zeros_like(dk_ref)
            dv_ref[...] = jnp.zeros_like(dv_ref)

        lse_row = jnp.concatenate([lse_ref[a] for a in range(A_G)], axis=1)
        dd = jnp.concatenate([jnp.sum(ot_ref[a] * dot_ref[a], axis=0, keepdims=True)
                              for a in range(A_G)], axis=1)
        qtp_sc[...] = jnp.zeros_like(qtp_sc)
        dotp_sc[...] = jnp.zeros_like(dotp_sc)
        qtp_sc[hrows, :] = jnp.concatenate([qt_ref[a] for a in range(A_G)], axis=1)
        dotp_sc[hrows, :] = _bf(jnp.concatenate([dot_ref[a] for a in range(A_G)], axis=1))
        dq_sc[...] = jnp.zeros_like(dq_sc)

        def products(c, s_ref, dp_ref):
            sl = pl.ds(pl.multiple_of(c * tk, tk), tk)
            s_ref[...] = _mm(k_ref[sl, :], qtp_sc[...])
            dp_ref[...] = _mm(v_ref[sl, :], dotp_sc[...])

        def accumulate(c, p_ref, ds_ref):
            sl = pl.ds(pl.multiple_of(c * tk, tk), tk)
            dv_ref[sl, :] += _mm(p_ref[...], do_ref[...].reshape(rows, LANES))
            dk_ref[sl, :] += _mm(ds_ref[...], q_ref[...].reshape(rows, LANES))
            dq_sc[...] += _mm(kt_ref[c, hrows, :], ds_ref[...])

        def stage(c, s_cur, dp_cur, s_nxt, dp_nxt, p_cur, ds_cur, p_prv, ds_prv, first=False, last=False):
            if not last:
                products(c + 1, s_nxt, dp_nxt)
            if not first:
                accumulate(c - 1, p_prv, ds_prv)
            for r in range(0, tk, rb):
                p = jnp.exp(s_cur[r:r + rb, :] - lse_row)
                p_cur[r:r + rb, :] = _bf(p)
                ds_cur[r:r + rb, :] = _bf(p * (dp_cur[r:r + rb, :] - dd))

        products(0, s_a, dp_a)
        stage(0, s_a, dp_a, s_b, dp_b, p_a, ds_a, p_b, ds_b, first=True)

        def pair(j, carry):
            stage(2 * j + 1, s_b, dp_b, s_a, dp_a, p_b, ds_b, p_a, ds_a)
            stage(2 * j + 2, s_a, dp_a, s_b, dp_b, p_a, ds_a, p_b, ds_b)
            return carry

        lax.fori_loop(0, nck // 2 - 1, pair, 0)
        stage(nck - 1, s_b, dp_b, s_a, dp_a, p_b, ds_b, p_a, ds_a, last=True)
        accumulate(nck - 1, p_b, ds_b)
        for a in range(A_G):
            dq_ref[0, a] = dq_sc[:, a * tq:(a + 1) * tq]

    qspec = pl.BlockSpec((A_G, tq, LANES), lambda s, g, i: (g, i, 0))
    tspec = pl.BlockSpec((A_G, A_HD, tq), lambda s, g, i: (g, 0, i))
    kspec = pl.BlockSpec((sh, LANES), lambda s, g, i: (s, 0))
    big = lambda dt: pltpu.VMEM((tk, rows), dt)
    return pl.pallas_call(
        body, name="attn_bwd", grid=(ksplit, A_KV, seq // tq),
        in_specs=[qspec, qspec, tspec, tspec, tspec, pl.BlockSpec((A_G, 1, tq), lambda s, g, i: (g, 0, i)),
                  kspec, kspec, pl.BlockSpec((nck, LANES, tk), lambda s, g, i: (s, 0, 0))],
        out_specs=(kspec, kspec, pl.BlockSpec((1, A_G, A_HD, tq), lambda s, g, i: (s, g, 0, i))),
        out_shape=(jax.ShapeDtypeStruct((seq, LANES), F32), jax.ShapeDtypeStruct((seq, LANES), F32),
                   jax.ShapeDtypeStruct((ksplit, A_H, A_HD, seq), F32)),
        scratch_shapes=[pltpu.VMEM((A_HD, rows), F32), pltpu.VMEM((LANES, rows), _MXU), pltpu.VMEM((LANES, rows), _MXU),
                        big(F32), big(F32), big(F32), big(F32), big(_MXU), big(_MXU), big(_MXU), big(_MXU)],
    )(qpad, dopad, qt8, ot, dot_, lse, k2, v2, k2t)


def _ret_tables(zb):
    c = CHUNK

    def body(z_ref, m_ref, mw_ref, qd_ref, qdw_ref, kd_ref, kdw_ref, g_ref, gw_ref):
        fwd = pl.program_id(0) < R_H
        z = z_ref[0]
        lam = jnp.minimum(z, 0.0) - jnp.log(1.0 + jnp.exp(-jnp.abs(z)))
        i = lax.broadcasted_iota(jnp.int32, (c, c), 0).astype(F32)
        j = lax.broadcasted_iota(jnp.int32, (c, c), 1).astype(F32)
        diff = jnp.where(fwd, i - j, j - i)
        keep = diff >= jnp.where(fwd, 0.0, 1.0)
        dist = jnp.maximum(diff, 0.0)
        m = jnp.where(keep, jnp.exp(lam * dist), 0.0)
        m_ref[0] = m
        mw_ref[0] = m * dist
        fq = jnp.where(fwd, i + 1.0, c - i)
        qd = jnp.exp(lam * fq)
        qd_ref[0] = qd
        qdw_ref[0] = qd * fq
        fk = jnp.where(fwd, c - 1.0 - i, i)
        kd = jnp.exp(lam * fk)
        kd_ref[0] = kd
        kdw_ref[0] = kd * fk
        gdec = jnp.exp(lam * c)
        g_ref[0] = gdec
        gw_ref[0] = gdec * c

    big = pl.BlockSpec((1, c, c), lambda t: (t, 0, 0))
    vec = pl.BlockSpec((1, 1, LANES), lambda t: (t, 0, 0))
    bshape = jax.ShapeDtypeStruct((2 * R_H, c, c), F32)
    vshape = jax.ShapeDtypeStruct((2 * R_H, 1, LANES), F32)
    return pl.pallas_call(
        body, name="ret_tables", grid=(2 * R_H,), in_specs=[vec],
        out_specs=(big, big, big, big, big, big, vec, vec),
        out_shape=(bshape,) * 6 + (vshape, vshape),
    )(zb)


def _ret_fwd(rq, rk, pr, m, qd, kd, gdec, cb):
    seq = rq.shape[0]
    c = CHUNK
    ns = seq // (cb * c)

    def body(q_ref, k_ref, v_ref, m_ref, qd_ref, kd_ref, g_ref, y_ref, pst_ref, p_sc):
        d = pl.program_id(1)

        @pl.when(pl.program_id(2) == 0)
        def _():
            p_sc[...] = jnp.zeros((R_HD, R_HD), F32)

        mm, qdd, kdd, gg = m_ref[0], qd_ref[0], kd_ref[0], g_ref[0]

        def chunk(j, carry):
            cc = jnp.where(d == 0, j, cb - 1 - j)
            sl = pl.ds(pl.multiple_of(cc * c, c), c)
            q, k, v = q_ref[sl, :], k_ref[sl, :], v_ref[sl, :]
            p = p_sc[...]
            pst_ref[0, cc] = p
            a = _mm_nt(q, k) * mm
            y_ref[0, sl, :] = _mm(a, v) + _mm(q * qdd, p)
            p_sc[...] = p * gg + _mm_tn(k * kdd, v)
            return carry

        lax.fori_loop(0, cb, chunk, 0)

    def step(d, n):
        return d * (ns - 1 - n) + (1 - d) * n

    blk = lambda off: pl.BlockSpec((cb * c, R_HD), lambda h, d, n: (step(d, n), off + h))
    big = pl.BlockSpec((1, c, c), lambda h, d, n: (d * R_H + h, 0, 0))
    vec = pl.BlockSpec((1, 1, LANES), lambda h, d, n: (d * R_H + h, 0, 0))
    return pl.pallas_call(
        body, name="ret_fwd", grid=(R_H, 2, ns),
        in_specs=[blk(0), blk(0), blk(2 * R_H), big, big, big, vec],
        out_specs=(pl.BlockSpec((1, cb * c, R_HD), lambda h, d, n: (d, step(d, n), h)),
                   pl.BlockSpec((1, cb, R_HD, R_HD), lambda h, d, n: (d * R_H + h, step(d, n), 0, 0))),
        out_shape=(jax.ShapeDtypeStruct((2, seq, R_W), F32),
                   jax.ShapeDtypeStruct((2 * R_H, seq // c, R_HD, R_HD), F32)),
        scratch_shapes=[pltpu.VMEM((R_HD, R_HD), F32)],
    )(rq, rk, pr, m, qd, kd, gdec)


def _ret_bwd(rq, rk, pr, dry, pst, m, mw, qd, qdw, kd, kdw, gdec, gw, cb):
    seq = rq.shape[0]
    c = CHUNK
    ns = seq // (cb * c)

    def body(q_ref, k_ref, v_ref, dy_ref, pst_ref, m_ref, mw_ref, qd_ref, qdw_ref, kd_ref, kdw_ref,
             g_ref, gw_ref, dq_ref, dk_ref, dv_ref, dlam_ref, r_sc, acc_sc, e_sc, g_sc):
        d = pl.program_id(1)
        n = pl.program_id(2)

        @pl.when(n == 0)
        def _():
            r_sc[...] = jnp.zeros_like(r_sc)
            acc_sc[...] = jnp.zeros_like(acc_sc)
            e_sc[...] = jnp.zeros_like(e_sc)
            g_sc[...] = jnp.zeros_like(g_sc)

        mm, mww, qdd, qdww, kdd, kdww, gg = (m_ref[0], mw_ref[0], qd_ref[0], qdw_ref[0],
                                             kd_ref[0], kdw_ref[0], g_ref[0])

        def chunk(j, carry):
            cc = jnp.where(d == 0, cb - 1 - j, j)
            sl = pl.ds(pl.multiple_of(cc * c, c), c)
            q, k, v, dy = q_ref[sl, :], k_ref[sl, :], v_ref[sl, :], dy_ref[sl, :]
            p = pst_ref[0, cc]
            r = r_sc[...]
            qk = _mm_nt(q, k)
            a = qk * mm
            ds = _mm_nt(dy, v)
            da = ds * mm
            dyp = _mm_nt(dy, p)
            vr = _mm_nt(v, r)
            dv_ref[0, sl, :] = _mm_tn(a, dy) + _mm(k * kdd, r)
            dq_ref[0, sl, :] = _mm(da, k) + dyp * qdd
            dk_ref[0, sl, :] = _mm_tn(da, q) + vr * kdd
            acc_sc[...] += dyp * q * qdww + vr * k * kdww
            e_sc[...] += ds * qk * mww
            g_sc[...] += r * p
            r_sc[...] = r * gg + _mm_tn(q * qdd, dy)
            return carry

        lax.fori_loop(0, cb, chunk, 0)

        @pl.when(n == ns - 1)
        def _():
            tot = (jnp.sum(jnp.sum(acc_sc[...] + e_sc[...] + g_sc[...] * gw_ref[0], axis=0, keepdims=True),
                           axis=1, keepdims=True))
            dlam_ref[0] = jnp.broadcast_to(tot, (1, LANES))

    def step(d, n):
        return d * n + (1 - d) * (ns - 1 - n)

    blk = lambda off: pl.BlockSpec((cb * c, R_HD), lambda h, d, n: (step(d, n), off + h))
    big = pl.BlockSpec((1, c, c), lambda h, d, n: (d * R_H + h, 0, 0))
    vec = pl.BlockSpec((1, 1, LANES), lambda h, d, n: (d * R_H + h, 0, 0))
    out = pl.BlockSpec((1, cb * c, R_HD), lambda h, d, n: (d, step(d, n), h))
    oshape = jax.ShapeDtypeStruct((2, seq, R_W), F32)
    return pl.pallas_call(
        body, name="ret_bwd", grid=(R_H, 2, ns),
        in_specs=[blk(0), blk(0), blk(2 * R_H), blk(0),
                  pl.BlockSpec((1, cb, R_HD, R_HD), lambda h, d, n: (d * R_H + h, step(d, n), 0, 0)),
                  big, big, big, big, big, big, vec, vec],
        out_specs=(out, out, out, vec),
        out_shape=(oshape, oshape, oshape, jax.ShapeDtypeStruct((2 * R_H, 1, LANES), F32)),
        scratch_shapes=[pltpu.VMEM((R_HD, R_HD), F32), pltpu.VMEM((c, R_HD), F32),
                        pltpu.VMEM((c, c), F32), pltpu.VMEM((R_HD, R_HD), F32)],
    )(rq, rk, pr, dry, pst, m, mw, qd, qdw, kd, kdw, gdec, gw)


def _group_norm(ry):
    yn, rs = [], []
    for h in range(R_H):
        s = ry[:, h * R_HD:(h + 1) * R_HD]
        mu = jnp.mean(s, axis=-1, keepdims=True)
        cen = s - mu
        r = lax.rsqrt(jnp.mean(cen * cen, axis=-1, keepdims=True) + GN_EPS)
        yn.append(cen * r)
        rs.append(r)
    return yn, rs


def _merge_fwd(x, o, y2, pr, pg, gain_r, wao, wro, wout):
    seq = x.shape[0]
    tm = _tile(seq, 256)

    def body(x_ref, o_ref, yf_ref, yb_ref, rg_ref, ga_ref, gr_ref, gn_ref, wao_ref, wro_ref, wout_ref,
             x1_ref, mg_ref, ri_ref):
        yn, _ = _group_norm(yf_ref[0] + yb_ref[0])
        rg = rg_ref[...]
        ret_in = jnp.concatenate(yn, axis=1) * gn_ref[...] * (rg * _sigmoid(rg))
        ri_ref[...] = _bf(ret_in)
        attn_out = _mm(o_ref[...], wao_ref[...])
        ret_out = _mm(ret_in, wro_ref[...])
        merged = _sigmoid(ga_ref[...]) * attn_out + _sigmoid(gr_ref[...]) * ret_out
        mg_ref[...] = _bf(merged)
        x1_ref[...] = x_ref[...] + _mm(merged, wout_ref[...])

    row = lambda w_, j=0: pl.BlockSpec((tm, w_), lambda i: (i, j))
    ydir = lambda d: pl.BlockSpec((1, tm, R_W), lambda i: (d, i, 0))
    return pl.pallas_call(
        body, name="merge_fwd", grid=(seq // tm,),
        in_specs=[row(D), row(AQ_W), ydir(0), ydir(1), row(R_W, 3), row(D, 0), row(D, 1),
                  _full((1, R_W)), _full((AQ_W, D)), _full((R_W, D)), _full((D, D))],
        out_specs=(row(D), row(D), row(R_W)),
        out_shape=(jax.ShapeDtypeStruct((seq, D), F32), jax.ShapeDtypeStruct((seq, D), _MXU),
                   jax.ShapeDtypeStruct((seq, R_W), _MXU)),
    )(x, o, y2, y2, pr, pg, pg, gain_r, wao, wro, wout)


def _mlp_fwd(x1, gain, wup, wdown):
    seq = x1.shape[0]
    tm = _tile(seq, 512)
    fc = 512
    nfc = FF // fc

    def body(x_ref, g_ref, wu_ref, wd_ref, x2_ref, hm_sc, acc_sc):
        c = pl.program_id(1)

        @pl.when(c == 0)
        def _():
            n, _ = _rms(x_ref[...])
            hm_sc[...] = _bf(n * g_ref[...])
            acc_sc[...] = jnp.zeros_like(acc_sc)

        u = jnp.maximum(_mm(hm_sc[...], wu_ref[...]), 0.0)
        acc_sc[...] += _mm(u * u, wd_ref[...])

        @pl.when(c == nfc - 1)
        def _():
            x2_ref[...] = x_ref[...] + acc_sc[...]

    return pl.pallas_call(
        body, name="mlp_fwd", grid=(seq // tm, nfc),
        in_specs=[pl.BlockSpec((tm, D), lambda i, c: (i, 0)), pl.BlockSpec((1, D), lambda i, c: (0, 0)),
                  pl.BlockSpec((D, fc), lambda i, c: (0, c)), pl.BlockSpec((fc, D), lambda i, c: (c, 0))],
        out_specs=pl.BlockSpec((tm, D), lambda i, c: (i, 0)),
        out_shape=jax.ShapeDtypeStruct((seq, D), F32),
        scratch_shapes=[pltpu.VMEM((tm, D), _MXU), pltpu.VMEM((tm, D), F32)],
    )(x1, gain, wup, wdown)


def _ple_loss(x2, p, tgt, g_ple, g_fin, wpg, wpgt, wple):
    seq = x2.shape[0]
    tm = _tile(seq, 256)

    def body(x2_ref, p_ref, t_ref, gp_ref, gf_ref, wpg_ref, wpgt_ref, wple_ref,
             dx2_ref, de_ref, dz_ref, hp_ref, loss_ref, dgf_ref, dgp_ref):
        @pl.when(pl.program_id(0) == 0)
        def _():
            loss_ref[...] = jnp.zeros_like(loss_ref)
            dgf_ref[...] = jnp.zeros_like(dgf_ref)
            dgp_ref[...] = jnp.zeros_like(dgp_ref)

        x2 = x2_ref[...]
        gp, gf = gp_ref[...], gf_ref[...]
        n2, r2 = _rms(x2)
        hp = _bf(n2 * gp)
        hp_ref[...] = hp
        gate = _sigmoid(_mm(hp, wpg_ref[...]))
        e = _mm(p_ref[...], wple_ref[...])
        x3 = x2 + gate * e
        n3, r3 = _rms(x3)
        diff = n3 * gf - t_ref[...]
        row_loss = jnp.mean(diff * diff, axis=-1, keepdims=True)
        loss_ref[...] += 0.5 * jnp.sum(row_loss, axis=0, keepdims=True)
        dy = diff * (1.0 / D)
        dgf_ref[...] += jnp.sum(dy * n3, axis=0, keepdims=True)
        dx3 = _rms_bwd(n3, r3, gf, dy)
        de_ref[...] = _bf(dx3 * gate)
        dz = dx3 * e * gate * (1.0 - gate)
        dz_ref[...] = _bf(dz)
        dhp = _mm(dz, wpgt_ref[...])
        dgp_ref[...] += jnp.sum(dhp * n2, axis=0, keepdims=True)
        dx2_ref[...] = dx3 + _rms_bwd(n2, r2, gp, dhp)

    row = lambda w_: pl.BlockSpec((tm, w_), lambda i: (i, 0))
    act = lambda dt: jax.ShapeDtypeStruct((seq, D), dt)
    return pl.pallas_call(
        body, name="ple_loss", grid=(seq // tm,),
        in_specs=[row(D), row(PLE), row(D), _full((1, D)), _full((1, D)),
                  _full((D, D)), _full((D, D)), _full((PLE, D))],
        out_specs=(row(D), row(D), row(D), row(D), _full((1, LANES)), _full((1, D)), _full((1, D))),
        out_shape=(act(F32), act(_MXU), act(_MXU), act(_MXU), jax.ShapeDtypeStruct((1, LANES), F32),
                   jax.ShapeDtypeStruct((1, D), F32), jax.ShapeDtypeStruct((1, D), F32)),
    )(x2, p, tgt, g_ple, g_fin, wpg, wpgt, wple)


def _mlp_bwd(x1, dx2, gain, wup, wdownt, wupt):
    seq = x1.shape[0]
    tm = _tile(seq, 512)
    fc = 512
    nfc = FF // fc

    def body(x_ref, dx2_ref, g_ref, wu_ref, wdt_ref, wut_ref,
             dx1_ref, a_ref, du_ref, hm_ref, dg_ref, dhm_sc):
        i = pl.program_id(0)
        c = pl.program_id(1)

        @pl.when((i == 0) & (c == 0))
        def _():
            dg_ref[...] = jnp.zeros_like(dg_ref)

        @pl.when(c == 0)
        def _():
            n, _ = _rms(x_ref[...])
            hm_ref[...] = _bf(n * g_ref[...])
            dhm_sc[...] = jnp.zeros_like(dhm_sc)

        u = jnp.maximum(_mm(hm_ref[...], wu_ref[...]), 0.0)
        a_ref[...] = _bf(u * u)
        du = _mm(dx2_ref[...], wdt_ref[...]) * (2.0 * u)
        du_ref[...] = _bf(du)
        dhm_sc[...] += _mm(du, wut_ref[...])

        @pl.when(c == nfc - 1)
        def _():
            n, r = _rms(x_ref[...])
            dhm = dhm_sc[...]
            dg_ref[...] += jnp.sum(dhm * n, axis=0, keepdims=True)
            dx1_ref[...] = dx2_ref[...] + _rms_bwd(n, r, g_ref[...], dhm)

    rowd = pl.BlockSpec((tm, D), lambda i, c: (i, 0))
    rowf = pl.BlockSpec((tm, fc), lambda i, c: (i, c))
    return pl.pallas_call(
        body, name="mlp_bwd", grid=(seq // tm, nfc),
        in_specs=[rowd, rowd, pl.BlockSpec((1, D), lambda i, c: (0, 0)),
                  pl.BlockSpec((D, fc), lambda i, c: (0, c)), pl.BlockSpec((D, fc), lambda i, c: (0, c)),
                  pl.BlockSpec((fc, D), lambda i, c: (c, 0))],
        out_specs=(rowd, rowf, rowf, rowd, pl.BlockSpec((1, D), lambda i, c: (0, 0))),
        out_shape=(jax.ShapeDtypeStruct((seq, D), F32), jax.ShapeDtypeStruct((seq, FF), _MXU),
                   jax.ShapeDtypeStruct((seq, FF), _MXU), jax.ShapeDtypeStruct((seq, D), _MXU),
                   jax.ShapeDtypeStruct((1, D), F32)),
        scratch_shapes=[pltpu.VMEM((tm, D), F32)],
    )(x1, dx2, gain, wup, wdownt, wupt)


def _merge_bwd(dx1, o, y2, pr, pg, gain_r, wao, wro, woutt, waot, wrot):
    seq = dx1.shape[0]
    tm = _tile(seq, 256)

    def body(dx1_ref, o_ref, yf_ref, yb_ref, rg_ref, ga_ref, gr_ref, gn_ref, wao_ref, wro_ref,
             woutt_ref, waot_ref, wrot_ref,
             dpg_ref, dao_ref, dro_ref, do_ref, dry_ref, drg_ref, dgn_ref):
        @pl.when(pl.program_id(0) == 0)
        def _():
            dgn_ref[...] = jnp.zeros_like(dgn_ref)

        yn_l, rs_l = _group_norm(yf_ref[0] + yb_ref[0])
        yn = jnp.concatenate(yn_l, axis=1)
        rg = rg_ref[...]
        gn = gn_ref[...]
        sg = _sigmoid(rg)
        sil = rg * sg
        ret_in = yn * gn * sil
        attn_out = _mm(o_ref[...], wao_ref[...])
        ret_out = _mm(ret_in, wro_ref[...])
        sa = _sigmoid(ga_ref[...])
        sr = _sigmoid(gr_ref[...])
        dm = _mm(dx1_ref[...], woutt_ref[...])
        dpg_ref[:, 0:D] = _bf(dm * attn_out * sa * (1.0 - sa))
        dpg_ref[:, D:2 * D] = _bf(dm * ret_out * sr * (1.0 - sr))
        dao = _bf(dm * sa)
        dro = _bf(dm * sr)
        dao_ref[...] = dao
        dro_ref[...] = dro
        do_ref[...] = _mm(dao, waot_ref[...])
        dri = _mm(dro, wrot_ref[...])
        dgn_ref[...] += jnp.sum(dri * yn * sil, axis=0, keepdims=True)
        drg_ref[...] = _bf(dri * yn * gn * (sg * (1.0 + rg * (1.0 - sg))))
        dyn = dri * gn * sil
        dry = []
        for h in range(R_H):
            dh = dyn[:, h * R_HD:(h + 1) * R_HD]
            dry.append(rs_l[h] * (dh - jnp.mean(dh, axis=-1, keepdims=True)
                                  - yn_l[h] * jnp.mean(dh * yn_l[h], axis=-1, keepdims=True)))
        dry_ref[...] = jnp.concatenate(dry, axis=1)

    row = lambda w_, j=0: pl.BlockSpec((tm, w_), lambda i: (i, j))
    ydir = lambda d: pl.BlockSpec((1, tm, R_W), lambda i: (d, i, 0))
    return pl.pallas_call(
        body, name="merge_bwd", grid=(seq // tm,),
        in_specs=[row(D), row(AQ_W), ydir(0), ydir(1), row(R_W, 3), row(D, 0), row(D, 1),
                  _full((1, R_W)), _full((AQ_W, D)), _full((R_W, D)), _full((D, D)),
                  _full((D, AQ_W)), _full((D, R_W))],
        out_specs=(row(PG_W), row(D), row(D), row(AQ_W), row(R_W), row(R_W), _full((1, R_W))),
        out_shape=(jax.ShapeDtypeStruct((seq, PG_W), _MXU), jax.ShapeDtypeStruct((seq, D), _MXU),
                   jax.ShapeDtypeStruct((seq, D), _MXU), jax.ShapeDtypeStruct((seq, AQ_W), F32),
                   jax.ShapeDtypeStruct((seq, R_W), F32), jax.ShapeDtypeStruct((seq, R_W), _MXU),
                   jax.ShapeDtypeStruct((1, R_W), F32)),
    )(dx1, o, y2, y2, pr, pg, pg, gain_r, wao, wro, woutt, waot, wrot)


def _qk_prep_bwd(pa, dqh, dk2, dv2, rdq, rdk, rdv, drg, gq, gk, seg, ca, sa, cr, sr):
    seq = pa.shape[0]
    tm = _tile(seq, 256)

    def body(pa_ref, dqh_ref, dk2_ref, dv2_ref, rdqf_ref, rdqb_ref, rdkf_ref, rdkb_ref, rdvf_ref, rdvb_ref,
             drg_ref, gq_ref, gk_ref, seg_ref, ca_ref, sa_ref, cr_ref, sr_ref,
             dpa_ref, dpr_ref, dgq_ref, dgk_ref):
        @pl.when(pl.program_id(0) == 0)
        def _():
            dgq_ref[...] = jnp.zeros_like(dgq_ref)
            dgk_ref[...] = jnp.zeros_like(dgk_ref)

        ca_, sa_ = ca_ref[...], sa_ref[...]

        def norm_bwd(raw, gain, dy, segm, dg_ref):
            msq = jnp.dot(raw * raw, segm, precision=HIGHEST, preferred_element_type=F32)
            r = lax.rsqrt(msq + EPS)
            n = raw * r
            dg_ref[...] += jnp.sum(dy * n, axis=0, keepdims=True)
            dn = dy * gain
            return r * (dn - n * jnp.dot(dn * n, segm, precision=HIGHEST, preferred_element_type=F32))

        dqn = _rope(dqh_ref[...] * (A_HD ** -0.5), _cat(ca_, 4), -_cat(sa_, 4), A_HD // 2)
        dpa_ref[:, 0:AQ_W] = _bf(norm_bwd(pa_ref[:, 0:AQ_W], gq_ref[...], dqn, seg_ref[...], dgq_ref))
        dkn = _rope(dk2_ref[...], ca_, -sa_, A_HD // 2)
        dpa_ref[:, AQ_W:AQ_W + AKV_W] = _bf(norm_bwd(pa_ref[:, AQ_W:AQ_W + AKV_W], gk_ref[...], dkn,
                                                     seg_ref[0:AKV_W, 0:AKV_W], dgk_ref))
        dpa_ref[:, AQ_W + AKV_W:PA_W] = _bf(dv2_ref[...])
        cr_, sr_ = _cat(cr_ref[...], 4), -_cat(sr_ref[...], 4)
        dpr_ref[:, 0:R_W] = _bf(_rope((rdqf_ref[0] + rdqb_ref[0]) * (R_HD ** -0.5), cr_, sr_, R_HD // 2))
        dpr_ref[:, R_W:2 * R_W] = _bf(_rope(rdkf_ref[0] + rdkb_ref[0], cr_, sr_, R_HD // 2))
        dpr_ref[:, 2 * R_W:3 * R_W] = _bf(rdvf_ref[0] + rdvb_ref[0])
        dpr_ref[:, 3 * R_W:4 * R_W] = drg_ref[...]

    row = lambda w_: pl.BlockSpec((tm, w_), lambda i: (i, 0))
    ydir = lambda d: pl.BlockSpec((1, tm, R_W), lambda i: (d, i, 0))
    return pl.pallas_call(
        body, name="qk_prep_bwd", grid=(seq // tm,),
        in_specs=[row(PA_W), row(AQ_W), row(AKV_W), row(AKV_W), ydir(0), ydir(1), ydir(0), ydir(1),
                  ydir(0), ydir(1), row(R_W), _full((1, AQ_W)), _full((1, AKV_W)), _full((AQ_W, AQ_W)),
                  row(LANES), row(LANES), row(LANES), row(LANES)],
        out_specs=(row(PA_W), row(PR_W), _full((1, AQ_W)), _full((1, AKV_W))),
        out_shape=(jax.ShapeDtypeStruct((seq, PA_W), _MXU), jax.ShapeDtypeStruct((seq, PR_W), _MXU),
                   jax.ShapeDtypeStruct((1, AQ_W), F32), jax.ShapeDtypeStruct((1, AKV_W), F32)),
    )(pa, dqh, dk2, dv2, rdq, rdq, rdk, rdk, rdv, rdv, drg, gq, gk, seg, ca, sa, cr, sr)


def _in_proj_bwd(x, dx1, gain, dpa, dpr, dpg, wint):
    seq = x.shape[0]
    tm = _tile(seq, 256)

    def body(x_ref, dx1_ref, g_ref, dpa_ref, dpr_ref, dpg_ref, wt_ref, dx_ref, dg_ref):
        @pl.when(pl.program_id(0) == 0)
        def _():
            dg_ref[...] = jnp.zeros_like(dg_ref)

        dh = (_mm(dpa_ref[...], wt_ref[0:PA_W, :]) + _mm(dpr_ref[...], wt_ref[PA_W:PA_W + PR_W, :])
              + _mm(dpg_ref[...], wt_ref[PA_W + PR_W:IN_W, :]))
        n, r = _rms(x_ref[...])
        dg_ref[...] += jnp.sum(dh * n, axis=0, keepdims=True)
        dx_ref[...] = dx1_ref[...] + _rms_bwd(n, r, g_ref[...], dh)

    row = lambda w_: pl.BlockSpec((tm, w_), lambda i: (i, 0))
    return pl.pallas_call(
        body, name="in_proj_bwd", grid=(seq // tm,),
        in_specs=[row(D), row(D), _full((1, D)), row(PA_W), row(PR_W), row(PG_W), _full((IN_W, D))],
        out_specs=(row(D), _full((1, D))),
        out_shape=(jax.ShapeDtypeStruct((seq, D), F32), jax.ShapeDtypeStruct((1, D), F32)),
    )(x, dx1, gain, dpa, dpr, dpg, wint)


def _wgrad(a, b, name):
    seq, m = a.shape
    n = b.shape[1]
    tm, tn, ts = _tile(m, 512), _tile(n, 1024), _tile(seq, 512)
    ns = seq // ts

    def body(a_ref, b_ref, o_ref):
        @pl.when(pl.program_id(2) == 0)
        def _():
            o_ref[...] = jnp.zeros_like(o_ref)

        o_ref[...] += _mm_tn(a_ref[...], b_ref[...])

    return pl.pallas_call(
        body, name=name, grid=(m // tm, n // tn, ns),
        in_specs=[pl.BlockSpec((ts, tm), lambda i, j, s: (s, i)), pl.BlockSpec((ts, tn), lambda i, j, s: (s, j))],
        out_specs=pl.BlockSpec((tm, tn), lambda i, j, s: (i, j)),
        out_shape=jax.ShapeDtypeStruct((m, n), F32),
    )(a, b)


def _adamw_math(w, g, m, v):
    m = B1 * m + (1.0 - B1) * g
    v = B2 * v + (1.0 - B2) * (g * g)
    m_hat = m / (1.0 - B1 ** STEP)
    v_hat = v / (1.0 - B2 ** STEP)
    delta = -LR * (m_hat / (jnp.sqrt(v_hat) + ADAM_EPS) + WD * w)
    return delta, m, v


def _adamw_big(land, own, w, m, v, name):
    rws, cols = w.shape
    tr = next(t for t in range(min(rws, 288), 0, -8) if rws % t == 0)

    def body(l_ref, o_ref, w_ref, m_ref, v_ref, g_ref, d_ref, nm_ref, nv_ref):
        x, y, c = _mesh_pos()
        me = 4 * x + 2 * y + c
        g = o_ref[...]
        for j in range(N_DEV):
            g = g + jnp.where(me == j, 0.0, l_ref[j].astype(F32))
        g_ref[...] = g
        d_ref[...], nm_ref[...], nv_ref[...] = _adamw_math(w_ref[...], g, m_ref[...], v_ref[...])

    row = pl.BlockSpec((tr, cols), lambda i: (i, 0))
    shp = jax.ShapeDtypeStruct((rws, cols), F32)
    return pl.pallas_call(
        body, name=name, grid=(rws // tr,),
        in_specs=[pl.BlockSpec((N_DEV, tr, cols), lambda i: (0, i, 0)), row, row, row, row],
        out_specs=(row, row, row, row), out_shape=(shp, shp, shp, shp),
    )(land, own, w, m, v)


def _adamw_small(sland, w, m, v):
    def body(l_ref, w_ref, m_ref, v_ref, g_ref, d_ref, nm_ref, nv_ref, loss_ref):
        s = l_ref[0]
        for j in range(1, N_DEV):
            s = s + l_ref[j]
        w = w_ref[...]
        gq = s[8:9]
        for h in range(1, A_H):
            gq = gq + s[8 + h:9 + h]
        gk = s[16:17] + s[17:18]
        gdec = s[5:6] * _sigmoid(-w[5:6])
        g = jnp.concatenate([s[0:5], gdec, gq, gk], axis=0)
        g_ref[...] = g
        d_ref[...], nm_ref[...], nv_ref[...] = _adamw_math(w, g, m_ref[...], v_ref[...])
        loss_ref[...] = s[6:7, 0:LANES]

    shp = jax.ShapeDtypeStruct((8, PACK_COLS), F32)
    return pl.pallas_call(
        body, name="adamw_small",
        out_shape=(shp, shp, shp, shp, jax.ShapeDtypeStruct((1, LANES), F32)),
    )(sland, w, m, v)


_BIG = (("w_attn_o", AQ_W, D, 1), ("w_ret_o", R_W, D, 1), ("w_out", D, D, 0),
        ("w_up", D, FF, 1), ("w_down", FF, D, 0), ("w_ple_gate", D, D, 0), ("w_ple", PLE, D, 1))
IN_SHARD = IN_W // N_DEV
_SMALL = ("mix_norm", "mlp_norm", "ple_norm", "final_norm", "ret_norm_gain", "ret_decay_logit",
          "attn_q_norm", "attn_k_norm")


def _shard_shape(rows, cols, axis):
    return (rows // N_DEV, cols) if axis == 0 else (rows, cols // N_DEV)


def _pack_shards(shards):
    flat = jnp.concatenate([s.reshape(-1) for s in shards])
    return flat.reshape(-1, PACK_COLS)


def _unpack_gathered(gathered):
    flat = gathered.reshape(N_DEV, -1)
    out, off = {}, 0
    for name, rows, cols, axis in _BIG:
        sr, sc = _shard_shape(rows, cols, axis)
        blk = flat[:, off:off + sr * sc].reshape(N_DEV, sr, sc)
        off += sr * sc
        out[name] = blk.reshape(rows, cols) if axis == 0 else blk.transpose(1, 0, 2).reshape(rows, cols)
    return out


def _pack_full_grads(grads):
    parts = []
    for name, rows, cols, axis in _BIG:
        sr, sc = _shard_shape(rows, cols, axis)
        g = grads[name]
        blk = g.reshape(N_DEV, sr, sc) if axis == 0 else g.reshape(rows, N_DEV, sc).transpose(1, 0, 2)
        parts.append(blk.reshape(N_DEV, -1))
    flat = jnp.concatenate(parts, axis=1)
    return flat.reshape(N_DEV, -1, PACK_COLS)


def _unpack_shard(packed):
    flat = packed.reshape(-1)
    out, off = {}, 0
    for name, rows, cols, axis in _BIG:
        sr, sc = _shard_shape(rows, cols, axis)
        out[name] = flat[off:off + sr * sc].reshape(1, sr, sc)
        off += sr * sc
    return out


def _pack_small(vals):
    rows = [jnp.pad(vals[n].reshape(-1), (0, PACK_COLS - vals[n].size)) for n in _SMALL]
    return jnp.stack(rows)


def _unpack_small(packed, like):
    return {n: packed[i, :like[n].size].reshape(like[n].shape) for i, n in enumerate(_SMALL)}


def _row(v):
    return jnp.pad(v.reshape(-1), (0, PACK_COLS - v.size))


def kernel(x, p, mix_norm, w_in, attn_q_norm, attn_k_norm, ret_decay_logit, ret_norm_gain, w_attn_o, w_ret_o, w_out, mlp_norm, w_up, w_down, ple_norm, w_ple_gate, w_ple, final_norm, loss_target, m_mix_norm, m_w_in, m_attn_q_norm, m_attn_k_norm, m_ret_decay_logit, m_ret_norm_gain, m_w_attn_o, m_w_ret_o, m_w_out, m_mlp_norm, m_w_up, m_w_down, m_ple_norm, m_w_ple_gate, m_w_ple, m_final_norm, v_mix_norm, v_w_in, v_attn_q_norm, v_attn_k_norm, v_ret_decay_logit, v_ret_norm_gain, v_w_attn_o, v_w_ret_o, v_w_out, v_mlp_norm, v_w_up, v_w_down, v_ple_norm, v_w_ple_gate, v_w_ple, v_final_norm):
    args = dict(locals())
    seq = x.shape[1]
    xs = x[0]
    ps = p[0, 0]
    tgt = loss_target[0]

    big_names = [b[0] for b in _BIG]
    wshard = _pack_shards([args[n] for n in big_names])
    win_g, rest_g = _all_gather([w_in[0].astype(_MXU), wshard.astype(_MXU)])
    win = win_g.transpose(1, 0, 2).reshape(D, IN_W)
    wfull = _unpack_gathered(rest_g)
    wao, wro, wout = wfull["w_attn_o"], wfull["w_ret_o"], wfull["w_out"]
    wup, wdown, wpg, wple = wfull["w_up"], wfull["w_down"], wfull["w_ple_gate"], wfull["w_ple"]

    g_mix, g_mlp, g_ple = mix_norm, mlp_norm, ple_norm
    g_fin = final_norm.reshape(1, D)
    gq = jnp.tile(attn_q_norm, (1, A_H))
    gk = jnp.tile(attn_k_norm, (1, A_KV))
    seg = _seg_mean_matrix()
    ca, sa, cr, sr = _rope_tables(seq)

    pa, pr, pg, h = _in_proj(xs, g_mix, win)
    qh, kh, vh, rqh, rkh = _qk_prep(pa, pr, gq, gk, seg, ca, sa, cr, sr)

    tq = _tile(seq, 128)
    tk = _tile(seq // 4, 1024)
    qpad = _pad_heads(qh)
    qt8 = qh.reshape(seq, A_H, A_HD).transpose(1, 2, 0)
    vta = jnp.stack([jnp.concatenate([_chunk_t(vh[:, g * A_HD:(g + 1) * A_HD], tk),
                                      jnp.ones((seq // tk, 16, tk), _MXU)], axis=1) for g in range(A_KV)])
    ot, lse = _attn_fwd(qt8, kh, vta, tq, tk)
    o = _heads_to_rows(ot)

    zb = jnp.broadcast_to(ret_decay_logit.reshape(2 * R_H, 1, 1), (2 * R_H, 1, LANES))
    tm_, tmw, tqd, tqdw, tkd, tkdw, tg, tgw = _ret_tables(zb)
    cb = _tile(seq // CHUNK, 8)
    y2, pst = _ret_fwd(rqh, rkh, pr, tm_, tqd, tkd, tg, cb)

    x1, merged, ret_in = _merge_fwd(xs, o, y2, pr, pg, ret_norm_gain, wao, wro, wout)
    x2 = _mlp_fwd(x1, g_mlp, wup, wdown)

    dx2, de, dz, hp, loss_p, dg_fin, dg_ple = _ple_loss(x2, ps, tgt, g_ple, g_fin, wpg, wpg.T, wple)
    dx1, act, du, hm, dg_mlp = _mlp_bwd(x1, dx2, g_mlp, wup, wdown.T, wup.T)
    dpg, dao, dro, do, dry, drg, dg_gn = _merge_bwd(dx1, o, y2, pr, pg, ret_norm_gain, wao, wro,
                                                    wout.T, wao.T, wro.T)
    rdq, rdk, rdv, dlam = _ret_bwd(rqh, rkh, pr, dry, pst, tm_, tmw, tqd, tqdw, tkd, tkdw, tg, tgw, cb)

    ksplit = 2
    tkb = _tile(seq // 4, 512)
    dot_ = do.reshape(seq, A_H, A_HD).transpose(1, 2, 0)
    dk2, dv2, dqt = _attn_bwd(qpad, _pad_heads(_bf(do)), qt8, ot, dot_, lse, kh, vh, _chunk_t(kh, tkb),
                              tq, tkb, ksplit)
    dqh = _heads_to_rows(jnp.sum(dqt, axis=0))
    dpa, dpr, dg_q, dg_k = _qk_prep_bwd(pa, dqh, dk2, dv2, rdq, rdk, rdv, drg, gq, gk, seg, ca, sa, cr, sr)
    grad_x, dg_mix = _in_proj_bwd(xs, dx1, g_mix, dpa, dpr, dpg, win.T)

    wg = {
        "w_in": jnp.concatenate([_wgrad(h, dpa, "wgrad_in_a"), _wgrad(h, dpr, "wgrad_in_r"),
                                 _wgrad(h, dpg, "wgrad_in_g")], axis=1),
        "w_attn_o": _wgrad(o, dao, "wgrad_attn_o"),
        "w_ret_o": _wgrad(ret_in, dro, "wgrad_ret_o"),
        "w_out": _wgrad(merged, dx1, "wgrad_out"),
        "w_up": _wgrad(hm, du, "wgrad_up"),
        "w_down": _wgrad(act, dx2, "wgrad_down"),
        "w_ple_gate": _wgrad(hp, dz, "wgrad_ple_gate"),
        "w_ple": _wgrad(ps, de, "wgrad_ple"),
    }
    gpack = _pack_full_grads(wg)
    gpack_in = wg["w_in"].reshape(D, N_DEV, IN_SHARD).transpose(1, 0, 2)
    small = jnp.stack(
        [_row(dg_mix), _row(dg_mlp), _row(dg_ple), _row(dg_fin), _row(dg_gn), _row(dlam[:, 0, 0]),
         _row(loss_p[0, 0:1]), jnp.zeros((PACK_COLS,), F32)]
        + [_row(dg_q[0, hh * A_HD:(hh + 1) * A_HD]) for hh in range(A_H)]
        + [_row(dg_k[0, hh * A_HD:(hh + 1) * A_HD]) for hh in range(A_KV)]
        + [jnp.zeros((PACK_COLS,), F32)] * (SMALL_ROWS - 18))

    me = 4 * lax.axis_index("x") + 2 * lax.axis_index("y") + lax.axis_index("c")
    own_in = lax.dynamic_index_in_dim(gpack_in, me, axis=0, keepdims=False)
    own = lax.dynamic_index_in_dim(gpack, me, axis=0, keepdims=False)
    land_in, land, sland = _exchange_grads([_bf(gpack_in), _bf(gpack), small[None]])
    in_sh = _adamw_big(land_in, own_in, w_in[0], m_w_in[0], v_w_in[0], "adamw_w_in")
    g_sh, d_sh, m_sh, v_sh = _adamw_big(land, own, wshard, _pack_shards([args["m_" + n] for n in big_names]),
                                        _pack_shards([args["v_" + n] for n in big_names]), "adamw_shard")
    g_sm, d_sm, m_sm, v_sm, loss_row = _adamw_small(
        sland, _pack_small({n: args[n] for n in _SMALL}), _pack_small({n: args["m_" + n] for n in _SMALL}),
        _pack_small({n: args["v_" + n] for n in _SMALL}))

    names = ["mix_norm", "w_in", "attn_q_norm", "attn_k_norm", "ret_decay_logit", "ret_norm_gain", "w_attn_o",
             "w_ret_o", "w_out", "mlp_norm", "w_up", "w_down", "ple_norm", "w_ple_gate", "w_ple", "final_norm"]
    like = {n: args[n] for n in _SMALL}
    outs = [loss_row[0, 0], grad_x[None]]
    for big, sm, w_in_part in ((g_sh, g_sm, in_sh[0]), (d_sh, d_sm, in_sh[1]), (m_sh, m_sm, in_sh[2]),
                               (v_sh, v_sm, in_sh[3])):
        table = {**_unpack_shard(big), **_unpack_small(sm, like), "w_in": w_in_part[None]}
        outs += [table[n] for n in names]
    return tuple(outs)
```

```python
import functools

import jax
import jax.numpy as jnp
from jax import lax
from jax.experimental import pallas as pl
from jax.experimental.pallas import tpu as pltpu

F32 = jnp.float32
_MXU = jnp.bfloat16

D = 1024
PLE = 256
GRID_W = 64
A_HD = 64
A_H = 8
A_KV = 2
A_G = A_H // A_KV
AQ_W = A_H * A_HD
AKV_W = A_KV * A_HD
R_HD = 128
R_H = 4
R_W = R_H * R_HD
IN_W = AQ_W + 2 * AKV_W + 4 * R_W + 2 * D
PA_W = AQ_W + 2 * AKV_W
PR_W = 4 * R_W
PG_W = 2 * D
FF = 4 * D
CHUNK = 128
ROPE_THETA = 10000.0
EPS = 1e-6
GN_EPS = 1e-5
N_DEV = 8

LR, B1, B2, ADAM_EPS, WD, STEP = 0.001, 0.9, 0.999, 1e-08, 0.01, 10

LANES = 128
PACK_COLS = 1024
SMALL_ROWS = 24
HIGHEST = lax.Precision.HIGHEST


def _tile(n, pref):
    t = min(n, pref)
    assert n % t == 0, (n, t)
    return t


def _bf(a):
    return a.astype(_MXU)


def _mm(a, b):
    return jnp.dot(_bf(a), _bf(b), preferred_element_type=F32)


def _mm_nt(a, b):
    return lax.dot_general(_bf(a), _bf(b), (((1,), (1,)), ((), ())), preferred_element_type=F32)


def _mm_tn(a, b):
    return lax.dot_general(_bf(a), _bf(b), (((0,), (0,)), ((), ())), preferred_element_type=F32)


def _sigmoid(z):
    return 1.0 / (1.0 + jnp.exp(-z))


def _rms(x):
    r = lax.rsqrt(jnp.mean(x * x, axis=-1, keepdims=True) + EPS)
    return x * r, r


def _rms_bwd(n, r, gain, dy):
    dn = dy * gain
    return r * (dn - n * jnp.mean(dn * n, axis=-1, keepdims=True))


def _swap_halves(x, half):
    n = x.shape[-1]
    lane = lax.broadcasted_iota(jnp.int32, x.shape, x.ndim - 1)
    first = (lane % (2 * half)) < half
    return jnp.where(first, pltpu.roll(x, n - half, axis=1), pltpu.roll(x, half, axis=1))


def _rope(x, cos, sin, half):
    return x * cos + _swap_halves(x, half) * sin


def _cat(t, reps):
    return jnp.concatenate([t] * reps, axis=1)


def _full(shape):
    nd = len(shape)
    return pl.BlockSpec(shape, lambda *_: (0,) * nd)


def _rope_tables(seq):
    def tab(head_dim):
        n_axis = head_dim // 4
        freqs = ROPE_THETA ** (-jnp.arange(n_axis, dtype=F32) / n_axis)
        rows = seq // GRID_W
        row = jnp.repeat(jnp.arange(rows, dtype=F32), GRID_W)
        col = jnp.tile(jnp.arange(GRID_W, dtype=F32), rows)
        ang = jnp.concatenate([row[:, None] * freqs, col[:, None] * freqs], axis=-1)
        c, s = jnp.cos(ang), jnp.sin(ang)
        return jnp.concatenate([c, c], axis=-1), jnp.concatenate([-s, s], axis=-1)
    ca, sa = tab(A_HD)
    cr, sr = tab(R_HD)
    return jnp.tile(ca, (1, 2)), jnp.tile(sa, (1, 2)), cr, sr


def _seg_mean_matrix():
    i = jnp.arange(AQ_W) // A_HD
    return (i[:, None] == i[None, :]).astype(F32) / A_HD


def _mesh_pos():
    return lax.axis_index("x"), lax.axis_index("y"), lax.axis_index("c")


def _all_gather(shards):
    n = len(shards)

    def body(*refs):
        x_refs, out_refs = refs[:n], refs[n:2 * n]
        send_sems, recv_sems, local_sems = refs[2 * n:]
        x, y, c = _mesh_pos()
        me, sibling = (x, y, c), (x, y, 1 - c)
        chips = [(1 - x, y), (x, 1 - y), (1 - x, 1 - y)]

        def slot(t, px, py, pc):
            return out_refs[t].at[4 * px + 2 * py + pc]

        def copy(t, k, block, to, src=None):
            return pltpu.make_async_remote_copy(
                src_ref=slot(t, *block) if src is None else src, dst_ref=slot(t, *block),
                send_sem=send_sems.at[7 * t + k], recv_sem=recv_sems.at[7 * t + k],
                device_id=to, device_id_type=pl.DeviceIdType.MESH)

        mine = [pltpu.make_async_copy(x_refs[t], slot(t, *me), local_sems.at[t]) for t in range(n)]
        for cp in mine:
            cp.start()
        first = []
        for t in range(n):
            first.append(copy(t, 0, me, sibling, src=x_refs[t]))
            first += [copy(t, 1 + j, me, (*chip, c), src=x_refs[t]) for j, chip in enumerate(chips)]
        for cp in first:
            cp.start()
        passed = []
        for t in range(n):
            for j, chip in enumerate(chips):
                copy(t, 1 + j, (*chip, c), me).wait_recv()
                passed.append(copy(t, 4 + j, (*chip, c), sibling))
                passed[-1].start()
        for t in range(n):
            copy(t, 0, sibling, me).wait_recv()
            for j, chip in enumerate(chips):
                copy(t, 4 + j, (*chip, 1 - c), me).wait_recv()
        for cp in first + passed:
            cp.wait_send()
        for cp in mine:
            cp.wait()

    anyspec = pl.BlockSpec(memory_space=pl.ANY)
    return pl.pallas_call(
        body, name="all_gather_weights",
        out_shape=tuple(jax.ShapeDtypeStruct((N_DEV,) + s.shape, s.dtype) for s in shards),
        in_specs=[anyspec] * n, out_specs=(anyspec,) * n,
        scratch_shapes=[pltpu.SemaphoreType.DMA((7 * n,)), pltpu.SemaphoreType.DMA((7 * n,)),
                        pltpu.SemaphoreType.DMA((n,))],
    )(*shards)


def _exchange_grads(packs):
    n = len(packs)

    def body(*refs):
        g_refs, land_refs = refs[:n], refs[n:2 * n]
        send_sems, recv_sems, local_sems = refs[2 * n:]
        x, y, c = _mesh_pos()
        me = 4 * x + 2 * y + c

        def row(t, j):
            return g_refs[t].at[j if packs[t].shape[0] == N_DEV else 0]

        own = [pltpu.make_async_copy(row(t, me), land_refs[t].at[me], local_sems.at[t]) for t in range(n)]
        for cp in own:
            cp.start()
        sends = []
        for k in range(1, N_DEV):
            peer = (x ^ ((k >> 2) & 1), y ^ ((k >> 1) & 1), c ^ (k & 1))
            pidx = 4 * peer[0] + 2 * peer[1] + peer[2]
            for t in range(n):
                sends.append(pltpu.make_async_remote_copy(
                    src_ref=row(t, pidx), dst_ref=land_refs[t].at[me],
                    send_sem=send_sems.at[7 * t + k - 1], recv_sem=recv_sems.at[7 * t + k - 1],
                    device_id=peer, device_id_type=pl.DeviceIdType.MESH))
                sends[-1].start()
        for k in range(1, N_DEV):
            peer = (x ^ ((k >> 2) & 1), y ^ ((k >> 1) & 1), c ^ (k & 1))
            pidx = 4 * peer[0] + 2 * peer[1] + peer[2]
            for t in range(n):
                pltpu.make_async_remote_copy(
                    src_ref=row(t, me), dst_ref=land_refs[t].at[pidx],
                    send_sem=send_sems.at[7 * t + k - 1], recv_sem=recv_sems.at[7 * t + k - 1],
                    device_id=peer, device_id_type=pl.DeviceIdType.MESH).wait_recv()
        for cp in sends:
            cp.wait_send()
        for cp in own:
            cp.wait()

    anyspec = pl.BlockSpec(memory_space=pl.ANY)
    return pl.pallas_call(
        body, name="exchange_grads",
        out_shape=tuple(jax.ShapeDtypeStruct((N_DEV,) + g.shape[1:], g.dtype) for g in packs),
        in_specs=[anyspec] * n, out_specs=(anyspec,) * n,
        scratch_shapes=[pltpu.SemaphoreType.DMA((7 * n,)), pltpu.SemaphoreType.DMA((7 * n,)),
                        pltpu.SemaphoreType.DMA((n,))],
    )(*packs)


def _in_proj(x, gain, w):
    seq = x.shape[0]
    tm = _tile(seq, 256)

    def body(x_ref, g_ref, w_ref, pa_ref, pr_ref, pg_ref, h_ref):
        n, _ = _rms(x_ref[...])
        h = _bf(n * g_ref[...])
        h_ref[...] = h
        pa_ref[...] = _mm(h, w_ref[:, 0:PA_W])
        pr_ref[...] = _mm(h, w_ref[:, PA_W:PA_W + PR_W])
        pg_ref[...] = _mm(h, w_ref[:, PA_W + PR_W:IN_W])

    row = lambda w_: pl.BlockSpec((tm, w_), lambda i: (i, 0))
    return pl.pallas_call(
        body, name="in_proj", grid=(seq // tm,),
        in_specs=[row(D), _full((1, D)), _full((D, IN_W))],
        out_specs=(row(PA_W), row(PR_W), row(PG_W), row(D)),
        out_shape=(jax.ShapeDtypeStruct((seq, PA_W), F32), jax.ShapeDtypeStruct((seq, PR_W), F32),
                   jax.ShapeDtypeStruct((seq, PG_W), F32), jax.ShapeDtypeStruct((seq, D), _MXU)),
    )(x, gain, w)


def _qk_prep(pa, pr, gq, gk, seg, ca, sa, cr, sr):
    seq = pa.shape[0]
    tm = _tile(seq, 256)

    def body(pa_ref, pr_ref, gq_ref, gk_ref, seg_ref, ca_ref, sa_ref, cr_ref, sr_ref,
             qh_ref, kh_ref, v_ref, rq_ref, rk_ref):
        q = pa_ref[:, 0:AQ_W]
        k = pa_ref[:, AQ_W:AQ_W + AKV_W]
        v_ref[...] = _bf(pa_ref[:, AQ_W + AKV_W:PA_W])
        ca_, sa_ = ca_ref[...], sa_ref[...]
        msq = jnp.dot(q * q, seg_ref[...], precision=HIGHEST, preferred_element_type=F32)
        qn = q * lax.rsqrt(msq + EPS) * gq_ref[...]
        qh_ref[...] = _bf(_rope(qn, _cat(ca_, 4), _cat(sa_, 4), A_HD // 2) * (A_HD ** -0.5))
        msk = jnp.dot(k * k, seg_ref[0:AKV_W, 0:AKV_W], precision=HIGHEST, preferred_element_type=F32)
        kn = k * lax.rsqrt(msk + EPS) * gk_ref[...]
        kh_ref[...] = _bf(_rope(kn, ca_, sa_, A_HD // 2))
        cr_, sr_ = _cat(cr_ref[...], 4), _cat(sr_ref[...], 4)
        rq_ref[...] = _rope(pr_ref[:, 0:R_W], cr_, sr_, R_HD // 2) * (R_HD ** -0.5)
        rk_ref[...] = _rope(pr_ref[:, R_W:2 * R_W], cr_, sr_, R_HD // 2)

    row = lambda w_: pl.BlockSpec((tm, w_), lambda i: (i, 0))
    return pl.pallas_call(
        body, name="qk_prep", grid=(seq // tm,),
        in_specs=[row(PA_W), row(2 * R_W), _full((1, AQ_W)), _full((1, AKV_W)), _full((AQ_W, AQ_W)),
                  row(LANES), row(LANES), row(LANES), row(LANES)],
        out_specs=(row(AQ_W), row(AKV_W), row(AKV_W), row(R_W), row(R_W)),
        out_shape=(jax.ShapeDtypeStruct((seq, AQ_W), _MXU), jax.ShapeDtypeStruct((seq, AKV_W), _MXU),
                   jax.ShapeDtypeStruct((seq, AKV_W), _MXU), jax.ShapeDtypeStruct((seq, R_W), F32),
                   jax.ShapeDtypeStruct((seq, R_W), F32)),
    )(pa, pr, gq, gk, seg, ca, sa, cr, sr)


def _pad_heads(a):
    seq = a.shape[0]
    t = a.reshape(seq, A_H, A_HD).transpose(1, 0, 2)
    z = jnp.zeros((A_G, seq, A_HD), a.dtype)
    return jnp.concatenate([jnp.concatenate([t[:A_G], z], axis=-1),
                            jnp.concatenate([z, t[A_G:]], axis=-1)], axis=0)


def _chunk_t(a, tk):
    seq = a.shape[0]
    return a.reshape(seq // tk, tk, a.shape[1]).transpose(0, 2, 1)


def _heads_to_rows(t):
    return t.transpose(2, 0, 1).reshape(t.shape[2], AQ_W)


def _attn_fwd(qt8, k2, vta, tq, tk):
    seq = k2.shape[0]
    nck = seq // tk
    rows = A_G * tq
    vrows = vta.shape[2]
    rb = _tile(tk, 256)
    assert nck % 2 == 0, nck

    def body(qt_ref, k_ref, vt_ref, o_ref, lse_ref, m_sc, acc_sc, qtp_sc, s_a, s_b, p_a, p_b, al_a, al_b):
        g = pl.program_id(0)
        qtp_sc[...] = jnp.zeros_like(qtp_sc)
        qtp_sc[pl.ds(pl.multiple_of(g * A_HD, A_HD), A_HD), :] = jnp.concatenate(
            [qt_ref[a] for a in range(A_G)], axis=1)
        m_sc[...] = jnp.full((1, rows), -jnp.inf, F32)
        acc_sc[...] = jnp.zeros_like(acc_sc)

        def scores(c):
            kc = k_ref[pl.ds(pl.multiple_of(c * tk, tk), tk), :]
            return _mm(kc, qtp_sc[...])

        def stage(c, s_cur, s_nxt, p_cur, p_prv, al_cur, al_prv, first=False, last=False):
            if not last:
                s_nxt[...] = scores(c + 1)
            if not first:
                acc_sc[...] = al_prv[...] * acc_sc[...] + _mm(vt_ref[0, c - 1], p_prv[...])
            for b in range(rows // LANES):
                cs = slice(b * LANES, (b + 1) * LANES)
                m_old = m_sc[:, cs]
                m_new = m_old
                for r in range(0, tk, rb):
                    m_new = jnp.maximum(m_new, jnp.max(s_cur[r:r + rb, cs], axis=0, keepdims=True))
                for r in range(0, tk, rb):
                    p_cur[r:r + rb, cs] = _bf(jnp.exp(s_cur[r:r + rb, cs] - m_new))
                al_cur[:, cs] = jnp.exp(m_old - m_new)
                m_sc[:, cs] = m_new

        s_a[...] = scores(0)
        stage(0, s_a, s_b, p_a, p_b, al_a, al_b, first=True)

        def pair(j, carry):
            stage(2 * j + 1, s_b, s_a, p_b, p_a, al_b, al_a)
            stage(2 * j + 2, s_a, s_b, p_a, p_b, al_a, al_b)
            return carry

        lax.fori_loop(0, nck // 2 - 1, pair, 0)
        stage(nck - 1, s_b, s_a, p_b, p_a, al_b, al_a, last=True)
        acc = al_b[...] * acc_sc[...] + _mm(vt_ref[0, nck - 1], p_b[...])
        l = acc[A_HD:A_HD + 1, :]
        lse = m_sc[...] + jnp.log(l)
        out = acc[0:A_HD, :] * (1.0 / l)
        for a in range(A_G):
            o_ref[a] = out[:, a * tq:(a + 1) * tq]
            lse_ref[a] = lse[:, a * tq:(a + 1) * tq]

    return pl.pallas_call(
        body, name="attn_fwd", grid=(A_KV, seq // tq),
        in_specs=[pl.BlockSpec((A_G, A_HD, tq), lambda g, i: (g, 0, i)),
                  _full((seq, LANES)), pl.BlockSpec((1, nck, vrows, tk), lambda g, i: (g, 0, 0, 0))],
        out_specs=(pl.BlockSpec((A_G, A_HD, tq), lambda g, i: (g, 0, i)),
                   pl.BlockSpec((A_G, 1, tq), lambda g, i: (g, 0, i))),
        out_shape=(jax.ShapeDtypeStruct((A_H, A_HD, seq), F32), jax.ShapeDtypeStruct((A_H, 1, seq), F32)),
        scratch_shapes=[pltpu.VMEM((1, rows), F32), pltpu.VMEM((vrows, rows), F32), pltpu.VMEM((LANES, rows), _MXU),
                        pltpu.VMEM((tk, rows), F32), pltpu.VMEM((tk, rows), F32),
                        pltpu.VMEM((tk, rows), _MXU), pltpu.VMEM((tk, rows), _MXU),
                        pltpu.VMEM((1, rows), F32), pltpu.VMEM((1, rows), F32)],
    )(qt8, k2, vta)


def _attn_bwd(qpad, dopad, qt8, ot, dot_, lse, k2, v2, k2t, tq, tk, ksplit):
    seq = k2.shape[0]
    sh = seq // ksplit
    nck = sh // tk
    rows = A_G * tq
    rb = _tile(tk, 64)
    assert nck % 2 == 0, nck

    def body(q_ref, do_ref, qt_ref, ot_ref, dot_ref, lse_ref, k_ref, v_ref, kt_ref,
             dk_ref, dv_ref, dq_ref, dq_sc, qtp_sc, dotp_sc, s_a, s_b, dp_a, dp_b, p_a, p_b, ds_a, ds_b):
        g = pl.program_id(1)
        hrows = pl.ds(pl.multiple_of(g * A_HD, A_HD), A_HD)

        @pl.when((g == 0) & (pl.program_id(2) == 0))
        def _():
            dk_ref[...] = jnp.zeros_like(dk_ref)
            dv_ref[...] = jnp.zeros_like(dv_ref)

        lse_row = jnp.concatenate([lse_ref[a] for a in range(A_G)], axis=1)
        dd = jnp.concatenate([jnp.sum(ot_ref[a] * dot_ref[a], axis=0, keepdims=True)
                              for a in range(A_G)], axis=1)
        qtp_sc[...] = jnp.zeros_like(qtp_sc)
        dotp_sc[...] = jnp.zeros_like(dotp_sc)
        qtp_sc[hrows, :] = jnp.concatenate([qt_ref[a] for a in range(A_G)], axis=1)
        dotp_sc[hrows, :] = _bf(jnp.concatenate([dot_ref[a] for a in range(A_G)], axis=1))
        dq_sc[...] = jnp.zeros_like(dq_sc)

        def products(c, s_ref, dp_ref):
            sl = pl.ds(pl.multiple_of(c * tk, tk), tk)
            s_ref[...] = _mm(k_ref[sl, :], qtp_sc[...])
            dp_ref[...] = _mm(v_ref[sl, :], dotp_sc[...])

        def accumulate(c, p_ref, ds_ref):
            sl = pl.ds(pl.multiple_of(c * tk, tk), tk)
            dv_ref[sl, :] += _mm(p_ref[...], do_ref[...].reshape(rows, LANES))
            dk_ref[sl, :] += _mm(ds_ref[...], q_ref[...].reshape(rows, LANES))
            dq_sc[...] += _mm(kt_ref[c, hrows, :], ds_ref[...])

        def stage(c, s_cur, dp_cur, s_nxt, dp_nxt, p_cur, ds_cur, p_prv, ds_prv, first=False, last=False):
            if not last:
                products(c + 1, s_nxt, dp_nxt)
            if not first:
                accumulate(c - 1, p_prv, ds_prv)
            for r in range(0, tk, rb):
                p = jnp.exp(s_cur[r:r + rb, :] - lse_row)
                p_cur[r:r + rb, :] = _bf(p)
                ds_cur[r:r + rb, :] = _bf(p * (dp_cur[r:r + rb, :] - dd))

        products(0, s_a, dp_a)
        stage(0, s_a, dp_a, s_b, dp_b, p_a, ds_a, p_b, ds_b, first=True)

        def pair(j, carry):
            stage(2 * j + 1, s_b, dp_b, s_a, dp_a, p_b, ds_b, p_a, ds_a)
            stage(2 * j + 2, s_a, dp_a, s_b, dp_b, p_a, ds_a, p_b, ds_b)
            return carry

        lax.fori_loop(0, nck // 2 - 1, pair, 0)
        stage(nck - 1, s_b, dp_b, s_a, dp_a, p_b, ds_b, p_a, ds_a, last=True)
        accumulate(nck - 1, p_b, ds_b)
        for a in range(A_G):
            dq_ref[0, a] = dq_sc[:, a * tq:(a + 1) * tq]

    qspec = pl.BlockSpec((A_G, tq, LANES), lambda s, g, i: (g, i, 0))
    tspec = pl.BlockSpec((A_G, A_HD, tq), lambda s, g, i: (g, 0, i))
    kspec = pl.BlockSpec((sh, LANES), lambda s, g, i: (s, 0))
    big = lambda dt: pltpu.VMEM((tk, rows), dt)
    return pl.pallas_call(
        body, name="attn_bwd", grid=(ksplit, A_KV, seq // tq),
        in_specs=[qspec, qspec, tspec, tspec, tspec, pl.BlockSpec((A_G, 1, tq), lambda s, g, i: (g, 0, i)),
                  kspec, kspec, pl.BlockSpec((nck, LANES, tk), lambda s, g, i: (s, 0, 0))],
        out_specs=(kspec, kspec, pl.BlockSpec((1, A_G, A_HD, tq), lambda s, g, i: (s, g, 0, i))),
        out_shape=(jax.ShapeDtypeStruct((seq, LANES), F32), jax.ShapeDtypeStruct((seq, LANES), F32),
                   jax.ShapeDtypeStruct((ksplit, A_H, A_HD, seq), F32)),
        scratch_shapes=[pltpu.VMEM((A_HD, rows), F32), pltpu.VMEM((LANES, rows), _MXU), pltpu.VMEM((LANES, rows), _MXU),
                        big(F32), big(F32), big(F32), big(F32), big(_MXU), big(_MXU), big(_MXU), big(_MXU)],
    )(qpad, dopad, qt8, ot, dot_, lse, k2, v2, k2t)


def _ret_tables(zb):
    c = CHUNK

    def body(z_ref, m_ref, mw_ref, qd_ref, qdw_ref, kd_ref, kdw_ref, g_ref, gw_ref):
        fwd = pl.program_id(0) < R_H
        z = z_ref[0]
        lam = jnp.minimum(z, 0.0) - jnp.log(1.0 + jnp.exp(-jnp.abs(z)))
        i = lax.broadcasted_iota(jnp.int32, (c, c), 0).astype(F32)
        j = lax.broadcasted_iota(jnp.int32, (c, c), 1).astype(F32)
        diff = jnp.where(fwd, i - j, j - i)
        keep = diff >= jnp.where(fwd, 0.0, 1.0)
        dist = jnp.maximum(diff, 0.0)
        m = jnp.where(keep, jnp.exp(lam * dist), 0.0)
        m_ref[0] = m
        mw_ref[0] = m * dist
        fq = jnp.where(fwd, i + 1.0, c - i)
        qd = jnp.exp(lam * fq)
        qd_ref[0] = qd
        qdw_ref[0] = qd * fq
        fk = jnp.where(fwd, c - 1.0 - i, i)
        kd = jnp.exp(lam * fk)
        kd_ref[0] = kd
        kdw_ref[0] = kd * fk
        gdec = jnp.exp(lam * c)
        g_ref[0] = gdec
        gw_ref[0] = gdec * c

    big = pl.BlockSpec((1, c, c), lambda t: (t, 0, 0))
    vec = pl.BlockSpec((1, 1, LANES), lambda t: (t, 0, 0))
    bshape = jax.ShapeDtypeStruct((2 * R_H, c, c), F32)
    vshape = jax.ShapeDtypeStruct((2 * R_H, 1, LANES), F32)
    return pl.pallas_call(
        body, name="ret_tables", grid=(2 * R_H,), in_specs=[vec],
        out_specs=(big, big, big, big, big, big, vec, vec),
        out_shape=(bshape,) * 6 + (vshape, vshape),
    )(zb)


def _ret_fwd(rq, rk, pr, m, qd, kd, gdec, cb):
    seq = rq.shape[0]
    c = CHUNK
    ns = seq // (cb * c)

    def body(q_ref, k_ref, v_ref, m_ref, qd_ref, kd_ref, g_ref, y_ref, pst_ref, p_sc):
        d = pl.program_id(0)

        @pl.when(pl.program_id(1) == 0)
        def _():
            p_sc[...] = jnp.zeros_like(p_sc)

        def chunk(j, carry):
            cc = jnp.where(d == 0, j, cb - 1 - j)
            sl = pl.ds(pl.multiple_of(cc * c, c), c)
            heads = [slice(h * R_HD, (h + 1) * R_HD) for h in range(R_H)]
            qk = [_mm_nt(q_ref[sl, hs], k_ref[sl, hs]) for hs in heads]
            qp = [_mm(q_ref[sl, hs] * qd_ref[h], p_sc[h]) for h, hs in enumerate(heads)]
            kv = [_mm_tn(k_ref[sl, hs] * kd_ref[h], v_ref[sl, hs]) for h, hs in enumerate(heads)]
            for h, hs in enumerate(heads):
                p = p_sc[h]
                pst_ref[h, cc] = p
                y_ref[0, sl, hs] = _mm(qk[h] * m_ref[h], v_ref[sl, hs]) + qp[h]
                p_sc[h] = p * g_ref[h] + kv[h]
            return carry

        lax.fori_loop(0, cb, chunk, 0)

    def step(d, n):
        return d * (ns - 1 - n) + (1 - d) * n

    blk = lambda off: pl.BlockSpec((cb * c, R_W), lambda d, n: (step(d, n), off))
    big = pl.BlockSpec((R_H, c, c), lambda d, n: (d, 0, 0))
    vec = pl.BlockSpec((R_H, 1, LANES), lambda d, n: (d, 0, 0))
    return pl.pallas_call(
        body, name="ret_fwd", grid=(2, ns),
        in_specs=[blk(0), blk(0), blk(2), big, big, big, vec],
        out_specs=(pl.BlockSpec((1, cb * c, R_W), lambda d, n: (d, step(d, n), 0)),
                   pl.BlockSpec((R_H, cb, R_HD, R_HD), lambda d, n: (d, step(d, n), 0, 0))),
        out_shape=(jax.ShapeDtypeStruct((2, seq, R_W), F32),
                   jax.ShapeDtypeStruct((2 * R_H, seq // c, R_HD, R_HD), F32)),
        scratch_shapes=[pltpu.VMEM((R_H, R_HD, R_HD), F32)],
    )(rq, rk, pr, m, qd, kd, gdec)


def _ret_bwd(rq, rk, pr, dry, pst, m, mw, qd, qdw, kd, kdw, gdec, gw, cb):
    seq = rq.shape[0]
    c = CHUNK
    ns = seq // (cb * c)

    def body(q_ref, k_ref, v_ref, dy_ref, pst_ref, m_ref, mw_ref, qd_ref, qdw_ref, kd_ref, kdw_ref,
             g_ref, gw_ref, dq_ref, dk_ref, dv_ref, dlam_ref, r_sc, acc_sc, e_sc, g_sc):
        d = pl.program_id(0)
        n = pl.program_id(1)

        @pl.when(n == 0)
        def _():
            r_sc[...] = jnp.zeros_like(r_sc)
            acc_sc[...] = jnp.zeros_like(acc_sc)
            e_sc[...] = jnp.zeros_like(e_sc)
            g_sc[...] = jnp.zeros_like(g_sc)

        def chunk(j, carry):
            cc = jnp.where(d == 0, cb - 1 - j, j)
            sl = pl.ds(pl.multiple_of(cc * c, c), c)
            heads = [slice(h * R_HD, (h + 1) * R_HD) for h in range(R_H)]
            first = []
            for h, hs in enumerate(heads):
                q, k, v, dy = q_ref[sl, hs], k_ref[sl, hs], v_ref[sl, hs], dy_ref[sl, hs]
                r = r_sc[h]
                first.append((_mm_nt(q, k), _mm_nt(dy, v), _mm_nt(dy, pst_ref[h, cc]), _mm_nt(v, r),
                              _mm(k * kd_ref[h], r), _mm_tn(q * qd_ref[h], dy)))
            for h, hs in enumerate(heads):
                qk, ds, dyp, vr, kr, qdy = first[h]
                q, k, dy = q_ref[sl, hs], k_ref[sl, hs], dy_ref[sl, hs]
                r = r_sc[h]
                da = ds * m_ref[h]
                dv_ref[0, sl, hs] = _mm_tn(qk * m_ref[h], dy) + kr
                dq_ref[0, sl, hs] = _mm(da, k) + dyp * qd_ref[h]
                dk_ref[0, sl, hs] = _mm_tn(da, q) + vr * kd_ref[h]
                acc_sc[h] += dyp * q * qdw_ref[h] + vr * k * kdw_ref[h]
                e_sc[h] += ds * qk * mw_ref[h]
                g_sc[h] += r * pst_ref[h, cc]
                r_sc[h] = r * g_ref[h] + qdy
            return carry

        lax.fori_loop(0, cb, chunk, 0)

        @pl.when(n == ns - 1)
        def _():
            for h in range(R_H):
                tot = jnp.sum(jnp.sum(acc_sc[h] + e_sc[h] + g_sc[h] * gw_ref[h], axis=0, keepdims=True),
                              axis=1, keepdims=True)
                dlam_ref[h] = jnp.broadcast_to(tot, (1, LANES))

    def step(d, n):
        return d * n + (1 - d) * (ns - 1 - n)

    blk = lambda off: pl.BlockSpec((cb * c, R_W), lambda d, n: (step(d, n), off))
    big = pl.BlockSpec((R_H, c, c), lambda d, n: (d, 0, 0))
    vec = pl.BlockSpec((R_H, 1, LANES), lambda d, n: (d, 0, 0))
    out = pl.BlockSpec((1, cb * c, R_W), lambda d, n: (d, step(d, n), 0))
    oshape = jax.ShapeDtypeStruct((2, seq, R_W), F32)
    sq = pltpu.VMEM((R_H, R_HD, R_HD), F32)
    return pl.pallas_call(
        body, name="ret_bwd", grid=(2, ns),
        in_specs=[blk(0), blk(0), blk(2), blk(0),
                  pl.BlockSpec((R_H, cb, R_HD, R_HD), lambda d, n: (d, step(d, n), 0, 0)),
                  big, big, big, big, big, big, vec, vec],
        out_specs=(out, out, out, vec),
        out_shape=(oshape, oshape, oshape, jax.ShapeDtypeStruct((2 * R_H, 1, LANES), F32)),
        scratch_shapes=[sq, sq, sq, sq],
    )(rq, rk, pr, dry, pst, m, mw, qd, qdw, kd, kdw, gdec, gw)


def _group_norm(ry):
    yn, rs = [], []
    for h in range(R_H):
        s = ry[:, h * R_HD:(h + 1) * R_HD]
        mu = jnp.mean(s, axis=-1, keepdims=True)
        cen = s - mu
        r = lax.rsqrt(jnp.mean(cen * cen, axis=-1, keepdims=True) + GN_EPS)
        yn.append(cen * r)
        rs.append(r)
    return yn, rs


def _merge_fwd(x, o, y2, pr, pg, gain_r, wao, wro, wout):
    seq = x.shape[0]
    tm = _tile(seq, 256)

    def body(x_ref, o_ref, yf_ref, yb_ref, rg_ref, ga_ref, gr_ref, gn_ref, wao_ref, wro_ref, wout_ref,
             x1_ref, mg_ref, ri_ref):
        yn, _ = _group_norm(yf_ref[0] + yb_ref[0])
        rg = rg_ref[...]
        ret_in = jnp.concatenate(yn, axis=1) * gn_ref[...] * (rg * _sigmoid(rg))
        ri_ref[...] = _bf(ret_in)
        attn_out = _mm(o_ref[...], wao_ref[...])
        ret_out = _mm(ret_in, wro_ref[...])
        merged = _sigmoid(ga_ref[...]) * attn_out + _sigmoid(gr_ref[...]) * ret_out
        mg_ref[...] = _bf(merged)
        x1_ref[...] = x_ref[...] + _mm(merged, wout_ref[...])

    row = lambda w_, j=0: pl.BlockSpec((tm, w_), lambda i: (i, j))
    ydir = lambda d: pl.BlockSpec((1, tm, R_W), lambda i: (d, i, 0))
    return pl.pallas_call(
        body, name="merge_fwd", grid=(seq // tm,),
        in_specs=[row(D), row(AQ_W), ydir(0), ydir(1), row(R_W, 3), row(D, 0), row(D, 1),
                  _full((1, R_W)), _full((AQ_W, D)), _full((R_W, D)), _full((D, D))],
        out_specs=(row(D), row(D), row(R_W)),
        out_shape=(jax.ShapeDtypeStruct((seq, D), F32), jax.ShapeDtypeStruct((seq, D), _MXU),
                   jax.ShapeDtypeStruct((seq, R_W), _MXU)),
    )(x, o, y2, y2, pr, pg, pg, gain_r, wao, wro, wout)


def _mlp_fwd(x1, gain, wup, wdown):
    seq = x1.shape[0]
    tm = _tile(seq, 512)
    fc = 512
    nfc = FF // fc

    def body(x_ref, g_ref, wu_ref, wd_ref, x2_ref, hm_sc, acc_sc):
        c = pl.program_id(1)

        @pl.when(c == 0)
        def _():
            n, _ = _rms(x_ref[...])
            hm_sc[...] = _bf(n * g_ref[...])
            acc_sc[...] = jnp.zeros_like(acc_sc)

        halves = (slice(0, fc // 2), slice(fc // 2, fc))
        ups = [jnp.maximum(_mm(hm_sc[...], wu_ref[:, hs]), 0.0) for hs in halves]
        acc_sc[...] += _mm(ups[0] * ups[0], wd_ref[halves[0], :]) + _mm(ups[1] * ups[1], wd_ref[halves[1], :])

        @pl.when(c == nfc - 1)
        def _():
            x2_ref[...] = x_ref[...] + acc_sc[...]

    return pl.pallas_call(
        body, name="mlp_fwd", grid=(seq // tm, nfc),
        in_specs=[pl.BlockSpec((tm, D), lambda i, c: (i, 0)), pl.BlockSpec((1, D), lambda i, c: (0, 0)),
                  pl.BlockSpec((D, fc), lambda i, c: (0, c)), pl.BlockSpec((fc, D), lambda i, c: (c, 0))],
        out_specs=pl.BlockSpec((tm, D), lambda i, c: (i, 0)),
        out_shape=jax.ShapeDtypeStruct((seq, D), F32),
        scratch_shapes=[pltpu.VMEM((tm, D), _MXU), pltpu.VMEM((tm, D), F32)],
    )(x1, gain, wup, wdown)


def _ple_loss(x2, p, tgt, g_ple, g_fin, wpg, wpgt, wple):
    seq = x2.shape[0]
    tm = _tile(seq, 256)

    def body(x2_ref, p_ref, t_ref, gp_ref, gf_ref, wpg_ref, wpgt_ref, wple_ref,
             dx2_ref, de_ref, dz_ref, hp_ref, loss_ref, dgf_ref, dgp_ref):
        @pl.when(pl.program_id(0) == 0)
        def _():
            loss_ref[...] = jnp.zeros_like(loss_ref)
            dgf_ref[...] = jnp.zeros_like(dgf_ref)
            dgp_ref[...] = jnp.zeros_like(dgp_ref)

        x2 = x2_ref[...]
        gp, gf = gp_ref[...], gf_ref[...]
        n2, r2 = _rms(x2)
        hp = _bf(n2 * gp)
        hp_ref[...] = hp
        gate = _sigmoid(_mm(hp, wpg_ref[...]))
        e = _mm(p_ref[...], wple_ref[...])
        x3 = x2 + gate * e
        n3, r3 = _rms(x3)
        diff = n3 * gf - t_ref[...]
        row_loss = jnp.mean(diff * diff, axis=-1, keepdims=True)
        loss_ref[...] += 0.5 * jnp.sum(row_loss, axis=0, keepdims=True)
        dy = diff * (1.0 / D)
        dgf_ref[...] += jnp.sum(dy * n3, axis=0, keepdims=True)
        dx3 = _rms_bwd(n3, r3, gf, dy)
        de_ref[...] = _bf(dx3 * gate)
        dz = dx3 * e * gate * (1.0 - gate)
        dz_ref[...] = _bf(dz)
        dhp = _mm(dz, wpgt_ref[...])
        dgp_ref[...] += jnp.sum(dhp * n2, axis=0, keepdims=True)
        dx2_ref[...] = dx3 + _rms_bwd(n2, r2, gp, dhp)

    row = lambda w_: pl.BlockSpec((tm, w_), lambda i: (i, 0))
    act = lambda dt: jax.ShapeDtypeStruct((seq, D), dt)
    return pl.pallas_call(
        body, name="ple_loss", grid=(seq // tm,),
        in_specs=[row(D), row(PLE), row(D), _full((1, D)), _full((1, D)),
                  _full((D, D)), _full((D, D)), _full((PLE, D))],
        out_specs=(row(D), row(D), row(D), row(D), _full((1, LANES)), _full((1, D)), _full((1, D))),
        out_shape=(act(F32), act(_MXU), act(_MXU), act(_MXU), jax.ShapeDtypeStruct((1, LANES), F32),
                   jax.ShapeDtypeStruct((1, D), F32), jax.ShapeDtypeStruct((1, D), F32)),
    )(x2, p, tgt, g_ple, g_fin, wpg, wpgt, wple)


def _mlp_bwd(x1, dx2, gain, wup, wdownt, wupt):
    seq = x1.shape[0]
    tm = _tile(seq, 512)
    fc = 512
    nfc = FF // fc

    def body(x_ref, dx2_ref, g_ref, wu_ref, wdt_ref, wut_ref,
             dx1_ref, a_ref, du_ref, hm_ref, dg_ref, dhm_sc):
        i = pl.program_id(0)
        c = pl.program_id(1)

        @pl.when((i == 0) & (c == 0))
        def _():
            dg_ref[...] = jnp.zeros_like(dg_ref)

        @pl.when(c == 0)
        def _():
            n, _ = _rms(x_ref[...])
            hm_ref[...] = _bf(n * g_ref[...])
            dhm_sc[...] = jnp.zeros_like(dhm_sc)

        halves = (slice(0, fc // 2), slice(fc // 2, fc))
        ups = [jnp.maximum(_mm(hm_ref[...], wu_ref[:, hs]), 0.0) for hs in halves]
        das = [_mm(dx2_ref[...], wdt_ref[:, hs]) for hs in halves]
        part = None
        for u, da, hs in zip(ups, das, halves):
            a_ref[:, hs] = _bf(u * u)
            du = _bf(da * (2.0 * u))
            du_ref[:, hs] = du
            t = _mm(du, wut_ref[hs, :])
            part = t if part is None else part + t
        dhm_sc[...] += part

        @pl.when(c == nfc - 1)
        def _():
            n, r = _rms(x_ref[...])
            dhm = dhm_sc[...]
            dg_ref[...] += jnp.sum(dhm * n, axis=0, keepdims=True)
            dx1_ref[...] = dx2_ref[...] + _rms_bwd(n, r, g_ref[...], dhm)

    rowd = pl.BlockSpec((tm, D), lambda i, c: (i, 0))
    rowf = pl.BlockSpec((tm, fc), lambda i, c: (i, c))
    return pl.pallas_call(
        body, name="mlp_bwd", grid=(seq // tm, nfc),
        in_specs=[rowd, rowd, pl.BlockSpec((1, D), lambda i, c: (0, 0)),
                  pl.BlockSpec((D, fc), lambda i, c: (0, c)), pl.BlockSpec((D, fc), lambda i, c: (0, c)),
                  pl.BlockSpec((fc, D), lambda i, c: (c, 0))],
        out_specs=(rowd, rowf, rowf, rowd, pl.BlockSpec((1, D), lambda i, c: (0, 0))),
        out_shape=(jax.ShapeDtypeStruct((seq, D), F32), jax.ShapeDtypeStruct((seq, FF), _MXU),
                   jax.ShapeDtypeStruct((seq, FF), _MXU), jax.ShapeDtypeStruct((seq, D), _MXU),
                   jax.ShapeDtypeStruct((1, D), F32)),
        scratch_shapes=[pltpu.VMEM((tm, D), F32)],
    )(x1, dx2, gain, wup, wdownt, wupt)


def _merge_bwd(dx1, o, y2, pr, pg, gain_r, wao, wro, woutt, waot, wrot):
    seq = dx1.shape[0]
    tm = _tile(seq, 256)

    def body(dx1_ref, o_ref, yf_ref, yb_ref, rg_ref, ga_ref, gr_ref, gn_ref, wao_ref, wro_ref,
             woutt_ref, waot_ref, wrot_ref,
             dpg_ref, dao_ref, dro_ref, do_ref, dry_ref, drg_ref, dgn_ref):
        @pl.when(pl.program_id(0) == 0)
        def _():
            dgn_ref[...] = jnp.zeros_like(dgn_ref)

        yn_l, rs_l = _group_norm(yf_ref[0] + yb_ref[0])
        yn = jnp.concatenate(yn_l, axis=1)
        rg = rg_ref[...]
        gn = gn_ref[...]
        sg = _sigmoid(rg)
        sil = rg * sg
        ret_in = yn * gn * sil
        attn_out = _mm(o_ref[...], wao_ref[...])
        ret_out = _mm(ret_in, wro_ref[...])
        sa = _sigmoid(ga_ref[...])
        sr = _sigmoid(gr_ref[...])
        dm = _mm(dx1_ref[...], woutt_ref[...])
        dpg_ref[:, 0:D] = _bf(dm * attn_out * sa * (1.0 - sa))
        dpg_ref[:, D:2 * D] = _bf(dm * ret_out * sr * (1.0 - sr))
        dao = _bf(dm * sa)
        dro = _bf(dm * sr)
        dao_ref[...] = dao
        dro_ref[...] = dro
        do_ref[...] = _mm(dao, waot_ref[...])
        dri = _mm(dro, wrot_ref[...])
        dgn_ref[...] += jnp.sum(dri * yn * sil, axis=0, keepdims=True)
        drg_ref[...] = _bf(dri * yn * gn * (sg * (1.0 + rg * (1.0 - sg))))
        dyn = dri * gn * sil
        dry = []
        for h in range(R_H):
            dh = dyn[:, h * R_HD:(h + 1) * R_HD]
            dry.append(rs_l[h] * (dh - jnp.mean(dh, axis=-1, keepdims=True)
                                  - yn_l[h] * jnp.mean(dh * yn_l[h], axis=-1, keepdims=True)))
        dry_ref[...] = jnp.concatenate(dry, axis=1)

    row = lambda w_, j=0: pl.BlockSpec((tm, w_), lambda i: (i, j))
    ydir = lambda d: pl.BlockSpec((1, tm, R_W), lambda i: (d, i, 0))
    return pl.pallas_call(
        body, name="merge_bwd", grid=(seq // tm,),
        in_specs=[row(D), row(AQ_W), ydir(0), ydir(1), row(R_W, 3), row(D, 0), row(D, 1),
                  _full((1, R_W)), _full((AQ_W, D)), _full((R_W, D)), _full((D, D)),
                  _full((D, AQ_W)), _full((D, R_W))],
        out_specs=(row(PG_W), row(D), row(D), row(AQ_W), row(R_W), row(R_W), _full((1, R_W))),
        out_shape=(jax.ShapeDtypeStruct((seq, PG_W), _MXU), jax.ShapeDtypeStruct((seq, D), _MXU),
                   jax.ShapeDtypeStruct((seq, D), _MXU), jax.ShapeDtypeStruct((seq, AQ_W), F32),
                   jax.ShapeDtypeStruct((seq, R_W), F32), jax.ShapeDtypeStruct((seq, R_W), _MXU),
                   jax.ShapeDtypeStruct((1, R_W), F32)),
    )(dx1, o, y2, y2, pr, pg, pg, gain_r, wao, wro, woutt, waot, wrot)


def _qk_prep_bwd(pa, dqh, dk2, dv2, rdq, rdk, rdv, drg, gq, gk, seg, ca, sa, cr, sr):
    seq = pa.shape[0]
    tm = _tile(seq, 256)

    def body(pa_ref, dqh_ref, dk2_ref, dv2_ref, rdqf_ref, rdqb_ref, rdkf_ref, rdkb_ref, rdvf_ref, rdvb_ref,
             drg_ref, gq_ref, gk_ref, seg_ref, ca_ref, sa_ref, cr_ref, sr_ref,
             dpa_ref, dpr_ref, dgq_ref, dgk_ref):
        @pl.when(pl.program_id(0) == 0)
        def _():
            dgq_ref[...] = jnp.zeros_like(dgq_ref)
            dgk_ref[...] = jnp.zeros_like(dgk_ref)

        ca_, sa_ = ca_ref[...], sa_ref[...]

        def norm_bwd(raw, gain, dy, segm, dg_ref):
            msq = jnp.dot(raw * raw, segm, precision=HIGHEST, preferred_element_type=F32)
            r = lax.rsqrt(msq + EPS)
            n = raw * r
            dg_ref[...] += jnp.sum(dy * n, axis=0, keepdims=True)
            dn = dy * gain
            return r * (dn - n * jnp.dot(dn * n, segm, precision=HIGHEST, preferred_element_type=F32))

        dqn = _rope(dqh_ref[...] * (A_HD ** -0.5), _cat(ca_, 4), -_cat(sa_, 4), A_HD // 2)
        dpa_ref[:, 0:AQ_W] = _bf(norm_bwd(pa_ref[:, 0:AQ_W], gq_ref[...], dqn, seg_ref[...], dgq_ref))
        dkn = _rope(dk2_ref[...], ca_, -sa_, A_HD // 2)
        dpa_ref[:, AQ_W:AQ_W + AKV_W] = _bf(norm_bwd(pa_ref[:, AQ_W:AQ_W + AKV_W], gk_ref[...], dkn,
                                                     seg_ref[0:AKV_W, 0:AKV_W], dgk_ref))
        dpa_ref[:, AQ_W + AKV_W:PA_W] = _bf(dv2_ref[...])
        cr_, sr_ = _cat(cr_ref[...], 4), -_cat(sr_ref[...], 4)
        dpr_ref[:, 0:R_W] = _bf(_rope((rdqf_ref[0] + rdqb_ref[0]) * (R_HD ** -0.5), cr_, sr_, R_HD // 2))
        dpr_ref[:, R_W:2 * R_W] = _bf(_rope(rdkf_ref[0] + rdkb_ref[0], cr_, sr_, R_HD // 2))
        dpr_ref[:, 2 * R_W:3 * R_W] = _bf(rdvf_ref[0] + rdvb_ref[0])
        dpr_ref[:, 3 * R_W:4 * R_W] = drg_ref[...]

    row = lambda w_: pl.BlockSpec((tm, w_), lambda i: (i, 0))
    ydir = lambda d: pl.BlockSpec((1, tm, R_W), lambda i: (d, i, 0))
    return pl.pallas_call(
        body, name="qk_prep_bwd", grid=(seq // tm,),
        in_specs=[row(PA_W), row(AQ_W), row(AKV_W), row(AKV_W), ydir(0), ydir(1), ydir(0), ydir(1),
                  ydir(0), ydir(1), row(R_W), _full((1, AQ_W)), _full((1, AKV_W)), _full((AQ_W, AQ_W)),
                  row(LANES), row(LANES), row(LANES), row(LANES)],
        out_specs=(row(PA_W), row(PR_W), _full((1, AQ_W)), _full((1, AKV_W))),
        out_shape=(jax.ShapeDtypeStruct((seq, PA_W), _MXU), jax.ShapeDtypeStruct((seq, PR_W), _MXU),
                   jax.ShapeDtypeStruct((1, AQ_W), F32), jax.ShapeDtypeStruct((1, AKV_W), F32)),
    )(pa, dqh, dk2, dv2, rdq, rdq, rdk, rdk, rdv, rdv, drg, gq, gk, seg, ca, sa, cr, sr)


def _in_proj_bwd(x, dx1, gain, dpa, dpr, dpg, wint):
    seq = x.shape[0]
    tm = _tile(seq, 256)

    def body(x_ref, dx1_ref, g_ref, dpa_ref, dpr_ref, dpg_ref, wt_ref, dx_ref, dg_ref):
        @pl.when(pl.program_id(0) == 0)
        def _():
            dg_ref[...] = jnp.zeros_like(dg_ref)

        dh = (_mm(dpa_ref[...], wt_ref[0:PA_W, :]) + _mm(dpr_ref[...], wt_ref[PA_W:PA_W + PR_W, :])
              + _mm(dpg_ref[...], wt_ref[PA_W + PR_W:IN_W, :]))
        n, r = _rms(x_ref[...])
        dg_ref[...] += jnp.sum(dh * n, axis=0, keepdims=True)
        dx_ref[...] = dx1_ref[...] + _rms_bwd(n, r, g_ref[...], dh)

    row = lambda w_: pl.BlockSpec((tm, w_), lambda i: (i, 0))
    return pl.pallas_call(
        body, name="in_proj_bwd", grid=(seq // tm,),
        in_specs=[row(D), row(D), _full((1, D)), row(PA_W), row(PR_W), row(PG_W), _full((IN_W, D))],
        out_specs=(row(D), _full((1, D))),
        out_shape=(jax.ShapeDtypeStruct((seq, D), F32), jax.ShapeDtypeStruct((1, D), F32)),
    )(x, dx1, gain, dpa, dpr, dpg, wint)


def _wgrad(a, b, name):
    seq, m = a.shape
    n = b.shape[1]
    tm, tn, ts = _tile(m, 512), _tile(n, 1024), _tile(seq, 2048)
    ns = seq // ts

    def body(a_ref, b_ref, o_ref):
        @pl.when(pl.program_id(2) == 0)
        def _():
            o_ref[...] = jnp.zeros_like(o_ref)

        o_ref[...] += _mm_tn(a_ref[...], b_ref[...])

    return pl.pallas_call(
        body, name=name, grid=(m // tm, n // tn, ns),
        in_specs=[pl.BlockSpec((ts, tm), lambda i, j, s: (s, i)), pl.BlockSpec((ts, tn), lambda i, j, s: (s, j))],
        out_specs=pl.BlockSpec((tm, tn), lambda i, j, s: (i, j)),
        out_shape=jax.ShapeDtypeStruct((m, n), F32),
    )(a, b)


def _adamw_math(w, g, m, v):
    m = B1 * m + (1.0 - B1) * g
    v = B2 * v + (1.0 - B2) * (g * g)
    m_hat = m / (1.0 - B1 ** STEP)
    v_hat = v / (1.0 - B2 ** STEP)
    delta = -LR * (m_hat / (jnp.sqrt(v_hat) + ADAM_EPS) + WD * w)
    return delta, m, v


def _adamw_big(land, own, w, m, v, name):
    rws, cols = w.shape
    tr = next(t for t in range(min(rws, 288), 0, -8) if rws % t == 0)

    def body(l_ref, o_ref, w_ref, m_ref, v_ref, g_ref, d_ref, nm_ref, nv_ref):
        x, y, c = _mesh_pos()
        me = 4 * x + 2 * y + c
        g = o_ref[...]
        for j in range(N_DEV):
            g = g + jnp.where(me == j, 0.0, l_ref[j].astype(F32))
        g_ref[...] = g
        d_ref[...], nm_ref[...], nv_ref[...] = _adamw_math(w_ref[...], g, m_ref[...], v_ref[...])

    row = pl.BlockSpec((tr, cols), lambda i: (i, 0))
    shp = jax.ShapeDtypeStruct((rws, cols), F32)
    return pl.pallas_call(
        body, name=name, grid=(rws // tr,),
        in_specs=[pl.BlockSpec((N_DEV, tr, cols), lambda i: (0, i, 0)), row, row, row, row],
        out_specs=(row, row, row, row), out_shape=(shp, shp, shp, shp),
    )(land, own, w, m, v)


def _adamw_small(sland, w, m, v):
    def body(l_ref, w_ref, m_ref, v_ref, g_ref, d_ref, nm_ref, nv_ref, loss_ref):
        s = l_ref[0]
        for j in range(1, N_DEV):
            s = s + l_ref[j]
        w = w_ref[...]
        gq = s[8:9]
        for h in range(1, A_H):
            gq = gq + s[8 + h:9 + h]
        gk = s[16:17] + s[17:18]
        gdec = s[5:6] * _sigmoid(-w[5:6])
        g = jnp.concatenate([s[0:5], gdec, gq, gk], axis=0)
        g_ref[...] = g
        d_ref[...], nm_ref[...], nv_ref[...] = _adamw_math(w, g, m_ref[...], v_ref[...])
        loss_ref[...] = s[6:7, 0:LANES]

    shp = jax.ShapeDtypeStruct((8, PACK_COLS), F32)
    return pl.pallas_call(
        body, name="adamw_small",
        out_shape=(shp, shp, shp, shp, jax.ShapeDtypeStruct((1, LANES), F32)),
    )(sland, w, m, v)


_BIG = (("w_attn_o", AQ_W, D, 1), ("w_ret_o", R_W, D, 1), ("w_out", D, D, 0),
        ("w_up", D, FF, 1), ("w_down", FF, D, 0), ("w_ple_gate", D, D, 0), ("w_ple", PLE, D, 1))
IN_SHARD = IN_W // N_DEV
_SMALL = ("mix_norm", "mlp_norm", "ple_norm", "final_norm", "ret_norm_gain", "ret_decay_logit",
          "attn_q_norm", "attn_k_norm")


def _shard_shape(rows, cols, axis):
    return (rows // N_DEV, cols) if axis == 0 else (rows, cols // N_DEV)


def _pack_shards(shards):
    flat = jnp.concatenate([s.reshape(-1) for s in shards])
    return flat.reshape(-1, PACK_COLS)


def _unpack_gathered(gathered):
    flat = gathered.reshape(N_DEV, -1)
    out, off = {}, 0
    for name, rows, cols, axis in _BIG:
        sr, sc = _shard_shape(rows, cols, axis)
        blk = flat[:, off:off + sr * sc].reshape(N_DEV, sr, sc)
        off += sr * sc
        out[name] = blk.reshape(rows, cols) if axis == 0 else blk.transpose(1, 0, 2).reshape(rows, cols)
    return out


def _pack_full_grads(grads):
    parts = []
    for name, rows, cols, axis in _BIG:
        sr, sc = _shard_shape(rows, cols, axis)
        g = grads[name]
        blk = g.reshape(N_DEV, sr, sc) if axis == 0 else g.reshape(rows, N_DEV, sc).transpose(1, 0, 2)
        parts.append(blk.reshape(N_DEV, -1))
    flat = jnp.concatenate(parts, axis=1)
    return flat.reshape(N_DEV, -1, PACK_COLS)


def _unpack_shard(packed):
    flat = packed.reshape(-1)
    out, off = {}, 0
    for name, rows, cols, axis in _BIG:
        sr, sc = _shard_shape(rows, cols, axis)
        out[name] = flat[off:off + sr * sc].reshape(1, sr, sc)
        off += sr * sc
    return out


def _pack_small(vals):
    rows = [jnp.pad(vals[n].reshape(-1), (0, PACK_COLS - vals[n].size)) for n in _SMALL]
    return jnp.stack(rows)


def _unpack_small(packed, like):
    return {n: packed[i, :like[n].size].reshape(like[n].shape) for i, n in enumerate(_SMALL)}


def _row(v):
    return jnp.pad(v.reshape(-1), (0, PACK_COLS - v.size))


def kernel(x, p, mix_norm, w_in, attn_q_norm, attn_k_norm, ret_decay_logit, ret_norm_gain, w_attn_o, w_ret_o, w_out, mlp_norm, w_up, w_down, ple_norm, w_ple_gate, w_ple, final_norm, loss_target, m_mix_norm, m_w_in, m_attn_q_norm, m_attn_k_norm, m_ret_decay_logit, m_ret_norm_gain, m_w_attn_o, m_w_ret_o, m_w_out, m_mlp_norm, m_w_up, m_w_down, m_ple_norm, m_w_ple_gate, m_w_ple, m_final_norm, v_mix_norm, v_w_in, v_attn_q_norm, v_attn_k_norm, v_ret_decay_logit, v_ret_norm_gain, v_w_attn_o, v_w_ret_o, v_w_out, v_mlp_norm, v_w_up, v_w_down, v_ple_norm, v_w_ple_gate, v_w_ple, v_final_norm):
    args = dict(locals())
    seq = x.shape[1]
    xs = x[0]
    ps = p[0, 0]
    tgt = loss_target[0]

    big_names = [b[0] for b in _BIG]
    wshard = _pack_shards([args[n] for n in big_names])
    win_g, rest_g = _all_gather([w_in[0].astype(_MXU), wshard.astype(_MXU)])
    win = win_g.transpose(1, 0, 2).reshape(D, IN_W)
    wfull = _unpack_gathered(rest_g)
    wao, wro, wout = wfull["w_attn_o"], wfull["w_ret_o"], wfull["w_out"]
    wup, wdown, wpg, wple = wfull["w_up"], wfull["w_down"], wfull["w_ple_gate"], wfull["w_ple"]

    g_mix, g_mlp, g_ple = mix_norm, mlp_norm, ple_norm
    g_fin = final_norm.reshape(1, D)
    gq = jnp.tile(attn_q_norm, (1, A_H))
    gk = jnp.tile(attn_k_norm, (1, A_KV))
    seg = _seg_mean_matrix()
    ca, sa, cr, sr = _rope_tables(seq)

    pa, pr, pg, h = _in_proj(xs, g_mix, win)
    qh, kh, vh, rqh, rkh = _qk_prep(pa, pr, gq, gk, seg, ca, sa, cr, sr)

    tq = _tile(seq, 128)
    tk = _tile(seq // 4, 1024)
    qpad = _pad_heads(qh)
    qt8 = qh.reshape(seq, A_H, A_HD).transpose(1, 2, 0)
    vta = jnp.stack([jnp.concatenate([_chunk_t(vh[:, g * A_HD:(g + 1) * A_HD], tk),
                                      jnp.ones((seq // tk, 16, tk), _MXU)], axis=1) for g in range(A_KV)])
    ot, lse = _attn_fwd(qt8, kh, vta, tq, tk)
    o = _heads_to_rows(ot)

    zb = jnp.broadcast_to(ret_decay_logit.reshape(2 * R_H, 1, 1), (2 * R_H, 1, LANES))
    tm_, tmw, tqd, tqdw, tkd, tkdw, tg, tgw = _ret_tables(zb)
    cb = _tile(seq // CHUNK, 8)
    y2, pst = _ret_fwd(rqh, rkh, pr, tm_, tqd, tkd, tg, cb)

    x1, merged, ret_in = _merge_fwd(xs, o, y2, pr, pg, ret_norm_gain, wao, wro, wout)
    x2 = _mlp_fwd(x1, g_mlp, wup, wdown)

    dx2, de, dz, hp, loss_p, dg_fin, dg_ple = _ple_loss(x2, ps, tgt, g_ple, g_fin, wpg, wpg.T, wple)
    dx1, act, du, hm, dg_mlp = _mlp_bwd(x1, dx2, g_mlp, wup, wdown.T, wup.T)
    dpg, dao, dro, do, dry, drg, dg_gn = _merge_bwd(dx1, o, y2, pr, pg, ret_norm_gain, wao, wro,
                                                    wout.T, wao.T, wro.T)
    rdq, rdk, rdv, dlam = _ret_bwd(rqh, rkh, pr, dry, pst, tm_, tmw, tqd, tqdw, tkd, tkdw, tg, tgw, cb)

    ksplit = 2
    tkb = _tile(seq // 4, 512)
    dot_ = do.reshape(seq, A_H, A_HD).transpose(1, 2, 0)
    dk2, dv2, dqt = _attn_bwd(qpad, _pad_heads(_bf(do)), qt8, ot, dot_, lse, kh, vh, _chunk_t(kh, tkb),
                              tq, tkb, ksplit)
    dqh = _heads_to_rows(jnp.sum(dqt, axis=0))
    dpa, dpr, dg_q, dg_k = _qk_prep_bwd(pa, dqh, dk2, dv2, rdq, rdk, rdv, drg, gq, gk, seg, ca, sa, cr, sr)
    grad_x, dg_mix = _in_proj_bwd(xs, dx1, g_mix, dpa, dpr, dpg, win.T)

    wg = {
        "w_in": jnp.concatenate([_wgrad(h, dpa, "wgrad_in_a"), _wgrad(h, dpr, "wgrad_in_r"),
                                 _wgrad(h, dpg, "wgrad_in_g")], axis=1),
        "w_attn_o": _wgrad(o, dao, "wgrad_attn_o"),
        "w_ret_o": _wgrad(ret_in, dro, "wgrad_ret_o"),
        "w_out": _wgrad(merged, dx1, "wgrad_out"),
        "w_up": _wgrad(hm, du, "wgrad_up"),
        "w_down": _wgrad(act, dx2, "wgrad_down"),
        "w_ple_gate": _wgrad(hp, dz, "wgrad_ple_gate"),
        "w_ple": _wgrad(ps, de, "wgrad_ple"),
    }
    gpack = _pack_full_grads(wg)
    gpack_in = wg["w_in"].reshape(D, N_DEV, IN_SHARD).transpose(1, 0, 2)
    small = jnp.stack(
        [_row(dg_mix), _row(dg_mlp), _row(dg_ple), _row(dg_fin), _row(dg_gn), _row(dlam[:, 0, 0]),
         _row(loss_p[0, 0:1]), jnp.zeros((PACK_COLS,), F32)]
        + [_row(dg_q[0, hh * A_HD:(hh + 1) * A_HD]) for hh in range(A_H)]
        + [_row(dg_k[0, hh * A_HD:(hh + 1) * A_HD]) for hh in range(A_KV)]
        + [jnp.zeros((PACK_COLS,), F32)] * (SMALL_ROWS - 18))

    me = 4 * lax.axis_index("x") + 2 * lax.axis_index("y") + lax.axis_index("c")
    own_in = lax.dynamic_index_in_dim(gpack_in, me, axis=0, keepdims=False)
    own = lax.dynamic_index_in_dim(gpack, me, axis=0, keepdims=False)
    land_in, land, sland = _exchange_grads([_bf(gpack_in), _bf(gpack), small[None]])
    in_sh = _adamw_big(land_in, own_in, w_in[0], m_w_in[0], v_w_in[0], "adamw_w_in")
    g_sh, d_sh, m_sh, v_sh = _adamw_big(land, own, wshard, _pack_shards([args["m_" + n] for n in big_names]),
                                        _pack_shards([args["v_" + n] for n in big_names]), "adamw_shard")
    g_sm, d_sm, m_sm, v_sm, loss_row = _adamw_small(
        sland, _pack_small({n: args[n] for n in _SMALL}), _pack_small({n: args["m_" + n] for n in _SMALL}),
        _pack_small({n: args["v_" + n] for n in _SMALL}))

    names = ["mix_norm", "w_in", "attn_q_norm", "attn_k_norm", "ret_decay_logit", "ret_norm_gain", "w_attn_o",
             "w_ret_o", "w_out", "mlp_norm", "w_up", "w_down", "ple_norm", "w_ple_gate", "w_ple", "final_norm"]
    like = {n: args[n] for n in _SMALL}
    outs = [loss_row[0, 0], grad_x[None]]
    for big, sm, w_in_part in ((g_sh, g_sm, in_sh[0]), (d_sh, d_sm, in_sh[1]), (m_sh, m_sm, in_sh[2]),
                               (v_sh, v_sm, in_sh[3])):
        table = {**_unpack_shard(big), **_unpack_small(sm, like), "w_in": w_in_part[None]}
        outs += [table[n] for n in names]
    return tuple(outs)
```

```python
import functools

import jax
import jax.numpy as jnp
from jax import lax
from jax.experimental import pallas as pl
from jax.experimental.pallas import tpu as pltpu

F32 = jnp.float32
_MXU = jnp.bfloat16

D = 1024
PLE = 256
GRID_W = 64
A_HD = 64
A_H = 8
A_KV = 2
A_G = A_H // A_KV
AQ_W = A_H * A_HD
AKV_W = A_KV * A_HD
R_HD = 128
R_H = 4
R_W = R_H * R_HD
IN_W = AQ_W + 2 * AKV_W + 4 * R_W + 2 * D
PA_W = AQ_W + 2 * AKV_W
PR_W = 4 * R_W
PG_W = 2 * D
FF = 4 * D
CHUNK = 128
ROPE_THETA = 10000.0
EPS = 1e-6
GN_EPS = 1e-5
N_DEV = 8

LR, B1, B2, ADAM_EPS, WD, STEP = 0.001, 0.9, 0.999, 1e-08, 0.01, 10

LANES = 128
PACK_COLS = 1024
SMALL_ROWS = 24
HIGHEST = lax.Precision.HIGHEST


def _tile(n, pref):
    t = min(n, pref)
    assert n % t == 0, (n, t)
    return t


def _bf(a):
    return a.astype(_MXU)


def _mm(a, b):
    return jnp.dot(_bf(a), _bf(b), preferred_element_type=F32)


def _mm_nt(a, b):
    return lax.dot_general(_bf(a), _bf(b), (((1,), (1,)), ((), ())), preferred_element_type=F32)


def _mm_tn(a, b):
    return lax.dot_general(_bf(a), _bf(b), (((0,), (0,)), ((), ())), preferred_element_type=F32)


def _sigmoid(z):
    return 1.0 / (1.0 + jnp.exp(-z))


def _rms(x):
    r = lax.rsqrt(jnp.mean(x * x, axis=-1, keepdims=True) + EPS)
    return x * r, r


def _rms_bwd(n, r, gain, dy):
    dn = dy * gain
    return r * (dn - n * jnp.mean(dn * n, axis=-1, keepdims=True))


def _swap_halves(x, half):
    n = x.shape[-1]
    lane = lax.broadcasted_iota(jnp.int32, x.shape, x.ndim - 1)
    first = (lane % (2 * half)) < half
    return jnp.where(first, pltpu.roll(x, n - half, axis=1), pltpu.roll(x, half, axis=1))


def _rope(x, cos, sin, half):
    return x * cos + _swap_halves(x, half) * sin


def _cat(t, reps):
    return jnp.concatenate([t] * reps, axis=1)


def _full(shape):
    nd = len(shape)
    return pl.BlockSpec(shape, lambda *_: (0,) * nd)


def _rope_tables(seq):
    def tab(head_dim):
        n_axis = head_dim // 4
        freqs = ROPE_THETA ** (-jnp.arange(n_axis, dtype=F32) / n_axis)
        rows = seq // GRID_W
        row = jnp.repeat(jnp.arange(rows, dtype=F32), GRID_W)
        col = jnp.tile(jnp.arange(GRID_W, dtype=F32), rows)
        ang = jnp.concatenate([row[:, None] * freqs, col[:, None] * freqs], axis=-1)
        c, s = jnp.cos(ang), jnp.sin(ang)
        return jnp.concatenate([c, c], axis=-1), jnp.concatenate([-s, s], axis=-1)
    ca, sa = tab(A_HD)
    cr, sr = tab(R_HD)
    return jnp.tile(ca, (1, 2)), jnp.tile(sa, (1, 2)), cr, sr


def _seg_mean_matrix():
    i = jnp.arange(AQ_W) // A_HD
    return (i[:, None] == i[None, :]).astype(F32) / A_HD


def _mesh_pos():
    return lax.axis_index("x"), lax.axis_index("y"), lax.axis_index("c")


def _all_gather(shards):
    n = len(shards)

    def body(*refs):
        x_refs, out_refs = refs[:n], refs[n:2 * n]
        send_sems, recv_sems, local_sems = refs[2 * n:]
        x, y, c = _mesh_pos()
        me, sibling = (x, y, c), (x, y, 1 - c)
        chips = [(1 - x, y), (x, 1 - y), (1 - x, 1 - y)]

        def slot(t, px, py, pc):
            return out_refs[t].at[4 * px + 2 * py + pc]

        def copy(t, k, block, to, src=None):
            return pltpu.make_async_remote_copy(
                src_ref=slot(t, *block) if src is None else src, dst_ref=slot(t, *block),
                send_sem=send_sems.at[7 * t + k], recv_sem=recv_sems.at[7 * t + k],
                device_id=to, device_id_type=pl.DeviceIdType.MESH)

        mine = [pltpu.make_async_copy(x_refs[t], slot(t, *me), local_sems.at[t]) for t in range(n)]
        for cp in mine:
            cp.start()
        first = []
        for t in range(n):
            first.append(copy(t, 0, me, sibling, src=x_refs[t]))
            first += [copy(t, 1 + j, me, (*chip, c), src=x_refs[t]) for j, chip in enumerate(chips)]
        for cp in first:
            cp.start()
        passed = []
        for t in range(n):
            for j, chip in enumerate(chips):
                copy(t, 1 + j, (*chip, c), me).wait_recv()
                passed.append(copy(t, 4 + j, (*chip, c), sibling))
                passed[-1].start()
        for t in range(n):
            copy(t, 0, sibling, me).wait_recv()
            for j, chip in enumerate(chips):
                copy(t, 4 + j, (*chip, 1 - c), me).wait_recv()
        for cp in first + passed:
            cp.wait_send()
        for cp in mine:
            cp.wait()

    anyspec = pl.BlockSpec(memory_space=pl.ANY)
    return pl.pallas_call(
        body, name="all_gather_weights",
        out_shape=tuple(jax.ShapeDtypeStruct((N_DEV,) + s.shape, s.dtype) for s in shards),
        in_specs=[anyspec] * n, out_specs=(anyspec,) * n,
        scratch_shapes=[pltpu.SemaphoreType.DMA((7 * n,)), pltpu.SemaphoreType.DMA((7 * n,)),
                        pltpu.SemaphoreType.DMA((n,))],
    )(*shards)


def _exchange_grads(packs):
    n = len(packs)

    def body(*refs):
        g_refs, land_refs = refs[:n], refs[n:2 * n]
        send_sems, recv_sems, local_sems = refs[2 * n:]
        x, y, c = _mesh_pos()
        me = 4 * x + 2 * y + c

        def row(t, j):
            return g_refs[t].at[j if packs[t].shape[0] == N_DEV else 0]

        own = [pltpu.make_async_copy(row(t, me), land_refs[t].at[me], local_sems.at[t]) for t in range(n)]
        for cp in own:
            cp.start()
        sends = []
        for k in range(1, N_DEV):
            peer = (x ^ ((k >> 2) & 1), y ^ ((k >> 1) & 1), c ^ (k & 1))
            pidx = 4 * peer[0] + 2 * peer[1] + peer[2]
            for t in range(n):
                sends.append(pltpu.make_async_remote_copy(
                    src_ref=row(t, pidx), dst_ref=land_refs[t].at[me],
                    send_sem=send_sems.at[7 * t + k - 1], recv_sem=recv_sems.at[7 * t + k - 1],
                    device_id=peer, device_id_type=pl.DeviceIdType.MESH))
                sends[-1].start()
        for k in range(1, N_DEV):
            peer = (x ^ ((k >> 2) & 1), y ^ ((k >> 1) & 1), c ^ (k & 1))
            pidx = 4 * peer[0] + 2 * peer[1] + peer[2]
            for t in range(n):
                pltpu.make_async_remote_copy(
                    src_ref=row(t, me), dst_ref=land_refs[t].at[pidx],
                    send_sem=send_sems.at[7 * t + k - 1], recv_sem=recv_sems.at[7 * t + k - 1],
                    device_id=peer, device_id_type=pl.DeviceIdType.MESH).wait_recv()
        for cp in sends:
            cp.wait_send()
        for cp in own:
            cp.wait()

    anyspec = pl.BlockSpec(memory_space=pl.ANY)
    return pl.pallas_call(
        body, name="exchange_grads",
        out_shape=tuple(jax.ShapeDtypeStruct((N_DEV,) + g.shape[1:], g.dtype) for g in packs),
        in_specs=[anyspec] * n, out_specs=(anyspec,) * n,
        scratch_shapes=[pltpu.SemaphoreType.DMA((7 * n,)), pltpu.SemaphoreType.DMA((7 * n,)),
                        pltpu.SemaphoreType.DMA((n,))],
    )(*packs)


def _in_proj(x, gain, w):
    seq = x.shape[0]
    tm = _tile(seq, 256)

    def body(x_ref, g_ref, w_ref, pa_ref, pr_ref, pg_ref, h_ref):
        n, _ = _rms(x_ref[...])
        h = _bf(n * g_ref[...])
        h_ref[...] = h
        pa_ref[...] = _mm(h, w_ref[:, 0:PA_W])
        pr_ref[...] = _mm(h, w_ref[:, PA_W:PA_W + PR_W])
        pg_ref[...] = _mm(h, w_ref[:, PA_W + PR_W:IN_W])

    row = lambda w_: pl.BlockSpec((tm, w_), lambda i: (i, 0))
    return pl.pallas_call(
        body, name="in_proj", grid=(seq // tm,),
        in_specs=[row(D), _full((1, D)), _full((D, IN_W))],
        out_specs=(row(PA_W), row(PR_W), row(PG_W), row(D)),
        out_shape=(jax.ShapeDtypeStruct((seq, PA_W), F32), jax.ShapeDtypeStruct((seq, PR_W), F32),
                   jax.ShapeDtypeStruct((seq, PG_W), F32), jax.ShapeDtypeStruct((seq, D), _MXU)),
    )(x, gain, w)


def _qk_prep(pa, pr, gq, gk, seg, ca, sa, cr, sr):
    seq = pa.shape[0]
    tm = _tile(seq, 256)

    def body(pa_ref, pr_ref, gq_ref, gk_ref, seg_ref, ca_ref, sa_ref, cr_ref, sr_ref,
             qh_ref, kh_ref, v_ref, rq_ref, rk_ref):
        q = pa_ref[:, 0:AQ_W]
        k = pa_ref[:, AQ_W:AQ_W + AKV_W]
        v_ref[...] = _bf(pa_ref[:, AQ_W + AKV_W:PA_W])
        ca_, sa_ = ca_ref[...], sa_ref[...]
        msq = jnp.dot(q * q, seg_ref[...], precision=HIGHEST, preferred_element_type=F32)
        qn = q * lax.rsqrt(msq + EPS) * gq_ref[...]
        qh_ref[...] = _bf(_rope(qn, _cat(ca_, 4), _cat(sa_, 4), A_HD // 2) * (A_HD ** -0.5))
        msk = jnp.dot(k * k, seg_ref[0:AKV_W, 0:AKV_W], precision=HIGHEST, preferred_element_type=F32)
        kn = k * lax.rsqrt(msk + EPS) * gk_ref[...]
        kh_ref[...] = _bf(_rope(kn, ca_, sa_, A_HD // 2))
        cr_, sr_ = _cat(cr_ref[...], 4), _cat(sr_ref[...], 4)
        rq_ref[...] = _rope(pr_ref[:, 0:R_W], cr_, sr_, R_HD // 2) * (R_HD ** -0.5)
        rk_ref[...] = _rope(pr_ref[:, R_W:2 * R_W], cr_, sr_, R_HD // 2)

    row = lambda w_: pl.BlockSpec((tm, w_), lambda i: (i, 0))
    return pl.pallas_call(
        body, name="qk_prep", grid=(seq // tm,),
        in_specs=[row(PA_W), row(2 * R_W), _full((1, AQ_W)), _full((1, AKV_W)), _full((AQ_W, AQ_W)),
                  row(LANES), row(LANES), row(LANES), row(LANES)],
        out_specs=(row(AQ_W), row(AKV_W), row(AKV_W), row(R_W), row(R_W)),
        out_shape=(jax.ShapeDtypeStruct((seq, AQ_W), _MXU), jax.ShapeDtypeStruct((seq, AKV_W), _MXU),
                   jax.ShapeDtypeStruct((seq, AKV_W), _MXU), jax.ShapeDtypeStruct((seq, R_W), F32),
                   jax.ShapeDtypeStruct((seq, R_W), F32)),
    )(pa, pr, gq, gk, seg, ca, sa, cr, sr)


def _chunk_t(a, tk):
    seq = a.shape[0]
    return a.reshape(seq // tk, tk, a.shape[1]).transpose(0, 2, 1)


def _heads_to_rows(t):
    return t.transpose(2, 0, 1).reshape(t.shape[2], AQ_W)


def _attn_fwd(qt8, k2, vta, tq, tk):
    seq = k2.shape[0]
    nck = seq // tk
    rows = A_G * tq
    vrows = vta.shape[2]
    rb = _tile(tk, 256)
    assert nck % 2 == 0, nck

    def body(qt_ref, k_ref, vt_ref, o_ref, lse_ref, m_sc, acc_sc, qtp_sc, s_a, s_b, p_a, p_b, al_a, al_b):
        g = pl.program_id(0)
        qtp_sc[...] = jnp.zeros_like(qtp_sc)
        qtp_sc[pl.ds(pl.multiple_of(g * A_HD, A_HD), A_HD), :] = jnp.concatenate(
            [qt_ref[a] for a in range(A_G)], axis=1)
        m_sc[...] = jnp.full((1, rows), -jnp.inf, F32)
        acc_sc[...] = jnp.zeros_like(acc_sc)

        def scores(c):
            kc = k_ref[pl.ds(pl.multiple_of(c * tk, tk), tk), :]
            return _mm(kc, qtp_sc[...])

        def stage(c, s_cur, s_nxt, p_cur, p_prv, al_cur, al_prv, first=False, last=False):
            if not last:
                s_nxt[...] = scores(c + 1)
            if not first:
                acc_sc[...] = al_prv[...] * acc_sc[...] + _mm(vt_ref[0, c - 1], p_prv[...])
            for b in range(rows // LANES):
                cs = slice(b * LANES, (b + 1) * LANES)
                m_old = m_sc[:, cs]
                m_new = m_old
                for r in range(0, tk, rb):
                    m_new = jnp.maximum(m_new, jnp.max(s_cur[r:r + rb, cs], axis=0, keepdims=True))
                for r in range(0, tk, rb):
                    p_cur[r:r + rb, cs] = _bf(jnp.exp(s_cur[r:r + rb, cs] - m_new))
                al_cur[:, cs] = jnp.exp(m_old - m_new)
                m_sc[:, cs] = m_new

        s_a[...] = scores(0)
        stage(0, s_a, s_b, p_a, p_b, al_a, al_b, first=True)

        def pair(j, carry):
            stage(2 * j + 1, s_b, s_a, p_b, p_a, al_b, al_a)
            stage(2 * j + 2, s_a, s_b, p_a, p_b, al_a, al_b)
            return carry

        lax.fori_loop(0, nck // 2 - 1, pair, 0)
        stage(nck - 1, s_b, s_a, p_b, p_a, al_b, al_a, last=True)
        acc = al_b[...] * acc_sc[...] + _mm(vt_ref[0, nck - 1], p_b[...])
        l = acc[A_HD:A_HD + 1, :]
        lse = m_sc[...] + jnp.log(l)
        out = acc[0:A_HD, :] * (1.0 / l)
        for a in range(A_G):
            o_ref[a] = out[:, a * tq:(a + 1) * tq]
            lse_ref[a] = lse[:, a * tq:(a + 1) * tq]

    return pl.pallas_call(
        body, name="attn_fwd", grid=(A_KV, seq // tq),
        in_specs=[pl.BlockSpec((A_G, A_HD, tq), lambda g, i: (g, 0, i)),
                  _full((seq, LANES)), pl.BlockSpec((1, nck, vrows, tk), lambda g, i: (g, 0, 0, 0))],
        out_specs=(pl.BlockSpec((A_G, A_HD, tq), lambda g, i: (g, 0, i)),
                   pl.BlockSpec((A_G, 1, tq), lambda g, i: (g, 0, i))),
        out_shape=(jax.ShapeDtypeStruct((A_H, A_HD, seq), F32), jax.ShapeDtypeStruct((A_H, 1, seq), F32)),
        scratch_shapes=[pltpu.VMEM((1, rows), F32), pltpu.VMEM((vrows, rows), F32), pltpu.VMEM((LANES, rows), _MXU),
                        pltpu.VMEM((tk, rows), F32), pltpu.VMEM((tk, rows), F32),
                        pltpu.VMEM((tk, rows), _MXU), pltpu.VMEM((tk, rows), _MXU),
                        pltpu.VMEM((1, rows), F32), pltpu.VMEM((1, rows), F32)],
    )(qt8, k2, vta)


def _attn_bwd(qt8, ot, dot_, lse, k2, v2, k2t, tq, tk, ksplit):
    seq = k2.shape[0]
    sh = seq // ksplit
    nck = sh // tk
    rows = A_G * tq
    rb = _tile(tk, 64)
    assert nck % 2 == 0, nck

    def body(qt_ref, ot_ref, dot_ref, lse_ref, k_ref, v_ref, kt_ref,
             dk_ref, dv_ref, dq_ref, dq_sc, qtp_sc, dotp_sc, pt_sc, dst_sc,
             s_a, s_b, dp_a, dp_b, p_a, p_b, ds_a, ds_b):
        g = pl.program_id(1)
        hrows = pl.ds(pl.multiple_of(g * A_HD, A_HD), A_HD)

        @pl.when(pl.program_id(2) == 0)
        def _():
            dk_ref[...] = jnp.zeros_like(dk_ref)
            dv_ref[...] = jnp.zeros_like(dv_ref)

        lse_row = jnp.concatenate([lse_ref[a] for a in range(A_G)], axis=1)
        dd = jnp.concatenate([jnp.sum(ot_ref[a] * dot_ref[a], axis=0, keepdims=True)
                              for a in range(A_G)], axis=1)
        qtp_sc[...] = jnp.zeros_like(qtp_sc)
        dotp_sc[...] = jnp.zeros_like(dotp_sc)
        qtp_sc[hrows, :] = jnp.concatenate([qt_ref[a] for a in range(A_G)], axis=1)
        dotp_sc[hrows, :] = _bf(jnp.concatenate([dot_ref[a] for a in range(A_G)], axis=1))
        dq_sc[...] = jnp.zeros_like(dq_sc)

        def products(c, s_ref, dp_ref):
            sl = pl.ds(pl.multiple_of(c * tk, tk), tk)
            s_ref[...] = _mm(k_ref[sl, :], qtp_sc[...])
            dp_ref[...] = _mm(v_ref[sl, :], dotp_sc[...])

        def accumulate(c, p_ref, ds_ref):
            pt_sc[...] = p_ref[...].T
            dst_sc[...] = ds_ref[...].T
            dq_sc[...] += _mm(kt_ref[c, hrows, :], ds_ref[...])
            dv_ref[0, c] += _mm(dotp_sc[hrows, :], pt_sc[...])
            dk_ref[0, c] += _mm(qtp_sc[hrows, :], dst_sc[...])

        def stage(c, s_cur, dp_cur, s_nxt, dp_nxt, p_cur, ds_cur, p_prv, ds_prv, first=False, last=False):
            if not last:
                products(c + 1, s_nxt, dp_nxt)
            if not first:
                accumulate(c - 1, p_prv, ds_prv)
            for r in range(0, tk, rb):
                p = jnp.exp(s_cur[r:r + rb, :] - lse_row)
                p_cur[r:r + rb, :] = _bf(p)
                ds_cur[r:r + rb, :] = _bf(p * (dp_cur[r:r + rb, :] - dd))

        products(0, s_a, dp_a)
        stage(0, s_a, dp_a, s_b, dp_b, p_a, ds_a, p_b, ds_b, first=True)

        def pair(j, carry):
            stage(2 * j + 1, s_b, dp_b, s_a, dp_a, p_b, ds_b, p_a, ds_a)
            stage(2 * j + 2, s_a, dp_a, s_b, dp_b, p_a, ds_a, p_b, ds_b)
            return carry

        lax.fori_loop(0, nck // 2 - 1, pair, 0)
        stage(nck - 1, s_b, dp_b, s_a, dp_a, p_b, ds_b, p_a, ds_a, last=True)
        accumulate(nck - 1, p_b, ds_b)
        for a in range(A_G):
            dq_ref[0, a] = dq_sc[:, a * tq:(a + 1) * tq]

    tspec = pl.BlockSpec((A_G, A_HD, tq), lambda s, g, i: (g, 0, i))
    kspec = pl.BlockSpec((sh, LANES), lambda s, g, i: (s, 0))
    gspec = pl.BlockSpec((1, nck, A_HD, tk), lambda s, g, i: (g, s, 0, 0))
    gshape = jax.ShapeDtypeStruct((A_KV, seq // tk, A_HD, tk), F32)
    big = lambda dt: pltpu.VMEM((tk, rows), dt)
    return pl.pallas_call(
        body, name="attn_bwd", grid=(ksplit, A_KV, seq // tq),
        in_specs=[tspec, tspec, tspec, pl.BlockSpec((A_G, 1, tq), lambda s, g, i: (g, 0, i)),
                  kspec, kspec, pl.BlockSpec((nck, LANES, tk), lambda s, g, i: (s, 0, 0))],
        out_specs=(gspec, gspec, pl.BlockSpec((1, A_G, A_HD, tq), lambda s, g, i: (s, g, 0, i))),
        out_shape=(gshape, gshape, jax.ShapeDtypeStruct((ksplit, A_H, A_HD, seq), F32)),
        scratch_shapes=[pltpu.VMEM((A_HD, rows), F32), pltpu.VMEM((LANES, rows), _MXU), pltpu.VMEM((LANES, rows), _MXU),
                        pltpu.VMEM((rows, tk), _MXU), pltpu.VMEM((rows, tk), _MXU),
                        big(F32), big(F32), big(F32), big(F32), big(_MXU), big(_MXU), big(_MXU), big(_MXU)],
    )(qt8, ot, dot_, lse, k2, v2, k2t)


def _chunks_to_rows(t):
    return t.transpose(1, 3, 0, 2).reshape(t.shape[1] * t.shape[3], AKV_W)


def _ret_tables(zb):
    c = CHUNK

    def body(z_ref, m_ref, mw_ref, qd_ref, qdw_ref, kd_ref, kdw_ref, g_ref, gw_ref):
        fwd = pl.program_id(0) < R_H
        z = z_ref[0]
        lam = jnp.minimum(z, 0.0) - jnp.log(1.0 + jnp.exp(-jnp.abs(z)))
        i = lax.broadcasted_iota(jnp.int32, (c, c), 0).astype(F32)
        j = lax.broadcasted_iota(jnp.int32, (c, c), 1).astype(F32)
        diff = jnp.where(fwd, i - j, j - i)
        keep = diff >= jnp.where(fwd, 0.0, 1.0)
        dist = jnp.maximum(diff, 0.0)
        m = jnp.where(keep, jnp.exp(lam * dist), 0.0)
        m_ref[0] = m
        mw_ref[0] = m * dist
        fq = jnp.where(fwd, i + 1.0, c - i)
        qd = jnp.exp(lam * fq)
        qd_ref[0] = qd
        qdw_ref[0] = qd * fq
        fk = jnp.where(fwd, c - 1.0 - i, i)
        kd = jnp.exp(lam * fk)
        kd_ref[0] = kd
        kdw_ref[0] = kd * fk
        gdec = jnp.exp(lam * c)
        g_ref[0] = gdec
        gw_ref[0] = gdec * c

    big = pl.BlockSpec((1, c, c), lambda t: (t, 0, 0))
    vec = pl.BlockSpec((1, 1, LANES), lambda t: (t, 0, 0))
    bshape = jax.ShapeDtypeStruct((2 * R_H, c, c), F32)
    vshape = jax.ShapeDtypeStruct((2 * R_H, 1, LANES), F32)
    return pl.pallas_call(
        body, name="ret_tables", grid=(2 * R_H,), in_specs=[vec],
        out_specs=(big, big, big, big, big, big, vec, vec),
        out_shape=(bshape,) * 6 + (vshape, vshape),
    )(zb)


def _ret_fwd(rq, rk, pr, m, qd, kd, gdec, cb):
    seq = rq.shape[0]
    c = CHUNK
    ns = seq // (cb * c)

    def body(q_ref, k_ref, v_ref, m_ref, qd_ref, kd_ref, g_ref, y_ref, pst_ref, p_sc):
        d = pl.program_id(0)

        @pl.when(pl.program_id(1) == 0)
        def _():
            p_sc[...] = jnp.zeros_like(p_sc)

        def chunk(j, carry):
            cc = jnp.where(d == 0, j, cb - 1 - j)
            sl = pl.ds(pl.multiple_of(cc * c, c), c)
            heads = [slice(h * R_HD, (h + 1) * R_HD) for h in range(R_H)]
            qk = [_mm_nt(q_ref[sl, hs], k_ref[sl, hs]) for hs in heads]
            qp = [_mm(q_ref[sl, hs] * qd_ref[h], p_sc[h]) for h, hs in enumerate(heads)]
            kv = [_mm_tn(k_ref[sl, hs] * kd_ref[h], v_ref[sl, hs]) for h, hs in enumerate(heads)]
            for h, hs in enumerate(heads):
                p = p_sc[h]
                pst_ref[h, cc] = p
                y_ref[0, sl, hs] = _mm(qk[h] * m_ref[h], v_ref[sl, hs]) + qp[h]
                p_sc[h] = p * g_ref[h] + kv[h]
            return carry

        lax.fori_loop(0, cb, chunk, 0)

    def step(d, n):
        return d * (ns - 1 - n) + (1 - d) * n

    blk = lambda off: pl.BlockSpec((cb * c, R_W), lambda d, n: (step(d, n), off))
    big = pl.BlockSpec((R_H, c, c), lambda d, n: (d, 0, 0))
    vec = pl.BlockSpec((R_H, 1, LANES), lambda d, n: (d, 0, 0))
    return pl.pallas_call(
        body, name="ret_fwd", grid=(2, ns),
        in_specs=[blk(0), blk(0), blk(2), big, big, big, vec],
        out_specs=(pl.BlockSpec((1, cb * c, R_W), lambda d, n: (d, step(d, n), 0)),
                   pl.BlockSpec((R_H, cb, R_HD, R_HD), lambda d, n: (d, step(d, n), 0, 0))),
        out_shape=(jax.ShapeDtypeStruct((2, seq, R_W), F32),
                   jax.ShapeDtypeStruct((2 * R_H, seq // c, R_HD, R_HD), F32)),
        scratch_shapes=[pltpu.VMEM((R_H, R_HD, R_HD), F32)],
    )(rq, rk, pr, m, qd, kd, gdec)


def _ret_bwd(rq, rk, pr, dry, pst, m, mw, qd, qdw, kd, kdw, gdec, gw, cb):
    seq = rq.shape[0]
    c = CHUNK
    ns = seq // (cb * c)

    def body(q_ref, k_ref, v_ref, dy_ref, pst_ref, m_ref, mw_ref, qd_ref, qdw_ref, kd_ref, kdw_ref,
             g_ref, gw_ref, dq_ref, dk_ref, dv_ref, dlam_ref, r_sc, acc_sc, e_sc, g_sc):
        d = pl.program_id(0)
        n = pl.program_id(1)

        @pl.when(n == 0)
        def _():
            r_sc[...] = jnp.zeros_like(r_sc)
            acc_sc[...] = jnp.zeros_like(acc_sc)
            e_sc[...] = jnp.zeros_like(e_sc)
            g_sc[...] = jnp.zeros_like(g_sc)

        def chunk(j, carry):
            cc = jnp.where(d == 0, cb - 1 - j, j)
            sl = pl.ds(pl.multiple_of(cc * c, c), c)
            heads = [slice(h * R_HD, (h + 1) * R_HD) for h in range(R_H)]
            first = []
            for h, hs in enumerate(heads):
                q, k, v, dy = q_ref[sl, hs], k_ref[sl, hs], v_ref[sl, hs], dy_ref[sl, hs]
                r = r_sc[h]
                first.append((_mm_nt(q, k), _mm_nt(dy, v), _mm_nt(dy, pst_ref[h, cc]), _mm_nt(v, r),
                              _mm(k * kd_ref[h], r), _mm_tn(q * qd_ref[h], dy)))
            for h, hs in enumerate(heads):
                qk, ds, dyp, vr, kr, qdy = first[h]
                q, k, dy = q_ref[sl, hs], k_ref[sl, hs], dy_ref[sl, hs]
                r = r_sc[h]
                da = ds * m_ref[h]
                dv_ref[0, sl, hs] = _mm_tn(qk * m_ref[h], dy) + kr
                dq_ref[0, sl, hs] = _mm(da, k) + dyp * qd_ref[h]
                dk_ref[0, sl, hs] = _mm_tn(da, q) + vr * kd_ref[h]
                acc_sc[h] += dyp * q * qdw_ref[h] + vr * k * kdw_ref[h]
                e_sc[h] += ds * qk * mw_ref[h]
                g_sc[h] += r * pst_ref[h, cc]
                r_sc[h] = r * g_ref[h] + qdy
            return carry

        lax.fori_loop(0, cb, chunk, 0)

        @pl.when(n == ns - 1)
        def _():
            for h in range(R_H):
                tot = jnp.sum(jnp.sum(acc_sc[h] + e_sc[h] + g_sc[h] * gw_ref[h], axis=0, keepdims=True),
                              axis=1, keepdims=True)
                dlam_ref[h] = jnp.broadcast_to(tot, (1, LANES))

    def step(d, n):
        return d * n + (1 - d) * (ns - 1 - n)

    blk = lambda off: pl.BlockSpec((cb * c, R_W), lambda d, n: (step(d, n), off))
    big = pl.BlockSpec((R_H, c, c), lambda d, n: (d, 0, 0))
    vec = pl.BlockSpec((R_H, 1, LANES), lambda d, n: (d, 0, 0))
    out = pl.BlockSpec((1, cb * c, R_W), lambda d, n: (d, step(d, n), 0))
    oshape = jax.ShapeDtypeStruct((2, seq, R_W), F32)
    sq = pltpu.VMEM((R_H, R_HD, R_HD), F32)
    return pl.pallas_call(
        body, name="ret_bwd", grid=(2, ns),
        in_specs=[blk(0), blk(0), blk(2), blk(0),
                  pl.BlockSpec((R_H, cb, R_HD, R_HD), lambda d, n: (d, step(d, n), 0, 0)),
                  big, big, big, big, big, big, vec, vec],
        out_specs=(out, out, out, vec),
        out_shape=(oshape, oshape, oshape, jax.ShapeDtypeStruct((2 * R_H, 1, LANES), F32)),
        scratch_shapes=[sq, sq, sq, sq],
    )(rq, rk, pr, dry, pst, m, mw, qd, qdw, kd, kdw, gdec, gw)


def _group_norm(ry):
    yn, rs = [], []
    for h in range(R_H):
        s = ry[:, h * R_HD:(h + 1) * R_HD]
        mu = jnp.mean(s, axis=-1, keepdims=True)
        cen = s - mu
        r = lax.rsqrt(jnp.mean(cen * cen, axis=-1, keepdims=True) + GN_EPS)
        yn.append(cen * r)
        rs.append(r)
    return yn, rs


def _merge_fwd(x, o, y2, pr, pg, gain_r, wao, wro, wout):
    seq = x.shape[0]
    tm = _tile(seq, 256)

    def body(x_ref, o_ref, yf_ref, yb_ref, rg_ref, ga_ref, gr_ref, gn_ref, wao_ref, wro_ref, wout_ref,
             x1_ref, mg_ref, ri_ref):
        yn, _ = _group_norm(yf_ref[0] + yb_ref[0])
        rg = rg_ref[...]
        ret_in = jnp.concatenate(yn, axis=1) * gn_ref[...] * (rg * _sigmoid(rg))
        ri_ref[...] = _bf(ret_in)
        attn_out = _mm(o_ref[...], wao_ref[...])
        ret_out = _mm(ret_in, wro_ref[...])
        merged = _sigmoid(ga_ref[...]) * attn_out + _sigmoid(gr_ref[...]) * ret_out
        mg_ref[...] = _bf(merged)
        x1_ref[...] = x_ref[...] + _mm(merged, wout_ref[...])

    row = lambda w_, j=0: pl.BlockSpec((tm, w_), lambda i: (i, j))
    ydir = lambda d: pl.BlockSpec((1, tm, R_W), lambda i: (d, i, 0))
    return pl.pallas_call(
        body, name="merge_fwd", grid=(seq // tm,),
        in_specs=[row(D), row(AQ_W), ydir(0), ydir(1), row(R_W, 3), row(D, 0), row(D, 1),
                  _full((1, R_W)), _full((AQ_W, D)), _full((R_W, D)), _full((D, D))],
        out_specs=(row(D), row(D), row(R_W)),
        out_shape=(jax.ShapeDtypeStruct((seq, D), F32), jax.ShapeDtypeStruct((seq, D), _MXU),
                   jax.ShapeDtypeStruct((seq, R_W), _MXU)),
    )(x, o, y2, y2, pr, pg, pg, gain_r, wao, wro, wout)


def _mlp_fwd(x1, gain, wup, wdown):
    seq = x1.shape[0]
    tm = _tile(seq, 512)
    fc = 512
    nfc = FF // fc

    def body(x_ref, g_ref, wu_ref, wd_ref, x2_ref, hm_sc, acc_sc):
        c = pl.program_id(1)

        @pl.when(c == 0)
        def _():
            n, _ = _rms(x_ref[...])
            hm_sc[...] = _bf(n * g_ref[...])
            acc_sc[...] = jnp.zeros_like(acc_sc)

        halves = (slice(0, fc // 2), slice(fc // 2, fc))
        ups = [jnp.maximum(_mm(hm_sc[...], wu_ref[:, hs]), 0.0) for hs in halves]
        acc_sc[...] += _mm(ups[0] * ups[0], wd_ref[halves[0], :]) + _mm(ups[1] * ups[1], wd_ref[halves[1], :])

        @pl.when(c == nfc - 1)
        def _():
            x2_ref[...] = x_ref[...] + acc_sc[...]

    return pl.pallas_call(
        body, name="mlp_fwd", grid=(seq // tm, nfc),
        in_specs=[pl.BlockSpec((tm, D), lambda i, c: (i, 0)), pl.BlockSpec((1, D), lambda i, c: (0, 0)),
                  pl.BlockSpec((D, fc), lambda i, c: (0, c)), pl.BlockSpec((fc, D), lambda i, c: (c, 0))],
        out_specs=pl.BlockSpec((tm, D), lambda i, c: (i, 0)),
        out_shape=jax.ShapeDtypeStruct((seq, D), F32),
        scratch_shapes=[pltpu.VMEM((tm, D), _MXU), pltpu.VMEM((tm, D), F32)],
    )(x1, gain, wup, wdown)


def _ple_loss(x2, p, tgt, g_ple, g_fin, wpg, wpgt, wple):
    seq = x2.shape[0]
    tm = _tile(seq, 256)

    def body(x2_ref, p_ref, t_ref, gp_ref, gf_ref, wpg_ref, wpgt_ref, wple_ref,
             dx2_ref, de_ref, dz_ref, hp_ref, loss_ref, dgf_ref, dgp_ref):
        @pl.when(pl.program_id(0) == 0)
        def _():
            loss_ref[...] = jnp.zeros_like(loss_ref)
            dgf_ref[...] = jnp.zeros_like(dgf_ref)
            dgp_ref[...] = jnp.zeros_like(dgp_ref)

        x2 = x2_ref[...]
        gp, gf = gp_ref[...], gf_ref[...]
        n2, r2 = _rms(x2)
        hp = _bf(n2 * gp)
        hp_ref[...] = hp
        gate = _sigmoid(_mm(hp, wpg_ref[...]))
        e = _mm(p_ref[...], wple_ref[...])
        x3 = x2 + gate * e
        n3, r3 = _rms(x3)
        diff = n3 * gf - t_ref[...]
        row_loss = jnp.mean(diff * diff, axis=-1, keepdims=True)
        loss_ref[...] += 0.5 * jnp.sum(row_loss, axis=0, keepdims=True)
        dy = diff * (1.0 / D)
        dgf_ref[...] += jnp.sum(dy * n3, axis=0, keepdims=True)
        dx3 = _rms_bwd(n3, r3, gf, dy)
        de_ref[...] = _bf(dx3 * gate)
        dz = dx3 * e * gate * (1.0 - gate)
        dz_ref[...] = _bf(dz)
        dhp = _mm(dz, wpgt_ref[...])
        dgp_ref[...] += jnp.sum(dhp * n2, axis=0, keepdims=True)
        dx2_ref[...] = dx3 + _rms_bwd(n2, r2, gp, dhp)

    row = lambda w_: pl.BlockSpec((tm, w_), lambda i: (i, 0))
    act = lambda dt: jax.ShapeDtypeStruct((seq, D), dt)
    return pl.pallas_call(
        body, name="ple_loss", grid=(seq // tm,),
        in_specs=[row(D), row(PLE), row(D), _full((1, D)), _full((1, D)),
                  _full((D, D)), _full((D, D)), _full((PLE, D))],
        out_specs=(row(D), row(D), row(D), row(D), _full((1, LANES)), _full((1, D)), _full((1, D))),
        out_shape=(act(F32), act(_MXU), act(_MXU), act(_MXU), jax.ShapeDtypeStruct((1, LANES), F32),
                   jax.ShapeDtypeStruct((1, D), F32), jax.ShapeDtypeStruct((1, D), F32)),
    )(x2, p, tgt, g_ple, g_fin, wpg, wpgt, wple)


def _mlp_bwd(x1, dx2, gain, wup, wdownt, wupt):
    seq = x1.shape[0]
    tm = _tile(seq, 512)
    fc = 512
    nfc = FF // fc

    def body(x_ref, dx2_ref, g_ref, wu_ref, wdt_ref, wut_ref,
             dx1_ref, a_ref, du_ref, hm_ref, dg_ref, dhm_sc):
        i = pl.program_id(0)
        c = pl.program_id(1)

        @pl.when((i == 0) & (c == 0))
        def _():
            dg_ref[...] = jnp.zeros_like(dg_ref)

        @pl.when(c == 0)
        def _():
            n, _ = _rms(x_ref[...])
            hm_ref[...] = _bf(n * g_ref[...])
            dhm_sc[...] = jnp.zeros_like(dhm_sc)

        halves = (slice(0, fc // 2), slice(fc // 2, fc))
        ups = [jnp.maximum(_mm(hm_ref[...], wu_ref[:, hs]), 0.0) for hs in halves]
        das = [_mm(dx2_ref[...], wdt_ref[:, hs]) for hs in halves]
        part = None
        for u, da, hs in zip(ups, das, halves):
            a_ref[:, hs] = _bf(u * u)
            du = _bf(da * (2.0 * u))
            du_ref[:, hs] = du
            t = _mm(du, wut_ref[hs, :])
            part = t if part is None else part + t
        dhm_sc[...] += part

        @pl.when(c == nfc - 1)
        def _():
            n, r = _rms(x_ref[...])
            dhm = dhm_sc[...]
            dg_ref[...] += jnp.sum(dhm * n, axis=0, keepdims=True)
            dx1_ref[...] = dx2_ref[...] + _rms_bwd(n, r, g_ref[...], dhm)

    rowd = pl.BlockSpec((tm, D), lambda i, c: (i, 0))
    rowf = pl.BlockSpec((tm, fc), lambda i, c: (i, c))
    return pl.pallas_call(
        body, name="mlp_bwd", grid=(seq // tm, nfc),
        in_specs=[rowd, rowd, pl.BlockSpec((1, D), lambda i, c: (0, 0)),
                  pl.BlockSpec((D, fc), lambda i, c: (0, c)), pl.BlockSpec((D, fc), lambda i, c: (0, c)),
                  pl.BlockSpec((fc, D), lambda i, c: (c, 0))],
        out_specs=(rowd, rowf, rowf, rowd, pl.BlockSpec((1, D), lambda i, c: (0, 0))),
        out_shape=(jax.ShapeDtypeStruct((seq, D), F32), jax.ShapeDtypeStruct((seq, FF), _MXU),
                   jax.ShapeDtypeStruct((seq, FF), _MXU), jax.ShapeDtypeStruct((seq, D), _MXU),
                   jax.ShapeDtypeStruct((1, D), F32)),
        scratch_shapes=[pltpu.VMEM((tm, D), F32)],
    )(x1, dx2, gain, wup, wdownt, wupt)


def _merge_bwd(dx1, o, y2, pr, pg, gain_r, wao, wro, woutt, waot, wrot):
    seq = dx1.shape[0]
    tm = _tile(seq, 256)

    def body(dx1_ref, o_ref, yf_ref, yb_ref, rg_ref, ga_ref, gr_ref, gn_ref, wao_ref, wro_ref,
             woutt_ref, waot_ref, wrot_ref,
             dpg_ref, dao_ref, dro_ref, do_ref, dry_ref, drg_ref, dgn_ref):
        @pl.when(pl.program_id(0) == 0)
        def _():
            dgn_ref[...] = jnp.zeros_like(dgn_ref)

        yn_l, rs_l = _group_norm(yf_ref[0] + yb_ref[0])
        yn = jnp.concatenate(yn_l, axis=1)
        rg = rg_ref[...]
        gn = gn_ref[...]
        sg = _sigmoid(rg)
        sil = rg * sg
        ret_in = yn * gn * sil
        attn_out = _mm(o_ref[...], wao_ref[...])
        ret_out = _mm(ret_in, wro_ref[...])
        sa = _sigmoid(ga_ref[...])
        sr = _sigmoid(gr_ref[...])
        dm = _mm(dx1_ref[...], woutt_ref[...])
        dpg_ref[:, 0:D] = _bf(dm * attn_out * sa * (1.0 - sa))
        dpg_ref[:, D:2 * D] = _bf(dm * ret_out * sr * (1.0 - sr))
        dao = _bf(dm * sa)
        dro = _bf(dm * sr)
        dao_ref[...] = dao
        dro_ref[...] = dro
        do_ref[...] = _mm(dao, waot_ref[...])
        dri = _mm(dro, wrot_ref[...])
        dgn_ref[...] += jnp.sum(dri * yn * sil, axis=0, keepdims=True)
        drg_ref[...] = _bf(dri * yn * gn * (sg * (1.0 + rg * (1.0 - sg))))
        dyn = dri * gn * sil
        dry = []
        for h in range(R_H):
            dh = dyn[:, h * R_HD:(h + 1) * R_HD]
            dry.append(rs_l[h] * (dh - jnp.mean(dh, axis=-1, keepdims=True)
                                  - yn_l[h] * jnp.mean(dh * yn_l[h], axis=-1, keepdims=True)))
        dry_ref[...] = jnp.concatenate(dry, axis=1)

    row = lambda w_, j=0: pl.BlockSpec((tm, w_), lambda i: (i, j))
    ydir = lambda d: pl.BlockSpec((1, tm, R_W), lambda i: (d, i, 0))
    return pl.pallas_call(
        body, name="merge_bwd", grid=(seq // tm,),
        in_specs=[row(D), row(AQ_W), ydir(0), ydir(1), row(R_W, 3), row(D, 0), row(D, 1),
                  _full((1, R_W)), _full((AQ_W, D)), _full((R_W, D)), _full((D, D)),
                  _full((D, AQ_W)), _full((D, R_W))],
        out_specs=(row(PG_W), row(D), row(D), row(AQ_W), row(R_W), row(R_W), _full((1, R_W))),
        out_shape=(jax.ShapeDtypeStruct((seq, PG_W), _MXU), jax.ShapeDtypeStruct((seq, D), _MXU),
                   jax.ShapeDtypeStruct((seq, D), _MXU), jax.ShapeDtypeStruct((seq, AQ_W), F32),
                   jax.ShapeDtypeStruct((seq, R_W), F32), jax.ShapeDtypeStruct((seq, R_W), _MXU),
                   jax.ShapeDtypeStruct((1, R_W), F32)),
    )(dx1, o, y2, y2, pr, pg, pg, gain_r, wao, wro, woutt, waot, wrot)


def _qk_prep_bwd(pa, dqh, dk2, dv2, rdq, rdk, rdv, drg, gq, gk, seg, ca, sa, cr, sr):
    seq = pa.shape[0]
    tm = _tile(seq, 256)

    def body(pa_ref, dqh_ref, dk2_ref, dv2_ref, rdqf_ref, rdqb_ref, rdkf_ref, rdkb_ref, rdvf_ref, rdvb_ref,
             drg_ref, gq_ref, gk_ref, seg_ref, ca_ref, sa_ref, cr_ref, sr_ref,
             dpa_ref, dpr_ref, dgq_ref, dgk_ref):
        @pl.when(pl.program_id(0) == 0)
        def _():
            dgq_ref[...] = jnp.zeros_like(dgq_ref)
            dgk_ref[...] = jnp.zeros_like(dgk_ref)

        ca_, sa_ = ca_ref[...], sa_ref[...]

        def norm_bwd(raw, gain, dy, segm, dg_ref):
            msq = jnp.dot(raw * raw, segm, precision=HIGHEST, preferred_element_type=F32)
            r = lax.rsqrt(msq + EPS)
            n = raw * r
            dg_ref[...] += jnp.sum(dy * n, axis=0, keepdims=True)
            dn = dy * gain
            return r * (dn - n * jnp.dot(dn * n, segm, precision=HIGHEST, preferred_element_type=F32))

        dqn = _rope(dqh_ref[...] * (A_HD ** -0.5), _cat(ca_, 4), -_cat(sa_, 4), A_HD // 2)
        dpa_ref[:, 0:AQ_W] = _bf(norm_bwd(pa_ref[:, 0:AQ_W], gq_ref[...], dqn, seg_ref[...], dgq_ref))
        dkn = _rope(dk2_ref[...], ca_, -sa_, A_HD // 2)
        dpa_ref[:, AQ_W:AQ_W + AKV_W] = _bf(norm_bwd(pa_ref[:, AQ_W:AQ_W + AKV_W], gk_ref[...], dkn,
                                                     seg_ref[0:AKV_W, 0:AKV_W], dgk_ref))
        dpa_ref[:, AQ_W + AKV_W:PA_W] = _bf(dv2_ref[...])
        cr_, sr_ = _cat(cr_ref[...], 4), -_cat(sr_ref[...], 4)
        dpr_ref[:, 0:R_W] = _bf(_rope((rdqf_ref[0] + rdqb_ref[0]) * (R_HD ** -0.5), cr_, sr_, R_HD // 2))
        dpr_ref[:, R_W:2 * R_W] = _bf(_rope(rdkf_ref[0] + rdkb_ref[0], cr_, sr_, R_HD // 2))
        dpr_ref[:, 2 * R_W:3 * R_W] = _bf(rdvf_ref[0] + rdvb_ref[0])
        dpr_ref[:, 3 * R_W:4 * R_W] = drg_ref[...]

    row = lambda w_: pl.BlockSpec((tm, w_), lambda i: (i, 0))
    ydir = lambda d: pl.BlockSpec((1, tm, R_W), lambda i: (d, i, 0))
    return pl.pallas_call(
        body, name="qk_prep_bwd", grid=(seq // tm,),
        in_specs=[row(PA_W), row(AQ_W), row(AKV_W), row(AKV_W), ydir(0), ydir(1), ydir(0), ydir(1),
                  ydir(0), ydir(1), row(R_W), _full((1, AQ_W)), _full((1, AKV_W)), _full((AQ_W, AQ_W)),
                  row(LANES), row(LANES), row(LANES), row(LANES)],
        out_specs=(row(PA_W), row(PR_W), _full((1, AQ_W)), _full((1, AKV_W))),
        out_shape=(jax.ShapeDtypeStruct((seq, PA_W), _MXU), jax.ShapeDtypeStruct((seq, PR_W), _MXU),
                   jax.ShapeDtypeStruct((1, AQ_W), F32), jax.ShapeDtypeStruct((1, AKV_W), F32)),
    )(pa, dqh, dk2, dv2, rdq, rdq, rdk, rdk, rdv, rdv, drg, gq, gk, seg, ca, sa, cr, sr)


def _in_proj_bwd(x, dx1, gain, dpa, dpr, dpg, wint):
    seq = x.shape[0]
    tm = _tile(seq, 256)

    def body(x_ref, dx1_ref, g_ref, dpa_ref, dpr_ref, dpg_ref, wt_ref, dx_ref, dg_ref):
        @pl.when(pl.program_id(0) == 0)
        def _():
            dg_ref[...] = jnp.zeros_like(dg_ref)

        dh = (_mm(dpa_ref[...], wt_ref[0:PA_W, :]) + _mm(dpr_ref[...], wt_ref[PA_W:PA_W + PR_W, :])
              + _mm(dpg_ref[...], wt_ref[PA_W + PR_W:IN_W, :]))
        n, r = _rms(x_ref[...])
        dg_ref[...] += jnp.sum(dh * n, axis=0, keepdims=True)
        dx_ref[...] = dx1_ref[...] + _rms_bwd(n, r, g_ref[...], dh)

    row = lambda w_: pl.BlockSpec((tm, w_), lambda i: (i, 0))
    return pl.pallas_call(
        body, name="in_proj_bwd", grid=(seq // tm,),
        in_specs=[row(D), row(D), _full((1, D)), row(PA_W), row(PR_W), row(PG_W), _full((IN_W, D))],
        out_specs=(row(D), _full((1, D))),
        out_shape=(jax.ShapeDtypeStruct((seq, D), F32), jax.ShapeDtypeStruct((1, D), F32)),
    )(x, dx1, gain, dpa, dpr, dpg, wint)


def _wgrad(a, b, name):
    seq, m = a.shape
    n = b.shape[1]
    tm, tn, ts = _tile(m, 512), _tile(n, 1024), _tile(seq, 2048)
    ns = seq // ts

    def body(a_ref, b_ref, o_ref):
        @pl.when(pl.program_id(2) == 0)
        def _():
            o_ref[...] = jnp.zeros_like(o_ref)

        o_ref[...] += _mm_tn(a_ref[...], b_ref[...])

    return pl.pallas_call(
        body, name=name, grid=(m // tm, n // tn, ns),
        in_specs=[pl.BlockSpec((ts, tm), lambda i, j, s: (s, i)), pl.BlockSpec((ts, tn), lambda i, j, s: (s, j))],
        out_specs=pl.BlockSpec((tm, tn), lambda i, j, s: (i, j)),
        out_shape=jax.ShapeDtypeStruct((m, n), F32),
    )(a, b)


def _adamw_math(w, g, m, v):
    m = B1 * m + (1.0 - B1) * g
    v = B2 * v + (1.0 - B2) * (g * g)
    m_hat = m / (1.0 - B1 ** STEP)
    v_hat = v / (1.0 - B2 ** STEP)
    delta = -LR * (m_hat / (jnp.sqrt(v_hat) + ADAM_EPS) + WD * w)
    return delta, m, v


def _adamw_big(land, own, w, m, v, name):
    rws, cols = w.shape
    tr = next(t for t in range(min(rws, 288), 0, -8) if rws % t == 0)

    def body(l_ref, o_ref, w_ref, m_ref, v_ref, g_ref, d_ref, nm_ref, nv_ref):
        x, y, c = _mesh_pos()
        me = 4 * x + 2 * y + c
        g = o_ref[...]
        for j in range(N_DEV):
            g = g + jnp.where(me == j, 0.0, l_ref[j].astype(F32))
        g_ref[...] = g
        d_ref[...], nm_ref[...], nv_ref[...] = _adamw_math(w_ref[...], g, m_ref[...], v_ref[...])

    row = pl.BlockSpec((tr, cols), lambda i: (i, 0))
    shp = jax.ShapeDtypeStruct((rws, cols), F32)
    return pl.pallas_call(
        body, name=name, grid=(rws // tr,),
        in_specs=[pl.BlockSpec((N_DEV, tr, cols), lambda i: (0, i, 0)), row, row, row, row],
        out_specs=(row, row, row, row), out_shape=(shp, shp, shp, shp),
    )(land, own, w, m, v)


def _adamw_small(sland, w, m, v):
    def body(l_ref, w_ref, m_ref, v_ref, g_ref, d_ref, nm_ref, nv_ref, loss_ref):
        s = l_ref[0]
        for j in range(1, N_DEV):
            s = s + l_ref[j]
        w = w_ref[...]
        gq = s[8:9]
        for h in range(1, A_H):
            gq = gq + s[8 + h:9 + h]
        gk = s[16:17] + s[17:18]
        gdec = s[5:6] * _sigmoid(-w[5:6])
        g = jnp.concatenate([s[0:5], gdec, gq, gk], axis=0)
        g_ref[...] = g
        d_ref[...], nm_ref[...], nv_ref[...] = _adamw_math(w, g, m_ref[...], v_ref[...])
        loss_ref[...] = s[6:7, 0:LANES]

    shp = jax.ShapeDtypeStruct((8, PACK_COLS), F32)
    return pl.pallas_call(
        body, name="adamw_small",
        out_shape=(shp, shp, shp, shp, jax.ShapeDtypeStruct((1, LANES), F32)),
    )(sland, w, m, v)


_BIG = (("w_attn_o", AQ_W, D, 1), ("w_ret_o", R_W, D, 1), ("w_out", D, D, 0),
        ("w_up", D, FF, 1), ("w_down", FF, D, 0), ("w_ple_gate", D, D, 0), ("w_ple", PLE, D, 1))
IN_SHARD = IN_W // N_DEV
_SMALL = ("mix_norm", "mlp_norm", "ple_norm", "final_norm", "ret_norm_gain", "ret_decay_logit",
          "attn_q_norm", "attn_k_norm")


def _shard_shape(rows, cols, axis):
    return (rows // N_DEV, cols) if axis == 0 else (rows, cols // N_DEV)


def _pack_shards(shards):
    flat = jnp.concatenate([s.reshape(-1) for s in shards])
    return flat.reshape(-1, PACK_COLS)


def _unpack_gathered(gathered):
    flat = gathered.reshape(N_DEV, -1)
    out, off = {}, 0
    for name, rows, cols, axis in _BIG:
        sr, sc = _shard_shape(rows, cols, axis)
        blk = flat[:, off:off + sr * sc].reshape(N_DEV, sr, sc)
        off += sr * sc
        out[name] = blk.reshape(rows, cols) if axis == 0 else blk.transpose(1, 0, 2).reshape(rows, cols)
    return out


def _pack_full_grads(grads):
    parts = []
    for name, rows, cols, axis in _BIG:
        sr, sc = _shard_shape(rows, cols, axis)
        g = grads[name]
        blk = g.reshape(N_DEV, sr, sc) if axis == 0 else g.reshape(rows, N_DEV, sc).transpose(1, 0, 2)
        parts.append(blk.reshape(N_DEV, -1))
    flat = jnp.concatenate(parts, axis=1)
    return flat.reshape(N_DEV, -1, PACK_COLS)


def _unpack_shard(packed):
    flat = packed.reshape(-1)
    out, off = {}, 0
    for name, rows, cols, axis in _BIG:
        sr, sc = _shard_shape(rows, cols, axis)
        out[name] = flat[off:off + sr * sc].reshape(1, sr, sc)
        off += sr * sc
    return out


def _pack_small(vals):
    rows = [jnp.pad(vals[n].reshape(-1), (0, PACK_COLS - vals[n].size)) for n in _SMALL]
    return jnp.stack(rows)


def _unpack_small(packed, like):
    return {n: packed[i, :like[n].size].reshape(like[n].shape) for i, n in enumerate(_SMALL)}


def _row(v):
    return jnp.pad(v.reshape(-1), (0, PACK_COLS - v.size))


def kernel(x, p, mix_norm, w_in, attn_q_norm, attn_k_norm, ret_decay_logit, ret_norm_gain, w_attn_o, w_ret_o, w_out, mlp_norm, w_up, w_down, ple_norm, w_ple_gate, w_ple, final_norm, loss_target, m_mix_norm, m_w_in, m_attn_q_norm, m_attn_k_norm, m_ret_decay_logit, m_ret_norm_gain, m_w_attn_o, m_w_ret_o, m_w_out, m_mlp_norm, m_w_up, m_w_down, m_ple_norm, m_w_ple_gate, m_w_ple, m_final_norm, v_mix_norm, v_w_in, v_attn_q_norm, v_attn_k_norm, v_ret_decay_logit, v_ret_norm_gain, v_w_attn_o, v_w_ret_o, v_w_out, v_mlp_norm, v_w_up, v_w_down, v_ple_norm, v_w_ple_gate, v_w_ple, v_final_norm):
    args = dict(locals())
    seq = x.shape[1]
    xs = x[0]
    ps = p[0, 0]
    tgt = loss_target[0]

    big_names = [b[0] for b in _BIG]
    wshard = _pack_shards([args[n] for n in big_names])
    win_g, rest_g = _all_gather([w_in[0].astype(_MXU), wshard.astype(_MXU)])
    win = win_g.transpose(1, 0, 2).reshape(D, IN_W)
    wfull = _unpack_gathered(rest_g)
    wao, wro, wout = wfull["w_attn_o"], wfull["w_ret_o"], wfull["w_out"]
    wup, wdown, wpg, wple = wfull["w_up"], wfull["w_down"], wfull["w_ple_gate"], wfull["w_ple"]

    g_mix, g_mlp, g_ple = mix_norm, mlp_norm, ple_norm
    g_fin = final_norm.reshape(1, D)
    gq = jnp.tile(attn_q_norm, (1, A_H))
    gk = jnp.tile(attn_k_norm, (1, A_KV))
    seg = _seg_mean_matrix()
    ca, sa, cr, sr = _rope_tables(seq)

    pa, pr, pg, h = _in_proj(xs, g_mix, win)
    qh, kh, vh, rqh, rkh = _qk_prep(pa, pr, gq, gk, seg, ca, sa, cr, sr)

    tq = _tile(seq, 128)
    tk = _tile(seq // 4, 1024)
    qt8 = qh.reshape(seq, A_H, A_HD).transpose(1, 2, 0)
    vta = jnp.stack([jnp.concatenate([_chunk_t(vh[:, g * A_HD:(g + 1) * A_HD], tk),
                                      jnp.ones((seq // tk, 16, tk), _MXU)], axis=1) for g in range(A_KV)])
    ot, lse = _attn_fwd(qt8, kh, vta, tq, tk)
    o = _heads_to_rows(ot)

    zb = jnp.broadcast_to(ret_decay_logit.reshape(2 * R_H, 1, 1), (2 * R_H, 1, LANES))
    tm_, tmw, tqd, tqdw, tkd, tkdw, tg, tgw = _ret_tables(zb)
    cb = _tile(seq // CHUNK, 8)
    y2, pst = _ret_fwd(rqh, rkh, pr, tm_, tqd, tkd, tg, cb)

    x1, merged, ret_in = _merge_fwd(xs, o, y2, pr, pg, ret_norm_gain, wao, wro, wout)
    x2 = _mlp_fwd(x1, g_mlp, wup, wdown)

    dx2, de, dz, hp, loss_p, dg_fin, dg_ple = _ple_loss(x2, ps, tgt, g_ple, g_fin, wpg, wpg.T, wple)
    dx1, act, du, hm, dg_mlp = _mlp_bwd(x1, dx2, g_mlp, wup, wdown.T, wup.T)
    dpg, dao, dro, do, dry, drg, dg_gn = _merge_bwd(dx1, o, y2, pr, pg, ret_norm_gain, wao, wro,
                                                    wout.T, wao.T, wro.T)
    rdq, rdk, rdv, dlam = _ret_bwd(rqh, rkh, pr, dry, pst, tm_, tmw, tqd, tqdw, tkd, tkdw, tg, tgw, cb)

    ksplit = 2
    tkb = _tile(seq // 4, 512)
    dot_ = do.reshape(seq, A_H, A_HD).transpose(1, 2, 0)
    dkt, dvt, dqt = _attn_bwd(qt8, ot, dot_, lse, kh, vh, _chunk_t(kh, tkb), tq, tkb, ksplit)
    dqh = _heads_to_rows(jnp.sum(dqt, axis=0))
    dpa, dpr, dg_q, dg_k = _qk_prep_bwd(pa, dqh, _chunks_to_rows(dkt), _chunks_to_rows(dvt), rdq, rdk, rdv, drg, gq, gk, seg, ca, sa, cr, sr)
    grad_x, dg_mix = _in_proj_bwd(xs, dx1, g_mix, dpa, dpr, dpg, win.T)

    wg = {
        "w_in": jnp.concatenate([_wgrad(h, dpa, "wgrad_in_a"), _wgrad(h, dpr, "wgrad_in_r"),
                                 _wgrad(h, dpg, "wgrad_in_g")], axis=1),
        "w_attn_o": _wgrad(o, dao, "wgrad_attn_o"),
        "w_ret_o": _wgrad(ret_in, dro, "wgrad_ret_o"),
        "w_out": _wgrad(merged, dx1, "wgrad_out"),
        "w_up": _wgrad(hm, du, "wgrad_up"),
        "w_down": _wgrad(act, dx2, "wgrad_down"),
        "w_ple_gate": _wgrad(hp, dz, "wgrad_ple_gate"),
        "w_ple": _wgrad(ps, de, "wgrad_ple"),
    }
    gpack = _pack_full_grads(wg)
    gpack_in = wg["w_in"].reshape(D, N_DEV, IN_SHARD).transpose(1, 0, 2)
    small = jnp.stack(
        [_row(dg_mix), _row(dg_mlp), _row(dg_ple), _row(dg_fin), _row(dg_gn), _row(dlam[:, 0, 0]),
         _row(loss_p[0, 0:1]), jnp.zeros((PACK_COLS,), F32)]
        + [_row(dg_q[0, hh * A_HD:(hh + 1) * A_HD]) for hh in range(A_H)]
        + [_row(dg_k[0, hh * A_HD:(hh + 1) * A_HD]) for hh in range(A_KV)]
        + [jnp.zeros((PACK_COLS,), F32)] * (SMALL_ROWS - 18))

    me = 4 * lax.axis_index("x") + 2 * lax.axis_index("y") + lax.axis_index("c")
    own_in = lax.dynamic_index_in_dim(gpack_in, me, axis=0, keepdims=False)
    own = lax.dynamic_index_in_dim(gpack, me, axis=0, keepdims=False)
    land_in, land, sland = _exchange_grads([_bf(gpack_in), _bf(gpack), small[None]])
    in_sh = _adamw_big(land_in, own_in, w_in[0], m_w_in[0], v_w_in[0], "adamw_w_in")
    g_sh, d_sh, m_sh, v_sh = _adamw_big(land, own, wshard, _pack_shards([args["m_" + n] for n in big_names]),
                                        _pack_shards([args["v_" + n] for n in big_names]), "adamw_shard")
    g_sm, d_sm, m_sm, v_sm, loss_row = _adamw_small(
        sland, _pack_small({n: args[n] for n in _SMALL}), _pack_small({n: args["m_" + n] for n in _SMALL}),
        _pack_small({n: args["v_" + n] for n in _SMALL}))

    names = ["mix_norm", "w_in", "attn_q_norm", "attn_k_norm", "ret_decay_logit", "ret_norm_gain", "w_attn_o",
             "w_ret_o", "w_out", "mlp_norm", "w_up", "w_down", "ple_norm", "w_ple_gate", "w_ple", "final_norm"]
    like = {n: args[n] for n in _SMALL}
    outs = [loss_row[0, 0], grad_x[None]]
    for big, sm, w_in_part in ((g_sh, g_sm, in_sh[0]), (d_sh, d_sm, in_sh[1]), (m_sh, m_sm, in_sh[2]),
                               (v_sh, v_sm, in_sh[3])):
        table = {**_unpack_shard(big), **_unpack_small(sm, like), "w_in": w_in_part[None]}
        outs += [table[n] for n in names]
    return tuple(outs)
```

```python
import functools

import jax
import jax.numpy as jnp
from jax import lax
from jax.experimental import pallas as pl
from jax.experimental.pallas import tpu as pltpu

F32 = jnp.float32
_MXU = jnp.bfloat16

D = 1024
PLE = 256
GRID_W = 64
A_HD = 64
A_H = 8
A_KV = 2
A_G = A_H // A_KV
AQ_W = A_H * A_HD
AKV_W = A_KV * A_HD
R_HD = 128
R_H = 4
R_W = R_H * R_HD
IN_W = AQ_W + 2 * AKV_W + 4 * R_W + 2 * D
PA_W = AQ_W + 2 * AKV_W
PR_W = 4 * R_W
PG_W = 2 * D
FF = 4 * D
CHUNK = 128
ROPE_THETA = 10000.0
EPS = 1e-6
GN_EPS = 1e-5
N_DEV = 8

LR, B1, B2, ADAM_EPS, WD, STEP = 0.001, 0.9, 0.999, 1e-08, 0.01, 10

LANES = 128
PACK_COLS = 1024
SMALL_ROWS = 24
HIGHEST = lax.Precision.HIGHEST


def _tile(n, pref):
    t = min(n, pref)
    assert n % t == 0, (n, t)
    return t


def _bf(a):
    return a.astype(_MXU)


def _mm(a, b):
    return jnp.dot(_bf(a), _bf(b), preferred_element_type=F32)


def _mm_nt(a, b):
    return lax.dot_general(_bf(a), _bf(b), (((1,), (1,)), ((), ())), preferred_element_type=F32)


def _mm_tn(a, b):
    return lax.dot_general(_bf(a), _bf(b), (((0,), (0,)), ((), ())), preferred_element_type=F32)


def _sigmoid(z):
    return 1.0 / (1.0 + jnp.exp(-z))


def _rms(x):
    r = lax.rsqrt(jnp.mean(x * x, axis=-1, keepdims=True) + EPS)
    return x * r, r


def _rms_bwd(n, r, gain, dy):
    dn = dy * gain
    return r * (dn - n * jnp.mean(dn * n, axis=-1, keepdims=True))


def _swap_halves(x, half):
    n = x.shape[-1]
    lane = lax.broadcasted_iota(jnp.int32, x.shape, x.ndim - 1)
    first = (lane % (2 * half)) < half
    return jnp.where(first, pltpu.roll(x, n - half, axis=1), pltpu.roll(x, half, axis=1))


def _rope(x, cos, sin, half):
    return x * cos + _swap_halves(x, half) * sin


def _cat(t, reps):
    return jnp.concatenate([t] * reps, axis=1)


def _full(shape):
    nd = len(shape)
    return pl.BlockSpec(shape, lambda *_: (0,) * nd)


def _rope_tables(seq):
    def tab(head_dim):
        n_axis = head_dim // 4
        freqs = ROPE_THETA ** (-jnp.arange(n_axis, dtype=F32) / n_axis)
        rows = seq // GRID_W
        row = jnp.repeat(jnp.arange(rows, dtype=F32), GRID_W)
        col = jnp.tile(jnp.arange(GRID_W, dtype=F32), rows)
        ang = jnp.concatenate([row[:, None] * freqs, col[:, None] * freqs], axis=-1)
        c, s = jnp.cos(ang), jnp.sin(ang)
        return jnp.concatenate([c, c], axis=-1), jnp.concatenate([-s, s], axis=-1)
    ca, sa = tab(A_HD)
    cr, sr = tab(R_HD)
    return jnp.tile(ca, (1, 2)), jnp.tile(sa, (1, 2)), cr, sr


def _seg_mean_matrix():
    i = jnp.arange(AQ_W) // A_HD
    return (i[:, None] == i[None, :]).astype(F32) / A_HD


def _mesh_pos():
    return lax.axis_index("x"), lax.axis_index("y"), lax.axis_index("c")


def _all_gather(shards):
    n = len(shards)

    def body(*refs):
        x_refs, out_refs = refs[:n], refs[n:2 * n]
        send_sems, recv_sems, local_sems = refs[2 * n:]
        x, y, c = _mesh_pos()
        me, sibling = (x, y, c), (x, y, 1 - c)
        chips = [(1 - x, y), (x, 1 - y), (1 - x, 1 - y)]

        def slot(t, px, py, pc):
            return out_refs[t].at[4 * px + 2 * py + pc]

        def copy(t, k, block, to, src=None):
            return pltpu.make_async_remote_copy(
                src_ref=slot(t, *block) if src is None else src, dst_ref=slot(t, *block),
                send_sem=send_sems.at[7 * t + k], recv_sem=recv_sems.at[7 * t + k],
                device_id=to, device_id_type=pl.DeviceIdType.MESH)

        mine = [pltpu.make_async_copy(x_refs[t], slot(t, *me), local_sems.at[t]) for t in range(n)]
        for cp in mine:
            cp.start()
        first = []
        for t in range(n):
            first.append(copy(t, 0, me, sibling, src=x_refs[t]))
            first += [copy(t, 1 + j, me, (*chip, c), src=x_refs[t]) for j, chip in enumerate(chips)]
        for cp in first:
            cp.start()
        passed = []
        for t in range(n):
            for j, chip in enumerate(chips):
                copy(t, 1 + j, (*chip, c), me).wait_recv()
                passed.append(copy(t, 4 + j, (*chip, c), sibling))
                passed[-1].start()
        for t in range(n):
            copy(t, 0, sibling, me).wait_recv()
            for j, chip in enumerate(chips):
                copy(t, 4 + j, (*chip, 1 - c), me).wait_recv()
        for cp in first + passed:
            cp.wait_send()
        for cp in mine:
            cp.wait()

    anyspec = pl.BlockSpec(memory_space=pl.ANY)
    return pl.pallas_call(
        body, name="all_gather_weights",
        out_shape=tuple(jax.ShapeDtypeStruct((N_DEV,) + s.shape, s.dtype) for s in shards),
        in_specs=[anyspec] * n, out_specs=(anyspec,) * n,
        scratch_shapes=[pltpu.SemaphoreType.DMA((7 * n,)), pltpu.SemaphoreType.DMA((7 * n,)),
                        pltpu.SemaphoreType.DMA((n,))],
    )(*shards)


def _exchange_grads(packs):
    n = len(packs)

    def body(*refs):
        g_refs, land_refs = refs[:n], refs[n:2 * n]
        send_sems, recv_sems, local_sems = refs[2 * n:]
        x, y, c = _mesh_pos()
        me = 4 * x + 2 * y + c

        def row(t, j):
            return g_refs[t].at[j if packs[t].shape[0] == N_DEV else 0]

        own = [pltpu.make_async_copy(row(t, me), land_refs[t].at[me], local_sems.at[t]) for t in range(n)]
        for cp in own:
            cp.start()
        sends = []
        for k in range(1, N_DEV):
            peer = (x ^ ((k >> 2) & 1), y ^ ((k >> 1) & 1), c ^ (k & 1))
            pidx = 4 * peer[0] + 2 * peer[1] + peer[2]
            for t in range(n):
                sends.append(pltpu.make_async_remote_copy(
                    src_ref=row(t, pidx), dst_ref=land_refs[t].at[me],
                    send_sem=send_sems.at[7 * t + k - 1], recv_sem=recv_sems.at[7 * t + k - 1],
                    device_id=peer, device_id_type=pl.DeviceIdType.MESH))
                sends[-1].start()
        for k in range(1, N_DEV):
            peer = (x ^ ((k >> 2) & 1), y ^ ((k >> 1) & 1), c ^ (k & 1))
            pidx = 4 * peer[0] + 2 * peer[1] + peer[2]
            for t in range(n):
                pltpu.make_async_remote_copy(
                    src_ref=row(t, me), dst_ref=land_refs[t].at[pidx],
                    send_sem=send_sems.at[7 * t + k - 1], recv_sem=recv_sems.at[7 * t + k - 1],
                    device_id=peer, device_id_type=pl.DeviceIdType.MESH).wait_recv()
        for cp in sends:
            cp.wait_send()
        for cp in own:
            cp.wait()

    anyspec = pl.BlockSpec(memory_space=pl.ANY)
    return pl.pallas_call(
        body, name="exchange_grads",
        out_shape=tuple(jax.ShapeDtypeStruct((N_DEV,) + g.shape[1:], g.dtype) for g in packs),
        in_specs=[anyspec] * n, out_specs=(anyspec,) * n,
        scratch_shapes=[pltpu.SemaphoreType.DMA((7 * n,)), pltpu.SemaphoreType.DMA((7 * n,)),
                        pltpu.SemaphoreType.DMA((n,))],
    )(*packs)


def _in_proj(x, gain, w):
    seq = x.shape[0]
    tm = _tile(seq, 256)

    def body(x_ref, g_ref, w_ref, pa_ref, pr_ref, pg_ref, h_ref):
        n, _ = _rms(x_ref[...])
        h = _bf(n * g_ref[...])
        h_ref[...] = h
        pa_ref[...] = _mm(h, w_ref[:, 0:PA_W])
        pr_ref[...] = _mm(h, w_ref[:, PA_W:PA_W + PR_W])
        pg_ref[...] = _mm(h, w_ref[:, PA_W + PR_W:IN_W])

    row = lambda w_: pl.BlockSpec((tm, w_), lambda i: (i, 0))
    return pl.pallas_call(
        body, name="in_proj", grid=(seq // tm,),
        in_specs=[row(D), _full((1, D)), _full((D, IN_W))],
        out_specs=(row(PA_W), row(PR_W), row(PG_W), row(D)),
        out_shape=(jax.ShapeDtypeStruct((seq, PA_W), F32), jax.ShapeDtypeStruct((seq, PR_W), F32),
                   jax.ShapeDtypeStruct((seq, PG_W), F32), jax.ShapeDtypeStruct((seq, D), _MXU)),
    )(x, gain, w)


def _qk_prep(pa, pr, gq, gk, seg, ca, sa, cr, sr):
    seq = pa.shape[0]
    tm = _tile(seq, 256)

    def body(pa_ref, pr_ref, gq_ref, gk_ref, seg_ref, ca_ref, sa_ref, cr_ref, sr_ref,
             qh_ref, kh_ref, v_ref, rq_ref, rk_ref):
        q = pa_ref[:, 0:AQ_W]
        k = pa_ref[:, AQ_W:AQ_W + AKV_W]
        v_ref[...] = _bf(pa_ref[:, AQ_W + AKV_W:PA_W])
        ca_, sa_ = ca_ref[...], sa_ref[...]
        msq = jnp.dot(q * q, seg_ref[...], precision=HIGHEST, preferred_element_type=F32)
        qn = q * lax.rsqrt(msq + EPS) * gq_ref[...]
        qh_ref[...] = _bf(_rope(qn, _cat(ca_, 4), _cat(sa_, 4), A_HD // 2) * (A_HD ** -0.5))
        msk = jnp.dot(k * k, seg_ref[0:AKV_W, 0:AKV_W], precision=HIGHEST, preferred_element_type=F32)
        kn = k * lax.rsqrt(msk + EPS) * gk_ref[...]
        kh_ref[...] = _bf(_rope(kn, ca_, sa_, A_HD // 2))
        cr_, sr_ = _cat(cr_ref[...], 4), _cat(sr_ref[...], 4)
        rq_ref[...] = _rope(pr_ref[:, 0:R_W], cr_, sr_, R_HD // 2) * (R_HD ** -0.5)
        rk_ref[...] = _rope(pr_ref[:, R_W:2 * R_W], cr_, sr_, R_HD // 2)

    row = lambda w_: pl.BlockSpec((tm, w_), lambda i: (i, 0))
    return pl.pallas_call(
        body, name="qk_prep", grid=(seq // tm,),
        in_specs=[row(PA_W), row(2 * R_W), _full((1, AQ_W)), _full((1, AKV_W)), _full((AQ_W, AQ_W)),
                  row(LANES), row(LANES), row(LANES), row(LANES)],
        out_specs=(row(AQ_W), row(AKV_W), row(AKV_W), row(R_W), row(R_W)),
        out_shape=(jax.ShapeDtypeStruct((seq, AQ_W), _MXU), jax.ShapeDtypeStruct((seq, AKV_W), _MXU),
                   jax.ShapeDtypeStruct((seq, AKV_W), _MXU), jax.ShapeDtypeStruct((seq, R_W), F32),
                   jax.ShapeDtypeStruct((seq, R_W), F32)),
    )(pa, pr, gq, gk, seg, ca, sa, cr, sr)


def _chunk_t(a, tk):
    seq = a.shape[0]
    return a.reshape(seq // tk, tk, a.shape[1]).transpose(0, 2, 1)


def _heads_to_rows(t):
    return t.transpose(2, 0, 1).reshape(t.shape[2], AQ_W)


def _attn_fwd(qt8, k2, vta, tq, tk):
    seq = k2.shape[0]
    nck = seq // tk
    rows = A_G * tq
    vrows = vta.shape[2]
    rb = _tile(tk, 256)
    assert nck % 2 == 0, nck

    def body(qt_ref, k_ref, vt_ref, o_ref, lse_ref, m_sc, acc_sc, qtp_sc, s_a, s_b, p_a, p_b, al_a, al_b):
        g = pl.program_id(0)
        qtp_sc[...] = jnp.zeros_like(qtp_sc)
        qtp_sc[pl.ds(pl.multiple_of(g * A_HD, A_HD), A_HD), :] = jnp.concatenate(
            [qt_ref[a] for a in range(A_G)], axis=1)
        m_sc[...] = jnp.full((1, rows), -jnp.inf, F32)
        acc_sc[...] = jnp.zeros_like(acc_sc)

        def scores(c):
            kc = k_ref[pl.ds(pl.multiple_of(c * tk, tk), tk), :]
            return _mm(kc, qtp_sc[...])

        def stage(c, s_cur, s_nxt, p_cur, p_prv, al_cur, al_prv, first=False, last=False):
            if not last:
                s_nxt[...] = scores(c + 1)
            if not first:
                acc_sc[...] = al_prv[...] * acc_sc[...] + _mm(vt_ref[0, c - 1], p_prv[...])
            for b in range(rows // LANES):
                cs = slice(b * LANES, (b + 1) * LANES)
                m_old = m_sc[:, cs]
                m_new = m_old
                for r in range(0, tk, rb):
                    m_new = jnp.maximum(m_new, jnp.max(s_cur[r:r + rb, cs], axis=0, keepdims=True))
                for r in range(0, tk, rb):
                    p_cur[r:r + rb, cs] = _bf(jnp.exp(s_cur[r:r + rb, cs] - m_new))
                al_cur[:, cs] = jnp.exp(m_old - m_new)
                m_sc[:, cs] = m_new

        s_a[...] = scores(0)
        stage(0, s_a, s_b, p_a, p_b, al_a, al_b, first=True)

        def pair(j, carry):
            stage(2 * j + 1, s_b, s_a, p_b, p_a, al_b, al_a)
            stage(2 * j + 2, s_a, s_b, p_a, p_b, al_a, al_b)
            return carry

        lax.fori_loop(0, nck // 2 - 1, pair, 0)
        stage(nck - 1, s_b, s_a, p_b, p_a, al_b, al_a, last=True)
        acc = al_b[...] * acc_sc[...] + _mm(vt_ref[0, nck - 1], p_b[...])
        l = acc[A_HD:A_HD + 1, :]
        lse = m_sc[...] + jnp.log(l)
        out = acc[0:A_HD, :] * (1.0 / l)
        for a in range(A_G):
            o_ref[a] = out[:, a * tq:(a + 1) * tq]
            lse_ref[a] = lse[:, a * tq:(a + 1) * tq]

    return pl.pallas_call(
        body, name="attn_fwd", grid=(A_KV, seq // tq),
        in_specs=[pl.BlockSpec((A_G, A_HD, tq), lambda g, i: (g, 0, i)),
                  _full((seq, LANES)), pl.BlockSpec((1, nck, vrows, tk), lambda g, i: (g, 0, 0, 0))],
        out_specs=(pl.BlockSpec((A_G, A_HD, tq), lambda g, i: (g, 0, i)),
                   pl.BlockSpec((A_G, 1, tq), lambda g, i: (g, 0, i))),
        out_shape=(jax.ShapeDtypeStruct((A_H, A_HD, seq), F32), jax.ShapeDtypeStruct((A_H, 1, seq), F32)),
        scratch_shapes=[pltpu.VMEM((1, rows), F32), pltpu.VMEM((vrows, rows), F32), pltpu.VMEM((LANES, rows), _MXU),
                        pltpu.VMEM((tk, rows), F32), pltpu.VMEM((tk, rows), F32),
                        pltpu.VMEM((tk, rows), _MXU), pltpu.VMEM((tk, rows), _MXU),
                        pltpu.VMEM((1, rows), F32), pltpu.VMEM((1, rows), F32)],
    )(qt8, k2, vta)


def _attn_bwd(qt8, ot, dot_, lse, k2, v2, k2t, tq, tk, ksplit):
    seq = k2.shape[0]
    sh = seq // ksplit
    nck = sh // tk
    rows = A_G * tq
    rb = _tile(tk, 128)
    assert nck % 2 == 0, nck

    def body(qt_ref, ot_ref, dot_ref, lse_ref, k_ref, v_ref, kt_ref,
             dk_ref, dv_ref, dq_ref, dq_sc, qtp_sc, dotp_sc,
             s_a, s_b, dp_a, dp_b, ds_a, ds_b, pt_a, pt_b, dst_a, dst_b):
        g = pl.program_id(1)
        hrows = pl.ds(pl.multiple_of(g * A_HD, A_HD), A_HD)

        @pl.when(pl.program_id(2) == 0)
        def _():
            dk_ref[...] = jnp.zeros_like(dk_ref)
            dv_ref[...] = jnp.zeros_like(dv_ref)

        lse_row = jnp.concatenate([lse_ref[a] for a in range(A_G)], axis=1)
        dd = jnp.concatenate([jnp.sum(ot_ref[a] * dot_ref[a], axis=0, keepdims=True)
                              for a in range(A_G)], axis=1)
        qtp_sc[...] = jnp.zeros_like(qtp_sc)
        dotp_sc[...] = jnp.zeros_like(dotp_sc)
        qtp_sc[hrows, :] = jnp.concatenate([qt_ref[a] for a in range(A_G)], axis=1)
        dotp_sc[hrows, :] = _bf(jnp.concatenate([dot_ref[a] for a in range(A_G)], axis=1))
        dq_sc[...] = jnp.zeros_like(dq_sc)

        def products(c, buf):
            sl = pl.ds(pl.multiple_of(c * tk, tk), tk)
            buf["s"][...] = _mm(k_ref[sl, :], qtp_sc[...])
            buf["dp"][...] = _bf(_mm(v_ref[sl, :], dotp_sc[...]))

        def accumulate(c, buf):
            dq_sc[...] += _mm(kt_ref[c, hrows, :], buf["ds"][...])
            dv_ref[0, c] += _mm(dotp_sc[hrows, :], buf["pt"][...])
            dk_ref[0, c] += _mm(qtp_sc[hrows, :], buf["dst"][...])

        def stage(c, cur, oth, first=False, last=False):
            if not last:
                products(c + 1, oth)
            if not first:
                accumulate(c - 1, oth)
            for r in range(0, tk, rb):
                p = jnp.exp(cur["s"][r:r + rb, :] - lse_row)
                ds = _bf(p * (cur["dp"][r:r + rb, :].astype(F32) - dd))
                cur["ds"][r:r + rb, :] = ds
                cur["pt"][:, r:r + rb] = _bf(p).T
                cur["dst"][:, r:r + rb] = ds.T

        buf_a = dict(s=s_a, dp=dp_a, ds=ds_a, pt=pt_a, dst=dst_a)
        buf_b = dict(s=s_b, dp=dp_b, ds=ds_b, pt=pt_b, dst=dst_b)
        products(0, buf_a)
        stage(0, buf_a, buf_b, first=True)

        def pair(j, carry):
            stage(2 * j + 1, buf_b, buf_a)
            stage(2 * j + 2, buf_a, buf_b)
            return carry

        lax.fori_loop(0, nck // 2 - 1, pair, 0)
        stage(nck - 1, buf_b, buf_a, last=True)
        accumulate(nck - 1, buf_b)
        for a in range(A_G):
            dq_ref[0, a] = dq_sc[:, a * tq:(a + 1) * tq]

    tspec = pl.BlockSpec((A_G, A_HD, tq), lambda s, g, i: (g, 0, i))
    kspec = pl.BlockSpec((sh, LANES), lambda s, g, i: (s, 0))
    gspec = pl.BlockSpec((1, nck, A_HD, tk), lambda s, g, i: (g, s, 0, 0))
    gshape = jax.ShapeDtypeStruct((A_KV, seq // tk, A_HD, tk), F32)
    big = lambda dt: pltpu.VMEM((tk, rows), dt)
    bigt = pltpu.VMEM((rows, tk), _MXU)
    return pl.pallas_call(
        body, name="attn_bwd", grid=(ksplit, A_KV, seq // tq),
        in_specs=[tspec, tspec, tspec, pl.BlockSpec((A_G, 1, tq), lambda s, g, i: (g, 0, i)),
                  kspec, kspec, pl.BlockSpec((nck, LANES, tk), lambda s, g, i: (s, 0, 0))],
        out_specs=(gspec, gspec, pl.BlockSpec((1, A_G, A_HD, tq), lambda s, g, i: (s, g, 0, i))),
        out_shape=(gshape, gshape, jax.ShapeDtypeStruct((ksplit, A_H, A_HD, seq), F32)),
        scratch_shapes=[pltpu.VMEM((A_HD, rows), F32), pltpu.VMEM((LANES, rows), _MXU), pltpu.VMEM((LANES, rows), _MXU),
                        big(F32), big(F32), big(_MXU), big(_MXU), big(_MXU), big(_MXU),
                        bigt, bigt, bigt, bigt],
    )(qt8, ot, dot_, lse, k2, v2, k2t)


def _chunks_to_rows(t):
    return t.transpose(1, 3, 0, 2).reshape(t.shape[1] * t.shape[3], AKV_W)


def _ret_tables(zb):
    c = CHUNK

    def body(z_ref, m_ref, mw_ref, qd_ref, qdw_ref, kd_ref, kdw_ref, g_ref, gw_ref):
        fwd = pl.program_id(0) < R_H
        z = z_ref[0]
        lam = jnp.minimum(z, 0.0) - jnp.log(1.0 + jnp.exp(-jnp.abs(z)))
        i = lax.broadcasted_iota(jnp.int32, (c, c), 0).astype(F32)
        j = lax.broadcasted_iota(jnp.int32, (c, c), 1).astype(F32)
        diff = jnp.where(fwd, i - j, j - i)
        keep = diff >= jnp.where(fwd, 0.0, 1.0)
        dist = jnp.maximum(diff, 0.0)
        m = jnp.where(keep, jnp.exp(lam * dist), 0.0)
        m_ref[0] = m
        mw_ref[0] = m * dist
        fq = jnp.where(fwd, i + 1.0, c - i)
        qd = jnp.exp(lam * fq)
        qd_ref[0] = qd
        qdw_ref[0] = qd * fq
        fk = jnp.where(fwd, c - 1.0 - i, i)
        kd = jnp.exp(lam * fk)
        kd_ref[0] = kd
        kdw_ref[0] = kd * fk
        gdec = jnp.exp(lam * c)
        g_ref[0] = gdec
        gw_ref[0] = gdec * c

    big = pl.BlockSpec((1, c, c), lambda t: (t, 0, 0))
    vec = pl.BlockSpec((1, 1, LANES), lambda t: (t, 0, 0))
    bshape = jax.ShapeDtypeStruct((2 * R_H, c, c), F32)
    vshape = jax.ShapeDtypeStruct((2 * R_H, 1, LANES), F32)
    return pl.pallas_call(
        body, name="ret_tables", grid=(2 * R_H,), in_specs=[vec],
        out_specs=(big, big, big, big, big, big, vec, vec),
        out_shape=(bshape,) * 6 + (vshape, vshape),
    )(zb)


def _ret_fwd(rq, rk, pr, m, qd, kd, gdec, cb):
    seq = rq.shape[0]
    c = CHUNK
    ns = seq // (cb * c)

    def body(q_ref, k_ref, v_ref, m_ref, qd_ref, kd_ref, g_ref, y_ref, pst_ref, p_sc):
        d = pl.program_id(0)

        @pl.when(pl.program_id(1) == 0)
        def _():
            p_sc[...] = jnp.zeros_like(p_sc)

        def chunk(j, carry):
            cc = jnp.where(d == 0, j, cb - 1 - j)
            sl = pl.ds(pl.multiple_of(cc * c, c), c)
            heads = [slice(h * R_HD, (h + 1) * R_HD) for h in range(R_H)]
            qk = [_mm_nt(q_ref[sl, hs], k_ref[sl, hs]) for hs in heads]
            qp = [_mm(q_ref[sl, hs] * qd_ref[h], p_sc[h]) for h, hs in enumerate(heads)]
            kv = [_mm_tn(k_ref[sl, hs] * kd_ref[h], v_ref[sl, hs]) for h, hs in enumerate(heads)]
            for h, hs in enumerate(heads):
                p = p_sc[h]
                pst_ref[h, cc] = p
                y_ref[0, sl, hs] = _mm(qk[h] * m_ref[h], v_ref[sl, hs]) + qp[h]
                p_sc[h] = p * g_ref[h] + kv[h]
            return carry

        lax.fori_loop(0, cb, chunk, 0)

    def step(d, n):
        return d * (ns - 1 - n) + (1 - d) * n

    blk = lambda off: pl.BlockSpec((cb * c, R_W), lambda d, n: (step(d, n), off))
    big = pl.BlockSpec((R_H, c, c), lambda d, n: (d, 0, 0))
    vec = pl.BlockSpec((R_H, 1, LANES), lambda d, n: (d, 0, 0))
    return pl.pallas_call(
        body, name="ret_fwd", grid=(2, ns),
        in_specs=[blk(0), blk(0), blk(2), big, big, big, vec],
        out_specs=(pl.BlockSpec((1, cb * c, R_W), lambda d, n: (d, step(d, n), 0)),
                   pl.BlockSpec((R_H, cb, R_HD, R_HD), lambda d, n: (d, step(d, n), 0, 0))),
        out_shape=(jax.ShapeDtypeStruct((2, seq, R_W), F32),
                   jax.ShapeDtypeStruct((2 * R_H, seq // c, R_HD, R_HD), F32)),
        scratch_shapes=[pltpu.VMEM((R_H, R_HD, R_HD), F32)],
    )(rq, rk, pr, m, qd, kd, gdec)


def _ret_bwd(rq, rk, pr, dry, pst, m, mw, qd, qdw, kd, kdw, gdec, gw, cb):
    seq = rq.shape[0]
    c = CHUNK
    ns = seq // (cb * c)

    def body(q_ref, k_ref, v_ref, dy_ref, pst_ref, m_ref, mw_ref, qd_ref, qdw_ref, kd_ref, kdw_ref,
             g_ref, gw_ref, dq_ref, dk_ref, dv_ref, dlam_ref, r_sc, acc_sc, e_sc, g_sc):
        d = pl.program_id(0)
        n = pl.program_id(1)

        @pl.when(n == 0)
        def _():
            r_sc[...] = jnp.zeros_like(r_sc)
            acc_sc[...] = jnp.zeros_like(acc_sc)
            e_sc[...] = jnp.zeros_like(e_sc)
            g_sc[...] = jnp.zeros_like(g_sc)

        def chunk(j, carry):
            cc = jnp.where(d == 0, cb - 1 - j, j)
            sl = pl.ds(pl.multiple_of(cc * c, c), c)
            heads = [slice(h * R_HD, (h + 1) * R_HD) for h in range(R_H)]
            first = []
            for h, hs in enumerate(heads):
                q, k, v, dy = q_ref[sl, hs], k_ref[sl, hs], v_ref[sl, hs], dy_ref[sl, hs]
                r = r_sc[h]
                first.append((_mm_nt(q, k), _mm_nt(dy, v), _mm_nt(dy, pst_ref[h, cc]), _mm_nt(v, r),
                              _mm(k * kd_ref[h], r), _mm_tn(q * qd_ref[h], dy)))
            for h, hs in enumerate(heads):
                qk, ds, dyp, vr, kr, qdy = first[h]
                q, k, dy = q_ref[sl, hs], k_ref[sl, hs], dy_ref[sl, hs]
                r = r_sc[h]
                da = ds * m_ref[h]
                dv_ref[0, sl, hs] = _mm_tn(qk * m_ref[h], dy) + kr
                dq_ref[0, sl, hs] = _mm(da, k) + dyp * qd_ref[h]
                dk_ref[0, sl, hs] = _mm_tn(da, q) + vr * kd_ref[h]
                acc_sc[h] += dyp * q * qdw_ref[h] + vr * k * kdw_ref[h]
                e_sc[h] += ds * qk * mw_ref[h]
                g_sc[h] += r * pst_ref[h, cc]
                r_sc[h] = r * g_ref[h] + qdy
            return carry

        lax.fori_loop(0, cb, chunk, 0)

        @pl.when(n == ns - 1)
        def _():
            for h in range(R_H):
                tot = jnp.sum(jnp.sum(acc_sc[h] + e_sc[h] + g_sc[h] * gw_ref[h], axis=0, keepdims=True),
                              axis=1, keepdims=True)
                dlam_ref[h] = jnp.broadcast_to(tot, (1, LANES))

    def step(d, n):
        return d * n + (1 - d) * (ns - 1 - n)

    blk = lambda off: pl.BlockSpec((cb * c, R_W), lambda d, n: (step(d, n), off))
    big = pl.BlockSpec((R_H, c, c), lambda d, n: (d, 0, 0))
    vec = pl.BlockSpec((R_H, 1, LANES), lambda d, n: (d, 0, 0))
    out = pl.BlockSpec((1, cb * c, R_W), lambda d, n: (d, step(d, n), 0))
    oshape = jax.ShapeDtypeStruct((2, seq, R_W), F32)
    sq = pltpu.VMEM((R_H, R_HD, R_HD), F32)
    return pl.pallas_call(
        body, name="ret_bwd", grid=(2, ns),
        in_specs=[blk(0), blk(0), blk(2), blk(0),
                  pl.BlockSpec((R_H, cb, R_HD, R_HD), lambda d, n: (d, step(d, n), 0, 0)),
                  big, big, big, big, big, big, vec, vec],
        out_specs=(out, out, out, vec),
        out_shape=(oshape, oshape, oshape, jax.ShapeDtypeStruct((2 * R_H, 1, LANES), F32)),
        scratch_shapes=[sq, sq, sq, sq],
    )(rq, rk, pr, dry, pst, m, mw, qd, qdw, kd, kdw, gdec, gw)


def _group_norm(ry):
    yn, rs = [], []
    for h in range(R_H):
        s = ry[:, h * R_HD:(h + 1) * R_HD]
        mu = jnp.mean(s, axis=-1, keepdims=True)
        cen = s - mu
        r = lax.rsqrt(jnp.mean(cen * cen, axis=-1, keepdims=True) + GN_EPS)
        yn.append(cen * r)
        rs.append(r)
    return yn, rs


def _merge_fwd(x, o, y2, pr, pg, gain_r, wao, wro, wout):
    seq = x.shape[0]
    tm = _tile(seq, 256)

    def body(x_ref, o_ref, yf_ref, yb_ref, rg_ref, ga_ref, gr_ref, gn_ref, wao_ref, wro_ref, wout_ref,
             x1_ref, mg_ref, ri_ref):
        yn, _ = _group_norm(yf_ref[0] + yb_ref[0])
        rg = rg_ref[...]
        ret_in = jnp.concatenate(yn, axis=1) * gn_ref[...] * (rg * _sigmoid(rg))
        ri_ref[...] = _bf(ret_in)
        attn_out = _mm(o_ref[...], wao_ref[...])
        ret_out = _mm(ret_in, wro_ref[...])
        merged = _sigmoid(ga_ref[...]) * attn_out + _sigmoid(gr_ref[...]) * ret_out
        mg_ref[...] = _bf(merged)
        x1_ref[...] = x_ref[...] + _mm(merged, wout_ref[...])

    row = lambda w_, j=0: pl.BlockSpec((tm, w_), lambda i: (i, j))
    ydir = lambda d: pl.BlockSpec((1, tm, R_W), lambda i: (d, i, 0))
    return pl.pallas_call(
        body, name="merge_fwd", grid=(seq // tm,),
        in_specs=[row(D), row(AQ_W), ydir(0), ydir(1), row(R_W, 3), row(D, 0), row(D, 1),
                  _full((1, R_W)), _full((AQ_W, D)), _full((R_W, D)), _full((D, D))],
        out_specs=(row(D), row(D), row(R_W)),
        out_shape=(jax.ShapeDtypeStruct((seq, D), F32), jax.ShapeDtypeStruct((seq, D), _MXU),
                   jax.ShapeDtypeStruct((seq, R_W), _MXU)),
    )(x, o, y2, y2, pr, pg, pg, gain_r, wao, wro, wout)


def _mlp_fwd(x1, gain, wup, wdown):
    seq = x1.shape[0]
    tm = _tile(seq, 512)
    fc = 512
    nfc = FF // fc

    def body(x_ref, g_ref, wu_ref, wd_ref, x2_ref, hm_sc, acc_sc):
        c = pl.program_id(1)

        @pl.when(c == 0)
        def _():
            n, _ = _rms(x_ref[...])
            hm_sc[...] = _bf(n * g_ref[...])
            acc_sc[...] = jnp.zeros_like(acc_sc)

        halves = (slice(0, fc // 2), slice(fc // 2, fc))
        ups = [jnp.maximum(_mm(hm_sc[...], wu_ref[:, hs]), 0.0) for hs in halves]
        acc_sc[...] += _mm(ups[0] * ups[0], wd_ref[halves[0], :]) + _mm(ups[1] * ups[1], wd_ref[halves[1], :])

        @pl.when(c == nfc - 1)
        def _():
            x2_ref[...] = x_ref[...] + acc_sc[...]

    return pl.pallas_call(
        body, name="mlp_fwd", grid=(seq // tm, nfc),
        in_specs=[pl.BlockSpec((tm, D), lambda i, c: (i, 0)), pl.BlockSpec((1, D), lambda i, c: (0, 0)),
                  pl.BlockSpec((D, fc), lambda i, c: (0, c)), pl.BlockSpec((fc, D), lambda i, c: (c, 0))],
        out_specs=pl.BlockSpec((tm, D), lambda i, c: (i, 0)),
        out_shape=jax.ShapeDtypeStruct((seq, D), F32),
        scratch_shapes=[pltpu.VMEM((tm, D), _MXU), pltpu.VMEM((tm, D), F32)],
    )(x1, gain, wup, wdown)


def _ple_loss(x2, p, tgt, g_ple, g_fin, wpg, wpgt, wple):
    seq = x2.shape[0]
    tm = _tile(seq, 256)

    def body(x2_ref, p_ref, t_ref, gp_ref, gf_ref, wpg_ref, wpgt_ref, wple_ref,
             dx2_ref, de_ref, dz_ref, hp_ref, loss_ref, dgf_ref, dgp_ref):
        @pl.when(pl.program_id(0) == 0)
        def _():
            loss_ref[...] = jnp.zeros_like(loss_ref)
            dgf_ref[...] = jnp.zeros_like(dgf_ref)
            dgp_ref[...] = jnp.zeros_like(dgp_ref)

        x2 = x2_ref[...]
        gp, gf = gp_ref[...], gf_ref[...]
        n2, r2 = _rms(x2)
        hp = _bf(n2 * gp)
        hp_ref[...] = hp
        gate = _sigmoid(_mm(hp, wpg_ref[...]))
        e = _mm(p_ref[...], wple_ref[...])
        x3 = x2 + gate * e
        n3, r3 = _rms(x3)
        diff = n3 * gf - t_ref[...]
        row_loss = jnp.mean(diff * diff, axis=-1, keepdims=True)
        loss_ref[...] += 0.5 * jnp.sum(row_loss, axis=0, keepdims=True)
        dy = diff * (1.0 / D)
        dgf_ref[...] += jnp.sum(dy * n3, axis=0, keepdims=True)
        dx3 = _rms_bwd(n3, r3, gf, dy)
        de_ref[...] = _bf(dx3 * gate)
        dz = dx3 * e * gate * (1.0 - gate)
        dz_ref[...] = _bf(dz)
        dhp = _mm(dz, wpgt_ref[...])
        dgp_ref[...] += jnp.sum(dhp * n2, axis=0, keepdims=True)
        dx2_ref[...] = dx3 + _rms_bwd(n2, r2, gp, dhp)

    row = lambda w_: pl.BlockSpec((tm, w_), lambda i: (i, 0))
    act = lambda dt: jax.ShapeDtypeStruct((seq, D), dt)
    return pl.pallas_call(
        body, name="ple_loss", grid=(seq // tm,),
        in_specs=[row(D), row(PLE), row(D), _full((1, D)), _full((1, D)),
                  _full((D, D)), _full((D, D)), _full((PLE, D))],
        out_specs=(row(D), row(D), row(D), row(D), _full((1, LANES)), _full((1, D)), _full((1, D))),
        out_shape=(act(F32), act(_MXU), act(_MXU), act(_MXU), jax.ShapeDtypeStruct((1, LANES), F32),
                   jax.ShapeDtypeStruct((1, D), F32), jax.ShapeDtypeStruct((1, D), F32)),
    )(x2, p, tgt, g_ple, g_fin, wpg, wpgt, wple)


def _mlp_bwd(x1, dx2, gain, wup, wdownt, wupt):
    seq = x1.shape[0]
    tm = _tile(seq, 512)
    fc = 512
    nfc = FF // fc

    def body(x_ref, dx2_ref, g_ref, wu_ref, wdt_ref, wut_ref,
             dx1_ref, a_ref, du_ref, hm_ref, dg_ref, dhm_sc):
        i = pl.program_id(0)
        c = pl.program_id(1)

        @pl.when((i == 0) & (c == 0))
        def _():
            dg_ref[...] = jnp.zeros_like(dg_ref)

        @pl.when(c == 0)
        def _():
            n, _ = _rms(x_ref[...])
            hm_ref[...] = _bf(n * g_ref[...])
            dhm_sc[...] = jnp.zeros_like(dhm_sc)

        halves = (slice(0, fc // 2), slice(fc // 2, fc))
        ups = [jnp.maximum(_mm(hm_ref[...], wu_ref[:, hs]), 0.0) for hs in halves]
        das = [_mm(dx2_ref[...], wdt_ref[:, hs]) for hs in halves]
        part = None
        for u, da, hs in zip(ups, das, halves):
            a_ref[:, hs] = _bf(u * u)
            du = _bf(da * (2.0 * u))
            du_ref[:, hs] = du
            t = _mm(du, wut_ref[hs, :])
            part = t if part is None else part + t
        dhm_sc[...] += part

        @pl.when(c == nfc - 1)
        def _():
            n, r = _rms(x_ref[...])
            dhm = dhm_sc[...]
            dg_ref[...] += jnp.sum(dhm * n, axis=0, keepdims=True)
            dx1_ref[...] = dx2_ref[...] + _rms_bwd(n, r, g_ref[...], dhm)

    rowd = pl.BlockSpec((tm, D), lambda i, c: (i, 0))
    rowf = pl.BlockSpec((tm, fc), lambda i, c: (i, c))
    return pl.pallas_call(
        body, name="mlp_bwd", grid=(seq // tm, nfc),
        in_specs=[rowd, rowd, pl.BlockSpec((1, D), lambda i, c: (0, 0)),
                  pl.BlockSpec((D, fc), lambda i, c: (0, c)), pl.BlockSpec((D, fc), lambda i, c: (0, c)),
                  pl.BlockSpec((fc, D), lambda i, c: (c, 0))],
        out_specs=(rowd, rowf, rowf, rowd, pl.BlockSpec((1, D), lambda i, c: (0, 0))),
        out_shape=(jax.ShapeDtypeStruct((seq, D), F32), jax.ShapeDtypeStruct((seq, FF), _MXU),
                   jax.ShapeDtypeStruct((seq, FF), _MXU), jax.ShapeDtypeStruct((seq, D), _MXU),
                   jax.ShapeDtypeStruct((1, D), F32)),
        scratch_shapes=[pltpu.VMEM((tm, D), F32)],
    )(x1, dx2, gain, wup, wdownt, wupt)


def _merge_bwd(dx1, o, y2, pr, pg, gain_r, wao, wro, woutt, waot, wrot):
    seq = dx1.shape[0]
    tm = _tile(seq, 256)

    def body(dx1_ref, o_ref, yf_ref, yb_ref, rg_ref, ga_ref, gr_ref, gn_ref, wao_ref, wro_ref,
             woutt_ref, waot_ref, wrot_ref,
             dpg_ref, dao_ref, dro_ref, do_ref, dry_ref, drg_ref, dgn_ref):
        @pl.when(pl.program_id(0) == 0)
        def _():
            dgn_ref[...] = jnp.zeros_like(dgn_ref)

        yn_l, rs_l = _group_norm(yf_ref[0] + yb_ref[0])
        yn = jnp.concatenate(yn_l, axis=1)
        rg = rg_ref[...]
        gn = gn_ref[...]
        sg = _sigmoid(rg)
        sil = rg * sg
        ret_in = yn * gn * sil
        attn_out = _mm(o_ref[...], wao_ref[...])
        ret_out = _mm(ret_in, wro_ref[...])
        sa = _sigmoid(ga_ref[...])
        sr = _sigmoid(gr_ref[...])
        dm = _mm(dx1_ref[...], woutt_ref[...])
        dpg_ref[:, 0:D] = _bf(dm * attn_out * sa * (1.0 - sa))
        dpg_ref[:, D:2 * D] = _bf(dm * ret_out * sr * (1.0 - sr))
        dao = _bf(dm * sa)
        dro = _bf(dm * sr)
        dao_ref[...] = dao
        dro_ref[...] = dro
        do_ref[...] = _mm(dao, waot_ref[...])
        dri = _mm(dro, wrot_ref[...])
        dgn_ref[...] += jnp.sum(dri * yn * sil, axis=0, keepdims=True)
        drg_ref[...] = _bf(dri * yn * gn * (sg * (1.0 + rg * (1.0 - sg))))
        dyn = dri * gn * sil
        dry = []
        for h in range(R_H):
            dh = dyn[:, h * R_HD:(h + 1) * R_HD]
            dry.append(rs_l[h] * (dh - jnp.mean(dh, axis=-1, keepdims=True)
                                  - yn_l[h] * jnp.mean(dh * yn_l[h], axis=-1, keepdims=True)))
        dry_ref[...] = jnp.concatenate(dry, axis=1)

    row = lambda w_, j=0: pl.BlockSpec((tm, w_), lambda i: (i, j))
    ydir = lambda d: pl.BlockSpec((1, tm, R_W), lambda i: (d, i, 0))
    return pl.pallas_call(
        body, name="merge_bwd", grid=(seq // tm,),
        in_specs=[row(D), row(AQ_W), ydir(0), ydir(1), row(R_W, 3), row(D, 0), row(D, 1),
                  _full((1, R_W)), _full((AQ_W, D)), _full((R_W, D)), _full((D, D)),
                  _full((D, AQ_W)), _full((D, R_W))],
        out_specs=(row(PG_W), row(D), row(D), row(AQ_W), row(R_W), row(R_W), _full((1, R_W))),
        out_shape=(jax.ShapeDtypeStruct((seq, PG_W), _MXU), jax.ShapeDtypeStruct((seq, D), _MXU),
                   jax.ShapeDtypeStruct((seq, D), _MXU), jax.ShapeDtypeStruct((seq, AQ_W), F32),
                   jax.ShapeDtypeStruct((seq, R_W), F32), jax.ShapeDtypeStruct((seq, R_W), _MXU),
                   jax.ShapeDtypeStruct((1, R_W), F32)),
    )(dx1, o, y2, y2, pr, pg, pg, gain_r, wao, wro, woutt, waot, wrot)


def _qk_prep_bwd(pa, dqh, dk2, dv2, rdq, rdk, rdv, drg, gq, gk, seg, ca, sa, cr, sr):
    seq = pa.shape[0]
    tm = _tile(seq, 256)

    def body(pa_ref, dqh_ref, dk2_ref, dv2_ref, rdqf_ref, rdqb_ref, rdkf_ref, rdkb_ref, rdvf_ref, rdvb_ref,
             drg_ref, gq_ref, gk_ref, seg_ref, ca_ref, sa_ref, cr_ref, sr_ref,
             dpa_ref, dpr_ref, dgq_ref, dgk_ref):
        @pl.when(pl.program_id(0) == 0)
        def _():
            dgq_ref[...] = jnp.zeros_like(dgq_ref)
            dgk_ref[...] = jnp.zeros_like(dgk_ref)

        ca_, sa_ = ca_ref[...], sa_ref[...]

        def norm_bwd(raw, gain, dy, segm, dg_ref):
            msq = jnp.dot(raw * raw, segm, precision=HIGHEST, preferred_element_type=F32)
            r = lax.rsqrt(msq + EPS)
            n = raw * r
            dg_ref[...] += jnp.sum(dy * n, axis=0, keepdims=True)
            dn = dy * gain
            return r * (dn - n * jnp.dot(dn * n, segm, precision=HIGHEST, preferred_element_type=F32))

        dqn = _rope(dqh_ref[...] * (A_HD ** -0.5), _cat(ca_, 4), -_cat(sa_, 4), A_HD // 2)
        dpa_ref[:, 0:AQ_W] = _bf(norm_bwd(pa_ref[:, 0:AQ_W], gq_ref[...], dqn, seg_ref[...], dgq_ref))
        dkn = _rope(dk2_ref[...], ca_, -sa_, A_HD // 2)
        dpa_ref[:, AQ_W:AQ_W + AKV_W] = _bf(norm_bwd(pa_ref[:, AQ_W:AQ_W + AKV_W], gk_ref[...], dkn,
                                                     seg_ref[0:AKV_W, 0:AKV_W], dgk_ref))
        dpa_ref[:, AQ_W + AKV_W:PA_W] = _bf(dv2_ref[...])
        cr_, sr_ = _cat(cr_ref[...], 4), -_cat(sr_ref[...], 4)
        dpr_ref[:, 0:R_W] = _bf(_rope((rdqf_ref[0] + rdqb_ref[0]) * (R_HD ** -0.5), cr_, sr_, R_HD // 2))
        dpr_ref[:, R_W:2 * R_W] = _bf(_rope(rdkf_ref[0] + rdkb_ref[0], cr_, sr_, R_HD // 2))
        dpr_ref[:, 2 * R_W:3 * R_W] = _bf(rdvf_ref[0] + rdvb_ref[0])
        dpr_ref[:, 3 * R_W:4 * R_W] = drg_ref[...]

    row = lambda w_: pl.BlockSpec((tm, w_), lambda i: (i, 0))
    ydir = lambda d: pl.BlockSpec((1, tm, R_W), lambda i: (d, i, 0))
    return pl.pallas_call(
        body, name="qk_prep_bwd", grid=(seq // tm,),
        in_specs=[row(PA_W), row(AQ_W), row(AKV_W), row(AKV_W), ydir(0), ydir(1), ydir(0), ydir(1),
                  ydir(0), ydir(1), row(R_W), _full((1, AQ_W)), _full((1, AKV_W)), _full((AQ_W, AQ_W)),
                  row(LANES), row(LANES), row(LANES), row(LANES)],
        out_specs=(row(PA_W), row(PR_W), _full((1, AQ_W)), _full((1, AKV_W))),
        out_shape=(jax.ShapeDtypeStruct((seq, PA_W), _MXU), jax.ShapeDtypeStruct((seq, PR_W), _MXU),
                   jax.ShapeDtypeStruct((1, AQ_W), F32), jax.ShapeDtypeStruct((1, AKV_W), F32)),
    )(pa, dqh, dk2, dv2, rdq, rdq, rdk, rdk, rdv, rdv, drg, gq, gk, seg, ca, sa, cr, sr)


def _in_proj_bwd(x, dx1, gain, dpa, dpr, dpg, wint):
    seq = x.shape[0]
    tm = _tile(seq, 256)

    def body(x_ref, dx1_ref, g_ref, dpa_ref, dpr_ref, dpg_ref, wt_ref, dx_ref, dg_ref):
        @pl.when(pl.program_id(0) == 0)
        def _():
            dg_ref[...] = jnp.zeros_like(dg_ref)

        dh = (_mm(dpa_ref[...], wt_ref[0:PA_W, :]) + _mm(dpr_ref[...], wt_ref[PA_W:PA_W + PR_W, :])
              + _mm(dpg_ref[...], wt_ref[PA_W + PR_W:IN_W, :]))
        n, r = _rms(x_ref[...])
        dg_ref[...] += jnp.sum(dh * n, axis=0, keepdims=True)
        dx_ref[...] = dx1_ref[...] + _rms_bwd(n, r, g_ref[...], dh)

    row = lambda w_: pl.BlockSpec((tm, w_), lambda i: (i, 0))
    return pl.pallas_call(
        body, name="in_proj_bwd", grid=(seq // tm,),
        in_specs=[row(D), row(D), _full((1, D)), row(PA_W), row(PR_W), row(PG_W), _full((IN_W, D))],
        out_specs=(row(D), _full((1, D))),
        out_shape=(jax.ShapeDtypeStruct((seq, D), F32), jax.ShapeDtypeStruct((1, D), F32)),
    )(x, dx1, gain, dpa, dpr, dpg, wint)


def _wgrad(a, b, name):
    seq, m = a.shape
    n = b.shape[1]
    tm, tn, ts = _tile(m, 1024), _tile(n, 1024), _tile(seq, 2048)
    ns = seq // ts

    def body(a_ref, b_ref, o_ref):
        @pl.when(pl.program_id(2) == 0)
        def _():
            o_ref[...] = jnp.zeros_like(o_ref)

        o_ref[...] += _mm_tn(a_ref[...], b_ref[...])

    return pl.pallas_call(
        body, name=name, grid=(m // tm, n // tn, ns),
        in_specs=[pl.BlockSpec((ts, tm), lambda i, j, s: (s, i)), pl.BlockSpec((ts, tn), lambda i, j, s: (s, j))],
        out_specs=pl.BlockSpec((tm, tn), lambda i, j, s: (i, j)),
        out_shape=jax.ShapeDtypeStruct((m, n), F32),
    )(a, b)


def _adamw_math(w, g, m, v):
    m = B1 * m + (1.0 - B1) * g
    v = B2 * v + (1.0 - B2) * (g * g)
    m_hat = m / (1.0 - B1 ** STEP)
    v_hat = v / (1.0 - B2 ** STEP)
    delta = -LR * (m_hat / (jnp.sqrt(v_hat) + ADAM_EPS) + WD * w)
    return delta, m, v


def _adamw_big(land, own, w, m, v, name):
    rws, cols = w.shape
    tr = next(t for t in range(min(rws, 288), 0, -8) if rws % t == 0)

    def body(l_ref, o_ref, w_ref, m_ref, v_ref, g_ref, d_ref, nm_ref, nv_ref):
        x, y, c = _mesh_pos()
        me = 4 * x + 2 * y + c
        g = o_ref[...]
        for j in range(N_DEV):
            g = g + jnp.where(me == j, 0.0, l_ref[j].astype(F32))
        g_ref[...] = g
        d_ref[...], nm_ref[...], nv_ref[...] = _adamw_math(w_ref[...], g, m_ref[...], v_ref[...])

    row = pl.BlockSpec((tr, cols), lambda i: (i, 0))
    shp = jax.ShapeDtypeStruct((rws, cols), F32)
    return pl.pallas_call(
        body, name=name, grid=(rws // tr,),
        in_specs=[pl.BlockSpec((N_DEV, tr, cols), lambda i: (0, i, 0)), row, row, row, row],
        out_specs=(row, row, row, row), out_shape=(shp, shp, shp, shp),
    )(land, own, w, m, v)


def _adamw_small(sland, w, m, v):
    def body(l_ref, w_ref, m_ref, v_ref, g_ref, d_ref, nm_ref, nv_ref, loss_ref):
        s = l_ref[0]
        for j in range(1, N_DEV):
            s = s + l_ref[j]
        w = w_ref[...]
        gq = s[8:9]
        for h in range(1, A_H):
            gq = gq + s[8 + h:9 + h]
        gk = s[16:17] + s[17:18]
        gdec = s[5:6] * _sigmoid(-w[5:6])
        g = jnp.concatenate([s[0:5], gdec, gq, gk], axis=0)
        g_ref[...] = g
        d_ref[...], nm_ref[...], nv_ref[...] = _adamw_math(w, g, m_ref[...], v_ref[...])
        loss_ref[...] = s[6:7, 0:LANES]

    shp = jax.ShapeDtypeStruct((8, PACK_COLS), F32)
    return pl.pallas_call(
        body, name="adamw_small",
        out_shape=(shp, shp, shp, shp, jax.ShapeDtypeStruct((1, LANES), F32)),
    )(sland, w, m, v)


_BIG = (("w_attn_o", AQ_W, D, 1), ("w_ret_o", R_W, D, 1), ("w_out", D, D, 0),
        ("w_up", D, FF, 1), ("w_down", FF, D, 0), ("w_ple_gate", D, D, 0), ("w_ple", PLE, D, 1))
IN_SHARD = IN_W // N_DEV
_SMALL = ("mix_norm", "mlp_norm", "ple_norm", "final_norm", "ret_norm_gain", "ret_decay_logit",
          "attn_q_norm", "attn_k_norm")


def _shard_shape(rows, cols, axis):
    return (rows // N_DEV, cols) if axis == 0 else (rows, cols // N_DEV)


def _pack_shards(shards):
    flat = jnp.concatenate([s.reshape(-1) for s in shards])
    return flat.reshape(-1, PACK_COLS)


def _unpack_gathered(gathered):
    flat = gathered.reshape(N_DEV, -1)
    out, off = {}, 0
    for name, rows, cols, axis in _BIG:
        sr, sc = _shard_shape(rows, cols, axis)
        blk = flat[:, off:off + sr * sc].reshape(N_DEV, sr, sc)
        off += sr * sc
        out[name] = blk.reshape(rows, cols) if axis == 0 else blk.transpose(1, 0, 2).reshape(rows, cols)
    return out


def _pack_full_grads(grads):
    parts = []
    for name, rows, cols, axis in _BIG:
        sr, sc = _shard_shape(rows, cols, axis)
        g = grads[name]
        blk = g.reshape(N_DEV, sr, sc) if axis == 0 else g.reshape(rows, N_DEV, sc).transpose(1, 0, 2)
        parts.append(blk.reshape(N_DEV, -1))
    flat = jnp.concatenate(parts, axis=1)
    return flat.reshape(N_DEV, -1, PACK_COLS)


def _unpack_shard(packed):
    flat = packed.reshape(-1)
    out, off = {}, 0
    for name, rows, cols, axis in _BIG:
        sr, sc = _shard_shape(rows, cols, axis)
        out[name] = flat[off:off + sr * sc].reshape(1, sr, sc)
        off += sr * sc
    return out


def _pack_small(vals):
    rows = [jnp.pad(vals[n].reshape(-1), (0, PACK_COLS - vals[n].size)) for n in _SMALL]
    return jnp.stack(rows)


def _unpack_small(packed, like):
    return {n: packed[i, :like[n].size].reshape(like[n].shape) for i, n in enumerate(_SMALL)}


def _row(v):
    return jnp.pad(v.reshape(-1), (0, PACK_COLS - v.size))


def kernel(x, p, mix_norm, w_in, attn_q_norm, attn_k_norm, ret_decay_logit, ret_norm_gain, w_attn_o, w_ret_o, w_out, mlp_norm, w_up, w_down, ple_norm, w_ple_gate, w_ple, final_norm, loss_target, m_mix_norm, m_w_in, m_attn_q_norm, m_attn_k_norm, m_ret_decay_logit, m_ret_norm_gain, m_w_attn_o, m_w_ret_o, m_w_out, m_mlp_norm, m_w_up, m_w_down, m_ple_norm, m_w_ple_gate, m_w_ple, m_final_norm, v_mix_norm, v_w_in, v_attn_q_norm, v_attn_k_norm, v_ret_decay_logit, v_ret_norm_gain, v_w_attn_o, v_w_ret_o, v_w_out, v_mlp_norm, v_w_up, v_w_down, v_ple_norm, v_w_ple_gate, v_w_ple, v_final_norm):
    args = dict(locals())
    seq = x.shape[1]
    xs = x[0]
    ps = p[0, 0]
    tgt = loss_target[0]

    big_names = [b[0] for b in _BIG]
    wshard = _pack_shards([args[n] for n in big_names])
    win_g, rest_g = _all_gather([w_in[0].astype(_MXU), wshard.astype(_MXU)])
    win = win_g.transpose(1, 0, 2).reshape(D, IN_W)
    wfull = _unpack_gathered(rest_g)
    wao, wro, wout = wfull["w_attn_o"], wfull["w_ret_o"], wfull["w_out"]
    wup, wdown, wpg, wple = wfull["w_up"], wfull["w_down"], wfull["w_ple_gate"], wfull["w_ple"]

    g_mix, g_mlp, g_ple = mix_norm, mlp_norm, ple_norm
    g_fin = final_norm.reshape(1, D)
    gq = jnp.tile(attn_q_norm, (1, A_H))
    gk = jnp.tile(attn_k_norm, (1, A_KV))
    seg = _seg_mean_matrix()
    ca, sa, cr, sr = _rope_tables(seq)

    pa, pr, pg, h = _in_proj(xs, g_mix, win)
    qh, kh, vh, rqh, rkh = _qk_prep(pa, pr, gq, gk, seg, ca, sa, cr, sr)

    tq = _tile(seq, 256)
    tk = _tile(seq // 4, 1024)
    qt8 = qh.reshape(seq, A_H, A_HD).transpose(1, 2, 0)
    vta = jnp.stack([jnp.concatenate([_chunk_t(vh[:, g * A_HD:(g + 1) * A_HD], tk),
                                      jnp.ones((seq // tk, 16, tk), _MXU)], axis=1) for g in range(A_KV)])
    ot, lse = _attn_fwd(qt8, kh, vta, _tile(seq, 256), tk)
    o = _heads_to_rows(ot)

    zb = jnp.broadcast_to(ret_decay_logit.reshape(2 * R_H, 1, 1), (2 * R_H, 1, LANES))
    tm_, tmw, tqd, tqdw, tkd, tkdw, tg, tgw = _ret_tables(zb)
    cb = _tile(seq // CHUNK, 8)
    y2, pst = _ret_fwd(rqh, rkh, pr, tm_, tqd, tkd, tg, cb)

    x1, merged, ret_in = _merge_fwd(xs, o, y2, pr, pg, ret_norm_gain, wao, wro, wout)
    x2 = _mlp_fwd(x1, g_mlp, wup, wdown)

    dx2, de, dz, hp, loss_p, dg_fin, dg_ple = _ple_loss(x2, ps, tgt, g_ple, g_fin, wpg, wpg.T, wple)
    dx1, act, du, hm, dg_mlp = _mlp_bwd(x1, dx2, g_mlp, wup, wdown.T, wup.T)
    dpg, dao, dro, do, dry, drg, dg_gn = _merge_bwd(dx1, o, y2, pr, pg, ret_norm_gain, wao, wro,
                                                    wout.T, wao.T, wro.T)
    rdq, rdk, rdv, dlam = _ret_bwd(rqh, rkh, pr, dry, pst, tm_, tmw, tqd, tqdw, tkd, tkdw, tg, tgw, cb)

    ksplit = 2
    tkb = _tile(seq // 4, 512)
    dot_ = do.reshape(seq, A_H, A_HD).transpose(1, 2, 0)
    dkt, dvt, dqt = _attn_bwd(qt8, ot, dot_, lse, kh, vh, _chunk_t(kh, tkb), tq, tkb, ksplit)
    dqh = _heads_to_rows(jnp.sum(dqt, axis=0))
    dpa, dpr, dg_q, dg_k = _qk_prep_bwd(pa, dqh, _chunks_to_rows(dkt), _chunks_to_rows(dvt), rdq, rdk, rdv, drg, gq, gk, seg, ca, sa, cr, sr)
    grad_x, dg_mix = _in_proj_bwd(xs, dx1, g_mix, dpa, dpr, dpg, win.T)

    wg = {
        "w_in": jnp.concatenate([_wgrad(h, dpa, "wgrad_in_a"), _wgrad(h, dpr, "wgrad_in_r"),
                                 _wgrad(h, dpg, "wgrad_in_g")], axis=1),
        "w_attn_o": _wgrad(o, dao, "wgrad_attn_o"),
        "w_ret_o": _wgrad(ret_in, dro, "wgrad_ret_o"),
        "w_out": _wgrad(merged, dx1, "wgrad_out"),
        "w_up": _wgrad(hm, du, "wgrad_up"),
        "w_down": _wgrad(act, dx2, "wgrad_down"),
        "w_ple_gate": _wgrad(hp, dz, "wgrad_ple_gate"),
        "w_ple": _wgrad(ps, de, "wgrad_ple"),
    }
    gpack = _pack_full_grads(wg)
    gpack_in = wg["w_in"].reshape(D, N_DEV, IN_SHARD).transpose(1, 0, 2)
    small = jnp.stack(
        [_row(dg_mix), _row(dg_mlp), _row(dg_ple), _row(dg_fin), _row(dg_gn), _row(dlam[:, 0, 0]),
         _row(loss_p[0, 0:1]), jnp.zeros((PACK_COLS,), F32)]
        + [_row(dg_q[0, hh * A_HD:(hh + 1) * A_HD]) for hh in range(A_H)]
        + [_row(dg_k[0, hh * A_HD:(hh + 1) * A_HD]) for hh in range(A_KV)]
        + [jnp.zeros((PACK_COLS,), F32)] * (SMALL_ROWS - 18))

    me = 4 * lax.axis_index("x") + 2 * lax.axis_index("y") + lax.axis_index("c")
    own_in = lax.dynamic_index_in_dim(gpack_in, me, axis=0, keepdims=False)
    own = lax.dynamic_index_in_dim(gpack, me, axis=0, keepdims=False)
    land_in, land, sland = _exchange_grads([_bf(gpack_in), _bf(gpack), small[None]])
    in_sh = _adamw_big(land_in, own_in, w_in[0], m_w_in[0], v_w_in[0], "adamw_w_in")
    g_sh, d_sh, m_sh, v_sh = _adamw_big(land, own, wshard, _pack_shards([args["m_" + n] for n in big_names]),
                                        _pack_shards([args["v_" + n] for n in big_names]), "adamw_shard")
    g_sm, d_sm, m_sm, v_sm, loss_row = _adamw_small(
        sland, _pack_small({n: args[n] for n in _SMALL}), _pack_small({n: args["m_" + n] for n in _SMALL}),
        _pack_small({n: args["v_" + n] for n in _SMALL}))

    names = ["mix_norm", "w_in", "attn_q_norm", "attn_k_norm", "ret_decay_logit", "ret_norm_gain", "w_attn_o",
             "w_ret_o", "w_out", "mlp_norm", "w_up", "w_down", "ple_norm", "w_ple_gate", "w_ple", "final_norm"]
    like = {n: args[n] for n in _SMALL}
    outs = [loss_row[0, 0], grad_x[None]]
    for big, sm, w_in_part in ((g_sh, g_sm, in_sh[0]), (d_sh, d_sm, in_sh[1]), (m_sh, m_sm, in_sh[2]),
                               (v_sh, v_sm, in_sh[3])):
        table = {**_unpack_shard(big), **_unpack_small(sm, like), "w_in": w_in_part[None]}
        outs += [table[n] for n in names]
    return tuple(outs)
```

```python
import functools

import jax
import jax.numpy as jnp
from jax import lax
from jax.experimental import pallas as pl
from jax.experimental.pallas import tpu as pltpu

F32 = jnp.float32
_MXU = jnp.bfloat16

D = 1024
PLE = 256
GRID_W = 64
A_HD = 64
A_H = 8
A_KV = 2
A_G = A_H // A_KV
AQ_W = A_H * A_HD
AKV_W = A_KV * A_HD
R_HD = 128
R_H = 4
R_W = R_H * R_HD
IN_W = AQ_W + 2 * AKV_W + 4 * R_W + 2 * D
PA_W = AQ_W + 2 * AKV_W
PR_W = 4 * R_W
PG_W = 2 * D
FF = 4 * D
CHUNK = 128
ROPE_THETA = 10000.0
EPS = 1e-6
GN_EPS = 1e-5
N_DEV = 8

LR, B1, B2, ADAM_EPS, WD, STEP = 0.001, 0.9, 0.999, 1e-08, 0.01, 10

LANES = 128
PACK_COLS = 1024
SMALL_ROWS = 24
HIGHEST = lax.Precision.HIGHEST


def _tile(n, pref):
    t = min(n, pref)
    assert n % t == 0, (n, t)
    return t


def _bf(a):
    return a.astype(_MXU)


def _mm(a, b):
    return jnp.dot(_bf(a), _bf(b), preferred_element_type=F32)


def _mm_nt(a, b):
    return lax.dot_general(_bf(a), _bf(b), (((1,), (1,)), ((), ())), preferred_element_type=F32)


def _mm_tn(a, b):
    return lax.dot_general(_bf(a), _bf(b), (((0,), (0,)), ((), ())), preferred_element_type=F32)


def _sigmoid(z):
    return 1.0 / (1.0 + jnp.exp(-z))


def _rms(x):
    r = lax.rsqrt(jnp.mean(x * x, axis=-1, keepdims=True) + EPS)
    return x * r, r


def _rms_bwd(n, r, gain, dy):
    dn = dy * gain
    return r * (dn - n * jnp.mean(dn * n, axis=-1, keepdims=True))


def _swap_halves(x, half):
    n = x.shape[-1]
    lane = lax.broadcasted_iota(jnp.int32, x.shape, x.ndim - 1)
    first = (lane % (2 * half)) < half
    return jnp.where(first, pltpu.roll(x, n - half, axis=1), pltpu.roll(x, half, axis=1))


def _rope(x, cos, sin, half):
    return x * cos + _swap_halves(x, half) * sin


def _cat(t, reps):
    return jnp.concatenate([t] * reps, axis=1)


def _full(shape):
    nd = len(shape)
    return pl.BlockSpec(shape, lambda *_: (0,) * nd)


def _rope_tables(seq):
    def tab(head_dim):
        n_axis = head_dim // 4
        freqs = ROPE_THETA ** (-jnp.arange(n_axis, dtype=F32) / n_axis)
        rows = seq // GRID_W
        row = jnp.repeat(jnp.arange(rows, dtype=F32), GRID_W)
        col = jnp.tile(jnp.arange(GRID_W, dtype=F32), rows)
        ang = jnp.concatenate([row[:, None] * freqs, col[:, None] * freqs], axis=-1)
        c, s = jnp.cos(ang), jnp.sin(ang)
        return jnp.concatenate([c, c], axis=-1), jnp.concatenate([-s, s], axis=-1)
    ca, sa = tab(A_HD)
    cr, sr = tab(R_HD)
    return jnp.tile(ca, (1, 2)), jnp.tile(sa, (1, 2)), cr, sr


def _seg_mean_matrix():
    i = jnp.arange(AQ_W) // A_HD
    return (i[:, None] == i[None, :]).astype(F32) / A_HD


def _mesh_pos():
    return lax.axis_index("x"), lax.axis_index("y"), lax.axis_index("c")


def _all_gather(shards):
    n = len(shards)

    def body(*refs):
        x_refs, out_refs = refs[:n], refs[n:2 * n]
        send_sems, recv_sems, local_sems = refs[2 * n:]
        x, y, c = _mesh_pos()
        me, sibling = (x, y, c), (x, y, 1 - c)
        chips = [(1 - x, y), (x, 1 - y), (1 - x, 1 - y)]

        def slot(t, px, py, pc):
            return out_refs[t].at[4 * px + 2 * py + pc]

        def copy(t, k, block, to, src=None):
            return pltpu.make_async_remote_copy(
                src_ref=slot(t, *block) if src is None else src, dst_ref=slot(t, *block),
                send_sem=send_sems.at[7 * t + k], recv_sem=recv_sems.at[7 * t + k],
                device_id=to, device_id_type=pl.DeviceIdType.MESH)

        mine = [pltpu.make_async_copy(x_refs[t], slot(t, *me), local_sems.at[t]) for t in range(n)]
        for cp in mine:
            cp.start()
        first = []
        for t in range(n):
            first.append(copy(t, 0, me, sibling, src=x_refs[t]))
            first += [copy(t, 1 + j, me, (*chip, c), src=x_refs[t]) for j, chip in enumerate(chips)]
        for cp in first:
            cp.start()
        passed = []
        for t in range(n):
            for j, chip in enumerate(chips):
                copy(t, 1 + j, (*chip, c), me).wait_recv()
                passed.append(copy(t, 4 + j, (*chip, c), sibling))
                passed[-1].start()
        for t in range(n):
            copy(t, 0, sibling, me).wait_recv()
            for j, chip in enumerate(chips):
                copy(t, 4 + j, (*chip, 1 - c), me).wait_recv()
        for cp in first + passed:
            cp.wait_send()
        for cp in mine:
            cp.wait()

    anyspec = pl.BlockSpec(memory_space=pl.ANY)
    return pl.pallas_call(
        body, name="all_gather_weights",
        out_shape=tuple(jax.ShapeDtypeStruct((N_DEV,) + s.shape, s.dtype) for s in shards),
        in_specs=[anyspec] * n, out_specs=(anyspec,) * n,
        scratch_shapes=[pltpu.SemaphoreType.DMA((7 * n,)), pltpu.SemaphoreType.DMA((7 * n,)),
                        pltpu.SemaphoreType.DMA((n,))],
    )(*shards)


def _exchange_grads(packs):
    n = len(packs)

    def body(*refs):
        g_refs, land_refs = refs[:n], refs[n:2 * n]
        send_sems, recv_sems, local_sems = refs[2 * n:]
        x, y, c = _mesh_pos()
        me = 4 * x + 2 * y + c

        def row(t, j):
            return g_refs[t].at[j if packs[t].shape[0] == N_DEV else 0]

        own = [pltpu.make_async_copy(row(t, me), land_refs[t].at[me], local_sems.at[t]) for t in range(n)]
        for cp in own:
            cp.start()
        sends = []
        for k in range(1, N_DEV):
            peer = (x ^ ((k >> 2) & 1), y ^ ((k >> 1) & 1), c ^ (k & 1))
            pidx = 4 * peer[0] + 2 * peer[1] + peer[2]
            for t in range(n):
                sends.append(pltpu.make_async_remote_copy(
                    src_ref=row(t, pidx), dst_ref=land_refs[t].at[me],
                    send_sem=send_sems.at[7 * t + k - 1], recv_sem=recv_sems.at[7 * t + k - 1],
                    device_id=peer, device_id_type=pl.DeviceIdType.MESH))
                sends[-1].start()
        for k in range(1, N_DEV):
            peer = (x ^ ((k >> 2) & 1), y ^ ((k >> 1) & 1), c ^ (k & 1))
            pidx = 4 * peer[0] + 2 * peer[1] + peer[2]
            for t in range(n):
                pltpu.make_async_remote_copy(
                    src_ref=row(t, me), dst_ref=land_refs[t].at[pidx],
                    send_sem=send_sems.at[7 * t + k - 1], recv_sem=recv_sems.at[7 * t + k - 1],
                    device_id=peer, device_id_type=pl.DeviceIdType.MESH).wait_recv()
        for cp in sends:
            cp.wait_send()
        for cp in own:
            cp.wait()

    anyspec = pl.BlockSpec(memory_space=pl.ANY)
    return pl.pallas_call(
        body, name="exchange_grads",
        out_shape=tuple(jax.ShapeDtypeStruct((N_DEV,) + g.shape[1:], g.dtype) for g in packs),
        in_specs=[anyspec] * n, out_specs=(anyspec,) * n,
        scratch_shapes=[pltpu.SemaphoreType.DMA((7 * n,)), pltpu.SemaphoreType.DMA((7 * n,)),
                        pltpu.SemaphoreType.DMA((n,))],
    )(*packs)


def _in_proj(x, gain, w):
    seq = x.shape[0]
    tm = _tile(seq, 256)

    def body(x_ref, g_ref, w_ref, pa_ref, pr_ref, pg_ref, h_ref):
        n, _ = _rms(x_ref[...])
        h = _bf(n * g_ref[...])
        h_ref[...] = h
        pa_ref[...] = _mm(h, w_ref[:, 0:PA_W])
        pr_ref[...] = _mm(h, w_ref[:, PA_W:PA_W + PR_W])
        pg_ref[...] = _mm(h, w_ref[:, PA_W + PR_W:IN_W])

    row = lambda w_: pl.BlockSpec((tm, w_), lambda i: (i, 0))
    return pl.pallas_call(
        body, name="in_proj", grid=(seq // tm,),
        in_specs=[row(D), _full((1, D)), _full((D, IN_W))],
        out_specs=(row(PA_W), row(PR_W), row(PG_W), row(D)),
        out_shape=(jax.ShapeDtypeStruct((seq, PA_W), F32), jax.ShapeDtypeStruct((seq, PR_W), F32),
                   jax.ShapeDtypeStruct((seq, PG_W), F32), jax.ShapeDtypeStruct((seq, D), _MXU)),
    )(x, gain, w)


def _qk_prep(pa, pr, gq, gk, seg, ca, sa, cr, sr):
    seq = pa.shape[0]
    tm = _tile(seq, 256)

    def body(pa_ref, pr_ref, gq_ref, gk_ref, seg_ref, ca_ref, sa_ref, cr_ref, sr_ref,
             qh_ref, kh_ref, v_ref, rq_ref, rk_ref):
        q = pa_ref[:, 0:AQ_W]
        k = pa_ref[:, AQ_W:AQ_W + AKV_W]
        v_ref[...] = _bf(pa_ref[:, AQ_W + AKV_W:PA_W])
        ca_, sa_ = ca_ref[...], sa_ref[...]
        msq = jnp.dot(q * q, seg_ref[...], precision=HIGHEST, preferred_element_type=F32)
        qn = q * lax.rsqrt(msq + EPS) * gq_ref[...]
        qh_ref[...] = _bf(_rope(qn, _cat(ca_, 4), _cat(sa_, 4), A_HD // 2) * (A_HD ** -0.5))
        msk = jnp.dot(k * k, seg_ref[0:AKV_W, 0:AKV_W], precision=HIGHEST, preferred_element_type=F32)
        kn = k * lax.rsqrt(msk + EPS) * gk_ref[...]
        kh_ref[...] = _bf(_rope(kn, ca_, sa_, A_HD // 2))
        cr_, sr_ = _cat(cr_ref[...], 4), _cat(sr_ref[...], 4)
        rq_ref[...] = _rope(pr_ref[:, 0:R_W], cr_, sr_, R_HD // 2) * (R_HD ** -0.5)
        rk_ref[...] = _rope(pr_ref[:, R_W:2 * R_W], cr_, sr_, R_HD // 2)

    row = lambda w_: pl.BlockSpec((tm, w_), lambda i: (i, 0))
    return pl.pallas_call(
        body, name="qk_prep", grid=(seq // tm,),
        in_specs=[row(PA_W), row(2 * R_W), _full((1, AQ_W)), _full((1, AKV_W)), _full((AQ_W, AQ_W)),
                  row(LANES), row(LANES), row(LANES), row(LANES)],
        out_specs=(row(AQ_W), row(AKV_W), row(AKV_W), row(R_W), row(R_W)),
        out_shape=(jax.ShapeDtypeStruct((seq, AQ_W), _MXU), jax.ShapeDtypeStruct((seq, AKV_W), _MXU),
                   jax.ShapeDtypeStruct((seq, AKV_W), _MXU), jax.ShapeDtypeStruct((seq, R_W), F32),
                   jax.ShapeDtypeStruct((seq, R_W), F32)),
    )(pa, pr, gq, gk, seg, ca, sa, cr, sr)


def _chunk_t(a, tk):
    seq = a.shape[0]
    return a.reshape(seq // tk, tk, a.shape[1]).transpose(0, 2, 1)


def _heads_to_rows(t):
    return t.transpose(2, 0, 1).reshape(t.shape[2], AQ_W)


def _attn_fwd(qt8, k2, vta, tq, tk):
    seq = k2.shape[0]
    nck = seq // tk
    rows = A_G * tq
    vrows = vta.shape[2]
    rb = _tile(tk, 64)
    assert nck % 2 == 0, nck

    def body(qt_ref, k_ref, vt_ref, o_ref, lse_ref, m_sc, acc_sc, qtp_sc, s_a, s_b, p_a, p_b, al_a, al_b):
        g = pl.program_id(0)
        qtp_sc[...] = jnp.zeros_like(qtp_sc)
        qtp_sc[pl.ds(pl.multiple_of(g * A_HD, A_HD), A_HD), :] = jnp.concatenate(
            [qt_ref[a] for a in range(A_G)], axis=1)
        m_sc[...] = jnp.full((1, rows), -jnp.inf, F32)
        acc_sc[...] = jnp.zeros_like(acc_sc)

        def scores(c):
            kc = k_ref[pl.ds(pl.multiple_of(c * tk, tk), tk), :]
            return _mm(kc, qtp_sc[...])

        def stage(c, s_cur, s_nxt, p_cur, p_prv, al_cur, al_prv, first=False, last=False):
            if not last:
                s_nxt[...] = scores(c + 1)
            if not first:
                acc_sc[...] = al_prv[...] * acc_sc[...] + _mm(vt_ref[0, c - 1], p_prv[...])
            m_old = m_sc[...]
            m_new = m_old
            for r in range(0, tk, rb):
                m_new = jnp.maximum(m_new, jnp.max(s_cur[r:r + rb, :], axis=0, keepdims=True))
            for r in range(0, tk, rb):
                p_cur[r:r + rb, :] = _bf(jnp.exp(s_cur[r:r + rb, :] - m_new))
            al_cur[...] = jnp.exp(m_old - m_new)
            m_sc[...] = m_new

        s_a[...] = scores(0)
        stage(0, s_a, s_b, p_a, p_b, al_a, al_b, first=True)

        def pair(j, carry):
            stage(2 * j + 1, s_b, s_a, p_b, p_a, al_b, al_a)
            stage(2 * j + 2, s_a, s_b, p_a, p_b, al_a, al_b)
            return carry

        lax.fori_loop(0, nck // 2 - 1, pair, 0)
        stage(nck - 1, s_b, s_a, p_b, p_a, al_b, al_a, last=True)
        acc = al_b[...] * acc_sc[...] + _mm(vt_ref[0, nck - 1], p_b[...])
        l = acc[A_HD:A_HD + 1, :]
        lse = m_sc[...] + jnp.log(l)
        out = acc[0:A_HD, :] * (1.0 / l)
        for a in range(A_G):
            o_ref[a] = out[:, a * tq:(a + 1) * tq]
            lse_ref[a] = lse[:, a * tq:(a + 1) * tq]

    return pl.pallas_call(
        body, name="attn_fwd", grid=(A_KV, seq // tq),
        in_specs=[pl.BlockSpec((A_G, A_HD, tq), lambda g, i: (g, 0, i)),
                  _full((seq, LANES)), pl.BlockSpec((1, nck, vrows, tk), lambda g, i: (g, 0, 0, 0))],
        out_specs=(pl.BlockSpec((A_G, A_HD, tq), lambda g, i: (g, 0, i)),
                   pl.BlockSpec((A_G, 1, tq), lambda g, i: (g, 0, i))),
        out_shape=(jax.ShapeDtypeStruct((A_H, A_HD, seq), F32), jax.ShapeDtypeStruct((A_H, 1, seq), F32)),
        scratch_shapes=[pltpu.VMEM((1, rows), F32), pltpu.VMEM((vrows, rows), F32), pltpu.VMEM((LANES, rows), _MXU),
                        pltpu.VMEM((tk, rows), F32), pltpu.VMEM((tk, rows), F32),
                        pltpu.VMEM((tk, rows), _MXU), pltpu.VMEM((tk, rows), _MXU),
                        pltpu.VMEM((1, rows), F32), pltpu.VMEM((1, rows), F32)],
    )(qt8, k2, vta)


def _attn_bwd(qt8, ot, dot_, lse, k2, v2, k2t, tq, tk, ksplit):
    seq = k2.shape[0]
    sh = seq // ksplit
    nck = sh // tk
    rows = A_G * tq
    rb = _tile(tk, 64)
    assert nck % 2 == 0, nck

    def body(qt_ref, ot_ref, dot_ref, lse_ref, k_ref, v_ref, kt_ref,
             dk_ref, dv_ref, dq_ref, dq_sc, qtp_sc, dotp_sc, pt_sc, dst_sc,
             s_a, s_b, dp_a, dp_b, p_a, p_b, ds_a, ds_b):
        g = pl.program_id(1)
        hrows = pl.ds(pl.multiple_of(g * A_HD, A_HD), A_HD)

        @pl.when(pl.program_id(2) == 0)
        def _():
            dk_ref[...] = jnp.zeros_like(dk_ref)
            dv_ref[...] = jnp.zeros_like(dv_ref)

        lse_row = jnp.concatenate([lse_ref[a] for a in range(A_G)], axis=1)
        dd = jnp.concatenate([jnp.sum(ot_ref[a] * dot_ref[a], axis=0, keepdims=True)
                              for a in range(A_G)], axis=1)
        qtp_sc[...] = jnp.zeros_like(qtp_sc)
        dotp_sc[...] = jnp.zeros_like(dotp_sc)
        qtp_sc[hrows, :] = jnp.concatenate([qt_ref[a] for a in range(A_G)], axis=1)
        dotp_sc[hrows, :] = _bf(jnp.concatenate([dot_ref[a] for a in range(A_G)], axis=1))
        dq_sc[...] = jnp.zeros_like(dq_sc)

        def products(c, s_ref, dp_ref):
            sl = pl.ds(pl.multiple_of(c * tk, tk), tk)
            s_ref[...] = _mm(k_ref[sl, :], qtp_sc[...])
            dp_ref[...] = _mm(v_ref[sl, :], dotp_sc[...])

        def accumulate(c, p_ref, ds_ref):
            pt_sc[...] = p_ref[...].T
            dst_sc[...] = ds_ref[...].T
            dq_sc[...] += _mm(kt_ref[c, hrows, :], ds_ref[...])
            dv_ref[0, c] += _mm(dotp_sc[hrows, :], pt_sc[...])
            dk_ref[0, c] += _mm(qtp_sc[hrows, :], dst_sc[...])

        def stage(c, s_cur, dp_cur, s_nxt, dp_nxt, p_cur, ds_cur, p_prv, ds_prv, first=False, last=False):
            if not last:
                products(c + 1, s_nxt, dp_nxt)
            if not first:
                accumulate(c - 1, p_prv, ds_prv)
            for r in range(0, tk, rb):
                p = jnp.exp(s_cur[r:r + rb, :] - lse_row)
                p_cur[r:r + rb, :] = _bf(p)
                ds_cur[r:r + rb, :] = _bf(p * (dp_cur[r:r + rb, :] - dd))

        products(0, s_a, dp_a)
        stage(0, s_a, dp_a, s_b, dp_b, p_a, ds_a, p_b, ds_b, first=True)

        def pair(j, carry):
            stage(2 * j + 1, s_b, dp_b, s_a, dp_a, p_b, ds_b, p_a, ds_a)
            stage(2 * j + 2, s_a, dp_a, s_b, dp_b, p_a, ds_a, p_b, ds_b)
            return carry

        lax.fori_loop(0, nck // 2 - 1, pair, 0)
        stage(nck - 1, s_b, dp_b, s_a, dp_a, p_b, ds_b, p_a, ds_a, last=True)
        accumulate(nck - 1, p_b, ds_b)
        for a in range(A_G):
            dq_ref[0, a] = dq_sc[:, a * tq:(a + 1) * tq]

    tspec = pl.BlockSpec((A_G, A_HD, tq), lambda s, g, i: (g, 0, i))
    kspec = pl.BlockSpec((sh, LANES), lambda s, g, i: (s, 0))
    gspec = pl.BlockSpec((1, nck, A_HD, tk), lambda s, g, i: (g, s, 0, 0))
    gshape = jax.ShapeDtypeStruct((A_KV, seq // tk, A_HD, tk), F32)
    big = lambda dt: pltpu.VMEM((tk, rows), dt)
    bigt = pltpu.VMEM((rows, tk), _MXU)
    return pl.pallas_call(
        body, name="attn_bwd", grid=(ksplit, A_KV, seq // tq),
        in_specs=[tspec, tspec, tspec, pl.BlockSpec((A_G, 1, tq), lambda s, g, i: (g, 0, i)),
                  kspec, kspec, pl.BlockSpec((nck, LANES, tk), lambda s, g, i: (s, 0, 0))],
        out_specs=(gspec, gspec, pl.BlockSpec((1, A_G, A_HD, tq), lambda s, g, i: (s, g, 0, i))),
        out_shape=(gshape, gshape, jax.ShapeDtypeStruct((ksplit, A_H, A_HD, seq), F32)),
        scratch_shapes=[pltpu.VMEM((A_HD, rows), F32), pltpu.VMEM((LANES, rows), _MXU), pltpu.VMEM((LANES, rows), _MXU),
                        bigt, bigt,
                        big(F32), big(F32), big(F32), big(F32), big(_MXU), big(_MXU), big(_MXU), big(_MXU)],
    )(qt8, ot, dot_, lse, k2, v2, k2t)


def _chunks_to_rows(t):
    return t.transpose(1, 3, 0, 2).reshape(t.shape[1] * t.shape[3], AKV_W)


def _ret_tables(zb):
    c = CHUNK

    def body(z_ref, m_ref, mw_ref, qd_ref, qdw_ref, kd_ref, kdw_ref, g_ref, gw_ref):
        fwd = pl.program_id(0) < R_H
        z = z_ref[0]
        lam = jnp.minimum(z, 0.0) - jnp.log(1.0 + jnp.exp(-jnp.abs(z)))
        i = lax.broadcasted_iota(jnp.int32, (c, c), 0).astype(F32)
        j = lax.broadcasted_iota(jnp.int32, (c, c), 1).astype(F32)
        diff = jnp.where(fwd, i - j, j - i)
        keep = diff >= jnp.where(fwd, 0.0, 1.0)
        dist = jnp.maximum(diff, 0.0)
        m = jnp.where(keep, jnp.exp(lam * dist), 0.0)
        m_ref[0] = m
        mw_ref[0] = m * dist
        fq = jnp.where(fwd, i + 1.0, c - i)
        qd = jnp.exp(lam * fq)
        qd_ref[0] = qd
        qdw_ref[0] = qd * fq
        fk = jnp.where(fwd, c - 1.0 - i, i)
        kd = jnp.exp(lam * fk)
        kd_ref[0] = kd
        kdw_ref[0] = kd * fk
        gdec = jnp.exp(lam * c)
        g_ref[0] = gdec
        gw_ref[0] = gdec * c

    big = pl.BlockSpec((1, c, c), lambda t: (t, 0, 0))
    vec = pl.BlockSpec((1, 1, LANES), lambda t: (t, 0, 0))
    bshape = jax.ShapeDtypeStruct((2 * R_H, c, c), F32)
    vshape = jax.ShapeDtypeStruct((2 * R_H, 1, LANES), F32)
    return pl.pallas_call(
        body, name="ret_tables", grid=(2 * R_H,), in_specs=[vec],
        out_specs=(big, big, big, big, big, big, vec, vec),
        out_shape=(bshape,) * 6 + (vshape, vshape),
    )(zb)


def _ret_fwd(rq, rk, pr, m, qd, kd, gdec, cb):
    seq = rq.shape[0]
    c = CHUNK
    ns = seq // (cb * c)

    def body(q_ref, k_ref, v_ref, m_ref, qd_ref, kd_ref, g_ref, y_ref, pst_ref, p_sc):
        d = pl.program_id(0)

        @pl.when(pl.program_id(1) == 0)
        def _():
            p_sc[...] = jnp.zeros_like(p_sc)

        def chunk(j, carry):
            cc = jnp.where(d == 0, j, cb - 1 - j)
            sl = pl.ds(pl.multiple_of(cc * c, c), c)
            heads = [slice(h * R_HD, (h + 1) * R_HD) for h in range(R_H)]
            qk = [_mm_nt(q_ref[sl, hs], k_ref[sl, hs]) for hs in heads]
            qp = [_mm(q_ref[sl, hs] * qd_ref[h], p_sc[h]) for h, hs in enumerate(heads)]
            kv = [_mm_tn(k_ref[sl, hs] * kd_ref[h], v_ref[sl, hs]) for h, hs in enumerate(heads)]
            for h, hs in enumerate(heads):
                p = p_sc[h]
                pst_ref[h, cc] = p
                y_ref[0, sl, hs] = _mm(qk[h] * m_ref[h], v_ref[sl, hs]) + qp[h]
                p_sc[h] = p * g_ref[h] + kv[h]
            return carry

        lax.fori_loop(0, cb, chunk, 0)

    def step(d, n):
        return d * (ns - 1 - n) + (1 - d) * n

    blk = lambda off: pl.BlockSpec((cb * c, R_W), lambda d, n: (step(d, n), off))
    big = pl.BlockSpec((R_H, c, c), lambda d, n: (d, 0, 0))
    vec = pl.BlockSpec((R_H, 1, LANES), lambda d, n: (d, 0, 0))
    return pl.pallas_call(
        body, name="ret_fwd", grid=(2, ns),
        in_specs=[blk(0), blk(0), blk(2), big, big, big, vec],
        out_specs=(pl.BlockSpec((1, cb * c, R_W), lambda d, n: (d, step(d, n), 0)),
                   pl.BlockSpec((R_H, cb, R_HD, R_HD), lambda d, n: (d, step(d, n), 0, 0))),
        out_shape=(jax.ShapeDtypeStruct((2, seq, R_W), F32),
                   jax.ShapeDtypeStruct((2 * R_H, seq // c, R_HD, R_HD), F32)),
        scratch_shapes=[pltpu.VMEM((R_H, R_HD, R_HD), F32)],
    )(rq, rk, pr, m, qd, kd, gdec)


def _ret_bwd(rq, rk, pr, dry, pst, m, mw, qd, qdw, kd, kdw, gdec, gw, cb):
    seq = rq.shape[0]
    c = CHUNK
    ns = seq // (cb * c)

    def body(q_ref, k_ref, v_ref, dy_ref, pst_ref, m_ref, mw_ref, qd_ref, qdw_ref, kd_ref, kdw_ref,
             g_ref, gw_ref, dq_ref, dk_ref, dv_ref, dlam_ref, r_sc, acc_sc, e_sc, g_sc):
        d = pl.program_id(0)
        n = pl.program_id(1)

        @pl.when(n == 0)
        def _():
            r_sc[...] = jnp.zeros_like(r_sc)
            acc_sc[...] = jnp.zeros_like(acc_sc)
            e_sc[...] = jnp.zeros_like(e_sc)
            g_sc[...] = jnp.zeros_like(g_sc)

        def chunk(j, carry):
            cc = jnp.where(d == 0, cb - 1 - j, j)
            sl = pl.ds(pl.multiple_of(cc * c, c), c)
            heads = [slice(h * R_HD, (h + 1) * R_HD) for h in range(R_H)]
            first = []
            for h, hs in enumerate(heads):
                q, k, v, dy = q_ref[sl, hs], k_ref[sl, hs], v_ref[sl, hs], dy_ref[sl, hs]
                r = r_sc[h]
                first.append((_mm_nt(q, k), _mm_nt(dy, v), _mm_nt(dy, pst_ref[h, cc]), _mm_nt(v, r),
                              _mm(k * kd_ref[h], r), _mm_tn(q * qd_ref[h], dy)))
            for h, hs in enumerate(heads):
                qk, ds, dyp, vr, kr, qdy = first[h]
                q, k, dy = q_ref[sl, hs], k_ref[sl, hs], dy_ref[sl, hs]
                r = r_sc[h]
                da = ds * m_ref[h]
                dv_ref[0, sl, hs] = _mm_tn(qk * m_ref[h], dy) + kr
                dq_ref[0, sl, hs] = _mm(da, k) + dyp * qd_ref[h]
                dk_ref[0, sl, hs] = _mm_tn(da, q) + vr * kd_ref[h]
                acc_sc[h] += dyp * q * qdw_ref[h] + vr * k * kdw_ref[h]
                e_sc[h] += ds * qk * mw_ref[h]
                g_sc[h] += r * pst_ref[h, cc]
                r_sc[h] = r * g_ref[h] + qdy
            return carry

        lax.fori_loop(0, cb, chunk, 0)

        @pl.when(n == ns - 1)
        def _():
            for h in range(R_H):
                tot = jnp.sum(jnp.sum(acc_sc[h] + e_sc[h] + g_sc[h] * gw_ref[h], axis=0, keepdims=True),
                              axis=1, keepdims=True)
                dlam_ref[h] = jnp.broadcast_to(tot, (1, LANES))

    def step(d, n):
        return d * n + (1 - d) * (ns - 1 - n)

    blk = lambda off: pl.BlockSpec((cb * c, R_W), lambda d, n: (step(d, n), off))
    big = pl.BlockSpec((R_H, c, c), lambda d, n: (d, 0, 0))
    vec = pl.BlockSpec((R_H, 1, LANES), lambda d, n: (d, 0, 0))
    out = pl.BlockSpec((1, cb * c, R_W), lambda d, n: (d, step(d, n), 0))
    oshape = jax.ShapeDtypeStruct((2, seq, R_W), F32)
    sq = pltpu.VMEM((R_H, R_HD, R_HD), F32)
    return pl.pallas_call(
        body, name="ret_bwd", grid=(2, ns),
        in_specs=[blk(0), blk(0), blk(2), blk(0),
                  pl.BlockSpec((R_H, cb, R_HD, R_HD), lambda d, n: (d, step(d, n), 0, 0)),
                  big, big, big, big, big, big, vec, vec],
        out_specs=(out, out, out, vec),
        out_shape=(oshape, oshape, oshape, jax.ShapeDtypeStruct((2 * R_H, 1, LANES), F32)),
        scratch_shapes=[sq, sq, sq, sq],
    )(rq, rk, pr, dry, pst, m, mw, qd, qdw, kd, kdw, gdec, gw)


def _group_norm(ry):
    yn, rs = [], []
    for h in range(R_H):
        s = ry[:, h * R_HD:(h + 1) * R_HD]
        mu = jnp.mean(s, axis=-1, keepdims=True)
        cen = s - mu
        r = lax.rsqrt(jnp.mean(cen * cen, axis=-1, keepdims=True) + GN_EPS)
        yn.append(cen * r)
        rs.append(r)
    return yn, rs


def _merge_fwd(x, o, y2, pr, pg, gain_r, wao, wro, wout):
    seq = x.shape[0]
    tm = _tile(seq, 256)

    def body(x_ref, o_ref, yf_ref, yb_ref, rg_ref, ga_ref, gr_ref, gn_ref, wao_ref, wro_ref, wout_ref,
             x1_ref, mg_ref, ri_ref):
        yn, _ = _group_norm(yf_ref[0] + yb_ref[0])
        rg = rg_ref[...]
        ret_in = jnp.concatenate(yn, axis=1) * gn_ref[...] * (rg * _sigmoid(rg))
        ri_ref[...] = _bf(ret_in)
        attn_out = _mm(o_ref[...], wao_ref[...])
        ret_out = _mm(ret_in, wro_ref[...])
        merged = _sigmoid(ga_ref[...]) * attn_out + _sigmoid(gr_ref[...]) * ret_out
        mg_ref[...] = _bf(merged)
        x1_ref[...] = x_ref[...] + _mm(merged, wout_ref[...])

    row = lambda w_, j=0: pl.BlockSpec((tm, w_), lambda i: (i, j))
    ydir = lambda d: pl.BlockSpec((1, tm, R_W), lambda i: (d, i, 0))
    return pl.pallas_call(
        body, name="merge_fwd", grid=(seq // tm,),
        in_specs=[row(D), row(AQ_W), ydir(0), ydir(1), row(R_W, 3), row(D, 0), row(D, 1),
                  _full((1, R_W)), _full((AQ_W, D)), _full((R_W, D)), _full((D, D))],
        out_specs=(row(D), row(D), row(R_W)),
        out_shape=(jax.ShapeDtypeStruct((seq, D), F32), jax.ShapeDtypeStruct((seq, D), _MXU),
                   jax.ShapeDtypeStruct((seq, R_W), _MXU)),
    )(x, o, y2, y2, pr, pg, pg, gain_r, wao, wro, wout)


def _mlp_fwd(x1, gain, wup, wdown):
    seq = x1.shape[0]
    tm = _tile(seq, 512)
    fc = 512
    nfc = FF // fc

    def body(x_ref, g_ref, wu_ref, wd_ref, x2_ref, hm_sc, acc_sc):
        c = pl.program_id(1)

        @pl.when(c == 0)
        def _():
            n, _ = _rms(x_ref[...])
            hm_sc[...] = _bf(n * g_ref[...])
            acc_sc[...] = jnp.zeros_like(acc_sc)

        halves = (slice(0, fc // 2), slice(fc // 2, fc))
        ups = [jnp.maximum(_mm(hm_sc[...], wu_ref[:, hs]), 0.0) for hs in halves]
        acc_sc[...] += _mm(ups[0] * ups[0], wd_ref[halves[0], :]) + _mm(ups[1] * ups[1], wd_ref[halves[1], :])

        @pl.when(c == nfc - 1)
        def _():
            x2_ref[...] = x_ref[...] + acc_sc[...]

    return pl.pallas_call(
        body, name="mlp_fwd", grid=(seq // tm, nfc),
        in_specs=[pl.BlockSpec((tm, D), lambda i, c: (i, 0)), pl.BlockSpec((1, D), lambda i, c: (0, 0)),
                  pl.BlockSpec((D, fc), lambda i, c: (0, c)), pl.BlockSpec((fc, D), lambda i, c: (c, 0))],
        out_specs=pl.BlockSpec((tm, D), lambda i, c: (i, 0)),
        out_shape=jax.ShapeDtypeStruct((seq, D), F32),
        scratch_shapes=[pltpu.VMEM((tm, D), _MXU), pltpu.VMEM((tm, D), F32)],
    )(x1, gain, wup, wdown)


def _ple_loss(x2, p, tgt, g_ple, g_fin, wpg, wpgt, wple):
    seq = x2.shape[0]
    tm = _tile(seq, 256)

    def body(x2_ref, p_ref, t_ref, gp_ref, gf_ref, wpg_ref, wpgt_ref, wple_ref,
             dx2_ref, de_ref, dz_ref, hp_ref, loss_ref, dgf_ref, dgp_ref):
        @pl.when(pl.program_id(0) == 0)
        def _():
            loss_ref[...] = jnp.zeros_like(loss_ref)
            dgf_ref[...] = jnp.zeros_like(dgf_ref)
            dgp_ref[...] = jnp.zeros_like(dgp_ref)

        x2 = x2_ref[...]
        gp, gf = gp_ref[...], gf_ref[...]
        n2, r2 = _rms(x2)
        hp = _bf(n2 * gp)
        hp_ref[...] = hp
        gate = _sigmoid(_mm(hp, wpg_ref[...]))
        e = _mm(p_ref[...], wple_ref[...])
        x3 = x2 + gate * e
        n3, r3 = _rms(x3)
        diff = n3 * gf - t_ref[...]
        row_loss = jnp.mean(diff * diff, axis=-1, keepdims=True)
        loss_ref[...] += 0.5 * jnp.sum(row_loss, axis=0, keepdims=True)
        dy = diff * (1.0 / D)
        dgf_ref[...] += jnp.sum(dy * n3, axis=0, keepdims=True)
        dx3 = _rms_bwd(n3, r3, gf, dy)
        de_ref[...] = _bf(dx3 * gate)
        dz = dx3 * e * gate * (1.0 - gate)
        dz_ref[...] = _bf(dz)
        dhp = _mm(dz, wpgt_ref[...])
        dgp_ref[...] += jnp.sum(dhp * n2, axis=0, keepdims=True)
        dx2_ref[...] = dx3 + _rms_bwd(n2, r2, gp, dhp)

    row = lambda w_: pl.BlockSpec((tm, w_), lambda i: (i, 0))
    act = lambda dt: jax.ShapeDtypeStruct((seq, D), dt)
    return pl.pallas_call(
        body, name="ple_loss", grid=(seq // tm,),
        in_specs=[row(D), row(PLE), row(D), _full((1, D)), _full((1, D)),
                  _full((D, D)), _full((D, D)), _full((PLE, D))],
        out_specs=(row(D), row(D), row(D), row(D), _full((1, LANES)), _full((1, D)), _full((1, D))),
        out_shape=(act(F32), act(_MXU), act(_MXU), act(_MXU), jax.ShapeDtypeStruct((1, LANES), F32),
                   jax.ShapeDtypeStruct((1, D), F32), jax.ShapeDtypeStruct((1, D), F32)),
    )(x2, p, tgt, g_ple, g_fin, wpg, wpgt, wple)


def _mlp_bwd(x1, dx2, gain, wup, wdownt, wupt):
    seq = x1.shape[0]
    tm = _tile(seq, 512)
    fc = 512
    nfc = FF // fc

    def body(x_ref, dx2_ref, g_ref, wu_ref, wdt_ref, wut_ref,
             dx1_ref, a_ref, du_ref, hm_ref, dg_ref, dhm_sc):
        i = pl.program_id(0)
        c = pl.program_id(1)

        @pl.when((i == 0) & (c == 0))
        def _():
            dg_ref[...] = jnp.zeros_like(dg_ref)

        @pl.when(c == 0)
        def _():
            n, _ = _rms(x_ref[...])
            hm_ref[...] = _bf(n * g_ref[...])
            dhm_sc[...] = jnp.zeros_like(dhm_sc)

        halves = (slice(0, fc // 2), slice(fc // 2, fc))
        ups = [jnp.maximum(_mm(hm_ref[...], wu_ref[:, hs]), 0.0) for hs in halves]
        das = [_mm(dx2_ref[...], wdt_ref[:, hs]) for hs in halves]
        part = None
        for u, da, hs in zip(ups, das, halves):
            a_ref[:, hs] = _bf(u * u)
            du = _bf(da * (2.0 * u))
            du_ref[:, hs] = du
            t = _mm(du, wut_ref[hs, :])
            part = t if part is None else part + t
        dhm_sc[...] += part

        @pl.when(c == nfc - 1)
        def _():
            n, r = _rms(x_ref[...])
            dhm = dhm_sc[...]
            dg_ref[...] += jnp.sum(dhm * n, axis=0, keepdims=True)
            dx1_ref[...] = dx2_ref[...] + _rms_bwd(n, r, g_ref[...], dhm)

    rowd = pl.BlockSpec((tm, D), lambda i, c: (i, 0))
    rowf = pl.BlockSpec((tm, fc), lambda i, c: (i, c))
    return pl.pallas_call(
        body, name="mlp_bwd", grid=(seq // tm, nfc),
        in_specs=[rowd, rowd, pl.BlockSpec((1, D), lambda i, c: (0, 0)),
                  pl.BlockSpec((D, fc), lambda i, c: (0, c)), pl.BlockSpec((D, fc), lambda i, c: (0, c)),
                  pl.BlockSpec((fc, D), lambda i, c: (c, 0))],
        out_specs=(rowd, rowf, rowf, rowd, pl.BlockSpec((1, D), lambda i, c: (0, 0))),
        out_shape=(jax.ShapeDtypeStruct((seq, D), F32), jax.ShapeDtypeStruct((seq, FF), _MXU),
                   jax.ShapeDtypeStruct((seq, FF), _MXU), jax.ShapeDtypeStruct((seq, D), _MXU),
                   jax.ShapeDtypeStruct((1, D), F32)),
        scratch_shapes=[pltpu.VMEM((tm, D), F32)],
    )(x1, dx2, gain, wup, wdownt, wupt)


def _merge_bwd(dx1, o, y2, pr, pg, gain_r, wao, wro, woutt, waot, wrot):
    seq = dx1.shape[0]
    tm = _tile(seq, 256)

    def body(dx1_ref, o_ref, yf_ref, yb_ref, rg_ref, ga_ref, gr_ref, gn_ref, wao_ref, wro_ref,
             woutt_ref, waot_ref, wrot_ref,
             dpg_ref, dao_ref, dro_ref, do_ref, dry_ref, drg_ref, dgn_ref):
        @pl.when(pl.program_id(0) == 0)
        def _():
            dgn_ref[...] = jnp.zeros_like(dgn_ref)

        yn_l, rs_l = _group_norm(yf_ref[0] + yb_ref[0])
        yn = jnp.concatenate(yn_l, axis=1)
        rg = rg_ref[...]
        gn = gn_ref[...]
        sg = _sigmoid(rg)
        sil = rg * sg
        ret_in = yn * gn * sil
        attn_out = _mm(o_ref[...], wao_ref[...])
        ret_out = _mm(ret_in, wro_ref[...])
        sa = _sigmoid(ga_ref[...])
        sr = _sigmoid(gr_ref[...])
        dm = _mm(dx1_ref[...], woutt_ref[...])
        dpg_ref[:, 0:D] = _bf(dm * attn_out * sa * (1.0 - sa))
        dpg_ref[:, D:2 * D] = _bf(dm * ret_out * sr * (1.0 - sr))
        dao = _bf(dm * sa)
        dro = _bf(dm * sr)
        dao_ref[...] = dao
        dro_ref[...] = dro
        do_ref[...] = _mm(dao, waot_ref[...])
        dri = _mm(dro, wrot_ref[...])
        dgn_ref[...] += jnp.sum(dri * yn * sil, axis=0, keepdims=True)
        drg_ref[...] = _bf(dri * yn * gn * (sg * (1.0 + rg * (1.0 - sg))))
        dyn = dri * gn * sil
        dry = []
        for h in range(R_H):
            dh = dyn[:, h * R_HD:(h + 1) * R_HD]
            dry.append(rs_l[h] * (dh - jnp.mean(dh, axis=-1, keepdims=True)
                                  - yn_l[h] * jnp.mean(dh * yn_l[h], axis=-1, keepdims=True)))
        dry_ref[...] = jnp.concatenate(dry, axis=1)

    row = lambda w_, j=0: pl.BlockSpec((tm, w_), lambda i: (i, j))
    ydir = lambda d: pl.BlockSpec((1, tm, R_W), lambda i: (d, i, 0))
    return pl.pallas_call(
        body, name="merge_bwd", grid=(seq // tm,),
        in_specs=[row(D), row(AQ_W), ydir(0), ydir(1), row(R_W, 3), row(D, 0), row(D, 1),
                  _full((1, R_W)), _full((AQ_W, D)), _full((R_W, D)), _full((D, D)),
                  _full((D, AQ_W)), _full((D, R_W))],
        out_specs=(row(PG_W), row(D), row(D), row(AQ_W), row(R_W), row(R_W), _full((1, R_W))),
        out_shape=(jax.ShapeDtypeStruct((seq, PG_W), _MXU), jax.ShapeDtypeStruct((seq, D), _MXU),
                   jax.ShapeDtypeStruct((seq, D), _MXU), jax.ShapeDtypeStruct((seq, AQ_W), F32),
                   jax.ShapeDtypeStruct((seq, R_W), F32), jax.ShapeDtypeStruct((seq, R_W), _MXU),
                   jax.ShapeDtypeStruct((1, R_W), F32)),
    )(dx1, o, y2, y2, pr, pg, pg, gain_r, wao, wro, woutt, waot, wrot)


def _qk_prep_bwd(pa, dqh, dk2, dv2, rdq, rdk, rdv, drg, gq, gk, seg, ca, sa, cr, sr):
    seq = pa.shape[0]
    tm = _tile(seq, 256)

    def body(pa_ref, dqh_ref, dk2_ref, dv2_ref, rdqf_ref, rdqb_ref, rdkf_ref, rdkb_ref, rdvf_ref, rdvb_ref,
             drg_ref, gq_ref, gk_ref, seg_ref, ca_ref, sa_ref, cr_ref, sr_ref,
             dpa_ref, dpr_ref, dgq_ref, dgk_ref):
        @pl.when(pl.program_id(0) == 0)
        def _():
            dgq_ref[...] = jnp.zeros_like(dgq_ref)
            dgk_ref[...] = jnp.zeros_like(dgk_ref)

        ca_, sa_ = ca_ref[...], sa_ref[...]

        def norm_bwd(raw, gain, dy, segm, dg_ref):
            msq = jnp.dot(raw * raw, segm, precision=HIGHEST, preferred_element_type=F32)
            r = lax.rsqrt(msq + EPS)
            n = raw * r
            dg_ref[...] += jnp.sum(dy * n, axis=0, keepdims=True)
            dn = dy * gain
            return r * (dn - n * jnp.dot(dn * n, segm, precision=HIGHEST, preferred_element_type=F32))

        dqn = _rope(dqh_ref[...] * (A_HD ** -0.5), _cat(ca_, 4), -_cat(sa_, 4), A_HD // 2)
        dpa_ref[:, 0:AQ_W] = _bf(norm_bwd(pa_ref[:, 0:AQ_W], gq_ref[...], dqn, seg_ref[...], dgq_ref))
        dkn = _rope(dk2_ref[...], ca_, -sa_, A_HD // 2)
        dpa_ref[:, AQ_W:AQ_W + AKV_W] = _bf(norm_bwd(pa_ref[:, AQ_W:AQ_W + AKV_W], gk_ref[...], dkn,
                                                     seg_ref[0:AKV_W, 0:AKV_W], dgk_ref))
        dpa_ref[:, AQ_W + AKV_W:PA_W] = _bf(dv2_ref[...])
        cr_, sr_ = _cat(cr_ref[...], 4), -_cat(sr_ref[...], 4)
        dpr_ref[:, 0:R_W] = _bf(_rope((rdqf_ref[0] + rdqb_ref[0]) * (R_HD ** -0.5), cr_, sr_, R_HD // 2))
        dpr_ref[:, R_W:2 * R_W] = _bf(_rope(rdkf_ref[0] + rdkb_ref[0], cr_, sr_, R_HD // 2))
        dpr_ref[:, 2 * R_W:3 * R_W] = _bf(rdvf_ref[0] + rdvb_ref[0])
        dpr_ref[:, 3 * R_W:4 * R_W] = drg_ref[...]

    row = lambda w_: pl.BlockSpec((tm, w_), lambda i: (i, 0))
    ydir = lambda d: pl.BlockSpec((1, tm, R_W), lambda i: (d, i, 0))
    return pl.pallas_call(
        body, name="qk_prep_bwd", grid=(seq // tm,),
        in_specs=[row(PA_W), row(AQ_W), row(AKV_W), row(AKV_W), ydir(0), ydir(1), ydir(0), ydir(1),
                  ydir(0), ydir(1), row(R_W), _full((1, AQ_W)), _full((1, AKV_W)), _full((AQ_W, AQ_W)),
                  row(LANES), row(LANES), row(LANES), row(LANES)],
        out_specs=(row(PA_W), row(PR_W), _full((1, AQ_W)), _full((1, AKV_W))),
        out_shape=(jax.ShapeDtypeStruct((seq, PA_W), _MXU), jax.ShapeDtypeStruct((seq, PR_W), _MXU),
                   jax.ShapeDtypeStruct((1, AQ_W), F32), jax.ShapeDtypeStruct((1, AKV_W), F32)),
    )(pa, dqh, dk2, dv2, rdq, rdq, rdk, rdk, rdv, rdv, drg, gq, gk, seg, ca, sa, cr, sr)


def _in_proj_bwd(x, dx1, gain, dpa, dpr, dpg, wint):
    seq = x.shape[0]
    tm = _tile(seq, 256)

    def body(x_ref, dx1_ref, g_ref, dpa_ref, dpr_ref, dpg_ref, wt_ref, dx_ref, dg_ref):
        @pl.when(pl.program_id(0) == 0)
        def _():
            dg_ref[...] = jnp.zeros_like(dg_ref)

        dh = (_mm(dpa_ref[...], wt_ref[0:PA_W, :]) + _mm(dpr_ref[...], wt_ref[PA_W:PA_W + PR_W, :])
              + _mm(dpg_ref[...], wt_ref[PA_W + PR_W:IN_W, :]))
        n, r = _rms(x_ref[...])
        dg_ref[...] += jnp.sum(dh * n, axis=0, keepdims=True)
        dx_ref[...] = dx1_ref[...] + _rms_bwd(n, r, g_ref[...], dh)

    row = lambda w_: pl.BlockSpec((tm, w_), lambda i: (i, 0))
    return pl.pallas_call(
        body, name="in_proj_bwd", grid=(seq // tm,),
        in_specs=[row(D), row(D), _full((1, D)), row(PA_W), row(PR_W), row(PG_W), _full((IN_W, D))],
        out_specs=(row(D), _full((1, D))),
        out_shape=(jax.ShapeDtypeStruct((seq, D), F32), jax.ShapeDtypeStruct((1, D), F32)),
    )(x, dx1, gain, dpa, dpr, dpg, wint)


def _wgrad(a, b, name):
    seq, m = a.shape
    n = b.shape[1]
    tm, tn, ts = _tile(m, 1024), _tile(n, 1024), _tile(seq, 2048)
    ns = seq // ts

    def body(a_ref, b_ref, o_ref):
        @pl.when(pl.program_id(2) == 0)
        def _():
            o_ref[...] = jnp.zeros_like(o_ref)

        o_ref[...] += _mm_tn(a_ref[...], b_ref[...])

    return pl.pallas_call(
        body, name=name, grid=(m // tm, n // tn, ns),
        in_specs=[pl.BlockSpec((ts, tm), lambda i, j, s: (s, i)), pl.BlockSpec((ts, tn), lambda i, j, s: (s, j))],
        out_specs=pl.BlockSpec((tm, tn), lambda i, j, s: (i, j)),
        out_shape=jax.ShapeDtypeStruct((m, n), F32),
    )(a, b)


def _adamw_math(w, g, m, v):
    m = B1 * m + (1.0 - B1) * g
    v = B2 * v + (1.0 - B2) * (g * g)
    m_hat = m / (1.0 - B1 ** STEP)
    v_hat = v / (1.0 - B2 ** STEP)
    delta = -LR * (m_hat / (jnp.sqrt(v_hat) + ADAM_EPS) + WD * w)
    return delta, m, v


def _adamw_big(land, own, w, m, v, name):
    rws, cols = w.shape
    tr = next(t for t in range(min(rws, 288), 0, -8) if rws % t == 0)

    def body(l_ref, o_ref, w_ref, m_ref, v_ref, g_ref, d_ref, nm_ref, nv_ref):
        x, y, c = _mesh_pos()
        me = 4 * x + 2 * y + c
        g = o_ref[...]
        for j in range(N_DEV):
            g = g + jnp.where(me == j, 0.0, l_ref[j].astype(F32))
        g_ref[...] = g
        d_ref[...], nm_ref[...], nv_ref[...] = _adamw_math(w_ref[...], g, m_ref[...], v_ref[...])

    row = pl.BlockSpec((tr, cols), lambda i: (i, 0))
    shp = jax.ShapeDtypeStruct((rws, cols), F32)
    return pl.pallas_call(
        body, name=name, grid=(rws // tr,),
        in_specs=[pl.BlockSpec((N_DEV, tr, cols), lambda i: (0, i, 0)), row, row, row, row],
        out_specs=(row, row, row, row), out_shape=(shp, shp, shp, shp),
    )(land, own, w, m, v)


def _adamw_small(sland, w, m, v):
    def body(l_ref, w_ref, m_ref, v_ref, g_ref, d_ref, nm_ref, nv_ref, loss_ref):
        s = l_ref[0]
        for j in range(1, N_DEV):
            s = s + l_ref[j]
        w = w_ref[...]
        gq = s[8:9]
        for h in range(1, A_H):
            gq = gq + s[8 + h:9 + h]
        gk = s[16:17] + s[17:18]
        gdec = s[5:6] * _sigmoid(-w[5:6])
        g = jnp.concatenate([s[0:5], gdec, gq, gk], axis=0)
        g_ref[...] = g
        d_ref[...], nm_ref[...], nv_ref[...] = _adamw_math(w, g, m_ref[...], v_ref[...])
        loss_ref[...] = s[6:7, 0:LANES]

    shp = jax.ShapeDtypeStruct((8, PACK_COLS), F32)
    return pl.pallas_call(
        body, name="adamw_small",
        out_shape=(shp, shp, shp, shp, jax.ShapeDtypeStruct((1, LANES), F32)),
    )(sland, w, m, v)


_BIG = (("w_attn_o", AQ_W, D, 1), ("w_ret_o", R_W, D, 1), ("w_out", D, D, 0),
        ("w_up", D, FF, 1), ("w_down", FF, D, 0), ("w_ple_gate", D, D, 0), ("w_ple", PLE, D, 1))
IN_SHARD = IN_W // N_DEV
_SMALL = ("mix_norm", "mlp_norm", "ple_norm", "final_norm", "ret_norm_gain", "ret_decay_logit",
          "attn_q_norm", "attn_k_norm")


def _shard_shape(rows, cols, axis):
    return (rows // N_DEV, cols) if axis == 0 else (rows, cols // N_DEV)


def _pack_shards(shards):
    flat = jnp.concatenate([s.reshape(-1) for s in shards])
    return flat.reshape(-1, PACK_COLS)


def _unpack_gathered(gathered):
    flat = gathered.reshape(N_DEV, -1)
    out, off = {}, 0
    for name, rows, cols, axis in _BIG:
        sr, sc = _shard_shape(rows, cols, axis)
        blk = flat[:, off:off + sr * sc].reshape(N_DEV, sr, sc)
        off += sr * sc
        out[name] = blk.reshape(rows, cols) if axis == 0 else blk.transpose(1, 0, 2).reshape(rows, cols)
    return out


def _pack_full_grads(grads):
    parts = []
    for name, rows, cols, axis in _BIG:
        sr, sc = _shard_shape(rows, cols, axis)
        g = grads[name]
        blk = g.reshape(N_DEV, sr, sc) if axis == 0 else g.reshape(rows, N_DEV, sc).transpose(1, 0, 2)
        parts.append(blk.reshape(N_DEV, -1))
    flat = jnp.concatenate(parts, axis=1)
    return flat.reshape(N_DEV, -1, PACK_COLS)


def _unpack_shard(packed):
    flat = packed.reshape(-1)
    out, off = {}, 0
    for name, rows, cols, axis in _BIG:
        sr, sc = _shard_shape(rows, cols, axis)
        out[name] = flat[off:off + sr * sc].reshape(1, sr, sc)
        off += sr * sc
    return out


def _pack_small(vals):
    rows = [jnp.pad(vals[n].reshape(-1), (0, PACK_COLS - vals[n].size)) for n in _SMALL]
    return jnp.stack(rows)


def _unpack_small(packed, like):
    return {n: packed[i, :like[n].size].reshape(like[n].shape) for i, n in enumerate(_SMALL)}


def _row(v):
    return jnp.pad(v.reshape(-1), (0, PACK_COLS - v.size))


def kernel(x, p, mix_norm, w_in, attn_q_norm, attn_k_norm, ret_decay_logit, ret_norm_gain, w_attn_o, w_ret_o, w_out, mlp_norm, w_up, w_down, ple_norm, w_ple_gate, w_ple, final_norm, loss_target, m_mix_norm, m_w_in, m_attn_q_norm, m_attn_k_norm, m_ret_decay_logit, m_ret_norm_gain, m_w_attn_o, m_w_ret_o, m_w_out, m_mlp_norm, m_w_up, m_w_down, m_ple_norm, m_w_ple_gate, m_w_ple, m_final_norm, v_mix_norm, v_w_in, v_attn_q_norm, v_attn_k_norm, v_ret_decay_logit, v_ret_norm_gain, v_w_attn_o, v_w_ret_o, v_w_out, v_mlp_norm, v_w_up, v_w_down, v_ple_norm, v_w_ple_gate, v_w_ple, v_final_norm):
    args = dict(locals())
    seq = x.shape[1]
    xs = x[0]
    ps = p[0, 0]
    tgt = loss_target[0]

    big_names = [b[0] for b in _BIG]
    wshard = _pack_shards([args[n] for n in big_names])
    win_g, rest_g = _all_gather([w_in[0].astype(_MXU), wshard.astype(_MXU)])
    win = win_g.transpose(1, 0, 2).reshape(D, IN_W)
    wfull = _unpack_gathered(rest_g)
    wao, wro, wout = wfull["w_attn_o"], wfull["w_ret_o"], wfull["w_out"]
    wup, wdown, wpg, wple = wfull["w_up"], wfull["w_down"], wfull["w_ple_gate"], wfull["w_ple"]

    g_mix, g_mlp, g_ple = mix_norm, mlp_norm, ple_norm
    g_fin = final_norm.reshape(1, D)
    gq = jnp.tile(attn_q_norm, (1, A_H))
    gk = jnp.tile(attn_k_norm, (1, A_KV))
    seg = _seg_mean_matrix()
    ca, sa, cr, sr = _rope_tables(seq)

    pa, pr, pg, h = _in_proj(xs, g_mix, win)
    qh, kh, vh, rqh, rkh = _qk_prep(pa, pr, gq, gk, seg, ca, sa, cr, sr)

    tq = _tile(seq, 128)
    tk = _tile(seq // 4, 1024)
    qt8 = qh.reshape(seq, A_H, A_HD).transpose(1, 2, 0)
    vta = jnp.stack([jnp.concatenate([_chunk_t(vh[:, g * A_HD:(g + 1) * A_HD], tk),
                                      jnp.ones((seq // tk, 16, tk), _MXU)], axis=1) for g in range(A_KV)])
    ot, lse = _attn_fwd(qt8, kh, vta, tq, tk)
    o = _heads_to_rows(ot)

    zb = jnp.broadcast_to(ret_decay_logit.reshape(2 * R_H, 1, 1), (2 * R_H, 1, LANES))
    tm_, tmw, tqd, tqdw, tkd, tkdw, tg, tgw = _ret_tables(zb)
    cb = _tile(seq // CHUNK, 8)
    y2, pst = _ret_fwd(rqh, rkh, pr, tm_, tqd, tkd, tg, cb)

    x1, merged, ret_in = _merge_fwd(xs, o, y2, pr, pg, ret_norm_gain, wao, wro, wout)
    x2 = _mlp_fwd(x1, g_mlp, wup, wdown)

    dx2, de, dz, hp, loss_p, dg_fin, dg_ple = _ple_loss(x2, ps, tgt, g_ple, g_fin, wpg, wpg.T, wple)
    dx1, act, du, hm, dg_mlp = _mlp_bwd(x1, dx2, g_mlp, wup, wdown.T, wup.T)
    dpg, dao, dro, do, dry, drg, dg_gn = _merge_bwd(dx1, o, y2, pr, pg, ret_norm_gain, wao, wro,
                                                    wout.T, wao.T, wro.T)
    rdq, rdk, rdv, dlam = _ret_bwd(rqh, rkh, pr, dry, pst, tm_, tmw, tqd, tqdw, tkd, tkdw, tg, tgw, cb)

    ksplit = 2
    tkb = _tile(seq // 4, 512)
    dot_ = do.reshape(seq, A_H, A_HD).transpose(1, 2, 0)
    dkt, dvt, dqt = _attn_bwd(qt8, ot, dot_, lse, kh, vh, _chunk_t(kh, tkb), tq, tkb, ksplit)
    dqh = _heads_to_rows(jnp.sum(dqt, axis=0))
    dpa, dpr, dg_q, dg_k = _qk_prep_bwd(pa, dqh, _chunks_to_rows(dkt), _chunks_to_rows(dvt), rdq, rdk, rdv, drg, gq, gk, seg, ca, sa, cr, sr)
    grad_x, dg_mix = _in_proj_bwd(xs, dx1, g_mix, dpa, dpr, dpg, win.T)

    wg = {
        "w_in": jnp.concatenate([_wgrad(h, dpa, "wgrad_in_a"), _wgrad(h, dpr, "wgrad_in_r"),
                                 _wgrad(h, dpg, "wgrad_in_g")], axis=1),
        "w_attn_o": _wgrad(o, dao, "wgrad_attn_o"),
        "w_ret_o": _wgrad(ret_in, dro, "wgrad_ret_o"),
        "w_out": _wgrad(merged, dx1, "wgrad_out"),
        "w_up": _wgrad(hm, du, "wgrad_up"),
        "w_down": _wgrad(act, dx2, "wgrad_down"),
        "w_ple_gate": _wgrad(hp, dz, "wgrad_ple_gate"),
        "w_ple": _wgrad(ps, de, "wgrad_ple"),
    }
    gpack = _pack_full_grads(wg)
    gpack_in = wg["w_in"].reshape(D, N_DEV, IN_SHARD).transpose(1, 0, 2)
    small = jnp.stack(
        [_row(dg_mix), _row(dg_mlp), _row(dg_ple), _row(dg_fin), _row(dg_gn), _row(dlam[:, 0, 0]),
         _row(loss_p[0, 0:1]), jnp.zeros((PACK_COLS,), F32)]
        + [_row(dg_q[0, hh * A_HD:(hh + 1) * A_HD]) for hh in range(A_H)]
        + [_row(dg_k[0, hh * A_HD:(hh + 1) * A_HD]) for hh in range(A_KV)]
        + [jnp.zeros((PACK_COLS,), F32)] * (SMALL_ROWS - 18))

    me = 4 * lax.axis_index("x") + 2 * lax.axis_index("y") + lax.axis_index("c")
    own_in = lax.dynamic_index_in_dim(gpack_in, me, axis=0, keepdims=False)
    own = lax.dynamic_index_in_dim(gpack, me, axis=0, keepdims=False)
    land_in, land, sland = _exchange_grads([_bf(gpack_in), _bf(gpack), small[None]])
    in_sh = _adamw_big(land_in, own_in, w_in[0], m_w_in[0], v_w_in[0], "adamw_w_in")
    g_sh, d_sh, m_sh, v_sh = _adamw_big(land, own, wshard, _pack_shards([args["m_" + n] for n in big_names]),
                                        _pack_shards([args["v_" + n] for n in big_names]), "adamw_shard")
    g_sm, d_sm, m_sm, v_sm, loss_row = _adamw_small(
        sland, _pack_small({n: args[n] for n in _SMALL}), _pack_small({n: args["m_" + n] for n in _SMALL}),
        _pack_small({n: args["v_" + n] for n in _SMALL}))

    names = ["mix_norm", "w_in", "attn_q_norm", "attn_k_norm", "ret_decay_logit", "ret_norm_gain", "w_attn_o",
             "w_ret_o", "w_out", "mlp_norm", "w_up", "w_down", "ple_norm", "w_ple_gate", "w_ple", "final_norm"]
    like = {n: args[n] for n in _SMALL}
    outs = [loss_row[0, 0], grad_x[None]]
    for big, sm, w_in_part in ((g_sh, g_sm, in_sh[0]), (d_sh, d_sm, in_sh[1]), (m_sh, m_sm, in_sh[2]),
                               (v_sh, v_sm, in_sh[3])):
        table = {**_unpack_shard(big), **_unpack_small(sm, like), "w_in": w_in_part[None]}
        outs += [table[n] for n in names]
    return tuple(outs)
```

```python
import functools

import jax
import jax.numpy as jnp
from jax import lax
from jax.experimental import pallas as pl
from jax.experimental.pallas import tpu as pltpu

F32 = jnp.float32
_MXU = jnp.bfloat16

D = 1024
PLE = 256
GRID_W = 64
A_HD = 64
A_H = 8
A_KV = 2
A_G = A_H // A_KV
AQ_W = A_H * A_HD
AKV_W = A_KV * A_HD
R_HD = 128
R_H = 4
R_W = R_H * R_HD
IN_W = AQ_W + 2 * AKV_W + 4 * R_W + 2 * D
PA_W = AQ_W + 2 * AKV_W
PR_W = 4 * R_W
PG_W = 2 * D
FF = 4 * D
CHUNK = 128
ROPE_THETA = 10000.0
EPS = 1e-6
GN_EPS = 1e-5
N_DEV = 8

LR, B1, B2, ADAM_EPS, WD, STEP = 0.001, 0.9, 0.999, 1e-08, 0.01, 10

LANES = 128
PACK_COLS = 1024
SMALL_ROWS = 24
HIGHEST = lax.Precision.HIGHEST


def _tile(n, pref):
    t = min(n, pref)
    assert n % t == 0, (n, t)
    return t


def _bf(a):
    return a.astype(_MXU)


def _mm(a, b):
    return jnp.dot(_bf(a), _bf(b), preferred_element_type=F32)


def _mm_nt(a, b):
    return lax.dot_general(_bf(a), _bf(b), (((1,), (1,)), ((), ())), preferred_element_type=F32)


def _mm_tn(a, b):
    return lax.dot_general(_bf(a), _bf(b), (((0,), (0,)), ((), ())), preferred_element_type=F32)


def _sigmoid(z):
    return 1.0 / (1.0 + jnp.exp(-z))


def _rms(x):
    r = lax.rsqrt(jnp.mean(x * x, axis=-1, keepdims=True) + EPS)
    return x * r, r


def _rms_bwd(n, r, gain, dy):
    dn = dy * gain
    return r * (dn - n * jnp.mean(dn * n, axis=-1, keepdims=True))


def _swap_halves(x, half):
    n = x.shape[-1]
    lane = lax.broadcasted_iota(jnp.int32, x.shape, x.ndim - 1)
    first = (lane % (2 * half)) < half
    return jnp.where(first, pltpu.roll(x, n - half, axis=1), pltpu.roll(x, half, axis=1))


def _rope(x, cos, sin, half):
    return x * cos + _swap_halves(x, half) * sin


def _cat(t, reps):
    return jnp.concatenate([t] * reps, axis=1)


def _full(shape):
    nd = len(shape)
    return pl.BlockSpec(shape, lambda *_: (0,) * nd)


def _rope_tables(seq):
    def tab(head_dim):
        n_axis = head_dim // 4
        freqs = ROPE_THETA ** (-jnp.arange(n_axis, dtype=F32) / n_axis)
        rows = seq // GRID_W
        row = jnp.repeat(jnp.arange(rows, dtype=F32), GRID_W)
        col = jnp.tile(jnp.arange(GRID_W, dtype=F32), rows)
        ang = jnp.concatenate([row[:, None] * freqs, col[:, None] * freqs], axis=-1)
        c, s = jnp.cos(ang), jnp.sin(ang)
        return jnp.concatenate([c, c], axis=-1), jnp.concatenate([-s, s], axis=-1)
    ca, sa = tab(A_HD)
    cr, sr = tab(R_HD)
    return jnp.tile(ca, (1, 2)), jnp.tile(sa, (1, 2)), cr, sr


def _seg_mean_matrix():
    i = jnp.arange(AQ_W) // A_HD
    return (i[:, None] == i[None, :]).astype(F32) / A_HD


def _mesh_pos():
    return lax.axis_index("x"), lax.axis_index("y"), lax.axis_index("c")


def _all_gather(shards):
    n = len(shards)

    def body(*refs):
        x_refs, out_refs = refs[:n], refs[n:2 * n]
        send_sems, recv_sems, local_sems = refs[2 * n:]
        x, y, c = _mesh_pos()
        me, sibling = (x, y, c), (x, y, 1 - c)
        chips = [(1 - x, y), (x, 1 - y), (1 - x, 1 - y)]

        def slot(t, px, py, pc):
            return out_refs[t].at[4 * px + 2 * py + pc]

        def copy(t, k, block, to, src=None):
            return pltpu.make_async_remote_copy(
                src_ref=slot(t, *block) if src is None else src, dst_ref=slot(t, *block),
                send_sem=send_sems.at[7 * t + k], recv_sem=recv_sems.at[7 * t + k],
                device_id=to, device_id_type=pl.DeviceIdType.MESH)

        mine = [pltpu.make_async_copy(x_refs[t], slot(t, *me), local_sems.at[t]) for t in range(n)]
        for cp in mine:
            cp.start()
        first = []
        for t in range(n):
            first.append(copy(t, 0, me, sibling, src=x_refs[t]))
            first += [copy(t, 1 + j, me, (*chip, c), src=x_refs[t]) for j, chip in enumerate(chips)]
        for cp in first:
            cp.start()
        passed = []
        for t in range(n):
            for j, chip in enumerate(chips):
                copy(t, 1 + j, (*chip, c), me).wait_recv()
                passed.append(copy(t, 4 + j, (*chip, c), sibling))
                passed[-1].start()
        for t in range(n):
            copy(t, 0, sibling, me).wait_recv()
            for j, chip in enumerate(chips):
                copy(t, 4 + j, (*chip, 1 - c), me).wait_recv()
        for cp in first + passed:
            cp.wait_send()
        for cp in mine:
            cp.wait()

    anyspec = pl.BlockSpec(memory_space=pl.ANY)
    return pl.pallas_call(
        body, name="all_gather_weights",
        out_shape=tuple(jax.ShapeDtypeStruct((N_DEV,) + s.shape, s.dtype) for s in shards),
        in_specs=[anyspec] * n, out_specs=(anyspec,) * n,
        scratch_shapes=[pltpu.SemaphoreType.DMA((7 * n,)), pltpu.SemaphoreType.DMA((7 * n,)),
                        pltpu.SemaphoreType.DMA((n,))],
    )(*shards)


def _exchange_grads(packs):
    n = len(packs)

    def body(*refs):
        g_refs, land_refs = refs[:n], refs[n:2 * n]
        send_sems, recv_sems, local_sems = refs[2 * n:]
        x, y, c = _mesh_pos()
        me = 4 * x + 2 * y + c

        def row(t, j):
            return g_refs[t].at[j if packs[t].shape[0] == N_DEV else 0]

        own = [pltpu.make_async_copy(row(t, me), land_refs[t].at[me], local_sems.at[t]) for t in range(n)]
        for cp in own:
            cp.start()
        sends = []
        for k in range(1, N_DEV):
            peer = (x ^ ((k >> 2) & 1), y ^ ((k >> 1) & 1), c ^ (k & 1))
            pidx = 4 * peer[0] + 2 * peer[1] + peer[2]
            for t in range(n):
                sends.append(pltpu.make_async_remote_copy(
                    src_ref=row(t, pidx), dst_ref=land_refs[t].at[me],
                    send_sem=send_sems.at[7 * t + k - 1], recv_sem=recv_sems.at[7 * t + k - 1],
                    device_id=peer, device_id_type=pl.DeviceIdType.MESH))
                sends[-1].start()
        for k in range(1, N_DEV):
            peer = (x ^ ((k >> 2) & 1), y ^ ((k >> 1) & 1), c ^ (k & 1))
            pidx = 4 * peer[0] + 2 * peer[1] + peer[2]
            for t in range(n):
                pltpu.make_async_remote_copy(
                    src_ref=row(t, me), dst_ref=land_refs[t].at[pidx],
                    send_sem=send_sems.at[7 * t + k - 1], recv_sem=recv_sems.at[7 * t + k - 1],
                    device_id=peer, device_id_type=pl.DeviceIdType.MESH).wait_recv()
        for cp in sends:
            cp.wait_send()
        for cp in own:
            cp.wait()

    anyspec = pl.BlockSpec(memory_space=pl.ANY)
    return pl.pallas_call(
        body, name="exchange_grads",
        out_shape=tuple(jax.ShapeDtypeStruct((N_DEV,) + g.shape[1:], g.dtype) for g in packs),
        in_specs=[anyspec] * n, out_specs=(anyspec,) * n,
        scratch_shapes=[pltpu.SemaphoreType.DMA((7 * n,)), pltpu.SemaphoreType.DMA((7 * n,)),
                        pltpu.SemaphoreType.DMA((n,))],
    )(*packs)


def _in_proj(x, gain, w):
    seq = x.shape[0]
    tm = _tile(seq, 256)

    def body(x_ref, g_ref, w_ref, pa_ref, pr_ref, pg_ref, h_ref):
        n, _ = _rms(x_ref[...])
        h = _bf(n * g_ref[...])
        h_ref[...] = h
        pa_ref[...] = _mm(h, w_ref[:, 0:PA_W])
        pr_ref[...] = _mm(h, w_ref[:, PA_W:PA_W + PR_W])
        pg_ref[...] = _mm(h, w_ref[:, PA_W + PR_W:IN_W])

    row = lambda w_: pl.BlockSpec((tm, w_), lambda i: (i, 0))
    return pl.pallas_call(
        body, name="in_proj", grid=(seq // tm,),
        in_specs=[row(D), _full((1, D)), _full((D, IN_W))],
        out_specs=(row(PA_W), row(PR_W), row(PG_W), row(D)),
        out_shape=(jax.ShapeDtypeStruct((seq, PA_W), F32), jax.ShapeDtypeStruct((seq, PR_W), F32),
                   jax.ShapeDtypeStruct((seq, PG_W), F32), jax.ShapeDtypeStruct((seq, D), _MXU)),
    )(x, gain, w)


def _qk_prep(pa, pr, gq, gk, seg, ca, sa, cr, sr):
    seq = pa.shape[0]
    tm = _tile(seq, 256)

    def body(pa_ref, pr_ref, gq_ref, gk_ref, seg_ref, ca_ref, sa_ref, cr_ref, sr_ref,
             qh_ref, kh_ref, v_ref, rq_ref, rk_ref):
        q = pa_ref[:, 0:AQ_W]
        k = pa_ref[:, AQ_W:AQ_W + AKV_W]
        v_ref[...] = _bf(pa_ref[:, AQ_W + AKV_W:PA_W])
        ca_, sa_ = ca_ref[...], sa_ref[...]
        msq = jnp.dot(q * q, seg_ref[...], precision=HIGHEST, preferred_element_type=F32)
        qn = q * lax.rsqrt(msq + EPS) * gq_ref[...]
        qh_ref[...] = _bf(_rope(qn, _cat(ca_, 4), _cat(sa_, 4), A_HD // 2) * (A_HD ** -0.5))
        msk = jnp.dot(k * k, seg_ref[0:AKV_W, 0:AKV_W], precision=HIGHEST, preferred_element_type=F32)
        kn = k * lax.rsqrt(msk + EPS) * gk_ref[...]
        kh_ref[...] = _bf(_rope(kn, ca_, sa_, A_HD // 2))
        cr_, sr_ = _cat(cr_ref[...], 4), _cat(sr_ref[...], 4)
        rq_ref[...] = _rope(pr_ref[:, 0:R_W], cr_, sr_, R_HD // 2) * (R_HD ** -0.5)
        rk_ref[...] = _rope(pr_ref[:, R_W:2 * R_W], cr_, sr_, R_HD // 2)

    row = lambda w_: pl.BlockSpec((tm, w_), lambda i: (i, 0))
    return pl.pallas_call(
        body, name="qk_prep", grid=(seq // tm,),
        in_specs=[row(PA_W), row(2 * R_W), _full((1, AQ_W)), _full((1, AKV_W)), _full((AQ_W, AQ_W)),
                  row(LANES), row(LANES), row(LANES), row(LANES)],
        out_specs=(row(AQ_W), row(AKV_W), row(AKV_W), row(R_W), row(R_W)),
        out_shape=(jax.ShapeDtypeStruct((seq, AQ_W), _MXU), jax.ShapeDtypeStruct((seq, AKV_W), _MXU),
                   jax.ShapeDtypeStruct((seq, AKV_W), _MXU), jax.ShapeDtypeStruct((seq, R_W), F32),
                   jax.ShapeDtypeStruct((seq, R_W), F32)),
    )(pa, pr, gq, gk, seg, ca, sa, cr, sr)


def _chunk_t(a, tk):
    seq = a.shape[0]
    return a.reshape(seq // tk, tk, a.shape[1]).transpose(0, 2, 1)


def _heads_to_rows(t):
    return t.transpose(2, 0, 1).reshape(t.shape[2], AQ_W)


def _attn_fwd(qt8, k2, vta, tq, tk):
    seq = k2.shape[0]
    nck = seq // tk
    rows = A_G * tq
    vrows = vta.shape[2]
    rb = _tile(tk, 64)
    assert nck % 2 == 0, nck

    def body(qt_ref, k_ref, vt_ref, o_ref, lse_ref, m_sc, acc_sc, qtp_sc, s_a, s_b, p_a, p_b, al_a, al_b):
        g = pl.program_id(0)
        qtp_sc[...] = jnp.zeros_like(qtp_sc)
        qtp_sc[pl.ds(pl.multiple_of(g * A_HD, A_HD), A_HD), :] = jnp.concatenate(
            [qt_ref[a] for a in range(A_G)], axis=1)
        m_sc[...] = jnp.full((1, rows), -jnp.inf, F32)
        acc_sc[...] = jnp.zeros_like(acc_sc)

        def scores(c):
            kc = k_ref[pl.ds(pl.multiple_of(c * tk, tk), tk), :]
            return _mm(kc, qtp_sc[...])

        def stage(c, s_cur, s_nxt, p_cur, p_prv, al_cur, al_prv, first=False, last=False):
            if not last:
                s_nxt[...] = scores(c + 1)
            if not first:
                acc_sc[...] = al_prv[...] * acc_sc[...] + _mm(vt_ref[0, c - 1], p_prv[...])
            m_old = m_sc[...]
            mx = None
            for r in range(0, tk, rb):
                bm = jnp.max(s_cur[r:r + rb, :].reshape(rb // 8, 8, rows), axis=0)
                mx = bm if mx is None else jnp.maximum(mx, bm)
            m_new = jnp.maximum(m_old, jnp.max(mx, axis=0, keepdims=True))
            for r in range(0, tk, rb):
                p_cur[r:r + rb, :] = _bf(jnp.exp(s_cur[r:r + rb, :] - m_new))
            al_cur[...] = jnp.exp(m_old - m_new)
            m_sc[...] = m_new

        s_a[...] = scores(0)
        stage(0, s_a, s_b, p_a, p_b, al_a, al_b, first=True)

        def pair(j, carry):
            stage(2 * j + 1, s_b, s_a, p_b, p_a, al_b, al_a)
            stage(2 * j + 2, s_a, s_b, p_a, p_b, al_a, al_b)
            return carry

        lax.fori_loop(0, nck // 2 - 1, pair, 0)
        stage(nck - 1, s_b, s_a, p_b, p_a, al_b, al_a, last=True)
        acc = al_b[...] * acc_sc[...] + _mm(vt_ref[0, nck - 1], p_b[...])
        l = acc[A_HD:A_HD + 1, :]
        lse = m_sc[...] + jnp.log(l)
        out = acc[0:A_HD, :] * (1.0 / l)
        for a in range(A_G):
            o_ref[a] = out[:, a * tq:(a + 1) * tq]
            lse_ref[a] = lse[:, a * tq:(a + 1) * tq]

    return pl.pallas_call(
        body, name="attn_fwd", grid=(A_KV, seq // tq),
        in_specs=[pl.BlockSpec((A_G, A_HD, tq), lambda g, i: (g, 0, i)),
                  _full((seq, LANES)), pl.BlockSpec((1, nck, vrows, tk), lambda g, i: (g, 0, 0, 0))],
        out_specs=(pl.BlockSpec((A_G, A_HD, tq), lambda g, i: (g, 0, i)),
                   pl.BlockSpec((A_G, 1, tq), lambda g, i: (g, 0, i))),
        out_shape=(jax.ShapeDtypeStruct((A_H, A_HD, seq), F32), jax.ShapeDtypeStruct((A_H, 1, seq), F32)),
        scratch_shapes=[pltpu.VMEM((1, rows), F32), pltpu.VMEM((vrows, rows), F32), pltpu.VMEM((LANES, rows), _MXU),
                        pltpu.VMEM((tk, rows), F32), pltpu.VMEM((tk, rows), F32),
                        pltpu.VMEM((tk, rows), _MXU), pltpu.VMEM((tk, rows), _MXU),
                        pltpu.VMEM((1, rows), F32), pltpu.VMEM((1, rows), F32)],
    )(qt8, k2, vta)


def _attn_bwd(qt8, ot, dot_, lse, k2, v2, k2t, tq, tk, ksplit):
    seq = k2.shape[0]
    sh = seq // ksplit
    nck = sh // tk
    rows = A_G * tq
    rb = _tile(tk, 32768 // rows)
    assert nck % 2 == 0, nck

    def body(qt_ref, ot_ref, dot_ref, lse_ref, k_ref, v_ref, kt_ref,
             dk_ref, dv_ref, dq_ref, dq_sc, qtp_sc, dotp_sc, pt_sc, dst_sc,
             s_a, s_b, dp_a, dp_b, p_a, p_b, ds_a, ds_b):
        g = pl.program_id(1)
        hrows = pl.ds(pl.multiple_of(g * A_HD, A_HD), A_HD)

        @pl.when(pl.program_id(2) == 0)
        def _():
            dk_ref[...] = jnp.zeros_like(dk_ref)
            dv_ref[...] = jnp.zeros_like(dv_ref)

        lse_row = jnp.concatenate([lse_ref[a] for a in range(A_G)], axis=1)
        dd = jnp.concatenate([jnp.sum(ot_ref[a] * dot_ref[a], axis=0, keepdims=True)
                              for a in range(A_G)], axis=1)
        qtp_sc[...] = jnp.zeros_like(qtp_sc)
        dotp_sc[...] = jnp.zeros_like(dotp_sc)
        qtp_sc[hrows, :] = jnp.concatenate([qt_ref[a] for a in range(A_G)], axis=1)
        dotp_sc[hrows, :] = _bf(jnp.concatenate([dot_ref[a] for a in range(A_G)], axis=1))
        dq_sc[...] = jnp.zeros_like(dq_sc)

        def products(c, s_ref, dp_ref):
            sl = pl.ds(pl.multiple_of(c * tk, tk), tk)
            s_ref[...] = _mm(k_ref[sl, :], qtp_sc[...])
            dp_ref[...] = _mm(v_ref[sl, :], dotp_sc[...])

        def accumulate(c, p_ref, ds_ref):
            pt_sc[...] = p_ref[...].T
            dst_sc[...] = ds_ref[...].T
            dq_sc[...] += _mm(kt_ref[c, hrows, :], ds_ref[...])
            dv_ref[0, c] += _mm(dotp_sc[hrows, :], pt_sc[...])
            dk_ref[0, c] += _mm(qtp_sc[hrows, :], dst_sc[...])

        def stage(c, s_cur, dp_cur, s_nxt, dp_nxt, p_cur, ds_cur, p_prv, ds_prv, first=False, last=False):
            if not last:
                products(c + 1, s_nxt, dp_nxt)
            if not first:
                accumulate(c - 1, p_prv, ds_prv)
            for r in range(0, tk, rb):
                p = jnp.exp(s_cur[r:r + rb, :] - lse_row)
                p_cur[r:r + rb, :] = _bf(p)
                ds_cur[r:r + rb, :] = _bf(p * (dp_cur[r:r + rb, :] - dd))

        products(0, s_a, dp_a)
        stage(0, s_a, dp_a, s_b, dp_b, p_a, ds_a, p_b, ds_b, first=True)

        def pair(j, carry):
            stage(2 * j + 1, s_b, dp_b, s_a, dp_a, p_b, ds_b, p_a, ds_a)
            stage(2 * j + 2, s_a, dp_a, s_b, dp_b, p_a, ds_a, p_b, ds_b)
            return carry

        lax.fori_loop(0, nck // 2 - 1, pair, 0)
        stage(nck - 1, s_b, dp_b, s_a, dp_a, p_b, ds_b, p_a, ds_a, last=True)
        accumulate(nck - 1, p_b, ds_b)
        for a in range(A_G):
            dq_ref[0, a] = dq_sc[:, a * tq:(a + 1) * tq]

    tspec = pl.BlockSpec((A_G, A_HD, tq), lambda s, g, i: (g, 0, i))
    kspec = pl.BlockSpec((sh, LANES), lambda s, g, i: (s, 0))
    gspec = pl.BlockSpec((1, nck, A_HD, tk), lambda s, g, i: (g, s, 0, 0))
    gshape = jax.ShapeDtypeStruct((A_KV, seq // tk, A_HD, tk), F32)
    big = lambda dt: pltpu.VMEM((tk, rows), dt)
    bigt = pltpu.VMEM((rows, tk), _MXU)
    return pl.pallas_call(
        body, name="attn_bwd", grid=(ksplit, A_KV, seq // tq),
        in_specs=[tspec, tspec, tspec, pl.BlockSpec((A_G, 1, tq), lambda s, g, i: (g, 0, i)),
                  kspec, kspec, pl.BlockSpec((nck, LANES, tk), lambda s, g, i: (s, 0, 0))],
        out_specs=(gspec, gspec, pl.BlockSpec((1, A_G, A_HD, tq), lambda s, g, i: (s, g, 0, i))),
        out_shape=(gshape, gshape, jax.ShapeDtypeStruct((ksplit, A_H, A_HD, seq), F32)),
        scratch_shapes=[pltpu.VMEM((A_HD, rows), F32), pltpu.VMEM((LANES, rows), _MXU), pltpu.VMEM((LANES, rows), _MXU),
                        bigt, bigt,
                        big(F32), big(F32), big(F32), big(F32), big(_MXU), big(_MXU), big(_MXU), big(_MXU)],
    )(qt8, ot, dot_, lse, k2, v2, k2t)


def _chunks_to_rows(t):
    return t.transpose(1, 3, 0, 2).reshape(t.shape[1] * t.shape[3], AKV_W)


def _ret_tables(zb):
    c = CHUNK

    def body(z_ref, m_ref, mw_ref, qd_ref, qdw_ref, kd_ref, kdw_ref, g_ref, gw_ref):
        fwd = pl.program_id(0) < R_H
        z = z_ref[0]
        lam = jnp.minimum(z, 0.0) - jnp.log(1.0 + jnp.exp(-jnp.abs(z)))
        i = lax.broadcasted_iota(jnp.int32, (c, c), 0).astype(F32)
        j = lax.broadcasted_iota(jnp.int32, (c, c), 1).astype(F32)
        diff = jnp.where(fwd, i - j, j - i)
        keep = diff >= jnp.where(fwd, 0.0, 1.0)
        dist = jnp.maximum(diff, 0.0)
        m = jnp.where(keep, jnp.exp(lam * dist), 0.0)
        m_ref[0] = m
        mw_ref[0] = m * dist
        fq = jnp.where(fwd, i + 1.0, c - i)
        qd = jnp.exp(lam * fq)
        qd_ref[0] = qd
        qdw_ref[0] = qd * fq
        fk = jnp.where(fwd, c - 1.0 - i, i)
        kd = jnp.exp(lam * fk)
        kd_ref[0] = kd
        kdw_ref[0] = kd * fk
        gdec = jnp.exp(lam * c)
        g_ref[0] = gdec
        gw_ref[0] = gdec * c

    big = pl.BlockSpec((1, c, c), lambda t: (t, 0, 0))
    vec = pl.BlockSpec((1, 1, LANES), lambda t: (t, 0, 0))
    bshape = jax.ShapeDtypeStruct((2 * R_H, c, c), F32)
    vshape = jax.ShapeDtypeStruct((2 * R_H, 1, LANES), F32)
    return pl.pallas_call(
        body, name="ret_tables", grid=(2 * R_H,), in_specs=[vec],
        out_specs=(big, big, big, big, big, big, vec, vec),
        out_shape=(bshape,) * 6 + (vshape, vshape),
    )(zb)


def _ret_fwd(rq, rk, pr, m, qd, kd, gdec, cb):
    seq = rq.shape[0]
    c = CHUNK
    ns = seq // (cb * c)

    def body(q_ref, k_ref, v_ref, m_ref, qd_ref, kd_ref, g_ref, y_ref, pst_ref, p_sc):
        d = pl.program_id(0)

        @pl.when(pl.program_id(1) == 0)
        def _():
            p_sc[...] = jnp.zeros_like(p_sc)

        def chunk(j, carry):
            cc = jnp.where(d == 0, j, cb - 1 - j)
            sl = pl.ds(pl.multiple_of(cc * c, c), c)
            heads = [slice(h * R_HD, (h + 1) * R_HD) for h in range(R_H)]
            qk = [_mm_nt(q_ref[sl, hs], k_ref[sl, hs]) for hs in heads]
            qp = [_mm(q_ref[sl, hs] * qd_ref[h], p_sc[h]) for h, hs in enumerate(heads)]
            kv = [_mm_tn(k_ref[sl, hs] * kd_ref[h], v_ref[sl, hs]) for h, hs in enumerate(heads)]
            for h, hs in enumerate(heads):
                p = p_sc[h]
                pst_ref[h, cc] = p
                y_ref[0, sl, hs] = _mm(qk[h] * m_ref[h], v_ref[sl, hs]) + qp[h]
                p_sc[h] = p * g_ref[h] + kv[h]
            return carry

        lax.fori_loop(0, cb, chunk, 0)

    def step(d, n):
        return d * (ns - 1 - n) + (1 - d) * n

    blk = lambda off: pl.BlockSpec((cb * c, R_W), lambda d, n: (step(d, n), off))
    big = pl.BlockSpec((R_H, c, c), lambda d, n: (d, 0, 0))
    vec = pl.BlockSpec((R_H, 1, LANES), lambda d, n: (d, 0, 0))
    return pl.pallas_call(
        body, name="ret_fwd", grid=(2, ns),
        in_specs=[blk(0), blk(0), blk(2), big, big, big, vec],
        out_specs=(pl.BlockSpec((1, cb * c, R_W), lambda d, n: (d, step(d, n), 0)),
                   pl.BlockSpec((R_H, cb, R_HD, R_HD), lambda d, n: (d, step(d, n), 0, 0))),
        out_shape=(jax.ShapeDtypeStruct((2, seq, R_W), F32),
                   jax.ShapeDtypeStruct((2 * R_H, seq // c, R_HD, R_HD), F32)),
        scratch_shapes=[pltpu.VMEM((R_H, R_HD, R_HD), F32)],
    )(rq, rk, pr, m, qd, kd, gdec)


def _ret_bwd(rq, rk, pr, dry, pst, m, mw, qd, qdw, kd, kdw, gdec, gw, cb):
    seq = rq.shape[0]
    c = CHUNK
    ns = seq // (cb * c)

    def body(q_ref, k_ref, v_ref, dy_ref, pst_ref, m_ref, mw_ref, qd_ref, qdw_ref, kd_ref, kdw_ref,
             g_ref, gw_ref, dq_ref, dk_ref, dv_ref, dlam_ref, r_sc, acc_sc, e_sc, g_sc):
        d = pl.program_id(0)
        n = pl.program_id(1)

        @pl.when(n == 0)
        def _():
            r_sc[...] = jnp.zeros_like(r_sc)
            acc_sc[...] = jnp.zeros_like(acc_sc)
            e_sc[...] = jnp.zeros_like(e_sc)
            g_sc[...] = jnp.zeros_like(g_sc)

        def chunk(j, carry):
            cc = jnp.where(d == 0, cb - 1 - j, j)
            sl = pl.ds(pl.multiple_of(cc * c, c), c)
            heads = [slice(h * R_HD, (h + 1) * R_HD) for h in range(R_H)]
            first = []
            for h, hs in enumerate(heads):
                q, k, v, dy = q_ref[sl, hs], k_ref[sl, hs], v_ref[sl, hs], dy_ref[sl, hs]
                r = r_sc[h]
                first.append((_mm_nt(q, k), _mm_nt(dy, v), _mm_nt(dy, pst_ref[h, cc]), _mm_nt(v, r),
                              _mm(k * kd_ref[h], r), _mm_tn(q * qd_ref[h], dy)))
            for h, hs in enumerate(heads):
                qk, ds, dyp, vr, kr, qdy = first[h]
                q, k, dy = q_ref[sl, hs], k_ref[sl, hs], dy_ref[sl, hs]
                r = r_sc[h]
                da = ds * m_ref[h]
                dv_ref[0, sl, hs] = _mm_tn(qk * m_ref[h], dy) + kr
                dq_ref[0, sl, hs] = _mm(da, k) + dyp * qd_ref[h]
                dk_ref[0, sl, hs] = _mm_tn(da, q) + vr * kd_ref[h]
                acc_sc[h] += dyp * q * qdw_ref[h] + vr * k * kdw_ref[h]
                e_sc[h] += ds * qk * mw_ref[h]
                g_sc[h] += r * pst_ref[h, cc]
                r_sc[h] = r * g_ref[h] + qdy
            return carry

        lax.fori_loop(0, cb, chunk, 0)

        @pl.when(n == ns - 1)
        def _():
            for h in range(R_H):
                tot = jnp.sum(jnp.sum(acc_sc[h] + e_sc[h] + g_sc[h] * gw_ref[h], axis=0, keepdims=True),
                              axis=1, keepdims=True)
                dlam_ref[h] = jnp.broadcast_to(tot, (1, LANES))

    def step(d, n):
        return d * n + (1 - d) * (ns - 1 - n)

    blk = lambda off: pl.BlockSpec((cb * c, R_W), lambda d, n: (step(d, n), off))
    big = pl.BlockSpec((R_H, c, c), lambda d, n: (d, 0, 0))
    vec = pl.BlockSpec((R_H, 1, LANES), lambda d, n: (d, 0, 0))
    out = pl.BlockSpec((1, cb * c, R_W), lambda d, n: (d, step(d, n), 0))
    oshape = jax.ShapeDtypeStruct((2, seq, R_W), F32)
    sq = pltpu.VMEM((R_H, R_HD, R_HD), F32)
    return pl.pallas_call(
        body, name="ret_bwd", grid=(2, ns),
        in_specs=[blk(0), blk(0), blk(2), blk(0),
                  pl.BlockSpec((R_H, cb, R_HD, R_HD), lambda d, n: (d, step(d, n), 0, 0)),
                  big, big, big, big, big, big, vec, vec],
        out_specs=(out, out, out, vec),
        out_shape=(oshape, oshape, oshape, jax.ShapeDtypeStruct((2 * R_H, 1, LANES), F32)),
        scratch_shapes=[sq, sq, sq, sq],
    )(rq, rk, pr, dry, pst, m, mw, qd, qdw, kd, kdw, gdec, gw)


def _group_norm(ry):
    yn, rs = [], []
    for h in range(R_H):
        s = ry[:, h * R_HD:(h + 1) * R_HD]
        mu = jnp.mean(s, axis=-1, keepdims=True)
        cen = s - mu
        r = lax.rsqrt(jnp.mean(cen * cen, axis=-1, keepdims=True) + GN_EPS)
        yn.append(cen * r)
        rs.append(r)
    return yn, rs


def _merge_fwd(x, o, y2, pr, pg, gain_r, wao, wro, wout):
    seq = x.shape[0]
    tm = _tile(seq, 256)

    def body(x_ref, o_ref, yf_ref, yb_ref, rg_ref, ga_ref, gr_ref, gn_ref, wao_ref, wro_ref, wout_ref,
             x1_ref, mg_ref, ri_ref):
        yn, _ = _group_norm(yf_ref[0] + yb_ref[0])
        rg = rg_ref[...]
        ret_in = jnp.concatenate(yn, axis=1) * gn_ref[...] * (rg * _sigmoid(rg))
        ri_ref[...] = _bf(ret_in)
        attn_out = _mm(o_ref[...], wao_ref[...])
        ret_out = _mm(ret_in, wro_ref[...])
        merged = _sigmoid(ga_ref[...]) * attn_out + _sigmoid(gr_ref[...]) * ret_out
        mg_ref[...] = _bf(merged)
        x1_ref[...] = x_ref[...] + _mm(merged, wout_ref[...])

    row = lambda w_, j=0: pl.BlockSpec((tm, w_), lambda i: (i, j))
    ydir = lambda d: pl.BlockSpec((1, tm, R_W), lambda i: (d, i, 0))
    return pl.pallas_call(
        body, name="merge_fwd", grid=(seq // tm,),
        in_specs=[row(D), row(AQ_W), ydir(0), ydir(1), row(R_W, 3), row(D, 0), row(D, 1),
                  _full((1, R_W)), _full((AQ_W, D)), _full((R_W, D)), _full((D, D))],
        out_specs=(row(D), row(D), row(R_W)),
        out_shape=(jax.ShapeDtypeStruct((seq, D), F32), jax.ShapeDtypeStruct((seq, D), _MXU),
                   jax.ShapeDtypeStruct((seq, R_W), _MXU)),
    )(x, o, y2, y2, pr, pg, pg, gain_r, wao, wro, wout)


def _mlp_fwd(x1, gain, wup, wdown):
    seq = x1.shape[0]
    tm = _tile(seq, 512)
    fc = 1024
    nfc = FF // fc

    def body(x_ref, g_ref, wu_ref, wd_ref, x2_ref, hm_sc, acc_sc):
        c = pl.program_id(1)

        @pl.when(c == 0)
        def _():
            n, _ = _rms(x_ref[...])
            hm_sc[...] = _bf(n * g_ref[...])
            acc_sc[...] = jnp.zeros_like(acc_sc)

        halves = (slice(0, fc // 2), slice(fc // 2, fc))
        ups = [jnp.maximum(_mm(hm_sc[...], wu_ref[:, hs]), 0.0) for hs in halves]
        acc_sc[...] += _mm(ups[0] * ups[0], wd_ref[halves[0], :]) + _mm(ups[1] * ups[1], wd_ref[halves[1], :])

        @pl.when(c == nfc - 1)
        def _():
            x2_ref[...] = x_ref[...] + acc_sc[...]

    return pl.pallas_call(
        body, name="mlp_fwd", grid=(seq // tm, nfc),
        in_specs=[pl.BlockSpec((tm, D), lambda i, c: (i, 0)), pl.BlockSpec((1, D), lambda i, c: (0, 0)),
                  pl.BlockSpec((D, fc), lambda i, c: (0, c)), pl.BlockSpec((fc, D), lambda i, c: (c, 0))],
        out_specs=pl.BlockSpec((tm, D), lambda i, c: (i, 0)),
        out_shape=jax.ShapeDtypeStruct((seq, D), F32),
        scratch_shapes=[pltpu.VMEM((tm, D), _MXU), pltpu.VMEM((tm, D), F32)],
    )(x1, gain, wup, wdown)


def _ple_loss(x2, p, tgt, g_ple, g_fin, wpg, wpgt, wple):
    seq = x2.shape[0]
    tm = _tile(seq, 256)

    def body(x2_ref, p_ref, t_ref, gp_ref, gf_ref, wpg_ref, wpgt_ref, wple_ref,
             dx2_ref, de_ref, dz_ref, hp_ref, loss_ref, dgf_ref, dgp_ref):
        @pl.when(pl.program_id(0) == 0)
        def _():
            loss_ref[...] = jnp.zeros_like(loss_ref)
            dgf_ref[...] = jnp.zeros_like(dgf_ref)
            dgp_ref[...] = jnp.zeros_like(dgp_ref)

        x2 = x2_ref[...]
        gp, gf = gp_ref[...], gf_ref[...]
        n2, r2 = _rms(x2)
        hp = _bf(n2 * gp)
        hp_ref[...] = hp
        gate = _sigmoid(_mm(hp, wpg_ref[...]))
        e = _mm(p_ref[...], wple_ref[...])
        x3 = x2 + gate * e
        n3, r3 = _rms(x3)
        diff = n3 * gf - t_ref[...]
        row_loss = jnp.mean(diff * diff, axis=-1, keepdims=True)
        loss_ref[...] += 0.5 * jnp.sum(row_loss, axis=0, keepdims=True)
        dy = diff * (1.0 / D)
        dgf_ref[...] += jnp.sum(dy * n3, axis=0, keepdims=True)
        dx3 = _rms_bwd(n3, r3, gf, dy)
        de_ref[...] = _bf(dx3 * gate)
        dz = dx3 * e * gate * (1.0 - gate)
        dz_ref[...] = _bf(dz)
        dhp = _mm(dz, wpgt_ref[...])
        dgp_ref[...] += jnp.sum(dhp * n2, axis=0, keepdims=True)
        dx2_ref[...] = dx3 + _rms_bwd(n2, r2, gp, dhp)

    row = lambda w_: pl.BlockSpec((tm, w_), lambda i: (i, 0))
    act = lambda dt: jax.ShapeDtypeStruct((seq, D), dt)
    return pl.pallas_call(
        body, name="ple_loss", grid=(seq // tm,),
        in_specs=[row(D), row(PLE), row(D), _full((1, D)), _full((1, D)),
                  _full((D, D)), _full((D, D)), _full((PLE, D))],
        out_specs=(row(D), row(D), row(D), row(D), _full((1, LANES)), _full((1, D)), _full((1, D))),
        out_shape=(act(F32), act(_MXU), act(_MXU), act(_MXU), jax.ShapeDtypeStruct((1, LANES), F32),
                   jax.ShapeDtypeStruct((1, D), F32), jax.ShapeDtypeStruct((1, D), F32)),
    )(x2, p, tgt, g_ple, g_fin, wpg, wpgt, wple)


def _mlp_bwd(x1, dx2, gain, wup, wdownt, wupt):
    seq = x1.shape[0]
    tm = _tile(seq, 512)
    fc = 1024
    nfc = FF // fc

    def body(x_ref, dx2_ref, g_ref, wu_ref, wdt_ref, wut_ref,
             dx1_ref, a_ref, du_ref, hm_ref, dg_ref, dhm_sc):
        i = pl.program_id(0)
        c = pl.program_id(1)

        @pl.when((i == 0) & (c == 0))
        def _():
            dg_ref[...] = jnp.zeros_like(dg_ref)

        @pl.when(c == 0)
        def _():
            n, _ = _rms(x_ref[...])
            hm_ref[...] = _bf(n * g_ref[...])
            dhm_sc[...] = jnp.zeros_like(dhm_sc)

        halves = (slice(0, fc // 2), slice(fc // 2, fc))
        ups = [jnp.maximum(_mm(hm_ref[...], wu_ref[:, hs]), 0.0) for hs in halves]
        das = [_mm(dx2_ref[...], wdt_ref[:, hs]) for hs in halves]
        part = None
        for u, da, hs in zip(ups, das, halves):
            a_ref[:, hs] = _bf(u * u)
            du = _bf(da * (2.0 * u))
            du_ref[:, hs] = du
            t = _mm(du, wut_ref[hs, :])
            part = t if part is None else part + t
        dhm_sc[...] += part

        @pl.when(c == nfc - 1)
        def _():
            n, r = _rms(x_ref[...])
            dhm = dhm_sc[...]
            dg_ref[...] += jnp.sum(dhm * n, axis=0, keepdims=True)
            dx1_ref[...] = dx2_ref[...] + _rms_bwd(n, r, g_ref[...], dhm)

    rowd = pl.BlockSpec((tm, D), lambda i, c: (i, 0))
    rowf = pl.BlockSpec((tm, fc), lambda i, c: (i, c))
    return pl.pallas_call(
        body, name="mlp_bwd", grid=(seq // tm, nfc),
        in_specs=[rowd, rowd, pl.BlockSpec((1, D), lambda i, c: (0, 0)),
                  pl.BlockSpec((D, fc), lambda i, c: (0, c)), pl.BlockSpec((D, fc), lambda i, c: (0, c)),
                  pl.BlockSpec((fc, D), lambda i, c: (c, 0))],
        out_specs=(rowd, rowf, rowf, rowd, pl.BlockSpec((1, D), lambda i, c: (0, 0))),
        out_shape=(jax.ShapeDtypeStruct((seq, D), F32), jax.ShapeDtypeStruct((seq, FF), _MXU),
                   jax.ShapeDtypeStruct((seq, FF), _MXU), jax.ShapeDtypeStruct((seq, D), _MXU),
                   jax.ShapeDtypeStruct((1, D), F32)),
        scratch_shapes=[pltpu.VMEM((tm, D), F32)],
    )(x1, dx2, gain, wup, wdownt, wupt)


def _merge_bwd(dx1, o, y2, pr, pg, gain_r, wao, wro, woutt, waot, wrot):
    seq = dx1.shape[0]
    tm = _tile(seq, 256)

    def body(dx1_ref, o_ref, yf_ref, yb_ref, rg_ref, ga_ref, gr_ref, gn_ref, wao_ref, wro_ref,
             woutt_ref, waot_ref, wrot_ref,
             dpg_ref, dao_ref, dro_ref, do_ref, dry_ref, drg_ref, dgn_ref):
        @pl.when(pl.program_id(0) == 0)
        def _():
            dgn_ref[...] = jnp.zeros_like(dgn_ref)

        yn_l, rs_l = _group_norm(yf_ref[0] + yb_ref[0])
        yn = jnp.concatenate(yn_l, axis=1)
        rg = rg_ref[...]
        gn = gn_ref[...]
        sg = _sigmoid(rg)
        sil = rg * sg
        ret_in = yn * gn * sil
        attn_out = _mm(o_ref[...], wao_ref[...])
        ret_out = _mm(ret_in, wro_ref[...])
        sa = _sigmoid(ga_ref[...])
        sr = _sigmoid(gr_ref[...])
        dm = _mm(dx1_ref[...], woutt_ref[...])
        dpg_ref[:, 0:D] = _bf(dm * attn_out * sa * (1.0 - sa))
        dpg_ref[:, D:2 * D] = _bf(dm * ret_out * sr * (1.0 - sr))
        dao = _bf(dm * sa)
        dro = _bf(dm * sr)
        dao_ref[...] = dao
        dro_ref[...] = dro
        do_ref[...] = _mm(dao, waot_ref[...])
        dri = _mm(dro, wrot_ref[...])
        dgn_ref[...] += jnp.sum(dri * yn * sil, axis=0, keepdims=True)
        drg_ref[...] = _bf(dri * yn * gn * (sg * (1.0 + rg * (1.0 - sg))))
        dyn = dri * gn * sil
        dry = []
        for h in range(R_H):
            dh = dyn[:, h * R_HD:(h + 1) * R_HD]
            dry.append(rs_l[h] * (dh - jnp.mean(dh, axis=-1, keepdims=True)
                                  - yn_l[h] * jnp.mean(dh * yn_l[h], axis=-1, keepdims=True)))
        dry_ref[...] = jnp.concatenate(dry, axis=1)

    row = lambda w_, j=0: pl.BlockSpec((tm, w_), lambda i: (i, j))
    ydir = lambda d: pl.BlockSpec((1, tm, R_W), lambda i: (d, i, 0))
    return pl.pallas_call(
        body, name="merge_bwd", grid=(seq // tm,),
        in_specs=[row(D), row(AQ_W), ydir(0), ydir(1), row(R_W, 3), row(D, 0), row(D, 1),
                  _full((1, R_W)), _full((AQ_W, D)), _full((R_W, D)), _full((D, D)),
                  _full((D, AQ_W)), _full((D, R_W))],
        out_specs=(row(PG_W), row(D), row(D), row(AQ_W), row(R_W), row(R_W), _full((1, R_W))),
        out_shape=(jax.ShapeDtypeStruct((seq, PG_W), _MXU), jax.ShapeDtypeStruct((seq, D), _MXU),
                   jax.ShapeDtypeStruct((seq, D), _MXU), jax.ShapeDtypeStruct((seq, AQ_W), F32),
                   jax.ShapeDtypeStruct((seq, R_W), F32), jax.ShapeDtypeStruct((seq, R_W), _MXU),
                   jax.ShapeDtypeStruct((1, R_W), F32)),
    )(dx1, o, y2, y2, pr, pg, pg, gain_r, wao, wro, woutt, waot, wrot)


def _qk_prep_bwd(pa, dqh, dk2, dv2, rdq, rdk, rdv, drg, gq, gk, seg, ca, sa, cr, sr):
    seq = pa.shape[0]
    tm = _tile(seq, 256)

    def body(pa_ref, dqh_ref, dk2_ref, dv2_ref, rdqf_ref, rdqb_ref, rdkf_ref, rdkb_ref, rdvf_ref, rdvb_ref,
             drg_ref, gq_ref, gk_ref, seg_ref, ca_ref, sa_ref, cr_ref, sr_ref,
             dpa_ref, dpr_ref, dgq_ref, dgk_ref):
        @pl.when(pl.program_id(0) == 0)
        def _():
            dgq_ref[...] = jnp.zeros_like(dgq_ref)
            dgk_ref[...] = jnp.zeros_like(dgk_ref)

        ca_, sa_ = ca_ref[...], sa_ref[...]

        def norm_bwd(raw, gain, dy, segm, dg_ref):
            msq = jnp.dot(raw * raw, segm, precision=HIGHEST, preferred_element_type=F32)
            r = lax.rsqrt(msq + EPS)
            n = raw * r
            dg_ref[...] += jnp.sum(dy * n, axis=0, keepdims=True)
            dn = dy * gain
            return r * (dn - n * jnp.dot(dn * n, segm, precision=HIGHEST, preferred_element_type=F32))

        dqn = _rope(dqh_ref[...] * (A_HD ** -0.5), _cat(ca_, 4), -_cat(sa_, 4), A_HD // 2)
        dpa_ref[:, 0:AQ_W] = _bf(norm_bwd(pa_ref[:, 0:AQ_W], gq_ref[...], dqn, seg_ref[...], dgq_ref))
        dkn = _rope(dk2_ref[...], ca_, -sa_, A_HD // 2)
        dpa_ref[:, AQ_W:AQ_W + AKV_W] = _bf(norm_bwd(pa_ref[:, AQ_W:AQ_W + AKV_W], gk_ref[...], dkn,
                                                     seg_ref[0:AKV_W, 0:AKV_W], dgk_ref))
        dpa_ref[:, AQ_W + AKV_W:PA_W] = _bf(dv2_ref[...])
        cr_, sr_ = _cat(cr_ref[...], 4), -_cat(sr_ref[...], 4)
        dpr_ref[:, 0:R_W] = _bf(_rope((rdqf_ref[0] + rdqb_ref[0]) * (R_HD ** -0.5), cr_, sr_, R_HD // 2))
        dpr_ref[:, R_W:2 * R_W] = _bf(_rope(rdkf_ref[0] + rdkb_ref[0], cr_, sr_, R_HD // 2))
        dpr_ref[:, 2 * R_W:3 * R_W] = _bf(rdvf_ref[0] + rdvb_ref[0])
        dpr_ref[:, 3 * R_W:4 * R_W] = drg_ref[...]

    row = lambda w_: pl.BlockSpec((tm, w_), lambda i: (i, 0))
    ydir = lambda d: pl.BlockSpec((1, tm, R_W), lambda i: (d, i, 0))
    return pl.pallas_call(
        body, name="qk_prep_bwd", grid=(seq // tm,),
        in_specs=[row(PA_W), row(AQ_W), row(AKV_W), row(AKV_W), ydir(0), ydir(1), ydir(0), ydir(1),
                  ydir(0), ydir(1), row(R_W), _full((1, AQ_W)), _full((1, AKV_W)), _full((AQ_W, AQ_W)),
                  row(LANES), row(LANES), row(LANES), row(LANES)],
        out_specs=(row(PA_W), row(PR_W), _full((1, AQ_W)), _full((1, AKV_W))),
        out_shape=(jax.ShapeDtypeStruct((seq, PA_W), _MXU), jax.ShapeDtypeStruct((seq, PR_W), _MXU),
                   jax.ShapeDtypeStruct((1, AQ_W), F32), jax.ShapeDtypeStruct((1, AKV_W), F32)),
    )(pa, dqh, dk2, dv2, rdq, rdq, rdk, rdk, rdv, rdv, drg, gq, gk, seg, ca, sa, cr, sr)


def _in_proj_bwd(x, dx1, gain, dpa, dpr, dpg, wint):
    seq = x.shape[0]
    tm = _tile(seq, 256)

    def body(x_ref, dx1_ref, g_ref, dpa_ref, dpr_ref, dpg_ref, wt_ref, dx_ref, dg_ref):
        @pl.when(pl.program_id(0) == 0)
        def _():
            dg_ref[...] = jnp.zeros_like(dg_ref)

        dh = (_mm(dpa_ref[...], wt_ref[0:PA_W, :]) + _mm(dpr_ref[...], wt_ref[PA_W:PA_W + PR_W, :])
              + _mm(dpg_ref[...], wt_ref[PA_W + PR_W:IN_W, :]))
        n, r = _rms(x_ref[...])
        dg_ref[...] += jnp.sum(dh * n, axis=0, keepdims=True)
        dx_ref[...] = dx1_ref[...] + _rms_bwd(n, r, g_ref[...], dh)

    row = lambda w_: pl.BlockSpec((tm, w_), lambda i: (i, 0))
    return pl.pallas_call(
        body, name="in_proj_bwd", grid=(seq // tm,),
        in_specs=[row(D), row(D), _full((1, D)), row(PA_W), row(PR_W), row(PG_W), _full((IN_W, D))],
        out_specs=(row(D), _full((1, D))),
        out_shape=(jax.ShapeDtypeStruct((seq, D), F32), jax.ShapeDtypeStruct((1, D), F32)),
    )(x, dx1, gain, dpa, dpr, dpg, wint)


def _wgrad(a, b, name):
    seq, m = a.shape
    n = b.shape[1]
    tm, tn, ts = _tile(m, 1024), _tile(n, 1024), _tile(seq, 2048)
    ns = seq // ts

    def body(a_ref, b_ref, o_ref):
        @pl.when(pl.program_id(2) == 0)
        def _():
            o_ref[...] = jnp.zeros_like(o_ref)

        o_ref[...] += _mm_tn(a_ref[...], b_ref[...])

    return pl.pallas_call(
        body, name=name, grid=(m // tm, n // tn, ns),
        in_specs=[pl.BlockSpec((ts, tm), lambda i, j, s: (s, i)), pl.BlockSpec((ts, tn), lambda i, j, s: (s, j))],
        out_specs=pl.BlockSpec((tm, tn), lambda i, j, s: (i, j)),
        out_shape=jax.ShapeDtypeStruct((m, n), F32),
    )(a, b)


def _adamw_math(w, g, m, v):
    m = B1 * m + (1.0 - B1) * g
    v = B2 * v + (1.0 - B2) * (g * g)
    m_hat = m / (1.0 - B1 ** STEP)
    v_hat = v / (1.0 - B2 ** STEP)
    delta = -LR * (m_hat / (jnp.sqrt(v_hat) + ADAM_EPS) + WD * w)
    return delta, m, v


def _adamw_big(land, own, w, m, v, name):
    rws, cols = w.shape
    tr = next(t for t in range(min(rws, 288), 0, -8) if rws % t == 0)

    def body(l_ref, o_ref, w_ref, m_ref, v_ref, g_ref, d_ref, nm_ref, nv_ref):
        x, y, c = _mesh_pos()
        me = 4 * x + 2 * y + c
        g = o_ref[...]
        for j in range(N_DEV):
            g = g + jnp.where(me == j, 0.0, l_ref[j].astype(F32))
        g_ref[...] = g
        d_ref[...], nm_ref[...], nv_ref[...] = _adamw_math(w_ref[...], g, m_ref[...], v_ref[...])

    row = pl.BlockSpec((tr, cols), lambda i: (i, 0))
    shp = jax.ShapeDtypeStruct((rws, cols), F32)
    return pl.pallas_call(
        body, name=name, grid=(rws // tr,),
        in_specs=[pl.BlockSpec((N_DEV, tr, cols), lambda i: (0, i, 0)), row, row, row, row],
        out_specs=(row, row, row, row), out_shape=(shp, shp, shp, shp),
    )(land, own, w, m, v)


def _adamw_small(sland, w, m, v):
    def body(l_ref, w_ref, m_ref, v_ref, g_ref, d_ref, nm_ref, nv_ref, loss_ref):
        s = l_ref[0]
        for j in range(1, N_DEV):
            s = s + l_ref[j]
        w = w_ref[...]
        gq = s[8:9]
        for h in range(1, A_H):
            gq = gq + s[8 + h:9 + h]
        gk = s[16:17] + s[17:18]
        gdec = s[5:6] * _sigmoid(-w[5:6])
        g = jnp.concatenate([s[0:5], gdec, gq, gk], axis=0)
        g_ref[...] = g
        d_ref[...], nm_ref[...], nv_ref[...] = _adamw_math(w, g, m_ref[...], v_ref[...])
        loss_ref[...] = s[6:7, 0:LANES]

    shp = jax.ShapeDtypeStruct((8, PACK_COLS), F32)
    return pl.pallas_call(
        body, name="adamw_small",
        out_shape=(shp, shp, shp, shp, jax.ShapeDtypeStruct((1, LANES), F32)),
    )(sland, w, m, v)


_BIG = (("w_attn_o", AQ_W, D, 1), ("w_ret_o", R_W, D, 1), ("w_out", D, D, 0),
        ("w_up", D, FF, 1), ("w_down", FF, D, 0), ("w_ple_gate", D, D, 0), ("w_ple", PLE, D, 1))
IN_SHARD = IN_W // N_DEV
_SMALL = ("mix_norm", "mlp_norm", "ple_norm", "final_norm", "ret_norm_gain", "ret_decay_logit",
          "attn_q_norm", "attn_k_norm")


def _shard_shape(rows, cols, axis):
    return (rows // N_DEV, cols) if axis == 0 else (rows, cols // N_DEV)


def _pack_shards(shards):
    flat = jnp.concatenate([s.reshape(-1) for s in shards])
    return flat.reshape(-1, PACK_COLS)


def _unpack_gathered(gathered):
    flat = gathered.reshape(N_DEV, -1)
    out, off = {}, 0
    for name, rows, cols, axis in _BIG:
        sr, sc = _shard_shape(rows, cols, axis)
        blk = flat[:, off:off + sr * sc].reshape(N_DEV, sr, sc)
        off += sr * sc
        out[name] = blk.reshape(rows, cols) if axis == 0 else blk.transpose(1, 0, 2).reshape(rows, cols)
    return out


def _pack_full_grads(grads):
    parts = []
    for name, rows, cols, axis in _BIG:
        sr, sc = _shard_shape(rows, cols, axis)
        g = grads[name]
        blk = g.reshape(N_DEV, sr, sc) if axis == 0 else g.reshape(rows, N_DEV, sc).transpose(1, 0, 2)
        parts.append(blk.reshape(N_DEV, -1))
    flat = jnp.concatenate(parts, axis=1)
    return flat.reshape(N_DEV, -1, PACK_COLS)


def _unpack_shard(packed):
    flat = packed.reshape(-1)
    out, off = {}, 0
    for name, rows, cols, axis in _BIG:
        sr, sc = _shard_shape(rows, cols, axis)
        out[name] = flat[off:off + sr * sc].reshape(1, sr, sc)
        off += sr * sc
    return out


def _pack_small(vals):
    rows = [jnp.pad(vals[n].reshape(-1), (0, PACK_COLS - vals[n].size)) for n in _SMALL]
    return jnp.stack(rows)


def _unpack_small(packed, like):
    return {n: packed[i, :like[n].size].reshape(like[n].shape) for i, n in enumerate(_SMALL)}


def _row(v):
    return jnp.pad(v.reshape(-1), (0, PACK_COLS - v.size))


def kernel(x, p, mix_norm, w_in, attn_q_norm, attn_k_norm, ret_decay_logit, ret_norm_gain, w_attn_o, w_ret_o, w_out, mlp_norm, w_up, w_down, ple_norm, w_ple_gate, w_ple, final_norm, loss_target, m_mix_norm, m_w_in, m_attn_q_norm, m_attn_k_norm, m_ret_decay_logit, m_ret_norm_gain, m_w_attn_o, m_w_ret_o, m_w_out, m_mlp_norm, m_w_up, m_w_down, m_ple_norm, m_w_ple_gate, m_w_ple, m_final_norm, v_mix_norm, v_w_in, v_attn_q_norm, v_attn_k_norm, v_ret_decay_logit, v_ret_norm_gain, v_w_attn_o, v_w_ret_o, v_w_out, v_mlp_norm, v_w_up, v_w_down, v_ple_norm, v_w_ple_gate, v_w_ple, v_final_norm):
    args = dict(locals())
    seq = x.shape[1]
    xs = x[0]
    ps = p[0, 0]
    tgt = loss_target[0]

    big_names = [b[0] for b in _BIG]
    wshard = _pack_shards([args[n] for n in big_names])
    win_g, rest_g = _all_gather([w_in[0].astype(_MXU), wshard.astype(_MXU)])
    win = win_g.transpose(1, 0, 2).reshape(D, IN_W)
    wfull = _unpack_gathered(rest_g)
    wao, wro, wout = wfull["w_attn_o"], wfull["w_ret_o"], wfull["w_out"]
    wup, wdown, wpg, wple = wfull["w_up"], wfull["w_down"], wfull["w_ple_gate"], wfull["w_ple"]

    g_mix, g_mlp, g_ple = mix_norm, mlp_norm, ple_norm
    g_fin = final_norm.reshape(1, D)
    gq = jnp.tile(attn_q_norm, (1, A_H))
    gk = jnp.tile(attn_k_norm, (1, A_KV))
    seg = _seg_mean_matrix()
    ca, sa, cr, sr = _rope_tables(seq)

    pa, pr, pg, h = _in_proj(xs, g_mix, win)
    qh, kh, vh, rqh, rkh = _qk_prep(pa, pr, gq, gk, seg, ca, sa, cr, sr)

    tq = _tile(seq, 128)
    tk = _tile(seq // 4, 1024)
    qt8 = qh.reshape(seq, A_H, A_HD).transpose(1, 2, 0)
    vta = jnp.stack([jnp.concatenate([_chunk_t(vh[:, g * A_HD:(g + 1) * A_HD], tk),
                                      jnp.ones((seq // tk, 16, tk), _MXU)], axis=1) for g in range(A_KV)])
    ot, lse = _attn_fwd(qt8, kh, vta, tq, tk)
    o = _heads_to_rows(ot)

    zb = jnp.broadcast_to(ret_decay_logit.reshape(2 * R_H, 1, 1), (2 * R_H, 1, LANES))
    tm_, tmw, tqd, tqdw, tkd, tkdw, tg, tgw = _ret_tables(zb)
    cb = _tile(seq // CHUNK, 8)
    y2, pst = _ret_fwd(rqh, rkh, pr, tm_, tqd, tkd, tg, cb)

    x1, merged, ret_in = _merge_fwd(xs, o, y2, pr, pg, ret_norm_gain, wao, wro, wout)
    x2 = _mlp_fwd(x1, g_mlp, wup, wdown)

    dx2, de, dz, hp, loss_p, dg_fin, dg_ple = _ple_loss(x2, ps, tgt, g_ple, g_fin, wpg, wpg.T, wple)
    dx1, act, du, hm, dg_mlp = _mlp_bwd(x1, dx2, g_mlp, wup, wdown.T, wup.T)
    dpg, dao, dro, do, dry, drg, dg_gn = _merge_bwd(dx1, o, y2, pr, pg, ret_norm_gain, wao, wro,
                                                    wout.T, wao.T, wro.T)
    rdq, rdk, rdv, dlam = _ret_bwd(rqh, rkh, pr, dry, pst, tm_, tmw, tqd, tqdw, tkd, tkdw, tg, tgw, cb)

    ksplit = 2
    tkb = _tile(seq // 4, 512)
    dot_ = do.reshape(seq, A_H, A_HD).transpose(1, 2, 0)
    dkt, dvt, dqt = _attn_bwd(qt8, ot, dot_, lse, kh, vh, _chunk_t(kh, tkb), _tile(seq, 256), tkb, ksplit)
    dqh = _heads_to_rows(jnp.sum(dqt, axis=0))
    dpa, dpr, dg_q, dg_k = _qk_prep_bwd(pa, dqh, _chunks_to_rows(dkt), _chunks_to_rows(dvt), rdq, rdk, rdv, drg, gq, gk, seg, ca, sa, cr, sr)
    grad_x, dg_mix = _in_proj_bwd(xs, dx1, g_mix, dpa, dpr, dpg, win.T)

    wg = {
        "w_in": jnp.concatenate([_wgrad(h, dpa, "wgrad_in_a"), _wgrad(h, dpr, "wgrad_in_r"),
                                 _wgrad(h, dpg, "wgrad_in_g")], axis=1),
        "w_attn_o": _wgrad(o, dao, "wgrad_attn_o"),
        "w_ret_o": _wgrad(ret_in, dro, "wgrad_ret_o"),
        "w_out": _wgrad(merged, dx1, "wgrad_out"),
        "w_up": _wgrad(hm, du, "wgrad_up"),
        "w_down": _wgrad(act, dx2, "wgrad_down"),
        "w_ple_gate": _wgrad(hp, dz, "wgrad_ple_gate"),
        "w_ple": _wgrad(ps, de, "wgrad_ple"),
    }
    gpack = _pack_full_grads(wg)
    gpack_in = wg["w_in"].reshape(D, N_DEV, IN_SHARD).transpose(1, 0, 2)
    small = jnp.stack(
        [_row(dg_mix), _row(dg_mlp), _row(dg_ple), _row(dg_fin), _row(dg_gn), _row(dlam[:, 0, 0]),
         _row(loss_p[0, 0:1]), jnp.zeros((PACK_COLS,), F32)]
        + [_row(dg_q[0, hh * A_HD:(hh + 1) * A_HD]) for hh in range(A_H)]
        + [_row(dg_k[0, hh * A_HD:(hh + 1) * A_HD]) for hh in range(A_KV)]
        + [jnp.zeros((PACK_COLS,), F32)] * (SMALL_ROWS - 18))

    me = 4 * lax.axis_index("x") + 2 * lax.axis_index("y") + lax.axis_index("c")
    own_in = lax.dynamic_index_in_dim(gpack_in, me, axis=0, keepdims=False)
    own = lax.dynamic_index_in_dim(gpack, me, axis=0, keepdims=False)
    land_in, land, sland = _exchange_grads([_bf(gpack_in), _bf(gpack), small[None]])
    in_sh = _adamw_big(land_in, own_in, w_in[0], m_w_in[0], v_w_in[0], "adamw_w_in")
    g_sh, d_sh, m_sh, v_sh = _adamw_big(land, own, wshard, _pack_shards([args["m_" + n] for n in big_names]),
                                        _pack_shards([args["v_" + n] for n in big_names]), "adamw_shard")
    g_sm, d_sm, m_sm, v_sm, loss_row = _adamw_small(
        sland, _pack_small({n: args[n] for n in _SMALL}), _pack_small({n: args["m_" + n] for n in _SMALL}),
        _pack_small({n: args["v_" + n] for n in _SMALL}))

    names = ["mix_norm", "w_in", "attn_q_norm", "attn_k_norm", "ret_decay_logit", "ret_norm_gain", "w_attn_o",
             "w_ret_o", "w_out", "mlp_norm", "w_up", "w_down", "ple_norm", "w_ple_gate", "w_ple", "final_norm"]
    like = {n: args[n] for n in _SMALL}
    outs = [loss_row[0, 0], grad_x[None]]
    for big, sm, w_in_part in ((g_sh, g_sm, in_sh[0]), (d_sh, d_sm, in_sh[1]), (m_sh, m_sm, in_sh[2]),
                               (v_sh, v_sm, in_sh[3])):
        table = {**_unpack_shard(big), **_unpack_small(sm, like), "w_in": w_in_part[None]}
        outs += [table[n] for n in names]
    return tuple(outs)
```

```python
import functools

import jax
import jax.numpy as jnp
from jax import lax
from jax.experimental import pallas as pl
from jax.experimental.pallas import tpu as pltpu

F32 = jnp.float32
_MXU = jnp.bfloat16

D = 1024
PLE = 256
GRID_W = 64
A_HD = 64
A_H = 8
A_KV = 2
A_G = A_H // A_KV
AQ_W = A_H * A_HD
AKV_W = A_KV * A_HD
R_HD = 128
R_H = 4
R_W = R_H * R_HD
IN_W = AQ_W + 2 * AKV_W + 4 * R_W + 2 * D
PA_W = AQ_W + 2 * AKV_W
PR_W = 4 * R_W
PG_W = 2 * D
FF = 4 * D
CHUNK = 128
ROPE_THETA = 10000.0
EPS = 1e-6
GN_EPS = 1e-5
N_DEV = 8

LR, B1, B2, ADAM_EPS, WD, STEP = 0.001, 0.9, 0.999, 1e-08, 0.01, 10

LANES = 128
PACK_COLS = 1024
SMALL_ROWS = 24
HIGHEST = lax.Precision.HIGHEST


def _tile(n, pref):
    t = min(n, pref)
    assert n % t == 0, (n, t)
    return t


def _bf(a):
    return a.astype(_MXU)


def _mm(a, b):
    return jnp.dot(_bf(a), _bf(b), preferred_element_type=F32)


def _mm_nt(a, b):
    return lax.dot_general(_bf(a), _bf(b), (((1,), (1,)), ((), ())), preferred_element_type=F32)


def _mm_tn(a, b):
    return lax.dot_general(_bf(a), _bf(b), (((0,), (0,)), ((), ())), preferred_element_type=F32)


def _sigmoid(z):
    return 1.0 / (1.0 + jnp.exp(-z))


def _rms(x):
    r = lax.rsqrt(jnp.mean(x * x, axis=-1, keepdims=True) + EPS)
    return x * r, r


def _rms_bwd(n, r, gain, dy):
    dn = dy * gain
    return r * (dn - n * jnp.mean(dn * n, axis=-1, keepdims=True))


def _swap_halves(x, half):
    n = x.shape[-1]
    lane = lax.broadcasted_iota(jnp.int32, x.shape, x.ndim - 1)
    first = (lane % (2 * half)) < half
    return jnp.where(first, pltpu.roll(x, n - half, axis=1), pltpu.roll(x, half, axis=1))


def _rope(x, cos, sin, half):
    return x * cos + _swap_halves(x, half) * sin


def _cat(t, reps):
    return jnp.concatenate([t] * reps, axis=1)


def _full(shape):
    nd = len(shape)
    return pl.BlockSpec(shape, lambda *_: (0,) * nd)


def _rope_tables(seq):
    def tab(head_dim):
        n_axis = head_dim // 4
        freqs = ROPE_THETA ** (-jnp.arange(n_axis, dtype=F32) / n_axis)
        rows = seq // GRID_W
        row = jnp.repeat(jnp.arange(rows, dtype=F32), GRID_W)
        col = jnp.tile(jnp.arange(GRID_W, dtype=F32), rows)
        ang = jnp.concatenate([row[:, None] * freqs, col[:, None] * freqs], axis=-1)
        c, s = jnp.cos(ang), jnp.sin(ang)
        return jnp.concatenate([c, c], axis=-1), jnp.concatenate([-s, s], axis=-1)
    ca, sa = tab(A_HD)
    cr, sr = tab(R_HD)
    return jnp.tile(ca, (1, 2)), jnp.tile(sa, (1, 2)), cr, sr


def _seg_mean_matrix():
    i = jnp.arange(AQ_W) // A_HD
    return (i[:, None] == i[None, :]).astype(F32) / A_HD


def _mesh_pos():
    return lax.axis_index("x"), lax.axis_index("y"), lax.axis_index("c")


def _all_gather(shards):
    n = len(shards)

    def body(*refs):
        x_refs, out_refs = refs[:n], refs[n:2 * n]
        send_sems, recv_sems, local_sems = refs[2 * n:]
        x, y, c = _mesh_pos()
        me, sibling = (x, y, c), (x, y, 1 - c)
        chips = [(1 - x, y), (x, 1 - y), (1 - x, 1 - y)]

        def slot(t, px, py, pc):
            return out_refs[t].at[4 * px + 2 * py + pc]

        def copy(t, k, block, to, src=None):
            return pltpu.make_async_remote_copy(
                src_ref=slot(t, *block) if src is None else src, dst_ref=slot(t, *block),
                send_sem=send_sems.at[7 * t + k], recv_sem=recv_sems.at[7 * t + k],
                device_id=to, device_id_type=pl.DeviceIdType.MESH)

        mine = [pltpu.make_async_copy(x_refs[t], slot(t, *me), local_sems.at[t]) for t in range(n)]
        for cp in mine:
            cp.start()
        first = []
        for t in range(n):
            first.append(copy(t, 0, me, sibling, src=x_refs[t]))
            first += [copy(t, 1 + j, me, (*chip, c), src=x_refs[t]) for j, chip in enumerate(chips)]
        for cp in first:
            cp.start()
        passed = []
        for t in range(n):
            for j, chip in enumerate(chips):
                copy(t, 1 + j, (*chip, c), me).wait_recv()
                passed.append(copy(t, 4 + j, (*chip, c), sibling))
                passed[-1].start()
        for t in range(n):
            copy(t, 0, sibling, me).wait_recv()
            for j, chip in enumerate(chips):
                copy(t, 4 + j, (*chip, 1 - c), me).wait_recv()
        for cp in first + passed:
            cp.wait_send()
        for cp in mine:
            cp.wait()

    anyspec = pl.BlockSpec(memory_space=pl.ANY)
    return pl.pallas_call(
        body, name="all_gather_weights",
        out_shape=tuple(jax.ShapeDtypeStruct((N_DEV,) + s.shape, s.dtype) for s in shards),
        in_specs=[anyspec] * n, out_specs=(anyspec,) * n,
        scratch_shapes=[pltpu.SemaphoreType.DMA((7 * n,)), pltpu.SemaphoreType.DMA((7 * n,)),
                        pltpu.SemaphoreType.DMA((n,))],
    )(*shards)


def _exchange_grads(packs):
    n = len(packs)

    def body(*refs):
        g_refs, land_refs = refs[:n], refs[n:2 * n]
        send_sems, recv_sems, local_sems = refs[2 * n:]
        x, y, c = _mesh_pos()
        me = 4 * x + 2 * y + c

        def row(t, j):
            return g_refs[t].at[j if packs[t].shape[0] == N_DEV else 0]

        own = [pltpu.make_async_copy(row(t, me), land_refs[t].at[me], local_sems.at[t]) for t in range(n)]
        for cp in own:
            cp.start()
        sends = []
        for k in range(1, N_DEV):
            peer = (x ^ ((k >> 2) & 1), y ^ ((k >> 1) & 1), c ^ (k & 1))
            pidx = 4 * peer[0] + 2 * peer[1] + peer[2]
            for t in range(n):
                sends.append(pltpu.make_async_remote_copy(
                    src_ref=row(t, pidx), dst_ref=land_refs[t].at[me],
                    send_sem=send_sems.at[7 * t + k - 1], recv_sem=recv_sems.at[7 * t + k - 1],
                    device_id=peer, device_id_type=pl.DeviceIdType.MESH))
                sends[-1].start()
        for k in range(1, N_DEV):
            peer = (x ^ ((k >> 2) & 1), y ^ ((k >> 1) & 1), c ^ (k & 1))
            pidx = 4 * peer[0] + 2 * peer[1] + peer[2]
            for t in range(n):
                pltpu.make_async_remote_copy(
                    src_ref=row(t, me), dst_ref=land_refs[t].at[pidx],
                    send_sem=send_sems.at[7 * t + k - 1], recv_sem=recv_sems.at[7 * t + k - 1],
                    device_id=peer, device_id_type=pl.DeviceIdType.MESH).wait_recv()
        for cp in sends:
            cp.wait_send()
        for cp in own:
            cp.wait()

    anyspec = pl.BlockSpec(memory_space=pl.ANY)
    return pl.pallas_call(
        body, name="exchange_grads",
        out_shape=tuple(jax.ShapeDtypeStruct((N_DEV,) + g.shape[1:], g.dtype) for g in packs),
        in_specs=[anyspec] * n, out_specs=(anyspec,) * n,
        scratch_shapes=[pltpu.SemaphoreType.DMA((7 * n,)), pltpu.SemaphoreType.DMA((7 * n,)),
                        pltpu.SemaphoreType.DMA((n,))],
    )(*packs)


def _in_proj(x, gain, w):
    seq = x.shape[0]
    tm = _tile(seq, 256)

    def body(x_ref, g_ref, w_ref, pa_ref, pr_ref, pg_ref, h_ref):
        n, _ = _rms(x_ref[...])
        h = _bf(n * g_ref[...])
        h_ref[...] = h
        pa_ref[...] = _mm(h, w_ref[:, 0:PA_W])
        pr_ref[...] = _mm(h, w_ref[:, PA_W:PA_W + PR_W])
        pg_ref[...] = _mm(h, w_ref[:, PA_W + PR_W:IN_W])

    row = lambda w_: pl.BlockSpec((tm, w_), lambda i: (i, 0))
    return pl.pallas_call(
        body, name="in_proj", grid=(seq // tm,),
        in_specs=[row(D), _full((1, D)), _full((D, IN_W))],
        out_specs=(row(PA_W), row(PR_W), row(PG_W), row(D)),
        out_shape=(jax.ShapeDtypeStruct((seq, PA_W), F32), jax.ShapeDtypeStruct((seq, PR_W), F32),
                   jax.ShapeDtypeStruct((seq, PG_W), F32), jax.ShapeDtypeStruct((seq, D), _MXU)),
    )(x, gain, w)


def _qk_prep(pa, pr, gq, gk, seg, ca, sa, cr, sr):
    seq = pa.shape[0]
    tm = _tile(seq, 256)

    def body(pa_ref, pr_ref, gq_ref, gk_ref, seg_ref, ca_ref, sa_ref, cr_ref, sr_ref,
             qh_ref, kh_ref, v_ref, rq_ref, rk_ref):
        q = pa_ref[:, 0:AQ_W]
        k = pa_ref[:, AQ_W:AQ_W + AKV_W]
        v_ref[...] = _bf(pa_ref[:, AQ_W + AKV_W:PA_W])
        ca_, sa_ = ca_ref[...], sa_ref[...]
        msq = jnp.dot(q * q, seg_ref[...], precision=HIGHEST, preferred_element_type=F32)
        qn = q * lax.rsqrt(msq + EPS) * gq_ref[...]
        qh_ref[...] = _bf(_rope(qn, _cat(ca_, 4), _cat(sa_, 4), A_HD // 2) * (A_HD ** -0.5))
        msk = jnp.dot(k * k, seg_ref[0:AKV_W, 0:AKV_W], precision=HIGHEST, preferred_element_type=F32)
        kn = k * lax.rsqrt(msk + EPS) * gk_ref[...]
        kh_ref[...] = _bf(_rope(kn, ca_, sa_, A_HD // 2))
        cr_, sr_ = _cat(cr_ref[...], 4), _cat(sr_ref[...], 4)
        rq_ref[...] = _rope(pr_ref[:, 0:R_W], cr_, sr_, R_HD // 2) * (R_HD ** -0.5)
        rk_ref[...] = _rope(pr_ref[:, R_W:2 * R_W], cr_, sr_, R_HD // 2)

    row = lambda w_: pl.BlockSpec((tm, w_), lambda i: (i, 0))
    return pl.pallas_call(
        body, name="qk_prep", grid=(seq // tm,),
        in_specs=[row(PA_W), row(2 * R_W), _full((1, AQ_W)), _full((1, AKV_W)), _full((AQ_W, AQ_W)),
                  row(LANES), row(LANES), row(LANES), row(LANES)],
        out_specs=(row(AQ_W), row(AKV_W), row(AKV_W), row(R_W), row(R_W)),
        out_shape=(jax.ShapeDtypeStruct((seq, AQ_W), _MXU), jax.ShapeDtypeStruct((seq, AKV_W), _MXU),
                   jax.ShapeDtypeStruct((seq, AKV_W), _MXU), jax.ShapeDtypeStruct((seq, R_W), F32),
                   jax.ShapeDtypeStruct((seq, R_W), F32)),
    )(pa, pr, gq, gk, seg, ca, sa, cr, sr)


def _chunk_t(a, tk):
    seq = a.shape[0]
    return a.reshape(seq // tk, tk, a.shape[1]).transpose(0, 2, 1)


def _heads_to_rows(t):
    return t.transpose(2, 0, 1).reshape(t.shape[2], AQ_W)


def _attn_fwd(qt8, k2, vta, tq, tk):
    seq = k2.shape[0]
    nck = seq // tk
    rows = A_G * tq
    vrows = vta.shape[2]
    rb = _tile(tk, 64)
    assert nck % 2 == 0, nck

    def body(qt_ref, k_ref, vt_ref, o_ref, lse_ref, m_sc, acc_sc, qtp_sc, s_a, s_b, p_a, p_b, al_a, al_b):
        g = pl.program_id(0)
        qtp_sc[...] = jnp.zeros_like(qtp_sc)
        qtp_sc[pl.ds(pl.multiple_of(g * A_HD, A_HD), A_HD), :] = jnp.concatenate(
            [qt_ref[a] for a in range(A_G)], axis=1)
        m_sc[...] = jnp.full((1, rows), -jnp.inf, F32)
        acc_sc[...] = jnp.zeros_like(acc_sc)

        def scores(c):
            kc = k_ref[pl.ds(pl.multiple_of(c * tk, tk), tk), :]
            return _mm(kc, qtp_sc[...])

        def stage(c, s_cur, s_nxt, p_cur, p_prv, al_cur, al_prv, first=False, last=False):
            if not last:
                s_nxt[...] = scores(c + 1)
            if not first:
                acc_sc[...] = al_prv[...] * acc_sc[...] + _mm(vt_ref[0, c - 1], p_prv[...])
            m_old = m_sc[...]
            mx = None
            for r in range(0, tk, rb):
                bm = jnp.max(s_cur[r:r + rb, :].reshape(rb // 8, 8, rows), axis=0)
                mx = bm if mx is None else jnp.maximum(mx, bm)
            m_new = jnp.maximum(m_old, jnp.max(mx, axis=0, keepdims=True))
            for r in range(0, tk, rb):
                p_cur[r:r + rb, :] = _bf(jnp.exp(s_cur[r:r + rb, :] - m_new))
            al_cur[...] = jnp.exp(m_old - m_new)
            m_sc[...] = m_new

        s_a[...] = scores(0)
        stage(0, s_a, s_b, p_a, p_b, al_a, al_b, first=True)

        def pair(j, carry):
            stage(2 * j + 1, s_b, s_a, p_b, p_a, al_b, al_a)
            stage(2 * j + 2, s_a, s_b, p_a, p_b, al_a, al_b)
            return carry

        lax.fori_loop(0, nck // 2 - 1, pair, 0)
        stage(nck - 1, s_b, s_a, p_b, p_a, al_b, al_a, last=True)
        acc = al_b[...] * acc_sc[...] + _mm(vt_ref[0, nck - 1], p_b[...])
        l = acc[A_HD:A_HD + 1, :]
        lse = m_sc[...] + jnp.log(l)
        out = acc[0:A_HD, :] * (1.0 / l)
        for a in range(A_G):
            o_ref[a] = out[:, a * tq:(a + 1) * tq]
            lse_ref[a] = lse[:, a * tq:(a + 1) * tq]

    return pl.pallas_call(
        body, name="attn_fwd", grid=(A_KV, seq // tq),
        in_specs=[pl.BlockSpec((A_G, A_HD, tq), lambda g, i: (g, 0, i)),
                  _full((seq, LANES)), pl.BlockSpec((1, nck, vrows, tk), lambda g, i: (g, 0, 0, 0))],
        out_specs=(pl.BlockSpec((A_G, A_HD, tq), lambda g, i: (g, 0, i)),
                   pl.BlockSpec((A_G, 1, tq), lambda g, i: (g, 0, i))),
        out_shape=(jax.ShapeDtypeStruct((A_H, A_HD, seq), F32), jax.ShapeDtypeStruct((A_H, 1, seq), F32)),
        scratch_shapes=[pltpu.VMEM((1, rows), F32), pltpu.VMEM((vrows, rows), F32), pltpu.VMEM((LANES, rows), _MXU),
                        pltpu.VMEM((tk, rows), F32), pltpu.VMEM((tk, rows), F32),
                        pltpu.VMEM((tk, rows), _MXU), pltpu.VMEM((tk, rows), _MXU),
                        pltpu.VMEM((1, rows), F32), pltpu.VMEM((1, rows), F32)],
    )(qt8, k2, vta)


def _attn_bwd(qt8, ot, dot_, lse, k2, v2, k2t, tq, tk, ksplit):
    seq = k2.shape[0]
    sh = seq // ksplit
    nck = sh // tk
    rows = A_G * tq
    rb = _tile(tk, 32768 // rows)
    assert nck % 2 == 0, nck

    def body(qt_ref, ot_ref, dot_ref, lse_ref, k_ref, v_ref, kt_ref,
             dk_ref, dv_ref, dq_ref, dq_sc, qtp_sc, dotp_sc, pt_sc, dst_sc,
             s_a, s_b, dp_a, dp_b, p_a, p_b, ds_a, ds_b):
        g = pl.program_id(1)
        hrows = pl.ds(pl.multiple_of(g * A_HD, A_HD), A_HD)

        @pl.when(pl.program_id(2) == 0)
        def _():
            dk_ref[...] = jnp.zeros_like(dk_ref)
            dv_ref[...] = jnp.zeros_like(dv_ref)

        lse_row = jnp.concatenate([lse_ref[a] for a in range(A_G)], axis=1)
        dd = jnp.concatenate([jnp.sum(ot_ref[a] * dot_ref[a], axis=0, keepdims=True)
                              for a in range(A_G)], axis=1)
        qtp_sc[...] = jnp.zeros_like(qtp_sc)
        dotp_sc[...] = jnp.zeros_like(dotp_sc)
        qtp_sc[hrows, :] = jnp.concatenate([qt_ref[a] for a in range(A_G)], axis=1)
        dotp_sc[hrows, :] = _bf(jnp.concatenate([dot_ref[a] for a in range(A_G)], axis=1))
        dq_sc[...] = jnp.zeros_like(dq_sc)

        def products(c, s_ref, dp_ref):
            sl = pl.ds(pl.multiple_of(c * tk, tk), tk)
            s_ref[...] = _mm(k_ref[sl, :], qtp_sc[...])
            dp_ref[...] = _mm(v_ref[sl, :], dotp_sc[...])

        def accumulate(c, p_ref, ds_ref):
            pt_sc[...] = p_ref[...].T
            dst_sc[...] = ds_ref[...].T
            dq_sc[...] += _mm(kt_ref[c, hrows, :], ds_ref[...])
            dv_ref[0, c] += _mm(dotp_sc[hrows, :], pt_sc[...])
            dk_ref[0, c] += _mm(qtp_sc[hrows, :], dst_sc[...])

        def stage(c, s_cur, dp_cur, s_nxt, dp_nxt, p_cur, ds_cur, p_prv, ds_prv, first=False, last=False):
            if not last:
                products(c + 1, s_nxt, dp_nxt)
            if not first:
                accumulate(c - 1, p_prv, ds_prv)
            for r in range(0, tk, rb):
                p = jnp.exp(s_cur[r:r + rb, :] - lse_row)
                p_cur[r:r + rb, :] = _bf(p)
                ds_cur[r:r + rb, :] = _bf(p * (dp_cur[r:r + rb, :] - dd))

        products(0, s_a, dp_a)
        stage(0, s_a, dp_a, s_b, dp_b, p_a, ds_a, p_b, ds_b, first=True)

        def pair(j, carry):
            stage(2 * j + 1, s_b, dp_b, s_a, dp_a, p_b, ds_b, p_a, ds_a)
            stage(2 * j + 2, s_a, dp_a, s_b, dp_b, p_a, ds_a, p_b, ds_b)
            return carry

        lax.fori_loop(0, nck // 2 - 1, pair, 0)
        stage(nck - 1, s_b, dp_b, s_a, dp_a, p_b, ds_b, p_a, ds_a, last=True)
        accumulate(nck - 1, p_b, ds_b)
        for a in range(A_G):
            dq_ref[0, a] = dq_sc[:, a * tq:(a + 1) * tq]

    tspec = pl.BlockSpec((A_G, A_HD, tq), lambda s, g, i: (g, 0, i))
    kspec = pl.BlockSpec((sh, LANES), lambda s, g, i: (s, 0))
    gspec = pl.BlockSpec((1, nck, A_HD, tk), lambda s, g, i: (g, s, 0, 0))
    gshape = jax.ShapeDtypeStruct((A_KV, seq // tk, A_HD, tk), F32)
    big = lambda dt: pltpu.VMEM((tk, rows), dt)
    bigt = pltpu.VMEM((rows, tk), _MXU)
    return pl.pallas_call(
        body, name="attn_bwd", grid=(ksplit, A_KV, seq // tq),
        in_specs=[tspec, tspec, tspec, pl.BlockSpec((A_G, 1, tq), lambda s, g, i: (g, 0, i)),
                  kspec, kspec, pl.BlockSpec((nck, LANES, tk), lambda s, g, i: (s, 0, 0))],
        out_specs=(gspec, gspec, pl.BlockSpec((1, A_G, A_HD, tq), lambda s, g, i: (s, g, 0, i))),
        out_shape=(gshape, gshape, jax.ShapeDtypeStruct((ksplit, A_H, A_HD, seq), F32)),
        scratch_shapes=[pltpu.VMEM((A_HD, rows), F32), pltpu.VMEM((LANES, rows), _MXU), pltpu.VMEM((LANES, rows), _MXU),
                        bigt, bigt,
                        big(F32), big(F32), big(F32), big(F32), big(_MXU), big(_MXU), big(_MXU), big(_MXU)],
    )(qt8, ot, dot_, lse, k2, v2, k2t)


def _chunks_to_rows(t):
    return t.transpose(1, 3, 0, 2).reshape(t.shape[1] * t.shape[3], AKV_W)


def _ret_tables(zb):
    c = CHUNK

    def body(z_ref, m_ref, mw_ref, qd_ref, qdw_ref, kd_ref, kdw_ref, g_ref, gw_ref):
        fwd = pl.program_id(0) < R_H
        z = z_ref[0]
        lam = jnp.minimum(z, 0.0) - jnp.log(1.0 + jnp.exp(-jnp.abs(z)))
        i = lax.broadcasted_iota(jnp.int32, (c, c), 0).astype(F32)
        j = lax.broadcasted_iota(jnp.int32, (c, c), 1).astype(F32)
        diff = jnp.where(fwd, i - j, j - i)
        keep = diff >= jnp.where(fwd, 0.0, 1.0)
        dist = jnp.maximum(diff, 0.0)
        m = jnp.where(keep, jnp.exp(lam * dist), 0.0)
        m_ref[0] = m
        mw_ref[0] = m * dist
        fq = jnp.where(fwd, i + 1.0, c - i)
        qd = jnp.exp(lam * fq)
        qd_ref[0] = qd
        qdw_ref[0] = qd * fq
        fk = jnp.where(fwd, c - 1.0 - i, i)
        kd = jnp.exp(lam * fk)
        kd_ref[0] = kd
        kdw_ref[0] = kd * fk
        gdec = jnp.exp(lam * c)
        g_ref[0] = gdec
        gw_ref[0] = gdec * c

    big = pl.BlockSpec((1, c, c), lambda t: (t, 0, 0))
    vec = pl.BlockSpec((1, 1, LANES), lambda t: (t, 0, 0))
    bshape = jax.ShapeDtypeStruct((2 * R_H, c, c), F32)
    vshape = jax.ShapeDtypeStruct((2 * R_H, 1, LANES), F32)
    return pl.pallas_call(
        body, name="ret_tables", grid=(2 * R_H,), in_specs=[vec],
        out_specs=(big, big, big, big, big, big, vec, vec),
        out_shape=(bshape,) * 6 + (vshape, vshape),
    )(zb)


def _ret_fwd(rq, rk, pr, m, qd, kd, gdec, cb):
    seq = rq.shape[0]
    c = CHUNK
    ns = seq // (cb * c)

    def body(q_ref, k_ref, v_ref, m_ref, qd_ref, kd_ref, g_ref, y_ref, pst_ref, p_sc):
        d = pl.program_id(0)

        @pl.when(pl.program_id(1) == 0)
        def _():
            p_sc[...] = jnp.zeros_like(p_sc)

        def chunk(j, carry):
            cc = jnp.where(d == 0, j, cb - 1 - j)
            sl = pl.ds(pl.multiple_of(cc * c, c), c)
            heads = [slice(h * R_HD, (h + 1) * R_HD) for h in range(R_H)]
            qk = [_mm_nt(q_ref[sl, hs], k_ref[sl, hs]) for hs in heads]
            qp = [_mm(q_ref[sl, hs] * qd_ref[h], p_sc[h]) for h, hs in enumerate(heads)]
            kv = [_mm_tn(k_ref[sl, hs] * kd_ref[h], v_ref[sl, hs]) for h, hs in enumerate(heads)]
            for h, hs in enumerate(heads):
                p = p_sc[h]
                pst_ref[h, cc] = p
                y_ref[0, sl, hs] = _mm(qk[h] * m_ref[h], v_ref[sl, hs]) + qp[h]
                p_sc[h] = p * g_ref[h] + kv[h]
            return carry

        lax.fori_loop(0, cb, chunk, 0)

    def step(d, n):
        return d * (ns - 1 - n) + (1 - d) * n

    blk = lambda off: pl.BlockSpec((cb * c, R_W), lambda d, n: (step(d, n), off))
    big = pl.BlockSpec((R_H, c, c), lambda d, n: (d, 0, 0))
    vec = pl.BlockSpec((R_H, 1, LANES), lambda d, n: (d, 0, 0))
    return pl.pallas_call(
        body, name="ret_fwd", grid=(2, ns),
        in_specs=[blk(0), blk(0), blk(2), big, big, big, vec],
        out_specs=(pl.BlockSpec((1, cb * c, R_W), lambda d, n: (d, step(d, n), 0)),
                   pl.BlockSpec((R_H, cb, R_HD, R_HD), lambda d, n: (d, step(d, n), 0, 0))),
        out_shape=(jax.ShapeDtypeStruct((2, seq, R_W), F32),
                   jax.ShapeDtypeStruct((2 * R_H, seq // c, R_HD, R_HD), F32)),
        scratch_shapes=[pltpu.VMEM((R_H, R_HD, R_HD), F32)],
    )(rq, rk, pr, m, qd, kd, gdec)


def _ret_bwd(rq, rk, pr, dry, pst, m, mw, qd, qdw, kd, kdw, gdec, gw, cb):
    seq = rq.shape[0]
    c = CHUNK
    ns = seq // (cb * c)

    def body(q_ref, k_ref, v_ref, dy_ref, pst_ref, m_ref, mw_ref, qd_ref, qdw_ref, kd_ref, kdw_ref,
             g_ref, gw_ref, dq_ref, dk_ref, dv_ref, dlam_ref, r_sc, acc_sc, e_sc, g_sc):
        d = pl.program_id(0)
        n = pl.program_id(1)

        @pl.when(n == 0)
        def _():
            r_sc[...] = jnp.zeros_like(r_sc)
            acc_sc[...] = jnp.zeros_like(acc_sc)
            e_sc[...] = jnp.zeros_like(e_sc)
            g_sc[...] = jnp.zeros_like(g_sc)

        def chunk(j, carry):
            cc = jnp.where(d == 0, cb - 1 - j, j)
            sl = pl.ds(pl.multiple_of(cc * c, c), c)
            heads = [slice(h * R_HD, (h + 1) * R_HD) for h in range(R_H)]
            first = []
            for h, hs in enumerate(heads):
                q, k, v, dy = q_ref[sl, hs], k_ref[sl, hs], v_ref[sl, hs], dy_ref[sl, hs]
                r = r_sc[h]
                first.append((_mm_nt(q, k), _mm_nt(dy, v), _mm_nt(dy, pst_ref[h, cc]), _mm_nt(v, r),
                              _mm(k * kd_ref[h], r), _mm_tn(q * qd_ref[h], dy)))
            for h, hs in enumerate(heads):
                qk, ds, dyp, vr, kr, qdy = first[h]
                q, k, dy = q_ref[sl, hs], k_ref[sl, hs], dy_ref[sl, hs]
                r = r_sc[h]
                da = ds * m_ref[h]
                dv_ref[0, sl, hs] = _mm_tn(qk * m_ref[h], dy) + kr
                dq_ref[0, sl, hs] = _mm(da, k) + dyp * qd_ref[h]
                dk_ref[0, sl, hs] = _mm_tn(da, q) + vr * kd_ref[h]
                acc_sc[h] += dyp * q * qdw_ref[h] + vr * k * kdw_ref[h]
                e_sc[h] += ds * qk * mw_ref[h]
                g_sc[h] += r * pst_ref[h, cc]
                r_sc[h] = r * g_ref[h] + qdy
            return carry

        lax.fori_loop(0, cb, chunk, 0)

        @pl.when(n == ns - 1)
        def _():
            for h in range(R_H):
                tot = jnp.sum(jnp.sum(acc_sc[h] + e_sc[h] + g_sc[h] * gw_ref[h], axis=0, keepdims=True),
                              axis=1, keepdims=True)
                dlam_ref[h] = jnp.broadcast_to(tot, (1, LANES))

    def step(d, n):
        return d * n + (1 - d) * (ns - 1 - n)

    blk = lambda off: pl.BlockSpec((cb * c, R_W), lambda d, n: (step(d, n), off))
    big = pl.BlockSpec((R_H, c, c), lambda d, n: (d, 0, 0))
    vec = pl.BlockSpec((R_H, 1, LANES), lambda d, n: (d, 0, 0))
    out = pl.BlockSpec((1, cb * c, R_W), lambda d, n: (d, step(d, n), 0))
    oshape = jax.ShapeDtypeStruct((2, seq, R_W), F32)
    sq = pltpu.VMEM((R_H, R_HD, R_HD), F32)
    return pl.pallas_call(
        body, name="ret_bwd", grid=(2, ns),
        in_specs=[blk(0), blk(0), blk(2), blk(0),
                  pl.BlockSpec((R_H, cb, R_HD, R_HD), lambda d, n: (d, step(d, n), 0, 0)),
                  big, big, big, big, big, big, vec, vec],
        out_specs=(out, out, out, vec),
        out_shape=(oshape, oshape, oshape, jax.ShapeDtypeStruct((2 * R_H, 1, LANES), F32)),
        scratch_shapes=[sq, sq, sq, sq],
    )(rq, rk, pr, dry, pst, m, mw, qd, qdw, kd, kdw, gdec, gw)


def _group_norm(ry):
    yn, rs = [], []
    for h in range(R_H):
        s = ry[:, h * R_HD:(h + 1) * R_HD]
        mu = jnp.mean(s, axis=-1, keepdims=True)
        cen = s - mu
        r = lax.rsqrt(jnp.mean(cen * cen, axis=-1, keepdims=True) + GN_EPS)
        yn.append(cen * r)
        rs.append(r)
    return yn, rs


def _merge_fwd(x, o, y2, pr, pg, gain_r, wao, wro, wout):
    seq = x.shape[0]
    tm = _tile(seq, 256)

    def body(x_ref, o_ref, yf_ref, yb_ref, rg_ref, ga_ref, gr_ref, gn_ref, wao_ref, wro_ref, wout_ref,
             x1_ref, mg_ref, ri_ref):
        yn, _ = _group_norm(yf_ref[0] + yb_ref[0])
        rg = rg_ref[...]
        ret_in = jnp.concatenate(yn, axis=1) * gn_ref[...] * (rg * _sigmoid(rg))
        ri_ref[...] = _bf(ret_in)
        attn_out = _mm(o_ref[...], wao_ref[...])
        ret_out = _mm(ret_in, wro_ref[...])
        merged = _sigmoid(ga_ref[...]) * attn_out + _sigmoid(gr_ref[...]) * ret_out
        mg_ref[...] = _bf(merged)
        x1_ref[...] = x_ref[...] + _mm(merged, wout_ref[...])

    row = lambda w_, j=0: pl.BlockSpec((tm, w_), lambda i: (i, j))
    ydir = lambda d: pl.BlockSpec((1, tm, R_W), lambda i: (d, i, 0))
    return pl.pallas_call(
        body, name="merge_fwd", grid=(seq // tm,),
        in_specs=[row(D), row(AQ_W), ydir(0), ydir(1), row(R_W, 3), row(D, 0), row(D, 1),
                  _full((1, R_W)), _full((AQ_W, D)), _full((R_W, D)), _full((D, D))],
        out_specs=(row(D), row(D), row(R_W)),
        out_shape=(jax.ShapeDtypeStruct((seq, D), F32), jax.ShapeDtypeStruct((seq, D), _MXU),
                   jax.ShapeDtypeStruct((seq, R_W), _MXU)),
    )(x, o, y2, y2, pr, pg, pg, gain_r, wao, wro, wout)


def _mlp_fwd(x1, gain, wup, wdown):
    seq = x1.shape[0]
    tm = _tile(seq, 512)
    fc = 2048
    nfc = FF // fc

    def body(x_ref, g_ref, wu_ref, wd_ref, x2_ref, hm_sc, acc_sc):
        c = pl.program_id(1)

        @pl.when(c == 0)
        def _():
            n, _ = _rms(x_ref[...])
            hm_sc[...] = _bf(n * g_ref[...])
            acc_sc[...] = jnp.zeros_like(acc_sc)

        halves = (slice(0, fc // 2), slice(fc // 2, fc))
        ups = [jnp.maximum(_mm(hm_sc[...], wu_ref[:, hs]), 0.0) for hs in halves]
        acc_sc[...] += _mm(ups[0] * ups[0], wd_ref[halves[0], :]) + _mm(ups[1] * ups[1], wd_ref[halves[1], :])

        @pl.when(c == nfc - 1)
        def _():
            x2_ref[...] = x_ref[...] + acc_sc[...]

    return pl.pallas_call(
        body, name="mlp_fwd", grid=(seq // tm, nfc),
        in_specs=[pl.BlockSpec((tm, D), lambda i, c: (i, 0)), pl.BlockSpec((1, D), lambda i, c: (0, 0)),
                  pl.BlockSpec((D, fc), lambda i, c: (0, c)), pl.BlockSpec((fc, D), lambda i, c: (c, 0))],
        out_specs=pl.BlockSpec((tm, D), lambda i, c: (i, 0)),
        out_shape=jax.ShapeDtypeStruct((seq, D), F32),
        scratch_shapes=[pltpu.VMEM((tm, D), _MXU), pltpu.VMEM((tm, D), F32)],
    )(x1, gain, wup, wdown)


def _ple_loss(x2, p, tgt, g_ple, g_fin, wpg, wpgt, wple):
    seq = x2.shape[0]
    tm = _tile(seq, 256)

    def body(x2_ref, p_ref, t_ref, gp_ref, gf_ref, wpg_ref, wpgt_ref, wple_ref,
             dx2_ref, de_ref, dz_ref, hp_ref, loss_ref, dgf_ref, dgp_ref):
        @pl.when(pl.program_id(0) == 0)
        def _():
            loss_ref[...] = jnp.zeros_like(loss_ref)
            dgf_ref[...] = jnp.zeros_like(dgf_ref)
            dgp_ref[...] = jnp.zeros_like(dgp_ref)

        x2 = x2_ref[...]
        gp, gf = gp_ref[...], gf_ref[...]
        n2, r2 = _rms(x2)
        hp = _bf(n2 * gp)
        hp_ref[...] = hp
        gate = _sigmoid(_mm(hp, wpg_ref[...]))
        e = _mm(p_ref[...], wple_ref[...])
        x3 = x2 + gate * e
        n3, r3 = _rms(x3)
        diff = n3 * gf - t_ref[...]
        row_loss = jnp.mean(diff * diff, axis=-1, keepdims=True)
        loss_ref[...] += 0.5 * jnp.sum(row_loss, axis=0, keepdims=True)
        dy = diff * (1.0 / D)
        dgf_ref[...] += jnp.sum(dy * n3, axis=0, keepdims=True)
        dx3 = _rms_bwd(n3, r3, gf, dy)
        de_ref[...] = _bf(dx3 * gate)
        dz = dx3 * e * gate * (1.0 - gate)
        dz_ref[...] = _bf(dz)
        dhp = _mm(dz, wpgt_ref[...])
        dgp_ref[...] += jnp.sum(dhp * n2, axis=0, keepdims=True)
        dx2_ref[...] = dx3 + _rms_bwd(n2, r2, gp, dhp)

    row = lambda w_: pl.BlockSpec((tm, w_), lambda i: (i, 0))
    act = lambda dt: jax.ShapeDtypeStruct((seq, D), dt)
    return pl.pallas_call(
        body, name="ple_loss", grid=(seq // tm,),
        in_specs=[row(D), row(PLE), row(D), _full((1, D)), _full((1, D)),
                  _full((D, D)), _full((D, D)), _full((PLE, D))],
        out_specs=(row(D), row(D), row(D), row(D), _full((1, LANES)), _full((1, D)), _full((1, D))),
        out_shape=(act(F32), act(_MXU), act(_MXU), act(_MXU), jax.ShapeDtypeStruct((1, LANES), F32),
                   jax.ShapeDtypeStruct((1, D), F32), jax.ShapeDtypeStruct((1, D), F32)),
    )(x2, p, tgt, g_ple, g_fin, wpg, wpgt, wple)


def _mlp_bwd(x1, dx2, gain, wup, wdownt, wupt):
    seq = x1.shape[0]
    tm = _tile(seq, 512)
    fc = 2048
    nfc = FF // fc

    def body(x_ref, dx2_ref, g_ref, wu_ref, wdt_ref, wut_ref,
             dx1_ref, a_ref, du_ref, hm_ref, dg_ref, dhm_sc):
        i = pl.program_id(0)
        c = pl.program_id(1)

        @pl.when((i == 0) & (c == 0))
        def _():
            dg_ref[...] = jnp.zeros_like(dg_ref)

        @pl.when(c == 0)
        def _():
            n, _ = _rms(x_ref[...])
            hm_ref[...] = _bf(n * g_ref[...])
            dhm_sc[...] = jnp.zeros_like(dhm_sc)

        halves = (slice(0, fc // 2), slice(fc // 2, fc))
        ups = [jnp.maximum(_mm(hm_ref[...], wu_ref[:, hs]), 0.0) for hs in halves]
        das = [_mm(dx2_ref[...], wdt_ref[:, hs]) for hs in halves]
        part = None
        for u, da, hs in zip(ups, das, halves):
            a_ref[:, hs] = _bf(u * u)
            du = _bf(da * (2.0 * u))
            du_ref[:, hs] = du
            t = _mm(du, wut_ref[hs, :])
            part = t if part is None else part + t
        dhm_sc[...] += part

        @pl.when(c == nfc - 1)
        def _():
            n, r = _rms(x_ref[...])
            dhm = dhm_sc[...]
            dg_ref[...] += jnp.sum(dhm * n, axis=0, keepdims=True)
            dx1_ref[...] = dx2_ref[...] + _rms_bwd(n, r, g_ref[...], dhm)

    rowd = pl.BlockSpec((tm, D), lambda i, c: (i, 0))
    rowf = pl.BlockSpec((tm, fc), lambda i, c: (i, c))
    return pl.pallas_call(
        body, name="mlp_bwd", grid=(seq // tm, nfc),
        in_specs=[rowd, rowd, pl.BlockSpec((1, D), lambda i, c: (0, 0)),
                  pl.BlockSpec((D, fc), lambda i, c: (0, c)), pl.BlockSpec((D, fc), lambda i, c: (0, c)),
                  pl.BlockSpec((fc, D), lambda i, c: (c, 0))],
        out_specs=(rowd, rowf, rowf, rowd, pl.BlockSpec((1, D), lambda i, c: (0, 0))),
        out_shape=(jax.ShapeDtypeStruct((seq, D), F32), jax.ShapeDtypeStruct((seq, FF), _MXU),
                   jax.ShapeDtypeStruct((seq, FF), _MXU), jax.ShapeDtypeStruct((seq, D), _MXU),
                   jax.ShapeDtypeStruct((1, D), F32)),
        scratch_shapes=[pltpu.VMEM((tm, D), F32)],
    )(x1, dx2, gain, wup, wdownt, wupt)


def _merge_bwd(dx1, o, y2, pr, pg, gain_r, wao, wro, woutt, waot, wrot):
    seq = dx1.shape[0]
    tm = _tile(seq, 256)

    def body(dx1_ref, o_ref, yf_ref, yb_ref, rg_ref, ga_ref, gr_ref, gn_ref, wao_ref, wro_ref,
             woutt_ref, waot_ref, wrot_ref,
             dpg_ref, dao_ref, dro_ref, do_ref, dry_ref, drg_ref, dgn_ref):
        @pl.when(pl.program_id(0) == 0)
        def _():
            dgn_ref[...] = jnp.zeros_like(dgn_ref)

        yn_l, rs_l = _group_norm(yf_ref[0] + yb_ref[0])
        yn = jnp.concatenate(yn_l, axis=1)
        rg = rg_ref[...]
        gn = gn_ref[...]
        sg = _sigmoid(rg)
        sil = rg * sg
        ret_in = yn * gn * sil
        attn_out = _mm(o_ref[...], wao_ref[...])
        ret_out = _mm(ret_in, wro_ref[...])
        sa = _sigmoid(ga_ref[...])
        sr = _sigmoid(gr_ref[...])
        dm = _mm(dx1_ref[...], woutt_ref[...])
        dpg_ref[:, 0:D] = _bf(dm * attn_out * sa * (1.0 - sa))
        dpg_ref[:, D:2 * D] = _bf(dm * ret_out * sr * (1.0 - sr))
        dao = _bf(dm * sa)
        dro = _bf(dm * sr)
        dao_ref[...] = dao
        dro_ref[...] = dro
        do_ref[...] = _mm(dao, waot_ref[...])
        dri = _mm(dro, wrot_ref[...])
        dgn_ref[...] += jnp.sum(dri * yn * sil, axis=0, keepdims=True)
        drg_ref[...] = _bf(dri * yn * gn * (sg * (1.0 + rg * (1.0 - sg))))
        dyn = dri * gn * sil
        dry = []
        for h in range(R_H):
            dh = dyn[:, h * R_HD:(h + 1) * R_HD]
            dry.append(rs_l[h] * (dh - jnp.mean(dh, axis=-1, keepdims=True)
                                  - yn_l[h] * jnp.mean(dh * yn_l[h], axis=-1, keepdims=True)))
        dry_ref[...] = jnp.concatenate(dry, axis=1)

    row = lambda w_, j=0: pl.BlockSpec((tm, w_), lambda i: (i, j))
    ydir = lambda d: pl.BlockSpec((1, tm, R_W), lambda i: (d, i, 0))
    return pl.pallas_call(
        body, name="merge_bwd", grid=(seq // tm,),
        in_specs=[row(D), row(AQ_W), ydir(0), ydir(1), row(R_W, 3), row(D, 0), row(D, 1),
                  _full((1, R_W)), _full((AQ_W, D)), _full((R_W, D)), _full((D, D)),
                  _full((D, AQ_W)), _full((D, R_W))],
        out_specs=(row(PG_W), row(D), row(D), row(AQ_W), row(R_W), row(R_W), _full((1, R_W))),
        out_shape=(jax.ShapeDtypeStruct((seq, PG_W), _MXU), jax.ShapeDtypeStruct((seq, D), _MXU),
                   jax.ShapeDtypeStruct((seq, D), _MXU), jax.ShapeDtypeStruct((seq, AQ_W), F32),
                   jax.ShapeDtypeStruct((seq, R_W), F32), jax.ShapeDtypeStruct((seq, R_W), _MXU),
                   jax.ShapeDtypeStruct((1, R_W), F32)),
    )(dx1, o, y2, y2, pr, pg, pg, gain_r, wao, wro, woutt, waot, wrot)


def _qk_prep_bwd(pa, dqh, dk2, dv2, rdq, rdk, rdv, drg, gq, gk, seg, ca, sa, cr, sr):
    seq = pa.shape[0]
    tm = _tile(seq, 256)

    def body(pa_ref, dqh_ref, dk2_ref, dv2_ref, rdqf_ref, rdqb_ref, rdkf_ref, rdkb_ref, rdvf_ref, rdvb_ref,
             drg_ref, gq_ref, gk_ref, seg_ref, ca_ref, sa_ref, cr_ref, sr_ref,
             dpa_ref, dpr_ref, dgq_ref, dgk_ref):
        @pl.when(pl.program_id(0) == 0)
        def _():
            dgq_ref[...] = jnp.zeros_like(dgq_ref)
            dgk_ref[...] = jnp.zeros_like(dgk_ref)

        ca_, sa_ = ca_ref[...], sa_ref[...]

        def norm_bwd(raw, gain, dy, segm, dg_ref):
            msq = jnp.dot(raw * raw, segm, precision=HIGHEST, preferred_element_type=F32)
            r = lax.rsqrt(msq + EPS)
            n = raw * r
            dg_ref[...] += jnp.sum(dy * n, axis=0, keepdims=True)
            dn = dy * gain
            return r * (dn - n * jnp.dot(dn * n, segm, precision=HIGHEST, preferred_element_type=F32))

        dqn = _rope(dqh_ref[...] * (A_HD ** -0.5), _cat(ca_, 4), -_cat(sa_, 4), A_HD // 2)
        dpa_ref[:, 0:AQ_W] = _bf(norm_bwd(pa_ref[:, 0:AQ_W], gq_ref[...], dqn, seg_ref[...], dgq_ref))
        dkn = _rope(dk2_ref[...], ca_, -sa_, A_HD // 2)
        dpa_ref[:, AQ_W:AQ_W + AKV_W] = _bf(norm_bwd(pa_ref[:, AQ_W:AQ_W + AKV_W], gk_ref[...], dkn,
                                                     seg_ref[0:AKV_W, 0:AKV_W], dgk_ref))
        dpa_ref[:, AQ_W + AKV_W:PA_W] = _bf(dv2_ref[...])
        cr_, sr_ = _cat(cr_ref[...], 4), -_cat(sr_ref[...], 4)
        dpr_ref[:, 0:R_W] = _bf(_rope((rdqf_ref[0] + rdqb_ref[0]) * (R_HD ** -0.5), cr_, sr_, R_HD // 2))
        dpr_ref[:, R_W:2 * R_W] = _bf(_rope(rdkf_ref[0] + rdkb_ref[0], cr_, sr_, R_HD // 2))
        dpr_ref[:, 2 * R_W:3 * R_W] = _bf(rdvf_ref[0] + rdvb_ref[0])
        dpr_ref[:, 3 * R_W:4 * R_W] = drg_ref[...]

    row = lambda w_: pl.BlockSpec((tm, w_), lambda i: (i, 0))
    ydir = lambda d: pl.BlockSpec((1, tm, R_W), lambda i: (d, i, 0))
    return pl.pallas_call(
        body, name="qk_prep_bwd", grid=(seq // tm,),
        in_specs=[row(PA_W), row(AQ_W), row(AKV_W), row(AKV_W), ydir(0), ydir(1), ydir(0), ydir(1),
                  ydir(0), ydir(1), row(R_W), _full((1, AQ_W)), _full((1, AKV_W)), _full((AQ_W, AQ_W)),
                  row(LANES), row(LANES), row(LANES), row(LANES)],
        out_specs=(row(PA_W), row(PR_W), _full((1, AQ_W)), _full((1, AKV_W))),
        out_shape=(jax.ShapeDtypeStruct((seq, PA_W), _MXU), jax.ShapeDtypeStruct((seq, PR_W), _MXU),
                   jax.ShapeDtypeStruct((1, AQ_W), F32), jax.ShapeDtypeStruct((1, AKV_W), F32)),
    )(pa, dqh, dk2, dv2, rdq, rdq, rdk, rdk, rdv, rdv, drg, gq, gk, seg, ca, sa, cr, sr)


def _in_proj_bwd(x, dx1, gain, dpa, dpr, dpg, wint):
    seq = x.shape[0]
    tm = _tile(seq, 256)

    def body(x_ref, dx1_ref, g_ref, dpa_ref, dpr_ref, dpg_ref, wt_ref, dx_ref, dg_ref):
        @pl.when(pl.program_id(0) == 0)
        def _():
            dg_ref[...] = jnp.zeros_like(dg_ref)

        dh = (_mm(dpa_ref[...], wt_ref[0:PA_W, :]) + _mm(dpr_ref[...], wt_ref[PA_W:PA_W + PR_W, :])
              + _mm(dpg_ref[...], wt_ref[PA_W + PR_W:IN_W, :]))
        n, r = _rms(x_ref[...])
        dg_ref[...] += jnp.sum(dh * n, axis=0, keepdims=True)
        dx_ref[...] = dx1_ref[...] + _rms_bwd(n, r, g_ref[...], dh)

    row = lambda w_: pl.BlockSpec((tm, w_), lambda i: (i, 0))
    return pl.pallas_call(
        body, name="in_proj_bwd", grid=(seq // tm,),
        in_specs=[row(D), row(D), _full((1, D)), row(PA_W), row(PR_W), row(PG_W), _full((IN_W, D))],
        out_specs=(row(D), _full((1, D))),
        out_shape=(jax.ShapeDtypeStruct((seq, D), F32), jax.ShapeDtypeStruct((1, D), F32)),
    )(x, dx1, gain, dpa, dpr, dpg, wint)


def _wgrad(a, b, name):
    seq, m = a.shape
    n = b.shape[1]
    tm, tn, ts = _tile(m, 1024), _tile(n, 1024), _tile(seq, 2048)
    ns = seq // ts

    def body(a_ref, b_ref, o_ref):
        @pl.when(pl.program_id(2) == 0)
        def _():
            o_ref[...] = jnp.zeros_like(o_ref)

        o_ref[...] += _mm_tn(a_ref[...], b_ref[...])

    return pl.pallas_call(
        body, name=name, grid=(m // tm, n // tn, ns),
        in_specs=[pl.BlockSpec((ts, tm), lambda i, j, s: (s, i)), pl.BlockSpec((ts, tn), lambda i, j, s: (s, j))],
        out_specs=pl.BlockSpec((tm, tn), lambda i, j, s: (i, j)),
        out_shape=jax.ShapeDtypeStruct((m, n), F32),
    )(a, b)


def _adamw_math(w, g, m, v):
    m = B1 * m + (1.0 - B1) * g
    v = B2 * v + (1.0 - B2) * (g * g)
    m_hat = m / (1.0 - B1 ** STEP)
    v_hat = v / (1.0 - B2 ** STEP)
    delta = -LR * (m_hat / (jnp.sqrt(v_hat) + ADAM_EPS) + WD * w)
    return delta, m, v


def _adamw_big(land, own, w, m, v, name):
    rws, cols = w.shape
    tr = next(t for t in range(min(rws, 288), 0, -8) if rws % t == 0)

    def body(l_ref, o_ref, w_ref, m_ref, v_ref, g_ref, d_ref, nm_ref, nv_ref):
        x, y, c = _mesh_pos()
        me = 4 * x + 2 * y + c
        g = o_ref[...]
        for j in range(N_DEV):
            g = g + jnp.where(me == j, 0.0, l_ref[j].astype(F32))
        g_ref[...] = g
        d_ref[...], nm_ref[...], nv_ref[...] = _adamw_math(w_ref[...], g, m_ref[...], v_ref[...])

    row = pl.BlockSpec((tr, cols), lambda i: (i, 0))
    shp = jax.ShapeDtypeStruct((rws, cols), F32)
    return pl.pallas_call(
        body, name=name, grid=(rws // tr,),
        in_specs=[pl.BlockSpec((N_DEV, tr, cols), lambda i: (0, i, 0)), row, row, row, row],
        out_specs=(row, row, row, row), out_shape=(shp, shp, shp, shp),
    )(land, own, w, m, v)


def _adamw_small(sland, w, m, v):
    def body(l_ref, w_ref, m_ref, v_ref, g_ref, d_ref, nm_ref, nv_ref, loss_ref):
        s = l_ref[0]
        for j in range(1, N_DEV):
            s = s + l_ref[j]
        w = w_ref[...]
        gq = s[8:9]
        for h in range(1, A_H):
            gq = gq + s[8 + h:9 + h]
        gk = s[16:17] + s[17:18]
        gdec = s[5:6] * _sigmoid(-w[5:6])
        g = jnp.concatenate([s[0:5], gdec, gq, gk], axis=0)
        g_ref[...] = g
        d_ref[...], nm_ref[...], nv_ref[...] = _adamw_math(w, g, m_ref[...], v_ref[...])
        loss_ref[...] = s[6:7, 0:LANES]

    shp = jax.ShapeDtypeStruct((8, PACK_COLS), F32)
    return pl.pallas_call(
        body, name="adamw_small",
        out_shape=(shp, shp, shp, shp, jax.ShapeDtypeStruct((1, LANES), F32)),
    )(sland, w, m, v)


_BIG = (("w_attn_o", AQ_W, D, 1), ("w_ret_o", R_W, D, 1), ("w_out", D, D, 0),
        ("w_up", D, FF, 1), ("w_down", FF, D, 0), ("w_ple_gate", D, D, 0), ("w_ple", PLE, D, 1))
IN_SHARD = IN_W // N_DEV
_SMALL = ("mix_norm", "mlp_norm", "ple_norm", "final_norm", "ret_norm_gain", "ret_decay_logit",
          "attn_q_norm", "attn_k_norm")


def _shard_shape(rows, cols, axis):
    return (rows // N_DEV, cols) if axis == 0 else (rows, cols // N_DEV)


def _pack_shards(shards):
    flat = jnp.concatenate([s.reshape(-1) for s in shards])
    return flat.reshape(-1, PACK_COLS)


def _unpack_gathered(gathered):
    flat = gathered.reshape(N_DEV, -1)
    out, off = {}, 0
    for name, rows, cols, axis in _BIG:
        sr, sc = _shard_shape(rows, cols, axis)
        blk = flat[:, off:off + sr * sc].reshape(N_DEV, sr, sc)
        off += sr * sc
        out[name] = blk.reshape(rows, cols) if axis == 0 else blk.transpose(1, 0, 2).reshape(rows, cols)
    return out


def _pack_full_grads(grads):
    parts = []
    for name, rows, cols, axis in _BIG:
        sr, sc = _shard_shape(rows, cols, axis)
        g = grads[name]
        blk = g.reshape(N_DEV, sr, sc) if axis == 0 else g.reshape(rows, N_DEV, sc).transpose(1, 0, 2)
        parts.append(blk.reshape(N_DEV, -1))
    flat = jnp.concatenate(parts, axis=1)
    return flat.reshape(N_DEV, -1, PACK_COLS)


def _unpack_shard(packed):
    flat = packed.reshape(-1)
    out, off = {}, 0
    for name, rows, cols, axis in _BIG:
        sr, sc = _shard_shape(rows, cols, axis)
        out[name] = flat[off:off + sr * sc].reshape(1, sr, sc)
        off += sr * sc
    return out


def _pack_small(vals):
    rows = [jnp.pad(vals[n].reshape(-1), (0, PACK_COLS - vals[n].size)) for n in _SMALL]
    return jnp.stack(rows)


def _unpack_small(packed, like):
    return {n: packed[i, :like[n].size].reshape(like[n].shape) for i, n in enumerate(_SMALL)}


def _row(v):
    return jnp.pad(v.reshape(-1), (0, PACK_COLS - v.size))


def kernel(x, p, mix_norm, w_in, attn_q_norm, attn_k_norm, ret_decay_logit, ret_norm_gain, w_attn_o, w_ret_o, w_out, mlp_norm, w_up, w_down, ple_norm, w_ple_gate, w_ple, final_norm, loss_target, m_mix_norm, m_w_in, m_attn_q_norm, m_attn_k_norm, m_ret_decay_logit, m_ret_norm_gain, m_w_attn_o, m_w_ret_o, m_w_out, m_mlp_norm, m_w_up, m_w_down, m_ple_norm, m_w_ple_gate, m_w_ple, m_final_norm, v_mix_norm, v_w_in, v_attn_q_norm, v_attn_k_norm, v_ret_decay_logit, v_ret_norm_gain, v_w_attn_o, v_w_ret_o, v_w_out, v_mlp_norm, v_w_up, v_w_down, v_ple_norm, v_w_ple_gate, v_w_ple, v_final_norm):
    args = dict(locals())
    seq = x.shape[1]
    xs = x[0]
    ps = p[0, 0]
    tgt = loss_target[0]

    big_names = [b[0] for b in _BIG]
    wshard = _pack_shards([args[n] for n in big_names])
    win_g, rest_g = _all_gather([w_in[0].astype(_MXU), wshard.astype(_MXU)])
    win = win_g.transpose(1, 0, 2).reshape(D, IN_W)
    wfull = _unpack_gathered(rest_g)
    wao, wro, wout = wfull["w_attn_o"], wfull["w_ret_o"], wfull["w_out"]
    wup, wdown, wpg, wple = wfull["w_up"], wfull["w_down"], wfull["w_ple_gate"], wfull["w_ple"]

    g_mix, g_mlp, g_ple = mix_norm, mlp_norm, ple_norm
    g_fin = final_norm.reshape(1, D)
    gq = jnp.tile(attn_q_norm, (1, A_H))
    gk = jnp.tile(attn_k_norm, (1, A_KV))
    seg = _seg_mean_matrix()
    ca, sa, cr, sr = _rope_tables(seq)

    pa, pr, pg, h = _in_proj(xs, g_mix, win)
    qh, kh, vh, rqh, rkh = _qk_prep(pa, pr, gq, gk, seg, ca, sa, cr, sr)

    tq = _tile(seq, 128)
    tk = _tile(seq // 4, 1024)
    qt8 = qh.reshape(seq, A_H, A_HD).transpose(1, 2, 0)
    vta = jnp.stack([jnp.concatenate([_chunk_t(vh[:, g * A_HD:(g + 1) * A_HD], tk),
                                      jnp.ones((seq // tk, 16, tk), _MXU)], axis=1) for g in range(A_KV)])
    ot, lse = _attn_fwd(qt8, kh, vta, tq, tk)
    o = _heads_to_rows(ot)

    zb = jnp.broadcast_to(ret_decay_logit.reshape(2 * R_H, 1, 1), (2 * R_H, 1, LANES))
    tm_, tmw, tqd, tqdw, tkd, tkdw, tg, tgw = _ret_tables(zb)
    cb = _tile(seq // CHUNK, 8)
    y2, pst = _ret_fwd(rqh, rkh, pr, tm_, tqd, tkd, tg, cb)

    x1, merged, ret_in = _merge_fwd(xs, o, y2, pr, pg, ret_norm_gain, wao, wro, wout)
    x2 = _mlp_fwd(x1, g_mlp, wup, wdown)

    dx2, de, dz, hp, loss_p, dg_fin, dg_ple = _ple_loss(x2, ps, tgt, g_ple, g_fin, wpg, wpg.T, wple)
    dx1, act, du, hm, dg_mlp = _mlp_bwd(x1, dx2, g_mlp, wup, wdown.T, wup.T)
    dpg, dao, dro, do, dry, drg, dg_gn = _merge_bwd(dx1, o, y2, pr, pg, ret_norm_gain, wao, wro,
                                                    wout.T, wao.T, wro.T)
    rdq, rdk, rdv, dlam = _ret_bwd(rqh, rkh, pr, dry, pst, tm_, tmw, tqd, tqdw, tkd, tkdw, tg, tgw, cb)

    ksplit = 2
    tkb = _tile(seq // 4, 256)
    dot_ = do.reshape(seq, A_H, A_HD).transpose(1, 2, 0)
    dkt, dvt, dqt = _attn_bwd(qt8, ot, dot_, lse, kh, vh, _chunk_t(kh, tkb), tq, tkb, ksplit)
    dqh = _heads_to_rows(jnp.sum(dqt, axis=0))
    dpa, dpr, dg_q, dg_k = _qk_prep_bwd(pa, dqh, _chunks_to_rows(dkt), _chunks_to_rows(dvt), rdq, rdk, rdv, drg, gq, gk, seg, ca, sa, cr, sr)
    grad_x, dg_mix = _in_proj_bwd(xs, dx1, g_mix, dpa, dpr, dpg, win.T)

    wg = {
        "w_in": jnp.concatenate([_wgrad(h, dpa, "wgrad_in_a"), _wgrad(h, dpr, "wgrad_in_r"),
                                 _wgrad(h, dpg, "wgrad_in_g")], axis=1),
        "w_attn_o": _wgrad(o, dao, "wgrad_attn_o"),
        "w_ret_o": _wgrad(ret_in, dro, "wgrad_ret_o"),
        "w_out": _wgrad(merged, dx1, "wgrad_out"),
        "w_up": _wgrad(hm, du, "wgrad_up"),
        "w_down": _wgrad(act, dx2, "wgrad_down"),
        "w_ple_gate": _wgrad(hp, dz, "wgrad_ple_gate"),
        "w_ple": _wgrad(ps, de, "wgrad_ple"),
    }
    gpack = _pack_full_grads(wg)
    gpack_in = wg["w_in"].reshape(D, N_DEV, IN_SHARD).transpose(1, 0, 2)
    small = jnp.stack(
        [_row(dg_mix), _row(dg_mlp), _row(dg_ple), _row(dg_fin), _row(dg_gn), _row(dlam[:, 0, 0]),
         _row(loss_p[0, 0:1]), jnp.zeros((PACK_COLS,), F32)]
        + [_row(dg_q[0, hh * A_HD:(hh + 1) * A_HD]) for hh in range(A_H)]
        + [_row(dg_k[0, hh * A_HD:(hh + 1) * A_HD]) for hh in range(A_KV)]
        + [jnp.zeros((PACK_COLS,), F32)] * (SMALL_ROWS - 18))

    me = 4 * lax.axis_index("x") + 2 * lax.axis_index("y") + lax.axis_index("c")
    own_in = lax.dynamic_index_in_dim(gpack_in, me, axis=0, keepdims=False)
    own = lax.dynamic_index_in_dim(gpack, me, axis=0, keepdims=False)
    land_in, land, sland = _exchange_grads([_bf(gpack_in), _bf(gpack), small[None]])
    in_sh = _adamw_big(land_in, own_in, w_in[0], m_w_in[0], v_w_in[0], "adamw_w_in")
    g_sh, d_sh, m_sh, v_sh = _adamw_big(land, own, wshard, _pack_shards([args["m_" + n] for n in big_names]),
                                        _pack_shards([args["v_" + n] for n in big_names]), "adamw_shard")
    g_sm, d_sm, m_sm, v_sm, loss_row = _adamw_small(
        sland, _pack_small({n: args[n] for n in _SMALL}), _pack_small({n: args["m_" + n] for n in _SMALL}),
        _pack_small({n: args["v_" + n] for n in _SMALL}))

    names = ["mix_norm", "w_in", "attn_q_norm", "attn_k_norm", "ret_decay_logit", "ret_norm_gain", "w_attn_o",
             "w_ret_o", "w_out", "mlp_norm", "w_up", "w_down", "ple_norm", "w_ple_gate", "w_ple", "final_norm"]
    like = {n: args[n] for n in _SMALL}
    outs = [loss_row[0, 0], grad_x[None]]
    for big, sm, w_in_part in ((g_sh, g_sm, in_sh[0]), (d_sh, d_sm, in_sh[1]), (m_sh, m_sm, in_sh[2]),
                               (v_sh, v_sm, in_sh[3])):
        table = {**_unpack_shard(big), **_unpack_small(sm, like), "w_in": w_in_part[None]}
        outs += [table[n] for n in names]
    return tuple(outs)
```

```python
import functools

import jax
import jax.numpy as jnp
from jax import lax
from jax.experimental import pallas as pl
from jax.experimental.pallas import tpu as pltpu

F32 = jnp.float32
_MXU = jnp.bfloat16

D = 1024
PLE = 256
GRID_W = 64
A_HD = 64
A_H = 8
A_KV = 2
A_G = A_H // A_KV
AQ_W = A_H * A_HD
AKV_W = A_KV * A_HD
R_HD = 128
R_H = 4
R_W = R_H * R_HD
IN_W = AQ_W + 2 * AKV_W + 4 * R_W + 2 * D
PA_W = AQ_W + 2 * AKV_W
PR_W = 4 * R_W
PG_W = 2 * D
FF = 4 * D
CHUNK = 128
ROPE_THETA = 10000.0
EPS = 1e-6
GN_EPS = 1e-5
N_DEV = 8

LR, B1, B2, ADAM_EPS, WD, STEP = 0.001, 0.9, 0.999, 1e-08, 0.01, 10

LANES = 128
PACK_COLS = 1024
SMALL_ROWS = 24
HIGHEST = lax.Precision.HIGHEST


def _tile(n, pref):
    t = min(n, pref)
    assert n % t == 0, (n, t)
    return t


def _bf(a):
    return a.astype(_MXU)


def _mm(a, b):
    return jnp.dot(_bf(a), _bf(b), preferred_element_type=F32)


def _mm_nt(a, b):
    return lax.dot_general(_bf(a), _bf(b), (((1,), (1,)), ((), ())), preferred_element_type=F32)


def _mm_tn(a, b):
    return lax.dot_general(_bf(a), _bf(b), (((0,), (0,)), ((), ())), preferred_element_type=F32)


def _sigmoid(z):
    return 1.0 / (1.0 + jnp.exp(-z))


def _rms(x):
    r = lax.rsqrt(jnp.mean(x * x, axis=-1, keepdims=True) + EPS)
    return x * r, r


def _rms_bwd(n, r, gain, dy):
    dn = dy * gain
    return r * (dn - n * jnp.mean(dn * n, axis=-1, keepdims=True))


def _swap_halves(x, half):
    n = x.shape[-1]
    lane = lax.broadcasted_iota(jnp.int32, x.shape, x.ndim - 1)
    first = (lane % (2 * half)) < half
    return jnp.where(first, pltpu.roll(x, n - half, axis=1), pltpu.roll(x, half, axis=1))


def _rope(x, cos, sin, half):
    return x * cos + _swap_halves(x, half) * sin


def _cat(t, reps):
    return jnp.concatenate([t] * reps, axis=1)


def _full(shape):
    nd = len(shape)
    return pl.BlockSpec(shape, lambda *_: (0,) * nd)


def _rope_tables(seq):
    def tab(head_dim):
        n_axis = head_dim // 4
        freqs = ROPE_THETA ** (-jnp.arange(n_axis, dtype=F32) / n_axis)
        rows = seq // GRID_W
        row = jnp.repeat(jnp.arange(rows, dtype=F32), GRID_W)
        col = jnp.tile(jnp.arange(GRID_W, dtype=F32), rows)
        ang = jnp.concatenate([row[:, None] * freqs, col[:, None] * freqs], axis=-1)
        c, s = jnp.cos(ang), jnp.sin(ang)
        return jnp.concatenate([c, c], axis=-1), jnp.concatenate([-s, s], axis=-1)
    ca, sa = tab(A_HD)
    cr, sr = tab(R_HD)
    return jnp.tile(ca, (1, 2)), jnp.tile(sa, (1, 2)), cr, sr


def _seg_mean_matrix():
    i = jnp.arange(AQ_W) // A_HD
    return (i[:, None] == i[None, :]).astype(F32) / A_HD


def _mesh_pos():
    return lax.axis_index("x"), lax.axis_index("y"), lax.axis_index("c")


def _all_gather(shards):
    n = len(shards)

    def body(*refs):
        x_refs, out_refs = refs[:n], refs[n:2 * n]
        send_sems, recv_sems, local_sems = refs[2 * n:]
        x, y, c = _mesh_pos()
        me, sibling = (x, y, c), (x, y, 1 - c)
        chips = [(1 - x, y), (x, 1 - y), (1 - x, 1 - y)]

        def slot(t, px, py, pc):
            return out_refs[t].at[4 * px + 2 * py + pc]

        def copy(t, k, block, to, src=None):
            return pltpu.make_async_remote_copy(
                src_ref=slot(t, *block) if src is None else src, dst_ref=slot(t, *block),
                send_sem=send_sems.at[7 * t + k], recv_sem=recv_sems.at[7 * t + k],
                device_id=to, device_id_type=pl.DeviceIdType.MESH)

        mine = [pltpu.make_async_copy(x_refs[t], slot(t, *me), local_sems.at[t]) for t in range(n)]
        for cp in mine:
            cp.start()
        first = []
        for t in range(n):
            first.append(copy(t, 0, me, sibling, src=x_refs[t]))
            first += [copy(t, 1 + j, me, (*chip, c), src=x_refs[t]) for j, chip in enumerate(chips)]
        for cp in first:
            cp.start()
        passed = []
        for t in range(n):
            for j, chip in enumerate(chips):
                copy(t, 1 + j, (*chip, c), me).wait_recv()
                passed.append(copy(t, 4 + j, (*chip, c), sibling))
                passed[-1].start()
        for t in range(n):
            copy(t, 0, sibling, me).wait_recv()
            for j, chip in enumerate(chips):
                copy(t, 4 + j, (*chip, 1 - c), me).wait_recv()
        for cp in first + passed:
            cp.wait_send()
        for cp in mine:
            cp.wait()

    anyspec = pl.BlockSpec(memory_space=pl.ANY)
    return pl.pallas_call(
        body, name="all_gather_weights",
        out_shape=tuple(jax.ShapeDtypeStruct((N_DEV,) + s.shape, s.dtype) for s in shards),
        in_specs=[anyspec] * n, out_specs=(anyspec,) * n,
        scratch_shapes=[pltpu.SemaphoreType.DMA((7 * n,)), pltpu.SemaphoreType.DMA((7 * n,)),
                        pltpu.SemaphoreType.DMA((n,))],
    )(*shards)


def _exchange_grads(packs):
    n = len(packs)

    def body(*refs):
        g_refs, land_refs = refs[:n], refs[n:2 * n]
        send_sems, recv_sems, local_sems = refs[2 * n:]
        x, y, c = _mesh_pos()
        me = 4 * x + 2 * y + c

        def row(t, j):
            return g_refs[t].at[j if packs[t].shape[0] == N_DEV else 0]

        own = [pltpu.make_async_copy(row(t, me), land_refs[t].at[me], local_sems.at[t]) for t in range(n)]
        for cp in own:
            cp.start()
        sends = []
        for k in range(1, N_DEV):
            peer = (x ^ ((k >> 2) & 1), y ^ ((k >> 1) & 1), c ^ (k & 1))
            pidx = 4 * peer[0] + 2 * peer[1] + peer[2]
            for t in range(n):
                sends.append(pltpu.make_async_remote_copy(
                    src_ref=row(t, pidx), dst_ref=land_refs[t].at[me],
                    send_sem=send_sems.at[7 * t + k - 1], recv_sem=recv_sems.at[7 * t + k - 1],
                    device_id=peer, device_id_type=pl.DeviceIdType.MESH))
                sends[-1].start()
        for k in range(1, N_DEV):
            peer = (x ^ ((k >> 2) & 1), y ^ ((k >> 1) & 1), c ^ (k & 1))
            pidx = 4 * peer[0] + 2 * peer[1] + peer[2]
            for t in range(n):
                pltpu.make_async_remote_copy(
                    src_ref=row(t, me), dst_ref=land_refs[t].at[pidx],
                    send_sem=send_sems.at[7 * t + k - 1], recv_sem=recv_sems.at[7 * t + k - 1],
                    device_id=peer, device_id_type=pl.DeviceIdType.MESH).wait_recv()
        for cp in sends:
            cp.wait_send()
        for cp in own:
            cp.wait()

    anyspec = pl.BlockSpec(memory_space=pl.ANY)
    return pl.pallas_call(
        body, name="exchange_grads",
        out_shape=tuple(jax.ShapeDtypeStruct((N_DEV,) + g.shape[1:], g.dtype) for g in packs),
        in_specs=[anyspec] * n, out_specs=(anyspec,) * n,
        scratch_shapes=[pltpu.SemaphoreType.DMA((7 * n,)), pltpu.SemaphoreType.DMA((7 * n,)),
                        pltpu.SemaphoreType.DMA((n,))],
    )(*packs)


def _in_proj(x, gain, w):
    seq = x.shape[0]
    tm = _tile(seq, 256)

    def body(x_ref, g_ref, w_ref, pa_ref, pr_ref, pg_ref, h_ref):
        n, _ = _rms(x_ref[...])
        h = _bf(n * g_ref[...])
        h_ref[...] = h
        pa_ref[...] = _mm(h, w_ref[:, 0:PA_W])
        pr_ref[...] = _mm(h, w_ref[:, PA_W:PA_W + PR_W])
        pg_ref[...] = _mm(h, w_ref[:, PA_W + PR_W:IN_W])

    row = lambda w_: pl.BlockSpec((tm, w_), lambda i: (i, 0))
    return pl.pallas_call(
        body, name="in_proj", grid=(seq // tm,),
        in_specs=[row(D), _full((1, D)), _full((D, IN_W))],
        out_specs=(row(PA_W), row(PR_W), row(PG_W), row(D)),
        out_shape=(jax.ShapeDtypeStruct((seq, PA_W), F32), jax.ShapeDtypeStruct((seq, PR_W), F32),
                   jax.ShapeDtypeStruct((seq, PG_W), F32), jax.ShapeDtypeStruct((seq, D), _MXU)),
    )(x, gain, w)


def _qk_prep(pa, pr, gq, gk, seg, ca, sa, cr, sr):
    seq = pa.shape[0]
    tm = _tile(seq, 256)

    def body(pa_ref, pr_ref, gq_ref, gk_ref, seg_ref, ca_ref, sa_ref, cr_ref, sr_ref,
             qh_ref, kh_ref, v_ref, rq_ref, rk_ref):
        q = pa_ref[:, 0:AQ_W]
        k = pa_ref[:, AQ_W:AQ_W + AKV_W]
        v_ref[...] = _bf(pa_ref[:, AQ_W + AKV_W:PA_W])
        ca_, sa_ = ca_ref[...], sa_ref[...]
        msq = jnp.dot(q * q, seg_ref[...], precision=HIGHEST, preferred_element_type=F32)
        qn = q * lax.rsqrt(msq + EPS) * gq_ref[...]
        qh_ref[...] = _bf(_rope(qn, _cat(ca_, 4), _cat(sa_, 4), A_HD // 2) * (A_HD ** -0.5))
        msk = jnp.dot(k * k, seg_ref[0:AKV_W, 0:AKV_W], precision=HIGHEST, preferred_element_type=F32)
        kn = k * lax.rsqrt(msk + EPS) * gk_ref[...]
        kh_ref[...] = _bf(_rope(kn, ca_, sa_, A_HD // 2))
        cr_, sr_ = _cat(cr_ref[...], 4), _cat(sr_ref[...], 4)
        rq_ref[...] = _rope(pr_ref[:, 0:R_W], cr_, sr_, R_HD // 2) * (R_HD ** -0.5)
        rk_ref[...] = _rope(pr_ref[:, R_W:2 * R_W], cr_, sr_, R_HD // 2)

    row = lambda w_: pl.BlockSpec((tm, w_), lambda i: (i, 0))
    return pl.pallas_call(
        body, name="qk_prep", grid=(seq // tm,),
        in_specs=[row(PA_W), row(2 * R_W), _full((1, AQ_W)), _full((1, AKV_W)), _full((AQ_W, AQ_W)),
                  row(LANES), row(LANES), row(LANES), row(LANES)],
        out_specs=(row(AQ_W), row(AKV_W), row(AKV_W), row(R_W), row(R_W)),
        out_shape=(jax.ShapeDtypeStruct((seq, AQ_W), _MXU), jax.ShapeDtypeStruct((seq, AKV_W), _MXU),
                   jax.ShapeDtypeStruct((seq, AKV_W), _MXU), jax.ShapeDtypeStruct((seq, R_W), F32),
                   jax.ShapeDtypeStruct((seq, R_W), F32)),
    )(pa, pr, gq, gk, seg, ca, sa, cr, sr)


def _chunk_t(a, tk):
    seq = a.shape[0]
    return a.reshape(seq // tk, tk, a.shape[1]).transpose(0, 2, 1)


def _heads_to_rows(t):
    return t.transpose(2, 0, 1).reshape(t.shape[2], AQ_W)


def _attn_fwd(qt8, k2, vta, tq, tk):
    seq = k2.shape[0]
    nck = seq // tk
    rows = A_G * tq
    vrows = vta.shape[2]
    rb = _tile(tk, 64)
    assert nck % 2 == 0, nck

    def body(qt_ref, k_ref, vt_ref, o_ref, lse_ref, m_sc, acc_sc, qtp_sc, s_a, s_b, p_a, p_b, al_a, al_b):
        g = pl.program_id(0)
        qtp_sc[...] = jnp.zeros_like(qtp_sc)
        qtp_sc[pl.ds(pl.multiple_of(g * A_HD, A_HD), A_HD), :] = jnp.concatenate(
            [qt_ref[a] for a in range(A_G)], axis=1)
        m_sc[...] = jnp.full((1, rows), -jnp.inf, F32)
        acc_sc[...] = jnp.zeros_like(acc_sc)

        def scores(c):
            kc = k_ref[pl.ds(pl.multiple_of(c * tk, tk), tk), :]
            return _mm(kc, qtp_sc[...])

        def stage(c, s_cur, s_nxt, p_cur, p_prv, al_cur, al_prv, first=False, last=False):
            if not last:
                s_nxt[...] = scores(c + 1)
            if not first:
                acc_sc[...] = al_prv[...] * acc_sc[...] + _mm(vt_ref[0, c - 1], p_prv[...])
            m_old = m_sc[...]
            mx = None
            for r in range(0, tk, rb):
                bm = jnp.max(s_cur[r:r + rb, :].reshape(rb // 8, 8, rows), axis=0)
                mx = bm if mx is None else jnp.maximum(mx, bm)
            m_new = jnp.maximum(m_old, jnp.max(mx, axis=0, keepdims=True))
            for r in range(0, tk, rb):
                p_cur[r:r + rb, :] = _bf(jnp.exp(s_cur[r:r + rb, :] - m_new))
            al_cur[...] = jnp.exp(m_old - m_new)
            m_sc[...] = m_new

        s_a[...] = scores(0)
        stage(0, s_a, s_b, p_a, p_b, al_a, al_b, first=True)

        def pair(j, carry):
            stage(2 * j + 1, s_b, s_a, p_b, p_a, al_b, al_a)
            stage(2 * j + 2, s_a, s_b, p_a, p_b, al_a, al_b)
            return carry

        lax.fori_loop(0, nck // 2 - 1, pair, 0)
        stage(nck - 1, s_b, s_a, p_b, p_a, al_b, al_a, last=True)
        acc = al_b[...] * acc_sc[...] + _mm(vt_ref[0, nck - 1], p_b[...])
        l = acc[A_HD:A_HD + 1, :]
        lse = m_sc[...] + jnp.log(l)
        out = acc[0:A_HD, :] * (1.0 / l)
        for a in range(A_G):
            o_ref[a] = out[:, a * tq:(a + 1) * tq]
            lse_ref[a] = lse[:, a * tq:(a + 1) * tq]

    return pl.pallas_call(
        body, name="attn_fwd", grid=(A_KV, seq // tq),
        in_specs=[pl.BlockSpec((A_G, A_HD, tq), lambda g, i: (g, 0, i)),
                  _full((seq, LANES)), pl.BlockSpec((1, nck, vrows, tk), lambda g, i: (g, 0, 0, 0))],
        out_specs=(pl.BlockSpec((A_G, A_HD, tq), lambda g, i: (g, 0, i)),
                   pl.BlockSpec((A_G, 1, tq), lambda g, i: (g, 0, i))),
        out_shape=(jax.ShapeDtypeStruct((A_H, A_HD, seq), F32), jax.ShapeDtypeStruct((A_H, 1, seq), F32)),
        scratch_shapes=[pltpu.VMEM((1, rows), F32), pltpu.VMEM((vrows, rows), F32), pltpu.VMEM((LANES, rows), _MXU),
                        pltpu.VMEM((tk, rows), F32), pltpu.VMEM((tk, rows), F32),
                        pltpu.VMEM((tk, rows), _MXU), pltpu.VMEM((tk, rows), _MXU),
                        pltpu.VMEM((1, rows), F32), pltpu.VMEM((1, rows), F32)],
    )(qt8, k2, vta)


def _attn_bwd(qt8, ot, dot_, lse, k2, v2, k2t, tq, tk, ksplit):
    seq = k2.shape[0]
    sh = seq // ksplit
    nck = sh // tk
    rows = A_G * tq
    rb = _tile(tk, 32768 // rows)
    assert nck % 2 == 0, nck

    def body(qt_ref, ot_ref, dot_ref, lse_ref, k_ref, v_ref, kt_ref,
             dk_ref, dv_ref, dq_ref, dq_sc, qtp_sc, dotp_sc, pt_sc, dst_sc,
             s_a, s_b, dp_a, dp_b, p_a, p_b, ds_a, ds_b):
        g = pl.program_id(1)
        hrows = pl.ds(pl.multiple_of(g * A_HD, A_HD), A_HD)

        @pl.when(pl.program_id(2) == 0)
        def _():
            dk_ref[...] = jnp.zeros_like(dk_ref)
            dv_ref[...] = jnp.zeros_like(dv_ref)

        lse_row = jnp.concatenate([lse_ref[a] for a in range(A_G)], axis=1)
        dd = jnp.concatenate([jnp.sum(ot_ref[a] * dot_ref[a], axis=0, keepdims=True)
                              for a in range(A_G)], axis=1)
        qtp_sc[...] = jnp.zeros_like(qtp_sc)
        dotp_sc[...] = jnp.zeros_like(dotp_sc)
        qtp_sc[hrows, :] = jnp.concatenate([qt_ref[a] for a in range(A_G)], axis=1)
        dotp_sc[hrows, :] = _bf(jnp.concatenate([dot_ref[a] for a in range(A_G)], axis=1))
        dq_sc[...] = jnp.zeros_like(dq_sc)

        def products(c, s_ref, dp_ref):
            sl = pl.ds(pl.multiple_of(c * tk, tk), tk)
            s_ref[...] = _mm(k_ref[sl, :], qtp_sc[...])
            dp_ref[...] = _mm(v_ref[sl, :], dotp_sc[...])

        def accumulate(c, p_ref, ds_ref):
            pt_sc[...] = p_ref[...].T
            dst_sc[...] = ds_ref[...].T
            dq_sc[...] += _mm(kt_ref[c, hrows, :], ds_ref[...])
            dv_ref[0, c] += _mm(dotp_sc[hrows, :], pt_sc[...])
            dk_ref[0, c] += _mm(qtp_sc[hrows, :], dst_sc[...])

        def stage(c, s_cur, dp_cur, s_nxt, dp_nxt, p_cur, ds_cur, p_prv, ds_prv, first=False, last=False):
            if not last:
                products(c + 1, s_nxt, dp_nxt)
            if not first:
                accumulate(c - 1, p_prv, ds_prv)
            for r in range(0, tk, rb):
                p = jnp.exp(s_cur[r:r + rb, :] - lse_row)
                p_cur[r:r + rb, :] = _bf(p)
                ds_cur[r:r + rb, :] = _bf(p * (dp_cur[r:r + rb, :] - dd))

        products(0, s_a, dp_a)
        stage(0, s_a, dp_a, s_b, dp_b, p_a, ds_a, p_b, ds_b, first=True)

        def pair(j, carry):
            stage(2 * j + 1, s_b, dp_b, s_a, dp_a, p_b, ds_b, p_a, ds_a)
            stage(2 * j + 2, s_a, dp_a, s_b, dp_b, p_a, ds_a, p_b, ds_b)
            return carry

        lax.fori_loop(0, nck // 2 - 1, pair, 0)
        stage(nck - 1, s_b, dp_b, s_a, dp_a, p_b, ds_b, p_a, ds_a, last=True)
        accumulate(nck - 1, p_b, ds_b)
        for a in range(A_G):
            dq_ref[0, a] = dq_sc[:, a * tq:(a + 1) * tq]

    tspec = pl.BlockSpec((A_G, A_HD, tq), lambda s, g, i: (g, 0, i))
    kspec = pl.BlockSpec((sh, LANES), lambda s, g, i: (s, 0))
    gspec = pl.BlockSpec((1, nck, A_HD, tk), lambda s, g, i: (g, s, 0, 0))
    gshape = jax.ShapeDtypeStruct((A_KV, seq // tk, A_HD, tk), F32)
    big = lambda dt: pltpu.VMEM((tk, rows), dt)
    bigt = pltpu.VMEM((rows, tk), _MXU)
    return pl.pallas_call(
        body, name="attn_bwd", grid=(ksplit, A_KV, seq // tq),
        in_specs=[tspec, tspec, tspec, pl.BlockSpec((A_G, 1, tq), lambda s, g, i: (g, 0, i)),
                  kspec, kspec, pl.BlockSpec((nck, LANES, tk), lambda s, g, i: (s, 0, 0))],
        out_specs=(gspec, gspec, pl.BlockSpec((1, A_G, A_HD, tq), lambda s, g, i: (s, g, 0, i))),
        out_shape=(gshape, gshape, jax.ShapeDtypeStruct((ksplit, A_H, A_HD, seq), F32)),
        scratch_shapes=[pltpu.VMEM((A_HD, rows), F32), pltpu.VMEM((LANES, rows), _MXU), pltpu.VMEM((LANES, rows), _MXU),
                        bigt, bigt,
                        big(F32), big(F32), big(F32), big(F32), big(_MXU), big(_MXU), big(_MXU), big(_MXU)],
    )(qt8, ot, dot_, lse, k2, v2, k2t)


def _chunks_to_rows(t):
    return t.transpose(1, 3, 0, 2).reshape(t.shape[1] * t.shape[3], AKV_W)


def _ret_tables(zb):
    c = CHUNK

    def body(z_ref, m_ref, mw_ref, qd_ref, qdw_ref, kd_ref, kdw_ref, g_ref, gw_ref):
        fwd = pl.program_id(0) < R_H
        z = z_ref[0]
        lam = jnp.minimum(z, 0.0) - jnp.log(1.0 + jnp.exp(-jnp.abs(z)))
        i = lax.broadcasted_iota(jnp.int32, (c, c), 0).astype(F32)
        j = lax.broadcasted_iota(jnp.int32, (c, c), 1).astype(F32)
        diff = jnp.where(fwd, i - j, j - i)
        keep = diff >= jnp.where(fwd, 0.0, 1.0)
        dist = jnp.maximum(diff, 0.0)
        m = jnp.where(keep, jnp.exp(lam * dist), 0.0)
        m_ref[0] = m
        mw_ref[0] = m * dist
        fq = jnp.where(fwd, i + 1.0, c - i)
        qd = jnp.exp(lam * fq)
        qd_ref[0] = qd
        qdw_ref[0] = qd * fq
        fk = jnp.where(fwd, c - 1.0 - i, i)
        kd = jnp.exp(lam * fk)
        kd_ref[0] = kd
        kdw_ref[0] = kd * fk
        gdec = jnp.exp(lam * c)
        g_ref[0] = gdec
        gw_ref[0] = gdec * c

    big = pl.BlockSpec((1, c, c), lambda t: (t, 0, 0))
    vec = pl.BlockSpec((1, 1, LANES), lambda t: (t, 0, 0))
    bshape = jax.ShapeDtypeStruct((2 * R_H, c, c), F32)
    vshape = jax.ShapeDtypeStruct((2 * R_H, 1, LANES), F32)
    return pl.pallas_call(
        body, name="ret_tables", grid=(2 * R_H,), in_specs=[vec],
        out_specs=(big, big, big, big, big, big, vec, vec),
        out_shape=(bshape,) * 6 + (vshape, vshape),
    )(zb)


def _ret_fwd(rq, rk, pr, m, qd, kd, gdec, cb):
    seq = rq.shape[0]
    c = CHUNK
    ns = seq // (cb * c)

    def body(q_ref, k_ref, v_ref, m_ref, qd_ref, kd_ref, g_ref, y_ref, pst_ref, p_sc):
        d = pl.program_id(0)

        @pl.when(pl.program_id(1) == 0)
        def _():
            p_sc[...] = jnp.zeros_like(p_sc)

        def chunk(j, carry):
            cc = jnp.where(d == 0, j, cb - 1 - j)
            sl = pl.ds(pl.multiple_of(cc * c, c), c)
            heads = [slice(h * R_HD, (h + 1) * R_HD) for h in range(R_H)]
            qk = [_mm_nt(q_ref[sl, hs], k_ref[sl, hs]) for hs in heads]
            qp = [_mm(q_ref[sl, hs] * qd_ref[h], p_sc[h]) for h, hs in enumerate(heads)]
            kv = [_mm_tn(k_ref[sl, hs] * kd_ref[h], v_ref[sl, hs]) for h, hs in enumerate(heads)]
            for h, hs in enumerate(heads):
                p = p_sc[h]
                pst_ref[h, cc] = p
                y_ref[0, sl, hs] = _mm(qk[h] * m_ref[h], v_ref[sl, hs]) + qp[h]
                p_sc[h] = p * g_ref[h] + kv[h]
            return carry

        lax.fori_loop(0, cb, chunk, 0)

    def step(d, n):
        return d * (ns - 1 - n) + (1 - d) * n

    blk = lambda off: pl.BlockSpec((cb * c, R_W), lambda d, n: (step(d, n), off))
    big = pl.BlockSpec((R_H, c, c), lambda d, n: (d, 0, 0))
    vec = pl.BlockSpec((R_H, 1, LANES), lambda d, n: (d, 0, 0))
    return pl.pallas_call(
        body, name="ret_fwd", grid=(2, ns),
        in_specs=[blk(0), blk(0), blk(2), big, big, big, vec],
        out_specs=(pl.BlockSpec((1, cb * c, R_W), lambda d, n: (d, step(d, n), 0)),
                   pl.BlockSpec((R_H, cb, R_HD, R_HD), lambda d, n: (d, step(d, n), 0, 0))),
        out_shape=(jax.ShapeDtypeStruct((2, seq, R_W), F32),
                   jax.ShapeDtypeStruct((2 * R_H, seq // c, R_HD, R_HD), F32)),
        scratch_shapes=[pltpu.VMEM((R_H, R_HD, R_HD), F32)],
    )(rq, rk, pr, m, qd, kd, gdec)


def _ret_bwd(rq, rk, pr, dry, pst, m, mw, qd, qdw, kd, kdw, gdec, gw, cb):
    seq = rq.shape[0]
    c = CHUNK
    ns = seq // (cb * c)

    def body(q_ref, k_ref, v_ref, dy_ref, pst_ref, m_ref, mw_ref, qd_ref, qdw_ref, kd_ref, kdw_ref,
             g_ref, gw_ref, dq_ref, dk_ref, dv_ref, dlam_ref, r_sc, acc_sc, e_sc, g_sc):
        d = pl.program_id(0)
        n = pl.program_id(1)

        @pl.when(n == 0)
        def _():
            r_sc[...] = jnp.zeros_like(r_sc)
            acc_sc[...] = jnp.zeros_like(acc_sc)
            e_sc[...] = jnp.zeros_like(e_sc)
            g_sc[...] = jnp.zeros_like(g_sc)

        def chunk(j, carry):
            cc = jnp.where(d == 0, cb - 1 - j, j)
            sl = pl.ds(pl.multiple_of(cc * c, c), c)
            heads = [slice(h * R_HD, (h + 1) * R_HD) for h in range(R_H)]
            first = []
            for h, hs in enumerate(heads):
                q, k, v, dy = q_ref[sl, hs], k_ref[sl, hs], v_ref[sl, hs], dy_ref[sl, hs]
                r = r_sc[h]
                first.append((_mm_nt(q, k), _mm_nt(dy, v), _mm_nt(dy, pst_ref[h, cc]), _mm_nt(v, r),
                              _mm(k * kd_ref[h], r), _mm_tn(q * qd_ref[h], dy)))
            for h, hs in enumerate(heads):
                qk, ds, dyp, vr, kr, qdy = first[h]
                q, k, dy = q_ref[sl, hs], k_ref[sl, hs], dy_ref[sl, hs]
                r = r_sc[h]
                da = ds * m_ref[h]
                dv_ref[0, sl, hs] = _mm_tn(qk * m_ref[h], dy) + kr
                dq_ref[0, sl, hs] = _mm(da, k) + dyp * qd_ref[h]
                dk_ref[0, sl, hs] = _mm_tn(da, q) + vr * kd_ref[h]
                acc_sc[h] += dyp * q * qdw_ref[h] + vr * k * kdw_ref[h]
                e_sc[h] += ds * qk * mw_ref[h]
                g_sc[h] += r * pst_ref[h, cc]
                r_sc[h] = r * g_ref[h] + qdy
            return carry

        lax.fori_loop(0, cb, chunk, 0)

        @pl.when(n == ns - 1)
        def _():
            for h in range(R_H):
                tot = jnp.sum(jnp.sum(acc_sc[h] + e_sc[h] + g_sc[h] * gw_ref[h], axis=0, keepdims=True),
                              axis=1, keepdims=True)
                dlam_ref[h] = jnp.broadcast_to(tot, (1, LANES))

    def step(d, n):
        return d * n + (1 - d) * (ns - 1 - n)

    blk = lambda off: pl.BlockSpec((cb * c, R_W), lambda d, n: (step(d, n), off))
    big = pl.BlockSpec((R_H, c, c), lambda d, n: (d, 0, 0))
    vec = pl.BlockSpec((R_H, 1, LANES), lambda d, n: (d, 0, 0))
    out = pl.BlockSpec((1, cb * c, R_W), lambda d, n: (d, step(d, n), 0))
    oshape = jax.ShapeDtypeStruct((2, seq, R_W), F32)
    sq = pltpu.VMEM((R_H, R_HD, R_HD), F32)
    return pl.pallas_call(
        body, name="ret_bwd", grid=(2, ns),
        in_specs=[blk(0), blk(0), blk(2), blk(0),
                  pl.BlockSpec((R_H, cb, R_HD, R_HD), lambda d, n: (d, step(d, n), 0, 0)),
                  big, big, big, big, big, big, vec, vec],
        out_specs=(out, out, out, vec),
        out_shape=(oshape, oshape, oshape, jax.ShapeDtypeStruct((2 * R_H, 1, LANES), F32)),
        scratch_shapes=[sq, sq, sq, sq],
    )(rq, rk, pr, dry, pst, m, mw, qd, qdw, kd, kdw, gdec, gw)


def _group_norm(ry):
    yn, rs = [], []
    for h in range(R_H):
        s = ry[:, h * R_HD:(h + 1) * R_HD]
        mu = jnp.mean(s, axis=-1, keepdims=True)
        cen = s - mu
        r = lax.rsqrt(jnp.mean(cen * cen, axis=-1, keepdims=True) + GN_EPS)
        yn.append(cen * r)
        rs.append(r)
    return yn, rs


def _merge_fwd(x, o, y2, pr, pg, gain_r, wao, wro, wout):
    seq = x.shape[0]
    tm = _tile(seq, 256)

    def body(x_ref, o_ref, yf_ref, yb_ref, rg_ref, ga_ref, gr_ref, gn_ref, wao_ref, wro_ref, wout_ref,
             x1_ref, mg_ref, ri_ref):
        yn, _ = _group_norm(yf_ref[0] + yb_ref[0])
        rg = rg_ref[...]
        ret_in = jnp.concatenate(yn, axis=1) * gn_ref[...] * (rg * _sigmoid(rg))
        ri_ref[...] = _bf(ret_in)
        attn_out = _mm(o_ref[...], wao_ref[...])
        ret_out = _mm(ret_in, wro_ref[...])
        merged = _sigmoid(ga_ref[...]) * attn_out + _sigmoid(gr_ref[...]) * ret_out
        mg_ref[...] = _bf(merged)
        x1_ref[...] = x_ref[...] + _mm(merged, wout_ref[...])

    row = lambda w_, j=0: pl.BlockSpec((tm, w_), lambda i: (i, j))
    ydir = lambda d: pl.BlockSpec((1, tm, R_W), lambda i: (d, i, 0))
    return pl.pallas_call(
        body, name="merge_fwd", grid=(seq // tm,),
        in_specs=[row(D), row(AQ_W), ydir(0), ydir(1), row(R_W, 3), row(D, 0), row(D, 1),
                  _full((1, R_W)), _full((AQ_W, D)), _full((R_W, D)), _full((D, D))],
        out_specs=(row(D), row(D), row(R_W)),
        out_shape=(jax.ShapeDtypeStruct((seq, D), F32), jax.ShapeDtypeStruct((seq, D), _MXU),
                   jax.ShapeDtypeStruct((seq, R_W), _MXU)),
    )(x, o, y2, y2, pr, pg, pg, gain_r, wao, wro, wout)


def _mlp_fwd(x1, gain, wup, wdown):
    seq = x1.shape[0]
    tm = _tile(seq, 512)
    fc = 2048
    nfc = FF // fc

    def body(x_ref, g_ref, wu_ref, wd_ref, x2_ref, hm_sc, acc_sc):
        c = pl.program_id(1)

        @pl.when(c == 0)
        def _():
            n, _ = _rms(x_ref[...])
            hm_sc[...] = _bf(n * g_ref[...])
            acc_sc[...] = jnp.zeros_like(acc_sc)

        halves = (slice(0, fc // 2), slice(fc // 2, fc))
        ups = [jnp.maximum(_mm(hm_sc[...], wu_ref[:, hs]), 0.0) for hs in halves]
        acc_sc[...] += _mm(ups[0] * ups[0], wd_ref[halves[0], :]) + _mm(ups[1] * ups[1], wd_ref[halves[1], :])

        @pl.when(c == nfc - 1)
        def _():
            x2_ref[...] = x_ref[...] + acc_sc[...]

    return pl.pallas_call(
        body, name="mlp_fwd", grid=(seq // tm, nfc),
        in_specs=[pl.BlockSpec((tm, D), lambda i, c: (i, 0)), pl.BlockSpec((1, D), lambda i, c: (0, 0)),
                  pl.BlockSpec((D, fc), lambda i, c: (0, c)), pl.BlockSpec((fc, D), lambda i, c: (c, 0))],
        out_specs=pl.BlockSpec((tm, D), lambda i, c: (i, 0)),
        out_shape=jax.ShapeDtypeStruct((seq, D), F32),
        scratch_shapes=[pltpu.VMEM((tm, D), _MXU), pltpu.VMEM((tm, D), F32)],
    )(x1, gain, wup, wdown)


def _ple_loss(x2, p, tgt, g_ple, g_fin, wpg, wpgt, wple):
    seq = x2.shape[0]
    tm = _tile(seq, 256)

    def body(x2_ref, p_ref, t_ref, gp_ref, gf_ref, wpg_ref, wpgt_ref, wple_ref,
             dx2_ref, de_ref, dz_ref, hp_ref, loss_ref, dgf_ref, dgp_ref):
        @pl.when(pl.program_id(0) == 0)
        def _():
            loss_ref[...] = jnp.zeros_like(loss_ref)
            dgf_ref[...] = jnp.zeros_like(dgf_ref)
            dgp_ref[...] = jnp.zeros_like(dgp_ref)

        x2 = x2_ref[...]
        gp, gf = gp_ref[...], gf_ref[...]
        n2, r2 = _rms(x2)
        hp = _bf(n2 * gp)
        hp_ref[...] = hp
        gate = _sigmoid(_mm(hp, wpg_ref[...]))
        e = _mm(p_ref[...], wple_ref[...])
        x3 = x2 + gate * e
        n3, r3 = _rms(x3)
        diff = n3 * gf - t_ref[...]
        row_loss = jnp.mean(diff * diff, axis=-1, keepdims=True)
        loss_ref[...] += 0.5 * jnp.sum(row_loss, axis=0, keepdims=True)
        dy = diff * (1.0 / D)
        dgf_ref[...] += jnp.sum(dy * n3, axis=0, keepdims=True)
        dx3 = _rms_bwd(n3, r3, gf, dy)
        de_ref[...] = _bf(dx3 * gate)
        dz = dx3 * e * gate * (1.0 - gate)
        dz_ref[...] = _bf(dz)
        dhp = _mm(dz, wpgt_ref[...])
        dgp_ref[...] += jnp.sum(dhp * n2, axis=0, keepdims=True)
        dx2_ref[...] = dx3 + _rms_bwd(n2, r2, gp, dhp)

    row = lambda w_: pl.BlockSpec((tm, w_), lambda i: (i, 0))
    act = lambda dt: jax.ShapeDtypeStruct((seq, D), dt)
    return pl.pallas_call(
        body, name="ple_loss", grid=(seq // tm,),
        in_specs=[row(D), row(PLE), row(D), _full((1, D)), _full((1, D)),
                  _full((D, D)), _full((D, D)), _full((PLE, D))],
        out_specs=(row(D), row(D), row(D), row(D), _full((1, LANES)), _full((1, D)), _full((1, D))),
        out_shape=(act(F32), act(_MXU), act(_MXU), act(_MXU), jax.ShapeDtypeStruct((1, LANES), F32),
                   jax.ShapeDtypeStruct((1, D), F32), jax.ShapeDtypeStruct((1, D), F32)),
    )(x2, p, tgt, g_ple, g_fin, wpg, wpgt, wple)


def _mlp_bwd(x1, dx2, gain, wup, wdownt, wupt):
    seq = x1.shape[0]
    tm = _tile(seq, 512)
    fc = 2048
    nfc = FF // fc

    def body(x_ref, dx2_ref, g_ref, wu_ref, wdt_ref, wut_ref,
             dx1_ref, a_ref, du_ref, hm_ref, dg_ref, dhm_sc):
        i = pl.program_id(0)
        c = pl.program_id(1)

        @pl.when((i == 0) & (c == 0))
        def _():
            dg_ref[...] = jnp.zeros_like(dg_ref)

        @pl.when(c == 0)
        def _():
            n, _ = _rms(x_ref[...])
            hm_ref[...] = _bf(n * g_ref[...])
            dhm_sc[...] = jnp.zeros_like(dhm_sc)

        halves = (slice(0, fc // 2), slice(fc // 2, fc))
        ups = [jnp.maximum(_mm(hm_ref[...], wu_ref[:, hs]), 0.0) for hs in halves]
        das = [_mm(dx2_ref[...], wdt_ref[:, hs]) for hs in halves]
        part = None
        for u, da, hs in zip(ups, das, halves):
            a_ref[:, hs] = _bf(u * u)
            du = _bf(da * (2.0 * u))
            du_ref[:, hs] = du
            t = _mm(du, wut_ref[hs, :])
            part = t if part is None else part + t
        dhm_sc[...] += part

        @pl.when(c == nfc - 1)
        def _():
            n, r = _rms(x_ref[...])
            dhm = dhm_sc[...]
            dg_ref[...] += jnp.sum(dhm * n, axis=0, keepdims=True)
            dx1_ref[...] = dx2_ref[...] + _rms_bwd(n, r, g_ref[...], dhm)

    rowd = pl.BlockSpec((tm, D), lambda i, c: (i, 0))
    rowf = pl.BlockSpec((tm, fc), lambda i, c: (i, c))
    return pl.pallas_call(
        body, name="mlp_bwd", grid=(seq // tm, nfc),
        in_specs=[rowd, rowd, pl.BlockSpec((1, D), lambda i, c: (0, 0)),
                  pl.BlockSpec((D, fc), lambda i, c: (0, c)), pl.BlockSpec((D, fc), lambda i, c: (0, c)),
                  pl.BlockSpec((fc, D), lambda i, c: (c, 0))],
        out_specs=(rowd, rowf, rowf, rowd, pl.BlockSpec((1, D), lambda i, c: (0, 0))),
        out_shape=(jax.ShapeDtypeStruct((seq, D), F32), jax.ShapeDtypeStruct((seq, FF), _MXU),
                   jax.ShapeDtypeStruct((seq, FF), _MXU), jax.ShapeDtypeStruct((seq, D), _MXU),
                   jax.ShapeDtypeStruct((1, D), F32)),
        scratch_shapes=[pltpu.VMEM((tm, D), F32)],
    )(x1, dx2, gain, wup, wdownt, wupt)


def _merge_bwd(dx1, o, y2, pr, pg, gain_r, wao, wro, woutt, waot, wrot):
    seq = dx1.shape[0]
    tm = _tile(seq, 256)

    def body(dx1_ref, o_ref, yf_ref, yb_ref, rg_ref, ga_ref, gr_ref, gn_ref, wao_ref, wro_ref,
             woutt_ref, waot_ref, wrot_ref,
             dpg_ref, dao_ref, dro_ref, do_ref, dry_ref, drg_ref, dgn_ref):
        @pl.when(pl.program_id(0) == 0)
        def _():
            dgn_ref[...] = jnp.zeros_like(dgn_ref)

        yn_l, rs_l = _group_norm(yf_ref[0] + yb_ref[0])
        yn = jnp.concatenate(yn_l, axis=1)
        rg = rg_ref[...]
        gn = gn_ref[...]
        sg = _sigmoid(rg)
        sil = rg * sg
        ret_in = yn * gn * sil
        attn_out = _mm(o_ref[...], wao_ref[...])
        ret_out = _mm(ret_in, wro_ref[...])
        sa = _sigmoid(ga_ref[...])
        sr = _sigmoid(gr_ref[...])
        dm = _mm(dx1_ref[...], woutt_ref[...])
        dpg_ref[:, 0:D] = _bf(dm * attn_out * sa * (1.0 - sa))
        dpg_ref[:, D:2 * D] = _bf(dm * ret_out * sr * (1.0 - sr))
        dao = _bf(dm * sa)
        dro = _bf(dm * sr)
        dao_ref[...] = dao
        dro_ref[...] = dro
        do_ref[...] = _mm(dao, waot_ref[...])
        dri = _mm(dro, wrot_ref[...])
        dgn_ref[...] += jnp.sum(dri * yn * sil, axis=0, keepdims=True)
        drg_ref[...] = _bf(dri * yn * gn * (sg * (1.0 + rg * (1.0 - sg))))
        dyn = dri * gn * sil
        dry = []
        for h in range(R_H):
            dh = dyn[:, h * R_HD:(h + 1) * R_HD]
            dry.append(rs_l[h] * (dh - jnp.mean(dh, axis=-1, keepdims=True)
                                  - yn_l[h] * jnp.mean(dh * yn_l[h], axis=-1, keepdims=True)))
        dry_ref[...] = jnp.concatenate(dry, axis=1)

    row = lambda w_, j=0: pl.BlockSpec((tm, w_), lambda i: (i, j))
    ydir = lambda d: pl.BlockSpec((1, tm, R_W), lambda i: (d, i, 0))
    return pl.pallas_call(
        body, name="merge_bwd", grid=(seq // tm,),
        in_specs=[row(D), row(AQ_W), ydir(0), ydir(1), row(R_W, 3), row(D, 0), row(D, 1),
                  _full((1, R_W)), _full((AQ_W, D)), _full((R_W, D)), _full((D, D)),
                  _full((D, AQ_W)), _full((D, R_W))],
        out_specs=(row(PG_W), row(D), row(D), row(AQ_W), row(R_W), row(R_W), _full((1, R_W))),
        out_shape=(jax.ShapeDtypeStruct((seq, PG_W), _MXU), jax.ShapeDtypeStruct((seq, D), _MXU),
                   jax.ShapeDtypeStruct((seq, D), _MXU), jax.ShapeDtypeStruct((seq, AQ_W), F32),
                   jax.ShapeDtypeStruct((seq, R_W), F32), jax.ShapeDtypeStruct((seq, R_W), _MXU),
                   jax.ShapeDtypeStruct((1, R_W), F32)),
    )(dx1, o, y2, y2, pr, pg, pg, gain_r, wao, wro, woutt, waot, wrot)


def _qk_prep_bwd(pa, dqh, dk2, dv2, rdq, rdk, rdv, drg, gq, gk, seg, ca, sa, cr, sr):
    seq = pa.shape[0]
    tm = _tile(seq, 256)

    def body(pa_ref, dqh_ref, dk2_ref, dv2_ref, rdqf_ref, rdqb_ref, rdkf_ref, rdkb_ref, rdvf_ref, rdvb_ref,
             drg_ref, gq_ref, gk_ref, seg_ref, ca_ref, sa_ref, cr_ref, sr_ref,
             dpa_ref, dpr_ref, dgq_ref, dgk_ref):
        @pl.when(pl.program_id(0) == 0)
        def _():
            dgq_ref[...] = jnp.zeros_like(dgq_ref)
            dgk_ref[...] = jnp.zeros_like(dgk_ref)

        ca_, sa_ = ca_ref[...], sa_ref[...]

        def norm_bwd(raw, gain, dy, segm, dg_ref):
            msq = jnp.dot(raw * raw, segm, precision=HIGHEST, preferred_element_type=F32)
            r = lax.rsqrt(msq + EPS)
            n = raw * r
            dg_ref[...] += jnp.sum(dy * n, axis=0, keepdims=True)
            dn = dy * gain
            return r * (dn - n * jnp.dot(dn * n, segm, precision=HIGHEST, preferred_element_type=F32))

        dqn = _rope(dqh_ref[...] * (A_HD ** -0.5), _cat(ca_, 4), -_cat(sa_, 4), A_HD // 2)
        dpa_ref[:, 0:AQ_W] = _bf(norm_bwd(pa_ref[:, 0:AQ_W], gq_ref[...], dqn, seg_ref[...], dgq_ref))
        dkn = _rope(dk2_ref[...], ca_, -sa_, A_HD // 2)
        dpa_ref[:, AQ_W:AQ_W + AKV_W] = _bf(norm_bwd(pa_ref[:, AQ_W:AQ_W + AKV_W], gk_ref[...], dkn,
                                                     seg_ref[0:AKV_W, 0:AKV_W], dgk_ref))
        dpa_ref[:, AQ_W + AKV_W:PA_W] = _bf(dv2_ref[...])
        cr_, sr_ = _cat(cr_ref[...], 4), -_cat(sr_ref[...], 4)
        dpr_ref[:, 0:R_W] = _bf(_rope((rdqf_ref[0] + rdqb_ref[0]) * (R_HD ** -0.5), cr_, sr_, R_HD // 2))
        dpr_ref[:, R_W:2 * R_W] = _bf(_rope(rdkf_ref[0] + rdkb_ref[0], cr_, sr_, R_HD // 2))
        dpr_ref[:, 2 * R_W:3 * R_W] = _bf(rdvf_ref[0] + rdvb_ref[0])
        dpr_ref[:, 3 * R_W:4 * R_W] = drg_ref[...]

    row = lambda w_: pl.BlockSpec((tm, w_), lambda i: (i, 0))
    ydir = lambda d: pl.BlockSpec((1, tm, R_W), lambda i: (d, i, 0))
    return pl.pallas_call(
        body, name="qk_prep_bwd", grid=(seq // tm,),
        in_specs=[row(PA_W), row(AQ_W), row(AKV_W), row(AKV_W), ydir(0), ydir(1), ydir(0), ydir(1),
                  ydir(0), ydir(1), row(R_W), _full((1, AQ_W)), _full((1, AKV_W)), _full((AQ_W, AQ_W)),
                  row(LANES), row(LANES), row(LANES), row(LANES)],
        out_specs=(row(PA_W), row(PR_W), _full((1, AQ_W)), _full((1, AKV_W))),
        out_shape=(jax.ShapeDtypeStruct((seq, PA_W), _MXU), jax.ShapeDtypeStruct((seq, PR_W), _MXU),
                   jax.ShapeDtypeStruct((1, AQ_W), F32), jax.ShapeDtypeStruct((1, AKV_W), F32)),
    )(pa, dqh, dk2, dv2, rdq, rdq, rdk, rdk, rdv, rdv, drg, gq, gk, seg, ca, sa, cr, sr)


def _in_proj_bwd(x, dx1, gain, dpa, dpr, dpg, wint):
    seq = x.shape[0]
    tm = _tile(seq, 256)

    def body(x_ref, dx1_ref, g_ref, dpa_ref, dpr_ref, dpg_ref, wt_ref, dx_ref, dg_ref):
        @pl.when(pl.program_id(0) == 0)
        def _():
            dg_ref[...] = jnp.zeros_like(dg_ref)

        dh = (_mm(dpa_ref[...], wt_ref[0:PA_W, :]) + _mm(dpr_ref[...], wt_ref[PA_W:PA_W + PR_W, :])
              + _mm(dpg_ref[...], wt_ref[PA_W + PR_W:IN_W, :]))
        n, r = _rms(x_ref[...])
        dg_ref[...] += jnp.sum(dh * n, axis=0, keepdims=True)
        dx_ref[...] = dx1_ref[...] + _rms_bwd(n, r, g_ref[...], dh)

    row = lambda w_: pl.BlockSpec((tm, w_), lambda i: (i, 0))
    return pl.pallas_call(
        body, name="in_proj_bwd", grid=(seq // tm,),
        in_specs=[row(D), row(D), _full((1, D)), row(PA_W), row(PR_W), row(PG_W), _full((IN_W, D))],
        out_specs=(row(D), _full((1, D))),
        out_shape=(jax.ShapeDtypeStruct((seq, D), F32), jax.ShapeDtypeStruct((1, D), F32)),
    )(x, dx1, gain, dpa, dpr, dpg, wint)


def _wgrad(a, b, name):
    seq, m = a.shape
    n = b.shape[1]
    tm, tn, ts = _tile(m, 1024), _tile(n, 1024), _tile(seq, 2048)
    ns = seq // ts

    def body(a_ref, b_ref, o_ref):
        @pl.when(pl.program_id(2) == 0)
        def _():
            o_ref[...] = jnp.zeros_like(o_ref)

        o_ref[...] += _mm_tn(a_ref[...], b_ref[...])

    return pl.pallas_call(
        body, name=name, grid=(m // tm, n // tn, ns),
        in_specs=[pl.BlockSpec((ts, tm), lambda i, j, s: (s, i)), pl.BlockSpec((ts, tn), lambda i, j, s: (s, j))],
        out_specs=pl.BlockSpec((tm, tn), lambda i, j, s: (i, j)),
        out_shape=jax.ShapeDtypeStruct((m, n), F32),
    )(a, b)


def _adamw_math(w, g, m, v):
    m = B1 * m + (1.0 - B1) * g
    v = B2 * v + (1.0 - B2) * (g * g)
    m_hat = m / (1.0 - B1 ** STEP)
    v_hat = v / (1.0 - B2 ** STEP)
    delta = -LR * (m_hat / (jnp.sqrt(v_hat) + ADAM_EPS) + WD * w)
    return delta, m, v


def _adamw_big(land, own, w, m, v, name):
    rws, cols = w.shape
    tr = next(t for t in range(min(rws, 288), 0, -8) if rws % t == 0)

    def body(l_ref, o_ref, w_ref, m_ref, v_ref, g_ref, d_ref, nm_ref, nv_ref):
        x, y, c = _mesh_pos()
        me = 4 * x + 2 * y + c
        g = o_ref[...]
        for j in range(N_DEV):
            g = g + jnp.where(me == j, 0.0, l_ref[j].astype(F32))
        g_ref[...] = g
        d_ref[...], nm_ref[...], nv_ref[...] = _adamw_math(w_ref[...], g, m_ref[...], v_ref[...])

    row = pl.BlockSpec((tr, cols), lambda i: (i, 0))
    shp = jax.ShapeDtypeStruct((rws, cols), F32)
    return pl.pallas_call(
        body, name=name, grid=(rws // tr,),
        in_specs=[pl.BlockSpec((N_DEV, tr, cols), lambda i: (0, i, 0)), row, row, row, row],
        out_specs=(row, row, row, row), out_shape=(shp, shp, shp, shp),
    )(land, own, w, m, v)


def _adamw_small(sland, w, m, v):
    def body(l_ref, w_ref, m_ref, v_ref, g_ref, d_ref, nm_ref, nv_ref, loss_ref):
        s = l_ref[0]
        for j in range(1, N_DEV):
            s = s + l_ref[j]
        w = w_ref[...]
        gq = s[8:9]
        for h in range(1, A_H):
            gq = gq + s[8 + h:9 + h]
        gk = s[16:17] + s[17:18]
        gdec = s[5:6] * _sigmoid(-w[5:6])
        g = jnp.concatenate([s[0:5], gdec, gq, gk], axis=0)
        g_ref[...] = g
        d_ref[...], nm_ref[...], nv_ref[...] = _adamw_math(w, g, m_ref[...], v_ref[...])
        loss_ref[...] = s[6:7, 0:LANES]

    shp = jax.ShapeDtypeStruct((8, PACK_COLS), F32)
    return pl.pallas_call(
        body, name="adamw_small",
        out_shape=(shp, shp, shp, shp, jax.ShapeDtypeStruct((1, LANES), F32)),
    )(sland, w, m, v)


_BIG = (("w_attn_o", AQ_W, D, 1), ("w_ret_o", R_W, D, 1), ("w_out", D, D, 0),
        ("w_up", D, FF, 1), ("w_down", FF, D, 0), ("w_ple_gate", D, D, 0), ("w_ple", PLE, D, 1))
IN_SHARD = IN_W // N_DEV
_SMALL = ("mix_norm", "mlp_norm", "ple_norm", "final_norm", "ret_norm_gain", "ret_decay_logit",
          "attn_q_norm", "attn_k_norm")


def _shard_shape(rows, cols, axis):
    return (rows // N_DEV, cols) if axis == 0 else (rows, cols // N_DEV)


def _pack_shards(shards):
    flat = jnp.concatenate([s.reshape(-1) for s in shards])
    return flat.reshape(-1, PACK_COLS)


def _unpack_gathered(gathered):
    flat = gathered.reshape(N_DEV, -1)
    out, off = {}, 0
    for name, rows, cols, axis in _BIG:
        sr, sc = _shard_shape(rows, cols, axis)
        blk = flat[:, off:off + sr * sc].reshape(N_DEV, sr, sc)
        off += sr * sc
        out[name] = blk.reshape(rows, cols) if axis == 0 else blk.transpose(1, 0, 2).reshape(rows, cols)
    return out


def _pack_full_grads(grads):
    parts = []
    for name, rows, cols, axis in _BIG:
        sr, sc = _shard_shape(rows, cols, axis)
        g = grads[name]
        blk = g.reshape(N_DEV, sr, sc) if axis == 0 else g.reshape(rows, N_DEV, sc).transpose(1, 0, 2)
        parts.append(blk.reshape(N_DEV, -1))
    flat = jnp.concatenate(parts, axis=1)
    return flat.reshape(N_DEV, -1, PACK_COLS)


def _unpack_shard(packed):
    flat = packed.reshape(-1)
    out, off = {}, 0
    for name, rows, cols, axis in _BIG:
        sr, sc = _shard_shape(rows, cols, axis)
        out[name] = flat[off:off + sr * sc].reshape(1, sr, sc)
        off += sr * sc
    return out


def _pack_small(vals):
    rows = [jnp.pad(vals[n].reshape(-1), (0, PACK_COLS - vals[n].size)) for n in _SMALL]
    return jnp.stack(rows)


def _unpack_small(packed, like):
    return {n: packed[i, :like[n].size].reshape(like[n].shape) for i, n in enumerate(_SMALL)}


def _row(v):
    return jnp.pad(v.reshape(-1), (0, PACK_COLS - v.size))


def kernel(x, p, mix_norm, w_in, attn_q_norm, attn_k_norm, ret_decay_logit, ret_norm_gain, w_attn_o, w_ret_o, w_out, mlp_norm, w_up, w_down, ple_norm, w_ple_gate, w_ple, final_norm, loss_target, m_mix_norm, m_w_in, m_attn_q_norm, m_attn_k_norm, m_ret_decay_logit, m_ret_norm_gain, m_w_attn_o, m_w_ret_o, m_w_out, m_mlp_norm, m_w_up, m_w_down, m_ple_norm, m_w_ple_gate, m_w_ple, m_final_norm, v_mix_norm, v_w_in, v_attn_q_norm, v_attn_k_norm, v_ret_decay_logit, v_ret_norm_gain, v_w_attn_o, v_w_ret_o, v_w_out, v_mlp_norm, v_w_up, v_w_down, v_ple_norm, v_w_ple_gate, v_w_ple, v_final_norm):
    args = dict(locals())
    seq = x.shape[1]
    xs = x[0]
    ps = p[0, 0]
    tgt = loss_target[0]

    big_names = [b[0] for b in _BIG]
    wshard = _pack_shards([args[n] for n in big_names])
    win_g, rest_g = _all_gather([w_in[0].astype(_MXU), wshard.astype(_MXU)])
    win = win_g.transpose(1, 0, 2).reshape(D, IN_W)
    wfull = _unpack_gathered(rest_g)
    wao, wro, wout = wfull["w_attn_o"], wfull["w_ret_o"], wfull["w_out"]
    wup, wdown, wpg, wple = wfull["w_up"], wfull["w_down"], wfull["w_ple_gate"], wfull["w_ple"]

    g_mix, g_mlp, g_ple = mix_norm, mlp_norm, ple_norm
    g_fin = final_norm.reshape(1, D)
    gq = jnp.tile(attn_q_norm, (1, A_H))
    gk = jnp.tile(attn_k_norm, (1, A_KV))
    seg = _seg_mean_matrix()
    ca, sa, cr, sr = _rope_tables(seq)

    pa, pr, pg, h = _in_proj(xs, g_mix, win)
    qh, kh, vh, rqh, rkh = _qk_prep(pa, pr, gq, gk, seg, ca, sa, cr, sr)

    tq = _tile(seq, 128)
    tk = _tile(seq // 4, 1024)
    qt8 = qh.reshape(seq, A_H, A_HD).transpose(1, 2, 0)
    vta = jnp.stack([jnp.concatenate([_chunk_t(vh[:, g * A_HD:(g + 1) * A_HD], tk),
                                      jnp.ones((seq // tk, 16, tk), _MXU)], axis=1) for g in range(A_KV)])
    ot, lse = _attn_fwd(qt8, kh, vta, tq, tk)
    o = _heads_to_rows(ot)

    zb = jnp.broadcast_to(ret_decay_logit.reshape(2 * R_H, 1, 1), (2 * R_H, 1, LANES))
    tm_, tmw, tqd, tqdw, tkd, tkdw, tg, tgw = _ret_tables(zb)
    cb = _tile(seq // CHUNK, 8)
    y2, pst = _ret_fwd(rqh, rkh, pr, tm_, tqd, tkd, tg, cb)

    x1, merged, ret_in = _merge_fwd(xs, o, y2, pr, pg, ret_norm_gain, wao, wro, wout)
    x2 = _mlp_fwd(x1, g_mlp, wup, wdown)

    dx2, de, dz, hp, loss_p, dg_fin, dg_ple = _ple_loss(x2, ps, tgt, g_ple, g_fin, wpg, wpg.T, wple)
    dx1, act, du, hm, dg_mlp = _mlp_bwd(x1, dx2, g_mlp, wup, wdown.T, wup.T)
    dpg, dao, dro, do, dry, drg, dg_gn = _merge_bwd(dx1, o, y2, pr, pg, ret_norm_gain, wao, wro,
                                                    wout.T, wao.T, wro.T)
    rdq, rdk, rdv, dlam = _ret_bwd(rqh, rkh, pr, dry, pst, tm_, tmw, tqd, tqdw, tkd, tkdw, tg, tgw, cb)

    ksplit = 1
    tkb = _tile(seq // 4, 512)
    dot_ = do.reshape(seq, A_H, A_HD).transpose(1, 2, 0)
    dkt, dvt, dqt = _attn_bwd(qt8, ot, dot_, lse, kh, vh, _chunk_t(kh, tkb), tq, tkb, ksplit)
    dqh = _heads_to_rows(jnp.sum(dqt, axis=0))
    dpa, dpr, dg_q, dg_k = _qk_prep_bwd(pa, dqh, _chunks_to_rows(dkt), _chunks_to_rows(dvt), rdq, rdk, rdv, drg, gq, gk, seg, ca, sa, cr, sr)
    grad_x, dg_mix = _in_proj_bwd(xs, dx1, g_mix, dpa, dpr, dpg, win.T)

    wg = {
        "w_in": jnp.concatenate([_wgrad(h, dpa, "wgrad_in_a"), _wgrad(h, dpr, "wgrad_in_r"),
                                 _wgrad(h, dpg, "wgrad_in_g")], axis=1),
        "w_attn_o": _wgrad(o, dao, "wgrad_attn_o"),
        "w_ret_o": _wgrad(ret_in, dro, "wgrad_ret_o"),
        "w_out": _wgrad(merged, dx1, "wgrad_out"),
        "w_up": _wgrad(hm, du, "wgrad_up"),
        "w_down": _wgrad(act, dx2, "wgrad_down"),
        "w_ple_gate": _wgrad(hp, dz, "wgrad_ple_gate"),
        "w_ple": _wgrad(ps, de, "wgrad_ple"),
    }
    gpack = _pack_full_grads(wg)
    gpack_in = wg["w_in"].reshape(D, N_DEV, IN_SHARD).transpose(1, 0, 2)
    small = jnp.stack(
        [_row(dg_mix), _row(dg_mlp), _row(dg_ple), _row(dg_fin), _row(dg_gn), _row(dlam[:, 0, 0]),
         _row(loss_p[0, 0:1]), jnp.zeros((PACK_COLS,), F32)]
        + [_row(dg_q[0, hh * A_HD:(hh + 1) * A_HD]) for hh in range(A_H)]
        + [_row(dg_k[0, hh * A_HD:(hh + 1) * A_HD]) for hh in range(A_KV)]
        + [jnp.zeros((PACK_COLS,), F32)] * (SMALL_ROWS - 18))

    me = 4 * lax.axis_index("x") + 2 * lax.axis_index("y") + lax.axis_index("c")
    own_in = lax.dynamic_index_in_dim(gpack_in, me, axis=0, keepdims=False)
    own = lax.dynamic_index_in_dim(gpack, me, axis=0, keepdims=False)
    land_in, land, sland = _exchange_grads([_bf(gpack_in), _bf(gpack), small[None]])
    in_sh = _adamw_big(land_in, own_in, w_in[0], m_w_in[0], v_w_in[0], "adamw_w_in")
    g_sh, d_sh, m_sh, v_sh = _adamw_big(land, own, wshard, _pack_shards([args["m_" + n] for n in big_names]),
                                        _pack_shards([args["v_" + n] for n in big_names]), "adamw_shard")
    g_sm, d_sm, m_sm, v_sm, loss_row = _adamw_small(
        sland, _pack_small({n: args[n] for n in _SMALL}), _pack_small({n: args["m_" + n] for n in _SMALL}),
        _pack_small({n: args["v_" + n] for n in _SMALL}))

    names = ["mix_norm", "w_in", "attn_q_norm", "attn_k_norm", "ret_decay_logit", "ret_norm_gain", "w_attn_o",
             "w_ret_o", "w_out", "mlp_norm", "w_up", "w_down", "ple_norm", "w_ple_gate", "w_ple", "final_norm"]
    like = {n: args[n] for n in _SMALL}
    outs = [loss_row[0, 0], grad_x[None]]
    for big, sm, w_in_part in ((g_sh, g_sm, in_sh[0]), (d_sh, d_sm, in_sh[1]), (m_sh, m_sm, in_sh[2]),
                               (v_sh, v_sm, in_sh[3])):
        table = {**_unpack_shard(big), **_unpack_small(sm, like), "w_in": w_in_part[None]}
        outs += [table[n] for n in names]
    return tuple(outs)
```

```python
import functools

import jax
import jax.numpy as jnp
from jax import lax
from jax.experimental import pallas as pl
from jax.experimental.pallas import tpu as pltpu

F32 = jnp.float32
_MXU = jnp.bfloat16

D = 1024
PLE = 256
GRID_W = 64
A_HD = 64
A_H = 8
A_KV = 2
A_G = A_H // A_KV
AQ_W = A_H * A_HD
AKV_W = A_KV * A_HD
R_HD = 128
R_H = 4
R_W = R_H * R_HD
IN_W = AQ_W + 2 * AKV_W + 4 * R_W + 2 * D
PA_W = AQ_W + 2 * AKV_W
PR_W = 4 * R_W
PG_W = 2 * D
FF = 4 * D
CHUNK = 128
ROPE_THETA = 10000.0
EPS = 1e-6
GN_EPS = 1e-5
N_DEV = 8

LR, B1, B2, ADAM_EPS, WD, STEP = 0.001, 0.9, 0.999, 1e-08, 0.01, 10

LANES = 128
PACK_COLS = 1024
SMALL_ROWS = 24
HIGHEST = lax.Precision.HIGHEST


def _tile(n, pref):
    t = min(n, pref)
    assert n % t == 0, (n, t)
    return t


def _bf(a):
    return a.astype(_MXU)


def _mm(a, b):
    return jnp.dot(_bf(a), _bf(b), preferred_element_type=F32)


def _mm_nt(a, b):
    return lax.dot_general(_bf(a), _bf(b), (((1,), (1,)), ((), ())), preferred_element_type=F32)


def _mm_tn(a, b):
    return lax.dot_general(_bf(a), _bf(b), (((0,), (0,)), ((), ())), preferred_element_type=F32)


def _sigmoid(z):
    return 1.0 / (1.0 + jnp.exp(-z))


def _rms(x):
    r = lax.rsqrt(jnp.mean(x * x, axis=-1, keepdims=True) + EPS)
    return x * r, r


def _rms_bwd(n, r, gain, dy):
    dn = dy * gain
    return r * (dn - n * jnp.mean(dn * n, axis=-1, keepdims=True))


def _swap_halves(x, half):
    n = x.shape[-1]
    lane = lax.broadcasted_iota(jnp.int32, x.shape, x.ndim - 1)
    first = (lane % (2 * half)) < half
    return jnp.where(first, pltpu.roll(x, n - half, axis=1), pltpu.roll(x, half, axis=1))


def _rope(x, cos, sin, half):
    return x * cos + _swap_halves(x, half) * sin


def _cat(t, reps):
    return jnp.concatenate([t] * reps, axis=1)


def _full(shape):
    nd = len(shape)
    return pl.BlockSpec(shape, lambda *_: (0,) * nd)


def _rope_tables(seq):
    def tab(head_dim):
        n_axis = head_dim // 4
        freqs = ROPE_THETA ** (-jnp.arange(n_axis, dtype=F32) / n_axis)
        rows = seq // GRID_W
        row = jnp.repeat(jnp.arange(rows, dtype=F32), GRID_W)
        col = jnp.tile(jnp.arange(GRID_W, dtype=F32), rows)
        ang = jnp.concatenate([row[:, None] * freqs, col[:, None] * freqs], axis=-1)
        c, s = jnp.cos(ang), jnp.sin(ang)
        return jnp.concatenate([c, c], axis=-1), jnp.concatenate([-s, s], axis=-1)
    ca, sa = tab(A_HD)
    cr, sr = tab(R_HD)
    return jnp.tile(ca, (1, 2)), jnp.tile(sa, (1, 2)), cr, sr


def _seg_mean_matrix():
    i = jnp.arange(AQ_W) // A_HD
    return (i[:, None] == i[None, :]).astype(F32) / A_HD


def _mesh_pos():
    return lax.axis_index("x"), lax.axis_index("y"), lax.axis_index("c")


def _gather_steps(x_ref, out_ref, send_sems, recv_sems, local_sem, base=0):
    x, y, c = _mesh_pos()
    me, sibling = (x, y, c), (x, y, 1 - c)
    chips = [(1 - x, y), (x, 1 - y), (1 - x, 1 - y)]

    def slot(px, py, pc):
        return out_ref.at[4 * px + 2 * py + pc]

    def copy(k, block, to, src=None):
        return pltpu.make_async_remote_copy(
            src_ref=slot(*block) if src is None else src, dst_ref=slot(*block),
            send_sem=send_sems.at[base + k], recv_sem=recv_sems.at[base + k],
            device_id=to, device_id_type=pl.DeviceIdType.MESH)

    mine = pltpu.make_async_copy(x_ref, slot(*me), local_sem)
    first = [copy(0, me, sibling, src=x_ref)]
    first += [copy(1 + j, me, (*chip, c), src=x_ref) for j, chip in enumerate(chips)]
    passed = [copy(4 + j, (*chip, c), sibling) for j, chip in enumerate(chips)]

    def start():
        mine.start()
        for cp in first:
            cp.start()

    def finish():
        for j, chip in enumerate(chips):
            copy(1 + j, (*chip, c), me).wait_recv()
            passed[j].start()
        copy(0, sibling, me).wait_recv()
        for j, chip in enumerate(chips):
            copy(4 + j, (*chip, 1 - c), me).wait_recv()
        for cp in first + passed:
            cp.wait_send()
        mine.wait()

    return start, finish


def _exchange_steps(g_ref, land_ref, send_sems, recv_sems, local_sem, per_device, base=0):
    x, y, c = _mesh_pos()
    me = 4 * x + 2 * y + c

    def row(j):
        return g_ref.at[j if per_device else 0]

    def peer(k):
        p = (x ^ ((k >> 2) & 1), y ^ ((k >> 1) & 1), c ^ (k & 1))
        return p, 4 * p[0] + 2 * p[1] + p[2]

    def copy(k, src, dst):
        return pltpu.make_async_remote_copy(
            src_ref=src, dst_ref=dst, send_sem=send_sems.at[base + k - 1], recv_sem=recv_sems.at[base + k - 1],
            device_id=peer(k)[0], device_id_type=pl.DeviceIdType.MESH)

    own = pltpu.make_async_copy(row(me), land_ref.at[me], local_sem)
    sends = [copy(k, row(peer(k)[1]), land_ref.at[me]) for k in range(1, N_DEV)]

    def start():
        own.start()
        for cp in sends:
            cp.start()

    def finish():
        for k in range(1, N_DEV):
            copy(k, row(me), land_ref.at[peer(k)[1]]).wait_recv()
        for cp in sends:
            cp.wait_send()
        own.wait()

    return start, finish


_COMM_SCRATCH = [pltpu.SemaphoreType.DMA((7,)), pltpu.SemaphoreType.DMA((7,)), pltpu.SemaphoreType.DMA]
_ANY = pl.BlockSpec(memory_space=pl.ANY)


def _all_gather(shard):
    def body(x_ref, out_ref, send_sems, recv_sems, local_sem):
        start, finish = _gather_steps(x_ref, out_ref, send_sems, recv_sems, local_sem)
        start()
        finish()

    return pl.pallas_call(
        body, name="all_gather_weights", out_shape=jax.ShapeDtypeStruct((N_DEV,) + shard.shape, shard.dtype),
        in_specs=[_ANY], out_specs=_ANY, scratch_shapes=list(_COMM_SCRATCH),
    )(shard)


def _exchange_grads(packs):
    n = len(packs)

    def body(*refs):
        g_refs, land_refs = refs[:n], refs[n:2 * n]
        send_sems, recv_sems, local_sems = refs[2 * n:]
        steps = [_exchange_steps(g_refs[t], land_refs[t], send_sems, recv_sems, local_sems.at[t],
                                 packs[t].shape[0] == N_DEV, base=7 * t) for t in range(n)]
        for start, _ in steps:
            start()
        for _, finish in steps:
            finish()

    return pl.pallas_call(
        body, name="exchange_grads",
        out_shape=tuple(jax.ShapeDtypeStruct((N_DEV,) + g.shape[1:], g.dtype) for g in packs),
        in_specs=[_ANY] * n, out_specs=(_ANY,) * n,
        scratch_shapes=[pltpu.SemaphoreType.DMA((7 * n,)), pltpu.SemaphoreType.DMA((7 * n,)),
                        pltpu.SemaphoreType.DMA((n,))],
    )(*packs)


def _in_proj(x, gain, w, rest):
    seq = x.shape[0]
    tm = _tile(seq, 256)
    nsteps = seq // tm

    def body(x_ref, g_ref, w_ref, rest_ref, pa_ref, pr_ref, pg_ref, h_ref, gath_ref, send_sems, recv_sems, local_sem):
        start, finish = _gather_steps(rest_ref, gath_ref, send_sems, recv_sems, local_sem)
        pl.when(pl.program_id(0) == 0)(start)
        n, _ = _rms(x_ref[...])
        h = _bf(n * g_ref[...])
        h_ref[...] = h
        pa_ref[...] = _mm(h, w_ref[:, 0:PA_W])
        pr_ref[...] = _mm(h, w_ref[:, PA_W:PA_W + PR_W])
        pg_ref[...] = _mm(h, w_ref[:, PA_W + PR_W:IN_W])
        pl.when(pl.program_id(0) == nsteps - 1)(finish)

    row = lambda w_: pl.BlockSpec((tm, w_), lambda i: (i, 0))
    return pl.pallas_call(
        body, name="in_proj", grid=(nsteps,),
        in_specs=[row(D), _full((1, D)), _full((D, IN_W)), _ANY],
        out_specs=(row(PA_W), row(PR_W), row(PG_W), row(D), _ANY),
        out_shape=(jax.ShapeDtypeStruct((seq, PA_W), F32), jax.ShapeDtypeStruct((seq, PR_W), F32),
                   jax.ShapeDtypeStruct((seq, PG_W), F32), jax.ShapeDtypeStruct((seq, D), _MXU),
                   jax.ShapeDtypeStruct((N_DEV,) + rest.shape, rest.dtype)),
        scratch_shapes=list(_COMM_SCRATCH),
    )(x, gain, w, rest)


def _qk_prep(pa, pr, gq, gk, seg, ca, sa, cr, sr):
    seq = pa.shape[0]
    tm = _tile(seq, 256)

    def body(pa_ref, pr_ref, gq_ref, gk_ref, seg_ref, ca_ref, sa_ref, cr_ref, sr_ref,
             qh_ref, kh_ref, v_ref, rq_ref, rk_ref):
        q = pa_ref[:, 0:AQ_W]
        k = pa_ref[:, AQ_W:AQ_W + AKV_W]
        v_ref[...] = _bf(pa_ref[:, AQ_W + AKV_W:PA_W])
        ca_, sa_ = ca_ref[...], sa_ref[...]
        msq = jnp.dot(q * q, seg_ref[...], precision=HIGHEST, preferred_element_type=F32)
        qn = q * lax.rsqrt(msq + EPS) * gq_ref[...]
        qh_ref[...] = _bf(_rope(qn, _cat(ca_, 4), _cat(sa_, 4), A_HD // 2) * (A_HD ** -0.5))
        msk = jnp.dot(k * k, seg_ref[0:AKV_W, 0:AKV_W], precision=HIGHEST, preferred_element_type=F32)
        kn = k * lax.rsqrt(msk + EPS) * gk_ref[...]
        kh_ref[...] = _bf(_rope(kn, ca_, sa_, A_HD // 2))
        cr_, sr_ = _cat(cr_ref[...], 4), _cat(sr_ref[...], 4)
        rq_ref[...] = _rope(pr_ref[:, 0:R_W], cr_, sr_, R_HD // 2) * (R_HD ** -0.5)
        rk_ref[...] = _rope(pr_ref[:, R_W:2 * R_W], cr_, sr_, R_HD // 2)

    row = lambda w_: pl.BlockSpec((tm, w_), lambda i: (i, 0))
    return pl.pallas_call(
        body, name="qk_prep", grid=(seq // tm,),
        in_specs=[row(PA_W), row(2 * R_W), _full((1, AQ_W)), _full((1, AKV_W)), _full((AQ_W, AQ_W)),
                  row(LANES), row(LANES), row(LANES), row(LANES)],
        out_specs=(row(AQ_W), row(AKV_W), row(AKV_W), row(R_W), row(R_W)),
        out_shape=(jax.ShapeDtypeStruct((seq, AQ_W), _MXU), jax.ShapeDtypeStruct((seq, AKV_W), _MXU),
                   jax.ShapeDtypeStruct((seq, AKV_W), _MXU), jax.ShapeDtypeStruct((seq, R_W), F32),
                   jax.ShapeDtypeStruct((seq, R_W), F32)),
    )(pa, pr, gq, gk, seg, ca, sa, cr, sr)


def _chunk_t(a, tk):
    seq = a.shape[0]
    return a.reshape(seq // tk, tk, a.shape[1]).transpose(0, 2, 1)


def _heads_to_rows(t):
    return t.transpose(2, 0, 1).reshape(t.shape[2], AQ_W)


def _attn_fwd(qt8, k2, vta, tq, tk):
    seq = k2.shape[0]
    nck = seq // tk
    rows = A_G * tq
    vrows = vta.shape[2]
    rb = _tile(tk, 64)
    assert nck % 2 == 0, nck

    def body(qt_ref, k_ref, vt_ref, o_ref, lse_ref, m_sc, acc_sc, qtp_sc, s_a, s_b, p_a, p_b, al_a, al_b):
        g = pl.program_id(0)
        qtp_sc[...] = jnp.zeros_like(qtp_sc)
        qtp_sc[pl.ds(pl.multiple_of(g * A_HD, A_HD), A_HD), :] = jnp.concatenate(
            [qt_ref[a] for a in range(A_G)], axis=1)
        m_sc[...] = jnp.full((1, rows), -jnp.inf, F32)
        acc_sc[...] = jnp.zeros_like(acc_sc)

        def scores(c):
            kc = k_ref[pl.ds(pl.multiple_of(c * tk, tk), tk), :]
            return _mm(kc, qtp_sc[...])

        def stage(c, s_cur, s_nxt, p_cur, p_prv, al_cur, al_prv, first=False, last=False):
            if not last:
                s_nxt[...] = scores(c + 1)
            if not first:
                acc_sc[...] = al_prv[...] * acc_sc[...] + _mm(vt_ref[0, c - 1], p_prv[...])
            m_old = m_sc[...]
            mx = None
            for r in range(0, tk, rb):
                bm = jnp.max(s_cur[r:r + rb, :].reshape(rb // 8, 8, rows), axis=0)
                mx = bm if mx is None else jnp.maximum(mx, bm)
            m_new = jnp.maximum(m_old, jnp.max(mx, axis=0, keepdims=True))
            for r in range(0, tk, rb):
                p_cur[r:r + rb, :] = _bf(jnp.exp(s_cur[r:r + rb, :] - m_new))
            al_cur[...] = jnp.exp(m_old - m_new)
            m_sc[...] = m_new

        s_a[...] = scores(0)
        stage(0, s_a, s_b, p_a, p_b, al_a, al_b, first=True)

        def pair(j, carry):
            stage(2 * j + 1, s_b, s_a, p_b, p_a, al_b, al_a)
            stage(2 * j + 2, s_a, s_b, p_a, p_b, al_a, al_b)
            return carry

        lax.fori_loop(0, nck // 2 - 1, pair, 0)
        stage(nck - 1, s_b, s_a, p_b, p_a, al_b, al_a, last=True)
        acc = al_b[...] * acc_sc[...] + _mm(vt_ref[0, nck - 1], p_b[...])
        l = acc[A_HD:A_HD + 1, :]
        lse = m_sc[...] + jnp.log(l)
        out = acc[0:A_HD, :] * (1.0 / l)
        for a in range(A_G):
            o_ref[a] = out[:, a * tq:(a + 1) * tq]
            lse_ref[a] = lse[:, a * tq:(a + 1) * tq]

    return pl.pallas_call(
        body, name="attn_fwd", grid=(A_KV, seq // tq),
        in_specs=[pl.BlockSpec((A_G, A_HD, tq), lambda g, i: (g, 0, i)),
                  _full((seq, LANES)), pl.BlockSpec((1, nck, vrows, tk), lambda g, i: (g, 0, 0, 0))],
        out_specs=(pl.BlockSpec((A_G, A_HD, tq), lambda g, i: (g, 0, i)),
                   pl.BlockSpec((A_G, 1, tq), lambda g, i: (g, 0, i))),
        out_shape=(jax.ShapeDtypeStruct((A_H, A_HD, seq), F32), jax.ShapeDtypeStruct((A_H, 1, seq), F32)),
        scratch_shapes=[pltpu.VMEM((1, rows), F32), pltpu.VMEM((vrows, rows), F32), pltpu.VMEM((LANES, rows), _MXU),
                        pltpu.VMEM((tk, rows), F32), pltpu.VMEM((tk, rows), F32),
                        pltpu.VMEM((tk, rows), _MXU), pltpu.VMEM((tk, rows), _MXU),
                        pltpu.VMEM((1, rows), F32), pltpu.VMEM((1, rows), F32)],
    )(qt8, k2, vta)


def _attn_bwd(qt8, ot, dot_, lse, k2, v2, k2t, tq, tk, ksplit):
    seq = k2.shape[0]
    sh = seq // ksplit
    nck = sh // tk
    rows = A_G * tq
    rb = _tile(tk, 32768 // rows)
    assert nck % 2 == 0, nck

    def body(qt_ref, ot_ref, dot_ref, lse_ref, k_ref, v_ref, kt_ref,
             dk_ref, dv_ref, dq_ref, dq_sc, qtp_sc, dotp_sc, pt_sc, dst_sc,
             s_a, s_b, dp_a, dp_b, p_a, p_b, ds_a, ds_b):
        g = pl.program_id(1)
        hrows = pl.ds(pl.multiple_of(g * A_HD, A_HD), A_HD)

        @pl.when(pl.program_id(2) == 0)
        def _():
            dk_ref[...] = jnp.zeros_like(dk_ref)
            dv_ref[...] = jnp.zeros_like(dv_ref)

        lse_row = jnp.concatenate([lse_ref[a] for a in range(A_G)], axis=1)
        dd = jnp.concatenate([jnp.sum(ot_ref[a] * dot_ref[a], axis=0, keepdims=True)
                              for a in range(A_G)], axis=1)
        qtp_sc[...] = jnp.zeros_like(qtp_sc)
        dotp_sc[...] = jnp.zeros_like(dotp_sc)
        qtp_sc[hrows, :] = jnp.concatenate([qt_ref[a] for a in range(A_G)], axis=1)
        dotp_sc[hrows, :] = _bf(jnp.concatenate([dot_ref[a] for a in range(A_G)], axis=1))
        dq_sc[...] = jnp.zeros_like(dq_sc)

        def products(c, s_ref, dp_ref):
            sl = pl.ds(pl.multiple_of(c * tk, tk), tk)
            s_ref[...] = _mm(k_ref[sl, :], qtp_sc[...])
            dp_ref[...] = _mm(v_ref[sl, :], dotp_sc[...])

        def accumulate(c, p_ref, ds_ref):
            pt_sc[...] = p_ref[...].T
            dst_sc[...] = ds_ref[...].T
            dq_sc[...] += _mm(kt_ref[c, hrows, :], ds_ref[...])
            dv_ref[0, c] += _mm(dotp_sc[hrows, :], pt_sc[...])
            dk_ref[0, c] += _mm(qtp_sc[hrows, :], dst_sc[...])

        def stage(c, s_cur, dp_cur, s_nxt, dp_nxt, p_cur, ds_cur, p_prv, ds_prv, first=False, last=False):
            if not last:
                products(c + 1, s_nxt, dp_nxt)
            if not first:
                accumulate(c - 1, p_prv, ds_prv)
            for r in range(0, tk, rb):
                p = jnp.exp(s_cur[r:r + rb, :] - lse_row)
                p_cur[r:r + rb, :] = _bf(p)
                ds_cur[r:r + rb, :] = _bf(p * (dp_cur[r:r + rb, :] - dd))

        products(0, s_a, dp_a)
        stage(0, s_a, dp_a, s_b, dp_b, p_a, ds_a, p_b, ds_b, first=True)

        def pair(j, carry):
            stage(2 * j + 1, s_b, dp_b, s_a, dp_a, p_b, ds_b, p_a, ds_a)
            stage(2 * j + 2, s_a, dp_a, s_b, dp_b, p_a, ds_a, p_b, ds_b)
            return carry

        lax.fori_loop(0, nck // 2 - 1, pair, 0)
        stage(nck - 1, s_b, dp_b, s_a, dp_a, p_b, ds_b, p_a, ds_a, last=True)
        accumulate(nck - 1, p_b, ds_b)
        for a in range(A_G):
            dq_ref[0, a] = dq_sc[:, a * tq:(a + 1) * tq]

    tspec = pl.BlockSpec((A_G, A_HD, tq), lambda s, g, i: (g, 0, i))
    kspec = pl.BlockSpec((sh, LANES), lambda s, g, i: (s, 0))
    gspec = pl.BlockSpec((1, nck, A_HD, tk), lambda s, g, i: (g, s, 0, 0))
    gshape = jax.ShapeDtypeStruct((A_KV, seq // tk, A_HD, tk), F32)
    big = lambda dt: pltpu.VMEM((tk, rows), dt)
    bigt = pltpu.VMEM((rows, tk), _MXU)
    return pl.pallas_call(
        body, name="attn_bwd", grid=(ksplit, A_KV, seq // tq),
        in_specs=[tspec, tspec, tspec, pl.BlockSpec((A_G, 1, tq), lambda s, g, i: (g, 0, i)),
                  kspec, kspec, pl.BlockSpec((nck, LANES, tk), lambda s, g, i: (s, 0, 0))],
        out_specs=(gspec, gspec, pl.BlockSpec((1, A_G, A_HD, tq), lambda s, g, i: (s, g, 0, i))),
        out_shape=(gshape, gshape, jax.ShapeDtypeStruct((ksplit, A_H, A_HD, seq), F32)),
        scratch_shapes=[pltpu.VMEM((A_HD, rows), F32), pltpu.VMEM((LANES, rows), _MXU), pltpu.VMEM((LANES, rows), _MXU),
                        bigt, bigt,
                        big(F32), big(F32), big(F32), big(F32), big(_MXU), big(_MXU), big(_MXU), big(_MXU)],
    )(qt8, ot, dot_, lse, k2, v2, k2t)


def _chunks_to_rows(t):
    return t.transpose(1, 3, 0, 2).reshape(t.shape[1] * t.shape[3], AKV_W)


def _ret_tables(zb):
    c = CHUNK

    def body(z_ref, m_ref, mw_ref, qd_ref, qdw_ref, kd_ref, kdw_ref, g_ref, gw_ref):
        fwd = pl.program_id(0) < R_H
        z = z_ref[0]
        lam = jnp.minimum(z, 0.0) - jnp.log(1.0 + jnp.exp(-jnp.abs(z)))
        i = lax.broadcasted_iota(jnp.int32, (c, c), 0).astype(F32)
        j = lax.broadcasted_iota(jnp.int32, (c, c), 1).astype(F32)
        diff = jnp.where(fwd, i - j, j - i)
        keep = diff >= jnp.where(fwd, 0.0, 1.0)
        dist = jnp.maximum(diff, 0.0)
        m = jnp.where(keep, jnp.exp(lam * dist), 0.0)
        m_ref[0] = m
        mw_ref[0] = m * dist
        fq = jnp.where(fwd, i + 1.0, c - i)
        qd = jnp.exp(lam * fq)
        qd_ref[0] = qd
        qdw_ref[0] = qd * fq
        fk = jnp.where(fwd, c - 1.0 - i, i)
        kd = jnp.exp(lam * fk)
        kd_ref[0] = kd
        kdw_ref[0] = kd * fk
        gdec = jnp.exp(lam * c)
        g_ref[0] = gdec
        gw_ref[0] = gdec * c

    big = pl.BlockSpec((1, c, c), lambda t: (t, 0, 0))
    vec = pl.BlockSpec((1, 1, LANES), lambda t: (t, 0, 0))
    bshape = jax.ShapeDtypeStruct((2 * R_H, c, c), F32)
    vshape = jax.ShapeDtypeStruct((2 * R_H, 1, LANES), F32)
    return pl.pallas_call(
        body, name="ret_tables", grid=(2 * R_H,), in_specs=[vec],
        out_specs=(big, big, big, big, big, big, vec, vec),
        out_shape=(bshape,) * 6 + (vshape, vshape),
    )(zb)


def _ret_fwd(rq, rk, pr, m, qd, kd, gdec, cb):
    seq = rq.shape[0]
    c = CHUNK
    ns = seq // (cb * c)

    def body(q_ref, k_ref, v_ref, m_ref, qd_ref, kd_ref, g_ref, y_ref, pst_ref, p_sc):
        d = pl.program_id(0)

        @pl.when(pl.program_id(1) == 0)
        def _():
            p_sc[...] = jnp.zeros_like(p_sc)

        def chunk(j, carry):
            cc = jnp.where(d == 0, j, cb - 1 - j)
            sl = pl.ds(pl.multiple_of(cc * c, c), c)
            heads = [slice(h * R_HD, (h + 1) * R_HD) for h in range(R_H)]
            qk = [_mm_nt(q_ref[sl, hs], k_ref[sl, hs]) for hs in heads]
            qp = [_mm(q_ref[sl, hs] * qd_ref[h], p_sc[h]) for h, hs in enumerate(heads)]
            kv = [_mm_tn(k_ref[sl, hs] * kd_ref[h], v_ref[sl, hs]) for h, hs in enumerate(heads)]
            for h, hs in enumerate(heads):
                p = p_sc[h]
                pst_ref[h, cc] = p
                y_ref[0, sl, hs] = _mm(qk[h] * m_ref[h], v_ref[sl, hs]) + qp[h]
                p_sc[h] = p * g_ref[h] + kv[h]
            return carry

        lax.fori_loop(0, cb, chunk, 0)

    def step(d, n):
        return d * (ns - 1 - n) + (1 - d) * n

    blk = lambda off: pl.BlockSpec((cb * c, R_W), lambda d, n: (step(d, n), off))
    big = pl.BlockSpec((R_H, c, c), lambda d, n: (d, 0, 0))
    vec = pl.BlockSpec((R_H, 1, LANES), lambda d, n: (d, 0, 0))
    return pl.pallas_call(
        body, name="ret_fwd", grid=(2, ns),
        in_specs=[blk(0), blk(0), blk(2), big, big, big, vec],
        out_specs=(pl.BlockSpec((1, cb * c, R_W), lambda d, n: (d, step(d, n), 0)),
                   pl.BlockSpec((R_H, cb, R_HD, R_HD), lambda d, n: (d, step(d, n), 0, 0))),
        out_shape=(jax.ShapeDtypeStruct((2, seq, R_W), F32),
                   jax.ShapeDtypeStruct((2 * R_H, seq // c, R_HD, R_HD), F32)),
        scratch_shapes=[pltpu.VMEM((R_H, R_HD, R_HD), F32)],
    )(rq, rk, pr, m, qd, kd, gdec)


def _ret_bwd(rq, rk, pr, dry, pst, m, mw, qd, qdw, kd, kdw, gdec, gw, cb):
    seq = rq.shape[0]
    c = CHUNK
    ns = seq // (cb * c)

    def body(q_ref, k_ref, v_ref, dy_ref, pst_ref, m_ref, mw_ref, qd_ref, qdw_ref, kd_ref, kdw_ref,
             g_ref, gw_ref, dq_ref, dk_ref, dv_ref, dlam_ref, r_sc, acc_sc, e_sc, g_sc):
        d = pl.program_id(0)
        n = pl.program_id(1)

        @pl.when(n == 0)
        def _():
            r_sc[...] = jnp.zeros_like(r_sc)
            acc_sc[...] = jnp.zeros_like(acc_sc)
            e_sc[...] = jnp.zeros_like(e_sc)
            g_sc[...] = jnp.zeros_like(g_sc)

        def chunk(j, carry):
            cc = jnp.where(d == 0, cb - 1 - j, j)
            sl = pl.ds(pl.multiple_of(cc * c, c), c)
            heads = [slice(h * R_HD, (h + 1) * R_HD) for h in range(R_H)]
            first = []
            for h, hs in enumerate(heads):
                q, k, v, dy = q_ref[sl, hs], k_ref[sl, hs], v_ref[sl, hs], dy_ref[sl, hs]
                r = r_sc[h]
                first.append((_mm_nt(q, k), _mm_nt(dy, v), _mm_nt(dy, pst_ref[h, cc]), _mm_nt(v, r),
                              _mm(k * kd_ref[h], r), _mm_tn(q * qd_ref[h], dy)))
            for h, hs in enumerate(heads):
                qk, ds, dyp, vr, kr, qdy = first[h]
                q, k, dy = q_ref[sl, hs], k_ref[sl, hs], dy_ref[sl, hs]
                r = r_sc[h]
                da = ds * m_ref[h]
                dv_ref[0, sl, hs] = _mm_tn(qk * m_ref[h], dy) + kr
                dq_ref[0, sl, hs] = _mm(da, k) + dyp * qd_ref[h]
                dk_ref[0, sl, hs] = _mm_tn(da, q) + vr * kd_ref[h]
                acc_sc[h] += dyp * q * qdw_ref[h] + vr * k * kdw_ref[h]
                e_sc[h] += ds * qk * mw_ref[h]
                g_sc[h] += r * pst_ref[h, cc]
                r_sc[h] = r * g_ref[h] + qdy
            return carry

        lax.fori_loop(0, cb, chunk, 0)

        @pl.when(n == ns - 1)
        def _():
            for h in range(R_H):
                tot = jnp.sum(jnp.sum(acc_sc[h] + e_sc[h] + g_sc[h] * gw_ref[h], axis=0, keepdims=True),
                              axis=1, keepdims=True)
                dlam_ref[h] = jnp.broadcast_to(tot, (1, LANES))

    def step(d, n):
        return d * n + (1 - d) * (ns - 1 - n)

    blk = lambda off: pl.BlockSpec((cb * c, R_W), lambda d, n: (step(d, n), off))
    big = pl.BlockSpec((R_H, c, c), lambda d, n: (d, 0, 0))
    vec = pl.BlockSpec((R_H, 1, LANES), lambda d, n: (d, 0, 0))
    out = pl.BlockSpec((1, cb * c, R_W), lambda d, n: (d, step(d, n), 0))
    oshape = jax.ShapeDtypeStruct((2, seq, R_W), F32)
    sq = pltpu.VMEM((R_H, R_HD, R_HD), F32)
    return pl.pallas_call(
        body, name="ret_bwd", grid=(2, ns),
        in_specs=[blk(0), blk(0), blk(2), blk(0),
                  pl.BlockSpec((R_H, cb, R_HD, R_HD), lambda d, n: (d, step(d, n), 0, 0)),
                  big, big, big, big, big, big, vec, vec],
        out_specs=(out, out, out, vec),
        out_shape=(oshape, oshape, oshape, jax.ShapeDtypeStruct((2 * R_H, 1, LANES), F32)),
        scratch_shapes=[sq, sq, sq, sq],
    )(rq, rk, pr, dry, pst, m, mw, qd, qdw, kd, kdw, gdec, gw)


def _group_norm(ry):
    yn, rs = [], []
    for h in range(R_H):
        s = ry[:, h * R_HD:(h + 1) * R_HD]
        mu = jnp.mean(s, axis=-1, keepdims=True)
        cen = s - mu
        r = lax.rsqrt(jnp.mean(cen * cen, axis=-1, keepdims=True) + GN_EPS)
        yn.append(cen * r)
        rs.append(r)
    return yn, rs


def _merge_fwd(x, o, y2, pr, pg, gain_r, wao, wro, wout):
    seq = x.shape[0]
    tm = _tile(seq, 256)

    def body(x_ref, o_ref, yf_ref, yb_ref, rg_ref, ga_ref, gr_ref, gn_ref, wao_ref, wro_ref, wout_ref,
             x1_ref, mg_ref, ri_ref):
        yn, _ = _group_norm(yf_ref[0] + yb_ref[0])
        rg = rg_ref[...]
        ret_in = jnp.concatenate(yn, axis=1) * gn_ref[...] * (rg * _sigmoid(rg))
        ri_ref[...] = _bf(ret_in)
        attn_out = _mm(o_ref[...], wao_ref[...])
        ret_out = _mm(ret_in, wro_ref[...])
        merged = _sigmoid(ga_ref[...]) * attn_out + _sigmoid(gr_ref[...]) * ret_out
        mg_ref[...] = _bf(merged)
        x1_ref[...] = x_ref[...] + _mm(merged, wout_ref[...])

    row = lambda w_, j=0: pl.BlockSpec((tm, w_), lambda i: (i, j))
    ydir = lambda d: pl.BlockSpec((1, tm, R_W), lambda i: (d, i, 0))
    return pl.pallas_call(
        body, name="merge_fwd", grid=(seq // tm,),
        in_specs=[row(D), row(AQ_W), ydir(0), ydir(1), row(R_W, 3), row(D, 0), row(D, 1),
                  _full((1, R_W)), _full((AQ_W, D)), _full((R_W, D)), _full((D, D))],
        out_specs=(row(D), row(D), row(R_W)),
        out_shape=(jax.ShapeDtypeStruct((seq, D), F32), jax.ShapeDtypeStruct((seq, D), _MXU),
                   jax.ShapeDtypeStruct((seq, R_W), _MXU)),
    )(x, o, y2, y2, pr, pg, pg, gain_r, wao, wro, wout)


def _mlp_fwd(x1, gain, wup, wdown):
    seq = x1.shape[0]
    tm = _tile(seq, 512)
    fc = 2048
    nfc = FF // fc

    def body(x_ref, g_ref, wu_ref, wd_ref, x2_ref, hm_sc, acc_sc):
        c = pl.program_id(1)

        @pl.when(c == 0)
        def _():
            n, _ = _rms(x_ref[...])
            hm_sc[...] = _bf(n * g_ref[...])
            acc_sc[...] = jnp.zeros_like(acc_sc)

        halves = (slice(0, fc // 2), slice(fc // 2, fc))
        ups = [jnp.maximum(_mm(hm_sc[...], wu_ref[:, hs]), 0.0) for hs in halves]
        acc_sc[...] += _mm(ups[0] * ups[0], wd_ref[halves[0], :]) + _mm(ups[1] * ups[1], wd_ref[halves[1], :])

        @pl.when(c == nfc - 1)
        def _():
            x2_ref[...] = x_ref[...] + acc_sc[...]

    return pl.pallas_call(
        body, name="mlp_fwd", grid=(seq // tm, nfc),
        in_specs=[pl.BlockSpec((tm, D), lambda i, c: (i, 0)), pl.BlockSpec((1, D), lambda i, c: (0, 0)),
                  pl.BlockSpec((D, fc), lambda i, c: (0, c)), pl.BlockSpec((fc, D), lambda i, c: (c, 0))],
        out_specs=pl.BlockSpec((tm, D), lambda i, c: (i, 0)),
        out_shape=jax.ShapeDtypeStruct((seq, D), F32),
        scratch_shapes=[pltpu.VMEM((tm, D), _MXU), pltpu.VMEM((tm, D), F32)],
    )(x1, gain, wup, wdown)


def _ple_loss(x2, p, tgt, g_ple, g_fin, wpg, wpgt, wple):
    seq = x2.shape[0]
    tm = _tile(seq, 256)

    def body(x2_ref, p_ref, t_ref, gp_ref, gf_ref, wpg_ref, wpgt_ref, wple_ref,
             dx2_ref, de_ref, dz_ref, hp_ref, loss_ref, dgf_ref, dgp_ref):
        @pl.when(pl.program_id(0) == 0)
        def _():
            loss_ref[...] = jnp.zeros_like(loss_ref)
            dgf_ref[...] = jnp.zeros_like(dgf_ref)
            dgp_ref[...] = jnp.zeros_like(dgp_ref)

        x2 = x2_ref[...]
        gp, gf = gp_ref[...], gf_ref[...]
        n2, r2 = _rms(x2)
        hp = _bf(n2 * gp)
        hp_ref[...] = hp
        gate = _sigmoid(_mm(hp, wpg_ref[...]))
        e = _mm(p_ref[...], wple_ref[...])
        x3 = x2 + gate * e
        n3, r3 = _rms(x3)
        diff = n3 * gf - t_ref[...]
        row_loss = jnp.mean(diff * diff, axis=-1, keepdims=True)
        loss_ref[...] += 0.5 * jnp.sum(row_loss, axis=0, keepdims=True)
        dy = diff * (1.0 / D)
        dgf_ref[...] += jnp.sum(dy * n3, axis=0, keepdims=True)
        dx3 = _rms_bwd(n3, r3, gf, dy)
        de_ref[...] = _bf(dx3 * gate)
        dz = dx3 * e * gate * (1.0 - gate)
        dz_ref[...] = _bf(dz)
        dhp = _mm(dz, wpgt_ref[...])
        dgp_ref[...] += jnp.sum(dhp * n2, axis=0, keepdims=True)
        dx2_ref[...] = dx3 + _rms_bwd(n2, r2, gp, dhp)

    row = lambda w_: pl.BlockSpec((tm, w_), lambda i: (i, 0))
    act = lambda dt: jax.ShapeDtypeStruct((seq, D), dt)
    return pl.pallas_call(
        body, name="ple_loss", grid=(seq // tm,),
        in_specs=[row(D), row(PLE), row(D), _full((1, D)), _full((1, D)),
                  _full((D, D)), _full((D, D)), _full((PLE, D))],
        out_specs=(row(D), row(D), row(D), row(D), _full((1, LANES)), _full((1, D)), _full((1, D))),
        out_shape=(act(F32), act(_MXU), act(_MXU), act(_MXU), jax.ShapeDtypeStruct((1, LANES), F32),
                   jax.ShapeDtypeStruct((1, D), F32), jax.ShapeDtypeStruct((1, D), F32)),
    )(x2, p, tgt, g_ple, g_fin, wpg, wpgt, wple)


def _mlp_bwd(x1, dx2, gain, wup, wdownt, wupt):
    seq = x1.shape[0]
    tm = _tile(seq, 512)
    fc = 2048
    nfc = FF // fc

    def body(x_ref, dx2_ref, g_ref, wu_ref, wdt_ref, wut_ref,
             dx1_ref, a_ref, du_ref, hm_ref, dg_ref, dhm_sc):
        i = pl.program_id(0)
        c = pl.program_id(1)

        @pl.when((i == 0) & (c == 0))
        def _():
            dg_ref[...] = jnp.zeros_like(dg_ref)

        @pl.when(c == 0)
        def _():
            n, _ = _rms(x_ref[...])
            hm_ref[...] = _bf(n * g_ref[...])
            dhm_sc[...] = jnp.zeros_like(dhm_sc)

        halves = (slice(0, fc // 2), slice(fc // 2, fc))
        ups = [jnp.maximum(_mm(hm_ref[...], wu_ref[:, hs]), 0.0) for hs in halves]
        das = [_mm(dx2_ref[...], wdt_ref[:, hs]) for hs in halves]
        part = None
        for u, da, hs in zip(ups, das, halves):
            a_ref[:, hs] = _bf(u * u)
            du = _bf(da * (2.0 * u))
            du_ref[:, hs] = du
            t = _mm(du, wut_ref[hs, :])
            part = t if part is None else part + t
        dhm_sc[...] += part

        @pl.when(c == nfc - 1)
        def _():
            n, r = _rms(x_ref[...])
            dhm = dhm_sc[...]
            dg_ref[...] += jnp.sum(dhm * n, axis=0, keepdims=True)
            dx1_ref[...] = dx2_ref[...] + _rms_bwd(n, r, g_ref[...], dhm)

    rowd = pl.BlockSpec((tm, D), lambda i, c: (i, 0))
    rowf = pl.BlockSpec((tm, fc), lambda i, c: (i, c))
    return pl.pallas_call(
        body, name="mlp_bwd", grid=(seq // tm, nfc),
        in_specs=[rowd, rowd, pl.BlockSpec((1, D), lambda i, c: (0, 0)),
                  pl.BlockSpec((D, fc), lambda i, c: (0, c)), pl.BlockSpec((D, fc), lambda i, c: (0, c)),
                  pl.BlockSpec((fc, D), lambda i, c: (c, 0))],
        out_specs=(rowd, rowf, rowf, rowd, pl.BlockSpec((1, D), lambda i, c: (0, 0))),
        out_shape=(jax.ShapeDtypeStruct((seq, D), F32), jax.ShapeDtypeStruct((seq, FF), _MXU),
                   jax.ShapeDtypeStruct((seq, FF), _MXU), jax.ShapeDtypeStruct((seq, D), _MXU),
                   jax.ShapeDtypeStruct((1, D), F32)),
        scratch_shapes=[pltpu.VMEM((tm, D), F32)],
    )(x1, dx2, gain, wup, wdownt, wupt)


def _merge_bwd(dx1, o, y2, pr, pg, gain_r, wao, wro, woutt, waot, wrot, gpack):
    seq = dx1.shape[0]
    tm = _tile(seq, 256)
    nsteps = seq // tm

    def body(dx1_ref, o_ref, yf_ref, yb_ref, rg_ref, ga_ref, gr_ref, gn_ref, wao_ref, wro_ref,
             woutt_ref, waot_ref, wrot_ref, gpack_ref,
             dpg_ref, dao_ref, dro_ref, do_ref, dry_ref, drg_ref, dgn_ref, land_ref,
             send_sems, recv_sems, local_sem):
        start, finish = _exchange_steps(gpack_ref, land_ref, send_sems, recv_sems, local_sem, True)
        pl.when(pl.program_id(0) == 0)(start)

        @pl.when(pl.program_id(0) == 0)
        def _():
            dgn_ref[...] = jnp.zeros_like(dgn_ref)

        yn_l, rs_l = _group_norm(yf_ref[0] + yb_ref[0])
        yn = jnp.concatenate(yn_l, axis=1)
        rg = rg_ref[...]
        gn = gn_ref[...]
        sg = _sigmoid(rg)
        sil = rg * sg
        ret_in = yn * gn * sil
        attn_out = _mm(o_ref[...], wao_ref[...])
        ret_out = _mm(ret_in, wro_ref[...])
        sa = _sigmoid(ga_ref[...])
        sr = _sigmoid(gr_ref[...])
        dm = _mm(dx1_ref[...], woutt_ref[...])
        dpg_ref[:, 0:D] = _bf(dm * attn_out * sa * (1.0 - sa))
        dpg_ref[:, D:2 * D] = _bf(dm * ret_out * sr * (1.0 - sr))
        dao = _bf(dm * sa)
        dro = _bf(dm * sr)
        dao_ref[...] = dao
        dro_ref[...] = dro
        do_ref[...] = _mm(dao, waot_ref[...])
        dri = _mm(dro, wrot_ref[...])
        dgn_ref[...] += jnp.sum(dri * yn * sil, axis=0, keepdims=True)
        drg_ref[...] = _bf(dri * yn * gn * (sg * (1.0 + rg * (1.0 - sg))))
        dyn = dri * gn * sil
        dry = []
        for h in range(R_H):
            dh = dyn[:, h * R_HD:(h + 1) * R_HD]
            dry.append(rs_l[h] * (dh - jnp.mean(dh, axis=-1, keepdims=True)
                                  - yn_l[h] * jnp.mean(dh * yn_l[h], axis=-1, keepdims=True)))
        dry_ref[...] = jnp.concatenate(dry, axis=1)
        pl.when(pl.program_id(0) == nsteps - 1)(finish)

    row = lambda w_, j=0: pl.BlockSpec((tm, w_), lambda i: (i, j))
    ydir = lambda d: pl.BlockSpec((1, tm, R_W), lambda i: (d, i, 0))
    return pl.pallas_call(
        body, name="merge_bwd", grid=(nsteps,),
        in_specs=[row(D), row(AQ_W), ydir(0), ydir(1), row(R_W, 3), row(D, 0), row(D, 1),
                  _full((1, R_W)), _full((AQ_W, D)), _full((R_W, D)), _full((D, D)),
                  _full((D, AQ_W)), _full((D, R_W)), _ANY],
        out_specs=(row(PG_W), row(D), row(D), row(AQ_W), row(R_W), row(R_W), _full((1, R_W)), _ANY),
        out_shape=(jax.ShapeDtypeStruct((seq, PG_W), _MXU), jax.ShapeDtypeStruct((seq, D), _MXU),
                   jax.ShapeDtypeStruct((seq, D), _MXU), jax.ShapeDtypeStruct((seq, AQ_W), F32),
                   jax.ShapeDtypeStruct((seq, R_W), F32), jax.ShapeDtypeStruct((seq, R_W), _MXU),
                   jax.ShapeDtypeStruct((1, R_W), F32), jax.ShapeDtypeStruct(gpack.shape, gpack.dtype)),
        scratch_shapes=list(_COMM_SCRATCH),
    )(dx1, o, y2, y2, pr, pg, pg, gain_r, wao, wro, woutt, waot, wrot, gpack)


def _qk_prep_bwd(pa, dqh, dk2, dv2, rdq, rdk, rdv, drg, gq, gk, seg, ca, sa, cr, sr):
    seq = pa.shape[0]
    tm = _tile(seq, 256)

    def body(pa_ref, dqh_ref, dk2_ref, dv2_ref, rdqf_ref, rdqb_ref, rdkf_ref, rdkb_ref, rdvf_ref, rdvb_ref,
             drg_ref, gq_ref, gk_ref, seg_ref, ca_ref, sa_ref, cr_ref, sr_ref,
             dpa_ref, dpr_ref, dgq_ref, dgk_ref):
        @pl.when(pl.program_id(0) == 0)
        def _():
            dgq_ref[...] = jnp.zeros_like(dgq_ref)
            dgk_ref[...] = jnp.zeros_like(dgk_ref)

        ca_, sa_ = ca_ref[...], sa_ref[...]

        def norm_bwd(raw, gain, dy, segm, dg_ref):
            msq = jnp.dot(raw * raw, segm, precision=HIGHEST, preferred_element_type=F32)
            r = lax.rsqrt(msq + EPS)
            n = raw * r
            dg_ref[...] += jnp.sum(dy * n, axis=0, keepdims=True)
            dn = dy * gain
            return r * (dn - n * jnp.dot(dn * n, segm, precision=HIGHEST, preferred_element_type=F32))

        dqn = _rope(dqh_ref[...] * (A_HD ** -0.5), _cat(ca_, 4), -_cat(sa_, 4), A_HD // 2)
        dpa_ref[:, 0:AQ_W] = _bf(norm_bwd(pa_ref[:, 0:AQ_W], gq_ref[...], dqn, seg_ref[...], dgq_ref))
        dkn = _rope(dk2_ref[...], ca_, -sa_, A_HD // 2)
        dpa_ref[:, AQ_W:AQ_W + AKV_W] = _bf(norm_bwd(pa_ref[:, AQ_W:AQ_W + AKV_W], gk_ref[...], dkn,
                                                     seg_ref[0:AKV_W, 0:AKV_W], dgk_ref))
        dpa_ref[:, AQ_W + AKV_W:PA_W] = _bf(dv2_ref[...])
        cr_, sr_ = _cat(cr_ref[...], 4), -_cat(sr_ref[...], 4)
        dpr_ref[:, 0:R_W] = _bf(_rope((rdqf_ref[0] + rdqb_ref[0]) * (R_HD ** -0.5), cr_, sr_, R_HD // 2))
        dpr_ref[:, R_W:2 * R_W] = _bf(_rope(rdkf_ref[0] + rdkb_ref[0], cr_, sr_, R_HD // 2))
        dpr_ref[:, 2 * R_W:3 * R_W] = _bf(rdvf_ref[0] + rdvb_ref[0])
        dpr_ref[:, 3 * R_W:4 * R_W] = drg_ref[...]

    row = lambda w_: pl.BlockSpec((tm, w_), lambda i: (i, 0))
    ydir = lambda d: pl.BlockSpec((1, tm, R_W), lambda i: (d, i, 0))
    return pl.pallas_call(
        body, name="qk_prep_bwd", grid=(seq // tm,),
        in_specs=[row(PA_W), row(AQ_W), row(AKV_W), row(AKV_W), ydir(0), ydir(1), ydir(0), ydir(1),
                  ydir(0), ydir(1), row(R_W), _full((1, AQ_W)), _full((1, AKV_W)), _full((AQ_W, AQ_W)),
                  row(LANES), row(LANES), row(LANES), row(LANES)],
        out_specs=(row(PA_W), row(PR_W), _full((1, AQ_W)), _full((1, AKV_W))),
        out_shape=(jax.ShapeDtypeStruct((seq, PA_W), _MXU), jax.ShapeDtypeStruct((seq, PR_W), _MXU),
                   jax.ShapeDtypeStruct((1, AQ_W), F32), jax.ShapeDtypeStruct((1, AKV_W), F32)),
    )(pa, dqh, dk2, dv2, rdq, rdq, rdk, rdk, rdv, rdv, drg, gq, gk, seg, ca, sa, cr, sr)


def _in_proj_bwd(x, dx1, gain, dpa, dpr, dpg, wint):
    seq = x.shape[0]
    tm = _tile(seq, 256)

    def body(x_ref, dx1_ref, g_ref, dpa_ref, dpr_ref, dpg_ref, wt_ref, dx_ref, dg_ref):
        @pl.when(pl.program_id(0) == 0)
        def _():
            dg_ref[...] = jnp.zeros_like(dg_ref)

        dh = (_mm(dpa_ref[...], wt_ref[0:PA_W, :]) + _mm(dpr_ref[...], wt_ref[PA_W:PA_W + PR_W, :])
              + _mm(dpg_ref[...], wt_ref[PA_W + PR_W:IN_W, :]))
        n, r = _rms(x_ref[...])
        dg_ref[...] += jnp.sum(dh * n, axis=0, keepdims=True)
        dx_ref[...] = dx1_ref[...] + _rms_bwd(n, r, g_ref[...], dh)

    row = lambda w_: pl.BlockSpec((tm, w_), lambda i: (i, 0))
    return pl.pallas_call(
        body, name="in_proj_bwd", grid=(seq // tm,),
        in_specs=[row(D), row(D), _full((1, D)), row(PA_W), row(PR_W), row(PG_W), _full((IN_W, D))],
        out_specs=(row(D), _full((1, D))),
        out_shape=(jax.ShapeDtypeStruct((seq, D), F32), jax.ShapeDtypeStruct((1, D), F32)),
    )(x, dx1, gain, dpa, dpr, dpg, wint)


def _wgrad(a, b, name):
    seq, m = a.shape
    n = b.shape[1]
    tm, tn, ts = _tile(m, 1024), _tile(n, 1024), _tile(seq, 2048)
    ns = seq // ts

    def body(a_ref, b_ref, o_ref):
        @pl.when(pl.program_id(2) == 0)
        def _():
            o_ref[...] = jnp.zeros_like(o_ref)

        o_ref[...] += _mm_tn(a_ref[...], b_ref[...])

    return pl.pallas_call(
        body, name=name, grid=(m // tm, n // tn, ns),
        in_specs=[pl.BlockSpec((ts, tm), lambda i, j, s: (s, i)), pl.BlockSpec((ts, tn), lambda i, j, s: (s, j))],
        out_specs=pl.BlockSpec((tm, tn), lambda i, j, s: (i, j)),
        out_shape=jax.ShapeDtypeStruct((m, n), F32),
    )(a, b)


def _adamw_math(w, g, m, v):
    m = B1 * m + (1.0 - B1) * g
    v = B2 * v + (1.0 - B2) * (g * g)
    m_hat = m / (1.0 - B1 ** STEP)
    v_hat = v / (1.0 - B2 ** STEP)
    delta = -LR * (m_hat / (jnp.sqrt(v_hat) + ADAM_EPS) + WD * w)
    return delta, m, v


def _adamw_big(land, own, w, m, v, name):
    rws, cols = w.shape
    tr = next(t for t in range(min(rws, 288) // 16 * 16, 0, -16) if rws % t == 0)

    def body(l_ref, o_ref, w_ref, m_ref, v_ref, g_ref, d_ref, nm_ref, nv_ref):
        x, y, c = _mesh_pos()
        me = 4 * x + 2 * y + c
        g = o_ref[...]
        for j in range(N_DEV):
            g = g + jnp.where(me == j, 0.0, l_ref[j].astype(F32))
        g_ref[...] = g
        d_ref[...], nm_ref[...], nv_ref[...] = _adamw_math(w_ref[...], g, m_ref[...], v_ref[...])

    row = pl.BlockSpec((tr, cols), lambda i: (i, 0))
    shp = jax.ShapeDtypeStruct((rws, cols), F32)
    return pl.pallas_call(
        body, name=name, grid=(rws // tr,),
        in_specs=[pl.BlockSpec((N_DEV, tr, cols), lambda i: (0, i, 0)), row, row, row, row],
        out_specs=(row, row, row, row), out_shape=(shp, shp, shp, shp),
    )(land, own, w, m, v)


def _adamw_small(sland, w, m, v):
    def body(l_ref, w_ref, m_ref, v_ref, g_ref, d_ref, nm_ref, nv_ref, loss_ref):
        s = l_ref[0]
        for j in range(1, N_DEV):
            s = s + l_ref[j]
        w = w_ref[...]
        gq = s[8:9]
        for h in range(1, A_H):
            gq = gq + s[8 + h:9 + h]
        gk = s[16:17] + s[17:18]
        gdec = s[5:6] * _sigmoid(-w[5:6])
        g = jnp.concatenate([s[0:5], gdec, gq, gk], axis=0)
        g_ref[...] = g
        d_ref[...], nm_ref[...], nv_ref[...] = _adamw_math(w, g, m_ref[...], v_ref[...])
        loss_ref[...] = s[6:7, 0:LANES]

    shp = jax.ShapeDtypeStruct((8, PACK_COLS), F32)
    return pl.pallas_call(
        body, name="adamw_small",
        out_shape=(shp, shp, shp, shp, jax.ShapeDtypeStruct((1, LANES), F32)),
    )(sland, w, m, v)


_BIG = (("w_attn_o", AQ_W, D, 1), ("w_ret_o", R_W, D, 1), ("w_out", D, D, 0),
        ("w_up", D, FF, 1), ("w_down", FF, D, 0), ("w_ple_gate", D, D, 0), ("w_ple", PLE, D, 1))
_LATE = _BIG[:3]
_EARLY = _BIG[3:]
IN_SHARD = IN_W // N_DEV
_SMALL = ("mix_norm", "mlp_norm", "ple_norm", "final_norm", "ret_norm_gain", "ret_decay_logit",
          "attn_q_norm", "attn_k_norm")


def _shard_shape(rows, cols, axis):
    return (rows // N_DEV, cols) if axis == 0 else (rows, cols // N_DEV)


def _pack_shards(shards):
    flat = jnp.concatenate([s.reshape(-1) for s in shards])
    return flat.reshape(-1, PACK_COLS)


def _unpack_gathered(gathered):
    flat = gathered.reshape(N_DEV, -1)
    out, off = {}, 0
    for name, rows, cols, axis in _BIG:
        sr, sc = _shard_shape(rows, cols, axis)
        blk = flat[:, off:off + sr * sc].reshape(N_DEV, sr, sc)
        off += sr * sc
        out[name] = blk.reshape(rows, cols) if axis == 0 else blk.transpose(1, 0, 2).reshape(rows, cols)
    return out


def _pack_full_grads(grads, group):
    parts = []
    for name, rows, cols, axis in group:
        sr, sc = _shard_shape(rows, cols, axis)
        g = grads[name]
        blk = g.reshape(N_DEV, sr, sc) if axis == 0 else g.reshape(rows, N_DEV, sc).transpose(1, 0, 2)
        parts.append(blk.reshape(N_DEV, -1))
    flat = jnp.concatenate(parts, axis=1)
    return flat.reshape(N_DEV, -1, PACK_COLS)


def _unpack_shard(packed, group):
    flat = packed.reshape(-1)
    out, off = {}, 0
    for name, rows, cols, axis in group:
        sr, sc = _shard_shape(rows, cols, axis)
        out[name] = flat[off:off + sr * sc].reshape(1, sr, sc)
        off += sr * sc
    return out


def _pack_small(vals):
    rows = [jnp.pad(vals[n].reshape(-1), (0, PACK_COLS - vals[n].size)) for n in _SMALL]
    return jnp.stack(rows)


def _unpack_small(packed, like):
    return {n: packed[i, :like[n].size].reshape(like[n].shape) for i, n in enumerate(_SMALL)}


def _row(v):
    return jnp.pad(v.reshape(-1), (0, PACK_COLS - v.size))


def kernel(x, p, mix_norm, w_in, attn_q_norm, attn_k_norm, ret_decay_logit, ret_norm_gain, w_attn_o, w_ret_o, w_out, mlp_norm, w_up, w_down, ple_norm, w_ple_gate, w_ple, final_norm, loss_target, m_mix_norm, m_w_in, m_attn_q_norm, m_attn_k_norm, m_ret_decay_logit, m_ret_norm_gain, m_w_attn_o, m_w_ret_o, m_w_out, m_mlp_norm, m_w_up, m_w_down, m_ple_norm, m_w_ple_gate, m_w_ple, m_final_norm, v_mix_norm, v_w_in, v_attn_q_norm, v_attn_k_norm, v_ret_decay_logit, v_ret_norm_gain, v_w_attn_o, v_w_ret_o, v_w_out, v_mlp_norm, v_w_up, v_w_down, v_ple_norm, v_w_ple_gate, v_w_ple, v_final_norm):
    args = dict(locals())
    seq = x.shape[1]
    xs = x[0]
    ps = p[0, 0]
    tgt = loss_target[0]

    big_names = [b[0] for b in _BIG]
    wshard = _pack_shards([args[n] for n in big_names])
    win = _all_gather(w_in[0].astype(_MXU)).transpose(1, 0, 2).reshape(D, IN_W)

    g_mix, g_mlp, g_ple = mix_norm, mlp_norm, ple_norm
    g_fin = final_norm.reshape(1, D)
    gq = jnp.tile(attn_q_norm, (1, A_H))
    gk = jnp.tile(attn_k_norm, (1, A_KV))
    seg = _seg_mean_matrix()
    ca, sa, cr, sr = _rope_tables(seq)

    pa, pr, pg, h, rest_g = _in_proj(xs, g_mix, win, wshard.astype(_MXU))
    wfull = _unpack_gathered(rest_g)
    wao, wro, wout = wfull["w_attn_o"], wfull["w_ret_o"], wfull["w_out"]
    wup, wdown, wpg, wple = wfull["w_up"], wfull["w_down"], wfull["w_ple_gate"], wfull["w_ple"]
    qh, kh, vh, rqh, rkh = _qk_prep(pa, pr, gq, gk, seg, ca, sa, cr, sr)

    tq = _tile(seq, 128)
    tk = _tile(seq // 4, 1024)
    qt8 = qh.reshape(seq, A_H, A_HD).transpose(1, 2, 0)
    vta = jnp.stack([jnp.concatenate([_chunk_t(vh[:, g * A_HD:(g + 1) * A_HD], tk),
                                      jnp.ones((seq // tk, 16, tk), _MXU)], axis=1) for g in range(A_KV)])
    ot, lse = _attn_fwd(qt8, kh, vta, tq, tk)
    o = _heads_to_rows(ot)

    zb = jnp.broadcast_to(ret_decay_logit.reshape(2 * R_H, 1, 1), (2 * R_H, 1, LANES))
    tm_, tmw, tqd, tqdw, tkd, tkdw, tg, tgw = _ret_tables(zb)
    cb = _tile(seq // CHUNK, 8)
    y2, pst = _ret_fwd(rqh, rkh, pr, tm_, tqd, tkd, tg, cb)

    x1, merged, ret_in = _merge_fwd(xs, o, y2, pr, pg, ret_norm_gain, wao, wro, wout)
    x2 = _mlp_fwd(x1, g_mlp, wup, wdown)

    dx2, de, dz, hp, loss_p, dg_fin, dg_ple = _ple_loss(x2, ps, tgt, g_ple, g_fin, wpg, wpg.T, wple)
    dx1, act, du, hm, dg_mlp = _mlp_bwd(x1, dx2, g_mlp, wup, wdown.T, wup.T)
    me = 4 * lax.axis_index("x") + 2 * lax.axis_index("y") + lax.axis_index("c")
    gpack_e = _pack_full_grads({"w_up": _wgrad(hm, du, "wgrad_up"), "w_down": _wgrad(act, dx2, "wgrad_down"),
                                "w_ple_gate": _wgrad(hp, dz, "wgrad_ple_gate"), "w_ple": _wgrad(ps, de, "wgrad_ple")},
                               _EARLY)
    dpg, dao, dro, do, dry, drg, dg_gn, land_e = _merge_bwd(dx1, o, y2, pr, pg, ret_norm_gain, wao, wro,
                                                            wout.T, wao.T, wro.T, _bf(gpack_e))
    rdq, rdk, rdv, dlam = _ret_bwd(rqh, rkh, pr, dry, pst, tm_, tmw, tqd, tqdw, tkd, tkdw, tg, tgw, cb)

    ksplit = 1
    tkb = _tile(seq // 4, 512)
    dot_ = do.reshape(seq, A_H, A_HD).transpose(1, 2, 0)
    dkt, dvt, dqt = _attn_bwd(qt8, ot, dot_, lse, kh, vh, _chunk_t(kh, tkb), tq, tkb, ksplit)
    dqh = _heads_to_rows(jnp.sum(dqt, axis=0))
    dpa, dpr, dg_q, dg_k = _qk_prep_bwd(pa, dqh, _chunks_to_rows(dkt), _chunks_to_rows(dvt), rdq, rdk, rdv, drg, gq, gk, seg, ca, sa, cr, sr)
    grad_x, dg_mix = _in_proj_bwd(xs, dx1, g_mix, dpa, dpr, dpg, win.T)

    wg_in = jnp.concatenate([_wgrad(h, dpa, "wgrad_in_a"), _wgrad(h, dpr, "wgrad_in_r"),
                             _wgrad(h, dpg, "wgrad_in_g")], axis=1)
    gpack_l = _pack_full_grads({"w_attn_o": _wgrad(o, dao, "wgrad_attn_o"),
                                "w_ret_o": _wgrad(ret_in, dro, "wgrad_ret_o"),
                                "w_out": _wgrad(merged, dx1, "wgrad_out")}, _LATE)
    gpack_in = wg_in.reshape(D, N_DEV, IN_SHARD).transpose(1, 0, 2)
    small = jnp.stack(
        [_row(dg_mix), _row(dg_mlp), _row(dg_ple), _row(dg_fin), _row(dg_gn), _row(dlam[:, 0, 0]),
         _row(loss_p[0, 0:1]), jnp.zeros((PACK_COLS,), F32)]
        + [_row(dg_q[0, hh * A_HD:(hh + 1) * A_HD]) for hh in range(A_H)]
        + [_row(dg_k[0, hh * A_HD:(hh + 1) * A_HD]) for hh in range(A_KV)]
        + [jnp.zeros((PACK_COLS,), F32)] * (SMALL_ROWS - 18))

    own = lambda pack: lax.dynamic_index_in_dim(pack, me, axis=0, keepdims=False)
    land_in, land_l, sland = _exchange_grads([_bf(gpack_in), _bf(gpack_l), small[None]])
    in_sh = _adamw_big(land_in, own(gpack_in), w_in[0], m_w_in[0], v_w_in[0], "adamw_w_in")
    group_sh = []
    for group, land, pack, name in ((_EARLY, land_e, gpack_e, "adamw_early"), (_LATE, land_l, gpack_l, "adamw_late")):
        packed = lambda pre: _pack_shards([args[pre + g[0]] for g in group])
        group_sh.append((group, _adamw_big(land, own(pack), packed(""), packed("m_"), packed("v_"), name)))
    g_sm, d_sm, m_sm, v_sm, loss_row = _adamw_small(
        sland, _pack_small({n: args[n] for n in _SMALL}), _pack_small({n: args["m_" + n] for n in _SMALL}),
        _pack_small({n: args["v_" + n] for n in _SMALL}))

    names = ["mix_norm", "w_in", "attn_q_norm", "attn_k_norm", "ret_decay_logit", "ret_norm_gain", "w_attn_o",
             "w_ret_o", "w_out", "mlp_norm", "w_up", "w_down", "ple_norm", "w_ple_gate", "w_ple", "final_norm"]
    like = {n: args[n] for n in _SMALL}
    outs = [loss_row[0, 0], grad_x[None]]
    for kind, sm in enumerate((g_sm, d_sm, m_sm, v_sm)):
        table = {**_unpack_small(sm, like), "w_in": in_sh[kind][None]}
        for group, res in group_sh:
            table.update(_unpack_shard(res[kind], group))
        outs += [table[n] for n in names]
    return tuple(outs)
```

```python
import functools

import jax
import jax.numpy as jnp
from jax import lax
from jax.experimental import pallas as pl
from jax.experimental.pallas import tpu as pltpu

F32 = jnp.float32
_MXU = jnp.bfloat16

D = 1024
PLE = 256
GRID_W = 64
A_HD = 64
A_H = 8
A_KV = 2
A_G = A_H // A_KV
AQ_W = A_H * A_HD
AKV_W = A_KV * A_HD
R_HD = 128
R_H = 4
R_W = R_H * R_HD
IN_W = AQ_W + 2 * AKV_W + 4 * R_W + 2 * D
PA_W = AQ_W + 2 * AKV_W
PR_W = 4 * R_W
PG_W = 2 * D
FF = 4 * D
CHUNK = 128
ROPE_THETA = 10000.0
EPS = 1e-6
GN_EPS = 1e-5
N_DEV = 8

LR, B1, B2, ADAM_EPS, WD, STEP = 0.001, 0.9, 0.999, 1e-08, 0.01, 10

LANES = 128
PACK_COLS = 1024
SMALL_ROWS = 24
HIGHEST = lax.Precision.HIGHEST


def _tile(n, pref):
    t = min(n, pref)
    assert n % t == 0, (n, t)
    return t


def _bf(a):
    return a.astype(_MXU)


def _mm(a, b):
    return jnp.dot(_bf(a), _bf(b), preferred_element_type=F32)


def _mm_nt(a, b):
    return lax.dot_general(_bf(a), _bf(b), (((1,), (1,)), ((), ())), preferred_element_type=F32)


def _mm_tn(a, b):
    return lax.dot_general(_bf(a), _bf(b), (((0,), (0,)), ((), ())), preferred_element_type=F32)


def _sigmoid(z):
    return 1.0 / (1.0 + jnp.exp(-z))


def _rms(x):
    r = lax.rsqrt(jnp.mean(x * x, axis=-1, keepdims=True) + EPS)
    return x * r, r


def _rms_bwd(n, r, gain, dy):
    dn = dy * gain
    return r * (dn - n * jnp.mean(dn * n, axis=-1, keepdims=True))


def _swap_halves(x, half):
    n = x.shape[-1]
    lane = lax.broadcasted_iota(jnp.int32, x.shape, x.ndim - 1)
    first = (lane % (2 * half)) < half
    return jnp.where(first, pltpu.roll(x, n - half, axis=1), pltpu.roll(x, half, axis=1))


def _rope(x, cos, sin, half):
    return x * cos + _swap_halves(x, half) * sin


def _cat(t, reps):
    return jnp.concatenate([t] * reps, axis=1)


def _full(shape):
    nd = len(shape)
    return pl.BlockSpec(shape, lambda *_: (0,) * nd)


def _rope_tables(seq):
    def tab(head_dim):
        n_axis = head_dim // 4
        freqs = ROPE_THETA ** (-jnp.arange(n_axis, dtype=F32) / n_axis)
        rows = seq // GRID_W
        row = jnp.repeat(jnp.arange(rows, dtype=F32), GRID_W)
        col = jnp.tile(jnp.arange(GRID_W, dtype=F32), rows)
        ang = jnp.concatenate([row[:, None] * freqs, col[:, None] * freqs], axis=-1)
        c, s = jnp.cos(ang), jnp.sin(ang)
        return jnp.concatenate([c, c], axis=-1), jnp.concatenate([-s, s], axis=-1)
    ca, sa = tab(A_HD)
    cr, sr = tab(R_HD)
    return jnp.tile(ca, (1, 2)), jnp.tile(sa, (1, 2)), cr, sr


def _seg_mean_matrix():
    i = jnp.arange(AQ_W) // A_HD
    return (i[:, None] == i[None, :]).astype(F32) / A_HD


def _mesh_pos():
    return lax.axis_index("x"), lax.axis_index("y"), lax.axis_index("c")


def _gather_steps(x_ref, out_ref, send_sems, recv_sems, local_sem, base=0):
    x, y, c = _mesh_pos()
    me, sibling = (x, y, c), (x, y, 1 - c)
    chips = [(1 - x, y), (x, 1 - y), (1 - x, 1 - y)]

    def slot(px, py, pc):
        return out_ref.at[4 * px + 2 * py + pc]

    def copy(k, block, to, src=None):
        return pltpu.make_async_remote_copy(
            src_ref=slot(*block) if src is None else src, dst_ref=slot(*block),
            send_sem=send_sems.at[base + k], recv_sem=recv_sems.at[base + k],
            device_id=to, device_id_type=pl.DeviceIdType.MESH)

    mine = pltpu.make_async_copy(x_ref, slot(*me), local_sem)
    first = [copy(0, me, sibling, src=x_ref)]
    first += [copy(1 + j, me, (*chip, c), src=x_ref) for j, chip in enumerate(chips)]
    passed = [copy(4 + j, (*chip, c), sibling) for j, chip in enumerate(chips)]

    def start():
        mine.start()
        for cp in first:
            cp.start()

    def finish():
        for j, chip in enumerate(chips):
            copy(1 + j, (*chip, c), me).wait_recv()
            passed[j].start()
        copy(0, sibling, me).wait_recv()
        for j, chip in enumerate(chips):
            copy(4 + j, (*chip, 1 - c), me).wait_recv()
        for cp in first + passed:
            cp.wait_send()
        mine.wait()

    return start, finish


def _exchange_steps(g_ref, land_ref, send_sems, recv_sems, local_sem, per_device, base=0):
    x, y, c = _mesh_pos()
    me = 4 * x + 2 * y + c

    def row(j):
        return g_ref.at[j if per_device else 0]

    def peer(k):
        p = (x ^ ((k >> 2) & 1), y ^ ((k >> 1) & 1), c ^ (k & 1))
        return p, 4 * p[0] + 2 * p[1] + p[2]

    def copy(k, src, dst):
        return pltpu.make_async_remote_copy(
            src_ref=src, dst_ref=dst, send_sem=send_sems.at[base + k - 1], recv_sem=recv_sems.at[base + k - 1],
            device_id=peer(k)[0], device_id_type=pl.DeviceIdType.MESH)

    own = pltpu.make_async_copy(row(me), land_ref.at[me], local_sem)
    sends = [copy(k, row(peer(k)[1]), land_ref.at[me]) for k in range(1, N_DEV)]

    def start():
        own.start()
        for cp in sends:
            cp.start()

    def finish():
        for k in range(1, N_DEV):
            copy(k, row(me), land_ref.at[peer(k)[1]]).wait_recv()
        for cp in sends:
            cp.wait_send()
        own.wait()

    return start, finish


_COMM_SCRATCH = [pltpu.SemaphoreType.DMA((7,)), pltpu.SemaphoreType.DMA((7,)), pltpu.SemaphoreType.DMA]
_ANY = pl.BlockSpec(memory_space=pl.ANY)


def _all_gather(shard):
    def body(x_ref, out_ref, send_sems, recv_sems, local_sem):
        start, finish = _gather_steps(x_ref, out_ref, send_sems, recv_sems, local_sem)
        start()
        finish()

    return pl.pallas_call(
        body, name="all_gather_weights", out_shape=jax.ShapeDtypeStruct((N_DEV,) + shard.shape, shard.dtype),
        in_specs=[_ANY], out_specs=_ANY, scratch_shapes=list(_COMM_SCRATCH),
    )(shard)


def _exchange_grads(packs):
    n = len(packs)

    def body(*refs):
        g_refs, land_refs = refs[:n], refs[n:2 * n]
        send_sems, recv_sems, local_sems = refs[2 * n:]
        steps = [_exchange_steps(g_refs[t], land_refs[t], send_sems, recv_sems, local_sems.at[t],
                                 packs[t].shape[0] == N_DEV, base=7 * t) for t in range(n)]
        for start, _ in steps:
            start()
        for _, finish in steps:
            finish()

    return pl.pallas_call(
        body, name="exchange_grads",
        out_shape=tuple(jax.ShapeDtypeStruct((N_DEV,) + g.shape[1:], g.dtype) for g in packs),
        in_specs=[_ANY] * n, out_specs=(_ANY,) * n,
        scratch_shapes=[pltpu.SemaphoreType.DMA((7 * n,)), pltpu.SemaphoreType.DMA((7 * n,)),
                        pltpu.SemaphoreType.DMA((n,))],
    )(*packs)


def _in_proj(x, gain, w, rest):
    seq = x.shape[0]
    tm = _tile(seq, 256)
    nsteps = seq // tm

    def body(x_ref, g_ref, w_ref, rest_ref, pa_ref, pr_ref, pg_ref, h_ref, gath_ref, send_sems, recv_sems, local_sem):
        start, finish = _gather_steps(rest_ref, gath_ref, send_sems, recv_sems, local_sem)
        pl.when(pl.program_id(0) == 0)(start)
        n, _ = _rms(x_ref[...])
        h = _bf(n * g_ref[...])
        h_ref[...] = h
        pa_ref[...] = _mm(h, w_ref[:, 0:PA_W])
        pr_ref[...] = _mm(h, w_ref[:, PA_W:PA_W + PR_W])
        pg_ref[...] = _mm(h, w_ref[:, PA_W + PR_W:IN_W])
        pl.when(pl.program_id(0) == nsteps - 1)(finish)

    row = lambda w_: pl.BlockSpec((tm, w_), lambda i: (i, 0))
    return pl.pallas_call(
        body, name="in_proj", grid=(nsteps,),
        in_specs=[row(D), _full((1, D)), _full((D, IN_W)), _ANY],
        out_specs=(row(PA_W), row(PR_W), row(PG_W), row(D), _ANY),
        out_shape=(jax.ShapeDtypeStruct((seq, PA_W), F32), jax.ShapeDtypeStruct((seq, PR_W), F32),
                   jax.ShapeDtypeStruct((seq, PG_W), F32), jax.ShapeDtypeStruct((seq, D), _MXU),
                   jax.ShapeDtypeStruct((N_DEV,) + rest.shape, rest.dtype)),
        scratch_shapes=list(_COMM_SCRATCH),
    )(x, gain, w, rest)


def _qk_prep(pa, pr, gq, gk, seg, ca, sa, cr, sr):
    seq = pa.shape[0]
    tm = _tile(seq, 256)

    def body(pa_ref, pr_ref, gq_ref, gk_ref, seg_ref, ca_ref, sa_ref, cr_ref, sr_ref,
             qh_ref, kh_ref, v_ref, rq_ref, rk_ref):
        q = pa_ref[:, 0:AQ_W]
        k = pa_ref[:, AQ_W:AQ_W + AKV_W]
        v_ref[...] = _bf(pa_ref[:, AQ_W + AKV_W:PA_W])
        ca_, sa_ = ca_ref[...], sa_ref[...]
        msq = jnp.dot(q * q, seg_ref[...], precision=HIGHEST, preferred_element_type=F32)
        qn = q * lax.rsqrt(msq + EPS) * gq_ref[...]
        qh_ref[...] = _bf(_rope(qn, _cat(ca_, 4), _cat(sa_, 4), A_HD // 2) * (A_HD ** -0.5))
        msk = jnp.dot(k * k, seg_ref[0:AKV_W, 0:AKV_W], precision=HIGHEST, preferred_element_type=F32)
        kn = k * lax.rsqrt(msk + EPS) * gk_ref[...]
        kh_ref[...] = _bf(_rope(kn, ca_, sa_, A_HD // 2))
        cr_, sr_ = _cat(cr_ref[...], 4), _cat(sr_ref[...], 4)
        rq_ref[...] = _rope(pr_ref[:, 0:R_W], cr_, sr_, R_HD // 2) * (R_HD ** -0.5)
        rk_ref[...] = _rope(pr_ref[:, R_W:2 * R_W], cr_, sr_, R_HD // 2)

    row = lambda w_: pl.BlockSpec((tm, w_), lambda i: (i, 0))
    return pl.pallas_call(
        body, name="qk_prep", grid=(seq // tm,),
        in_specs=[row(PA_W), row(2 * R_W), _full((1, AQ_W)), _full((1, AKV_W)), _full((AQ_W, AQ_W)),
                  row(LANES), row(LANES), row(LANES), row(LANES)],
        out_specs=(row(AQ_W), row(AKV_W), row(AKV_W), row(R_W), row(R_W)),
        out_shape=(jax.ShapeDtypeStruct((seq, AQ_W), _MXU), jax.ShapeDtypeStruct((seq, AKV_W), _MXU),
                   jax.ShapeDtypeStruct((seq, AKV_W), _MXU), jax.ShapeDtypeStruct((seq, R_W), F32),
                   jax.ShapeDtypeStruct((seq, R_W), F32)),
    )(pa, pr, gq, gk, seg, ca, sa, cr, sr)


def _chunk_t(a, tk):
    seq = a.shape[0]
    return a.reshape(seq // tk, tk, a.shape[1]).transpose(0, 2, 1)


def _heads_to_rows(t):
    return t.transpose(2, 0, 1).reshape(t.shape[2], AQ_W)


def _attn_fwd(qt8, k2, vta, tq, tk):
    seq = k2.shape[0]
    nck = seq // tk
    rows = A_G * tq
    vrows = vta.shape[2]
    rb = _tile(tk, 64)
    assert nck % 2 == 0, nck

    def body(qt_ref, k_ref, vt_ref, o_ref, lse_ref, m_sc, acc_sc, qtp_sc, s_a, s_b, p_a, p_b, al_a, al_b):
        g = pl.program_id(0)
        qtp_sc[...] = jnp.zeros_like(qtp_sc)
        qtp_sc[pl.ds(pl.multiple_of(g * A_HD, A_HD), A_HD), :] = jnp.concatenate(
            [qt_ref[a] for a in range(A_G)], axis=1)
        m_sc[...] = jnp.full((1, rows), -jnp.inf, F32)
        acc_sc[...] = jnp.zeros_like(acc_sc)

        def scores(c):
            kc = k_ref[pl.ds(pl.multiple_of(c * tk, tk), tk), :]
            return _mm(kc, qtp_sc[...])

        def stage(c, s_cur, s_nxt, p_cur, p_prv, al_cur, al_prv, first=False, last=False):
            if not last:
                s_nxt[...] = scores(c + 1)
            if not first:
                acc_sc[...] = al_prv[...] * acc_sc[...] + _mm(vt_ref[0, c - 1], p_prv[...])
            m_old = m_sc[...]
            mx = None
            for r in range(0, tk, rb):
                bm = jnp.max(s_cur[r:r + rb, :].reshape(rb // 8, 8, rows), axis=0)
                mx = bm if mx is None else jnp.maximum(mx, bm)
            m_new = jnp.maximum(m_old, jnp.max(mx, axis=0, keepdims=True))
            for r in range(0, tk, rb):
                p_cur[r:r + rb, :] = _bf(jnp.exp(s_cur[r:r + rb, :] - m_new))
            al_cur[...] = jnp.exp(m_old - m_new)
            m_sc[...] = m_new

        s_a[...] = scores(0)
        stage(0, s_a, s_b, p_a, p_b, al_a, al_b, first=True)

        def pair(j, carry):
            stage(2 * j + 1, s_b, s_a, p_b, p_a, al_b, al_a)
            stage(2 * j + 2, s_a, s_b, p_a, p_b, al_a, al_b)
            return carry

        lax.fori_loop(0, nck // 2 - 1, pair, 0)
        stage(nck - 1, s_b, s_a, p_b, p_a, al_b, al_a, last=True)
        acc = al_b[...] * acc_sc[...] + _mm(vt_ref[0, nck - 1], p_b[...])
        l = acc[A_HD:A_HD + 1, :]
        lse = m_sc[...] + jnp.log(l)
        out = acc[0:A_HD, :] * (1.0 / l)
        for a in range(A_G):
            o_ref[a] = out[:, a * tq:(a + 1) * tq]
            lse_ref[a] = lse[:, a * tq:(a + 1) * tq]

    return pl.pallas_call(
        body, name="attn_fwd", grid=(A_KV, seq // tq),
        in_specs=[pl.BlockSpec((A_G, A_HD, tq), lambda g, i: (g, 0, i)),
                  _full((seq, LANES)), pl.BlockSpec((1, nck, vrows, tk), lambda g, i: (g, 0, 0, 0))],
        out_specs=(pl.BlockSpec((A_G, A_HD, tq), lambda g, i: (g, 0, i)),
                   pl.BlockSpec((A_G, 1, tq), lambda g, i: (g, 0, i))),
        out_shape=(jax.ShapeDtypeStruct((A_H, A_HD, seq), F32), jax.ShapeDtypeStruct((A_H, 1, seq), F32)),
        scratch_shapes=[pltpu.VMEM((1, rows), F32), pltpu.VMEM((vrows, rows), F32), pltpu.VMEM((LANES, rows), _MXU),
                        pltpu.VMEM((tk, rows), F32), pltpu.VMEM((tk, rows), F32),
                        pltpu.VMEM((tk, rows), _MXU), pltpu.VMEM((tk, rows), _MXU),
                        pltpu.VMEM((1, rows), F32), pltpu.VMEM((1, rows), F32)],
    )(qt8, k2, vta)


def _attn_bwd(qt8, ot, dot_, lse, k2, v2, k2t, tq, tk, ksplit):
    seq = k2.shape[0]
    sh = seq // ksplit
    nck = sh // tk
    rows = A_G * tq
    rb = _tile(tk, 32768 // rows)
    assert nck % 2 == 0, nck

    def body(qt_ref, ot_ref, dot_ref, lse_ref, k_ref, v_ref, kt_ref,
             dk_ref, dv_ref, dq_ref, dq_sc, qtp_sc, dotp_sc, pt_sc, dst_sc,
             s_a, s_b, dp_a, dp_b, p_a, p_b, ds_a, ds_b):
        g = pl.program_id(1)
        hrows = pl.ds(pl.multiple_of(g * A_HD, A_HD), A_HD)

        @pl.when(pl.program_id(2) == 0)
        def _():
            dk_ref[...] = jnp.zeros_like(dk_ref)
            dv_ref[...] = jnp.zeros_like(dv_ref)

        lse_row = jnp.concatenate([lse_ref[a] for a in range(A_G)], axis=1)
        dd = jnp.concatenate([jnp.sum(ot_ref[a] * dot_ref[a], axis=0, keepdims=True)
                              for a in range(A_G)], axis=1)
        qtp_sc[...] = jnp.zeros_like(qtp_sc)
        dotp_sc[...] = jnp.zeros_like(dotp_sc)
        qtp_sc[hrows, :] = jnp.concatenate([qt_ref[a] for a in range(A_G)], axis=1)
        dotp_sc[hrows, :] = _bf(jnp.concatenate([dot_ref[a] for a in range(A_G)], axis=1))
        dq_sc[...] = jnp.zeros_like(dq_sc)

        def products(c, s_ref, dp_ref):
            sl = pl.ds(pl.multiple_of(c * tk, tk), tk)
            s_ref[...] = _mm(k_ref[sl, :], qtp_sc[...])
            dp_ref[...] = _mm(v_ref[sl, :], dotp_sc[...])

        def accumulate(c, p_ref, ds_ref):
            pt_sc[...] = p_ref[...].T
            dst_sc[...] = ds_ref[...].T
            dq_sc[...] += _mm(kt_ref[c, hrows, :], ds_ref[...])
            dv_ref[0, c] += _mm(dotp_sc[hrows, :], pt_sc[...])
            dk_ref[0, c] += _mm(qtp_sc[hrows, :], dst_sc[...])

        def stage(c, s_cur, dp_cur, s_nxt, dp_nxt, p_cur, ds_cur, p_prv, ds_prv, first=False, last=False):
            if not last:
                products(c + 1, s_nxt, dp_nxt)
            if not first:
                accumulate(c - 1, p_prv, ds_prv)
            for r in range(0, tk, rb):
                p = jnp.exp(s_cur[r:r + rb, :] - lse_row)
                p_cur[r:r + rb, :] = _bf(p)
                ds_cur[r:r + rb, :] = _bf(p * (dp_cur[r:r + rb, :] - dd))

        products(0, s_a, dp_a)
        stage(0, s_a, dp_a, s_b, dp_b, p_a, ds_a, p_b, ds_b, first=True)

        def pair(j, carry):
            stage(2 * j + 1, s_b, dp_b, s_a, dp_a, p_b, ds_b, p_a, ds_a)
            stage(2 * j + 2, s_a, dp_a, s_b, dp_b, p_a, ds_a, p_b, ds_b)
            return carry

        lax.fori_loop(0, nck // 2 - 1, pair, 0)
        stage(nck - 1, s_b, dp_b, s_a, dp_a, p_b, ds_b, p_a, ds_a, last=True)
        accumulate(nck - 1, p_b, ds_b)
        for a in range(A_G):
            dq_ref[0, a] = dq_sc[:, a * tq:(a + 1) * tq]

    tspec = pl.BlockSpec((A_G, A_HD, tq), lambda s, g, i: (g, 0, i))
    kspec = pl.BlockSpec((sh, LANES), lambda s, g, i: (s, 0))
    gspec = pl.BlockSpec((1, nck, A_HD, tk), lambda s, g, i: (g, s, 0, 0))
    gshape = jax.ShapeDtypeStruct((A_KV, seq // tk, A_HD, tk), F32)
    big = lambda dt: pltpu.VMEM((tk, rows), dt)
    bigt = pltpu.VMEM((rows, tk), _MXU)
    return pl.pallas_call(
        body, name="attn_bwd", grid=(ksplit, A_KV, seq // tq),
        in_specs=[tspec, tspec, tspec, pl.BlockSpec((A_G, 1, tq), lambda s, g, i: (g, 0, i)),
                  kspec, kspec, pl.BlockSpec((nck, LANES, tk), lambda s, g, i: (s, 0, 0))],
        out_specs=(gspec, gspec, pl.BlockSpec((1, A_G, A_HD, tq), lambda s, g, i: (s, g, 0, i))),
        out_shape=(gshape, gshape, jax.ShapeDtypeStruct((ksplit, A_H, A_HD, seq), F32)),
        scratch_shapes=[pltpu.VMEM((A_HD, rows), F32), pltpu.VMEM((LANES, rows), _MXU), pltpu.VMEM((LANES, rows), _MXU),
                        bigt, bigt,
                        big(F32), big(F32), big(F32), big(F32), big(_MXU), big(_MXU), big(_MXU), big(_MXU)],
    )(qt8, ot, dot_, lse, k2, v2, k2t)


def _chunks_to_rows(t):
    return t.transpose(1, 3, 0, 2).reshape(t.shape[1] * t.shape[3], AKV_W)


def _ret_tables(zb):
    c = CHUNK

    def body(z_ref, m_ref, mw_ref, qd_ref, qdw_ref, kd_ref, kdw_ref, g_ref, gw_ref):
        fwd = pl.program_id(0) < R_H
        z = z_ref[0]
        lam = jnp.minimum(z, 0.0) - jnp.log(1.0 + jnp.exp(-jnp.abs(z)))
        i = lax.broadcasted_iota(jnp.int32, (c, c), 0).astype(F32)
        j = lax.broadcasted_iota(jnp.int32, (c, c), 1).astype(F32)
        diff = jnp.where(fwd, i - j, j - i)
        keep = diff >= jnp.where(fwd, 0.0, 1.0)
        dist = jnp.maximum(diff, 0.0)
        m = jnp.where(keep, jnp.exp(lam * dist), 0.0)
        m_ref[0] = m
        mw_ref[0] = m * dist
        fq = jnp.where(fwd, i + 1.0, c - i)
        qd = jnp.exp(lam * fq)
        qd_ref[0] = qd
        qdw_ref[0] = qd * fq
        fk = jnp.where(fwd, c - 1.0 - i, i)
        kd = jnp.exp(lam * fk)
        kd_ref[0] = kd
        kdw_ref[0] = kd * fk
        gdec = jnp.exp(lam * c)
        g_ref[0] = gdec
        gw_ref[0] = gdec * c

    big = pl.BlockSpec((1, c, c), lambda t: (t, 0, 0))
    vec = pl.BlockSpec((1, 1, LANES), lambda t: (t, 0, 0))
    bshape = jax.ShapeDtypeStruct((2 * R_H, c, c), F32)
    vshape = jax.ShapeDtypeStruct((2 * R_H, 1, LANES), F32)
    return pl.pallas_call(
        body, name="ret_tables", grid=(2 * R_H,), in_specs=[vec],
        out_specs=(big, big, big, big, big, big, vec, vec),
        out_shape=(bshape,) * 6 + (vshape, vshape),
    )(zb)


def _ret_fwd(rq, rk, pr, m, qd, kd, gdec, cb):
    seq = rq.shape[0]
    c = CHUNK
    ns = seq // (cb * c)

    def body(q_ref, k_ref, v_ref, m_ref, qd_ref, kd_ref, g_ref, y_ref, pst_ref, p_sc):
        d = pl.program_id(0)

        @pl.when(pl.program_id(1) == 0)
        def _():
            p_sc[...] = jnp.zeros_like(p_sc)

        def chunk(j, carry):
            cc = jnp.where(d == 0, j, cb - 1 - j)
            sl = pl.ds(pl.multiple_of(cc * c, c), c)
            heads = [slice(h * R_HD, (h + 1) * R_HD) for h in range(R_H)]
            qk = [_mm_nt(q_ref[sl, hs], k_ref[sl, hs]) for hs in heads]
            qp = [_mm(q_ref[sl, hs] * qd_ref[h], p_sc[h]) for h, hs in enumerate(heads)]
            kv = [_mm_tn(k_ref[sl, hs] * kd_ref[h], v_ref[sl, hs]) for h, hs in enumerate(heads)]
            for h, hs in enumerate(heads):
                p = p_sc[h]
                pst_ref[h, cc] = p
                y_ref[0, sl, hs] = _mm(qk[h] * m_ref[h], v_ref[sl, hs]) + qp[h]
                p_sc[h] = p * g_ref[h] + kv[h]
            return carry

        lax.fori_loop(0, cb, chunk, 0)

    def step(d, n):
        return d * (ns - 1 - n) + (1 - d) * n

    blk = lambda off: pl.BlockSpec((cb * c, R_W), lambda d, n: (step(d, n), off))
    big = pl.BlockSpec((R_H, c, c), lambda d, n: (d, 0, 0))
    vec = pl.BlockSpec((R_H, 1, LANES), lambda d, n: (d, 0, 0))
    return pl.pallas_call(
        body, name="ret_fwd", grid=(2, ns),
        in_specs=[blk(0), blk(0), blk(2), big, big, big, vec],
        out_specs=(pl.BlockSpec((1, cb * c, R_W), lambda d, n: (d, step(d, n), 0)),
                   pl.BlockSpec((R_H, cb, R_HD, R_HD), lambda d, n: (d, step(d, n), 0, 0))),
        out_shape=(jax.ShapeDtypeStruct((2, seq, R_W), F32),
                   jax.ShapeDtypeStruct((2 * R_H, seq // c, R_HD, R_HD), F32)),
        scratch_shapes=[pltpu.VMEM((R_H, R_HD, R_HD), F32)],
    )(rq, rk, pr, m, qd, kd, gdec)


def _ret_bwd(rq, rk, pr, dry, pst, m, mw, qd, qdw, kd, kdw, gdec, gw, cb):
    seq = rq.shape[0]
    c = CHUNK
    ns = seq // (cb * c)

    def body(q_ref, k_ref, v_ref, dy_ref, pst_ref, m_ref, mw_ref, qd_ref, qdw_ref, kd_ref, kdw_ref,
             g_ref, gw_ref, dq_ref, dk_ref, dv_ref, dlam_ref, r_sc, acc_sc, e_sc, g_sc):
        d = pl.program_id(0)
        n = pl.program_id(1)

        @pl.when(n == 0)
        def _():
            r_sc[...] = jnp.zeros_like(r_sc)
            acc_sc[...] = jnp.zeros_like(acc_sc)
            e_sc[...] = jnp.zeros_like(e_sc)
            g_sc[...] = jnp.zeros_like(g_sc)

        def chunk(j, carry):
            cc = jnp.where(d == 0, cb - 1 - j, j)
            sl = pl.ds(pl.multiple_of(cc * c, c), c)
            heads = [slice(h * R_HD, (h + 1) * R_HD) for h in range(R_H)]
            first = []
            for h, hs in enumerate(heads):
                q, k, v, dy = q_ref[sl, hs], k_ref[sl, hs], v_ref[sl, hs], dy_ref[sl, hs]
                r = r_sc[h]
                first.append((_mm_nt(q, k), _mm_nt(dy, v), _mm_nt(dy, pst_ref[h, cc]), _mm_nt(v, r),
                              _mm(k * kd_ref[h], r), _mm_tn(q * qd_ref[h], dy)))
            for h, hs in enumerate(heads):
                qk, ds, dyp, vr, kr, qdy = first[h]
                q, k, dy = q_ref[sl, hs], k_ref[sl, hs], dy_ref[sl, hs]
                r = r_sc[h]
                da = ds * m_ref[h]
                dv_ref[0, sl, hs] = _mm_tn(qk * m_ref[h], dy) + kr
                dq_ref[0, sl, hs] = _mm(da, k) + dyp * qd_ref[h]
                dk_ref[0, sl, hs] = _mm_tn(da, q) + vr * kd_ref[h]
                acc_sc[h] += dyp * q * qdw_ref[h] + vr * k * kdw_ref[h]
                e_sc[h] += ds * qk * mw_ref[h]
                g_sc[h] += r * pst_ref[h, cc]
                r_sc[h] = r * g_ref[h] + qdy
            return carry

        lax.fori_loop(0, cb, chunk, 0)

        @pl.when(n == ns - 1)
        def _():
            for h in range(R_H):
                tot = jnp.sum(jnp.sum(acc_sc[h] + e_sc[h] + g_sc[h] * gw_ref[h], axis=0, keepdims=True),
                              axis=1, keepdims=True)
                dlam_ref[h] = jnp.broadcast_to(tot, (1, LANES))

    def step(d, n):
        return d * n + (1 - d) * (ns - 1 - n)

    blk = lambda off: pl.BlockSpec((cb * c, R_W), lambda d, n: (step(d, n), off))
    big = pl.BlockSpec((R_H, c, c), lambda d, n: (d, 0, 0))
    vec = pl.BlockSpec((R_H, 1, LANES), lambda d, n: (d, 0, 0))
    out = pl.BlockSpec((1, cb * c, R_W), lambda d, n: (d, step(d, n), 0))
    oshape = jax.ShapeDtypeStruct((2, seq, R_W), F32)
    sq = pltpu.VMEM((R_H, R_HD, R_HD), F32)
    return pl.pallas_call(
        body, name="ret_bwd", grid=(2, ns),
        in_specs=[blk(0), blk(0), blk(2), blk(0),
                  pl.BlockSpec((R_H, cb, R_HD, R_HD), lambda d, n: (d, step(d, n), 0, 0)),
                  big, big, big, big, big, big, vec, vec],
        out_specs=(out, out, out, vec),
        out_shape=(oshape, oshape, oshape, jax.ShapeDtypeStruct((2 * R_H, 1, LANES), F32)),
        scratch_shapes=[sq, sq, sq, sq],
    )(rq, rk, pr, dry, pst, m, mw, qd, qdw, kd, kdw, gdec, gw)


def _group_norm(ry):
    yn, rs = [], []
    for h in range(R_H):
        s = ry[:, h * R_HD:(h + 1) * R_HD]
        mu = jnp.mean(s, axis=-1, keepdims=True)
        cen = s - mu
        r = lax.rsqrt(jnp.mean(cen * cen, axis=-1, keepdims=True) + GN_EPS)
        yn.append(cen * r)
        rs.append(r)
    return yn, rs


def _merge_fwd(x, o, y2, pr, pg, gain_r, wao, wro, wout):
    seq = x.shape[0]
    tm = _tile(seq, 256)

    def body(x_ref, o_ref, yf_ref, yb_ref, rg_ref, ga_ref, gr_ref, gn_ref, wao_ref, wro_ref, wout_ref,
             x1_ref, mg_ref, ri_ref):
        yn, _ = _group_norm(yf_ref[0] + yb_ref[0])
        rg = rg_ref[...]
        ret_in = jnp.concatenate(yn, axis=1) * gn_ref[...] * (rg * _sigmoid(rg))
        ri_ref[...] = _bf(ret_in)
        attn_out = _mm(o_ref[...], wao_ref[...])
        ret_out = _mm(ret_in, wro_ref[...])
        merged = _sigmoid(ga_ref[...]) * attn_out + _sigmoid(gr_ref[...]) * ret_out
        mg_ref[...] = _bf(merged)
        x1_ref[...] = x_ref[...] + _mm(merged, wout_ref[...])

    row = lambda w_, j=0: pl.BlockSpec((tm, w_), lambda i: (i, j))
    ydir = lambda d: pl.BlockSpec((1, tm, R_W), lambda i: (d, i, 0))
    return pl.pallas_call(
        body, name="merge_fwd", grid=(seq // tm,),
        in_specs=[row(D), row(AQ_W), ydir(0), ydir(1), row(R_W, 3), row(D, 0), row(D, 1),
                  _full((1, R_W)), _full((AQ_W, D)), _full((R_W, D)), _full((D, D))],
        out_specs=(row(D), row(D), row(R_W)),
        out_shape=(jax.ShapeDtypeStruct((seq, D), F32), jax.ShapeDtypeStruct((seq, D), _MXU),
                   jax.ShapeDtypeStruct((seq, R_W), _MXU)),
    )(x, o, y2, y2, pr, pg, pg, gain_r, wao, wro, wout)


def _mlp_fwd(x1, gain, wup, wdown):
    seq = x1.shape[0]
    tm = _tile(seq, 512)
    fc = 2048
    nfc = FF // fc

    def body(x_ref, g_ref, wu_ref, wd_ref, x2_ref, hm_sc, acc_sc):
        c = pl.program_id(1)

        @pl.when(c == 0)
        def _():
            n, _ = _rms(x_ref[...])
            hm_sc[...] = _bf(n * g_ref[...])
            acc_sc[...] = jnp.zeros_like(acc_sc)

        halves = (slice(0, fc // 2), slice(fc // 2, fc))
        ups = [jnp.maximum(_mm(hm_sc[...], wu_ref[:, hs]), 0.0) for hs in halves]
        acc_sc[...] += _mm(ups[0] * ups[0], wd_ref[halves[0], :]) + _mm(ups[1] * ups[1], wd_ref[halves[1], :])

        @pl.when(c == nfc - 1)
        def _():
            x2_ref[...] = x_ref[...] + acc_sc[...]

    return pl.pallas_call(
        body, name="mlp_fwd", grid=(seq // tm, nfc),
        in_specs=[pl.BlockSpec((tm, D), lambda i, c: (i, 0)), pl.BlockSpec((1, D), lambda i, c: (0, 0)),
                  pl.BlockSpec((D, fc), lambda i, c: (0, c)), pl.BlockSpec((fc, D), lambda i, c: (c, 0))],
        out_specs=pl.BlockSpec((tm, D), lambda i, c: (i, 0)),
        out_shape=jax.ShapeDtypeStruct((seq, D), F32),
        scratch_shapes=[pltpu.VMEM((tm, D), _MXU), pltpu.VMEM((tm, D), F32)],
    )(x1, gain, wup, wdown)


def _ple_loss(x2, p, tgt, g_ple, g_fin, wpg, wpgt, wple):
    seq = x2.shape[0]
    tm = _tile(seq, 256)

    def body(x2_ref, p_ref, t_ref, gp_ref, gf_ref, wpg_ref, wpgt_ref, wple_ref,
             dx2_ref, de_ref, dz_ref, hp_ref, loss_ref, dgf_ref, dgp_ref):
        @pl.when(pl.program_id(0) == 0)
        def _():
            loss_ref[...] = jnp.zeros_like(loss_ref)
            dgf_ref[...] = jnp.zeros_like(dgf_ref)
            dgp_ref[...] = jnp.zeros_like(dgp_ref)

        x2 = x2_ref[...]
        gp, gf = gp_ref[...], gf_ref[...]
        n2, r2 = _rms(x2)
        hp = _bf(n2 * gp)
        hp_ref[...] = hp
        gate = _sigmoid(_mm(hp, wpg_ref[...]))
        e = _mm(p_ref[...], wple_ref[...])
        x3 = x2 + gate * e
        n3, r3 = _rms(x3)
        diff = n3 * gf - t_ref[...]
        row_loss = jnp.mean(diff * diff, axis=-1, keepdims=True)
        loss_ref[...] += 0.5 * jnp.sum(row_loss, axis=0, keepdims=True)
        dy = diff * (1.0 / D)
        dgf_ref[...] += jnp.sum(dy * n3, axis=0, keepdims=True)
        dx3 = _rms_bwd(n3, r3, gf, dy)
        de_ref[...] = _bf(dx3 * gate)
        dz = dx3 * e * gate * (1.0 - gate)
        dz_ref[...] = _bf(dz)
        dhp = _mm(dz, wpgt_ref[...])
        dgp_ref[...] += jnp.sum(dhp * n2, axis=0, keepdims=True)
        dx2_ref[...] = dx3 + _rms_bwd(n2, r2, gp, dhp)

    row = lambda w_: pl.BlockSpec((tm, w_), lambda i: (i, 0))
    act = lambda dt: jax.ShapeDtypeStruct((seq, D), dt)
    return pl.pallas_call(
        body, name="ple_loss", grid=(seq // tm,),
        in_specs=[row(D), row(PLE), row(D), _full((1, D)), _full((1, D)),
                  _full((D, D)), _full((D, D)), _full((PLE, D))],
        out_specs=(row(D), row(D), row(D), row(D), _full((1, LANES)), _full((1, D)), _full((1, D))),
        out_shape=(act(F32), act(_MXU), act(_MXU), act(_MXU), jax.ShapeDtypeStruct((1, LANES), F32),
                   jax.ShapeDtypeStruct((1, D), F32), jax.ShapeDtypeStruct((1, D), F32)),
    )(x2, p, tgt, g_ple, g_fin, wpg, wpgt, wple)


def _mlp_bwd(x1, dx2, gain, wup, wdownt, wupt):
    seq = x1.shape[0]
    tm = _tile(seq, 512)
    fc = 2048
    nfc = FF // fc

    def body(x_ref, dx2_ref, g_ref, wu_ref, wdt_ref, wut_ref,
             dx1_ref, a_ref, du_ref, hm_ref, dg_ref, dhm_sc):
        i = pl.program_id(0)
        c = pl.program_id(1)

        @pl.when((i == 0) & (c == 0))
        def _():
            dg_ref[...] = jnp.zeros_like(dg_ref)

        @pl.when(c == 0)
        def _():
            n, _ = _rms(x_ref[...])
            hm_ref[...] = _bf(n * g_ref[...])
            dhm_sc[...] = jnp.zeros_like(dhm_sc)

        halves = (slice(0, fc // 2), slice(fc // 2, fc))
        ups = [jnp.maximum(_mm(hm_ref[...], wu_ref[:, hs]), 0.0) for hs in halves]
        das = [_mm(dx2_ref[...], wdt_ref[:, hs]) for hs in halves]
        part = None
        for u, da, hs in zip(ups, das, halves):
            a_ref[:, hs] = _bf(u * u)
            du = _bf(da * (2.0 * u))
            du_ref[:, hs] = du
            t = _mm(du, wut_ref[hs, :])
            part = t if part is None else part + t
        dhm_sc[...] += part

        @pl.when(c == nfc - 1)
        def _():
            n, r = _rms(x_ref[...])
            dhm = dhm_sc[...]
            dg_ref[...] += jnp.sum(dhm * n, axis=0, keepdims=True)
            dx1_ref[...] = dx2_ref[...] + _rms_bwd(n, r, g_ref[...], dhm)

    rowd = pl.BlockSpec((tm, D), lambda i, c: (i, 0))
    rowf = pl.BlockSpec((tm, fc), lambda i, c: (i, c))
    return pl.pallas_call(
        body, name="mlp_bwd", grid=(seq // tm, nfc),
        in_specs=[rowd, rowd, pl.BlockSpec((1, D), lambda i, c: (0, 0)),
                  pl.BlockSpec((D, fc), lambda i, c: (0, c)), pl.BlockSpec((D, fc), lambda i, c: (0, c)),
                  pl.BlockSpec((fc, D), lambda i, c: (c, 0))],
        out_specs=(rowd, rowf, rowf, rowd, pl.BlockSpec((1, D), lambda i, c: (0, 0))),
        out_shape=(jax.ShapeDtypeStruct((seq, D), F32), jax.ShapeDtypeStruct((seq, FF), _MXU),
                   jax.ShapeDtypeStruct((seq, FF), _MXU), jax.ShapeDtypeStruct((seq, D), _MXU),
                   jax.ShapeDtypeStruct((1, D), F32)),
        scratch_shapes=[pltpu.VMEM((tm, D), F32)],
    )(x1, dx2, gain, wup, wdownt, wupt)


def _merge_bwd(dx1, o, y2, pr, pg, gain_r, wao, wro, woutt, waot, wrot, gpack):
    seq = dx1.shape[0]
    tm = _tile(seq, 256)
    nsteps = seq // tm

    def body(dx1_ref, o_ref, yf_ref, yb_ref, rg_ref, ga_ref, gr_ref, gn_ref, wao_ref, wro_ref,
             woutt_ref, waot_ref, wrot_ref, gpack_ref,
             dpg_ref, dao_ref, dro_ref, do_ref, dry_ref, drg_ref, dgn_ref, land_ref,
             send_sems, recv_sems, local_sem):
        start, finish = _exchange_steps(gpack_ref, land_ref, send_sems, recv_sems, local_sem, True)
        pl.when(pl.program_id(0) == 0)(start)

        @pl.when(pl.program_id(0) == 0)
        def _():
            dgn_ref[...] = jnp.zeros_like(dgn_ref)

        yn_l, rs_l = _group_norm(yf_ref[0] + yb_ref[0])
        yn = jnp.concatenate(yn_l, axis=1)
        rg = rg_ref[...]
        gn = gn_ref[...]
        sg = _sigmoid(rg)
        sil = rg * sg
        ret_in = yn * gn * sil
        attn_out = _mm(o_ref[...], wao_ref[...])
        ret_out = _mm(ret_in, wro_ref[...])
        sa = _sigmoid(ga_ref[...])
        sr = _sigmoid(gr_ref[...])
        dm = _mm(dx1_ref[...], woutt_ref[...])
        dpg_ref[:, 0:D] = _bf(dm * attn_out * sa * (1.0 - sa))
        dpg_ref[:, D:2 * D] = _bf(dm * ret_out * sr * (1.0 - sr))
        dao = _bf(dm * sa)
        dro = _bf(dm * sr)
        dao_ref[...] = dao
        dro_ref[...] = dro
        do_ref[...] = _mm(dao, waot_ref[...])
        dri = _mm(dro, wrot_ref[...])
        dgn_ref[...] += jnp.sum(dri * yn * sil, axis=0, keepdims=True)
        drg_ref[...] = _bf(dri * yn * gn * (sg * (1.0 + rg * (1.0 - sg))))
        dyn = dri * gn * sil
        dry = []
        for h in range(R_H):
            dh = dyn[:, h * R_HD:(h + 1) * R_HD]
            dry.append(rs_l[h] * (dh - jnp.mean(dh, axis=-1, keepdims=True)
                                  - yn_l[h] * jnp.mean(dh * yn_l[h], axis=-1, keepdims=True)))
        dry_ref[...] = jnp.concatenate(dry, axis=1)
        pl.when(pl.program_id(0) == nsteps - 1)(finish)

    row = lambda w_, j=0: pl.BlockSpec((tm, w_), lambda i: (i, j))
    ydir = lambda d: pl.BlockSpec((1, tm, R_W), lambda i: (d, i, 0))
    return pl.pallas_call(
        body, name="merge_bwd", grid=(nsteps,),
        in_specs=[row(D), row(AQ_W), ydir(0), ydir(1), row(R_W, 3), row(D, 0), row(D, 1),
                  _full((1, R_W)), _full((AQ_W, D)), _full((R_W, D)), _full((D, D)),
                  _full((D, AQ_W)), _full((D, R_W)), _ANY],
        out_specs=(row(PG_W), row(D), row(D), row(AQ_W), row(R_W), row(R_W), _full((1, R_W)), _ANY),
        out_shape=(jax.ShapeDtypeStruct((seq, PG_W), _MXU), jax.ShapeDtypeStruct((seq, D), _MXU),
                   jax.ShapeDtypeStruct((seq, D), _MXU), jax.ShapeDtypeStruct((seq, AQ_W), F32),
                   jax.ShapeDtypeStruct((seq, R_W), F32), jax.ShapeDtypeStruct((seq, R_W), _MXU),
                   jax.ShapeDtypeStruct((1, R_W), F32), jax.ShapeDtypeStruct(gpack.shape, gpack.dtype)),
        scratch_shapes=list(_COMM_SCRATCH),
    )(dx1, o, y2, y2, pr, pg, pg, gain_r, wao, wro, woutt, waot, wrot, gpack)


def _qk_prep_bwd(pa, dqh, dk2, dv2, rdq, rdk, rdv, drg, gq, gk, seg, ca, sa, cr, sr):
    seq = pa.shape[0]
    tm = _tile(seq, 256)

    def body(pa_ref, dqh_ref, dk2_ref, dv2_ref, rdqf_ref, rdqb_ref, rdkf_ref, rdkb_ref, rdvf_ref, rdvb_ref,
             drg_ref, gq_ref, gk_ref, seg_ref, ca_ref, sa_ref, cr_ref, sr_ref,
             dpa_ref, dpr_ref, dgq_ref, dgk_ref):
        @pl.when(pl.program_id(0) == 0)
        def _():
            dgq_ref[...] = jnp.zeros_like(dgq_ref)
            dgk_ref[...] = jnp.zeros_like(dgk_ref)

        ca_, sa_ = ca_ref[...], sa_ref[...]

        def norm_bwd(raw, gain, dy, segm, dg_ref):
            msq = jnp.dot(raw * raw, segm, precision=HIGHEST, preferred_element_type=F32)
            r = lax.rsqrt(msq + EPS)
            n = raw * r
            dg_ref[...] += jnp.sum(dy * n, axis=0, keepdims=True)
            dn = dy * gain
            return r * (dn - n * jnp.dot(dn * n, segm, precision=HIGHEST, preferred_element_type=F32))

        dqn = _rope(dqh_ref[...] * (A_HD ** -0.5), _cat(ca_, 4), -_cat(sa_, 4), A_HD // 2)
        dpa_ref[:, 0:AQ_W] = _bf(norm_bwd(pa_ref[:, 0:AQ_W], gq_ref[...], dqn, seg_ref[...], dgq_ref))
        dkn = _rope(dk2_ref[...], ca_, -sa_, A_HD // 2)
        dpa_ref[:, AQ_W:AQ_W + AKV_W] = _bf(norm_bwd(pa_ref[:, AQ_W:AQ_W + AKV_W], gk_ref[...], dkn,
                                                     seg_ref[0:AKV_W, 0:AKV_W], dgk_ref))
        dpa_ref[:, AQ_W + AKV_W:PA_W] = _bf(dv2_ref[...])
        cr_, sr_ = _cat(cr_ref[...], 4), -_cat(sr_ref[...], 4)
        dpr_ref[:, 0:R_W] = _bf(_rope((rdqf_ref[0] + rdqb_ref[0]) * (R_HD ** -0.5), cr_, sr_, R_HD // 2))
        dpr_ref[:, R_W:2 * R_W] = _bf(_rope(rdkf_ref[0] + rdkb_ref[0], cr_, sr_, R_HD // 2))
        dpr_ref[:, 2 * R_W:3 * R_W] = _bf(rdvf_ref[0] + rdvb_ref[0])
        dpr_ref[:, 3 * R_W:4 * R_W] = drg_ref[...]

    row = lambda w_: pl.BlockSpec((tm, w_), lambda i: (i, 0))
    ydir = lambda d: pl.BlockSpec((1, tm, R_W), lambda i: (d, i, 0))
    return pl.pallas_call(
        body, name="qk_prep_bwd", grid=(seq // tm,),
        in_specs=[row(PA_W), row(AQ_W), row(AKV_W), row(AKV_W), ydir(0), ydir(1), ydir(0), ydir(1),
                  ydir(0), ydir(1), row(R_W), _full((1, AQ_W)), _full((1, AKV_W)), _full((AQ_W, AQ_W)),
                  row(LANES), row(LANES), row(LANES), row(LANES)],
        out_specs=(row(PA_W), row(PR_W), _full((1, AQ_W)), _full((1, AKV_W))),
        out_shape=(jax.ShapeDtypeStruct((seq, PA_W), _MXU), jax.ShapeDtypeStruct((seq, PR_W), _MXU),
                   jax.ShapeDtypeStruct((1, AQ_W), F32), jax.ShapeDtypeStruct((1, AKV_W), F32)),
    )(pa, dqh, dk2, dv2, rdq, rdq, rdk, rdk, rdv, rdv, drg, gq, gk, seg, ca, sa, cr, sr)


def _in_proj_bwd(x, dx1, gain, dpa, dpr, dpg, wint, packs):
    seq = x.shape[0]
    tm = _tile(seq, 256)
    nsteps = seq // tm
    npk = len(packs)

    def body(x_ref, dx1_ref, g_ref, dpa_ref, dpr_ref, dpg_ref, wt_ref, *rest):
        pack_refs, (dx_ref, dg_ref), land_refs = rest[:npk], rest[npk:npk + 2], rest[npk + 2:2 * npk + 2]
        send_sems, recv_sems, local_sems = rest[2 * npk + 2:]
        steps = [_exchange_steps(pack_refs[t], land_refs[t], send_sems, recv_sems, local_sems.at[t], True, base=7 * t)
                 for t in range(npk)]

        @pl.when(pl.program_id(0) == 0)
        def _():
            for start, _ in steps:
                start()
            dg_ref[...] = jnp.zeros_like(dg_ref)

        dh = (_mm(dpa_ref[...], wt_ref[0:PA_W, :]) + _mm(dpr_ref[...], wt_ref[PA_W:PA_W + PR_W, :])
              + _mm(dpg_ref[...], wt_ref[PA_W + PR_W:IN_W, :]))
        n, r = _rms(x_ref[...])
        dg_ref[...] += jnp.sum(dh * n, axis=0, keepdims=True)
        dx_ref[...] = dx1_ref[...] + _rms_bwd(n, r, g_ref[...], dh)

        @pl.when(pl.program_id(0) == nsteps - 1)
        def _():
            for _, finish in steps:
                finish()

    row = lambda w_: pl.BlockSpec((tm, w_), lambda i: (i, 0))
    return pl.pallas_call(
        body, name="in_proj_bwd", grid=(nsteps,),
        in_specs=[row(D), row(D), _full((1, D)), row(PA_W), row(PR_W), row(PG_W), _full((IN_W, D))] + [_ANY] * npk,
        out_specs=(row(D), _full((1, D))) + (_ANY,) * npk,
        out_shape=(jax.ShapeDtypeStruct((seq, D), F32), jax.ShapeDtypeStruct((1, D), F32))
        + tuple(jax.ShapeDtypeStruct(g.shape, g.dtype) for g in packs),
        scratch_shapes=[pltpu.SemaphoreType.DMA((7 * npk,)), pltpu.SemaphoreType.DMA((7 * npk,)),
                        pltpu.SemaphoreType.DMA((npk,))],
    )(x, dx1, gain, dpa, dpr, dpg, wint, *packs)


def _wgrad(a, b, name):
    seq, m = a.shape
    n = b.shape[1]
    tm, tn, ts = _tile(m, 1024), _tile(n, 1024), _tile(seq, 2048)
    ns = seq // ts

    def body(a_ref, b_ref, o_ref):
        @pl.when(pl.program_id(2) == 0)
        def _():
            o_ref[...] = jnp.zeros_like(o_ref)

        o_ref[...] += _mm_tn(a_ref[...], b_ref[...])

    return pl.pallas_call(
        body, name=name, grid=(m // tm, n // tn, ns),
        in_specs=[pl.BlockSpec((ts, tm), lambda i, j, s: (s, i)), pl.BlockSpec((ts, tn), lambda i, j, s: (s, j))],
        out_specs=pl.BlockSpec((tm, tn), lambda i, j, s: (i, j)),
        out_shape=jax.ShapeDtypeStruct((m, n), F32),
    )(a, b)


def _adamw_math(w, g, m, v):
    m = B1 * m + (1.0 - B1) * g
    v = B2 * v + (1.0 - B2) * (g * g)
    m_hat = m / (1.0 - B1 ** STEP)
    v_hat = v / (1.0 - B2 ** STEP)
    delta = -LR * (m_hat / (jnp.sqrt(v_hat) + ADAM_EPS) + WD * w)
    return delta, m, v


def _adamw_big(land, own, w, m, v, name):
    rws, cols = w.shape
    tr = next(t for t in range(min(rws, 288) // 16 * 16, 0, -16) if rws % t == 0)

    def body(l_ref, o_ref, w_ref, m_ref, v_ref, g_ref, d_ref, nm_ref, nv_ref):
        x, y, c = _mesh_pos()
        me = 4 * x + 2 * y + c
        g = o_ref[...]
        for j in range(N_DEV):
            g = g + jnp.where(me == j, 0.0, l_ref[j].astype(F32))
        g_ref[...] = g
        d_ref[...], nm_ref[...], nv_ref[...] = _adamw_math(w_ref[...], g, m_ref[...], v_ref[...])

    row = pl.BlockSpec((tr, cols), lambda i: (i, 0))
    shp = jax.ShapeDtypeStruct((rws, cols), F32)
    return pl.pallas_call(
        body, name=name, grid=(rws // tr,),
        in_specs=[pl.BlockSpec((N_DEV, tr, cols), lambda i: (0, i, 0)), row, row, row, row],
        out_specs=(row, row, row, row), out_shape=(shp, shp, shp, shp),
    )(land, own, w, m, v)


def _adamw_small(sland, w, m, v):
    def body(l_ref, w_ref, m_ref, v_ref, g_ref, d_ref, nm_ref, nv_ref, loss_ref):
        s = l_ref[0]
        for j in range(1, N_DEV):
            s = s + l_ref[j]
        w = w_ref[...]
        gq = s[8:9]
        for h in range(1, A_H):
            gq = gq + s[8 + h:9 + h]
        gk = s[16:17] + s[17:18]
        gdec = s[5:6] * _sigmoid(-w[5:6])
        g = jnp.concatenate([s[0:5], gdec, gq, gk], axis=0)
        g_ref[...] = g
        d_ref[...], nm_ref[...], nv_ref[...] = _adamw_math(w, g, m_ref[...], v_ref[...])
        loss_ref[...] = s[6:7, 0:LANES]

    shp = jax.ShapeDtypeStruct((8, PACK_COLS), F32)
    return pl.pallas_call(
        body, name="adamw_small",
        out_shape=(shp, shp, shp, shp, jax.ShapeDtypeStruct((1, LANES), F32)),
    )(sland, w, m, v)


_BIG = (("w_attn_o", AQ_W, D, 1), ("w_ret_o", R_W, D, 1), ("w_out", D, D, 0),
        ("w_up", D, FF, 1), ("w_down", FF, D, 0), ("w_ple_gate", D, D, 0), ("w_ple", PLE, D, 1))
_LATE = _BIG[:3]
_EARLY = _BIG[3:]
IN_SHARD = IN_W // N_DEV
_SMALL = ("mix_norm", "mlp_norm", "ple_norm", "final_norm", "ret_norm_gain", "ret_decay_logit",
          "attn_q_norm", "attn_k_norm")


def _shard_shape(rows, cols, axis):
    return (rows // N_DEV, cols) if axis == 0 else (rows, cols // N_DEV)


def _pack_shards(shards):
    flat = jnp.concatenate([s.reshape(-1) for s in shards])
    return flat.reshape(-1, PACK_COLS)


def _unpack_gathered(gathered):
    flat = gathered.reshape(N_DEV, -1)
    out, off = {}, 0
    for name, rows, cols, axis in _BIG:
        sr, sc = _shard_shape(rows, cols, axis)
        blk = flat[:, off:off + sr * sc].reshape(N_DEV, sr, sc)
        off += sr * sc
        out[name] = blk.reshape(rows, cols) if axis == 0 else blk.transpose(1, 0, 2).reshape(rows, cols)
    return out


def _pack_full_grads(grads, group):
    parts = []
    for name, rows, cols, axis in group:
        sr, sc = _shard_shape(rows, cols, axis)
        g = grads[name]
        blk = g.reshape(N_DEV, sr, sc) if axis == 0 else g.reshape(rows, N_DEV, sc).transpose(1, 0, 2)
        parts.append(blk.reshape(N_DEV, -1))
    flat = jnp.concatenate(parts, axis=1)
    return flat.reshape(N_DEV, -1, PACK_COLS)


def _unpack_shard(packed, group):
    flat = packed.reshape(-1)
    out, off = {}, 0
    for name, rows, cols, axis in group:
        sr, sc = _shard_shape(rows, cols, axis)
        out[name] = flat[off:off + sr * sc].reshape(1, sr, sc)
        off += sr * sc
    return out


def _pack_small(vals):
    rows = [jnp.pad(vals[n].reshape(-1), (0, PACK_COLS - vals[n].size)) for n in _SMALL]
    return jnp.stack(rows)


def _unpack_small(packed, like):
    return {n: packed[i, :like[n].size].reshape(like[n].shape) for i, n in enumerate(_SMALL)}


def _row(v):
    return jnp.pad(v.reshape(-1), (0, PACK_COLS - v.size))


def kernel(x, p, mix_norm, w_in, attn_q_norm, attn_k_norm, ret_decay_logit, ret_norm_gain, w_attn_o, w_ret_o, w_out, mlp_norm, w_up, w_down, ple_norm, w_ple_gate, w_ple, final_norm, loss_target, m_mix_norm, m_w_in, m_attn_q_norm, m_attn_k_norm, m_ret_decay_logit, m_ret_norm_gain, m_w_attn_o, m_w_ret_o, m_w_out, m_mlp_norm, m_w_up, m_w_down, m_ple_norm, m_w_ple_gate, m_w_ple, m_final_norm, v_mix_norm, v_w_in, v_attn_q_norm, v_attn_k_norm, v_ret_decay_logit, v_ret_norm_gain, v_w_attn_o, v_w_ret_o, v_w_out, v_mlp_norm, v_w_up, v_w_down, v_ple_norm, v_w_ple_gate, v_w_ple, v_final_norm):
    args = dict(locals())
    seq = x.shape[1]
    xs = x[0]
    ps = p[0, 0]
    tgt = loss_target[0]

    big_names = [b[0] for b in _BIG]
    wshard = _pack_shards([args[n] for n in big_names])
    win = _all_gather(w_in[0].astype(_MXU)).transpose(1, 0, 2).reshape(D, IN_W)

    g_mix, g_mlp, g_ple = mix_norm, mlp_norm, ple_norm
    g_fin = final_norm.reshape(1, D)
    gq = jnp.tile(attn_q_norm, (1, A_H))
    gk = jnp.tile(attn_k_norm, (1, A_KV))
    seg = _seg_mean_matrix()
    ca, sa, cr, sr = _rope_tables(seq)

    pa, pr, pg, h, rest_g = _in_proj(xs, g_mix, win, wshard.astype(_MXU))
    wfull = _unpack_gathered(rest_g)
    wao, wro, wout = wfull["w_attn_o"], wfull["w_ret_o"], wfull["w_out"]
    wup, wdown, wpg, wple = wfull["w_up"], wfull["w_down"], wfull["w_ple_gate"], wfull["w_ple"]
    qh, kh, vh, rqh, rkh = _qk_prep(pa, pr, gq, gk, seg, ca, sa, cr, sr)

    tq = _tile(seq, 128)
    tk = _tile(seq // 4, 1024)
    qt8 = qh.reshape(seq, A_H, A_HD).transpose(1, 2, 0)
    vta = jnp.stack([jnp.concatenate([_chunk_t(vh[:, g * A_HD:(g + 1) * A_HD], tk),
                                      jnp.ones((seq // tk, 16, tk), _MXU)], axis=1) for g in range(A_KV)])
    ot, lse = _attn_fwd(qt8, kh, vta, tq, tk)
    o = _heads_to_rows(ot)

    zb = jnp.broadcast_to(ret_decay_logit.reshape(2 * R_H, 1, 1), (2 * R_H, 1, LANES))
    tm_, tmw, tqd, tqdw, tkd, tkdw, tg, tgw = _ret_tables(zb)
    cb = _tile(seq // CHUNK, 8)
    y2, pst = _ret_fwd(rqh, rkh, pr, tm_, tqd, tkd, tg, cb)

    x1, merged, ret_in = _merge_fwd(xs, o, y2, pr, pg, ret_norm_gain, wao, wro, wout)
    x2 = _mlp_fwd(x1, g_mlp, wup, wdown)

    dx2, de, dz, hp, loss_p, dg_fin, dg_ple = _ple_loss(x2, ps, tgt, g_ple, g_fin, wpg, wpg.T, wple)
    dx1, act, du, hm, dg_mlp = _mlp_bwd(x1, dx2, g_mlp, wup, wdown.T, wup.T)
    me = 4 * lax.axis_index("x") + 2 * lax.axis_index("y") + lax.axis_index("c")
    gpack_e = _pack_full_grads({"w_up": _wgrad(hm, du, "wgrad_up"), "w_down": _wgrad(act, dx2, "wgrad_down"),
                                "w_ple_gate": _wgrad(hp, dz, "wgrad_ple_gate"), "w_ple": _wgrad(ps, de, "wgrad_ple")},
                               _EARLY)
    dpg, dao, dro, do, dry, drg, dg_gn, land_e = _merge_bwd(dx1, o, y2, pr, pg, ret_norm_gain, wao, wro,
                                                            wout.T, wao.T, wro.T, _bf(gpack_e))
    rdq, rdk, rdv, dlam = _ret_bwd(rqh, rkh, pr, dry, pst, tm_, tmw, tqd, tqdw, tkd, tkdw, tg, tgw, cb)

    ksplit = 1
    tkb = _tile(seq // 4, 512)
    dot_ = do.reshape(seq, A_H, A_HD).transpose(1, 2, 0)
    dkt, dvt, dqt = _attn_bwd(qt8, ot, dot_, lse, kh, vh, _chunk_t(kh, tkb), tq, tkb, ksplit)
    dqh = _heads_to_rows(jnp.sum(dqt, axis=0))
    dpa, dpr, dg_q, dg_k = _qk_prep_bwd(pa, dqh, _chunks_to_rows(dkt), _chunks_to_rows(dvt), rdq, rdk, rdv, drg, gq, gk, seg, ca, sa, cr, sr)
    wg_in = jnp.concatenate([_wgrad(h, dpa, "wgrad_in_a"), _wgrad(h, dpr, "wgrad_in_r"),
                             _wgrad(h, dpg, "wgrad_in_g")], axis=1)
    gpack_l = _pack_full_grads({"w_attn_o": _wgrad(o, dao, "wgrad_attn_o"),
                                "w_ret_o": _wgrad(ret_in, dro, "wgrad_ret_o"),
                                "w_out": _wgrad(merged, dx1, "wgrad_out")}, _LATE)
    gpack_in = wg_in.reshape(D, N_DEV, IN_SHARD).transpose(1, 0, 2)
    grad_x, dg_mix, land_in, land_l = _in_proj_bwd(xs, dx1, g_mix, dpa, dpr, dpg, win.T,
                                                   [_bf(gpack_in), _bf(gpack_l)])
    small = jnp.stack(
        [_row(dg_mix), _row(dg_mlp), _row(dg_ple), _row(dg_fin), _row(dg_gn), _row(dlam[:, 0, 0]),
         _row(loss_p[0, 0:1]), jnp.zeros((PACK_COLS,), F32)]
        + [_row(dg_q[0, hh * A_HD:(hh + 1) * A_HD]) for hh in range(A_H)]
        + [_row(dg_k[0, hh * A_HD:(hh + 1) * A_HD]) for hh in range(A_KV)]
        + [jnp.zeros((PACK_COLS,), F32)] * (SMALL_ROWS - 18))

    own = lambda pack: lax.dynamic_index_in_dim(pack, me, axis=0, keepdims=False)
    (sland,) = _exchange_grads([small[None]])
    in_sh = _adamw_big(land_in, own(gpack_in), w_in[0], m_w_in[0], v_w_in[0], "adamw_w_in")
    group_sh = []
    for group, land, pack, name in ((_EARLY, land_e, gpack_e, "adamw_early"), (_LATE, land_l, gpack_l, "adamw_late")):
        packed = lambda pre: _pack_shards([args[pre + g[0]] for g in group])
        group_sh.append((group, _adamw_big(land, own(pack), packed(""), packed("m_"), packed("v_"), name)))
    g_sm, d_sm, m_sm, v_sm, loss_row = _adamw_small(
        sland, _pack_small({n: args[n] for n in _SMALL}), _pack_small({n: args["m_" + n] for n in _SMALL}),
        _pack_small({n: args["v_" + n] for n in _SMALL}))

    names = ["mix_norm", "w_in", "attn_q_norm", "attn_k_norm", "ret_decay_logit", "ret_norm_gain", "w_attn_o",
             "w_ret_o", "w_out", "mlp_norm", "w_up", "w_down", "ple_norm", "w_ple_gate", "w_ple", "final_norm"]
    like = {n: args[n] for n in _SMALL}
    outs = [loss_row[0, 0], grad_x[None]]
    for kind, sm in enumerate((g_sm, d_sm, m_sm, v_sm)):
        table = {**_unpack_small(sm, like), "w_in": in_sh[kind][None]}
        for group, res in group_sh:
            table.update(_unpack_shard(res[kind], group))
        outs += [table[n] for n in names]
    return tuple(outs)
```

```python
import functools

import jax
import jax.numpy as jnp
from jax import lax
from jax.experimental import pallas as pl
from jax.experimental.pallas import tpu as pltpu

F32 = jnp.float32
_MXU = jnp.bfloat16

D = 1024
PLE = 256
GRID_W = 64
A_HD = 64
A_H = 8
A_KV = 2
A_G = A_H // A_KV
AQ_W = A_H * A_HD
AKV_W = A_KV * A_HD
R_HD = 128
R_H = 4
R_W = R_H * R_HD
IN_W = AQ_W + 2 * AKV_W + 4 * R_W + 2 * D
PA_W = AQ_W + 2 * AKV_W
PR_W = 4 * R_W
PG_W = 2 * D
FF = 4 * D
CHUNK = 128
ROPE_THETA = 10000.0
EPS = 1e-6
GN_EPS = 1e-5
N_DEV = 8

LR, B1, B2, ADAM_EPS, WD, STEP = 0.001, 0.9, 0.999, 1e-08, 0.01, 10

LANES = 128
PACK_COLS = 1024
SMALL_ROWS = 24
HIGHEST = lax.Precision.HIGHEST


def _tile(n, pref):
    t = min(n, pref)
    assert n % t == 0, (n, t)
    return t


def _bf(a):
    return a.astype(_MXU)


def _mm(a, b):
    return jnp.dot(_bf(a), _bf(b), preferred_element_type=F32)


def _mm_nt(a, b):
    return lax.dot_general(_bf(a), _bf(b), (((1,), (1,)), ((), ())), preferred_element_type=F32)


def _mm_tn(a, b):
    return lax.dot_general(_bf(a), _bf(b), (((0,), (0,)), ((), ())), preferred_element_type=F32)


def _sigmoid(z):
    return 1.0 / (1.0 + jnp.exp(-z))


def _rms(x):
    r = lax.rsqrt(jnp.mean(x * x, axis=-1, keepdims=True) + EPS)
    return x * r, r


def _rms_bwd(n, r, gain, dy):
    dn = dy * gain
    return r * (dn - n * jnp.mean(dn * n, axis=-1, keepdims=True))


def _swap_halves(x, half):
    n = x.shape[-1]
    lane = lax.broadcasted_iota(jnp.int32, x.shape, x.ndim - 1)
    first = (lane % (2 * half)) < half
    return jnp.where(first, pltpu.roll(x, n - half, axis=1), pltpu.roll(x, half, axis=1))


def _rope(x, cos, sin, half):
    return x * cos + _swap_halves(x, half) * sin


def _cat(t, reps):
    return jnp.concatenate([t] * reps, axis=1)


def _full(shape):
    nd = len(shape)
    return pl.BlockSpec(shape, lambda *_: (0,) * nd)


def _rope_tables(seq):
    def tab(head_dim):
        n_axis = head_dim // 4
        freqs = ROPE_THETA ** (-jnp.arange(n_axis, dtype=F32) / n_axis)
        rows = seq // GRID_W
        row = jnp.repeat(jnp.arange(rows, dtype=F32), GRID_W)
        col = jnp.tile(jnp.arange(GRID_W, dtype=F32), rows)
        ang = jnp.concatenate([row[:, None] * freqs, col[:, None] * freqs], axis=-1)
        c, s = jnp.cos(ang), jnp.sin(ang)
        return jnp.concatenate([c, c], axis=-1), jnp.concatenate([-s, s], axis=-1)
    ca, sa = tab(A_HD)
    cr, sr = tab(R_HD)
    return jnp.tile(ca, (1, 2)), jnp.tile(sa, (1, 2)), cr, sr


def _seg_mean_matrix():
    i = jnp.arange(AQ_W) // A_HD
    return (i[:, None] == i[None, :]).astype(F32) / A_HD


def _mesh_pos():
    return lax.axis_index("x"), lax.axis_index("y"), lax.axis_index("c")


def _gather_steps(x_ref, out_ref, send_sems, recv_sems, local_sem, base=0):
    x, y, c = _mesh_pos()
    me, sibling = (x, y, c), (x, y, 1 - c)
    chips = [(1 - x, y), (x, 1 - y), (1 - x, 1 - y)]

    def slot(px, py, pc):
        return out_ref.at[4 * px + 2 * py + pc]

    def copy(k, block, to, src=None):
        return pltpu.make_async_remote_copy(
            src_ref=slot(*block) if src is None else src, dst_ref=slot(*block),
            send_sem=send_sems.at[base + k], recv_sem=recv_sems.at[base + k],
            device_id=to, device_id_type=pl.DeviceIdType.MESH)

    mine = pltpu.make_async_copy(x_ref, slot(*me), local_sem)
    first = [copy(0, me, sibling, src=x_ref)]
    first += [copy(1 + j, me, (*chip, c), src=x_ref) for j, chip in enumerate(chips)]
    passed = [copy(4 + j, (*chip, c), sibling) for j, chip in enumerate(chips)]

    def start():
        mine.start()
        for cp in first:
            cp.start()

    def finish():
        for j, chip in enumerate(chips):
            copy(1 + j, (*chip, c), me).wait_recv()
            passed[j].start()
        copy(0, sibling, me).wait_recv()
        for j, chip in enumerate(chips):
            copy(4 + j, (*chip, 1 - c), me).wait_recv()
        for cp in first + passed:
            cp.wait_send()
        mine.wait()

    return start, finish


def _exchange_steps(g_ref, land_ref, send_sems, recv_sems, local_sem, per_device, base=0):
    x, y, c = _mesh_pos()
    me = 4 * x + 2 * y + c

    def row(j):
        return g_ref.at[j if per_device else 0]

    def peer(k):
        p = (x ^ ((k >> 2) & 1), y ^ ((k >> 1) & 1), c ^ (k & 1))
        return p, 4 * p[0] + 2 * p[1] + p[2]

    def copy(k, src, dst):
        return pltpu.make_async_remote_copy(
            src_ref=src, dst_ref=dst, send_sem=send_sems.at[base + k - 1], recv_sem=recv_sems.at[base + k - 1],
            device_id=peer(k)[0], device_id_type=pl.DeviceIdType.MESH)

    own = pltpu.make_async_copy(row(me), land_ref.at[me], local_sem)
    sends = [copy(k, row(peer(k)[1]), land_ref.at[me]) for k in range(1, N_DEV)]

    def start():
        own.start()
        for cp in sends:
            cp.start()

    def finish():
        for k in range(1, N_DEV):
            copy(k, row(me), land_ref.at[peer(k)[1]]).wait_recv()
        for cp in sends:
            cp.wait_send()
        own.wait()

    return start, finish


_COMM_SCRATCH = [pltpu.SemaphoreType.DMA((7,)), pltpu.SemaphoreType.DMA((7,)), pltpu.SemaphoreType.DMA]
_ANY = pl.BlockSpec(memory_space=pl.ANY)


def _all_gather(shard):
    def body(x_ref, out_ref, send_sems, recv_sems, local_sem):
        start, finish = _gather_steps(x_ref, out_ref, send_sems, recv_sems, local_sem)
        start()
        finish()

    return pl.pallas_call(
        body, name="all_gather_weights", out_shape=jax.ShapeDtypeStruct((N_DEV,) + shard.shape, shard.dtype),
        in_specs=[_ANY], out_specs=_ANY, scratch_shapes=list(_COMM_SCRATCH),
    )(shard)


def _exchange_grads(packs):
    n = len(packs)

    def body(*refs):
        g_refs, land_refs = refs[:n], refs[n:2 * n]
        send_sems, recv_sems, local_sems = refs[2 * n:]
        steps = [_exchange_steps(g_refs[t], land_refs[t], send_sems, recv_sems, local_sems.at[t],
                                 packs[t].shape[0] == N_DEV, base=7 * t) for t in range(n)]
        for start, _ in steps:
            start()
        for _, finish in steps:
            finish()

    return pl.pallas_call(
        body, name="exchange_grads",
        out_shape=tuple(jax.ShapeDtypeStruct((N_DEV,) + g.shape[1:], g.dtype) for g in packs),
        in_specs=[_ANY] * n, out_specs=(_ANY,) * n,
        scratch_shapes=[pltpu.SemaphoreType.DMA((7 * n,)), pltpu.SemaphoreType.DMA((7 * n,)),
                        pltpu.SemaphoreType.DMA((n,))],
    )(*packs)


def _in_proj(x, gain, w, rest):
    seq = x.shape[0]
    tm = _tile(seq, 256)
    nsteps = seq // tm

    def body(x_ref, g_ref, w_ref, rest_ref, pa_ref, pr_ref, pg_ref, h_ref, gath_ref, send_sems, recv_sems, local_sem):
        start, finish = _gather_steps(rest_ref, gath_ref, send_sems, recv_sems, local_sem)
        pl.when(pl.program_id(0) == 0)(start)
        n, _ = _rms(x_ref[...])
        h = _bf(n * g_ref[...])
        h_ref[...] = h
        pa_ref[...] = _mm(h, w_ref[:, 0:PA_W])
        pr_ref[...] = _mm(h, w_ref[:, PA_W:PA_W + PR_W])
        pg_ref[...] = _mm(h, w_ref[:, PA_W + PR_W:IN_W])
        pl.when(pl.program_id(0) == nsteps - 1)(finish)

    row = lambda w_: pl.BlockSpec((tm, w_), lambda i: (i, 0))
    return pl.pallas_call(
        body, name="in_proj", grid=(nsteps,),
        in_specs=[row(D), _full((1, D)), _full((D, IN_W)), _ANY],
        out_specs=(row(PA_W), row(PR_W), row(PG_W), row(D), _ANY),
        out_shape=(jax.ShapeDtypeStruct((seq, PA_W), F32), jax.ShapeDtypeStruct((seq, PR_W), F32),
                   jax.ShapeDtypeStruct((seq, PG_W), F32), jax.ShapeDtypeStruct((seq, D), _MXU),
                   jax.ShapeDtypeStruct((N_DEV,) + rest.shape, rest.dtype)),
        scratch_shapes=list(_COMM_SCRATCH),
    )(x, gain, w, rest)


def _qk_prep(pa, pr, gq, gk, seg, ca, sa, cr, sr):
    seq = pa.shape[0]
    tm = _tile(seq, 256)

    def body(pa_ref, pr_ref, gq_ref, gk_ref, seg_ref, ca_ref, sa_ref, cr_ref, sr_ref,
             qh_ref, kh_ref, v_ref, rq_ref, rk_ref):
        q = pa_ref[:, 0:AQ_W]
        k = pa_ref[:, AQ_W:AQ_W + AKV_W]
        v_ref[...] = _bf(pa_ref[:, AQ_W + AKV_W:PA_W])
        ca_, sa_ = ca_ref[...], sa_ref[...]
        msq = jnp.dot(q * q, seg_ref[...], precision=HIGHEST, preferred_element_type=F32)
        qn = q * lax.rsqrt(msq + EPS) * gq_ref[...]
        qh_ref[...] = _bf(_rope(qn, _cat(ca_, 4), _cat(sa_, 4), A_HD // 2) * (A_HD ** -0.5))
        msk = jnp.dot(k * k, seg_ref[0:AKV_W, 0:AKV_W], precision=HIGHEST, preferred_element_type=F32)
        kn = k * lax.rsqrt(msk + EPS) * gk_ref[...]
        kh_ref[...] = _bf(_rope(kn, ca_, sa_, A_HD // 2))
        cr_, sr_ = _cat(cr_ref[...], 4), _cat(sr_ref[...], 4)
        rq_ref[...] = _rope(pr_ref[:, 0:R_W], cr_, sr_, R_HD // 2) * (R_HD ** -0.5)
        rk_ref[...] = _rope(pr_ref[:, R_W:2 * R_W], cr_, sr_, R_HD // 2)

    row = lambda w_: pl.BlockSpec((tm, w_), lambda i: (i, 0))
    return pl.pallas_call(
        body, name="qk_prep", grid=(seq // tm,),
        in_specs=[row(PA_W), row(2 * R_W), _full((1, AQ_W)), _full((1, AKV_W)), _full((AQ_W, AQ_W)),
                  row(LANES), row(LANES), row(LANES), row(LANES)],
        out_specs=(row(AQ_W), row(AKV_W), row(AKV_W), row(R_W), row(R_W)),
        out_shape=(jax.ShapeDtypeStruct((seq, AQ_W), _MXU), jax.ShapeDtypeStruct((seq, AKV_W), _MXU),
                   jax.ShapeDtypeStruct((seq, AKV_W), _MXU), jax.ShapeDtypeStruct((seq, R_W), F32),
                   jax.ShapeDtypeStruct((seq, R_W), F32)),
    )(pa, pr, gq, gk, seg, ca, sa, cr, sr)


def _chunk_t(a, tk):
    seq = a.shape[0]
    return a.reshape(seq // tk, tk, a.shape[1]).transpose(0, 2, 1)


def _heads_to_rows(t):
    return t.transpose(2, 0, 1).reshape(t.shape[2], AQ_W)


def _attn_fwd(qt8, k2, vta, tq, tk):
    seq = k2.shape[0]
    nck = seq // tk
    rows = A_G * tq
    vrows = vta.shape[2]
    rb = _tile(tk, 64)
    assert nck % 2 == 0, nck

    def body(qt_ref, k_ref, vt_ref, o_ref, lse_ref, m_sc, acc_sc, qtp_sc, s_a, s_b, p_a, p_b, al_a, al_b):
        g = pl.program_id(0)
        qtp_sc[...] = jnp.zeros_like(qtp_sc)
        qtp_sc[pl.ds(pl.multiple_of(g * A_HD, A_HD), A_HD), :] = jnp.concatenate(
            [qt_ref[a] for a in range(A_G)], axis=1)
        m_sc[...] = jnp.full((1, rows), -jnp.inf, F32)
        acc_sc[...] = jnp.zeros_like(acc_sc)

        def scores(c):
            kc = k_ref[pl.ds(pl.multiple_of(c * tk, tk), tk), :]
            return _mm(kc, qtp_sc[...])

        def stage(c, s_cur, s_nxt, p_cur, p_prv, al_cur, al_prv, first=False, last=False):
            if not last:
                s_nxt[...] = scores(c + 1)
            if not first:
                acc_sc[...] = al_prv[...] * acc_sc[...] + _mm(vt_ref[0, c - 1], p_prv[...])
            m_old = m_sc[...]
            mx = None
            for r in range(0, tk, rb):
                bm = jnp.max(s_cur[r:r + rb, :].reshape(rb // 8, 8, rows), axis=0)
                mx = bm if mx is None else jnp.maximum(mx, bm)
            m_new = jnp.maximum(m_old, jnp.max(mx, axis=0, keepdims=True))
            for r in range(0, tk, rb):
                p_cur[r:r + rb, :] = _bf(jnp.exp(s_cur[r:r + rb, :] - m_new))
            al_cur[...] = jnp.exp(m_old - m_new)
            m_sc[...] = m_new

        s_a[...] = scores(0)
        stage(0, s_a, s_b, p_a, p_b, al_a, al_b, first=True)

        def pair(j, carry):
            stage(2 * j + 1, s_b, s_a, p_b, p_a, al_b, al_a)
            stage(2 * j + 2, s_a, s_b, p_a, p_b, al_a, al_b)
            return carry

        lax.fori_loop(0, nck // 2 - 1, pair, 0)
        stage(nck - 1, s_b, s_a, p_b, p_a, al_b, al_a, last=True)
        acc = al_b[...] * acc_sc[...] + _mm(vt_ref[0, nck - 1], p_b[...])
        l = acc[A_HD:A_HD + 1, :]
        lse = m_sc[...] + jnp.log(l)
        out = acc[0:A_HD, :] * (1.0 / l)
        for a in range(A_G):
            o_ref[a] = out[:, a * tq:(a + 1) * tq]
            lse_ref[a] = lse[:, a * tq:(a + 1) * tq]

    return pl.pallas_call(
        body, name="attn_fwd", grid=(A_KV, seq // tq),
        in_specs=[pl.BlockSpec((A_G, A_HD, tq), lambda g, i: (g, 0, i)),
                  _full((seq, LANES)), pl.BlockSpec((1, nck, vrows, tk), lambda g, i: (g, 0, 0, 0))],
        out_specs=(pl.BlockSpec((A_G, A_HD, tq), lambda g, i: (g, 0, i)),
                   pl.BlockSpec((A_G, 1, tq), lambda g, i: (g, 0, i))),
        out_shape=(jax.ShapeDtypeStruct((A_H, A_HD, seq), F32), jax.ShapeDtypeStruct((A_H, 1, seq), F32)),
        scratch_shapes=[pltpu.VMEM((1, rows), F32), pltpu.VMEM((vrows, rows), F32), pltpu.VMEM((LANES, rows), _MXU),
                        pltpu.VMEM((tk, rows), F32), pltpu.VMEM((tk, rows), F32),
                        pltpu.VMEM((tk, rows), _MXU), pltpu.VMEM((tk, rows), _MXU),
                        pltpu.VMEM((1, rows), F32), pltpu.VMEM((1, rows), F32)],
    )(qt8, k2, vta)


def _attn_bwd(qt8, ot, dot_, lse, k2, v2, k2t, tq, tk, ksplit):
    seq = k2.shape[0]
    sh = seq // ksplit
    nck = sh // tk
    rows = A_G * tq
    rb = _tile(tk, 32768 // rows)
    assert nck % 2 == 0, nck

    def body(qt_ref, ot_ref, dot_ref, lse_ref, k_ref, v_ref, kt_ref,
             dk_ref, dv_ref, dq_ref, dq_sc, qtp_sc, dotp_sc, pt_sc, dst_sc,
             s_a, s_b, dp_a, dp_b, p_a, p_b, ds_a, ds_b):
        g = pl.program_id(1)
        hrows = pl.ds(pl.multiple_of(g * A_HD, A_HD), A_HD)

        @pl.when(pl.program_id(2) == 0)
        def _():
            dk_ref[...] = jnp.zeros_like(dk_ref)
            dv_ref[...] = jnp.zeros_like(dv_ref)

        lse_row = jnp.concatenate([lse_ref[a] for a in range(A_G)], axis=1)
        dd = jnp.concatenate([jnp.sum(ot_ref[a] * dot_ref[a], axis=0, keepdims=True)
                              for a in range(A_G)], axis=1)
        qtp_sc[...] = jnp.zeros_like(qtp_sc)
        dotp_sc[...] = jnp.zeros_like(dotp_sc)
        qtp_sc[hrows, :] = jnp.concatenate([qt_ref[a] for a in range(A_G)], axis=1)
        dotp_sc[hrows, :] = _bf(jnp.concatenate([dot_ref[a] for a in range(A_G)], axis=1))
        dq_sc[...] = jnp.zeros_like(dq_sc)

        def products(c, s_ref, dp_ref):
            sl = pl.ds(pl.multiple_of(c * tk, tk), tk)
            s_ref[...] = _mm(k_ref[sl, :], qtp_sc[...])
            dp_ref[...] = _mm(v_ref[sl, :], dotp_sc[...])

        def accumulate(c, p_ref, ds_ref):
            pt_sc[...] = p_ref[...].T
            dst_sc[...] = ds_ref[...].T
            dq_sc[...] += _mm(kt_ref[c, hrows, :], ds_ref[...])
            dv_ref[0, c] += _mm(dotp_sc[hrows, :], pt_sc[...])
            dk_ref[0, c] += _mm(qtp_sc[hrows, :], dst_sc[...])

        def stage(c, s_cur, dp_cur, s_nxt, dp_nxt, p_cur, ds_cur, p_prv, ds_prv, first=False, last=False):
            if not last:
                products(c + 1, s_nxt, dp_nxt)
            if not first:
                accumulate(c - 1, p_prv, ds_prv)
            for r in range(0, tk, rb):
                p = jnp.exp(s_cur[r:r + rb, :] - lse_row)
                p_cur[r:r + rb, :] = _bf(p)
                ds_cur[r:r + rb, :] = _bf(p * (dp_cur[r:r + rb, :] - dd))

        products(0, s_a, dp_a)
        stage(0, s_a, dp_a, s_b, dp_b, p_a, ds_a, p_b, ds_b, first=True)

        def pair(j, carry):
            stage(2 * j + 1, s_b, dp_b, s_a, dp_a, p_b, ds_b, p_a, ds_a)
            stage(2 * j + 2, s_a, dp_a, s_b, dp_b, p_a, ds_a, p_b, ds_b)
            return carry

        lax.fori_loop(0, nck // 2 - 1, pair, 0)
        stage(nck - 1, s_b, dp_b, s_a, dp_a, p_b, ds_b, p_a, ds_a, last=True)
        accumulate(nck - 1, p_b, ds_b)
        for a in range(A_G):
            dq_ref[0, a] = dq_sc[:, a * tq:(a + 1) * tq]

    tspec = pl.BlockSpec((A_G, A_HD, tq), lambda s, g, i: (g, 0, i))
    kspec = pl.BlockSpec((sh, LANES), lambda s, g, i: (s, 0))
    gspec = pl.BlockSpec((1, nck, A_HD, tk), lambda s, g, i: (g, s, 0, 0))
    gshape = jax.ShapeDtypeStruct((A_KV, seq // tk, A_HD, tk), F32)
    big = lambda dt: pltpu.VMEM((tk, rows), dt)
    bigt = pltpu.VMEM((rows, tk), _MXU)
    return pl.pallas_call(
        body, name="attn_bwd", grid=(ksplit, A_KV, seq // tq),
        in_specs=[tspec, tspec, tspec, pl.BlockSpec((A_G, 1, tq), lambda s, g, i: (g, 0, i)),
                  kspec, kspec, pl.BlockSpec((nck, LANES, tk), lambda s, g, i: (s, 0, 0))],
        out_specs=(gspec, gspec, pl.BlockSpec((1, A_G, A_HD, tq), lambda s, g, i: (s, g, 0, i))),
        out_shape=(gshape, gshape, jax.ShapeDtypeStruct((ksplit, A_H, A_HD, seq), F32)),
        scratch_shapes=[pltpu.VMEM((A_HD, rows), F32), pltpu.VMEM((LANES, rows), _MXU), pltpu.VMEM((LANES, rows), _MXU),
                        bigt, bigt,
                        big(F32), big(F32), big(F32), big(F32), big(_MXU), big(_MXU), big(_MXU), big(_MXU)],
    )(qt8, ot, dot_, lse, k2, v2, k2t)


def _chunks_to_rows(t):
    return t.transpose(1, 3, 0, 2).reshape(t.shape[1] * t.shape[3], AKV_W)


def _ret_tables(zb):
    c = CHUNK

    def body(z_ref, m_ref, mw_ref, qd_ref, qdw_ref, kd_ref, kdw_ref, g_ref, gw_ref):
        fwd = pl.program_id(0) < R_H
        z = z_ref[0]
        lam = jnp.minimum(z, 0.0) - jnp.log(1.0 + jnp.exp(-jnp.abs(z)))
        i = lax.broadcasted_iota(jnp.int32, (c, c), 0).astype(F32)
        j = lax.broadcasted_iota(jnp.int32, (c, c), 1).astype(F32)
        diff = jnp.where(fwd, i - j, j - i)
        keep = diff >= jnp.where(fwd, 0.0, 1.0)
        dist = jnp.maximum(diff, 0.0)
        m = jnp.where(keep, jnp.exp(lam * dist), 0.0)
        m_ref[0] = m
        mw_ref[0] = m * dist
        fq = jnp.where(fwd, i + 1.0, c - i)
        qd = jnp.exp(lam * fq)
        qd_ref[0] = qd
        qdw_ref[0] = qd * fq
        fk = jnp.where(fwd, c - 1.0 - i, i)
        kd = jnp.exp(lam * fk)
        kd_ref[0] = kd
        kdw_ref[0] = kd * fk
        gdec = jnp.exp(lam * c)
        g_ref[0] = gdec
        gw_ref[0] = gdec * c

    big = pl.BlockSpec((1, c, c), lambda t: (t, 0, 0))
    vec = pl.BlockSpec((1, 1, LANES), lambda t: (t, 0, 0))
    bshape = jax.ShapeDtypeStruct((2 * R_H, c, c), F32)
    vshape = jax.ShapeDtypeStruct((2 * R_H, 1, LANES), F32)
    return pl.pallas_call(
        body, name="ret_tables", grid=(2 * R_H,), in_specs=[vec],
        out_specs=(big, big, big, big, big, big, vec, vec),
        out_shape=(bshape,) * 6 + (vshape, vshape),
    )(zb)


def _ret_fwd(rq, rk, pr, m, qd, kd, gdec, cb):
    seq = rq.shape[0]
    c = CHUNK
    ns = seq // (cb * c)

    def body(qf_ref, kf_ref, vf_ref, qb_ref, kb_ref, vb_ref, m_ref, qd_ref, kd_ref, g_ref,
             yf_ref, yb_ref, pstf_ref, pstb_ref, p_sc):
        @pl.when(pl.program_id(0) == 0)
        def _():
            p_sc[...] = jnp.zeros_like(p_sc)

        dirs = ((qf_ref, kf_ref, vf_ref, yf_ref, pstf_ref), (qb_ref, kb_ref, vb_ref, yb_ref, pstb_ref))
        heads = [slice(h * R_HD, (h + 1) * R_HD) for h in range(R_H)]

        def chunk(j, carry):
            early = []
            for d, (q_ref, k_ref, v_ref, _, _) in enumerate(dirs):
                cc = j if d == 0 else cb - 1 - j
                sl = pl.ds(pl.multiple_of(cc * c, c), c)
                for h, hs in enumerate(heads):
                    t = d * R_H + h
                    early.append((_mm_nt(q_ref[sl, hs], k_ref[sl, hs]), _mm(q_ref[sl, hs] * qd_ref[t], p_sc[t]),
                                  _mm_tn(k_ref[sl, hs] * kd_ref[t], v_ref[sl, hs])))
            for d, (_, _, v_ref, y_ref, pst_ref) in enumerate(dirs):
                cc = j if d == 0 else cb - 1 - j
                sl = pl.ds(pl.multiple_of(cc * c, c), c)
                for h, hs in enumerate(heads):
                    t = d * R_H + h
                    qk, qp, kv = early[t]
                    p = p_sc[t]
                    pst_ref[h, cc] = p
                    y_ref[sl, hs] = _mm(qk * m_ref[t], v_ref[sl, hs]) + qp
                    p_sc[t] = p * g_ref[t] + kv
            return carry

        lax.fori_loop(0, cb, chunk, 0)

    asc = lambda off: pl.BlockSpec((cb * c, R_W), lambda n: (n, off))
    desc = lambda off: pl.BlockSpec((cb * c, R_W), lambda n: (ns - 1 - n, off))
    return pl.pallas_call(
        body, name="ret_fwd", grid=(ns,),
        in_specs=[asc(0), asc(0), asc(2), desc(0), desc(0), desc(2),
                  _full((2 * R_H, c, c)), _full((2 * R_H, c, c)), _full((2 * R_H, c, c)), _full((2 * R_H, 1, LANES))],
        out_specs=(asc(0), desc(0),
                   pl.BlockSpec((R_H, cb, R_HD, R_HD), lambda n: (0, n, 0, 0)),
                   pl.BlockSpec((R_H, cb, R_HD, R_HD), lambda n: (0, ns - 1 - n, 0, 0))),
        out_shape=(jax.ShapeDtypeStruct((seq, R_W), F32), jax.ShapeDtypeStruct((seq, R_W), F32),
                   jax.ShapeDtypeStruct((R_H, seq // c, R_HD, R_HD), F32),
                   jax.ShapeDtypeStruct((R_H, seq // c, R_HD, R_HD), F32)),
        scratch_shapes=[pltpu.VMEM((2 * R_H, R_HD, R_HD), F32)],
    )(rq, rk, pr, rq, rk, pr, m, qd, kd, gdec)


def _ret_bwd(rq, rk, pr, dry, pstf, pstb, m, mw, qd, qdw, kd, kdw, gdec, gw, cb):
    seq = rq.shape[0]
    c = CHUNK
    ns = seq // (cb * c)

    def body(qf_ref, kf_ref, vf_ref, dyf_ref, pstf_ref, qb_ref, kb_ref, vb_ref, dyb_ref, pstb_ref,
             m_ref, mw_ref, qd_ref, qdw_ref, kd_ref, kdw_ref, g_ref, gw_ref,
             dqf_ref, dkf_ref, dvf_ref, dqb_ref, dkb_ref, dvb_ref, dlam_ref, r_sc, acc_sc, e_sc, g_sc):
        n = pl.program_id(0)

        @pl.when(n == 0)
        def _():
            r_sc[...] = jnp.zeros_like(r_sc)
            acc_sc[...] = jnp.zeros_like(acc_sc)
            e_sc[...] = jnp.zeros_like(e_sc)
            g_sc[...] = jnp.zeros_like(g_sc)

        dirs = ((qf_ref, kf_ref, vf_ref, dyf_ref, pstf_ref, dqf_ref, dkf_ref, dvf_ref),
                (qb_ref, kb_ref, vb_ref, dyb_ref, pstb_ref, dqb_ref, dkb_ref, dvb_ref))
        heads = [slice(h * R_HD, (h + 1) * R_HD) for h in range(R_H)]

        def chunk(j, carry):
            first = []
            for d, (q_ref, k_ref, v_ref, dy_ref, pst_ref, _, _, _) in enumerate(dirs):
                cc = cb - 1 - j if d == 0 else j
                sl = pl.ds(pl.multiple_of(cc * c, c), c)
                for h, hs in enumerate(heads):
                    t = d * R_H + h
                    q, k, v, dy = q_ref[sl, hs], k_ref[sl, hs], v_ref[sl, hs], dy_ref[sl, hs]
                    r = r_sc[t]
                    first.append((_mm_nt(q, k), _mm_nt(dy, v), _mm_nt(dy, pst_ref[h, cc]), _mm_nt(v, r),
                                  _mm(k * kd_ref[t], r), _mm_tn(q * qd_ref[t], dy)))
            for d, (q_ref, k_ref, _, dy_ref, pst_ref, dq_ref, dk_ref, dv_ref) in enumerate(dirs):
                cc = cb - 1 - j if d == 0 else j
                sl = pl.ds(pl.multiple_of(cc * c, c), c)
                for h, hs in enumerate(heads):
                    t = d * R_H + h
                    qk, ds, dyp, vr, kr, qdy = first[t]
                    q, k, dy = q_ref[sl, hs], k_ref[sl, hs], dy_ref[sl, hs]
                    r = r_sc[t]
                    da = ds * m_ref[t]
                    dv_ref[sl, hs] = _mm_tn(qk * m_ref[t], dy) + kr
                    dq_ref[sl, hs] = _mm(da, k) + dyp * qd_ref[t]
                    dk_ref[sl, hs] = _mm_tn(da, q) + vr * kd_ref[t]
                    acc_sc[t] += dyp * q * qdw_ref[t] + vr * k * kdw_ref[t]
                    e_sc[t] += ds * qk * mw_ref[t]
                    g_sc[t] += r * pst_ref[h, cc]
                    r_sc[t] = r * g_ref[t] + qdy
            return carry

        lax.fori_loop(0, cb, chunk, 0)

        @pl.when(n == ns - 1)
        def _():
            for t in range(2 * R_H):
                tot = jnp.sum(jnp.sum(acc_sc[t] + e_sc[t] + g_sc[t] * gw_ref[t], axis=0, keepdims=True),
                              axis=1, keepdims=True)
                dlam_ref[t] = jnp.broadcast_to(tot, (1, LANES))

    asc = lambda off: pl.BlockSpec((cb * c, R_W), lambda n: (n, off))
    desc = lambda off: pl.BlockSpec((cb * c, R_W), lambda n: (ns - 1 - n, off))
    big = _full((2 * R_H, c, c))
    vec = _full((2 * R_H, 1, LANES))
    oshape = jax.ShapeDtypeStruct((seq, R_W), F32)
    sq = pltpu.VMEM((2 * R_H, R_HD, R_HD), F32)
    return pl.pallas_call(
        body, name="ret_bwd", grid=(ns,),
        in_specs=[desc(0), desc(0), desc(2), desc(0),
                  pl.BlockSpec((R_H, cb, R_HD, R_HD), lambda n: (0, ns - 1 - n, 0, 0)),
                  asc(0), asc(0), asc(2), asc(0),
                  pl.BlockSpec((R_H, cb, R_HD, R_HD), lambda n: (0, n, 0, 0)),
                  big, big, big, big, big, big, vec, vec],
        out_specs=(desc(0), desc(0), desc(0), asc(0), asc(0), asc(0), vec),
        out_shape=(oshape,) * 6 + (jax.ShapeDtypeStruct((2 * R_H, 1, LANES), F32),),
        scratch_shapes=[sq, sq, sq, sq],
    )(rq, rk, pr, dry, pstf, rq, rk, pr, dry, pstb, m, mw, qd, qdw, kd, kdw, gdec, gw)


def _group_norm(ry):
    yn, rs = [], []
    for h in range(R_H):
        s = ry[:, h * R_HD:(h + 1) * R_HD]
        mu = jnp.mean(s, axis=-1, keepdims=True)
        cen = s - mu
        r = lax.rsqrt(jnp.mean(cen * cen, axis=-1, keepdims=True) + GN_EPS)
        yn.append(cen * r)
        rs.append(r)
    return yn, rs


def _merge_fwd(x, o, yf, yb, pr, pg, gain_r, wao, wro, wout):
    seq = x.shape[0]
    tm = _tile(seq, 256)

    def body(x_ref, o_ref, yf_ref, yb_ref, rg_ref, ga_ref, gr_ref, gn_ref, wao_ref, wro_ref, wout_ref,
             x1_ref, mg_ref, ri_ref):
        yn, _ = _group_norm(yf_ref[...] + yb_ref[...])
        rg = rg_ref[...]
        ret_in = jnp.concatenate(yn, axis=1) * gn_ref[...] * (rg * _sigmoid(rg))
        ri_ref[...] = _bf(ret_in)
        attn_out = _mm(o_ref[...], wao_ref[...])
        ret_out = _mm(ret_in, wro_ref[...])
        merged = _sigmoid(ga_ref[...]) * attn_out + _sigmoid(gr_ref[...]) * ret_out
        mg_ref[...] = _bf(merged)
        x1_ref[...] = x_ref[...] + _mm(merged, wout_ref[...])

    row = lambda w_, j=0: pl.BlockSpec((tm, w_), lambda i: (i, j))
    return pl.pallas_call(
        body, name="merge_fwd", grid=(seq // tm,),
        in_specs=[row(D), row(AQ_W), row(R_W), row(R_W), row(R_W, 3), row(D, 0), row(D, 1),
                  _full((1, R_W)), _full((AQ_W, D)), _full((R_W, D)), _full((D, D))],
        out_specs=(row(D), row(D), row(R_W)),
        out_shape=(jax.ShapeDtypeStruct((seq, D), F32), jax.ShapeDtypeStruct((seq, D), _MXU),
                   jax.ShapeDtypeStruct((seq, R_W), _MXU)),
    )(x, o, yf, yb, pr, pg, pg, gain_r, wao, wro, wout)


def _mlp_fwd(x1, gain, wup, wdown):
    seq = x1.shape[0]
    tm = _tile(seq, 512)
    fc = 2048
    nfc = FF // fc

    def body(x_ref, g_ref, wu_ref, wd_ref, x2_ref, hm_sc, acc_sc):
        c = pl.program_id(1)

        @pl.when(c == 0)
        def _():
            n, _ = _rms(x_ref[...])
            hm_sc[...] = _bf(n * g_ref[...])
            acc_sc[...] = jnp.zeros_like(acc_sc)

        halves = (slice(0, fc // 2), slice(fc // 2, fc))
        ups = [jnp.maximum(_mm(hm_sc[...], wu_ref[:, hs]), 0.0) for hs in halves]
        acc_sc[...] += _mm(ups[0] * ups[0], wd_ref[halves[0], :]) + _mm(ups[1] * ups[1], wd_ref[halves[1], :])

        @pl.when(c == nfc - 1)
        def _():
            x2_ref[...] = x_ref[...] + acc_sc[...]

    return pl.pallas_call(
        body, name="mlp_fwd", grid=(seq // tm, nfc),
        in_specs=[pl.BlockSpec((tm, D), lambda i, c: (i, 0)), pl.BlockSpec((1, D), lambda i, c: (0, 0)),
                  pl.BlockSpec((D, fc), lambda i, c: (0, c)), pl.BlockSpec((fc, D), lambda i, c: (c, 0))],
        out_specs=pl.BlockSpec((tm, D), lambda i, c: (i, 0)),
        out_shape=jax.ShapeDtypeStruct((seq, D), F32),
        scratch_shapes=[pltpu.VMEM((tm, D), _MXU), pltpu.VMEM((tm, D), F32)],
    )(x1, gain, wup, wdown)


def _ple_loss(x2, p, tgt, g_ple, g_fin, wpg, wpgt, wple):
    seq = x2.shape[0]
    tm = _tile(seq, 256)

    def body(x2_ref, p_ref, t_ref, gp_ref, gf_ref, wpg_ref, wpgt_ref, wple_ref,
             dx2_ref, de_ref, dz_ref, hp_ref, loss_ref, dgf_ref, dgp_ref):
        @pl.when(pl.program_id(0) == 0)
        def _():
            loss_ref[...] = jnp.zeros_like(loss_ref)
            dgf_ref[...] = jnp.zeros_like(dgf_ref)
            dgp_ref[...] = jnp.zeros_like(dgp_ref)

        x2 = x2_ref[...]
        gp, gf = gp_ref[...], gf_ref[...]
        n2, r2 = _rms(x2)
        hp = _bf(n2 * gp)
        hp_ref[...] = hp
        gate = _sigmoid(_mm(hp, wpg_ref[...]))
        e = _mm(p_ref[...], wple_ref[...])
        x3 = x2 + gate * e
        n3, r3 = _rms(x3)
        diff = n3 * gf - t_ref[...]
        row_loss = jnp.mean(diff * diff, axis=-1, keepdims=True)
        loss_ref[...] += 0.5 * jnp.sum(row_loss, axis=0, keepdims=True)
        dy = diff * (1.0 / D)
        dgf_ref[...] += jnp.sum(dy * n3, axis=0, keepdims=True)
        dx3 = _rms_bwd(n3, r3, gf, dy)
        de_ref[...] = _bf(dx3 * gate)
        dz = dx3 * e * gate * (1.0 - gate)
        dz_ref[...] = _bf(dz)
        dhp = _mm(dz, wpgt_ref[...])
        dgp_ref[...] += jnp.sum(dhp * n2, axis=0, keepdims=True)
        dx2_ref[...] = dx3 + _rms_bwd(n2, r2, gp, dhp)

    row = lambda w_: pl.BlockSpec((tm, w_), lambda i: (i, 0))
    act = lambda dt: jax.ShapeDtypeStruct((seq, D), dt)
    return pl.pallas_call(
        body, name="ple_loss", grid=(seq // tm,),
        in_specs=[row(D), row(PLE), row(D), _full((1, D)), _full((1, D)),
                  _full((D, D)), _full((D, D)), _full((PLE, D))],
        out_specs=(row(D), row(D), row(D), row(D), _full((1, LANES)), _full((1, D)), _full((1, D))),
        out_shape=(act(F32), act(_MXU), act(_MXU), act(_MXU), jax.ShapeDtypeStruct((1, LANES), F32),
                   jax.ShapeDtypeStruct((1, D), F32), jax.ShapeDtypeStruct((1, D), F32)),
    )(x2, p, tgt, g_ple, g_fin, wpg, wpgt, wple)


def _mlp_bwd(x1, dx2, gain, wup, wdownt, wupt):
    seq = x1.shape[0]
    tm = _tile(seq, 512)
    fc = 2048
    nfc = FF // fc

    def body(x_ref, dx2_ref, g_ref, wu_ref, wdt_ref, wut_ref,
             dx1_ref, a_ref, du_ref, hm_ref, dg_ref, dhm_sc):
        i = pl.program_id(0)
        c = pl.program_id(1)

        @pl.when((i == 0) & (c == 0))
        def _():
            dg_ref[...] = jnp.zeros_like(dg_ref)

        @pl.when(c == 0)
        def _():
            n, _ = _rms(x_ref[...])
            hm_ref[...] = _bf(n * g_ref[...])
            dhm_sc[...] = jnp.zeros_like(dhm_sc)

        halves = (slice(0, fc // 2), slice(fc // 2, fc))
        ups = [jnp.maximum(_mm(hm_ref[...], wu_ref[:, hs]), 0.0) for hs in halves]
        das = [_mm(dx2_ref[...], wdt_ref[:, hs]) for hs in halves]
        part = None
        for u, da, hs in zip(ups, das, halves):
            a_ref[:, hs] = _bf(u * u)
            du = _bf(da * (2.0 * u))
            du_ref[:, hs] = du
            t = _mm(du, wut_ref[hs, :])
            part = t if part is None else part + t
        dhm_sc[...] += part

        @pl.when(c == nfc - 1)
        def _():
            n, r = _rms(x_ref[...])
            dhm = dhm_sc[...]
            dg_ref[...] += jnp.sum(dhm * n, axis=0, keepdims=True)
            dx1_ref[...] = dx2_ref[...] + _rms_bwd(n, r, g_ref[...], dhm)

    rowd = pl.BlockSpec((tm, D), lambda i, c: (i, 0))
    rowf = pl.BlockSpec((tm, fc), lambda i, c: (i, c))
    return pl.pallas_call(
        body, name="mlp_bwd", grid=(seq // tm, nfc),
        in_specs=[rowd, rowd, pl.BlockSpec((1, D), lambda i, c: (0, 0)),
                  pl.BlockSpec((D, fc), lambda i, c: (0, c)), pl.BlockSpec((D, fc), lambda i, c: (0, c)),
                  pl.BlockSpec((fc, D), lambda i, c: (c, 0))],
        out_specs=(rowd, rowf, rowf, rowd, pl.BlockSpec((1, D), lambda i, c: (0, 0))),
        out_shape=(jax.ShapeDtypeStruct((seq, D), F32), jax.ShapeDtypeStruct((seq, FF), _MXU),
                   jax.ShapeDtypeStruct((seq, FF), _MXU), jax.ShapeDtypeStruct((seq, D), _MXU),
                   jax.ShapeDtypeStruct((1, D), F32)),
        scratch_shapes=[pltpu.VMEM((tm, D), F32)],
    )(x1, dx2, gain, wup, wdownt, wupt)


def _merge_bwd(dx1, o, yf, yb, pr, pg, gain_r, wao, wro, woutt, waot, wrot, gpack):
    seq = dx1.shape[0]
    tm = _tile(seq, 256)
    nsteps = seq // tm

    def body(dx1_ref, o_ref, yf_ref, yb_ref, rg_ref, ga_ref, gr_ref, gn_ref, wao_ref, wro_ref,
             woutt_ref, waot_ref, wrot_ref, gpack_ref,
             dpg_ref, dao_ref, dro_ref, do_ref, dry_ref, drg_ref, dgn_ref, land_ref,
             send_sems, recv_sems, local_sem):
        start, finish = _exchange_steps(gpack_ref, land_ref, send_sems, recv_sems, local_sem, True)
        pl.when(pl.program_id(0) == 0)(start)

        @pl.when(pl.program_id(0) == 0)
        def _():
            dgn_ref[...] = jnp.zeros_like(dgn_ref)

        yn_l, rs_l = _group_norm(yf_ref[...] + yb_ref[...])
        yn = jnp.concatenate(yn_l, axis=1)
        rg = rg_ref[...]
        gn = gn_ref[...]
        sg = _sigmoid(rg)
        sil = rg * sg
        ret_in = yn * gn * sil
        attn_out = _mm(o_ref[...], wao_ref[...])
        ret_out = _mm(ret_in, wro_ref[...])
        sa = _sigmoid(ga_ref[...])
        sr = _sigmoid(gr_ref[...])
        dm = _mm(dx1_ref[...], woutt_ref[...])
        dpg_ref[:, 0:D] = _bf(dm * attn_out * sa * (1.0 - sa))
        dpg_ref[:, D:2 * D] = _bf(dm * ret_out * sr * (1.0 - sr))
        dao = _bf(dm * sa)
        dro = _bf(dm * sr)
        dao_ref[...] = dao
        dro_ref[...] = dro
        do_ref[...] = _mm(dao, waot_ref[...])
        dri = _mm(dro, wrot_ref[...])
        dgn_ref[...] += jnp.sum(dri * yn * sil, axis=0, keepdims=True)
        drg_ref[...] = _bf(dri * yn * gn * (sg * (1.0 + rg * (1.0 - sg))))
        dyn = dri * gn * sil
        dry = []
        for h in range(R_H):
            dh = dyn[:, h * R_HD:(h + 1) * R_HD]
            dry.append(rs_l[h] * (dh - jnp.mean(dh, axis=-1, keepdims=True)
                                  - yn_l[h] * jnp.mean(dh * yn_l[h], axis=-1, keepdims=True)))
        dry_ref[...] = jnp.concatenate(dry, axis=1)
        pl.when(pl.program_id(0) == nsteps - 1)(finish)

    row = lambda w_, j=0: pl.BlockSpec((tm, w_), lambda i: (i, j))
    return pl.pallas_call(
        body, name="merge_bwd", grid=(nsteps,),
        in_specs=[row(D), row(AQ_W), row(R_W), row(R_W), row(R_W, 3), row(D, 0), row(D, 1),
                  _full((1, R_W)), _full((AQ_W, D)), _full((R_W, D)), _full((D, D)),
                  _full((D, AQ_W)), _full((D, R_W)), _ANY],
        out_specs=(row(PG_W), row(D), row(D), row(AQ_W), row(R_W), row(R_W), _full((1, R_W)), _ANY),
        out_shape=(jax.ShapeDtypeStruct((seq, PG_W), _MXU), jax.ShapeDtypeStruct((seq, D), _MXU),
                   jax.ShapeDtypeStruct((seq, D), _MXU), jax.ShapeDtypeStruct((seq, AQ_W), F32),
                   jax.ShapeDtypeStruct((seq, R_W), F32), jax.ShapeDtypeStruct((seq, R_W), _MXU),
                   jax.ShapeDtypeStruct((1, R_W), F32), jax.ShapeDtypeStruct(gpack.shape, gpack.dtype)),
        scratch_shapes=list(_COMM_SCRATCH),
    )(dx1, o, yf, yb, pr, pg, pg, gain_r, wao, wro, woutt, waot, wrot, gpack)


def _qk_prep_bwd(pa, dqh, dk2, dv2, rdf, rdb, drg, gq, gk, seg, ca, sa, cr, sr):
    seq = pa.shape[0]
    tm = _tile(seq, 256)

    def body(pa_ref, dqh_ref, dk2_ref, dv2_ref, rdqf_ref, rdqb_ref, rdkf_ref, rdkb_ref, rdvf_ref, rdvb_ref,
             drg_ref, gq_ref, gk_ref, seg_ref, ca_ref, sa_ref, cr_ref, sr_ref,
             dpa_ref, dpr_ref, dgq_ref, dgk_ref):
        @pl.when(pl.program_id(0) == 0)
        def _():
            dgq_ref[...] = jnp.zeros_like(dgq_ref)
            dgk_ref[...] = jnp.zeros_like(dgk_ref)

        ca_, sa_ = ca_ref[...], sa_ref[...]

        def norm_bwd(raw, gain, dy, segm, dg_ref):
            msq = jnp.dot(raw * raw, segm, precision=HIGHEST, preferred_element_type=F32)
            r = lax.rsqrt(msq + EPS)
            n = raw * r
            dg_ref[...] += jnp.sum(dy * n, axis=0, keepdims=True)
            dn = dy * gain
            return r * (dn - n * jnp.dot(dn * n, segm, precision=HIGHEST, preferred_element_type=F32))

        dqn = _rope(dqh_ref[...] * (A_HD ** -0.5), _cat(ca_, 4), -_cat(sa_, 4), A_HD // 2)
        dpa_ref[:, 0:AQ_W] = _bf(norm_bwd(pa_ref[:, 0:AQ_W], gq_ref[...], dqn, seg_ref[...], dgq_ref))
        dkn = _rope(dk2_ref[...], ca_, -sa_, A_HD // 2)
        dpa_ref[:, AQ_W:AQ_W + AKV_W] = _bf(norm_bwd(pa_ref[:, AQ_W:AQ_W + AKV_W], gk_ref[...], dkn,
                                                     seg_ref[0:AKV_W, 0:AKV_W], dgk_ref))
        dpa_ref[:, AQ_W + AKV_W:PA_W] = _bf(dv2_ref[...])
        cr_, sr_ = _cat(cr_ref[...], 4), -_cat(sr_ref[...], 4)
        dpr_ref[:, 0:R_W] = _bf(_rope((rdqf_ref[...] + rdqb_ref[...]) * (R_HD ** -0.5), cr_, sr_, R_HD // 2))
        dpr_ref[:, R_W:2 * R_W] = _bf(_rope(rdkf_ref[...] + rdkb_ref[...], cr_, sr_, R_HD // 2))
        dpr_ref[:, 2 * R_W:3 * R_W] = _bf(rdvf_ref[...] + rdvb_ref[...])
        dpr_ref[:, 3 * R_W:4 * R_W] = drg_ref[...]

    row = lambda w_: pl.BlockSpec((tm, w_), lambda i: (i, 0))
    return pl.pallas_call(
        body, name="qk_prep_bwd", grid=(seq // tm,),
        in_specs=[row(PA_W), row(AQ_W), row(AKV_W), row(AKV_W), row(R_W), row(R_W), row(R_W), row(R_W),
                  row(R_W), row(R_W), row(R_W), _full((1, AQ_W)), _full((1, AKV_W)), _full((AQ_W, AQ_W)),
                  row(LANES), row(LANES), row(LANES), row(LANES)],
        out_specs=(row(PA_W), row(PR_W), _full((1, AQ_W)), _full((1, AKV_W))),
        out_shape=(jax.ShapeDtypeStruct((seq, PA_W), _MXU), jax.ShapeDtypeStruct((seq, PR_W), _MXU),
                   jax.ShapeDtypeStruct((1, AQ_W), F32), jax.ShapeDtypeStruct((1, AKV_W), F32)),
    )(pa, dqh, dk2, dv2, rdf[0], rdb[0], rdf[1], rdb[1], rdf[2], rdb[2], drg, gq, gk, seg, ca, sa, cr, sr)


def _in_proj_bwd(x, dx1, gain, dpa, dpr, dpg, wint, packs):
    seq = x.shape[0]
    tm = _tile(seq, 256)
    nsteps = seq // tm
    npk = len(packs)

    def body(x_ref, dx1_ref, g_ref, dpa_ref, dpr_ref, dpg_ref, wt_ref, *rest):
        pack_refs, (dx_ref, dg_ref), land_refs = rest[:npk], rest[npk:npk + 2], rest[npk + 2:2 * npk + 2]
        send_sems, recv_sems, local_sems = rest[2 * npk + 2:]
        steps = [_exchange_steps(pack_refs[t], land_refs[t], send_sems, recv_sems, local_sems.at[t], True, base=7 * t)
                 for t in range(npk)]

        @pl.when(pl.program_id(0) == 0)
        def _():
            for start, _ in steps:
                start()
            dg_ref[...] = jnp.zeros_like(dg_ref)

        dh = (_mm(dpa_ref[...], wt_ref[0:PA_W, :]) + _mm(dpr_ref[...], wt_ref[PA_W:PA_W + PR_W, :])
              + _mm(dpg_ref[...], wt_ref[PA_W + PR_W:IN_W, :]))
        n, r = _rms(x_ref[...])
        dg_ref[...] += jnp.sum(dh * n, axis=0, keepdims=True)
        dx_ref[...] = dx1_ref[...] + _rms_bwd(n, r, g_ref[...], dh)

        @pl.when(pl.program_id(0) == nsteps - 1)
        def _():
            for _, finish in steps:
                finish()

    row = lambda w_: pl.BlockSpec((tm, w_), lambda i: (i, 0))
    return pl.pallas_call(
        body, name="in_proj_bwd", grid=(nsteps,),
        in_specs=[row(D), row(D), _full((1, D)), row(PA_W), row(PR_W), row(PG_W), _full((IN_W, D))] + [_ANY] * npk,
        out_specs=(row(D), _full((1, D))) + (_ANY,) * npk,
        out_shape=(jax.ShapeDtypeStruct((seq, D), F32), jax.ShapeDtypeStruct((1, D), F32))
        + tuple(jax.ShapeDtypeStruct(g.shape, g.dtype) for g in packs),
        scratch_shapes=[pltpu.SemaphoreType.DMA((7 * npk,)), pltpu.SemaphoreType.DMA((7 * npk,)),
                        pltpu.SemaphoreType.DMA((npk,))],
    )(x, dx1, gain, dpa, dpr, dpg, wint, *packs)


def _wgrad(a, b, name):
    seq, m = a.shape
    n = b.shape[1]
    tm, tn, ts = _tile(m, 1024), _tile(n, 1024), _tile(seq, 2048)
    ns = seq // ts

    def body(a_ref, b_ref, o_ref):
        @pl.when(pl.program_id(2) == 0)
        def _():
            o_ref[...] = jnp.zeros_like(o_ref)

        o_ref[...] += _mm_tn(a_ref[...], b_ref[...])

    return pl.pallas_call(
        body, name=name, grid=(m // tm, n // tn, ns),
        in_specs=[pl.BlockSpec((ts, tm), lambda i, j, s: (s, i)), pl.BlockSpec((ts, tn), lambda i, j, s: (s, j))],
        out_specs=pl.BlockSpec((tm, tn), lambda i, j, s: (i, j)),
        out_shape=jax.ShapeDtypeStruct((m, n), F32),
    )(a, b)


def _adamw_math(w, g, m, v):
    m = B1 * m + (1.0 - B1) * g
    v = B2 * v + (1.0 - B2) * (g * g)
    m_hat = m / (1.0 - B1 ** STEP)
    v_hat = v / (1.0 - B2 ** STEP)
    delta = -LR * (m_hat / (jnp.sqrt(v_hat) + ADAM_EPS) + WD * w)
    return delta, m, v


def _adamw_big(land, own, w, m, v, name):
    rws, cols = w.shape
    tr = next(t for t in range(min(rws, 288) // 16 * 16, 0, -16) if rws % t == 0)

    def body(l_ref, o_ref, w_ref, m_ref, v_ref, g_ref, d_ref, nm_ref, nv_ref):
        x, y, c = _mesh_pos()
        me = 4 * x + 2 * y + c
        g = o_ref[...]
        for j in range(N_DEV):
            g = g + jnp.where(me == j, 0.0, l_ref[j].astype(F32))
        g_ref[...] = g
        d_ref[...], nm_ref[...], nv_ref[...] = _adamw_math(w_ref[...], g, m_ref[...], v_ref[...])

    row = pl.BlockSpec((tr, cols), lambda i: (i, 0))
    shp = jax.ShapeDtypeStruct((rws, cols), F32)
    return pl.pallas_call(
        body, name=name, grid=(rws // tr,),
        in_specs=[pl.BlockSpec((N_DEV, tr, cols), lambda i: (0, i, 0)), row, row, row, row],
        out_specs=(row, row, row, row), out_shape=(shp, shp, shp, shp),
    )(land, own, w, m, v)


def _adamw_small(sland, w, m, v):
    def body(l_ref, w_ref, m_ref, v_ref, g_ref, d_ref, nm_ref, nv_ref, loss_ref):
        s = l_ref[0]
        for j in range(1, N_DEV):
            s = s + l_ref[j]
        w = w_ref[...]
        gq = s[8:9]
        for h in range(1, A_H):
            gq = gq + s[8 + h:9 + h]
        gk = s[16:17] + s[17:18]
        gdec = s[5:6] * _sigmoid(-w[5:6])
        g = jnp.concatenate([s[0:5], gdec, gq, gk], axis=0)
        g_ref[...] = g
        d_ref[...], nm_ref[...], nv_ref[...] = _adamw_math(w, g, m_ref[...], v_ref[...])
        loss_ref[...] = s[6:7, 0:LANES]

    shp = jax.ShapeDtypeStruct((8, PACK_COLS), F32)
    return pl.pallas_call(
        body, name="adamw_small",
        out_shape=(shp, shp, shp, shp, jax.ShapeDtypeStruct((1, LANES), F32)),
    )(sland, w, m, v)


_BIG = (("w_attn_o", AQ_W, D, 1), ("w_ret_o", R_W, D, 1), ("w_out", D, D, 0),
        ("w_up", D, FF, 1), ("w_down", FF, D, 0), ("w_ple_gate", D, D, 0), ("w_ple", PLE, D, 1))
_LATE = _BIG[:3]
_EARLY = _BIG[3:]
IN_SHARD = IN_W // N_DEV
_SMALL = ("mix_norm", "mlp_norm", "ple_norm", "final_norm", "ret_norm_gain", "ret_decay_logit",
          "attn_q_norm", "attn_k_norm")


def _shard_shape(rows, cols, axis):
    return (rows // N_DEV, cols) if axis == 0 else (rows, cols // N_DEV)


def _pack_shards(shards):
    flat = jnp.concatenate([s.reshape(-1) for s in shards])
    return flat.reshape(-1, PACK_COLS)


def _unpack_gathered(gathered):
    flat = gathered.reshape(N_DEV, -1)
    out, off = {}, 0
    for name, rows, cols, axis in _BIG:
        sr, sc = _shard_shape(rows, cols, axis)
        blk = flat[:, off:off + sr * sc].reshape(N_DEV, sr, sc)
        off += sr * sc
        out[name] = blk.reshape(rows, cols) if axis == 0 else blk.transpose(1, 0, 2).reshape(rows, cols)
    return out


def _pack_full_grads(grads, group):
    parts = []
    for name, rows, cols, axis in group:
        sr, sc = _shard_shape(rows, cols, axis)
        g = grads[name]
        blk = g.reshape(N_DEV, sr, sc) if axis == 0 else g.reshape(rows, N_DEV, sc).transpose(1, 0, 2)
        parts.append(blk.reshape(N_DEV, -1))
    flat = jnp.concatenate(parts, axis=1)
    return flat.reshape(N_DEV, -1, PACK_COLS)


def _unpack_shard(packed, group):
    flat = packed.reshape(-1)
    out, off = {}, 0
    for name, rows, cols, axis in group:
        sr, sc = _shard_shape(rows, cols, axis)
        out[name] = flat[off:off + sr * sc].reshape(1, sr, sc)
        off += sr * sc
    return out


def _pack_small(vals):
    rows = [jnp.pad(vals[n].reshape(-1), (0, PACK_COLS - vals[n].size)) for n in _SMALL]
    return jnp.stack(rows)


def _unpack_small(packed, like):
    return {n: packed[i, :like[n].size].reshape(like[n].shape) for i, n in enumerate(_SMALL)}


def _row(v):
    return jnp.pad(v.reshape(-1), (0, PACK_COLS - v.size))


def kernel(x, p, mix_norm, w_in, attn_q_norm, attn_k_norm, ret_decay_logit, ret_norm_gain, w_attn_o, w_ret_o, w_out, mlp_norm, w_up, w_down, ple_norm, w_ple_gate, w_ple, final_norm, loss_target, m_mix_norm, m_w_in, m_attn_q_norm, m_attn_k_norm, m_ret_decay_logit, m_ret_norm_gain, m_w_attn_o, m_w_ret_o, m_w_out, m_mlp_norm, m_w_up, m_w_down, m_ple_norm, m_w_ple_gate, m_w_ple, m_final_norm, v_mix_norm, v_w_in, v_attn_q_norm, v_attn_k_norm, v_ret_decay_logit, v_ret_norm_gain, v_w_attn_o, v_w_ret_o, v_w_out, v_mlp_norm, v_w_up, v_w_down, v_ple_norm, v_w_ple_gate, v_w_ple, v_final_norm):
    args = dict(locals())
    seq = x.shape[1]
    xs = x[0]
    ps = p[0, 0]
    tgt = loss_target[0]

    big_names = [b[0] for b in _BIG]
    wshard = _pack_shards([args[n] for n in big_names])
    win = _all_gather(w_in[0].astype(_MXU)).transpose(1, 0, 2).reshape(D, IN_W)

    g_mix, g_mlp, g_ple = mix_norm, mlp_norm, ple_norm
    g_fin = final_norm.reshape(1, D)
    gq = jnp.tile(attn_q_norm, (1, A_H))
    gk = jnp.tile(attn_k_norm, (1, A_KV))
    seg = _seg_mean_matrix()
    ca, sa, cr, sr = _rope_tables(seq)

    pa, pr, pg, h, rest_g = _in_proj(xs, g_mix, win, wshard.astype(_MXU))
    wfull = _unpack_gathered(rest_g)
    wao, wro, wout = wfull["w_attn_o"], wfull["w_ret_o"], wfull["w_out"]
    wup, wdown, wpg, wple = wfull["w_up"], wfull["w_down"], wfull["w_ple_gate"], wfull["w_ple"]
    qh, kh, vh, rqh, rkh = _qk_prep(pa, pr, gq, gk, seg, ca, sa, cr, sr)

    tq = _tile(seq, 128)
    tk = _tile(seq // 4, 1024)
    qt8 = qh.reshape(seq, A_H, A_HD).transpose(1, 2, 0)
    vta = jnp.stack([jnp.concatenate([_chunk_t(vh[:, g * A_HD:(g + 1) * A_HD], tk),
                                      jnp.ones((seq // tk, 16, tk), _MXU)], axis=1) for g in range(A_KV)])
    ot, lse = _attn_fwd(qt8, kh, vta, tq, tk)
    o = _heads_to_rows(ot)

    zb = jnp.broadcast_to(ret_decay_logit.reshape(2 * R_H, 1, 1), (2 * R_H, 1, LANES))
    tm_, tmw, tqd, tqdw, tkd, tkdw, tg, tgw = _ret_tables(zb)
    cb = _tile(seq // CHUNK, 4)
    yf, yb, pstf, pstb = _ret_fwd(rqh, rkh, pr, tm_, tqd, tkd, tg, cb)

    x1, merged, ret_in = _merge_fwd(xs, o, yf, yb, pr, pg, ret_norm_gain, wao, wro, wout)
    x2 = _mlp_fwd(x1, g_mlp, wup, wdown)

    dx2, de, dz, hp, loss_p, dg_fin, dg_ple = _ple_loss(x2, ps, tgt, g_ple, g_fin, wpg, wpg.T, wple)
    dx1, act, du, hm, dg_mlp = _mlp_bwd(x1, dx2, g_mlp, wup, wdown.T, wup.T)
    me = 4 * lax.axis_index("x") + 2 * lax.axis_index("y") + lax.axis_index("c")
    gpack_e = _pack_full_grads({"w_up": _wgrad(hm, du, "wgrad_up"), "w_down": _wgrad(act, dx2, "wgrad_down"),
                                "w_ple_gate": _wgrad(hp, dz, "wgrad_ple_gate"), "w_ple": _wgrad(ps, de, "wgrad_ple")},
                               _EARLY)
    dpg, dao, dro, do, dry, drg, dg_gn, land_e = _merge_bwd(dx1, o, yf, yb, pr, pg, ret_norm_gain, wao, wro,
                                                            wout.T, wao.T, wro.T, _bf(gpack_e))
    *rd, dlam = _ret_bwd(rqh, rkh, pr, dry, pstf, pstb, tm_, tmw, tqd, tqdw, tkd, tkdw, tg, tgw, cb)

    ksplit = 1
    tkb = _tile(seq // 4, 512)
    dot_ = do.reshape(seq, A_H, A_HD).transpose(1, 2, 0)
    dkt, dvt, dqt = _attn_bwd(qt8, ot, dot_, lse, kh, vh, _chunk_t(kh, tkb), tq, tkb, ksplit)
    dqh = _heads_to_rows(dqt[0])
    dpa, dpr, dg_q, dg_k = _qk_prep_bwd(pa, dqh, _chunks_to_rows(dkt), _chunks_to_rows(dvt), rd[0:3], rd[3:6], drg, gq, gk, seg, ca, sa, cr, sr)
    wg_in = jnp.concatenate([_wgrad(h, dpa, "wgrad_in_a"), _wgrad(h, dpr, "wgrad_in_r"),
                             _wgrad(h, dpg, "wgrad_in_g")], axis=1)
    gpack_l = _pack_full_grads({"w_attn_o": _wgrad(o, dao, "wgrad_attn_o"),
                                "w_ret_o": _wgrad(ret_in, dro, "wgrad_ret_o"),
                                "w_out": _wgrad(merged, dx1, "wgrad_out")}, _LATE)
    gpack_in = wg_in.reshape(D, N_DEV, IN_SHARD).transpose(1, 0, 2)
    grad_x, dg_mix, land_in, land_l = _in_proj_bwd(xs, dx1, g_mix, dpa, dpr, dpg, win.T,
                                                   [_bf(gpack_in), _bf(gpack_l)])
    small = jnp.stack(
        [_row(dg_mix), _row(dg_mlp), _row(dg_ple), _row(dg_fin), _row(dg_gn), _row(dlam[:, 0, 0]),
         _row(loss_p[0, 0:1]), jnp.zeros((PACK_COLS,), F32)]
        + [_row(dg_q[0, hh * A_HD:(hh + 1) * A_HD]) for hh in range(A_H)]
        + [_row(dg_k[0, hh * A_HD:(hh + 1) * A_HD]) for hh in range(A_KV)]
        + [jnp.zeros((PACK_COLS,), F32)] * (SMALL_ROWS - 18))

    own = lambda pack: lax.dynamic_index_in_dim(pack, me, axis=0, keepdims=False)
    (sland,) = _exchange_grads([small[None]])
    in_sh = _adamw_big(land_in, own(gpack_in), w_in[0], m_w_in[0], v_w_in[0], "adamw_w_in")
    group_sh = []
    for group, land, pack, name in ((_EARLY, land_e, gpack_e, "adamw_early"), (_LATE, land_l, gpack_l, "adamw_late")):
        packed = lambda pre: _pack_shards([args[pre + g[0]] for g in group])
        group_sh.append((group, _adamw_big(land, own(pack), packed(""), packed("m_"), packed("v_"), name)))
    g_sm, d_sm, m_sm, v_sm, loss_row = _adamw_small(
        sland, _pack_small({n: args[n] for n in _SMALL}), _pack_small({n: args["m_" + n] for n in _SMALL}),
        _pack_small({n: args["v_" + n] for n in _SMALL}))

    names = ["mix_norm", "w_in", "attn_q_norm", "attn_k_norm", "ret_decay_logit", "ret_norm_gain", "w_attn_o",
             "w_ret_o", "w_out", "mlp_norm", "w_up", "w_down", "ple_norm", "w_ple_gate", "w_ple", "final_norm"]
    like = {n: args[n] for n in _SMALL}
    outs = [loss_row[0, 0], grad_x[None]]
    for kind, sm in enumerate((g_sm, d_sm, m_sm, v_sm)):
        table = {**_unpack_small(sm, like), "w_in": in_sh[kind][None]}
        for group, res in group_sh:
            table.update(_unpack_shard(res[kind], group))
        outs += [table[n] for n in names]
    return tuple(outs)
```

```python
import functools

import jax
import jax.numpy as jnp
from jax import lax
from jax.experimental import pallas as pl
from jax.experimental.pallas import tpu as pltpu

F32 = jnp.float32
_MXU = jnp.bfloat16

D = 1024
PLE = 256
GRID_W = 64
A_HD = 64
A_H = 8
A_KV = 2
A_G = A_H // A_KV
AQ_W = A_H * A_HD
AKV_W = A_KV * A_HD
R_HD = 128
R_H = 4
R_W = R_H * R_HD
IN_W = AQ_W + 2 * AKV_W + 4 * R_W + 2 * D
PA_W = AQ_W + 2 * AKV_W
PR_W = 4 * R_W
PG_W = 2 * D
FF = 4 * D
CHUNK = 128
ROPE_THETA = 10000.0
EPS = 1e-6
GN_EPS = 1e-5
N_DEV = 8

LR, B1, B2, ADAM_EPS, WD, STEP = 0.001, 0.9, 0.999, 1e-08, 0.01, 10

LANES = 128
PACK_COLS = 1024
SMALL_ROWS = 24


def _tile(n, pref):
    t = min(n, pref)
    assert n % t == 0, (n, t)
    return t


def _bf(a):
    return a.astype(_MXU)


def _mm(a, b):
    return jnp.dot(_bf(a), _bf(b), preferred_element_type=F32)


def _mm_nt(a, b):
    return lax.dot_general(_bf(a), _bf(b), (((1,), (1,)), ((), ())), preferred_element_type=F32)


def _mm_tn(a, b):
    return lax.dot_general(_bf(a), _bf(b), (((0,), (0,)), ((), ())), preferred_element_type=F32)


def _seg_mean(v, segm):
    hi = _bf(v)
    lo = _bf(v - hi.astype(F32))
    return _mm(hi, segm) + _mm(lo, segm)


def _sigmoid(z):
    return 1.0 / (1.0 + jnp.exp(-z))


def _rms(x):
    r = lax.rsqrt(jnp.mean(x * x, axis=-1, keepdims=True) + EPS)
    return x * r, r


def _rms_bwd(n, r, gain, dy):
    dn = dy * gain
    return r * (dn - n * jnp.mean(dn * n, axis=-1, keepdims=True))


def _swap_halves(x, half):
    n = x.shape[-1]
    lane = lax.broadcasted_iota(jnp.int32, x.shape, x.ndim - 1)
    first = (lane % (2 * half)) < half
    return jnp.where(first, pltpu.roll(x, n - half, axis=1), pltpu.roll(x, half, axis=1))


def _rope(x, cos, sin, half):
    return x * cos + _swap_halves(x, half) * sin


def _cat(t, reps):
    return jnp.concatenate([t] * reps, axis=1)


def _full(shape):
    nd = len(shape)
    return pl.BlockSpec(shape, lambda *_: (0,) * nd)


def _rope_tables(seq):
    def tab(head_dim):
        n_axis = head_dim // 4
        freqs = ROPE_THETA ** (-jnp.arange(n_axis, dtype=F32) / n_axis)
        rows = seq // GRID_W
        row = jnp.repeat(jnp.arange(rows, dtype=F32), GRID_W)
        col = jnp.tile(jnp.arange(GRID_W, dtype=F32), rows)
        ang = jnp.concatenate([row[:, None] * freqs, col[:, None] * freqs], axis=-1)
        c, s = jnp.cos(ang), jnp.sin(ang)
        return jnp.concatenate([c, c], axis=-1), jnp.concatenate([-s, s], axis=-1)
    ca, sa = tab(A_HD)
    cr, sr = tab(R_HD)
    return jnp.tile(ca, (1, 2)), jnp.tile(sa, (1, 2)), cr, sr


def _seg_mean_matrix():
    i = jnp.arange(AQ_W) // A_HD
    return (i[:, None] == i[None, :]).astype(F32) / A_HD


def _mesh_pos():
    return lax.axis_index("x"), lax.axis_index("y"), lax.axis_index("c")


def _gather_steps(x_ref, out_ref, send_sems, recv_sems, local_sem, base=0):
    x, y, c = _mesh_pos()
    me, sibling = (x, y, c), (x, y, 1 - c)
    chips = [(1 - x, y), (x, 1 - y), (1 - x, 1 - y)]

    def slot(px, py, pc):
        return out_ref.at[4 * px + 2 * py + pc]

    def copy(k, block, to, src=None):
        return pltpu.make_async_remote_copy(
            src_ref=slot(*block) if src is None else src, dst_ref=slot(*block),
            send_sem=send_sems.at[base + k], recv_sem=recv_sems.at[base + k],
            device_id=to, device_id_type=pl.DeviceIdType.MESH)

    mine = pltpu.make_async_copy(x_ref, slot(*me), local_sem)
    first = [copy(0, me, sibling, src=x_ref)]
    first += [copy(1 + j, me, (*chip, c), src=x_ref) for j, chip in enumerate(chips)]
    passed = [copy(4 + j, (*chip, c), sibling) for j, chip in enumerate(chips)]

    def start():
        mine.start()
        for cp in first:
            cp.start()

    def finish():
        for j, chip in enumerate(chips):
            copy(1 + j, (*chip, c), me).wait_recv()
            passed[j].start()
        copy(0, sibling, me).wait_recv()
        for j, chip in enumerate(chips):
            copy(4 + j, (*chip, 1 - c), me).wait_recv()
        for cp in first + passed:
            cp.wait_send()
        mine.wait()

    return start, finish


def _exchange_steps(g_ref, land_ref, send_sems, recv_sems, local_sem, per_device, base=0):
    x, y, c = _mesh_pos()
    me = 4 * x + 2 * y + c

    def row(j):
        return g_ref.at[j if per_device else 0]

    def peer(k):
        p = (x ^ ((k >> 2) & 1), y ^ ((k >> 1) & 1), c ^ (k & 1))
        return p, 4 * p[0] + 2 * p[1] + p[2]

    def copy(k, src, dst):
        return pltpu.make_async_remote_copy(
            src_ref=src, dst_ref=dst, send_sem=send_sems.at[base + k - 1], recv_sem=recv_sems.at[base + k - 1],
            device_id=peer(k)[0], device_id_type=pl.DeviceIdType.MESH)

    own = pltpu.make_async_copy(row(me), land_ref.at[me], local_sem)
    sends = [copy(k, row(peer(k)[1]), land_ref.at[me]) for k in range(1, N_DEV)]

    def start():
        own.start()
        for cp in sends:
            cp.start()

    def finish():
        for k in range(1, N_DEV):
            copy(k, row(me), land_ref.at[peer(k)[1]]).wait_recv()
        for cp in sends:
            cp.wait_send()
        own.wait()

    return start, finish


_COMM_SCRATCH = [pltpu.SemaphoreType.DMA((7,)), pltpu.SemaphoreType.DMA((7,)), pltpu.SemaphoreType.DMA]
_ANY = pl.BlockSpec(memory_space=pl.ANY)


def _all_gather(shard):
    def body(x_ref, out_ref, send_sems, recv_sems, local_sem):
        start, finish = _gather_steps(x_ref, out_ref, send_sems, recv_sems, local_sem)
        start()
        finish()

    return pl.pallas_call(
        body, name="all_gather_weights", out_shape=jax.ShapeDtypeStruct((N_DEV,) + shard.shape, shard.dtype),
        in_specs=[_ANY], out_specs=_ANY, scratch_shapes=list(_COMM_SCRATCH),
    )(shard)


def _exchange_grads(packs):
    n = len(packs)

    def body(*refs):
        g_refs, land_refs = refs[:n], refs[n:2 * n]
        send_sems, recv_sems, local_sems = refs[2 * n:]
        steps = [_exchange_steps(g_refs[t], land_refs[t], send_sems, recv_sems, local_sems.at[t],
                                 packs[t].shape[0] == N_DEV, base=7 * t) for t in range(n)]
        for start, _ in steps:
            start()
        for _, finish in steps:
            finish()

    return pl.pallas_call(
        body, name="exchange_grads",
        out_shape=tuple(jax.ShapeDtypeStruct((N_DEV,) + g.shape[1:], g.dtype) for g in packs),
        in_specs=[_ANY] * n, out_specs=(_ANY,) * n,
        scratch_shapes=[pltpu.SemaphoreType.DMA((7 * n,)), pltpu.SemaphoreType.DMA((7 * n,)),
                        pltpu.SemaphoreType.DMA((n,))],
    )(*packs)


def _in_proj(x, gain, w, rest):
    seq = x.shape[0]
    tm = _tile(seq, 256)
    nsteps = seq // tm

    def body(x_ref, g_ref, w_ref, rest_ref, pa_ref, pr_ref, pg_ref, h_ref, gath_ref, send_sems, recv_sems, local_sem):
        start, finish = _gather_steps(rest_ref, gath_ref, send_sems, recv_sems, local_sem)
        pl.when(pl.program_id(0) == 0)(start)
        n, _ = _rms(x_ref[...])
        h = _bf(n * g_ref[...])
        h_ref[...] = h
        pa_ref[...] = _mm(h, w_ref[:, 0:PA_W])
        pr_ref[...] = _mm(h, w_ref[:, PA_W:PA_W + PR_W])
        pg_ref[...] = _mm(h, w_ref[:, PA_W + PR_W:IN_W])
        pl.when(pl.program_id(0) == nsteps - 1)(finish)

    row = lambda w_: pl.BlockSpec((tm, w_), lambda i: (i, 0))
    return pl.pallas_call(
        body, name="in_proj", grid=(nsteps,),
        in_specs=[row(D), _full((1, D)), _full((D, IN_W)), _ANY],
        out_specs=(row(PA_W), row(PR_W), row(PG_W), row(D), _ANY),
        out_shape=(jax.ShapeDtypeStruct((seq, PA_W), F32), jax.ShapeDtypeStruct((seq, PR_W), F32),
                   jax.ShapeDtypeStruct((seq, PG_W), F32), jax.ShapeDtypeStruct((seq, D), _MXU),
                   jax.ShapeDtypeStruct((N_DEV,) + rest.shape, rest.dtype)),
        scratch_shapes=list(_COMM_SCRATCH),
    )(x, gain, w, rest)


def _qk_prep(pa, pr, gq, gk, seg, ca, sa, cr, sr):
    seq = pa.shape[0]
    tm = _tile(seq, 256)

    def body(pa_ref, pr_ref, gq_ref, gk_ref, seg_ref, ca_ref, sa_ref, cr_ref, sr_ref,
             qh_ref, kh_ref, v_ref, rq_ref, rk_ref):
        q = pa_ref[:, 0:AQ_W]
        k = pa_ref[:, AQ_W:AQ_W + AKV_W]
        v_ref[...] = _bf(pa_ref[:, AQ_W + AKV_W:PA_W])
        ca_, sa_ = ca_ref[...], sa_ref[...]
        msq = _seg_mean(q * q, seg_ref[...])
        qn = q * lax.rsqrt(msq + EPS) * gq_ref[...]
        qh_ref[...] = _bf(_rope(qn, _cat(ca_, 4), _cat(sa_, 4), A_HD // 2) * (A_HD ** -0.5))
        msk = _seg_mean(k * k, seg_ref[0:AKV_W, 0:AKV_W])
        kn = k * lax.rsqrt(msk + EPS) * gk_ref[...]
        kh_ref[...] = _bf(_rope(kn, ca_, sa_, A_HD // 2))
        cr_, sr_ = _cat(cr_ref[...], 4), _cat(sr_ref[...], 4)
        rq_ref[...] = _rope(pr_ref[:, 0:R_W], cr_, sr_, R_HD // 2) * (R_HD ** -0.5)
        rk_ref[...] = _rope(pr_ref[:, R_W:2 * R_W], cr_, sr_, R_HD // 2)

    row = lambda w_: pl.BlockSpec((tm, w_), lambda i: (i, 0))
    return pl.pallas_call(
        body, name="qk_prep", grid=(seq // tm,),
        in_specs=[row(PA_W), row(2 * R_W), _full((1, AQ_W)), _full((1, AKV_W)), _full((AQ_W, AQ_W)),
                  row(LANES), row(LANES), row(LANES), row(LANES)],
        out_specs=(row(AQ_W), row(AKV_W), row(AKV_W), row(R_W), row(R_W)),
        out_shape=(jax.ShapeDtypeStruct((seq, AQ_W), _MXU), jax.ShapeDtypeStruct((seq, AKV_W), _MXU),
                   jax.ShapeDtypeStruct((seq, AKV_W), _MXU), jax.ShapeDtypeStruct((seq, R_W), F32),
                   jax.ShapeDtypeStruct((seq, R_W), F32)),
    )(pa, pr, gq, gk, seg, ca, sa, cr, sr)


def _chunk_t(a, tk):
    seq = a.shape[0]
    return a.reshape(seq // tk, tk, a.shape[1]).transpose(0, 2, 1)


def _heads_to_rows(t):
    return t.transpose(2, 0, 1).reshape(t.shape[2], AQ_W)


def _attn_fwd(qt8, k2, vta, tq, tk):
    seq = k2.shape[0]
    nck = seq // tk
    rows = A_G * tq
    vrows = vta.shape[2]
    rb = _tile(tk, 64)
    assert nck % 2 == 0, nck

    def body(qt_ref, k_ref, vt_ref, o_ref, lse_ref, m_sc, acc_sc, qtp_sc, s_a, s_b, p_a, p_b, al_a, al_b):
        g = pl.program_id(0)
        qtp_sc[...] = jnp.zeros_like(qtp_sc)
        qtp_sc[pl.ds(pl.multiple_of(g * A_HD, A_HD), A_HD), :] = jnp.concatenate(
            [qt_ref[a] for a in range(A_G)], axis=1)
        m_sc[...] = jnp.full((1, rows), -jnp.inf, F32)
        acc_sc[...] = jnp.zeros_like(acc_sc)

        def scores(c):
            kc = k_ref[pl.ds(pl.multiple_of(c * tk, tk), tk), :]
            return _mm(kc, qtp_sc[...])

        def stage(c, s_cur, s_nxt, p_cur, p_prv, al_cur, al_prv, first=False, last=False):
            if not last:
                s_nxt[...] = scores(c + 1)
            if not first:
                acc_sc[...] = al_prv[...] * acc_sc[...] + _mm(vt_ref[0, c - 1], p_prv[...])
            m_old = m_sc[...]
            mx = None
            for r in range(0, tk, rb):
                bm = jnp.max(s_cur[r:r + rb, :].reshape(rb // 8, 8, rows), axis=0)
                mx = bm if mx is None else jnp.maximum(mx, bm)
            m_new = jnp.maximum(m_old, jnp.max(mx, axis=0, keepdims=True))
            for r in range(0, tk, rb):
                p_cur[r:r + rb, :] = _bf(jnp.exp(s_cur[r:r + rb, :] - m_new))
            al_cur[...] = jnp.exp(m_old - m_new)
            m_sc[...] = m_new

        s_a[...] = scores(0)
        stage(0, s_a, s_b, p_a, p_b, al_a, al_b, first=True)

        def pair(j, carry):
            stage(2 * j + 1, s_b, s_a, p_b, p_a, al_b, al_a)
            stage(2 * j + 2, s_a, s_b, p_a, p_b, al_a, al_b)
            return carry

        lax.fori_loop(0, nck // 2 - 1, pair, 0)
        stage(nck - 1, s_b, s_a, p_b, p_a, al_b, al_a, last=True)
        acc = al_b[...] * acc_sc[...] + _mm(vt_ref[0, nck - 1], p_b[...])
        l = acc[A_HD:A_HD + 1, :]
        lse = m_sc[...] + jnp.log(l)
        out = acc[0:A_HD, :] * (1.0 / l)
        for a in range(A_G):
            o_ref[a] = out[:, a * tq:(a + 1) * tq]
            lse_ref[a] = lse[:, a * tq:(a + 1) * tq]

    return pl.pallas_call(
        body, name="attn_fwd", grid=(A_KV, seq // tq),
        in_specs=[pl.BlockSpec((A_G, A_HD, tq), lambda g, i: (g, 0, i)),
                  _full((seq, LANES)), pl.BlockSpec((1, nck, vrows, tk), lambda g, i: (g, 0, 0, 0))],
        out_specs=(pl.BlockSpec((A_G, A_HD, tq), lambda g, i: (g, 0, i)),
                   pl.BlockSpec((A_G, 1, tq), lambda g, i: (g, 0, i))),
        out_shape=(jax.ShapeDtypeStruct((A_H, A_HD, seq), F32), jax.ShapeDtypeStruct((A_H, 1, seq), F32)),
        scratch_shapes=[pltpu.VMEM((1, rows), F32), pltpu.VMEM((vrows, rows), F32), pltpu.VMEM((LANES, rows), _MXU),
                        pltpu.VMEM((tk, rows), F32), pltpu.VMEM((tk, rows), F32),
                        pltpu.VMEM((tk, rows), _MXU), pltpu.VMEM((tk, rows), _MXU),
                        pltpu.VMEM((1, rows), F32), pltpu.VMEM((1, rows), F32)],
    )(qt8, k2, vta)


def _attn_bwd(qt8, ot, dot_, lse, k2, v2, k2t, tq, tk, ksplit):
    seq = k2.shape[0]
    sh = seq // ksplit
    nck = sh // tk
    rows = A_G * tq
    rb = _tile(tk, 32768 // rows)
    assert nck % 2 == 0, nck

    def body(qt_ref, ot_ref, dot_ref, lse_ref, k_ref, v_ref, kt_ref,
             dk_ref, dv_ref, dq_ref, dq_sc, qtp_sc, dotp_sc, pt_sc, dst_sc,
             s_a, s_b, dp_a, dp_b, p_a, p_b, ds_a, ds_b):
        g = pl.program_id(1)
        hrows = pl.ds(pl.multiple_of(g * A_HD, A_HD), A_HD)

        @pl.when(pl.program_id(2) == 0)
        def _():
            dk_ref[...] = jnp.zeros_like(dk_ref)
            dv_ref[...] = jnp.zeros_like(dv_ref)

        lse_row = jnp.concatenate([lse_ref[a] for a in range(A_G)], axis=1)
        dd = jnp.concatenate([jnp.sum(ot_ref[a] * dot_ref[a], axis=0, keepdims=True)
                              for a in range(A_G)], axis=1)
        qtp_sc[...] = jnp.zeros_like(qtp_sc)
        dotp_sc[...] = jnp.zeros_like(dotp_sc)
        qtp_sc[hrows, :] = jnp.concatenate([qt_ref[a] for a in range(A_G)], axis=1)
        dotp_sc[hrows, :] = _bf(jnp.concatenate([dot_ref[a] for a in range(A_G)], axis=1))
        dq_sc[...] = jnp.zeros_like(dq_sc)

        def products(c, s_ref, dp_ref):
            sl = pl.ds(pl.multiple_of(c * tk, tk), tk)
            s_ref[...] = _mm(k_ref[sl, :], qtp_sc[...])
            dp_ref[...] = _mm(v_ref[sl, :], dotp_sc[...])

        def accumulate(c, p_ref, ds_ref):
            pt_sc[...] = p_ref[...].T
            dst_sc[...] = ds_ref[...].T
            dq_sc[...] += _mm(kt_ref[c, hrows, :], ds_ref[...])
            dv_ref[0, c] += _mm(dotp_sc[hrows, :], pt_sc[...])
            dk_ref[0, c] += _mm(qtp_sc[hrows, :], dst_sc[...])

        def stage(c, s_cur, dp_cur, s_nxt, dp_nxt, p_cur, ds_cur, p_prv, ds_prv, first=False, last=False):
            if not last:
                products(c + 1, s_nxt, dp_nxt)
            if not first:
                accumulate(c - 1, p_prv, ds_prv)
            for r in range(0, tk, rb):
                p = jnp.exp(s_cur[r:r + rb, :] - lse_row)
                p_cur[r:r + rb, :] = _bf(p)
                ds_cur[r:r + rb, :] = _bf(p * (dp_cur[r:r + rb, :] - dd))

        products(0, s_a, dp_a)
        stage(0, s_a, dp_a, s_b, dp_b, p_a, ds_a, p_b, ds_b, first=True)

        def pair(j, carry):
            stage(2 * j + 1, s_b, dp_b, s_a, dp_a, p_b, ds_b, p_a, ds_a)
            stage(2 * j + 2, s_a, dp_a, s_b, dp_b, p_a, ds_a, p_b, ds_b)
            return carry

        lax.fori_loop(0, nck // 2 - 1, pair, 0)
        stage(nck - 1, s_b, dp_b, s_a, dp_a, p_b, ds_b, p_a, ds_a, last=True)
        accumulate(nck - 1, p_b, ds_b)
        for a in range(A_G):
            dq_ref[0, a] = dq_sc[:, a * tq:(a + 1) * tq]

    tspec = pl.BlockSpec((A_G, A_HD, tq), lambda s, g, i: (g, 0, i))
    kspec = pl.BlockSpec((sh, LANES), lambda s, g, i: (s, 0))
    gspec = pl.BlockSpec((1, nck, A_HD, tk), lambda s, g, i: (g, s, 0, 0))
    gshape = jax.ShapeDtypeStruct((A_KV, seq // tk, A_HD, tk), F32)
    big = lambda dt: pltpu.VMEM((tk, rows), dt)
    bigt = pltpu.VMEM((rows, tk), _MXU)
    return pl.pallas_call(
        body, name="attn_bwd", grid=(ksplit, A_KV, seq // tq),
        in_specs=[tspec, tspec, tspec, pl.BlockSpec((A_G, 1, tq), lambda s, g, i: (g, 0, i)),
                  kspec, kspec, pl.BlockSpec((nck, LANES, tk), lambda s, g, i: (s, 0, 0))],
        out_specs=(gspec, gspec, pl.BlockSpec((1, A_G, A_HD, tq), lambda s, g, i: (s, g, 0, i))),
        out_shape=(gshape, gshape, jax.ShapeDtypeStruct((ksplit, A_H, A_HD, seq), F32)),
        scratch_shapes=[pltpu.VMEM((A_HD, rows), F32), pltpu.VMEM((LANES, rows), _MXU), pltpu.VMEM((LANES, rows), _MXU),
                        bigt, bigt,
                        big(F32), big(F32), big(F32), big(F32), big(_MXU), big(_MXU), big(_MXU), big(_MXU)],
    )(qt8, ot, dot_, lse, k2, v2, k2t)


def _chunks_to_rows(t):
    return t.transpose(1, 3, 0, 2).reshape(t.shape[1] * t.shape[3], AKV_W)


def _ret_tables(zb):
    c = CHUNK

    def body(z_ref, m_ref, mw_ref, qd_ref, qdw_ref, kd_ref, kdw_ref, g_ref, gw_ref):
        fwd = pl.program_id(0) < R_H
        z = z_ref[0]
        lam = jnp.minimum(z, 0.0) - jnp.log(1.0 + jnp.exp(-jnp.abs(z)))
        i = lax.broadcasted_iota(jnp.int32, (c, c), 0).astype(F32)
        j = lax.broadcasted_iota(jnp.int32, (c, c), 1).astype(F32)
        diff = jnp.where(fwd, i - j, j - i)
        keep = diff >= jnp.where(fwd, 0.0, 1.0)
        dist = jnp.maximum(diff, 0.0)
        m = jnp.where(keep, jnp.exp(lam * dist), 0.0)
        m_ref[0] = m
        mw_ref[0] = m * dist
        fq = jnp.where(fwd, i + 1.0, c - i)
        qd = jnp.exp(lam * fq)
        qd_ref[0] = qd
        qdw_ref[0] = qd * fq
        fk = jnp.where(fwd, c - 1.0 - i, i)
        kd = jnp.exp(lam * fk)
        kd_ref[0] = kd
        kdw_ref[0] = kd * fk
        gdec = jnp.exp(lam * c)
        g_ref[0] = gdec
        gw_ref[0] = gdec * c

    big = pl.BlockSpec((1, c, c), lambda t: (t, 0, 0))
    vec = pl.BlockSpec((1, 1, LANES), lambda t: (t, 0, 0))
    bshape = jax.ShapeDtypeStruct((2 * R_H, c, c), F32)
    vshape = jax.ShapeDtypeStruct((2 * R_H, 1, LANES), F32)
    return pl.pallas_call(
        body, name="ret_tables", grid=(2 * R_H,), in_specs=[vec],
        out_specs=(big, big, big, big, big, big, vec, vec),
        out_shape=(bshape,) * 6 + (vshape, vshape),
    )(zb)


def _ret_fwd(rq, rk, pr, m, qd, kd, gdec, cb):
    seq = rq.shape[0]
    c = CHUNK
    ns = seq // (cb * c)

    def body(qf_ref, kf_ref, vf_ref, qb_ref, kb_ref, vb_ref, m_ref, qd_ref, kd_ref, g_ref,
             yf_ref, yb_ref, pstf_ref, pstb_ref, p_sc):
        @pl.when(pl.program_id(0) == 0)
        def _():
            p_sc[...] = jnp.zeros_like(p_sc)

        dirs = ((qf_ref, kf_ref, vf_ref, yf_ref, pstf_ref), (qb_ref, kb_ref, vb_ref, yb_ref, pstb_ref))
        heads = [slice(h * R_HD, (h + 1) * R_HD) for h in range(R_H)]

        def chunk(j, carry):
            early = []
            for d, (q_ref, k_ref, v_ref, _, _) in enumerate(dirs):
                cc = j if d == 0 else cb - 1 - j
                sl = pl.ds(pl.multiple_of(cc * c, c), c)
                for h, hs in enumerate(heads):
                    t = d * R_H + h
                    early.append((_mm_nt(q_ref[sl, hs], k_ref[sl, hs]), _mm(q_ref[sl, hs] * qd_ref[t], p_sc[t]),
                                  _mm_tn(k_ref[sl, hs] * kd_ref[t], v_ref[sl, hs])))
            for d, (_, _, v_ref, y_ref, pst_ref) in enumerate(dirs):
                cc = j if d == 0 else cb - 1 - j
                sl = pl.ds(pl.multiple_of(cc * c, c), c)
                for h, hs in enumerate(heads):
                    t = d * R_H + h
                    qk, qp, kv = early[t]
                    p = p_sc[t]
                    pst_ref[h, cc] = p
                    y_ref[sl, hs] = _mm(qk * m_ref[t], v_ref[sl, hs]) + qp
                    p_sc[t] = p * g_ref[t] + kv
            return carry

        lax.fori_loop(0, cb, chunk, 0)

    asc = lambda off: pl.BlockSpec((cb * c, R_W), lambda n: (n, off))
    desc = lambda off: pl.BlockSpec((cb * c, R_W), lambda n: (ns - 1 - n, off))
    return pl.pallas_call(
        body, name="ret_fwd", grid=(ns,),
        in_specs=[asc(0), asc(0), asc(2), desc(0), desc(0), desc(2),
                  _full((2 * R_H, c, c)), _full((2 * R_H, c, c)), _full((2 * R_H, c, c)), _full((2 * R_H, 1, LANES))],
        out_specs=(asc(0), desc(0),
                   pl.BlockSpec((R_H, cb, R_HD, R_HD), lambda n: (0, n, 0, 0)),
                   pl.BlockSpec((R_H, cb, R_HD, R_HD), lambda n: (0, ns - 1 - n, 0, 0))),
        out_shape=(jax.ShapeDtypeStruct((seq, R_W), F32), jax.ShapeDtypeStruct((seq, R_W), F32),
                   jax.ShapeDtypeStruct((R_H, seq // c, R_HD, R_HD), F32),
                   jax.ShapeDtypeStruct((R_H, seq // c, R_HD, R_HD), F32)),
        scratch_shapes=[pltpu.VMEM((2 * R_H, R_HD, R_HD), F32)],
    )(rq, rk, pr, rq, rk, pr, m, qd, kd, gdec)


def _ret_bwd(rq, rk, pr, dry, pstf, pstb, m, mw, qd, qdw, kd, kdw, gdec, gw, cb):
    seq = rq.shape[0]
    c = CHUNK
    ns = seq // (cb * c)

    def body(qf_ref, kf_ref, vf_ref, dyf_ref, pstf_ref, qb_ref, kb_ref, vb_ref, dyb_ref, pstb_ref,
             m_ref, mw_ref, qd_ref, qdw_ref, kd_ref, kdw_ref, g_ref, gw_ref,
             dqf_ref, dkf_ref, dvf_ref, dqb_ref, dkb_ref, dvb_ref, dlam_ref, r_sc, acc_sc, e_sc, g_sc):
        n = pl.program_id(0)

        @pl.when(n == 0)
        def _():
            r_sc[...] = jnp.zeros_like(r_sc)
            acc_sc[...] = jnp.zeros_like(acc_sc)
            e_sc[...] = jnp.zeros_like(e_sc)
            g_sc[...] = jnp.zeros_like(g_sc)

        dirs = ((qf_ref, kf_ref, vf_ref, dyf_ref, pstf_ref, dqf_ref, dkf_ref, dvf_ref),
                (qb_ref, kb_ref, vb_ref, dyb_ref, pstb_ref, dqb_ref, dkb_ref, dvb_ref))
        heads = [slice(h * R_HD, (h + 1) * R_HD) for h in range(R_H)]

        def chunk(j, carry):
            first = []
            for d, (q_ref, k_ref, v_ref, dy_ref, pst_ref, _, _, _) in enumerate(dirs):
                cc = cb - 1 - j if d == 0 else j
                sl = pl.ds(pl.multiple_of(cc * c, c), c)
                for h, hs in enumerate(heads):
                    t = d * R_H + h
                    q, k, v, dy = q_ref[sl, hs], k_ref[sl, hs], v_ref[sl, hs], dy_ref[sl, hs]
                    r = r_sc[t]
                    first.append((_mm_nt(q, k), _mm_nt(dy, v), _mm_nt(dy, pst_ref[h, cc]), _mm_nt(v, r),
                                  _mm(k * kd_ref[t], r), _mm_tn(q * qd_ref[t], dy)))
            for d, (q_ref, k_ref, _, dy_ref, pst_ref, dq_ref, dk_ref, dv_ref) in enumerate(dirs):
                cc = cb - 1 - j if d == 0 else j
                sl = pl.ds(pl.multiple_of(cc * c, c), c)
                for h, hs in enumerate(heads):
                    t = d * R_H + h
                    qk, ds, dyp, vr, kr, qdy = first[t]
                    q, k, dy = q_ref[sl, hs], k_ref[sl, hs], dy_ref[sl, hs]
                    r = r_sc[t]
                    da = ds * m_ref[t]
                    dv_ref[sl, hs] = _mm_tn(qk * m_ref[t], dy) + kr
                    dq_ref[sl, hs] = _mm(da, k) + dyp * qd_ref[t]
                    dk_ref[sl, hs] = _mm_tn(da, q) + vr * kd_ref[t]
                    acc_sc[t] += dyp * q * qdw_ref[t] + vr * k * kdw_ref[t]
                    e_sc[t] += ds * qk * mw_ref[t]
                    g_sc[t] += r * pst_ref[h, cc]
                    r_sc[t] = r * g_ref[t] + qdy
            return carry

        lax.fori_loop(0, cb, chunk, 0)

        @pl.when(n == ns - 1)
        def _():
            for t in range(2 * R_H):
                tot = jnp.sum(jnp.sum(acc_sc[t] + e_sc[t] + g_sc[t] * gw_ref[t], axis=0, keepdims=True),
                              axis=1, keepdims=True)
                dlam_ref[t] = jnp.broadcast_to(tot, (1, LANES))

    asc = lambda off: pl.BlockSpec((cb * c, R_W), lambda n: (n, off))
    desc = lambda off: pl.BlockSpec((cb * c, R_W), lambda n: (ns - 1 - n, off))
    big = _full((2 * R_H, c, c))
    vec = _full((2 * R_H, 1, LANES))
    oshape = jax.ShapeDtypeStruct((seq, R_W), F32)
    sq = pltpu.VMEM((2 * R_H, R_HD, R_HD), F32)
    return pl.pallas_call(
        body, name="ret_bwd", grid=(ns,),
        in_specs=[desc(0), desc(0), desc(2), desc(0),
                  pl.BlockSpec((R_H, cb, R_HD, R_HD), lambda n: (0, ns - 1 - n, 0, 0)),
                  asc(0), asc(0), asc(2), asc(0),
                  pl.BlockSpec((R_H, cb, R_HD, R_HD), lambda n: (0, n, 0, 0)),
                  big, big, big, big, big, big, vec, vec],
        out_specs=(desc(0), desc(0), desc(0), asc(0), asc(0), asc(0), vec),
        out_shape=(oshape,) * 6 + (jax.ShapeDtypeStruct((2 * R_H, 1, LANES), F32),),
        scratch_shapes=[sq, sq, sq, sq],
    )(rq, rk, pr, dry, pstf, rq, rk, pr, dry, pstb, m, mw, qd, qdw, kd, kdw, gdec, gw)


def _group_norm(ry):
    yn, rs = [], []
    for h in range(R_H):
        s = ry[:, h * R_HD:(h + 1) * R_HD]
        mu = jnp.mean(s, axis=-1, keepdims=True)
        cen = s - mu
        r = lax.rsqrt(jnp.mean(cen * cen, axis=-1, keepdims=True) + GN_EPS)
        yn.append(cen * r)
        rs.append(r)
    return yn, rs


def _merge_fwd(x, o, yf, yb, pr, pg, gain_r, wao, wro, wout):
    seq = x.shape[0]
    tm = _tile(seq, 256)

    def body(x_ref, o_ref, yf_ref, yb_ref, rg_ref, ga_ref, gr_ref, gn_ref, wao_ref, wro_ref, wout_ref,
             x1_ref, mg_ref, ri_ref):
        yn, _ = _group_norm(yf_ref[...] + yb_ref[...])
        rg = rg_ref[...]
        ret_in = jnp.concatenate(yn, axis=1) * gn_ref[...] * (rg * _sigmoid(rg))
        ri_ref[...] = _bf(ret_in)
        attn_out = _mm(o_ref[...], wao_ref[...])
        ret_out = _mm(ret_in, wro_ref[...])
        merged = _sigmoid(ga_ref[...]) * attn_out + _sigmoid(gr_ref[...]) * ret_out
        mg_ref[...] = _bf(merged)
        x1_ref[...] = x_ref[...] + _mm(merged, wout_ref[...])

    row = lambda w_, j=0: pl.BlockSpec((tm, w_), lambda i: (i, j))
    return pl.pallas_call(
        body, name="merge_fwd", grid=(seq // tm,),
        in_specs=[row(D), row(AQ_W), row(R_W), row(R_W), row(R_W, 3), row(D, 0), row(D, 1),
                  _full((1, R_W)), _full((AQ_W, D)), _full((R_W, D)), _full((D, D))],
        out_specs=(row(D), row(D), row(R_W)),
        out_shape=(jax.ShapeDtypeStruct((seq, D), F32), jax.ShapeDtypeStruct((seq, D), _MXU),
                   jax.ShapeDtypeStruct((seq, R_W), _MXU)),
    )(x, o, yf, yb, pr, pg, pg, gain_r, wao, wro, wout)


def _mlp_fwd(x1, gain, wup, wdown):
    seq = x1.shape[0]
    tm = _tile(seq, 512)
    fc = 2048
    nfc = FF // fc

    def body(x_ref, g_ref, wu_ref, wd_ref, x2_ref, hm_sc, acc_sc):
        c = pl.program_id(1)

        @pl.when(c == 0)
        def _():
            n, _ = _rms(x_ref[...])
            hm_sc[...] = _bf(n * g_ref[...])
            acc_sc[...] = jnp.zeros_like(acc_sc)

        halves = (slice(0, fc // 2), slice(fc // 2, fc))
        ups = [jnp.maximum(_mm(hm_sc[...], wu_ref[:, hs]), 0.0) for hs in halves]
        acc_sc[...] += _mm(ups[0] * ups[0], wd_ref[halves[0], :]) + _mm(ups[1] * ups[1], wd_ref[halves[1], :])

        @pl.when(c == nfc - 1)
        def _():
            x2_ref[...] = x_ref[...] + acc_sc[...]

    return pl.pallas_call(
        body, name="mlp_fwd", grid=(seq // tm, nfc),
        in_specs=[pl.BlockSpec((tm, D), lambda i, c: (i, 0)), pl.BlockSpec((1, D), lambda i, c: (0, 0)),
                  pl.BlockSpec((D, fc), lambda i, c: (0, c)), pl.BlockSpec((fc, D), lambda i, c: (c, 0))],
        out_specs=pl.BlockSpec((tm, D), lambda i, c: (i, 0)),
        out_shape=jax.ShapeDtypeStruct((seq, D), F32),
        scratch_shapes=[pltpu.VMEM((tm, D), _MXU), pltpu.VMEM((tm, D), F32)],
    )(x1, gain, wup, wdown)


def _ple_loss(x2, p, tgt, g_ple, g_fin, wpg, wpgt, wple):
    seq = x2.shape[0]
    tm = _tile(seq, 256)

    def body(x2_ref, p_ref, t_ref, gp_ref, gf_ref, wpg_ref, wpgt_ref, wple_ref,
             dx2_ref, de_ref, dz_ref, hp_ref, loss_ref, dgf_ref, dgp_ref):
        @pl.when(pl.program_id(0) == 0)
        def _():
            loss_ref[...] = jnp.zeros_like(loss_ref)
            dgf_ref[...] = jnp.zeros_like(dgf_ref)
            dgp_ref[...] = jnp.zeros_like(dgp_ref)

        x2 = x2_ref[...]
        gp, gf = gp_ref[...], gf_ref[...]
        n2, r2 = _rms(x2)
        hp = _bf(n2 * gp)
        hp_ref[...] = hp
        gate = _sigmoid(_mm(hp, wpg_ref[...]))
        e = _mm(p_ref[...], wple_ref[...])
        x3 = x2 + gate * e
        n3, r3 = _rms(x3)
        diff = n3 * gf - t_ref[...]
        row_loss = jnp.mean(diff * diff, axis=-1, keepdims=True)
        loss_ref[...] += 0.5 * jnp.sum(row_loss, axis=0, keepdims=True)
        dy = diff * (1.0 / D)
        dgf_ref[...] += jnp.sum(dy * n3, axis=0, keepdims=True)
        dx3 = _rms_bwd(n3, r3, gf, dy)
        de_ref[...] = _bf(dx3 * gate)
        dz = dx3 * e * gate * (1.0 - gate)
        dz_ref[...] = _bf(dz)
        dhp = _mm(dz, wpgt_ref[...])
        dgp_ref[...] += jnp.sum(dhp * n2, axis=0, keepdims=True)
        dx2_ref[...] = dx3 + _rms_bwd(n2, r2, gp, dhp)

    row = lambda w_: pl.BlockSpec((tm, w_), lambda i: (i, 0))
    act = lambda dt: jax.ShapeDtypeStruct((seq, D), dt)
    return pl.pallas_call(
        body, name="ple_loss", grid=(seq // tm,),
        in_specs=[row(D), row(PLE), row(D), _full((1, D)), _full((1, D)),
                  _full((D, D)), _full((D, D)), _full((PLE, D))],
        out_specs=(row(D), row(D), row(D), row(D), _full((1, LANES)), _full((1, D)), _full((1, D))),
        out_shape=(act(F32), act(_MXU), act(_MXU), act(_MXU), jax.ShapeDtypeStruct((1, LANES), F32),
                   jax.ShapeDtypeStruct((1, D), F32), jax.ShapeDtypeStruct((1, D), F32)),
    )(x2, p, tgt, g_ple, g_fin, wpg, wpgt, wple)


def _mlp_bwd(x1, dx2, gain, wup, wdownt, wupt):
    seq = x1.shape[0]
    tm = _tile(seq, 512)
    fc = 2048
    nfc = FF // fc

    def body(x_ref, dx2_ref, g_ref, wu_ref, wdt_ref, wut_ref,
             dx1_ref, a_ref, du_ref, hm_ref, dg_ref, dhm_sc):
        i = pl.program_id(0)
        c = pl.program_id(1)

        @pl.when((i == 0) & (c == 0))
        def _():
            dg_ref[...] = jnp.zeros_like(dg_ref)

        @pl.when(c == 0)
        def _():
            n, _ = _rms(x_ref[...])
            hm_ref[...] = _bf(n * g_ref[...])
            dhm_sc[...] = jnp.zeros_like(dhm_sc)

        halves = (slice(0, fc // 2), slice(fc // 2, fc))
        ups = [jnp.maximum(_mm(hm_ref[...], wu_ref[:, hs]), 0.0) for hs in halves]
        das = [_mm(dx2_ref[...], wdt_ref[:, hs]) for hs in halves]
        part = None
        for u, da, hs in zip(ups, das, halves):
            a_ref[:, hs] = _bf(u * u)
            du = _bf(da * (2.0 * u))
            du_ref[:, hs] = du
            t = _mm(du, wut_ref[hs, :])
            part = t if part is None else part + t
        dhm_sc[...] += part

        @pl.when(c == nfc - 1)
        def _():
            n, r = _rms(x_ref[...])
            dhm = dhm_sc[...]
            dg_ref[...] += jnp.sum(dhm * n, axis=0, keepdims=True)
            dx1_ref[...] = dx2_ref[...] + _rms_bwd(n, r, g_ref[...], dhm)

    rowd = pl.BlockSpec((tm, D), lambda i, c: (i, 0))
    rowf = pl.BlockSpec((tm, fc), lambda i, c: (i, c))
    return pl.pallas_call(
        body, name="mlp_bwd", grid=(seq // tm, nfc),
        in_specs=[rowd, rowd, pl.BlockSpec((1, D), lambda i, c: (0, 0)),
                  pl.BlockSpec((D, fc), lambda i, c: (0, c)), pl.BlockSpec((D, fc), lambda i, c: (0, c)),
                  pl.BlockSpec((fc, D), lambda i, c: (c, 0))],
        out_specs=(rowd, rowf, rowf, rowd, pl.BlockSpec((1, D), lambda i, c: (0, 0))),
        out_shape=(jax.ShapeDtypeStruct((seq, D), F32), jax.ShapeDtypeStruct((seq, FF), _MXU),
                   jax.ShapeDtypeStruct((seq, FF), _MXU), jax.ShapeDtypeStruct((seq, D), _MXU),
                   jax.ShapeDtypeStruct((1, D), F32)),
        scratch_shapes=[pltpu.VMEM((tm, D), F32)],
    )(x1, dx2, gain, wup, wdownt, wupt)


def _merge_bwd(dx1, o, yf, yb, pr, pg, gain_r, wao, wro, woutt, waot, wrot, gpack):
    seq = dx1.shape[0]
    tm = _tile(seq, 256)
    nsteps = seq // tm

    def body(dx1_ref, o_ref, yf_ref, yb_ref, rg_ref, ga_ref, gr_ref, gn_ref, wao_ref, wro_ref,
             woutt_ref, waot_ref, wrot_ref, gpack_ref,
             dpg_ref, dao_ref, dro_ref, do_ref, dry_ref, drg_ref, dgn_ref, land_ref,
             send_sems, recv_sems, local_sem):
        start, finish = _exchange_steps(gpack_ref, land_ref, send_sems, recv_sems, local_sem, True)
        pl.when(pl.program_id(0) == 0)(start)

        @pl.when(pl.program_id(0) == 0)
        def _():
            dgn_ref[...] = jnp.zeros_like(dgn_ref)

        yn_l, rs_l = _group_norm(yf_ref[...] + yb_ref[...])
        yn = jnp.concatenate(yn_l, axis=1)
        rg = rg_ref[...]
        gn = gn_ref[...]
        sg = _sigmoid(rg)
        sil = rg * sg
        ret_in = yn * gn * sil
        attn_out = _mm(o_ref[...], wao_ref[...])
        ret_out = _mm(ret_in, wro_ref[...])
        sa = _sigmoid(ga_ref[...])
        sr = _sigmoid(gr_ref[...])
        dm = _mm(dx1_ref[...], woutt_ref[...])
        dpg_ref[:, 0:D] = _bf(dm * attn_out * sa * (1.0 - sa))
        dpg_ref[:, D:2 * D] = _bf(dm * ret_out * sr * (1.0 - sr))
        dao = _bf(dm * sa)
        dro = _bf(dm * sr)
        dao_ref[...] = dao
        dro_ref[...] = dro
        do_ref[...] = _mm(dao, waot_ref[...])
        dri = _mm(dro, wrot_ref[...])
        dgn_ref[...] += jnp.sum(dri * yn * sil, axis=0, keepdims=True)
        drg_ref[...] = _bf(dri * yn * gn * (sg * (1.0 + rg * (1.0 - sg))))
        dyn = dri * gn * sil
        dry = []
        for h in range(R_H):
            dh = dyn[:, h * R_HD:(h + 1) * R_HD]
            dry.append(rs_l[h] * (dh - jnp.mean(dh, axis=-1, keepdims=True)
                                  - yn_l[h] * jnp.mean(dh * yn_l[h], axis=-1, keepdims=True)))
        dry_ref[...] = jnp.concatenate(dry, axis=1)
        pl.when(pl.program_id(0) == nsteps - 1)(finish)

    row = lambda w_, j=0: pl.BlockSpec((tm, w_), lambda i: (i, j))
    return pl.pallas_call(
        body, name="merge_bwd", grid=(nsteps,),
        in_specs=[row(D), row(AQ_W), row(R_W), row(R_W), row(R_W, 3), row(D, 0), row(D, 1),
                  _full((1, R_W)), _full((AQ_W, D)), _full((R_W, D)), _full((D, D)),
                  _full((D, AQ_W)), _full((D, R_W)), _ANY],
        out_specs=(row(PG_W), row(D), row(D), row(AQ_W), row(R_W), row(R_W), _full((1, R_W)), _ANY),
        out_shape=(jax.ShapeDtypeStruct((seq, PG_W), _MXU), jax.ShapeDtypeStruct((seq, D), _MXU),
                   jax.ShapeDtypeStruct((seq, D), _MXU), jax.ShapeDtypeStruct((seq, AQ_W), F32),
                   jax.ShapeDtypeStruct((seq, R_W), F32), jax.ShapeDtypeStruct((seq, R_W), _MXU),
                   jax.ShapeDtypeStruct((1, R_W), F32), jax.ShapeDtypeStruct(gpack.shape, gpack.dtype)),
        scratch_shapes=list(_COMM_SCRATCH),
    )(dx1, o, yf, yb, pr, pg, pg, gain_r, wao, wro, woutt, waot, wrot, gpack)


def _qk_prep_bwd(pa, dqh, dk2, dv2, rdf, rdb, drg, gq, gk, seg, ca, sa, cr, sr):
    seq = pa.shape[0]
    tm = _tile(seq, 256)

    def body(pa_ref, dqh_ref, dk2_ref, dv2_ref, rdqf_ref, rdqb_ref, rdkf_ref, rdkb_ref, rdvf_ref, rdvb_ref,
             drg_ref, gq_ref, gk_ref, seg_ref, ca_ref, sa_ref, cr_ref, sr_ref,
             dpa_ref, dpr_ref, dgq_ref, dgk_ref):
        @pl.when(pl.program_id(0) == 0)
        def _():
            dgq_ref[...] = jnp.zeros_like(dgq_ref)
            dgk_ref[...] = jnp.zeros_like(dgk_ref)

        ca_, sa_ = ca_ref[...], sa_ref[...]

        def norm_bwd(raw, gain, dy, segm, dg_ref):
            msq = _seg_mean(raw * raw, segm)
            r = lax.rsqrt(msq + EPS)
            n = raw * r
            dg_ref[...] += jnp.sum(dy * n, axis=0, keepdims=True)
            dn = dy * gain
            return r * (dn - n * _seg_mean(dn * n, segm))

        dqn = _rope(dqh_ref[...] * (A_HD ** -0.5), _cat(ca_, 4), -_cat(sa_, 4), A_HD // 2)
        dpa_ref[:, 0:AQ_W] = _bf(norm_bwd(pa_ref[:, 0:AQ_W], gq_ref[...], dqn, seg_ref[...], dgq_ref))
        dkn = _rope(dk2_ref[...], ca_, -sa_, A_HD // 2)
        dpa_ref[:, AQ_W:AQ_W + AKV_W] = _bf(norm_bwd(pa_ref[:, AQ_W:AQ_W + AKV_W], gk_ref[...], dkn,
                                                     seg_ref[0:AKV_W, 0:AKV_W], dgk_ref))
        dpa_ref[:, AQ_W + AKV_W:PA_W] = _bf(dv2_ref[...])
        cr_, sr_ = _cat(cr_ref[...], 4), -_cat(sr_ref[...], 4)
        dpr_ref[:, 0:R_W] = _bf(_rope((rdqf_ref[...] + rdqb_ref[...]) * (R_HD ** -0.5), cr_, sr_, R_HD // 2))
        dpr_ref[:, R_W:2 * R_W] = _bf(_rope(rdkf_ref[...] + rdkb_ref[...], cr_, sr_, R_HD // 2))
        dpr_ref[:, 2 * R_W:3 * R_W] = _bf(rdvf_ref[...] + rdvb_ref[...])
        dpr_ref[:, 3 * R_W:4 * R_W] = drg_ref[...]

    row = lambda w_: pl.BlockSpec((tm, w_), lambda i: (i, 0))
    return pl.pallas_call(
        body, name="qk_prep_bwd", grid=(seq // tm,),
        in_specs=[row(PA_W), row(AQ_W), row(AKV_W), row(AKV_W), row(R_W), row(R_W), row(R_W), row(R_W),
                  row(R_W), row(R_W), row(R_W), _full((1, AQ_W)), _full((1, AKV_W)), _full((AQ_W, AQ_W)),
                  row(LANES), row(LANES), row(LANES), row(LANES)],
        out_specs=(row(PA_W), row(PR_W), _full((1, AQ_W)), _full((1, AKV_W))),
        out_shape=(jax.ShapeDtypeStruct((seq, PA_W), _MXU), jax.ShapeDtypeStruct((seq, PR_W), _MXU),
                   jax.ShapeDtypeStruct((1, AQ_W), F32), jax.ShapeDtypeStruct((1, AKV_W), F32)),
    )(pa, dqh, dk2, dv2, rdf[0], rdb[0], rdf[1], rdb[1], rdf[2], rdb[2], drg, gq, gk, seg, ca, sa, cr, sr)


def _in_proj_bwd(x, dx1, gain, dpa, dpr, dpg, wint, packs):
    seq = x.shape[0]
    tm = _tile(seq, 256)
    nsteps = seq // tm
    npk = len(packs)

    def body(x_ref, dx1_ref, g_ref, dpa_ref, dpr_ref, dpg_ref, wt_ref, *rest):
        pack_refs, (dx_ref, dg_ref), land_refs = rest[:npk], rest[npk:npk + 2], rest[npk + 2:2 * npk + 2]
        send_sems, recv_sems, local_sems = rest[2 * npk + 2:]
        steps = [_exchange_steps(pack_refs[t], land_refs[t], send_sems, recv_sems, local_sems.at[t], True, base=7 * t)
                 for t in range(npk)]

        @pl.when(pl.program_id(0) == 0)
        def _():
            for start, _ in steps:
                start()
            dg_ref[...] = jnp.zeros_like(dg_ref)

        dh = (_mm(dpa_ref[...], wt_ref[0:PA_W, :]) + _mm(dpr_ref[...], wt_ref[PA_W:PA_W + PR_W, :])
              + _mm(dpg_ref[...], wt_ref[PA_W + PR_W:IN_W, :]))
        n, r = _rms(x_ref[...])
        dg_ref[...] += jnp.sum(dh * n, axis=0, keepdims=True)
        dx_ref[...] = dx1_ref[...] + _rms_bwd(n, r, g_ref[...], dh)

        @pl.when(pl.program_id(0) == nsteps - 1)
        def _():
            for _, finish in steps:
                finish()

    row = lambda w_: pl.BlockSpec((tm, w_), lambda i: (i, 0))
    return pl.pallas_call(
        body, name="in_proj_bwd", grid=(nsteps,),
        in_specs=[row(D), row(D), _full((1, D)), row(PA_W), row(PR_W), row(PG_W), _full((IN_W, D))] + [_ANY] * npk,
        out_specs=(row(D), _full((1, D))) + (_ANY,) * npk,
        out_shape=(jax.ShapeDtypeStruct((seq, D), F32), jax.ShapeDtypeStruct((1, D), F32))
        + tuple(jax.ShapeDtypeStruct(g.shape, g.dtype) for g in packs),
        scratch_shapes=[pltpu.SemaphoreType.DMA((7 * npk,)), pltpu.SemaphoreType.DMA((7 * npk,)),
                        pltpu.SemaphoreType.DMA((npk,))],
    )(x, dx1, gain, dpa, dpr, dpg, wint, *packs)


def _wgrad(a, b, name):
    seq, m = a.shape
    n = b.shape[1]
    tm, tn, ts = _tile(m, 1024), _tile(n, 1024), _tile(seq, 2048)
    ns = seq // ts

    def body(a_ref, b_ref, o_ref):
        @pl.when(pl.program_id(2) == 0)
        def _():
            o_ref[...] = jnp.zeros_like(o_ref)

        o_ref[...] += _mm_tn(a_ref[...], b_ref[...])

    return pl.pallas_call(
        body, name=name, grid=(m // tm, n // tn, ns),
        in_specs=[pl.BlockSpec((ts, tm), lambda i, j, s: (s, i)), pl.BlockSpec((ts, tn), lambda i, j, s: (s, j))],
        out_specs=pl.BlockSpec((tm, tn), lambda i, j, s: (i, j)),
        out_shape=jax.ShapeDtypeStruct((m, n), F32),
    )(a, b)


def _adamw_math(w, g, m, v):
    m = B1 * m + (1.0 - B1) * g
    v = B2 * v + (1.0 - B2) * (g * g)
    m_hat = m / (1.0 - B1 ** STEP)
    v_hat = v / (1.0 - B2 ** STEP)
    delta = -LR * (m_hat / (jnp.sqrt(v_hat) + ADAM_EPS) + WD * w)
    return delta, m, v


def _adamw_big(land, own, w, m, v, name):
    rws, cols = w.shape
    tr = next(t for t in range(min(rws, 288) // 16 * 16, 0, -16) if rws % t == 0)

    def body(l_ref, o_ref, w_ref, m_ref, v_ref, g_ref, d_ref, nm_ref, nv_ref):
        x, y, c = _mesh_pos()
        me = 4 * x + 2 * y + c
        g = o_ref[...]
        for j in range(N_DEV):
            g = g + jnp.where(me == j, 0.0, l_ref[j].astype(F32))
        g_ref[...] = g
        d_ref[...], nm_ref[...], nv_ref[...] = _adamw_math(w_ref[...], g, m_ref[...], v_ref[...])

    row = pl.BlockSpec((tr, cols), lambda i: (i, 0))
    shp = jax.ShapeDtypeStruct((rws, cols), F32)
    return pl.pallas_call(
        body, name=name, grid=(rws // tr,),
        in_specs=[pl.BlockSpec((N_DEV, tr, cols), lambda i: (0, i, 0)), row, row, row, row],
        out_specs=(row, row, row, row), out_shape=(shp, shp, shp, shp),
    )(land, own, w, m, v)


def _adamw_small(sland, w, m, v):
    def body(l_ref, w_ref, m_ref, v_ref, g_ref, d_ref, nm_ref, nv_ref, loss_ref):
        s = l_ref[0]
        for j in range(1, N_DEV):
            s = s + l_ref[j]
        w = w_ref[...]
        gq = s[8:9]
        for h in range(1, A_H):
            gq = gq + s[8 + h:9 + h]
        gk = s[16:17] + s[17:18]
        gdec = s[5:6] * _sigmoid(-w[5:6])
        g = jnp.concatenate([s[0:5], gdec, gq, gk], axis=0)
        g_ref[...] = g
        d_ref[...], nm_ref[...], nv_ref[...] = _adamw_math(w, g, m_ref[...], v_ref[...])
        loss_ref[...] = s[6:7, 0:LANES]

    shp = jax.ShapeDtypeStruct((8, PACK_COLS), F32)
    return pl.pallas_call(
        body, name="adamw_small",
        out_shape=(shp, shp, shp, shp, jax.ShapeDtypeStruct((1, LANES), F32)),
    )(sland, w, m, v)


_BIG = (("w_attn_o", AQ_W, D, 1), ("w_ret_o", R_W, D, 1), ("w_out", D, D, 0),
        ("w_up", D, FF, 1), ("w_down", FF, D, 0), ("w_ple_gate", D, D, 0), ("w_ple", PLE, D, 1))
_LATE = _BIG[:3]
_EARLY = _BIG[3:]
IN_SHARD = IN_W // N_DEV
_SMALL = ("mix_norm", "mlp_norm", "ple_norm", "final_norm", "ret_norm_gain", "ret_decay_logit",
          "attn_q_norm", "attn_k_norm")


def _shard_shape(rows, cols, axis):
    return (rows // N_DEV, cols) if axis == 0 else (rows, cols // N_DEV)


def _pack_shards(shards):
    flat = jnp.concatenate([s.reshape(-1) for s in shards])
    return flat.reshape(-1, PACK_COLS)


def _unpack_gathered(gathered):
    flat = gathered.reshape(N_DEV, -1)
    out, off = {}, 0
    for name, rows, cols, axis in _BIG:
        sr, sc = _shard_shape(rows, cols, axis)
        blk = flat[:, off:off + sr * sc].reshape(N_DEV, sr, sc)
        off += sr * sc
        out[name] = blk.reshape(rows, cols) if axis == 0 else blk.transpose(1, 0, 2).reshape(rows, cols)
    return out


def _pack_full_grads(grads, group):
    parts = []
    for name, rows, cols, axis in group:
        sr, sc = _shard_shape(rows, cols, axis)
        g = grads[name]
        blk = g.reshape(N_DEV, sr, sc) if axis == 0 else g.reshape(rows, N_DEV, sc).transpose(1, 0, 2)
        parts.append(blk.reshape(N_DEV, -1))
    flat = jnp.concatenate(parts, axis=1)
    return flat.reshape(N_DEV, -1, PACK_COLS)


def _unpack_shard(packed, group):
    flat = packed.reshape(-1)
    out, off = {}, 0
    for name, rows, cols, axis in group:
        sr, sc = _shard_shape(rows, cols, axis)
        out[name] = flat[off:off + sr * sc].reshape(1, sr, sc)
        off += sr * sc
    return out


def _pack_small(vals):
    rows = [jnp.pad(vals[n].reshape(-1), (0, PACK_COLS - vals[n].size)) for n in _SMALL]
    return jnp.stack(rows)


def _unpack_small(packed, like):
    return {n: packed[i, :like[n].size].reshape(like[n].shape) for i, n in enumerate(_SMALL)}


def _row(v):
    return jnp.pad(v.reshape(-1), (0, PACK_COLS - v.size))


def kernel(x, p, mix_norm, w_in, attn_q_norm, attn_k_norm, ret_decay_logit, ret_norm_gain, w_attn_o, w_ret_o, w_out, mlp_norm, w_up, w_down, ple_norm, w_ple_gate, w_ple, final_norm, loss_target, m_mix_norm, m_w_in, m_attn_q_norm, m_attn_k_norm, m_ret_decay_logit, m_ret_norm_gain, m_w_attn_o, m_w_ret_o, m_w_out, m_mlp_norm, m_w_up, m_w_down, m_ple_norm, m_w_ple_gate, m_w_ple, m_final_norm, v_mix_norm, v_w_in, v_attn_q_norm, v_attn_k_norm, v_ret_decay_logit, v_ret_norm_gain, v_w_attn_o, v_w_ret_o, v_w_out, v_mlp_norm, v_w_up, v_w_down, v_ple_norm, v_w_ple_gate, v_w_ple, v_final_norm):
    args = dict(locals())
    seq = x.shape[1]
    xs = x[0]
    ps = p[0, 0]
    tgt = loss_target[0]

    big_names = [b[0] for b in _BIG]
    wshard = _pack_shards([args[n] for n in big_names])
    win = _all_gather(w_in[0].astype(_MXU)).transpose(1, 0, 2).reshape(D, IN_W)

    g_mix, g_mlp, g_ple = mix_norm, mlp_norm, ple_norm
    g_fin = final_norm.reshape(1, D)
    gq = jnp.tile(attn_q_norm, (1, A_H))
    gk = jnp.tile(attn_k_norm, (1, A_KV))
    seg = _seg_mean_matrix()
    ca, sa, cr, sr = _rope_tables(seq)

    pa, pr, pg, h, rest_g = _in_proj(xs, g_mix, win, wshard.astype(_MXU))
    wfull = _unpack_gathered(rest_g)
    wao, wro, wout = wfull["w_attn_o"], wfull["w_ret_o"], wfull["w_out"]
    wup, wdown, wpg, wple = wfull["w_up"], wfull["w_down"], wfull["w_ple_gate"], wfull["w_ple"]
    qh, kh, vh, rqh, rkh = _qk_prep(pa, pr, gq, gk, seg, ca, sa, cr, sr)

    tq = _tile(seq, 128)
    tk = _tile(seq // 4, 2048)
    qt8 = qh.reshape(seq, A_H, A_HD).transpose(1, 2, 0)
    vta = jnp.stack([jnp.concatenate([_chunk_t(vh[:, g * A_HD:(g + 1) * A_HD], tk),
                                      jnp.ones((seq // tk, 16, tk), _MXU)], axis=1) for g in range(A_KV)])
    ot, lse = _attn_fwd(qt8, kh, vta, tq, tk)
    o = _heads_to_rows(ot)

    zb = jnp.broadcast_to(ret_decay_logit.reshape(2 * R_H, 1, 1), (2 * R_H, 1, LANES))
    tm_, tmw, tqd, tqdw, tkd, tkdw, tg, tgw = _ret_tables(zb)
    cb = _tile(seq // CHUNK, 4)
    yf, yb, pstf, pstb = _ret_fwd(rqh, rkh, pr, tm_, tqd, tkd, tg, cb)

    x1, merged, ret_in = _merge_fwd(xs, o, yf, yb, pr, pg, ret_norm_gain, wao, wro, wout)
    x2 = _mlp_fwd(x1, g_mlp, wup, wdown)

    dx2, de, dz, hp, loss_p, dg_fin, dg_ple = _ple_loss(x2, ps, tgt, g_ple, g_fin, wpg, wpg.T, wple)
    dx1, act, du, hm, dg_mlp = _mlp_bwd(x1, dx2, g_mlp, wup, wdown.T, wup.T)
    me = 4 * lax.axis_index("x") + 2 * lax.axis_index("y") + lax.axis_index("c")
    gpack_e = _pack_full_grads({"w_up": _wgrad(hm, du, "wgrad_up"), "w_down": _wgrad(act, dx2, "wgrad_down"),
                                "w_ple_gate": _wgrad(hp, dz, "wgrad_ple_gate"), "w_ple": _wgrad(ps, de, "wgrad_ple")},
                               _EARLY)
    dpg, dao, dro, do, dry, drg, dg_gn, land_e = _merge_bwd(dx1, o, yf, yb, pr, pg, ret_norm_gain, wao, wro,
                                                            wout.T, wao.T, wro.T, _bf(gpack_e))
    *rd, dlam = _ret_bwd(rqh, rkh, pr, dry, pstf, pstb, tm_, tmw, tqd, tqdw, tkd, tkdw, tg, tgw, cb)

    ksplit = 1
    tkb = _tile(seq // 4, 512)
    dot_ = do.reshape(seq, A_H, A_HD).transpose(1, 2, 0)
    dkt, dvt, dqt = _attn_bwd(qt8, ot, dot_, lse, kh, vh, _chunk_t(kh, tkb), tq, tkb, ksplit)
    dqh = _heads_to_rows(dqt[0])
    dpa, dpr, dg_q, dg_k = _qk_prep_bwd(pa, dqh, _chunks_to_rows(dkt), _chunks_to_rows(dvt), rd[0:3], rd[3:6], drg, gq, gk, seg, ca, sa, cr, sr)
    wg_in = jnp.concatenate([_wgrad(h, dpa, "wgrad_in_a"), _wgrad(h, dpr, "wgrad_in_r"),
                             _wgrad(h, dpg, "wgrad_in_g")], axis=1)
    gpack_l = _pack_full_grads({"w_attn_o": _wgrad(o, dao, "wgrad_attn_o"),
                                "w_ret_o": _wgrad(ret_in, dro, "wgrad_ret_o"),
                                "w_out": _wgrad(merged, dx1, "wgrad_out")}, _LATE)
    gpack_in = wg_in.reshape(D, N_DEV, IN_SHARD).transpose(1, 0, 2)
    grad_x, dg_mix, land_in, land_l = _in_proj_bwd(xs, dx1, g_mix, dpa, dpr, dpg, win.T,
                                                   [_bf(gpack_in), _bf(gpack_l)])
    small = jnp.stack(
        [_row(dg_mix), _row(dg_mlp), _row(dg_ple), _row(dg_fin), _row(dg_gn), _row(dlam[:, 0, 0]),
         _row(loss_p[0, 0:1]), jnp.zeros((PACK_COLS,), F32)]
        + [_row(dg_q[0, hh * A_HD:(hh + 1) * A_HD]) for hh in range(A_H)]
        + [_row(dg_k[0, hh * A_HD:(hh + 1) * A_HD]) for hh in range(A_KV)]
        + [jnp.zeros((PACK_COLS,), F32)] * (SMALL_ROWS - 18))

    own = lambda pack: lax.dynamic_index_in_dim(pack, me, axis=0, keepdims=False)
    (sland,) = _exchange_grads([small[None]])
    in_sh = _adamw_big(land_in, own(gpack_in), w_in[0], m_w_in[0], v_w_in[0], "adamw_w_in")
    group_sh = []
    for group, land, pack, name in ((_EARLY, land_e, gpack_e, "adamw_early"), (_LATE, land_l, gpack_l, "adamw_late")):
        packed = lambda pre: _pack_shards([args[pre + g[0]] for g in group])
        group_sh.append((group, _adamw_big(land, own(pack), packed(""), packed("m_"), packed("v_"), name)))
    g_sm, d_sm, m_sm, v_sm, loss_row = _adamw_small(
        sland, _pack_small({n: args[n] for n in _SMALL}), _pack_small({n: args["m_" + n] for n in _SMALL}),
        _pack_small({n: args["v_" + n] for n in _SMALL}))

    names = ["mix_norm", "w_in", "attn_q_norm", "attn_k_norm", "ret_decay_logit", "ret_norm_gain", "w_attn_o",
             "w_ret_o", "w_out", "mlp_norm", "w_up", "w_down", "ple_norm", "w_ple_gate", "w_ple", "final_norm"]
    like = {n: args[n] for n in _SMALL}
    outs = [loss_row[0, 0], grad_x[None]]
    for kind, sm in enumerate((g_sm, d_sm, m_sm, v_sm)):
        table = {**_unpack_small(sm, like), "w_in": in_sh[kind][None]}
        for group, res in group_sh:
            table.update(_unpack_shard(res[kind], group))
        outs += [table[n] for n in names]
    return tuple(outs)
```

```python
import functools

import jax
import jax.numpy as jnp
from jax import lax
from jax.experimental import pallas as pl
from jax.experimental.pallas import tpu as pltpu

F32 = jnp.float32
_MXU = jnp.bfloat16

D = 1024
PLE = 256
GRID_W = 64
A_HD = 64
A_H = 8
A_KV = 2
A_G = A_H // A_KV
AQ_W = A_H * A_HD
AKV_W = A_KV * A_HD
R_HD = 128
R_H = 4
R_W = R_H * R_HD
IN_W = AQ_W + 2 * AKV_W + 4 * R_W + 2 * D
PA_W = AQ_W + 2 * AKV_W
PR_W = 4 * R_W
PG_W = 2 * D
FF = 4 * D
CHUNK = 128
ROPE_THETA = 10000.0
EPS = 1e-6
GN_EPS = 1e-5
N_DEV = 8

LR, B1, B2, ADAM_EPS, WD, STEP = 0.001, 0.9, 0.999, 1e-08, 0.01, 10

LANES = 128
PACK_COLS = 1024
SMALL_ROWS = 24


def _tile(n, pref):
    t = min(n, pref)
    assert n % t == 0, (n, t)
    return t


def _bf(a):
    return a.astype(_MXU)


def _mm(a, b):
    return jnp.dot(_bf(a), _bf(b), preferred_element_type=F32)


def _mm_nt(a, b):
    return lax.dot_general(_bf(a), _bf(b), (((1,), (1,)), ((), ())), preferred_element_type=F32)


def _mm_tn(a, b):
    return lax.dot_general(_bf(a), _bf(b), (((0,), (0,)), ((), ())), preferred_element_type=F32)


def _seg_mean(v, segm):
    hi = _bf(v)
    lo = _bf(v - hi.astype(F32))
    return _mm(hi, segm) + _mm(lo, segm)


def _sigmoid(z):
    return 1.0 / (1.0 + jnp.exp(-z))


def _rms(x):
    r = lax.rsqrt(jnp.mean(x * x, axis=-1, keepdims=True) + EPS)
    return x * r, r


def _rms_bwd(n, r, gain, dy):
    dn = dy * gain
    return r * (dn - n * jnp.mean(dn * n, axis=-1, keepdims=True))


def _swap_halves(x, half):
    n = x.shape[-1]
    lane = lax.broadcasted_iota(jnp.int32, x.shape, x.ndim - 1)
    first = (lane % (2 * half)) < half
    return jnp.where(first, pltpu.roll(x, n - half, axis=1), pltpu.roll(x, half, axis=1))


def _rope(x, cos, sin, half):
    return x * cos + _swap_halves(x, half) * sin


def _cat(t, reps):
    return jnp.concatenate([t] * reps, axis=1)


def _full(shape):
    nd = len(shape)
    return pl.BlockSpec(shape, lambda *_: (0,) * nd)


def _rope_tables(seq):
    def tab(head_dim):
        n_axis = head_dim // 4
        freqs = ROPE_THETA ** (-jnp.arange(n_axis, dtype=F32) / n_axis)
        rows = seq // GRID_W
        row = jnp.repeat(jnp.arange(rows, dtype=F32), GRID_W)
        col = jnp.tile(jnp.arange(GRID_W, dtype=F32), rows)
        ang = jnp.concatenate([row[:, None] * freqs, col[:, None] * freqs], axis=-1)
        c, s = jnp.cos(ang), jnp.sin(ang)
        return jnp.concatenate([c, c], axis=-1), jnp.concatenate([-s, s], axis=-1)
    ca, sa = tab(A_HD)
    cr, sr = tab(R_HD)
    return jnp.tile(ca, (1, 2)), jnp.tile(sa, (1, 2)), cr, sr


def _seg_mean_matrix():
    i = jnp.arange(AQ_W) // A_HD
    return (i[:, None] == i[None, :]).astype(F32) / A_HD


def _mesh_pos():
    return lax.axis_index("x"), lax.axis_index("y"), lax.axis_index("c")


def _gather_steps(x_ref, out_ref, send_sems, recv_sems, local_sem, base=0):
    x, y, c = _mesh_pos()
    me, sibling = (x, y, c), (x, y, 1 - c)
    chips = [(1 - x, y), (x, 1 - y), (1 - x, 1 - y)]

    def slot(px, py, pc):
        return out_ref.at[4 * px + 2 * py + pc]

    def copy(k, block, to, src=None):
        return pltpu.make_async_remote_copy(
            src_ref=slot(*block) if src is None else src, dst_ref=slot(*block),
            send_sem=send_sems.at[base + k], recv_sem=recv_sems.at[base + k],
            device_id=to, device_id_type=pl.DeviceIdType.MESH)

    mine = pltpu.make_async_copy(x_ref, slot(*me), local_sem)
    first = [copy(0, me, sibling, src=x_ref)]
    first += [copy(1 + j, me, (*chip, c), src=x_ref) for j, chip in enumerate(chips)]
    passed = [copy(4 + j, (*chip, c), sibling) for j, chip in enumerate(chips)]

    def start():
        mine.start()
        for cp in first:
            cp.start()

    def finish():
        for j, chip in enumerate(chips):
            copy(1 + j, (*chip, c), me).wait_recv()
            passed[j].start()
        copy(0, sibling, me).wait_recv()
        for j, chip in enumerate(chips):
            copy(4 + j, (*chip, 1 - c), me).wait_recv()
        for cp in first + passed:
            cp.wait_send()
        mine.wait()

    return start, finish


def _exchange_steps(g_ref, land_ref, send_sems, recv_sems, local_sem, per_device, base=0):
    x, y, c = _mesh_pos()
    me = 4 * x + 2 * y + c

    def row(j):
        return g_ref.at[j if per_device else 0]

    def peer(k):
        p = (x ^ ((k >> 2) & 1), y ^ ((k >> 1) & 1), c ^ (k & 1))
        return p, 4 * p[0] + 2 * p[1] + p[2]

    def copy(k, src, dst):
        return pltpu.make_async_remote_copy(
            src_ref=src, dst_ref=dst, send_sem=send_sems.at[base + k - 1], recv_sem=recv_sems.at[base + k - 1],
            device_id=peer(k)[0], device_id_type=pl.DeviceIdType.MESH)

    own = pltpu.make_async_copy(row(me), land_ref.at[me], local_sem)
    sends = [copy(k, row(peer(k)[1]), land_ref.at[me]) for k in range(1, N_DEV)]

    def start():
        own.start()
        for cp in sends:
            cp.start()

    def finish():
        for k in range(1, N_DEV):
            copy(k, row(me), land_ref.at[peer(k)[1]]).wait_recv()
        for cp in sends:
            cp.wait_send()
        own.wait()

    return start, finish


_COMM_SCRATCH = [pltpu.SemaphoreType.DMA((7,)), pltpu.SemaphoreType.DMA((7,)), pltpu.SemaphoreType.DMA]
_ANY = pl.BlockSpec(memory_space=pl.ANY)


def _all_gather(shard):
    def body(x_ref, out_ref, send_sems, recv_sems, local_sem):
        start, finish = _gather_steps(x_ref, out_ref, send_sems, recv_sems, local_sem)
        start()
        finish()

    return pl.pallas_call(
        body, name="all_gather_weights", out_shape=jax.ShapeDtypeStruct((N_DEV,) + shard.shape, shard.dtype),
        in_specs=[_ANY], out_specs=_ANY, scratch_shapes=list(_COMM_SCRATCH),
    )(shard)


def _exchange_grads(packs):
    n = len(packs)

    def body(*refs):
        g_refs, land_refs = refs[:n], refs[n:2 * n]
        send_sems, recv_sems, local_sems = refs[2 * n:]
        steps = [_exchange_steps(g_refs[t], land_refs[t], send_sems, recv_sems, local_sems.at[t],
                                 packs[t].shape[0] == N_DEV, base=7 * t) for t in range(n)]
        for start, _ in steps:
            start()
        for _, finish in steps:
            finish()

    return pl.pallas_call(
        body, name="exchange_grads",
        out_shape=tuple(jax.ShapeDtypeStruct((N_DEV,) + g.shape[1:], g.dtype) for g in packs),
        in_specs=[_ANY] * n, out_specs=(_ANY,) * n,
        scratch_shapes=[pltpu.SemaphoreType.DMA((7 * n,)), pltpu.SemaphoreType.DMA((7 * n,)),
                        pltpu.SemaphoreType.DMA((n,))],
    )(*packs)


def _in_proj(x, gain, w, rest):
    seq = x.shape[0]
    tm = _tile(seq, 256)
    nsteps = seq // tm

    def body(x_ref, g_ref, w_ref, rest_ref, pa_ref, pr_ref, pg_ref, h_ref, gath_ref, send_sems, recv_sems, local_sem):
        start, finish = _gather_steps(rest_ref, gath_ref, send_sems, recv_sems, local_sem)
        pl.when(pl.program_id(0) == 0)(start)
        n, _ = _rms(x_ref[...])
        h = _bf(n * g_ref[...])
        h_ref[...] = h
        pa_ref[...] = _mm(h, w_ref[:, 0:PA_W])
        pr_ref[...] = _mm(h, w_ref[:, PA_W:PA_W + PR_W])
        pg_ref[...] = _mm(h, w_ref[:, PA_W + PR_W:IN_W])
        pl.when(pl.program_id(0) == nsteps - 1)(finish)

    row = lambda w_: pl.BlockSpec((tm, w_), lambda i: (i, 0))
    return pl.pallas_call(
        body, name="in_proj", grid=(nsteps,),
        in_specs=[row(D), _full((1, D)), _full((D, IN_W)), _ANY],
        out_specs=(row(PA_W), row(PR_W), row(PG_W), row(D), _ANY),
        out_shape=(jax.ShapeDtypeStruct((seq, PA_W), F32), jax.ShapeDtypeStruct((seq, PR_W), F32),
                   jax.ShapeDtypeStruct((seq, PG_W), F32), jax.ShapeDtypeStruct((seq, D), _MXU),
                   jax.ShapeDtypeStruct((N_DEV,) + rest.shape, rest.dtype)),
        scratch_shapes=list(_COMM_SCRATCH),
    )(x, gain, w, rest)


def _qk_prep(pa, pr, gq, gk, seg, ca, sa, cr, sr):
    seq = pa.shape[0]
    tm = _tile(seq, 256)

    def body(pa_ref, pr_ref, gq_ref, gk_ref, seg_ref, ca_ref, sa_ref, cr_ref, sr_ref,
             qh_ref, kh_ref, v_ref, rq_ref, rk_ref):
        q = pa_ref[:, 0:AQ_W]
        k = pa_ref[:, AQ_W:AQ_W + AKV_W]
        v_ref[...] = _bf(pa_ref[:, AQ_W + AKV_W:PA_W])
        ca_, sa_ = ca_ref[...], sa_ref[...]
        msq = _seg_mean(q * q, seg_ref[...])
        qn = q * lax.rsqrt(msq + EPS) * gq_ref[...]
        qh_ref[...] = _bf(_rope(qn, _cat(ca_, 4), _cat(sa_, 4), A_HD // 2) * (A_HD ** -0.5))
        msk = _seg_mean(k * k, seg_ref[0:AKV_W, 0:AKV_W])
        kn = k * lax.rsqrt(msk + EPS) * gk_ref[...]
        kh_ref[...] = _bf(_rope(kn, ca_, sa_, A_HD // 2))
        cr_, sr_ = _cat(cr_ref[...], 4), _cat(sr_ref[...], 4)
        rq_ref[...] = _rope(pr_ref[:, 0:R_W], cr_, sr_, R_HD // 2) * (R_HD ** -0.5)
        rk_ref[...] = _rope(pr_ref[:, R_W:2 * R_W], cr_, sr_, R_HD // 2)

    row = lambda w_: pl.BlockSpec((tm, w_), lambda i: (i, 0))
    return pl.pallas_call(
        body, name="qk_prep", grid=(seq // tm,),
        in_specs=[row(PA_W), row(2 * R_W), _full((1, AQ_W)), _full((1, AKV_W)), _full((AQ_W, AQ_W)),
                  row(LANES), row(LANES), row(LANES), row(LANES)],
        out_specs=(row(AQ_W), row(AKV_W), row(AKV_W), row(R_W), row(R_W)),
        out_shape=(jax.ShapeDtypeStruct((seq, AQ_W), _MXU), jax.ShapeDtypeStruct((seq, AKV_W), _MXU),
                   jax.ShapeDtypeStruct((seq, AKV_W), _MXU), jax.ShapeDtypeStruct((seq, R_W), F32),
                   jax.ShapeDtypeStruct((seq, R_W), F32)),
    )(pa, pr, gq, gk, seg, ca, sa, cr, sr)


def _chunk_t(a, tk):
    seq = a.shape[0]
    return a.reshape(seq // tk, tk, a.shape[1]).transpose(0, 2, 1)


def _heads_to_rows(t):
    return t.transpose(2, 0, 1).reshape(t.shape[2], AQ_W)


def _attn_fwd(qt8, k2, vta, tq, tk):
    seq = k2.shape[0]
    nck = seq // tk
    rows = A_G * tq
    vrows = vta.shape[2]
    rb = _tile(tk, 128)
    assert nck % 2 == 0, nck

    def body(qt_ref, k_ref, vt_ref, o_ref, lse_ref, m_sc, acc_sc, qtp_sc, s_a, s_b, p_a, p_b, al_a, al_b):
        g = pl.program_id(0)
        qtp_sc[...] = jnp.zeros_like(qtp_sc)
        qtp_sc[pl.ds(pl.multiple_of(g * A_HD, A_HD), A_HD), :] = jnp.concatenate(
            [qt_ref[a] for a in range(A_G)], axis=1)
        m_sc[...] = jnp.full((1, rows), -jnp.inf, F32)
        acc_sc[...] = jnp.zeros_like(acc_sc)

        def scores(c):
            kc = k_ref[pl.ds(pl.multiple_of(c * tk, tk), tk), :]
            return _mm(kc, qtp_sc[...])

        def stage(c, s_cur, s_nxt, p_cur, p_prv, al_cur, al_prv, first=False, last=False):
            if not last:
                s_nxt[...] = scores(c + 1)
            if not first:
                acc_sc[...] = al_prv[...] * acc_sc[...] + _mm(vt_ref[0, c - 1], p_prv[...])
            m_old = m_sc[...]
            mx = None
            for r in range(0, tk, rb):
                bm = jnp.max(s_cur[r:r + rb, :].reshape(rb // 8, 8, rows), axis=0)
                mx = bm if mx is None else jnp.maximum(mx, bm)
            m_new = jnp.maximum(m_old, jnp.max(mx, axis=0, keepdims=True))
            for r in range(0, tk, rb):
                p_cur[r:r + rb, :] = _bf(jnp.exp(s_cur[r:r + rb, :] - m_new))
            al_cur[...] = jnp.exp(m_old - m_new)
            m_sc[...] = m_new

        s_a[...] = scores(0)
        stage(0, s_a, s_b, p_a, p_b, al_a, al_b, first=True)

        def pair(j, carry):
            stage(2 * j + 1, s_b, s_a, p_b, p_a, al_b, al_a)
            stage(2 * j + 2, s_a, s_b, p_a, p_b, al_a, al_b)
            return carry

        lax.fori_loop(0, nck // 2 - 1, pair, 0)
        stage(nck - 1, s_b, s_a, p_b, p_a, al_b, al_a, last=True)
        acc = al_b[...] * acc_sc[...] + _mm(vt_ref[0, nck - 1], p_b[...])
        l = acc[A_HD:A_HD + 1, :]
        lse = m_sc[...] + jnp.log(l)
        out = acc[0:A_HD, :] * (1.0 / l)
        for a in range(A_G):
            o_ref[a] = out[:, a * tq:(a + 1) * tq]
            lse_ref[a] = lse[:, a * tq:(a + 1) * tq]

    return pl.pallas_call(
        body, name="attn_fwd", grid=(A_KV, seq // tq),
        in_specs=[pl.BlockSpec((A_G, A_HD, tq), lambda g, i: (g, 0, i)),
                  _full((seq, LANES)), pl.BlockSpec((1, nck, vrows, tk), lambda g, i: (g, 0, 0, 0))],
        out_specs=(pl.BlockSpec((A_G, A_HD, tq), lambda g, i: (g, 0, i)),
                   pl.BlockSpec((A_G, 1, tq), lambda g, i: (g, 0, i))),
        out_shape=(jax.ShapeDtypeStruct((A_H, A_HD, seq), F32), jax.ShapeDtypeStruct((A_H, 1, seq), F32)),
        scratch_shapes=[pltpu.VMEM((1, rows), F32), pltpu.VMEM((vrows, rows), F32), pltpu.VMEM((LANES, rows), _MXU),
                        pltpu.VMEM((tk, rows), F32), pltpu.VMEM((tk, rows), F32),
                        pltpu.VMEM((tk, rows), _MXU), pltpu.VMEM((tk, rows), _MXU),
                        pltpu.VMEM((1, rows), F32), pltpu.VMEM((1, rows), F32)],
    )(qt8, k2, vta)


def _attn_bwd(qt8, ot, dot_, lse, k2, v2, k2t, tq, tk, ksplit):
    seq = k2.shape[0]
    sh = seq // ksplit
    nck = sh // tk
    rows = A_G * tq
    rb = _tile(tk, 16384 // rows)
    assert nck % 2 == 0, nck

    def body(qt_ref, ot_ref, dot_ref, lse_ref, k_ref, v_ref, kt_ref,
             dk_ref, dv_ref, dq_ref, dq_sc, qtp_sc, dotp_sc, pt_sc, dst_sc,
             s_a, s_b, dp_a, dp_b, p_a, p_b, ds_a, ds_b):
        g = pl.program_id(1)
        hrows = pl.ds(pl.multiple_of(g * A_HD, A_HD), A_HD)

        @pl.when(pl.program_id(2) == 0)
        def _():
            dk_ref[...] = jnp.zeros_like(dk_ref)
            dv_ref[...] = jnp.zeros_like(dv_ref)

        lse_row = jnp.concatenate([lse_ref[a] for a in range(A_G)], axis=1)
        dd = jnp.concatenate([jnp.sum(ot_ref[a] * dot_ref[a], axis=0, keepdims=True)
                              for a in range(A_G)], axis=1)
        qtp_sc[...] = jnp.zeros_like(qtp_sc)
        dotp_sc[...] = jnp.zeros_like(dotp_sc)
        qtp_sc[hrows, :] = jnp.concatenate([qt_ref[a] for a in range(A_G)], axis=1)
        dotp_sc[hrows, :] = _bf(jnp.concatenate([dot_ref[a] for a in range(A_G)], axis=1))
        dq_sc[...] = jnp.zeros_like(dq_sc)

        def products(c, s_ref, dp_ref):
            sl = pl.ds(pl.multiple_of(c * tk, tk), tk)
            s_ref[...] = _mm(k_ref[sl, :], qtp_sc[...])
            dp_ref[...] = _mm(v_ref[sl, :], dotp_sc[...])

        def accumulate(c, p_ref, ds_ref):
            pt_sc[...] = p_ref[...].T
            dst_sc[...] = ds_ref[...].T
            dq_sc[...] += _mm(kt_ref[c, hrows, :], ds_ref[...])
            dv_ref[0, c] += _mm(dotp_sc[hrows, :], pt_sc[...])
            dk_ref[0, c] += _mm(qtp_sc[hrows, :], dst_sc[...])

        def stage(c, s_cur, dp_cur, s_nxt, dp_nxt, p_cur, ds_cur, p_prv, ds_prv, first=False, last=False):
            if not last:
                products(c + 1, s_nxt, dp_nxt)
            if not first:
                accumulate(c - 1, p_prv, ds_prv)
            for r in range(0, tk, rb):
                p = jnp.exp(s_cur[r:r + rb, :] - lse_row)
                p_cur[r:r + rb, :] = _bf(p)
                ds_cur[r:r + rb, :] = _bf(p * (dp_cur[r:r + rb, :] - dd))

        products(0, s_a, dp_a)
        stage(0, s_a, dp_a, s_b, dp_b, p_a, ds_a, p_b, ds_b, first=True)

        def pair(j, carry):
            stage(2 * j + 1, s_b, dp_b, s_a, dp_a, p_b, ds_b, p_a, ds_a)
            stage(2 * j + 2, s_a, dp_a, s_b, dp_b, p_a, ds_a, p_b, ds_b)
            return carry

        lax.fori_loop(0, nck // 2 - 1, pair, 0)
        stage(nck - 1, s_b, dp_b, s_a, dp_a, p_b, ds_b, p_a, ds_a, last=True)
        accumulate(nck - 1, p_b, ds_b)
        for a in range(A_G):
            dq_ref[0, a] = dq_sc[:, a * tq:(a + 1) * tq]

    tspec = pl.BlockSpec((A_G, A_HD, tq), lambda s, g, i: (g, 0, i))
    kspec = pl.BlockSpec((sh, LANES), lambda s, g, i: (s, 0))
    gspec = pl.BlockSpec((1, nck, A_HD, tk), lambda s, g, i: (g, s, 0, 0))
    gshape = jax.ShapeDtypeStruct((A_KV, seq // tk, A_HD, tk), F32)
    big = lambda dt: pltpu.VMEM((tk, rows), dt)
    bigt = pltpu.VMEM((rows, tk), _MXU)
    return pl.pallas_call(
        body, name="attn_bwd", grid=(ksplit, A_KV, seq // tq),
        in_specs=[tspec, tspec, tspec, pl.BlockSpec((A_G, 1, tq), lambda s, g, i: (g, 0, i)),
                  kspec, kspec, pl.BlockSpec((nck, LANES, tk), lambda s, g, i: (s, 0, 0))],
        out_specs=(gspec, gspec, pl.BlockSpec((1, A_G, A_HD, tq), lambda s, g, i: (s, g, 0, i))),
        out_shape=(gshape, gshape, jax.ShapeDtypeStruct((ksplit, A_H, A_HD, seq), F32)),
        scratch_shapes=[pltpu.VMEM((A_HD, rows), F32), pltpu.VMEM((LANES, rows), _MXU), pltpu.VMEM((LANES, rows), _MXU),
                        bigt, bigt,
                        big(F32), big(F32), big(F32), big(F32), big(_MXU), big(_MXU), big(_MXU), big(_MXU)],
    )(qt8, ot, dot_, lse, k2, v2, k2t)


def _chunks_to_rows(t):
    return t.transpose(1, 3, 0, 2).reshape(t.shape[1] * t.shape[3], AKV_W)


def _ret_tables(zb):
    c = CHUNK

    def body(z_ref, m_ref, mw_ref, qd_ref, qdw_ref, kd_ref, kdw_ref, g_ref, gw_ref):
        fwd = pl.program_id(0) < R_H
        z = z_ref[0]
        lam = jnp.minimum(z, 0.0) - jnp.log(1.0 + jnp.exp(-jnp.abs(z)))
        i = lax.broadcasted_iota(jnp.int32, (c, c), 0).astype(F32)
        j = lax.broadcasted_iota(jnp.int32, (c, c), 1).astype(F32)
        diff = jnp.where(fwd, i - j, j - i)
        keep = diff >= jnp.where(fwd, 0.0, 1.0)
        dist = jnp.maximum(diff, 0.0)
        m = jnp.where(keep, jnp.exp(lam * dist), 0.0)
        m_ref[0] = m
        mw_ref[0] = m * dist
        fq = jnp.where(fwd, i + 1.0, c - i)
        qd = jnp.exp(lam * fq)
        qd_ref[0] = qd
        qdw_ref[0] = qd * fq
        fk = jnp.where(fwd, c - 1.0 - i, i)
        kd = jnp.exp(lam * fk)
        kd_ref[0] = kd
        kdw_ref[0] = kd * fk
        gdec = jnp.exp(lam * c)
        g_ref[0] = gdec
        gw_ref[0] = gdec * c

    big = pl.BlockSpec((1, c, c), lambda t: (t, 0, 0))
    vec = pl.BlockSpec((1, 1, LANES), lambda t: (t, 0, 0))
    bshape = jax.ShapeDtypeStruct((2 * R_H, c, c), F32)
    vshape = jax.ShapeDtypeStruct((2 * R_H, 1, LANES), F32)
    return pl.pallas_call(
        body, name="ret_tables", grid=(2 * R_H,), in_specs=[vec],
        out_specs=(big, big, big, big, big, big, vec, vec),
        out_shape=(bshape,) * 6 + (vshape, vshape),
    )(zb)


def _ret_fwd(rq, rk, pr, m, qd, kd, gdec, cb):
    seq = rq.shape[0]
    c = CHUNK
    ns = seq // (cb * c)

    def body(qf_ref, kf_ref, vf_ref, qb_ref, kb_ref, vb_ref, m_ref, qd_ref, kd_ref, g_ref,
             yf_ref, yb_ref, pstf_ref, pstb_ref, p_sc):
        @pl.when(pl.program_id(0) == 0)
        def _():
            p_sc[...] = jnp.zeros_like(p_sc)

        dirs = ((qf_ref, kf_ref, vf_ref, yf_ref, pstf_ref), (qb_ref, kb_ref, vb_ref, yb_ref, pstb_ref))
        heads = [slice(h * R_HD, (h + 1) * R_HD) for h in range(R_H)]

        def chunk(j, carry):
            early = []
            for d, (q_ref, k_ref, v_ref, _, _) in enumerate(dirs):
                cc = j if d == 0 else cb - 1 - j
                sl = pl.ds(pl.multiple_of(cc * c, c), c)
                for h, hs in enumerate(heads):
                    t = d * R_H + h
                    early.append((_mm_nt(q_ref[sl, hs], k_ref[sl, hs]), _mm(q_ref[sl, hs] * qd_ref[t], p_sc[t]),
                                  _mm_tn(k_ref[sl, hs] * kd_ref[t], v_ref[sl, hs])))
            for d, (_, _, v_ref, y_ref, pst_ref) in enumerate(dirs):
                cc = j if d == 0 else cb - 1 - j
                sl = pl.ds(pl.multiple_of(cc * c, c), c)
                for h, hs in enumerate(heads):
                    t = d * R_H + h
                    qk, qp, kv = early[t]
                    p = p_sc[t]
                    pst_ref[h, cc] = p
                    y_ref[sl, hs] = _mm(qk * m_ref[t], v_ref[sl, hs]) + qp
                    p_sc[t] = p * g_ref[t] + kv
            return carry

        lax.fori_loop(0, cb, chunk, 0)

    asc = lambda off: pl.BlockSpec((cb * c, R_W), lambda n: (n, off))
    desc = lambda off: pl.BlockSpec((cb * c, R_W), lambda n: (ns - 1 - n, off))
    return pl.pallas_call(
        body, name="ret_fwd", grid=(ns,),
        in_specs=[asc(0), asc(0), asc(2), desc(0), desc(0), desc(2),
                  _full((2 * R_H, c, c)), _full((2 * R_H, c, c)), _full((2 * R_H, c, c)), _full((2 * R_H, 1, LANES))],
        out_specs=(asc(0), desc(0),
                   pl.BlockSpec((R_H, cb, R_HD, R_HD), lambda n: (0, n, 0, 0)),
                   pl.BlockSpec((R_H, cb, R_HD, R_HD), lambda n: (0, ns - 1 - n, 0, 0))),
        out_shape=(jax.ShapeDtypeStruct((seq, R_W), F32), jax.ShapeDtypeStruct((seq, R_W), F32),
                   jax.ShapeDtypeStruct((R_H, seq // c, R_HD, R_HD), F32),
                   jax.ShapeDtypeStruct((R_H, seq // c, R_HD, R_HD), F32)),
        scratch_shapes=[pltpu.VMEM((2 * R_H, R_HD, R_HD), F32)],
    )(rq, rk, pr, rq, rk, pr, m, qd, kd, gdec)


def _ret_bwd(rq, rk, pr, dry, pstf, pstb, m, mw, qd, qdw, kd, kdw, gdec, gw, cb):
    seq = rq.shape[0]
    c = CHUNK
    ns = seq // (cb * c)

    def body(qf_ref, kf_ref, vf_ref, dyf_ref, pstf_ref, qb_ref, kb_ref, vb_ref, dyb_ref, pstb_ref,
             m_ref, mw_ref, qd_ref, qdw_ref, kd_ref, kdw_ref, g_ref, gw_ref,
             dqf_ref, dkf_ref, dvf_ref, dqb_ref, dkb_ref, dvb_ref, dlam_ref, r_sc, acc_sc, e_sc, g_sc):
        n = pl.program_id(0)

        @pl.when(n == 0)
        def _():
            r_sc[...] = jnp.zeros_like(r_sc)
            acc_sc[...] = jnp.zeros_like(acc_sc)
            e_sc[...] = jnp.zeros_like(e_sc)
            g_sc[...] = jnp.zeros_like(g_sc)

        dirs = ((qf_ref, kf_ref, vf_ref, dyf_ref, pstf_ref, dqf_ref, dkf_ref, dvf_ref),
                (qb_ref, kb_ref, vb_ref, dyb_ref, pstb_ref, dqb_ref, dkb_ref, dvb_ref))
        heads = [slice(h * R_HD, (h + 1) * R_HD) for h in range(R_H)]

        def chunk(j, carry):
            first = []
            for d, (q_ref, k_ref, v_ref, dy_ref, pst_ref, _, _, _) in enumerate(dirs):
                cc = cb - 1 - j if d == 0 else j
                sl = pl.ds(pl.multiple_of(cc * c, c), c)
                for h, hs in enumerate(heads):
                    t = d * R_H + h
                    q, k, v, dy = q_ref[sl, hs], k_ref[sl, hs], v_ref[sl, hs], dy_ref[sl, hs]
                    r = r_sc[t]
                    first.append((_mm_nt(q, k), _mm_nt(dy, v), _mm_nt(dy, pst_ref[h, cc]), _mm_nt(v, r),
                                  _mm(k * kd_ref[t], r), _mm_tn(q * qd_ref[t], dy)))
            for d, (q_ref, k_ref, _, dy_ref, pst_ref, dq_ref, dk_ref, dv_ref) in enumerate(dirs):
                cc = cb - 1 - j if d == 0 else j
                sl = pl.ds(pl.multiple_of(cc * c, c), c)
                for h, hs in enumerate(heads):
                    t = d * R_H + h
                    qk, ds, dyp, vr, kr, qdy = first[t]
                    q, k, dy = q_ref[sl, hs], k_ref[sl, hs], dy_ref[sl, hs]
                    r = r_sc[t]
                    da = ds * m_ref[t]
                    dv_ref[sl, hs] = _mm_tn(qk * m_ref[t], dy) + kr
                    dq_ref[sl, hs] = _mm(da, k) + dyp * qd_ref[t]
                    dk_ref[sl, hs] = _mm_tn(da, q) + vr * kd_ref[t]
                    acc_sc[t] += dyp * q * qdw_ref[t] + vr * k * kdw_ref[t]
                    e_sc[t] += ds * qk * mw_ref[t]
                    g_sc[t] += r * pst_ref[h, cc]
                    r_sc[t] = r * g_ref[t] + qdy
            return carry

        lax.fori_loop(0, cb, chunk, 0)

        @pl.when(n == ns - 1)
        def _():
            for t in range(2 * R_H):
                tot = jnp.sum(jnp.sum(acc_sc[t] + e_sc[t] + g_sc[t] * gw_ref[t], axis=0, keepdims=True),
                              axis=1, keepdims=True)
                dlam_ref[t] = jnp.broadcast_to(tot, (1, LANES))

    asc = lambda off: pl.BlockSpec((cb * c, R_W), lambda n: (n, off))
    desc = lambda off: pl.BlockSpec((cb * c, R_W), lambda n: (ns - 1 - n, off))
    big = _full((2 * R_H, c, c))
    vec = _full((2 * R_H, 1, LANES))
    oshape = jax.ShapeDtypeStruct((seq, R_W), F32)
    sq = pltpu.VMEM((2 * R_H, R_HD, R_HD), F32)
    return pl.pallas_call(
        body, name="ret_bwd", grid=(ns,),
        in_specs=[desc(0), desc(0), desc(2), desc(0),
                  pl.BlockSpec((R_H, cb, R_HD, R_HD), lambda n: (0, ns - 1 - n, 0, 0)),
                  asc(0), asc(0), asc(2), asc(0),
                  pl.BlockSpec((R_H, cb, R_HD, R_HD), lambda n: (0, n, 0, 0)),
                  big, big, big, big, big, big, vec, vec],
        out_specs=(desc(0), desc(0), desc(0), asc(0), asc(0), asc(0), vec),
        out_shape=(oshape,) * 6 + (jax.ShapeDtypeStruct((2 * R_H, 1, LANES), F32),),
        scratch_shapes=[sq, sq, sq, sq],
    )(rq, rk, pr, dry, pstf, rq, rk, pr, dry, pstb, m, mw, qd, qdw, kd, kdw, gdec, gw)


def _group_norm(ry):
    yn, rs = [], []
    for h in range(R_H):
        s = ry[:, h * R_HD:(h + 1) * R_HD]
        mu = jnp.mean(s, axis=-1, keepdims=True)
        cen = s - mu
        r = lax.rsqrt(jnp.mean(cen * cen, axis=-1, keepdims=True) + GN_EPS)
        yn.append(cen * r)
        rs.append(r)
    return yn, rs


def _merge_fwd(x, o, yf, yb, pr, pg, gain_r, wao, wro, wout):
    seq = x.shape[0]
    tm = _tile(seq, 256)

    def body(x_ref, o_ref, yf_ref, yb_ref, rg_ref, ga_ref, gr_ref, gn_ref, wao_ref, wro_ref, wout_ref,
             x1_ref, mg_ref, ri_ref):
        yn, _ = _group_norm(yf_ref[...] + yb_ref[...])
        rg = rg_ref[...]
        ret_in = jnp.concatenate(yn, axis=1) * gn_ref[...] * (rg * _sigmoid(rg))
        ri_ref[...] = _bf(ret_in)
        attn_out = _mm(o_ref[...], wao_ref[...])
        ret_out = _mm(ret_in, wro_ref[...])
        merged = _sigmoid(ga_ref[...]) * attn_out + _sigmoid(gr_ref[...]) * ret_out
        mg_ref[...] = _bf(merged)
        x1_ref[...] = x_ref[...] + _mm(merged, wout_ref[...])

    row = lambda w_, j=0: pl.BlockSpec((tm, w_), lambda i: (i, j))
    return pl.pallas_call(
        body, name="merge_fwd", grid=(seq // tm,),
        in_specs=[row(D), row(AQ_W), row(R_W), row(R_W), row(R_W, 3), row(D, 0), row(D, 1),
                  _full((1, R_W)), _full((AQ_W, D)), _full((R_W, D)), _full((D, D))],
        out_specs=(row(D), row(D), row(R_W)),
        out_shape=(jax.ShapeDtypeStruct((seq, D), F32), jax.ShapeDtypeStruct((seq, D), _MXU),
                   jax.ShapeDtypeStruct((seq, R_W), _MXU)),
    )(x, o, yf, yb, pr, pg, pg, gain_r, wao, wro, wout)


def _mlp_fwd(x1, gain, wup, wdown):
    seq = x1.shape[0]
    tm = _tile(seq, 1024)
    fc = 2048
    nfc = FF // fc

    def body(x_ref, g_ref, wu_ref, wd_ref, x2_ref, hm_sc, acc_sc):
        c = pl.program_id(1)

        @pl.when(c == 0)
        def _():
            n, _ = _rms(x_ref[...])
            hm_sc[...] = _bf(n * g_ref[...])
            acc_sc[...] = jnp.zeros_like(acc_sc)

        halves = (slice(0, fc // 2), slice(fc // 2, fc))
        ups = [jnp.maximum(_mm(hm_sc[...], wu_ref[:, hs]), 0.0) for hs in halves]
        acc_sc[...] += _mm(ups[0] * ups[0], wd_ref[halves[0], :]) + _mm(ups[1] * ups[1], wd_ref[halves[1], :])

        @pl.when(c == nfc - 1)
        def _():
            x2_ref[...] = x_ref[...] + acc_sc[...]

    return pl.pallas_call(
        body, name="mlp_fwd", grid=(seq // tm, nfc),
        in_specs=[pl.BlockSpec((tm, D), lambda i, c: (i, 0)), pl.BlockSpec((1, D), lambda i, c: (0, 0)),
                  pl.BlockSpec((D, fc), lambda i, c: (0, c)), pl.BlockSpec((fc, D), lambda i, c: (c, 0))],
        out_specs=pl.BlockSpec((tm, D), lambda i, c: (i, 0)),
        out_shape=jax.ShapeDtypeStruct((seq, D), F32),
        scratch_shapes=[pltpu.VMEM((tm, D), _MXU), pltpu.VMEM((tm, D), F32)],
    )(x1, gain, wup, wdown)


def _ple_loss(x2, p, tgt, g_ple, g_fin, wpg, wpgt, wple):
    seq = x2.shape[0]
    tm = _tile(seq, 256)

    def body(x2_ref, p_ref, t_ref, gp_ref, gf_ref, wpg_ref, wpgt_ref, wple_ref,
             dx2_ref, de_ref, dz_ref, hp_ref, loss_ref, dgf_ref, dgp_ref):
        @pl.when(pl.program_id(0) == 0)
        def _():
            loss_ref[...] = jnp.zeros_like(loss_ref)
            dgf_ref[...] = jnp.zeros_like(dgf_ref)
            dgp_ref[...] = jnp.zeros_like(dgp_ref)

        x2 = x2_ref[...]
        gp, gf = gp_ref[...], gf_ref[...]
        n2, r2 = _rms(x2)
        hp = _bf(n2 * gp)
        hp_ref[...] = hp
        gate = _sigmoid(_mm(hp, wpg_ref[...]))
        e = _mm(p_ref[...], wple_ref[...])
        x3 = x2 + gate * e
        n3, r3 = _rms(x3)
        diff = n3 * gf - t_ref[...]
        row_loss = jnp.mean(diff * diff, axis=-1, keepdims=True)
        loss_ref[...] += 0.5 * jnp.sum(row_loss, axis=0, keepdims=True)
        dy = diff * (1.0 / D)
        dgf_ref[...] += jnp.sum(dy * n3, axis=0, keepdims=True)
        dx3 = _rms_bwd(n3, r3, gf, dy)
        de_ref[...] = _bf(dx3 * gate)
        dz = dx3 * e * gate * (1.0 - gate)
        dz_ref[...] = _bf(dz)
        dhp = _mm(dz, wpgt_ref[...])
        dgp_ref[...] += jnp.sum(dhp * n2, axis=0, keepdims=True)
        dx2_ref[...] = dx3 + _rms_bwd(n2, r2, gp, dhp)

    row = lambda w_: pl.BlockSpec((tm, w_), lambda i: (i, 0))
    act = lambda dt: jax.ShapeDtypeStruct((seq, D), dt)
    return pl.pallas_call(
        body, name="ple_loss", grid=(seq // tm,),
        in_specs=[row(D), row(PLE), row(D), _full((1, D)), _full((1, D)),
                  _full((D, D)), _full((D, D)), _full((PLE, D))],
        out_specs=(row(D), row(D), row(D), row(D), _full((1, LANES)), _full((1, D)), _full((1, D))),
        out_shape=(act(F32), act(_MXU), act(_MXU), act(_MXU), jax.ShapeDtypeStruct((1, LANES), F32),
                   jax.ShapeDtypeStruct((1, D), F32), jax.ShapeDtypeStruct((1, D), F32)),
    )(x2, p, tgt, g_ple, g_fin, wpg, wpgt, wple)


def _mlp_bwd(x1, dx2, gain, wup, wdownt, wupt):
    seq = x1.shape[0]
    tm = _tile(seq, 512)
    fc = 2048
    nfc = FF // fc

    def body(x_ref, dx2_ref, g_ref, wu_ref, wdt_ref, wut_ref,
             dx1_ref, a_ref, du_ref, hm_ref, dg_ref, dhm_sc):
        i = pl.program_id(0)
        c = pl.program_id(1)

        @pl.when((i == 0) & (c == 0))
        def _():
            dg_ref[...] = jnp.zeros_like(dg_ref)

        @pl.when(c == 0)
        def _():
            n, _ = _rms(x_ref[...])
            hm_ref[...] = _bf(n * g_ref[...])
            dhm_sc[...] = jnp.zeros_like(dhm_sc)

        halves = (slice(0, fc // 2), slice(fc // 2, fc))
        ups = [jnp.maximum(_mm(hm_ref[...], wu_ref[:, hs]), 0.0) for hs in halves]
        das = [_mm(dx2_ref[...], wdt_ref[:, hs]) for hs in halves]
        part = None
        for u, da, hs in zip(ups, das, halves):
            a_ref[:, hs] = _bf(u * u)
            du = _bf(da * (2.0 * u))
            du_ref[:, hs] = du
            t = _mm(du, wut_ref[hs, :])
            part = t if part is None else part + t
        dhm_sc[...] += part

        @pl.when(c == nfc - 1)
        def _():
            n, r = _rms(x_ref[...])
            dhm = dhm_sc[...]
            dg_ref[...] += jnp.sum(dhm * n, axis=0, keepdims=True)
            dx1_ref[...] = dx2_ref[...] + _rms_bwd(n, r, g_ref[...], dhm)

    rowd = pl.BlockSpec((tm, D), lambda i, c: (i, 0))
    rowf = pl.BlockSpec((tm, fc), lambda i, c: (i, c))
    return pl.pallas_call(
        body, name="mlp_bwd", grid=(seq // tm, nfc),
        in_specs=[rowd, rowd, pl.BlockSpec((1, D), lambda i, c: (0, 0)),
                  pl.BlockSpec((D, fc), lambda i, c: (0, c)), pl.BlockSpec((D, fc), lambda i, c: (0, c)),
                  pl.BlockSpec((fc, D), lambda i, c: (c, 0))],
        out_specs=(rowd, rowf, rowf, rowd, pl.BlockSpec((1, D), lambda i, c: (0, 0))),
        out_shape=(jax.ShapeDtypeStruct((seq, D), F32), jax.ShapeDtypeStruct((seq, FF), _MXU),
                   jax.ShapeDtypeStruct((seq, FF), _MXU), jax.ShapeDtypeStruct((seq, D), _MXU),
                   jax.ShapeDtypeStruct((1, D), F32)),
        scratch_shapes=[pltpu.VMEM((tm, D), F32)],
    )(x1, dx2, gain, wup, wdownt, wupt)


def _merge_bwd(dx1, o, yf, yb, pr, pg, gain_r, wao, wro, woutt, waot, wrot, gpack):
    seq = dx1.shape[0]
    tm = _tile(seq, 256)
    nsteps = seq // tm

    def body(dx1_ref, o_ref, yf_ref, yb_ref, rg_ref, ga_ref, gr_ref, gn_ref, wao_ref, wro_ref,
             woutt_ref, waot_ref, wrot_ref, gpack_ref,
             dpg_ref, dao_ref, dro_ref, do_ref, dry_ref, drg_ref, dgn_ref, land_ref,
             send_sems, recv_sems, local_sem):
        start, finish = _exchange_steps(gpack_ref, land_ref, send_sems, recv_sems, local_sem, True)
        pl.when(pl.program_id(0) == 0)(start)

        @pl.when(pl.program_id(0) == 0)
        def _():
            dgn_ref[...] = jnp.zeros_like(dgn_ref)

        yn_l, rs_l = _group_norm(yf_ref[...] + yb_ref[...])
        yn = jnp.concatenate(yn_l, axis=1)
        rg = rg_ref[...]
        gn = gn_ref[...]
        sg = _sigmoid(rg)
        sil = rg * sg
        ret_in = yn * gn * sil
        attn_out = _mm(o_ref[...], wao_ref[...])
        ret_out = _mm(ret_in, wro_ref[...])
        sa = _sigmoid(ga_ref[...])
        sr = _sigmoid(gr_ref[...])
        dm = _mm(dx1_ref[...], woutt_ref[...])
        dpg_ref[:, 0:D] = _bf(dm * attn_out * sa * (1.0 - sa))
        dpg_ref[:, D:2 * D] = _bf(dm * ret_out * sr * (1.0 - sr))
        dao = _bf(dm * sa)
        dro = _bf(dm * sr)
        dao_ref[...] = dao
        dro_ref[...] = dro
        do_ref[...] = _mm(dao, waot_ref[...])
        dri = _mm(dro, wrot_ref[...])
        dgn_ref[...] += jnp.sum(dri * yn * sil, axis=0, keepdims=True)
        drg_ref[...] = _bf(dri * yn * gn * (sg * (1.0 + rg * (1.0 - sg))))
        dyn = dri * gn * sil
        dry = []
        for h in range(R_H):
            dh = dyn[:, h * R_HD:(h + 1) * R_HD]
            dry.append(rs_l[h] * (dh - jnp.mean(dh, axis=-1, keepdims=True)
                                  - yn_l[h] * jnp.mean(dh * yn_l[h], axis=-1, keepdims=True)))
        dry_ref[...] = jnp.concatenate(dry, axis=1)
        pl.when(pl.program_id(0) == nsteps - 1)(finish)

    row = lambda w_, j=0: pl.BlockSpec((tm, w_), lambda i: (i, j))
    return pl.pallas_call(
        body, name="merge_bwd", grid=(nsteps,),
        in_specs=[row(D), row(AQ_W), row(R_W), row(R_W), row(R_W, 3), row(D, 0), row(D, 1),
                  _full((1, R_W)), _full((AQ_W, D)), _full((R_W, D)), _full((D, D)),
                  _full((D, AQ_W)), _full((D, R_W)), _ANY],
        out_specs=(row(PG_W), row(D), row(D), row(AQ_W), row(R_W), row(R_W), _full((1, R_W)), _ANY),
        out_shape=(jax.ShapeDtypeStruct((seq, PG_W), _MXU), jax.ShapeDtypeStruct((seq, D), _MXU),
                   jax.ShapeDtypeStruct((seq, D), _MXU), jax.ShapeDtypeStruct((seq, AQ_W), F32),
                   jax.ShapeDtypeStruct((seq, R_W), F32), jax.ShapeDtypeStruct((seq, R_W), _MXU),
                   jax.ShapeDtypeStruct((1, R_W), F32), jax.ShapeDtypeStruct(gpack.shape, gpack.dtype)),
        scratch_shapes=list(_COMM_SCRATCH),
    )(dx1, o, yf, yb, pr, pg, pg, gain_r, wao, wro, woutt, waot, wrot, gpack)


def _qk_prep_bwd(pa, dqh, dk2, dv2, rdf, rdb, drg, gq, gk, seg, ca, sa, cr, sr):
    seq = pa.shape[0]
    tm = _tile(seq, 256)

    def body(pa_ref, dqh_ref, dk2_ref, dv2_ref, rdqf_ref, rdqb_ref, rdkf_ref, rdkb_ref, rdvf_ref, rdvb_ref,
             drg_ref, gq_ref, gk_ref, seg_ref, ca_ref, sa_ref, cr_ref, sr_ref,
             dpa_ref, dpr_ref, dgq_ref, dgk_ref):
        @pl.when(pl.program_id(0) == 0)
        def _():
            dgq_ref[...] = jnp.zeros_like(dgq_ref)
            dgk_ref[...] = jnp.zeros_like(dgk_ref)

        ca_, sa_ = ca_ref[...], sa_ref[...]

        def norm_bwd(raw, gain, dy, segm, dg_ref):
            msq = _seg_mean(raw * raw, segm)
            r = lax.rsqrt(msq + EPS)
            n = raw * r
            dg_ref[...] += jnp.sum(dy * n, axis=0, keepdims=True)
            dn = dy * gain
            return r * (dn - n * _seg_mean(dn * n, segm))

        dqn = _rope(dqh_ref[...] * (A_HD ** -0.5), _cat(ca_, 4), -_cat(sa_, 4), A_HD // 2)
        dpa_ref[:, 0:AQ_W] = _bf(norm_bwd(pa_ref[:, 0:AQ_W], gq_ref[...], dqn, seg_ref[...], dgq_ref))
        dkn = _rope(dk2_ref[...], ca_, -sa_, A_HD // 2)
        dpa_ref[:, AQ_W:AQ_W + AKV_W] = _bf(norm_bwd(pa_ref[:, AQ_W:AQ_W + AKV_W], gk_ref[...], dkn,
                                                     seg_ref[0:AKV_W, 0:AKV_W], dgk_ref))
        dpa_ref[:, AQ_W + AKV_W:PA_W] = _bf(dv2_ref[...])
        cr_, sr_ = _cat(cr_ref[...], 4), -_cat(sr_ref[...], 4)
        dpr_ref[:, 0:R_W] = _bf(_rope((rdqf_ref[...] + rdqb_ref[...]) * (R_HD ** -0.5), cr_, sr_, R_HD // 2))
        dpr_ref[:, R_W:2 * R_W] = _bf(_rope(rdkf_ref[...] + rdkb_ref[...], cr_, sr_, R_HD // 2))
        dpr_ref[:, 2 * R_W:3 * R_W] = _bf(rdvf_ref[...] + rdvb_ref[...])
        dpr_ref[:, 3 * R_W:4 * R_W] = drg_ref[...]

    row = lambda w_: pl.BlockSpec((tm, w_), lambda i: (i, 0))
    return pl.pallas_call(
        body, name="qk_prep_bwd", grid=(seq // tm,),
        in_specs=[row(PA_W), row(AQ_W), row(AKV_W), row(AKV_W), row(R_W), row(R_W), row(R_W), row(R_W),
                  row(R_W), row(R_W), row(R_W), _full((1, AQ_W)), _full((1, AKV_W)), _full((AQ_W, AQ_W)),
                  row(LANES), row(LANES), row(LANES), row(LANES)],
        out_specs=(row(PA_W), row(PR_W), _full((1, AQ_W)), _full((1, AKV_W))),
        out_shape=(jax.ShapeDtypeStruct((seq, PA_W), _MXU), jax.ShapeDtypeStruct((seq, PR_W), _MXU),
                   jax.ShapeDtypeStruct((1, AQ_W), F32), jax.ShapeDtypeStruct((1, AKV_W), F32)),
    )(pa, dqh, dk2, dv2, rdf[0], rdb[0], rdf[1], rdb[1], rdf[2], rdb[2], drg, gq, gk, seg, ca, sa, cr, sr)


def _in_proj_bwd(x, dx1, gain, dpa, dpr, dpg, wint, packs):
    seq = x.shape[0]
    tm = _tile(seq, 256)
    nsteps = seq // tm
    npk = len(packs)

    def body(x_ref, dx1_ref, g_ref, dpa_ref, dpr_ref, dpg_ref, wt_ref, *rest):
        pack_refs, (dx_ref, dg_ref), land_refs = rest[:npk], rest[npk:npk + 2], rest[npk + 2:2 * npk + 2]
        send_sems, recv_sems, local_sems = rest[2 * npk + 2:]
        steps = [_exchange_steps(pack_refs[t], land_refs[t], send_sems, recv_sems, local_sems.at[t], True, base=7 * t)
                 for t in range(npk)]

        @pl.when(pl.program_id(0) == 0)
        def _():
            for start, _ in steps:
                start()
            dg_ref[...] = jnp.zeros_like(dg_ref)

        dh = (_mm(dpa_ref[...], wt_ref[0:PA_W, :]) + _mm(dpr_ref[...], wt_ref[PA_W:PA_W + PR_W, :])
              + _mm(dpg_ref[...], wt_ref[PA_W + PR_W:IN_W, :]))
        n, r = _rms(x_ref[...])
        dg_ref[...] += jnp.sum(dh * n, axis=0, keepdims=True)
        dx_ref[...] = dx1_ref[...] + _rms_bwd(n, r, g_ref[...], dh)

        @pl.when(pl.program_id(0) == nsteps - 1)
        def _():
            for _, finish in steps:
                finish()

    row = lambda w_: pl.BlockSpec((tm, w_), lambda i: (i, 0))
    return pl.pallas_call(
        body, name="in_proj_bwd", grid=(nsteps,),
        in_specs=[row(D), row(D), _full((1, D)), row(PA_W), row(PR_W), row(PG_W), _full((IN_W, D))] + [_ANY] * npk,
        out_specs=(row(D), _full((1, D))) + (_ANY,) * npk,
        out_shape=(jax.ShapeDtypeStruct((seq, D), F32), jax.ShapeDtypeStruct((1, D), F32))
        + tuple(jax.ShapeDtypeStruct(g.shape, g.dtype) for g in packs),
        scratch_shapes=[pltpu.SemaphoreType.DMA((7 * npk,)), pltpu.SemaphoreType.DMA((7 * npk,)),
                        pltpu.SemaphoreType.DMA((npk,))],
    )(x, dx1, gain, dpa, dpr, dpg, wint, *packs)


def _wgrad(a, b, name):
    seq, m = a.shape
    n = b.shape[1]
    tm, tn, ts = _tile(m, 1024), _tile(n, 1024), _tile(seq, 2048)
    ns = seq // ts

    def body(a_ref, b_ref, o_ref):
        @pl.when(pl.program_id(2) == 0)
        def _():
            o_ref[...] = jnp.zeros_like(o_ref)

        o_ref[...] += _mm_tn(a_ref[...], b_ref[...])

    return pl.pallas_call(
        body, name=name, grid=(m // tm, n // tn, ns),
        in_specs=[pl.BlockSpec((ts, tm), lambda i, j, s: (s, i)), pl.BlockSpec((ts, tn), lambda i, j, s: (s, j))],
        out_specs=pl.BlockSpec((tm, tn), lambda i, j, s: (i, j)),
        out_shape=jax.ShapeDtypeStruct((m, n), F32),
    )(a, b)


def _adamw_math(w, g, m, v):
    m = B1 * m + (1.0 - B1) * g
    v = B2 * v + (1.0 - B2) * (g * g)
    m_hat = m / (1.0 - B1 ** STEP)
    v_hat = v / (1.0 - B2 ** STEP)
    delta = -LR * (m_hat / (jnp.sqrt(v_hat) + ADAM_EPS) + WD * w)
    return delta, m, v


def _adamw_big(land, own, w, m, v, name):
    rws, cols = w.shape
    tr = next(t for t in range(min(rws, 288) // 16 * 16, 0, -16) if rws % t == 0)

    def body(l_ref, o_ref, w_ref, m_ref, v_ref, g_ref, d_ref, nm_ref, nv_ref):
        x, y, c = _mesh_pos()
        me = 4 * x + 2 * y + c
        g = o_ref[...]
        for j in range(N_DEV):
            g = g + jnp.where(me == j, 0.0, l_ref[j].astype(F32))
        g_ref[...] = g
        d_ref[...], nm_ref[...], nv_ref[...] = _adamw_math(w_ref[...], g, m_ref[...], v_ref[...])

    row = pl.BlockSpec((tr, cols), lambda i: (i, 0))
    shp = jax.ShapeDtypeStruct((rws, cols), F32)
    return pl.pallas_call(
        body, name=name, grid=(rws // tr,),
        in_specs=[pl.BlockSpec((N_DEV, tr, cols), lambda i: (0, i, 0)), row, row, row, row],
        out_specs=(row, row, row, row), out_shape=(shp, shp, shp, shp),
    )(land, own, w, m, v)


def _adamw_small(sland, w, m, v):
    def body(l_ref, w_ref, m_ref, v_ref, g_ref, d_ref, nm_ref, nv_ref, loss_ref):
        s = l_ref[0]
        for j in range(1, N_DEV):
            s = s + l_ref[j]
        w = w_ref[...]
        gq = s[8:9]
        for h in range(1, A_H):
            gq = gq + s[8 + h:9 + h]
        gk = s[16:17] + s[17:18]
        gdec = s[5:6] * _sigmoid(-w[5:6])
        g = jnp.concatenate([s[0:5], gdec, gq, gk], axis=0)
        g_ref[...] = g
        d_ref[...], nm_ref[...], nv_ref[...] = _adamw_math(w, g, m_ref[...], v_ref[...])
        loss_ref[...] = s[6:7, 0:LANES]

    shp = jax.ShapeDtypeStruct((8, PACK_COLS), F32)
    return pl.pallas_call(
        body, name="adamw_small",
        out_shape=(shp, shp, shp, shp, jax.ShapeDtypeStruct((1, LANES), F32)),
    )(sland, w, m, v)


_BIG = (("w_attn_o", AQ_W, D, 1), ("w_ret_o", R_W, D, 1), ("w_out", D, D, 0),
        ("w_up", D, FF, 1), ("w_down", FF, D, 0), ("w_ple_gate", D, D, 0), ("w_ple", PLE, D, 1))
_LATE = _BIG[:3]
_EARLY = _BIG[3:]
IN_SHARD = IN_W // N_DEV
_SMALL = ("mix_norm", "mlp_norm", "ple_norm", "final_norm", "ret_norm_gain", "ret_decay_logit",
          "attn_q_norm", "attn_k_norm")


def _shard_shape(rows, cols, axis):
    return (rows // N_DEV, cols) if axis == 0 else (rows, cols // N_DEV)


def _pack_shards(shards):
    flat = jnp.concatenate([s.reshape(-1) for s in shards])
    return flat.reshape(-1, PACK_COLS)


def _unpack_gathered(gathered):
    flat = gathered.reshape(N_DEV, -1)
    out, off = {}, 0
    for name, rows, cols, axis in _BIG:
        sr, sc = _shard_shape(rows, cols, axis)
        blk = flat[:, off:off + sr * sc].reshape(N_DEV, sr, sc)
        off += sr * sc
        out[name] = blk.reshape(rows, cols) if axis == 0 else blk.transpose(1, 0, 2).reshape(rows, cols)
    return out


def _pack_full_grads(grads, group):
    parts = []
    for name, rows, cols, axis in group:
        sr, sc = _shard_shape(rows, cols, axis)
        g = grads[name]
        blk = g.reshape(N_DEV, sr, sc) if axis == 0 else g.reshape(rows, N_DEV, sc).transpose(1, 0, 2)
        parts.append(blk.reshape(N_DEV, -1))
    flat = jnp.concatenate(parts, axis=1)
    return flat.reshape(N_DEV, -1, PACK_COLS)


def _unpack_shard(packed, group):
    flat = packed.reshape(-1)
    out, off = {}, 0
    for name, rows, cols, axis in group:
        sr, sc = _shard_shape(rows, cols, axis)
        out[name] = flat[off:off + sr * sc].reshape(1, sr, sc)
        off += sr * sc
    return out


def _pack_small(vals):
    rows = [jnp.pad(vals[n].reshape(-1), (0, PACK_COLS - vals[n].size)) for n in _SMALL]
    return jnp.stack(rows)


def _unpack_small(packed, like):
    return {n: packed[i, :like[n].size].reshape(like[n].shape) for i, n in enumerate(_SMALL)}


def _row(v):
    return jnp.pad(v.reshape(-1), (0, PACK_COLS - v.size))


def kernel(x, p, mix_norm, w_in, attn_q_norm, attn_k_norm, ret_decay_logit, ret_norm_gain, w_attn_o, w_ret_o, w_out, mlp_norm, w_up, w_down, ple_norm, w_ple_gate, w_ple, final_norm, loss_target, m_mix_norm, m_w_in, m_attn_q_norm, m_attn_k_norm, m_ret_decay_logit, m_ret_norm_gain, m_w_attn_o, m_w_ret_o, m_w_out, m_mlp_norm, m_w_up, m_w_down, m_ple_norm, m_w_ple_gate, m_w_ple, m_final_norm, v_mix_norm, v_w_in, v_attn_q_norm, v_attn_k_norm, v_ret_decay_logit, v_ret_norm_gain, v_w_attn_o, v_w_ret_o, v_w_out, v_mlp_norm, v_w_up, v_w_down, v_ple_norm, v_w_ple_gate, v_w_ple, v_final_norm):
    args = dict(locals())
    seq = x.shape[1]
    xs = x[0]
    ps = p[0, 0]
    tgt = loss_target[0]

    big_names = [b[0] for b in _BIG]
    wshard = _pack_shards([args[n] for n in big_names])
    win = _all_gather(w_in[0].astype(_MXU)).transpose(1, 0, 2).reshape(D, IN_W)

    g_mix, g_mlp, g_ple = mix_norm, mlp_norm, ple_norm
    g_fin = final_norm.reshape(1, D)
    gq = jnp.tile(attn_q_norm, (1, A_H))
    gk = jnp.tile(attn_k_norm, (1, A_KV))
    seg = _seg_mean_matrix()
    ca, sa, cr, sr = _rope_tables(seq)

    pa, pr, pg, h, rest_g = _in_proj(xs, g_mix, win, wshard.astype(_MXU))
    wfull = _unpack_gathered(rest_g)
    wao, wro, wout = wfull["w_attn_o"], wfull["w_ret_o"], wfull["w_out"]
    wup, wdown, wpg, wple = wfull["w_up"], wfull["w_down"], wfull["w_ple_gate"], wfull["w_ple"]
    qh, kh, vh, rqh, rkh = _qk_prep(pa, pr, gq, gk, seg, ca, sa, cr, sr)

    tq = _tile(seq, 128)
    tk = _tile(seq // 4, 2048)
    qt8 = qh.reshape(seq, A_H, A_HD).transpose(1, 2, 0)
    vta = jnp.stack([jnp.concatenate([_chunk_t(vh[:, g * A_HD:(g + 1) * A_HD], tk),
                                      jnp.ones((seq // tk, 16, tk), _MXU)], axis=1) for g in range(A_KV)])
    ot, lse = _attn_fwd(qt8, kh, vta, tq, tk)
    o = _heads_to_rows(ot)

    zb = jnp.broadcast_to(ret_decay_logit.reshape(2 * R_H, 1, 1), (2 * R_H, 1, LANES))
    tm_, tmw, tqd, tqdw, tkd, tkdw, tg, tgw = _ret_tables(zb)
    cb = _tile(seq // CHUNK, 4)
    yf, yb, pstf, pstb = _ret_fwd(rqh, rkh, pr, tm_, tqd, tkd, tg, cb)

    x1, merged, ret_in = _merge_fwd(xs, o, yf, yb, pr, pg, ret_norm_gain, wao, wro, wout)
    x2 = _mlp_fwd(x1, g_mlp, wup, wdown)

    dx2, de, dz, hp, loss_p, dg_fin, dg_ple = _ple_loss(x2, ps, tgt, g_ple, g_fin, wpg, wpg.T, wple)
    dx1, act, du, hm, dg_mlp = _mlp_bwd(x1, dx2, g_mlp, wup, wdown.T, wup.T)
    me = 4 * lax.axis_index("x") + 2 * lax.axis_index("y") + lax.axis_index("c")
    gpack_e = _pack_full_grads({"w_up": _wgrad(hm, du, "wgrad_up"), "w_down": _wgrad(act, dx2, "wgrad_down"),
                                "w_ple_gate": _wgrad(hp, dz, "wgrad_ple_gate"), "w_ple": _wgrad(ps, de, "wgrad_ple")},
                               _EARLY)
    dpg, dao, dro, do, dry, drg, dg_gn, land_e = _merge_bwd(dx1, o, yf, yb, pr, pg, ret_norm_gain, wao, wro,
                                                            wout.T, wao.T, wro.T, _bf(gpack_e))
    *rd, dlam = _ret_bwd(rqh, rkh, pr, dry, pstf, pstb, tm_, tmw, tqd, tqdw, tkd, tkdw, tg, tgw, cb)

    ksplit = 1
    tkb = _tile(seq // 4, 512)
    dot_ = do.reshape(seq, A_H, A_HD).transpose(1, 2, 0)
    dkt, dvt, dqt = _attn_bwd(qt8, ot, dot_, lse, kh, vh, _chunk_t(kh, tkb), tq, tkb, ksplit)
    dqh = _heads_to_rows(dqt[0] if ksplit == 1 else jnp.sum(dqt, axis=0))
    dpa, dpr, dg_q, dg_k = _qk_prep_bwd(pa, dqh, _chunks_to_rows(dkt), _chunks_to_rows(dvt), rd[0:3], rd[3:6], drg, gq, gk, seg, ca, sa, cr, sr)
    wg_in = jnp.concatenate([_wgrad(h, dpa, "wgrad_in_a"), _wgrad(h, dpr, "wgrad_in_r"),
                             _wgrad(h, dpg, "wgrad_in_g")], axis=1)
    gpack_l = _pack_full_grads({"w_attn_o": _wgrad(o, dao, "wgrad_attn_o"),
                                "w_ret_o": _wgrad(ret_in, dro, "wgrad_ret_o"),
                                "w_out": _wgrad(merged, dx1, "wgrad_out")}, _LATE)
    gpack_in = wg_in.reshape(D, N_DEV, IN_SHARD).transpose(1, 0, 2)
    grad_x, dg_mix, land_in, land_l = _in_proj_bwd(xs, dx1, g_mix, dpa, dpr, dpg, win.T,
                                                   [_bf(gpack_in), _bf(gpack_l)])
    small = jnp.stack(
        [_row(dg_mix), _row(dg_mlp), _row(dg_ple), _row(dg_fin), _row(dg_gn), _row(dlam[:, 0, 0]),
         _row(loss_p[0, 0:1]), jnp.zeros((PACK_COLS,), F32)]
        + [_row(dg_q[0, hh * A_HD:(hh + 1) * A_HD]) for hh in range(A_H)]
        + [_row(dg_k[0, hh * A_HD:(hh + 1) * A_HD]) for hh in range(A_KV)]
        + [jnp.zeros((PACK_COLS,), F32)] * (SMALL_ROWS - 18))

    own = lambda pack: lax.dynamic_index_in_dim(pack, me, axis=0, keepdims=False)
    (sland,) = _exchange_grads([small[None]])
    in_sh = _adamw_big(land_in, own(gpack_in), w_in[0], m_w_in[0], v_w_in[0], "adamw_w_in")
    group_sh = []
    for group, land, pack, name in ((_EARLY, land_e, gpack_e, "adamw_early"), (_LATE, land_l, gpack_l, "adamw_late")):
        packed = lambda pre: _pack_shards([args[pre + g[0]] for g in group])
        group_sh.append((group, _adamw_big(land, own(pack), packed(""), packed("m_"), packed("v_"), name)))
    g_sm, d_sm, m_sm, v_sm, loss_row = _adamw_small(
        sland, _pack_small({n: args[n] for n in _SMALL}), _pack_small({n: args["m_" + n] for n in _SMALL}),
        _pack_small({n: args["v_" + n] for n in _SMALL}))

    names = ["mix_norm", "w_in", "attn_q_norm", "attn_k_norm", "ret_decay_logit", "ret_norm_gain", "w_attn_o",
             "w_ret_o", "w_out", "mlp_norm", "w_up", "w_down", "ple_norm", "w_ple_gate", "w_ple", "final_norm"]
    like = {n: args[n] for n in _SMALL}
    outs = [loss_row[0, 0], grad_x[None]]
    for kind, sm in enumerate((g_sm, d_sm, m_sm, v_sm)):
        table = {**_unpack_small(sm, like), "w_in": in_sh[kind][None]}
        for group, res in group_sh:
            table.update(_unpack_shard(res[kind], group))
        outs += [table[n] for n in names]
    return tuple(outs)
```

```python
import functools

import jax
import jax.numpy as jnp
from jax import lax
from jax.experimental import pallas as pl
from jax.experimental.pallas import tpu as pltpu

F32 = jnp.float32
_MXU = jnp.bfloat16

D = 1024
PLE = 256
GRID_W = 64
A_HD = 64
A_H = 8
A_KV = 2
A_G = A_H // A_KV
AQ_W = A_H * A_HD
AKV_W = A_KV * A_HD
R_HD = 128
R_H = 4
R_W = R_H * R_HD
IN_W = AQ_W + 2 * AKV_W + 4 * R_W + 2 * D
PA_W = AQ_W + 2 * AKV_W
PR_W = 4 * R_W
PG_W = 2 * D
FF = 4 * D
CHUNK = 128
ROPE_THETA = 10000.0
EPS = 1e-6
GN_EPS = 1e-5
N_DEV = 8

LR, B1, B2, ADAM_EPS, WD, STEP = 0.001, 0.9, 0.999, 1e-08, 0.01, 10

LANES = 128
PACK_COLS = 1024
SMALL_ROWS = 24


def _tile(n, pref):
    t = min(n, pref)
    assert n % t == 0, (n, t)
    return t


def _bf(a):
    return a.astype(_MXU)


def _mm(a, b):
    return jnp.dot(_bf(a), _bf(b), preferred_element_type=F32)


def _mm_nt(a, b):
    return lax.dot_general(_bf(a), _bf(b), (((1,), (1,)), ((), ())), preferred_element_type=F32)


def _mm_tn(a, b):
    return lax.dot_general(_bf(a), _bf(b), (((0,), (0,)), ((), ())), preferred_element_type=F32)


def _seg_mean(v, segm):
    hi = _bf(v)
    lo = _bf(v - hi.astype(F32))
    return _mm(hi, segm) + _mm(lo, segm)


def _sigmoid(z):
    return 1.0 / (1.0 + jnp.exp(-z))


def _rms(x):
    r = lax.rsqrt(jnp.mean(x * x, axis=-1, keepdims=True) + EPS)
    return x * r, r


def _rms_bwd(n, r, gain, dy):
    dn = dy * gain
    return r * (dn - n * jnp.mean(dn * n, axis=-1, keepdims=True))


def _swap_halves(x, half):
    n = x.shape[-1]
    lane = lax.broadcasted_iota(jnp.int32, x.shape, x.ndim - 1)
    first = (lane % (2 * half)) < half
    return jnp.where(first, pltpu.roll(x, n - half, axis=1), pltpu.roll(x, half, axis=1))


def _rope(x, cos, sin, half):
    return x * cos + _swap_halves(x, half) * sin


def _cat(t, reps):
    return jnp.concatenate([t] * reps, axis=1)


def _full(shape):
    nd = len(shape)
    return pl.BlockSpec(shape, lambda *_: (0,) * nd)


def _rope_tables(seq):
    def tab(head_dim):
        n_axis = head_dim // 4
        freqs = ROPE_THETA ** (-jnp.arange(n_axis, dtype=F32) / n_axis)
        rows = seq // GRID_W
        row = jnp.repeat(jnp.arange(rows, dtype=F32), GRID_W)
        col = jnp.tile(jnp.arange(GRID_W, dtype=F32), rows)
        ang = jnp.concatenate([row[:, None] * freqs, col[:, None] * freqs], axis=-1)
        c, s = jnp.cos(ang), jnp.sin(ang)
        return jnp.concatenate([c, c], axis=-1), jnp.concatenate([-s, s], axis=-1)
    ca, sa = tab(A_HD)
    cr, sr = tab(R_HD)
    return jnp.tile(ca, (1, 2)), jnp.tile(sa, (1, 2)), cr, sr


def _seg_mean_matrix():
    i = jnp.arange(AQ_W) // A_HD
    return (i[:, None] == i[None, :]).astype(F32) / A_HD


def _mesh_pos():
    return lax.axis_index("x"), lax.axis_index("y"), lax.axis_index("c")


def _gather_steps(x_ref, out_ref, send_sems, recv_sems, local_sem, base=0):
    x, y, c = _mesh_pos()
    me, sibling = (x, y, c), (x, y, 1 - c)
    chips = [(1 - x, y), (x, 1 - y), (1 - x, 1 - y)]

    def slot(px, py, pc):
        return out_ref.at[4 * px + 2 * py + pc]

    def copy(k, block, to, src=None):
        return pltpu.make_async_remote_copy(
            src_ref=slot(*block) if src is None else src, dst_ref=slot(*block),
            send_sem=send_sems.at[base + k], recv_sem=recv_sems.at[base + k],
            device_id=to, device_id_type=pl.DeviceIdType.MESH)

    mine = pltpu.make_async_copy(x_ref, slot(*me), local_sem)
    first = [copy(0, me, sibling, src=x_ref)]
    first += [copy(1 + j, me, (*chip, c), src=x_ref) for j, chip in enumerate(chips)]
    passed = [copy(4 + j, (*chip, c), sibling) for j, chip in enumerate(chips)]

    def start():
        mine.start()
        for cp in first:
            cp.start()

    def finish():
        for j, chip in enumerate(chips):
            copy(1 + j, (*chip, c), me).wait_recv()
            passed[j].start()
        copy(0, sibling, me).wait_recv()
        for j, chip in enumerate(chips):
            copy(4 + j, (*chip, 1 - c), me).wait_recv()
        for cp in first + passed:
            cp.wait_send()
        mine.wait()

    return start, finish


def _exchange_steps(g_ref, land_ref, send_sems, recv_sems, local_sem, per_device, base=0):
    x, y, c = _mesh_pos()
    me = 4 * x + 2 * y + c

    def row(j):
        return g_ref.at[j if per_device else 0]

    def peer(k):
        p = (x ^ ((k >> 2) & 1), y ^ ((k >> 1) & 1), c ^ (k & 1))
        return p, 4 * p[0] + 2 * p[1] + p[2]

    def copy(k, src, dst):
        return pltpu.make_async_remote_copy(
            src_ref=src, dst_ref=dst, send_sem=send_sems.at[base + k - 1], recv_sem=recv_sems.at[base + k - 1],
            device_id=peer(k)[0], device_id_type=pl.DeviceIdType.MESH)

    own = pltpu.make_async_copy(row(me), land_ref.at[me], local_sem)
    sends = [copy(k, row(peer(k)[1]), land_ref.at[me]) for k in range(1, N_DEV)]

    def start():
        own.start()
        for cp in sends:
            cp.start()

    def finish():
        for k in range(1, N_DEV):
            copy(k, row(me), land_ref.at[peer(k)[1]]).wait_recv()
        for cp in sends:
            cp.wait_send()
        own.wait()

    return start, finish


_COMM_SCRATCH = [pltpu.SemaphoreType.DMA((7,)), pltpu.SemaphoreType.DMA((7,)), pltpu.SemaphoreType.DMA]
_ANY = pl.BlockSpec(memory_space=pl.ANY)


def _all_gather(shard):
    def body(x_ref, out_ref, send_sems, recv_sems, local_sem):
        start, finish = _gather_steps(x_ref, out_ref, send_sems, recv_sems, local_sem)
        start()
        finish()

    return pl.pallas_call(
        body, name="all_gather_weights", out_shape=jax.ShapeDtypeStruct((N_DEV,) + shard.shape, shard.dtype),
        in_specs=[_ANY], out_specs=_ANY, scratch_shapes=list(_COMM_SCRATCH),
    )(shard)


def _exchange_grads(packs):
    n = len(packs)

    def body(*refs):
        g_refs, land_refs = refs[:n], refs[n:2 * n]
        send_sems, recv_sems, local_sems = refs[2 * n:]
        steps = [_exchange_steps(g_refs[t], land_refs[t], send_sems, recv_sems, local_sems.at[t],
                                 packs[t].shape[0] == N_DEV, base=7 * t) for t in range(n)]
        for start, _ in steps:
            start()
        for _, finish in steps:
            finish()

    return pl.pallas_call(
        body, name="exchange_grads",
        out_shape=tuple(jax.ShapeDtypeStruct((N_DEV,) + g.shape[1:], g.dtype) for g in packs),
        in_specs=[_ANY] * n, out_specs=(_ANY,) * n,
        scratch_shapes=[pltpu.SemaphoreType.DMA((7 * n,)), pltpu.SemaphoreType.DMA((7 * n,)),
                        pltpu.SemaphoreType.DMA((n,))],
    )(*packs)


def _in_proj(x, gain, w, rest):
    seq = x.shape[0]
    tm = _tile(seq, 256)
    nsteps = seq // tm

    def body(x_ref, g_ref, w_ref, rest_ref, pa_ref, pr_ref, pg_ref, h_ref, gath_ref, send_sems, recv_sems, local_sem):
        start, finish = _gather_steps(rest_ref, gath_ref, send_sems, recv_sems, local_sem)
        pl.when(pl.program_id(0) == 0)(start)
        n, _ = _rms(x_ref[...])
        h = _bf(n * g_ref[...])
        h_ref[...] = h
        pa_ref[...] = _mm(h, w_ref[:, 0:PA_W])
        pr_ref[...] = _mm(h, w_ref[:, PA_W:PA_W + PR_W])
        pg_ref[...] = _mm(h, w_ref[:, PA_W + PR_W:IN_W])
        pl.when(pl.program_id(0) == nsteps - 1)(finish)

    row = lambda w_: pl.BlockSpec((tm, w_), lambda i: (i, 0))
    return pl.pallas_call(
        body, name="in_proj", grid=(nsteps,),
        in_specs=[row(D), _full((1, D)), _full((D, IN_W)), _ANY],
        out_specs=(row(PA_W), row(PR_W), row(PG_W), row(D), _ANY),
        out_shape=(jax.ShapeDtypeStruct((seq, PA_W), F32), jax.ShapeDtypeStruct((seq, PR_W), F32),
                   jax.ShapeDtypeStruct((seq, PG_W), F32), jax.ShapeDtypeStruct((seq, D), _MXU),
                   jax.ShapeDtypeStruct((N_DEV,) + rest.shape, rest.dtype)),
        scratch_shapes=list(_COMM_SCRATCH),
    )(x, gain, w, rest)


def _qk_prep(pa, pr, gq, gk, seg, ca, sa, cr, sr):
    seq = pa.shape[0]
    tm = _tile(seq, 256)

    def body(pa_ref, pr_ref, gq_ref, gk_ref, seg_ref, ca_ref, sa_ref, cr_ref, sr_ref,
             qh_ref, kh_ref, v_ref, rq_ref, rk_ref):
        q = pa_ref[:, 0:AQ_W]
        k = pa_ref[:, AQ_W:AQ_W + AKV_W]
        v_ref[...] = _bf(pa_ref[:, AQ_W + AKV_W:PA_W])
        ca_, sa_ = ca_ref[...], sa_ref[...]
        msq = _seg_mean(q * q, seg_ref[...])
        qn = q * lax.rsqrt(msq + EPS) * gq_ref[...]
        qh_ref[...] = _bf(_rope(qn, _cat(ca_, 4), _cat(sa_, 4), A_HD // 2) * (A_HD ** -0.5))
        msk = _seg_mean(k * k, seg_ref[0:AKV_W, 0:AKV_W])
        kn = k * lax.rsqrt(msk + EPS) * gk_ref[...]
        kh_ref[...] = _bf(_rope(kn, ca_, sa_, A_HD // 2))
        cr_, sr_ = _cat(cr_ref[...], 4), _cat(sr_ref[...], 4)
        rq_ref[...] = _rope(pr_ref[:, 0:R_W], cr_, sr_, R_HD // 2) * (R_HD ** -0.5)
        rk_ref[...] = _rope(pr_ref[:, R_W:2 * R_W], cr_, sr_, R_HD // 2)

    row = lambda w_: pl.BlockSpec((tm, w_), lambda i: (i, 0))
    return pl.pallas_call(
        body, name="qk_prep", grid=(seq // tm,),
        in_specs=[row(PA_W), row(2 * R_W), _full((1, AQ_W)), _full((1, AKV_W)), _full((AQ_W, AQ_W)),
                  row(LANES), row(LANES), row(LANES), row(LANES)],
        out_specs=(row(AQ_W), row(AKV_W), row(AKV_W), row(R_W), row(R_W)),
        out_shape=(jax.ShapeDtypeStruct((seq, AQ_W), _MXU), jax.ShapeDtypeStruct((seq, AKV_W), _MXU),
                   jax.ShapeDtypeStruct((seq, AKV_W), _MXU), jax.ShapeDtypeStruct((seq, R_W), F32),
                   jax.ShapeDtypeStruct((seq, R_W), F32)),
    )(pa, pr, gq, gk, seg, ca, sa, cr, sr)


def _chunk_t(a, tk):
    seq = a.shape[0]
    return a.reshape(seq // tk, tk, a.shape[1]).transpose(0, 2, 1)


def _heads_to_rows(t):
    return t.transpose(2, 0, 1).reshape(t.shape[2], AQ_W)


def _attn_fwd(qt8, k2, vta, tq, tk):
    seq = k2.shape[0]
    nck = seq // tk
    rows = A_G * tq
    vrows = vta.shape[2]
    rb = _tile(tk, 256)
    assert nck % 2 == 0, nck

    def body(qt_ref, k_ref, vt_ref, o_ref, lse_ref, m_sc, acc_sc, qtp_sc, s_a, s_b, p_a, p_b, al_a, al_b):
        g = pl.program_id(0)
        qtp_sc[...] = jnp.zeros_like(qtp_sc)
        qtp_sc[pl.ds(pl.multiple_of(g * A_HD, A_HD), A_HD), :] = jnp.concatenate(
            [qt_ref[a] for a in range(A_G)], axis=1)
        m_sc[...] = jnp.full((1, rows), -jnp.inf, F32)
        acc_sc[...] = jnp.zeros_like(acc_sc)

        def scores(c):
            kc = k_ref[pl.ds(pl.multiple_of(c * tk, tk), tk), :]
            return _mm(kc, qtp_sc[...])

        def stage(c, s_cur, s_nxt, p_cur, p_prv, al_cur, al_prv, first=False, last=False):
            if not last:
                s_nxt[...] = scores(c + 1)
            if not first:
                acc_sc[...] = al_prv[...] * acc_sc[...] + _mm(vt_ref[0, c - 1], p_prv[...])
            m_old = m_sc[...]
            mx = None
            for r in range(0, tk, rb):
                bm = jnp.max(s_cur[r:r + rb, :].reshape(rb // 8, 8, rows), axis=0)
                mx = bm if mx is None else jnp.maximum(mx, bm)
            m_new = jnp.maximum(m_old, jnp.max(mx, axis=0, keepdims=True))
            for r in range(0, tk, rb):
                p_cur[r:r + rb, :] = _bf(jnp.exp(s_cur[r:r + rb, :] - m_new))
            al_cur[...] = jnp.exp(m_old - m_new)
            m_sc[...] = m_new

        s_a[...] = scores(0)
        stage(0, s_a, s_b, p_a, p_b, al_a, al_b, first=True)

        def pair(j, carry):
            stage(2 * j + 1, s_b, s_a, p_b, p_a, al_b, al_a)
            stage(2 * j + 2, s_a, s_b, p_a, p_b, al_a, al_b)
            return carry

        lax.fori_loop(0, nck // 2 - 1, pair, 0)
        stage(nck - 1, s_b, s_a, p_b, p_a, al_b, al_a, last=True)
        acc = al_b[...] * acc_sc[...] + _mm(vt_ref[0, nck - 1], p_b[...])
        l = acc[A_HD:A_HD + 1, :]
        lse = m_sc[...] + jnp.log(l)
        out = acc[0:A_HD, :] * (1.0 / l)
        for a in range(A_G):
            o_ref[a] = out[:, a * tq:(a + 1) * tq]
            lse_ref[a] = lse[:, a * tq:(a + 1) * tq]

    return pl.pallas_call(
        body, name="attn_fwd", grid=(A_KV, seq // tq),
        in_specs=[pl.BlockSpec((A_G, A_HD, tq), lambda g, i: (g, 0, i)),
                  _full((seq, LANES)), pl.BlockSpec((1, nck, vrows, tk), lambda g, i: (g, 0, 0, 0))],
        out_specs=(pl.BlockSpec((A_G, A_HD, tq), lambda g, i: (g, 0, i)),
                   pl.BlockSpec((A_G, 1, tq), lambda g, i: (g, 0, i))),
        out_shape=(jax.ShapeDtypeStruct((A_H, A_HD, seq), F32), jax.ShapeDtypeStruct((A_H, 1, seq), F32)),
        scratch_shapes=[pltpu.VMEM((1, rows), F32), pltpu.VMEM((vrows, rows), F32), pltpu.VMEM((LANES, rows), _MXU),
                        pltpu.VMEM((tk, rows), F32), pltpu.VMEM((tk, rows), F32),
                        pltpu.VMEM((tk, rows), _MXU), pltpu.VMEM((tk, rows), _MXU),
                        pltpu.VMEM((1, rows), F32), pltpu.VMEM((1, rows), F32)],
    )(qt8, k2, vta)


def _attn_bwd(qt8, ot, dot_, lse, k2, v2, k2t, tq, tk, ksplit):
    seq = k2.shape[0]
    sh = seq // ksplit
    nck = sh // tk
    rows = A_G * tq
    rb = _tile(tk, 16384 // rows)
    assert nck % 2 == 0, nck

    def body(qt_ref, ot_ref, dot_ref, lse_ref, k_ref, v_ref, kt_ref,
             dk_ref, dv_ref, dq_ref, dq_sc, qtp_sc, dotp_sc, pt_sc, dst_sc,
             s_a, s_b, dp_a, dp_b, p_a, p_b, ds_a, ds_b):
        g = pl.program_id(1)
        hrows = pl.ds(pl.multiple_of(g * A_HD, A_HD), A_HD)

        @pl.when(pl.program_id(2) == 0)
        def _():
            dk_ref[...] = jnp.zeros_like(dk_ref)
            dv_ref[...] = jnp.zeros_like(dv_ref)

        lse_row = jnp.concatenate([lse_ref[a] for a in range(A_G)], axis=1)
        dd = jnp.concatenate([jnp.sum(ot_ref[a] * dot_ref[a], axis=0, keepdims=True)
                              for a in range(A_G)], axis=1)
        qtp_sc[...] = jnp.zeros_like(qtp_sc)
        dotp_sc[...] = jnp.zeros_like(dotp_sc)
        qtp_sc[hrows, :] = jnp.concatenate([qt_ref[a] for a in range(A_G)], axis=1)
        dotp_sc[hrows, :] = _bf(jnp.concatenate([dot_ref[a] for a in range(A_G)], axis=1))
        dq_sc[...] = jnp.zeros_like(dq_sc)

        def products(c, s_ref, dp_ref):
            sl = pl.ds(pl.multiple_of(c * tk, tk), tk)
            s_ref[...] = _mm(k_ref[sl, :], qtp_sc[...])
            dp_ref[...] = _mm(v_ref[sl, :], dotp_sc[...])

        def accumulate(c, p_ref, ds_ref):
            pt_sc[...] = p_ref[...].T
            dst_sc[...] = ds_ref[...].T
            dq_sc[...] += _mm(kt_ref[c, hrows, :], ds_ref[...])
            dv_ref[0, c] += _mm(dotp_sc[hrows, :], pt_sc[...])
            dk_ref[0, c] += _mm(qtp_sc[hrows, :], dst_sc[...])

        def stage(c, s_cur, dp_cur, s_nxt, dp_nxt, p_cur, ds_cur, p_prv, ds_prv, first=False, last=False):
            if not last:
                products(c + 1, s_nxt, dp_nxt)
            if not first:
                accumulate(c - 1, p_prv, ds_prv)
            for r in range(0, tk, rb):
                p = jnp.exp(s_cur[r:r + rb, :] - lse_row)
                p_cur[r:r + rb, :] = _bf(p)
                ds_cur[r:r + rb, :] = _bf(p * (dp_cur[r:r + rb, :] - dd))

        products(0, s_a, dp_a)
        stage(0, s_a, dp_a, s_b, dp_b, p_a, ds_a, p_b, ds_b, first=True)

        def pair(j, carry):
            stage(2 * j + 1, s_b, dp_b, s_a, dp_a, p_b, ds_b, p_a, ds_a)
            stage(2 * j + 2, s_a, dp_a, s_b, dp_b, p_a, ds_a, p_b, ds_b)
            return carry

        lax.fori_loop(0, nck // 2 - 1, pair, 0)
        stage(nck - 1, s_b, dp_b, s_a, dp_a, p_b, ds_b, p_a, ds_a, last=True)
        accumulate(nck - 1, p_b, ds_b)
        for a in range(A_G):
            dq_ref[0, a] = dq_sc[:, a * tq:(a + 1) * tq]

    tspec = pl.BlockSpec((A_G, A_HD, tq), lambda s, g, i: (g, 0, i))
    kspec = pl.BlockSpec((sh, LANES), lambda s, g, i: (s, 0))
    gspec = pl.BlockSpec((1, nck, A_HD, tk), lambda s, g, i: (g, s, 0, 0))
    gshape = jax.ShapeDtypeStruct((A_KV, seq // tk, A_HD, tk), F32)
    big = lambda dt: pltpu.VMEM((tk, rows), dt)
    bigt = pltpu.VMEM((rows, tk), _MXU)
    return pl.pallas_call(
        body, name="attn_bwd", grid=(ksplit, A_KV, seq // tq),
        in_specs=[tspec, tspec, tspec, pl.BlockSpec((A_G, 1, tq), lambda s, g, i: (g, 0, i)),
                  kspec, kspec, pl.BlockSpec((nck, LANES, tk), lambda s, g, i: (s, 0, 0))],
        out_specs=(gspec, gspec, pl.BlockSpec((1, A_G, A_HD, tq), lambda s, g, i: (s, g, 0, i))),
        out_shape=(gshape, gshape, jax.ShapeDtypeStruct((ksplit, A_H, A_HD, seq), F32)),
        scratch_shapes=[pltpu.VMEM((A_HD, rows), F32), pltpu.VMEM((LANES, rows), _MXU), pltpu.VMEM((LANES, rows), _MXU),
                        bigt, bigt,
                        big(F32), big(F32), big(F32), big(F32), big(_MXU), big(_MXU), big(_MXU), big(_MXU)],
    )(qt8, ot, dot_, lse, k2, v2, k2t)


def _chunks_to_rows(t):
    return t.transpose(1, 3, 0, 2).reshape(t.shape[1] * t.shape[3], AKV_W)


def _ret_tables(zb):
    c = CHUNK

    def body(z_ref, m_ref, mw_ref, qd_ref, qdw_ref, kd_ref, kdw_ref, g_ref, gw_ref):
        fwd = pl.program_id(0) < R_H
        z = z_ref[0]
        lam = jnp.minimum(z, 0.0) - jnp.log(1.0 + jnp.exp(-jnp.abs(z)))
        i = lax.broadcasted_iota(jnp.int32, (c, c), 0).astype(F32)
        j = lax.broadcasted_iota(jnp.int32, (c, c), 1).astype(F32)
        diff = jnp.where(fwd, i - j, j - i)
        keep = diff >= jnp.where(fwd, 0.0, 1.0)
        dist = jnp.maximum(diff, 0.0)
        m = jnp.where(keep, jnp.exp(lam * dist), 0.0)
        m_ref[0] = m
        mw_ref[0] = m * dist
        fq = jnp.where(fwd, i + 1.0, c - i)
        qd = jnp.exp(lam * fq)
        qd_ref[0] = qd
        qdw_ref[0] = qd * fq
        fk = jnp.where(fwd, c - 1.0 - i, i)
        kd = jnp.exp(lam * fk)
        kd_ref[0] = kd
        kdw_ref[0] = kd * fk
        gdec = jnp.exp(lam * c)
        g_ref[0] = gdec
        gw_ref[0] = gdec * c

    big = pl.BlockSpec((1, c, c), lambda t: (t, 0, 0))
    vec = pl.BlockSpec((1, 1, LANES), lambda t: (t, 0, 0))
    bshape = jax.ShapeDtypeStruct((2 * R_H, c, c), F32)
    vshape = jax.ShapeDtypeStruct((2 * R_H, 1, LANES), F32)
    return pl.pallas_call(
        body, name="ret_tables", grid=(2 * R_H,), in_specs=[vec],
        out_specs=(big, big, big, big, big, big, vec, vec),
        out_shape=(bshape,) * 6 + (vshape, vshape),
    )(zb)


def _ret_fwd(rq, rk, pr, m, qd, kd, gdec, cb):
    seq = rq.shape[0]
    c = CHUNK
    ns = seq // (cb * c)

    def body(qf_ref, kf_ref, vf_ref, qb_ref, kb_ref, vb_ref, m_ref, qd_ref, kd_ref, g_ref,
             yf_ref, yb_ref, pstf_ref, pstb_ref, p_sc):
        @pl.when(pl.program_id(0) == 0)
        def _():
            p_sc[...] = jnp.zeros_like(p_sc)

        dirs = ((qf_ref, kf_ref, vf_ref, yf_ref, pstf_ref), (qb_ref, kb_ref, vb_ref, yb_ref, pstb_ref))
        heads = [slice(h * R_HD, (h + 1) * R_HD) for h in range(R_H)]

        def chunk(j, carry):
            early = []
            for d, (q_ref, k_ref, v_ref, _, _) in enumerate(dirs):
                cc = j if d == 0 else cb - 1 - j
                sl = pl.ds(pl.multiple_of(cc * c, c), c)
                for h, hs in enumerate(heads):
                    t = d * R_H + h
                    early.append((_mm_nt(q_ref[sl, hs], k_ref[sl, hs]), _mm(q_ref[sl, hs] * qd_ref[t], p_sc[t]),
                                  _mm_tn(k_ref[sl, hs] * kd_ref[t], v_ref[sl, hs])))
            for d, (_, _, v_ref, y_ref, pst_ref) in enumerate(dirs):
                cc = j if d == 0 else cb - 1 - j
                sl = pl.ds(pl.multiple_of(cc * c, c), c)
                for h, hs in enumerate(heads):
                    t = d * R_H + h
                    qk, qp, kv = early[t]
                    p = p_sc[t]
                    pst_ref[h, cc] = p
                    y_ref[sl, hs] = _mm(qk * m_ref[t], v_ref[sl, hs]) + qp
                    p_sc[t] = p * g_ref[t] + kv
            return carry

        lax.fori_loop(0, cb, chunk, 0)

    asc = lambda off: pl.BlockSpec((cb * c, R_W), lambda n: (n, off))
    desc = lambda off: pl.BlockSpec((cb * c, R_W), lambda n: (ns - 1 - n, off))
    return pl.pallas_call(
        body, name="ret_fwd", grid=(ns,),
        in_specs=[asc(0), asc(0), asc(2), desc(0), desc(0), desc(2),
                  _full((2 * R_H, c, c)), _full((2 * R_H, c, c)), _full((2 * R_H, c, c)), _full((2 * R_H, 1, LANES))],
        out_specs=(asc(0), desc(0),
                   pl.BlockSpec((R_H, cb, R_HD, R_HD), lambda n: (0, n, 0, 0)),
                   pl.BlockSpec((R_H, cb, R_HD, R_HD), lambda n: (0, ns - 1 - n, 0, 0))),
        out_shape=(jax.ShapeDtypeStruct((seq, R_W), F32), jax.ShapeDtypeStruct((seq, R_W), F32),
                   jax.ShapeDtypeStruct((R_H, seq // c, R_HD, R_HD), F32),
                   jax.ShapeDtypeStruct((R_H, seq // c, R_HD, R_HD), F32)),
        scratch_shapes=[pltpu.VMEM((2 * R_H, R_HD, R_HD), F32)],
    )(rq, rk, pr, rq, rk, pr, m, qd, kd, gdec)


def _ret_bwd(rq, rk, pr, dry, pstf, pstb, m, mw, qd, qdw, kd, kdw, gdec, gw, cb):
    seq = rq.shape[0]
    c = CHUNK
    ns = seq // (cb * c)

    def body(qf_ref, kf_ref, vf_ref, dyf_ref, pstf_ref, qb_ref, kb_ref, vb_ref, dyb_ref, pstb_ref,
             m_ref, mw_ref, qd_ref, qdw_ref, kd_ref, kdw_ref, g_ref, gw_ref,
             dqf_ref, dkf_ref, dvf_ref, dqb_ref, dkb_ref, dvb_ref, dlam_ref, r_sc, acc_sc, e_sc, g_sc):
        n = pl.program_id(0)

        @pl.when(n == 0)
        def _():
            r_sc[...] = jnp.zeros_like(r_sc)
            acc_sc[...] = jnp.zeros_like(acc_sc)
            e_sc[...] = jnp.zeros_like(e_sc)
            g_sc[...] = jnp.zeros_like(g_sc)

        dirs = ((qf_ref, kf_ref, vf_ref, dyf_ref, pstf_ref, dqf_ref, dkf_ref, dvf_ref),
                (qb_ref, kb_ref, vb_ref, dyb_ref, pstb_ref, dqb_ref, dkb_ref, dvb_ref))
        heads = [slice(h * R_HD, (h + 1) * R_HD) for h in range(R_H)]

        def chunk(j, carry):
            first = []
            for d, (q_ref, k_ref, v_ref, dy_ref, pst_ref, _, _, _) in enumerate(dirs):
                cc = cb - 1 - j if d == 0 else j
                sl = pl.ds(pl.multiple_of(cc * c, c), c)
                for h, hs in enumerate(heads):
                    t = d * R_H + h
                    q, k, v, dy = q_ref[sl, hs], k_ref[sl, hs], v_ref[sl, hs], dy_ref[sl, hs]
                    r = r_sc[t]
                    first.append((_mm_nt(q, k), _mm_nt(dy, v), _mm_nt(dy, pst_ref[h, cc]), _mm_nt(v, r),
                                  _mm(k * kd_ref[t], r), _mm_tn(q * qd_ref[t], dy)))
            for d, (q_ref, k_ref, _, dy_ref, pst_ref, dq_ref, dk_ref, dv_ref) in enumerate(dirs):
                cc = cb - 1 - j if d == 0 else j
                sl = pl.ds(pl.multiple_of(cc * c, c), c)
                for h, hs in enumerate(heads):
                    t = d * R_H + h
                    qk, ds, dyp, vr, kr, qdy = first[t]
                    q, k, dy = q_ref[sl, hs], k_ref[sl, hs], dy_ref[sl, hs]
                    r = r_sc[t]
                    da = ds * m_ref[t]
                    dv_ref[sl, hs] = _mm_tn(qk * m_ref[t], dy) + kr
                    dq_ref[sl, hs] = _mm(da, k) + dyp * qd_ref[t]
                    dk_ref[sl, hs] = _mm_tn(da, q) + vr * kd_ref[t]
                    acc_sc[t] += dyp * q * qdw_ref[t] + vr * k * kdw_ref[t]
                    e_sc[t] += ds * qk * mw_ref[t]
                    g_sc[t] += r * pst_ref[h, cc]
                    r_sc[t] = r * g_ref[t] + qdy
            return carry

        lax.fori_loop(0, cb, chunk, 0)

        @pl.when(n == ns - 1)
        def _():
            for t in range(2 * R_H):
                tot = jnp.sum(jnp.sum(acc_sc[t] + e_sc[t] + g_sc[t] * gw_ref[t], axis=0, keepdims=True),
                              axis=1, keepdims=True)
                dlam_ref[t] = jnp.broadcast_to(tot, (1, LANES))

    asc = lambda off: pl.BlockSpec((cb * c, R_W), lambda n: (n, off))
    desc = lambda off: pl.BlockSpec((cb * c, R_W), lambda n: (ns - 1 - n, off))
    big = _full((2 * R_H, c, c))
    vec = _full((2 * R_H, 1, LANES))
    oshape = jax.ShapeDtypeStruct((seq, R_W), F32)
    sq = pltpu.VMEM((2 * R_H, R_HD, R_HD), F32)
    return pl.pallas_call(
        body, name="ret_bwd", grid=(ns,),
        in_specs=[desc(0), desc(0), desc(2), desc(0),
                  pl.BlockSpec((R_H, cb, R_HD, R_HD), lambda n: (0, ns - 1 - n, 0, 0)),
                  asc(0), asc(0), asc(2), asc(0),
                  pl.BlockSpec((R_H, cb, R_HD, R_HD), lambda n: (0, n, 0, 0)),
                  big, big, big, big, big, big, vec, vec],
        out_specs=(desc(0), desc(0), desc(0), asc(0), asc(0), asc(0), vec),
        out_shape=(oshape,) * 6 + (jax.ShapeDtypeStruct((2 * R_H, 1, LANES), F32),),
        scratch_shapes=[sq, sq, sq, sq],
    )(rq, rk, pr, dry, pstf, rq, rk, pr, dry, pstb, m, mw, qd, qdw, kd, kdw, gdec, gw)


def _group_norm(ry):
    yn, rs = [], []
    for h in range(R_H):
        s = ry[:, h * R_HD:(h + 1) * R_HD]
        mu = jnp.mean(s, axis=-1, keepdims=True)
        cen = s - mu
        r = lax.rsqrt(jnp.mean(cen * cen, axis=-1, keepdims=True) + GN_EPS)
        yn.append(cen * r)
        rs.append(r)
    return yn, rs


def _merge_fwd(x, o, yf, yb, pr, pg, gain_r, wao, wro, wout):
    seq = x.shape[0]
    tm = _tile(seq, 256)

    def body(x_ref, o_ref, yf_ref, yb_ref, rg_ref, ga_ref, gr_ref, gn_ref, wao_ref, wro_ref, wout_ref,
             x1_ref, mg_ref, ri_ref):
        yn, _ = _group_norm(yf_ref[...] + yb_ref[...])
        rg = rg_ref[...]
        ret_in = jnp.concatenate(yn, axis=1) * gn_ref[...] * (rg * _sigmoid(rg))
        ri_ref[...] = _bf(ret_in)
        attn_out = _mm(o_ref[...], wao_ref[...])
        ret_out = _mm(ret_in, wro_ref[...])
        merged = _sigmoid(ga_ref[...]) * attn_out + _sigmoid(gr_ref[...]) * ret_out
        mg_ref[...] = _bf(merged)
        x1_ref[...] = x_ref[...] + _mm(merged, wout_ref[...])

    row = lambda w_, j=0: pl.BlockSpec((tm, w_), lambda i: (i, j))
    return pl.pallas_call(
        body, name="merge_fwd", grid=(seq // tm,),
        in_specs=[row(D), row(AQ_W), row(R_W), row(R_W), row(R_W, 3), row(D, 0), row(D, 1),
                  _full((1, R_W)), _full((AQ_W, D)), _full((R_W, D)), _full((D, D))],
        out_specs=(row(D), row(D), row(R_W)),
        out_shape=(jax.ShapeDtypeStruct((seq, D), F32), jax.ShapeDtypeStruct((seq, D), _MXU),
                   jax.ShapeDtypeStruct((seq, R_W), _MXU)),
    )(x, o, yf, yb, pr, pg, pg, gain_r, wao, wro, wout)


def _mlp_fwd(x1, gain, wup, wdown):
    seq = x1.shape[0]
    tm = _tile(seq, 1024)
    fc = 2048
    nfc = FF // fc

    def body(x_ref, g_ref, wu_ref, wd_ref, x2_ref, hm_sc, acc_sc):
        c = pl.program_id(1)

        @pl.when(c == 0)
        def _():
            n, _ = _rms(x_ref[...])
            hm_sc[...] = _bf(n * g_ref[...])
            acc_sc[...] = jnp.zeros_like(acc_sc)

        halves = (slice(0, fc // 2), slice(fc // 2, fc))
        ups = [jnp.maximum(_mm(hm_sc[...], wu_ref[:, hs]), 0.0) for hs in halves]
        acc_sc[...] += _mm(ups[0] * ups[0], wd_ref[halves[0], :]) + _mm(ups[1] * ups[1], wd_ref[halves[1], :])

        @pl.when(c == nfc - 1)
        def _():
            x2_ref[...] = x_ref[...] + acc_sc[...]

    return pl.pallas_call(
        body, name="mlp_fwd", grid=(seq // tm, nfc),
        in_specs=[pl.BlockSpec((tm, D), lambda i, c: (i, 0)), pl.BlockSpec((1, D), lambda i, c: (0, 0)),
                  pl.BlockSpec((D, fc), lambda i, c: (0, c)), pl.BlockSpec((fc, D), lambda i, c: (c, 0))],
        out_specs=pl.BlockSpec((tm, D), lambda i, c: (i, 0)),
        out_shape=jax.ShapeDtypeStruct((seq, D), F32),
        scratch_shapes=[pltpu.VMEM((tm, D), _MXU), pltpu.VMEM((tm, D), F32)],
    )(x1, gain, wup, wdown)


def _ple_loss(x2, p, tgt, g_ple, g_fin, wpg, wpgt, wple):
    seq = x2.shape[0]
    tm = _tile(seq, 256)

    def body(x2_ref, p_ref, t_ref, gp_ref, gf_ref, wpg_ref, wpgt_ref, wple_ref,
             dx2_ref, de_ref, dz_ref, hp_ref, loss_ref, dgf_ref, dgp_ref):
        @pl.when(pl.program_id(0) == 0)
        def _():
            loss_ref[...] = jnp.zeros_like(loss_ref)
            dgf_ref[...] = jnp.zeros_like(dgf_ref)
            dgp_ref[...] = jnp.zeros_like(dgp_ref)

        x2 = x2_ref[...]
        gp, gf = gp_ref[...], gf_ref[...]
        n2, r2 = _rms(x2)
        hp = _bf(n2 * gp)
        hp_ref[...] = hp
        gate = _sigmoid(_mm(hp, wpg_ref[...]))
        e = _mm(p_ref[...], wple_ref[...])
        x3 = x2 + gate * e
        n3, r3 = _rms(x3)
        diff = n3 * gf - t_ref[...]
        row_loss = jnp.mean(diff * diff, axis=-1, keepdims=True)
        loss_ref[...] += 0.5 * jnp.sum(row_loss, axis=0, keepdims=True)
        dy = diff * (1.0 / D)
        dgf_ref[...] += jnp.sum(dy * n3, axis=0, keepdims=True)
        dx3 = _rms_bwd(n3, r3, gf, dy)
        de_ref[...] = _bf(dx3 * gate)
        dz = dx3 * e * gate * (1.0 - gate)
        dz_ref[...] = _bf(dz)
        dhp = _mm(dz, wpgt_ref[...])
        dgp_ref[...] += jnp.sum(dhp * n2, axis=0, keepdims=True)
        dx2_ref[...] = dx3 + _rms_bwd(n2, r2, gp, dhp)

    row = lambda w_: pl.BlockSpec((tm, w_), lambda i: (i, 0))
    act = lambda dt: jax.ShapeDtypeStruct((seq, D), dt)
    return pl.pallas_call(
        body, name="ple_loss", grid=(seq // tm,),
        in_specs=[row(D), row(PLE), row(D), _full((1, D)), _full((1, D)),
                  _full((D, D)), _full((D, D)), _full((PLE, D))],
        out_specs=(row(D), row(D), row(D), row(D), _full((1, LANES)), _full((1, D)), _full((1, D))),
        out_shape=(act(F32), act(_MXU), act(_MXU), act(_MXU), jax.ShapeDtypeStruct((1, LANES), F32),
                   jax.ShapeDtypeStruct((1, D), F32), jax.ShapeDtypeStruct((1, D), F32)),
    )(x2, p, tgt, g_ple, g_fin, wpg, wpgt, wple)


def _mlp_bwd(x1, dx2, gain, wup, wdownt, wupt):
    seq = x1.shape[0]
    tm = _tile(seq, 512)
    fc = 2048
    nfc = FF // fc

    def body(x_ref, dx2_ref, g_ref, wu_ref, wdt_ref, wut_ref,
             dx1_ref, a_ref, du_ref, hm_ref, dg_ref, dhm_sc):
        i = pl.program_id(0)
        c = pl.program_id(1)

        @pl.when((i == 0) & (c == 0))
        def _():
            dg_ref[...] = jnp.zeros_like(dg_ref)

        @pl.when(c == 0)
        def _():
            n, _ = _rms(x_ref[...])
            hm_ref[...] = _bf(n * g_ref[...])
            dhm_sc[...] = jnp.zeros_like(dhm_sc)

        halves = (slice(0, fc // 2), slice(fc // 2, fc))
        ups = [jnp.maximum(_mm(hm_ref[...], wu_ref[:, hs]), 0.0) for hs in halves]
        das = [_mm(dx2_ref[...], wdt_ref[:, hs]) for hs in halves]
        part = None
        for u, da, hs in zip(ups, das, halves):
            a_ref[:, hs] = _bf(u * u)
            du = _bf(da * (2.0 * u))
            du_ref[:, hs] = du
            t = _mm(du, wut_ref[hs, :])
            part = t if part is None else part + t
        dhm_sc[...] += part

        @pl.when(c == nfc - 1)
        def _():
            n, r = _rms(x_ref[...])
            dhm = dhm_sc[...]
            dg_ref[...] += jnp.sum(dhm * n, axis=0, keepdims=True)
            dx1_ref[...] = dx2_ref[...] + _rms_bwd(n, r, g_ref[...], dhm)

    rowd = pl.BlockSpec((tm, D), lambda i, c: (i, 0))
    rowf = pl.BlockSpec((tm, fc), lambda i, c: (i, c))
    return pl.pallas_call(
        body, name="mlp_bwd", grid=(seq // tm, nfc),
        in_specs=[rowd, rowd, pl.BlockSpec((1, D), lambda i, c: (0, 0)),
                  pl.BlockSpec((D, fc), lambda i, c: (0, c)), pl.BlockSpec((D, fc), lambda i, c: (0, c)),
                  pl.BlockSpec((fc, D), lambda i, c: (c, 0))],
        out_specs=(rowd, rowf, rowf, rowd, pl.BlockSpec((1, D), lambda i, c: (0, 0))),
        out_shape=(jax.ShapeDtypeStruct((seq, D), F32), jax.ShapeDtypeStruct((seq, FF), _MXU),
                   jax.ShapeDtypeStruct((seq, FF), _MXU), jax.ShapeDtypeStruct((seq, D), _MXU),
                   jax.ShapeDtypeStruct((1, D), F32)),
        scratch_shapes=[pltpu.VMEM((tm, D), F32)],
    )(x1, dx2, gain, wup, wdownt, wupt)


def _merge_bwd(dx1, o, yf, yb, pr, pg, gain_r, wao, wro, woutt, waot, wrot, gpack):
    seq = dx1.shape[0]
    tm = _tile(seq, 256)
    nsteps = seq // tm

    def body(dx1_ref, o_ref, yf_ref, yb_ref, rg_ref, ga_ref, gr_ref, gn_ref, wao_ref, wro_ref,
             woutt_ref, waot_ref, wrot_ref, gpack_ref,
             dpg_ref, dao_ref, dro_ref, do_ref, dry_ref, drg_ref, dgn_ref, land_ref,
             send_sems, recv_sems, local_sem):
        start, finish = _exchange_steps(gpack_ref, land_ref, send_sems, recv_sems, local_sem, True)
        pl.when(pl.program_id(0) == 0)(start)

        @pl.when(pl.program_id(0) == 0)
        def _():
            dgn_ref[...] = jnp.zeros_like(dgn_ref)

        yn_l, rs_l = _group_norm(yf_ref[...] + yb_ref[...])
        yn = jnp.concatenate(yn_l, axis=1)
        rg = rg_ref[...]
        gn = gn_ref[...]
        sg = _sigmoid(rg)
        sil = rg * sg
        ret_in = yn * gn * sil
        attn_out = _mm(o_ref[...], wao_ref[...])
        ret_out = _mm(ret_in, wro_ref[...])
        sa = _sigmoid(ga_ref[...])
        sr = _sigmoid(gr_ref[...])
        dm = _mm(dx1_ref[...], woutt_ref[...])
        dpg_ref[:, 0:D] = _bf(dm * attn_out * sa * (1.0 - sa))
        dpg_ref[:, D:2 * D] = _bf(dm * ret_out * sr * (1.0 - sr))
        dao = _bf(dm * sa)
        dro = _bf(dm * sr)
        dao_ref[...] = dao
        dro_ref[...] = dro
        do_ref[...] = _mm(dao, waot_ref[...])
        dri = _mm(dro, wrot_ref[...])
        dgn_ref[...] += jnp.sum(dri * yn * sil, axis=0, keepdims=True)
        drg_ref[...] = _bf(dri * yn * gn * (sg * (1.0 + rg * (1.0 - sg))))
        dyn = dri * gn * sil
        dry = []
        for h in range(R_H):
            dh = dyn[:, h * R_HD:(h + 1) * R_HD]
            dry.append(rs_l[h] * (dh - jnp.mean(dh, axis=-1, keepdims=True)
                                  - yn_l[h] * jnp.mean(dh * yn_l[h], axis=-1, keepdims=True)))
        dry_ref[...] = jnp.concatenate(dry, axis=1)
        pl.when(pl.program_id(0) == nsteps - 1)(finish)

    row = lambda w_, j=0: pl.BlockSpec((tm, w_), lambda i: (i, j))
    return pl.pallas_call(
        body, name="merge_bwd", grid=(nsteps,),
        in_specs=[row(D), row(AQ_W), row(R_W), row(R_W), row(R_W, 3), row(D, 0), row(D, 1),
                  _full((1, R_W)), _full((AQ_W, D)), _full((R_W, D)), _full((D, D)),
                  _full((D, AQ_W)), _full((D, R_W)), _ANY],
        out_specs=(row(PG_W), row(D), row(D), row(AQ_W), row(R_W), row(R_W), _full((1, R_W)), _ANY),
        out_shape=(jax.ShapeDtypeStruct((seq, PG_W), _MXU), jax.ShapeDtypeStruct((seq, D), _MXU),
                   jax.ShapeDtypeStruct((seq, D), _MXU), jax.ShapeDtypeStruct((seq, AQ_W), F32),
                   jax.ShapeDtypeStruct((seq, R_W), F32), jax.ShapeDtypeStruct((seq, R_W), _MXU),
                   jax.ShapeDtypeStruct((1, R_W), F32), jax.ShapeDtypeStruct(gpack.shape, gpack.dtype)),
        scratch_shapes=list(_COMM_SCRATCH),
    )(dx1, o, yf, yb, pr, pg, pg, gain_r, wao, wro, woutt, waot, wrot, gpack)


def _qk_prep_bwd(pa, dqh, dk2, dv2, rdf, rdb, drg, gq, gk, seg, ca, sa, cr, sr):
    seq = pa.shape[0]
    tm = _tile(seq, 256)

    def body(pa_ref, dqh_ref, dk2_ref, dv2_ref, rdqf_ref, rdqb_ref, rdkf_ref, rdkb_ref, rdvf_ref, rdvb_ref,
             drg_ref, gq_ref, gk_ref, seg_ref, ca_ref, sa_ref, cr_ref, sr_ref,
             dpa_ref, dpr_ref, dgq_ref, dgk_ref):
        @pl.when(pl.program_id(0) == 0)
        def _():
            dgq_ref[...] = jnp.zeros_like(dgq_ref)
            dgk_ref[...] = jnp.zeros_like(dgk_ref)

        ca_, sa_ = ca_ref[...], sa_ref[...]

        def norm_bwd(raw, gain, dy, segm, dg_ref):
            msq = _seg_mean(raw * raw, segm)
            r = lax.rsqrt(msq + EPS)
            n = raw * r
            dg_ref[...] += jnp.sum(dy * n, axis=0, keepdims=True)
            dn = dy * gain
            return r * (dn - n * _seg_mean(dn * n, segm))

        dqn = _rope(dqh_ref[...] * (A_HD ** -0.5), _cat(ca_, 4), -_cat(sa_, 4), A_HD // 2)
        dpa_ref[:, 0:AQ_W] = _bf(norm_bwd(pa_ref[:, 0:AQ_W], gq_ref[...], dqn, seg_ref[...], dgq_ref))
        dkn = _rope(dk2_ref[...], ca_, -sa_, A_HD // 2)
        dpa_ref[:, AQ_W:AQ_W + AKV_W] = _bf(norm_bwd(pa_ref[:, AQ_W:AQ_W + AKV_W], gk_ref[...], dkn,
                                                     seg_ref[0:AKV_W, 0:AKV_W], dgk_ref))
        dpa_ref[:, AQ_W + AKV_W:PA_W] = _bf(dv2_ref[...])
        cr_, sr_ = _cat(cr_ref[...], 4), -_cat(sr_ref[...], 4)
        dpr_ref[:, 0:R_W] = _bf(_rope((rdqf_ref[...] + rdqb_ref[...]) * (R_HD ** -0.5), cr_, sr_, R_HD // 2))
        dpr_ref[:, R_W:2 * R_W] = _bf(_rope(rdkf_ref[...] + rdkb_ref[...], cr_, sr_, R_HD // 2))
        dpr_ref[:, 2 * R_W:3 * R_W] = _bf(rdvf_ref[...] + rdvb_ref[...])
        dpr_ref[:, 3 * R_W:4 * R_W] = drg_ref[...]

    row = lambda w_: pl.BlockSpec((tm, w_), lambda i: (i, 0))
    return pl.pallas_call(
        body, name="qk_prep_bwd", grid=(seq // tm,),
        in_specs=[row(PA_W), row(AQ_W), row(AKV_W), row(AKV_W), row(R_W), row(R_W), row(R_W), row(R_W),
                  row(R_W), row(R_W), row(R_W), _full((1, AQ_W)), _full((1, AKV_W)), _full((AQ_W, AQ_W)),
                  row(LANES), row(LANES), row(LANES), row(LANES)],
        out_specs=(row(PA_W), row(PR_W), _full((1, AQ_W)), _full((1, AKV_W))),
        out_shape=(jax.ShapeDtypeStruct((seq, PA_W), _MXU), jax.ShapeDtypeStruct((seq, PR_W), _MXU),
                   jax.ShapeDtypeStruct((1, AQ_W), F32), jax.ShapeDtypeStruct((1, AKV_W), F32)),
    )(pa, dqh, dk2, dv2, rdf[0], rdb[0], rdf[1], rdb[1], rdf[2], rdb[2], drg, gq, gk, seg, ca, sa, cr, sr)


def _in_proj_bwd(x, dx1, gain, dpa, dpr, dpg, wint, packs):
    seq = x.shape[0]
    tm = _tile(seq, 256)
    nsteps = seq // tm
    npk = len(packs)

    def body(x_ref, dx1_ref, g_ref, dpa_ref, dpr_ref, dpg_ref, wt_ref, *rest):
        pack_refs, (dx_ref, dg_ref), land_refs = rest[:npk], rest[npk:npk + 2], rest[npk + 2:2 * npk + 2]
        send_sems, recv_sems, local_sems = rest[2 * npk + 2:]
        steps = [_exchange_steps(pack_refs[t], land_refs[t], send_sems, recv_sems, local_sems.at[t], True, base=7 * t)
                 for t in range(npk)]

        @pl.when(pl.program_id(0) == 0)
        def _():
            for start, _ in steps:
                start()
            dg_ref[...] = jnp.zeros_like(dg_ref)

        dh = (_mm(dpa_ref[...], wt_ref[0:PA_W, :]) + _mm(dpr_ref[...], wt_ref[PA_W:PA_W + PR_W, :])
              + _mm(dpg_ref[...], wt_ref[PA_W + PR_W:IN_W, :]))
        n, r = _rms(x_ref[...])
        dg_ref[...] += jnp.sum(dh * n, axis=0, keepdims=True)
        dx_ref[...] = dx1_ref[...] + _rms_bwd(n, r, g_ref[...], dh)

        @pl.when(pl.program_id(0) == nsteps - 1)
        def _():
            for _, finish in steps:
                finish()

    row = lambda w_: pl.BlockSpec((tm, w_), lambda i: (i, 0))
    return pl.pallas_call(
        body, name="in_proj_bwd", grid=(nsteps,),
        in_specs=[row(D), row(D), _full((1, D)), row(PA_W), row(PR_W), row(PG_W), _full((IN_W, D))] + [_ANY] * npk,
        out_specs=(row(D), _full((1, D))) + (_ANY,) * npk,
        out_shape=(jax.ShapeDtypeStruct((seq, D), F32), jax.ShapeDtypeStruct((1, D), F32))
        + tuple(jax.ShapeDtypeStruct(g.shape, g.dtype) for g in packs),
        scratch_shapes=[pltpu.SemaphoreType.DMA((7 * npk,)), pltpu.SemaphoreType.DMA((7 * npk,)),
                        pltpu.SemaphoreType.DMA((npk,))],
    )(x, dx1, gain, dpa, dpr, dpg, wint, *packs)


def _wgrad(a, b, name):
    seq, m = a.shape
    n = b.shape[1]
    tm, tn, ts = _tile(m, 1024), _tile(n, 1024), _tile(seq, 2048)
    ns = seq // ts

    def body(a_ref, b_ref, o_ref):
        @pl.when(pl.program_id(2) == 0)
        def _():
            o_ref[...] = jnp.zeros_like(o_ref)

        o_ref[...] += _mm_tn(a_ref[...], b_ref[...])

    return pl.pallas_call(
        body, name=name, grid=(m // tm, n // tn, ns),
        in_specs=[pl.BlockSpec((ts, tm), lambda i, j, s: (s, i)), pl.BlockSpec((ts, tn), lambda i, j, s: (s, j))],
        out_specs=pl.BlockSpec((tm, tn), lambda i, j, s: (i, j)),
        out_shape=jax.ShapeDtypeStruct((m, n), F32),
    )(a, b)


def _adamw_math(w, g, m, v):
    m = B1 * m + (1.0 - B1) * g
    v = B2 * v + (1.0 - B2) * (g * g)
    m_hat = m / (1.0 - B1 ** STEP)
    v_hat = v / (1.0 - B2 ** STEP)
    delta = -LR * (m_hat / (jnp.sqrt(v_hat) + ADAM_EPS) + WD * w)
    return delta, m, v


def _adamw_big(land, own, w, m, v, name):
    rws, cols = w.shape
    tr = next(t for t in range(min(rws, 288) // 16 * 16, 0, -16) if rws % t == 0)

    def body(l_ref, o_ref, w_ref, m_ref, v_ref, g_ref, d_ref, nm_ref, nv_ref):
        x, y, c = _mesh_pos()
        me = 4 * x + 2 * y + c
        g = o_ref[...]
        for j in range(N_DEV):
            g = g + jnp.where(me == j, 0.0, l_ref[j].astype(F32))
        g_ref[...] = g
        d_ref[...], nm_ref[...], nv_ref[...] = _adamw_math(w_ref[...], g, m_ref[...], v_ref[...])

    row = pl.BlockSpec((tr, cols), lambda i: (i, 0))
    shp = jax.ShapeDtypeStruct((rws, cols), F32)
    return pl.pallas_call(
        body, name=name, grid=(rws // tr,),
        in_specs=[pl.BlockSpec((N_DEV, tr, cols), lambda i: (0, i, 0)), row, row, row, row],
        out_specs=(row, row, row, row), out_shape=(shp, shp, shp, shp),
    )(land, own, w, m, v)


def _adamw_small(sland, w, m, v):
    def body(l_ref, w_ref, m_ref, v_ref, g_ref, d_ref, nm_ref, nv_ref, loss_ref):
        s = l_ref[0]
        for j in range(1, N_DEV):
            s = s + l_ref[j]
        w = w_ref[...]
        gq = s[8:9]
        for h in range(1, A_H):
            gq = gq + s[8 + h:9 + h]
        gk = s[16:17] + s[17:18]
        gdec = s[5:6] * _sigmoid(-w[5:6])
        g = jnp.concatenate([s[0:5], gdec, gq, gk], axis=0)
        g_ref[...] = g
        d_ref[...], nm_ref[...], nv_ref[...] = _adamw_math(w, g, m_ref[...], v_ref[...])
        loss_ref[...] = s[6:7, 0:LANES]

    shp = jax.ShapeDtypeStruct((8, PACK_COLS), F32)
    return pl.pallas_call(
        body, name="adamw_small",
        out_shape=(shp, shp, shp, shp, jax.ShapeDtypeStruct((1, LANES), F32)),
    )(sland, w, m, v)


_BIG = (("w_attn_o", AQ_W, D, 1), ("w_ret_o", R_W, D, 1), ("w_out", D, D, 0),
        ("w_up", D, FF, 1), ("w_down", FF, D, 0), ("w_ple_gate", D, D, 0), ("w_ple", PLE, D, 1))
_LATE = _BIG[:3]
_EARLY = _BIG[3:]
IN_SHARD = IN_W // N_DEV
_SMALL = ("mix_norm", "mlp_norm", "ple_norm", "final_norm", "ret_norm_gain", "ret_decay_logit",
          "attn_q_norm", "attn_k_norm")


def _shard_shape(rows, cols, axis):
    return (rows // N_DEV, cols) if axis == 0 else (rows, cols // N_DEV)


def _pack_shards(shards):
    flat = jnp.concatenate([s.reshape(-1) for s in shards])
    return flat.reshape(-1, PACK_COLS)


def _unpack_gathered(gathered):
    flat = gathered.reshape(N_DEV, -1)
    out, off = {}, 0
    for name, rows, cols, axis in _BIG:
        sr, sc = _shard_shape(rows, cols, axis)
        blk = flat[:, off:off + sr * sc].reshape(N_DEV, sr, sc)
        off += sr * sc
        out[name] = blk.reshape(rows, cols) if axis == 0 else blk.transpose(1, 0, 2).reshape(rows, cols)
    return out


def _pack_full_grads(grads, group):
    parts = []
    for name, rows, cols, axis in group:
        sr, sc = _shard_shape(rows, cols, axis)
        g = grads[name]
        blk = g.reshape(N_DEV, sr, sc) if axis == 0 else g.reshape(rows, N_DEV, sc).transpose(1, 0, 2)
        parts.append(blk.reshape(N_DEV, -1))
    flat = jnp.concatenate(parts, axis=1)
    return flat.reshape(N_DEV, -1, PACK_COLS)


def _unpack_shard(packed, group):
    flat = packed.reshape(-1)
    out, off = {}, 0
    for name, rows, cols, axis in group:
        sr, sc = _shard_shape(rows, cols, axis)
        out[name] = flat[off:off + sr * sc].reshape(1, sr, sc)
        off += sr * sc
    return out


def _pack_small(vals):
    rows = [jnp.pad(vals[n].reshape(-1), (0, PACK_COLS - vals[n].size)) for n in _SMALL]
    return jnp.stack(rows)


def _unpack_small(packed, like):
    return {n: packed[i, :like[n].size].reshape(like[n].shape) for i, n in enumerate(_SMALL)}


def _row(v):
    return jnp.pad(v.reshape(-1), (0, PACK_COLS - v.size))


def kernel(x, p, mix_norm, w_in, attn_q_norm, attn_k_norm, ret_decay_logit, ret_norm_gain, w_attn_o, w_ret_o, w_out, mlp_norm, w_up, w_down, ple_norm, w_ple_gate, w_ple, final_norm, loss_target, m_mix_norm, m_w_in, m_attn_q_norm, m_attn_k_norm, m_ret_decay_logit, m_ret_norm_gain, m_w_attn_o, m_w_ret_o, m_w_out, m_mlp_norm, m_w_up, m_w_down, m_ple_norm, m_w_ple_gate, m_w_ple, m_final_norm, v_mix_norm, v_w_in, v_attn_q_norm, v_attn_k_norm, v_ret_decay_logit, v_ret_norm_gain, v_w_attn_o, v_w_ret_o, v_w_out, v_mlp_norm, v_w_up, v_w_down, v_ple_norm, v_w_ple_gate, v_w_ple, v_final_norm):
    args = dict(locals())
    seq = x.shape[1]
    xs = x[0]
    ps = p[0, 0]
    tgt = loss_target[0]

    big_names = [b[0] for b in _BIG]
    wshard = _pack_shards([args[n] for n in big_names])
    win = _all_gather(w_in[0].astype(_MXU)).transpose(1, 0, 2).reshape(D, IN_W)

    g_mix, g_mlp, g_ple = mix_norm, mlp_norm, ple_norm
    g_fin = final_norm.reshape(1, D)
    gq = jnp.tile(attn_q_norm, (1, A_H))
    gk = jnp.tile(attn_k_norm, (1, A_KV))
    seg = _seg_mean_matrix()
    ca, sa, cr, sr = _rope_tables(seq)

    pa, pr, pg, h, rest_g = _in_proj(xs, g_mix, win, wshard.astype(_MXU))
    wfull = _unpack_gathered(rest_g)
    wao, wro, wout = wfull["w_attn_o"], wfull["w_ret_o"], wfull["w_out"]
    wup, wdown, wpg, wple = wfull["w_up"], wfull["w_down"], wfull["w_ple_gate"], wfull["w_ple"]
    qh, kh, vh, rqh, rkh = _qk_prep(pa, pr, gq, gk, seg, ca, sa, cr, sr)

    tq = _tile(seq, 128)
    tk = _tile(seq // 4, 2048)
    qt8 = qh.reshape(seq, A_H, A_HD).transpose(1, 2, 0)
    vta = jnp.stack([jnp.concatenate([_chunk_t(vh[:, g * A_HD:(g + 1) * A_HD], tk),
                                      jnp.ones((seq // tk, 16, tk), _MXU)], axis=1) for g in range(A_KV)])
    ot, lse = _attn_fwd(qt8, kh, vta, tq, tk)
    o = _heads_to_rows(ot)

    zb = jnp.broadcast_to(ret_decay_logit.reshape(2 * R_H, 1, 1), (2 * R_H, 1, LANES))
    tm_, tmw, tqd, tqdw, tkd, tkdw, tg, tgw = _ret_tables(zb)
    cb = _tile(seq // CHUNK, 4)
    yf, yb, pstf, pstb = _ret_fwd(rqh, rkh, pr, tm_, tqd, tkd, tg, cb)

    x1, merged, ret_in = _merge_fwd(xs, o, yf, yb, pr, pg, ret_norm_gain, wao, wro, wout)
    x2 = _mlp_fwd(x1, g_mlp, wup, wdown)

    dx2, de, dz, hp, loss_p, dg_fin, dg_ple = _ple_loss(x2, ps, tgt, g_ple, g_fin, wpg, wpg.T, wple)
    dx1, act, du, hm, dg_mlp = _mlp_bwd(x1, dx2, g_mlp, wup, wdown.T, wup.T)
    me = 4 * lax.axis_index("x") + 2 * lax.axis_index("y") + lax.axis_index("c")
    gpack_e = _pack_full_grads({"w_up": _wgrad(hm, du, "wgrad_up"), "w_down": _wgrad(act, dx2, "wgrad_down"),
                                "w_ple_gate": _wgrad(hp, dz, "wgrad_ple_gate"), "w_ple": _wgrad(ps, de, "wgrad_ple")},
                               _EARLY)
    dpg, dao, dro, do, dry, drg, dg_gn, land_e = _merge_bwd(dx1, o, yf, yb, pr, pg, ret_norm_gain, wao, wro,
                                                            wout.T, wao.T, wro.T, _bf(gpack_e))
    *rd, dlam = _ret_bwd(rqh, rkh, pr, dry, pstf, pstb, tm_, tmw, tqd, tqdw, tkd, tkdw, tg, tgw, cb)

    ksplit = 1
    tkb = _tile(seq // 4, 512)
    dot_ = do.reshape(seq, A_H, A_HD).transpose(1, 2, 0)
    dkt, dvt, dqt = _attn_bwd(qt8, ot, dot_, lse, kh, vh, _chunk_t(kh, tkb), tq, tkb, ksplit)
    dqh = _heads_to_rows(dqt[0] if ksplit == 1 else jnp.sum(dqt, axis=0))
    dpa, dpr, dg_q, dg_k = _qk_prep_bwd(pa, dqh, _chunks_to_rows(dkt), _chunks_to_rows(dvt), rd[0:3], rd[3:6], drg, gq, gk, seg, ca, sa, cr, sr)
    wg_in = jnp.concatenate([_wgrad(h, dpa, "wgrad_in_a"), _wgrad(h, dpr, "wgrad_in_r"),
                             _wgrad(h, dpg, "wgrad_in_g")], axis=1)
    gpack_l = _pack_full_grads({"w_attn_o": _wgrad(o, dao, "wgrad_attn_o"),
                                "w_ret_o": _wgrad(ret_in, dro, "wgrad_ret_o"),
                                "w_out": _wgrad(merged, dx1, "wgrad_out")}, _LATE)
    gpack_in = wg_in.reshape(D, N_DEV, IN_SHARD).transpose(1, 0, 2)
    grad_x, dg_mix, land_in, land_l = _in_proj_bwd(xs, dx1, g_mix, dpa, dpr, dpg, win.T,
                                                   [_bf(gpack_in), _bf(gpack_l)])
    small = jnp.stack(
        [_row(dg_mix), _row(dg_mlp), _row(dg_ple), _row(dg_fin), _row(dg_gn), _row(dlam[:, 0, 0]),
         _row(loss_p[0, 0:1]), jnp.zeros((PACK_COLS,), F32)]
        + [_row(dg_q[0, hh * A_HD:(hh + 1) * A_HD]) for hh in range(A_H)]
        + [_row(dg_k[0, hh * A_HD:(hh + 1) * A_HD]) for hh in range(A_KV)]
        + [jnp.zeros((PACK_COLS,), F32)] * (SMALL_ROWS - 18))

    own = lambda pack: lax.dynamic_index_in_dim(pack, me, axis=0, keepdims=False)
    (sland,) = _exchange_grads([small[None]])
    in_sh = _adamw_big(land_in, own(gpack_in), w_in[0], m_w_in[0], v_w_in[0], "adamw_w_in")
    group_sh = []
    for group, land, pack, name in ((_EARLY, land_e, gpack_e, "adamw_early"), (_LATE, land_l, gpack_l, "adamw_late")):
        packed = lambda pre: _pack_shards([args[pre + g[0]] for g in group])
        group_sh.append((group, _adamw_big(land, own(pack), packed(""), packed("m_"), packed("v_"), name)))
    g_sm, d_sm, m_sm, v_sm, loss_row = _adamw_small(
        sland, _pack_small({n: args[n] for n in _SMALL}), _pack_small({n: args["m_" + n] for n in _SMALL}),
        _pack_small({n: args["v_" + n] for n in _SMALL}))

    names = ["mix_norm", "w_in", "attn_q_norm", "attn_k_norm", "ret_decay_logit", "ret_norm_gain", "w_attn_o",
             "w_ret_o", "w_out", "mlp_norm", "w_up", "w_down", "ple_norm", "w_ple_gate", "w_ple", "final_norm"]
    like = {n: args[n] for n in _SMALL}
    outs = [loss_row[0, 0], grad_x[None]]
    for kind, sm in enumerate((g_sm, d_sm, m_sm, v_sm)):
        table = {**_unpack_small(sm, like), "w_in": in_sh[kind][None]}
        for group, res in group_sh:
            table.update(_unpack_shard(res[kind], group))
        outs += [table[n] for n in names]
    return tuple(outs)
```

```python
import functools

import jax
import jax.numpy as jnp
from jax import lax
from jax.experimental import pallas as pl
from jax.experimental.pallas import tpu as pltpu

F32 = jnp.float32
_MXU = jnp.bfloat16

D = 1024
PLE = 256
GRID_W = 64
A_HD = 64
A_H = 8
A_KV = 2
A_G = A_H // A_KV
AQ_W = A_H * A_HD
AKV_W = A_KV * A_HD
R_HD = 128
R_H = 4
R_W = R_H * R_HD
IN_W = AQ_W + 2 * AKV_W + 4 * R_W + 2 * D
PA_W = AQ_W + 2 * AKV_W
PR_W = 4 * R_W
PG_W = 2 * D
FF = 4 * D
CHUNK = 128
ROPE_THETA = 10000.0
EPS = 1e-6
GN_EPS = 1e-5
N_DEV = 8

LR, B1, B2, ADAM_EPS, WD, STEP = 0.001, 0.9, 0.999, 1e-08, 0.01, 10

LANES = 128
PACK_COLS = 1024
SMALL_ROWS = 24


def _tile(n, pref):
    t = min(n, pref)
    assert n % t == 0, (n, t)
    return t


def _bf(a):
    return a.astype(_MXU)


def _mm(a, b):
    return jnp.dot(_bf(a), _bf(b), preferred_element_type=F32)


def _mm_nt(a, b):
    return lax.dot_general(_bf(a), _bf(b), (((1,), (1,)), ((), ())), preferred_element_type=F32)


def _mm_tn(a, b):
    return lax.dot_general(_bf(a), _bf(b), (((0,), (0,)), ((), ())), preferred_element_type=F32)


def _seg_mean(v, segm):
    hi = _bf(v)
    lo = _bf(v - hi.astype(F32))
    return _mm(hi, segm) + _mm(lo, segm)


def _sigmoid(z):
    return 1.0 / (1.0 + jnp.exp(-z))


def _rms(x):
    r = lax.rsqrt(jnp.mean(x * x, axis=-1, keepdims=True) + EPS)
    return x * r, r


def _rms_bwd(n, r, gain, dy):
    dn = dy * gain
    return r * (dn - n * jnp.mean(dn * n, axis=-1, keepdims=True))


def _swap_halves(x, half):
    n = x.shape[-1]
    lane = lax.broadcasted_iota(jnp.int32, x.shape, x.ndim - 1)
    first = (lane % (2 * half)) < half
    return jnp.where(first, pltpu.roll(x, n - half, axis=1), pltpu.roll(x, half, axis=1))


def _rope(x, cos, sin, half):
    return x * cos + _swap_halves(x, half) * sin


def _cat(t, reps):
    return jnp.concatenate([t] * reps, axis=1)


def _full(shape):
    nd = len(shape)
    return pl.BlockSpec(shape, lambda *_: (0,) * nd)


def _rope_tables(seq):
    def tab(head_dim):
        n_axis = head_dim // 4
        freqs = ROPE_THETA ** (-jnp.arange(n_axis, dtype=F32) / n_axis)
        rows = seq // GRID_W
        row = jnp.repeat(jnp.arange(rows, dtype=F32), GRID_W)
        col = jnp.tile(jnp.arange(GRID_W, dtype=F32), rows)
        ang = jnp.concatenate([row[:, None] * freqs, col[:, None] * freqs], axis=-1)
        c, s = jnp.cos(ang), jnp.sin(ang)
        return jnp.concatenate([c, c], axis=-1), jnp.concatenate([-s, s], axis=-1)
    ca, sa = tab(A_HD)
    cr, sr = tab(R_HD)
    return jnp.tile(ca, (1, 2)), jnp.tile(sa, (1, 2)), cr, sr


def _seg_mean_matrix():
    i = jnp.arange(AQ_W) // A_HD
    return (i[:, None] == i[None, :]).astype(F32) / A_HD


def _mesh_pos():
    return lax.axis_index("x"), lax.axis_index("y"), lax.axis_index("c")


def _gather_steps(x_ref, out_ref, send_sems, recv_sems, local_sem, base=0):
    x, y, c = _mesh_pos()
    me, sibling = (x, y, c), (x, y, 1 - c)
    chips = [(1 - x, y), (x, 1 - y), (1 - x, 1 - y)]

    def slot(px, py, pc):
        return out_ref.at[4 * px + 2 * py + pc]

    def copy(k, block, to, src=None):
        return pltpu.make_async_remote_copy(
            src_ref=slot(*block) if src is None else src, dst_ref=slot(*block),
            send_sem=send_sems.at[base + k], recv_sem=recv_sems.at[base + k],
            device_id=to, device_id_type=pl.DeviceIdType.MESH)

    mine = pltpu.make_async_copy(x_ref, slot(*me), local_sem)
    first = [copy(0, me, sibling, src=x_ref)]
    first += [copy(1 + j, me, (*chip, c), src=x_ref) for j, chip in enumerate(chips)]
    passed = [copy(4 + j, (*chip, c), sibling) for j, chip in enumerate(chips)]

    def start():
        mine.start()
        for cp in first:
            cp.start()

    def finish():
        for j, chip in enumerate(chips):
            copy(1 + j, (*chip, c), me).wait_recv()
            passed[j].start()
        copy(0, sibling, me).wait_recv()
        for j, chip in enumerate(chips):
            copy(4 + j, (*chip, 1 - c), me).wait_recv()
        for cp in first + passed:
            cp.wait_send()
        mine.wait()

    return start, finish


def _exchange_steps(g_ref, land_ref, send_sems, recv_sems, local_sem, per_device, base=0):
    x, y, c = _mesh_pos()
    me = 4 * x + 2 * y + c

    def row(j):
        return g_ref.at[j if per_device else 0]

    def peer(k):
        p = (x ^ ((k >> 2) & 1), y ^ ((k >> 1) & 1), c ^ (k & 1))
        return p, 4 * p[0] + 2 * p[1] + p[2]

    def copy(k, src, dst):
        return pltpu.make_async_remote_copy(
            src_ref=src, dst_ref=dst, send_sem=send_sems.at[base + k - 1], recv_sem=recv_sems.at[base + k - 1],
            device_id=peer(k)[0], device_id_type=pl.DeviceIdType.MESH)

    own = pltpu.make_async_copy(row(me), land_ref.at[me], local_sem)
    sends = [copy(k, row(peer(k)[1]), land_ref.at[me]) for k in range(1, N_DEV)]

    def start():
        own.start()
        for cp in sends:
            cp.start()

    def finish():
        for k in range(1, N_DEV):
            copy(k, row(me), land_ref.at[peer(k)[1]]).wait_recv()
        for cp in sends:
            cp.wait_send()
        own.wait()

    return start, finish


_COMM_SCRATCH = [pltpu.SemaphoreType.DMA((7,)), pltpu.SemaphoreType.DMA((7,)), pltpu.SemaphoreType.DMA]
_ANY = pl.BlockSpec(memory_space=pl.ANY)


def _all_gather(shard):
    def body(x_ref, out_ref, send_sems, recv_sems, local_sem):
        start, finish = _gather_steps(x_ref, out_ref, send_sems, recv_sems, local_sem)
        start()
        finish()

    return pl.pallas_call(
        body, name="all_gather_weights", out_shape=jax.ShapeDtypeStruct((N_DEV,) + shard.shape, shard.dtype),
        in_specs=[_ANY], out_specs=_ANY, scratch_shapes=list(_COMM_SCRATCH),
    )(shard)


def _exchange_grads(packs):
    n = len(packs)

    def body(*refs):
        g_refs, land_refs = refs[:n], refs[n:2 * n]
        send_sems, recv_sems, local_sems = refs[2 * n:]
        steps = [_exchange_steps(g_refs[t], land_refs[t], send_sems, recv_sems, local_sems.at[t],
                                 packs[t].shape[0] == N_DEV, base=7 * t) for t in range(n)]
        for start, _ in steps:
            start()
        for _, finish in steps:
            finish()

    return pl.pallas_call(
        body, name="exchange_grads",
        out_shape=tuple(jax.ShapeDtypeStruct((N_DEV,) + g.shape[1:], g.dtype) for g in packs),
        in_specs=[_ANY] * n, out_specs=(_ANY,) * n,
        scratch_shapes=[pltpu.SemaphoreType.DMA((7 * n,)), pltpu.SemaphoreType.DMA((7 * n,)),
                        pltpu.SemaphoreType.DMA((n,))],
    )(*packs)


def _in_proj(x, gain, w, rest):
    seq = x.shape[0]
    tm = _tile(seq, 256)
    nsteps = seq // tm

    def body(x_ref, g_ref, w_ref, rest_ref, pa_ref, pr_ref, pg_ref, h_ref, gath_ref, send_sems, recv_sems, local_sem):
        start, finish = _gather_steps(rest_ref, gath_ref, send_sems, recv_sems, local_sem)
        pl.when(pl.program_id(0) == 0)(start)
        n, _ = _rms(x_ref[...])
        h = _bf(n * g_ref[...])
        h_ref[...] = h
        pa_ref[...] = _mm(h, w_ref[:, 0:PA_W])
        pr_ref[...] = _mm(h, w_ref[:, PA_W:PA_W + PR_W])
        pg_ref[...] = _mm(h, w_ref[:, PA_W + PR_W:IN_W])
        pl.when(pl.program_id(0) == nsteps - 1)(finish)

    row = lambda w_: pl.BlockSpec((tm, w_), lambda i: (i, 0))
    return pl.pallas_call(
        body, name="in_proj", grid=(nsteps,),
        in_specs=[row(D), _full((1, D)), _full((D, IN_W)), _ANY],
        out_specs=(row(PA_W), row(PR_W), row(PG_W), row(D), _ANY),
        out_shape=(jax.ShapeDtypeStruct((seq, PA_W), F32), jax.ShapeDtypeStruct((seq, PR_W), F32),
                   jax.ShapeDtypeStruct((seq, PG_W), F32), jax.ShapeDtypeStruct((seq, D), _MXU),
                   jax.ShapeDtypeStruct((N_DEV,) + rest.shape, rest.dtype)),
        scratch_shapes=list(_COMM_SCRATCH),
    )(x, gain, w, rest)


def _qk_prep(pa, pr, gq, gk, seg, ca, sa, cr, sr):
    seq = pa.shape[0]
    tm = _tile(seq, 512)

    def body(pa_ref, pr_ref, gq_ref, gk_ref, seg_ref, ca_ref, sa_ref, cr_ref, sr_ref,
             qh_ref, kh_ref, v_ref, rq_ref, rk_ref):
        q = pa_ref[:, 0:AQ_W]
        k = pa_ref[:, AQ_W:AQ_W + AKV_W]
        v_ref[...] = _bf(pa_ref[:, AQ_W + AKV_W:PA_W])
        ca_, sa_ = ca_ref[...], sa_ref[...]
        msq = _seg_mean(q * q, seg_ref[...])
        qn = q * lax.rsqrt(msq + EPS) * gq_ref[...]
        qh_ref[...] = _bf(_rope(qn, _cat(ca_, 4), _cat(sa_, 4), A_HD // 2) * (A_HD ** -0.5))
        msk = _seg_mean(k * k, seg_ref[0:AKV_W, 0:AKV_W])
        kn = k * lax.rsqrt(msk + EPS) * gk_ref[...]
        kh_ref[...] = _bf(_rope(kn, ca_, sa_, A_HD // 2))
        cr_, sr_ = _cat(cr_ref[...], 4), _cat(sr_ref[...], 4)
        rq_ref[...] = _rope(pr_ref[:, 0:R_W], cr_, sr_, R_HD // 2) * (R_HD ** -0.5)
        rk_ref[...] = _rope(pr_ref[:, R_W:2 * R_W], cr_, sr_, R_HD // 2)

    row = lambda w_: pl.BlockSpec((tm, w_), lambda i: (i, 0))
    return pl.pallas_call(
        body, name="qk_prep", grid=(seq // tm,),
        in_specs=[row(PA_W), row(2 * R_W), _full((1, AQ_W)), _full((1, AKV_W)), _full((AQ_W, AQ_W)),
                  row(LANES), row(LANES), row(LANES), row(LANES)],
        out_specs=(row(AQ_W), row(AKV_W), row(AKV_W), row(R_W), row(R_W)),
        out_shape=(jax.ShapeDtypeStruct((seq, AQ_W), _MXU), jax.ShapeDtypeStruct((seq, AKV_W), _MXU),
                   jax.ShapeDtypeStruct((seq, AKV_W), _MXU), jax.ShapeDtypeStruct((seq, R_W), F32),
                   jax.ShapeDtypeStruct((seq, R_W), F32)),
    )(pa, pr, gq, gk, seg, ca, sa, cr, sr)


def _chunk_t(a, tk):
    seq = a.shape[0]
    return a.reshape(seq // tk, tk, a.shape[1]).transpose(0, 2, 1)


def _heads_to_rows(t):
    return t.transpose(2, 0, 1).reshape(t.shape[2], AQ_W)


def _attn_fwd(qt8, k2, vta, tq, tk):
    seq = k2.shape[0]
    nck = seq // tk
    rows = A_G * tq
    vrows = vta.shape[2]
    rb = _tile(tk, 256)
    assert nck % 2 == 0, nck

    def body(qt_ref, k_ref, vt_ref, o_ref, lse_ref, m_sc, acc_sc, qtp_sc, s_a, s_b, p_a, p_b, al_a, al_b):
        g = pl.program_id(0)
        qtp_sc[...] = jnp.zeros_like(qtp_sc)
        qtp_sc[pl.ds(pl.multiple_of(g * A_HD, A_HD), A_HD), :] = jnp.concatenate(
            [qt_ref[a] for a in range(A_G)], axis=1)
        m_sc[...] = jnp.full((1, rows), -jnp.inf, F32)
        acc_sc[...] = jnp.zeros_like(acc_sc)

        def scores(c):
            kc = k_ref[pl.ds(pl.multiple_of(c * tk, tk), tk), :]
            return _mm(kc, qtp_sc[...])

        def stage(c, s_cur, s_nxt, p_cur, p_prv, al_cur, al_prv, first=False, last=False):
            if not last:
                s_nxt[...] = scores(c + 1)
            if not first:
                acc_sc[...] = al_prv[...] * acc_sc[...] + _mm(vt_ref[0, c - 1], p_prv[...])
            m_old = m_sc[...]
            mx = None
            for r in range(0, tk, rb):
                bm = jnp.max(s_cur[r:r + rb, :].reshape(rb // 8, 8, rows), axis=0)
                mx = bm if mx is None else jnp.maximum(mx, bm)
            m_new = jnp.maximum(m_old, jnp.max(mx, axis=0, keepdims=True))
            for r in range(0, tk, rb):
                p_cur[r:r + rb, :] = _bf(jnp.exp(s_cur[r:r + rb, :] - m_new))
            al_cur[...] = jnp.exp(m_old - m_new)
            m_sc[...] = m_new

        s_a[...] = scores(0)
        stage(0, s_a, s_b, p_a, p_b, al_a, al_b, first=True)

        def pair(j, carry):
            stage(2 * j + 1, s_b, s_a, p_b, p_a, al_b, al_a)
            stage(2 * j + 2, s_a, s_b, p_a, p_b, al_a, al_b)
            return carry

        lax.fori_loop(0, nck // 2 - 1, pair, 0)
        stage(nck - 1, s_b, s_a, p_b, p_a, al_b, al_a, last=True)
        acc = al_b[...] * acc_sc[...] + _mm(vt_ref[0, nck - 1], p_b[...])
        l = acc[A_HD:A_HD + 1, :]
        lse = m_sc[...] + jnp.log(l)
        out = acc[0:A_HD, :] * (1.0 / l)
        for a in range(A_G):
            o_ref[a] = out[:, a * tq:(a + 1) * tq]
            lse_ref[a] = lse[:, a * tq:(a + 1) * tq]

    return pl.pallas_call(
        body, name="attn_fwd", grid=(A_KV, seq // tq),
        in_specs=[pl.BlockSpec((A_G, A_HD, tq), lambda g, i: (g, 0, i)),
                  _full((seq, LANES)), pl.BlockSpec((1, nck, vrows, tk), lambda g, i: (g, 0, 0, 0))],
        out_specs=(pl.BlockSpec((A_G, A_HD, tq), lambda g, i: (g, 0, i)),
                   pl.BlockSpec((A_G, 1, tq), lambda g, i: (g, 0, i))),
        out_shape=(jax.ShapeDtypeStruct((A_H, A_HD, seq), F32), jax.ShapeDtypeStruct((A_H, 1, seq), F32)),
        scratch_shapes=[pltpu.VMEM((1, rows), F32), pltpu.VMEM((vrows, rows), F32), pltpu.VMEM((LANES, rows), _MXU),
                        pltpu.VMEM((tk, rows), F32), pltpu.VMEM((tk, rows), F32),
                        pltpu.VMEM((tk, rows), _MXU), pltpu.VMEM((tk, rows), _MXU),
                        pltpu.VMEM((1, rows), F32), pltpu.VMEM((1, rows), F32)],
    )(qt8, k2, vta)


def _attn_bwd(qt8, ot, dot_, lse, k2, v2, k2t, tq, tk, ksplit):
    seq = k2.shape[0]
    sh = seq // ksplit
    nck = sh // tk
    rows = A_G * tq
    rb = _tile(tk, 16384 // rows)
    assert nck % 2 == 0, nck

    def body(qt_ref, ot_ref, dot_ref, lse_ref, k_ref, v_ref, kt_ref,
             dk_ref, dv_ref, dq_ref, dq_sc, qtp_sc, dotp_sc, pt_sc, dst_sc,
             s_a, s_b, dp_a, dp_b, p_a, p_b, ds_a, ds_b):
        g = pl.program_id(1)
        hrows = pl.ds(pl.multiple_of(g * A_HD, A_HD), A_HD)

        @pl.when(pl.program_id(2) == 0)
        def _():
            dk_ref[...] = jnp.zeros_like(dk_ref)
            dv_ref[...] = jnp.zeros_like(dv_ref)

        lse_row = jnp.concatenate([lse_ref[a] for a in range(A_G)], axis=1)
        dd = jnp.concatenate([jnp.sum(ot_ref[a] * dot_ref[a], axis=0, keepdims=True)
                              for a in range(A_G)], axis=1)
        qtp_sc[...] = jnp.zeros_like(qtp_sc)
        dotp_sc[...] = jnp.zeros_like(dotp_sc)
        qtp_sc[hrows, :] = jnp.concatenate([qt_ref[a] for a in range(A_G)], axis=1)
        dotp_sc[hrows, :] = _bf(jnp.concatenate([dot_ref[a] for a in range(A_G)], axis=1))
        dq_sc[...] = jnp.zeros_like(dq_sc)

        def products(c, s_ref, dp_ref):
            sl = pl.ds(pl.multiple_of(c * tk, tk), tk)
            s_ref[...] = _mm(k_ref[sl, :], qtp_sc[...])
            dp_ref[...] = _mm(v_ref[sl, :], dotp_sc[...])

        def accumulate(c, p_ref, ds_ref):
            pt_sc[...] = p_ref[...].T
            dst_sc[...] = ds_ref[...].T
            dq_sc[...] += _mm(kt_ref[c, hrows, :], ds_ref[...])
            dv_ref[0, c] += _mm(dotp_sc[hrows, :], pt_sc[...])
            dk_ref[0, c] += _mm(qtp_sc[hrows, :], dst_sc[...])

        def stage(c, s_cur, dp_cur, s_nxt, dp_nxt, p_cur, ds_cur, p_prv, ds_prv, first=False, last=False):
            if not last:
                products(c + 1, s_nxt, dp_nxt)
            if not first:
                accumulate(c - 1, p_prv, ds_prv)
            for r in range(0, tk, rb):
                p = jnp.exp(s_cur[r:r + rb, :] - lse_row)
                p_cur[r:r + rb, :] = _bf(p)
                ds_cur[r:r + rb, :] = _bf(p * (dp_cur[r:r + rb, :] - dd))

        products(0, s_a, dp_a)
        stage(0, s_a, dp_a, s_b, dp_b, p_a, ds_a, p_b, ds_b, first=True)

        def pair(j, carry):
            stage(2 * j + 1, s_b, dp_b, s_a, dp_a, p_b, ds_b, p_a, ds_a)
            stage(2 * j + 2, s_a, dp_a, s_b, dp_b, p_a, ds_a, p_b, ds_b)
            return carry

        lax.fori_loop(0, nck // 2 - 1, pair, 0)
        stage(nck - 1, s_b, dp_b, s_a, dp_a, p_b, ds_b, p_a, ds_a, last=True)
        accumulate(nck - 1, p_b, ds_b)
        for a in range(A_G):
            dq_ref[0, a] = dq_sc[:, a * tq:(a + 1) * tq]

    tspec = pl.BlockSpec((A_G, A_HD, tq), lambda s, g, i: (g, 0, i))
    kspec = pl.BlockSpec((sh, LANES), lambda s, g, i: (s, 0))
    gspec = pl.BlockSpec((1, nck, A_HD, tk), lambda s, g, i: (g, s, 0, 0))
    gshape = jax.ShapeDtypeStruct((A_KV, seq // tk, A_HD, tk), F32)
    big = lambda dt: pltpu.VMEM((tk, rows), dt)
    bigt = pltpu.VMEM((rows, tk), _MXU)
    return pl.pallas_call(
        body, name="attn_bwd", grid=(ksplit, A_KV, seq // tq),
        in_specs=[tspec, tspec, tspec, pl.BlockSpec((A_G, 1, tq), lambda s, g, i: (g, 0, i)),
                  kspec, kspec, pl.BlockSpec((nck, LANES, tk), lambda s, g, i: (s, 0, 0))],
        out_specs=(gspec, gspec, pl.BlockSpec((1, A_G, A_HD, tq), lambda s, g, i: (s, g, 0, i))),
        out_shape=(gshape, gshape, jax.ShapeDtypeStruct((ksplit, A_H, A_HD, seq), F32)),
        scratch_shapes=[pltpu.VMEM((A_HD, rows), F32), pltpu.VMEM((LANES, rows), _MXU), pltpu.VMEM((LANES, rows), _MXU),
                        bigt, bigt,
                        big(F32), big(F32), big(F32), big(F32), big(_MXU), big(_MXU), big(_MXU), big(_MXU)],
    )(qt8, ot, dot_, lse, k2, v2, k2t)


def _chunks_to_rows(t):
    return t.transpose(1, 3, 0, 2).reshape(t.shape[1] * t.shape[3], AKV_W)


def _ret_tables(zb):
    c = CHUNK

    def body(z_ref, m_ref, mw_ref, qd_ref, qdw_ref, kd_ref, kdw_ref, g_ref, gw_ref):
        fwd = pl.program_id(0) < R_H
        z = z_ref[0]
        lam = jnp.minimum(z, 0.0) - jnp.log(1.0 + jnp.exp(-jnp.abs(z)))
        i = lax.broadcasted_iota(jnp.int32, (c, c), 0).astype(F32)
        j = lax.broadcasted_iota(jnp.int32, (c, c), 1).astype(F32)
        diff = jnp.where(fwd, i - j, j - i)
        keep = diff >= jnp.where(fwd, 0.0, 1.0)
        dist = jnp.maximum(diff, 0.0)
        m = jnp.where(keep, jnp.exp(lam * dist), 0.0)
        m_ref[0] = m
        mw_ref[0] = m * dist
        fq = jnp.where(fwd, i + 1.0, c - i)
        qd = jnp.exp(lam * fq)
        qd_ref[0] = qd
        qdw_ref[0] = qd * fq
        fk = jnp.where(fwd, c - 1.0 - i, i)
        kd = jnp.exp(lam * fk)
        kd_ref[0] = kd
        kdw_ref[0] = kd * fk
        gdec = jnp.exp(lam * c)
        g_ref[0] = gdec
        gw_ref[0] = gdec * c

    big = pl.BlockSpec((1, c, c), lambda t: (t, 0, 0))
    vec = pl.BlockSpec((1, 1, LANES), lambda t: (t, 0, 0))
    bshape = jax.ShapeDtypeStruct((2 * R_H, c, c), F32)
    vshape = jax.ShapeDtypeStruct((2 * R_H, 1, LANES), F32)
    return pl.pallas_call(
        body, name="ret_tables", grid=(2 * R_H,), in_specs=[vec],
        out_specs=(big, big, big, big, big, big, vec, vec),
        out_shape=(bshape,) * 6 + (vshape, vshape),
    )(zb)


def _ret_fwd(rq, rk, pr, m, qd, kd, gdec, cb):
    seq = rq.shape[0]
    c = CHUNK
    ns = seq // (cb * c)

    def body(qf_ref, kf_ref, vf_ref, qb_ref, kb_ref, vb_ref, m_ref, qd_ref, kd_ref, g_ref,
             yf_ref, yb_ref, pstf_ref, pstb_ref, p_sc):
        @pl.when(pl.program_id(0) == 0)
        def _():
            p_sc[...] = jnp.zeros_like(p_sc)

        dirs = ((qf_ref, kf_ref, vf_ref, yf_ref, pstf_ref), (qb_ref, kb_ref, vb_ref, yb_ref, pstb_ref))
        heads = [slice(h * R_HD, (h + 1) * R_HD) for h in range(R_H)]

        def chunk(j, carry):
            early = []
            for d, (q_ref, k_ref, v_ref, _, _) in enumerate(dirs):
                cc = j if d == 0 else cb - 1 - j
                sl = pl.ds(pl.multiple_of(cc * c, c), c)
                for h, hs in enumerate(heads):
                    t = d * R_H + h
                    early.append((_mm_nt(q_ref[sl, hs], k_ref[sl, hs]), _mm(q_ref[sl, hs] * qd_ref[t], p_sc[t]),
                                  _mm_tn(k_ref[sl, hs] * kd_ref[t], v_ref[sl, hs])))
            for d, (_, _, v_ref, y_ref, pst_ref) in enumerate(dirs):
                cc = j if d == 0 else cb - 1 - j
                sl = pl.ds(pl.multiple_of(cc * c, c), c)
                for h, hs in enumerate(heads):
                    t = d * R_H + h
                    qk, qp, kv = early[t]
                    p = p_sc[t]
                    pst_ref[h, cc] = p
                    y_ref[sl, hs] = _mm(qk * m_ref[t], v_ref[sl, hs]) + qp
                    p_sc[t] = p * g_ref[t] + kv
            return carry

        lax.fori_loop(0, cb, chunk, 0)

    asc = lambda off: pl.BlockSpec((cb * c, R_W), lambda n: (n, off))
    desc = lambda off: pl.BlockSpec((cb * c, R_W), lambda n: (ns - 1 - n, off))
    return pl.pallas_call(
        body, name="ret_fwd", grid=(ns,),
        in_specs=[asc(0), asc(0), asc(2), desc(0), desc(0), desc(2),
                  _full((2 * R_H, c, c)), _full((2 * R_H, c, c)), _full((2 * R_H, c, c)), _full((2 * R_H, 1, LANES))],
        out_specs=(asc(0), desc(0),
                   pl.BlockSpec((R_H, cb, R_HD, R_HD), lambda n: (0, n, 0, 0)),
                   pl.BlockSpec((R_H, cb, R_HD, R_HD), lambda n: (0, ns - 1 - n, 0, 0))),
        out_shape=(jax.ShapeDtypeStruct((seq, R_W), F32), jax.ShapeDtypeStruct((seq, R_W), F32),
                   jax.ShapeDtypeStruct((R_H, seq // c, R_HD, R_HD), F32),
                   jax.ShapeDtypeStruct((R_H, seq // c, R_HD, R_HD), F32)),
        scratch_shapes=[pltpu.VMEM((2 * R_H, R_HD, R_HD), F32)],
    )(rq, rk, pr, rq, rk, pr, m, qd, kd, gdec)


def _ret_bwd(rq, rk, pr, dry, pstf, pstb, m, mw, qd, qdw, kd, kdw, gdec, gw, cb):
    seq = rq.shape[0]
    c = CHUNK
    ns = seq // (cb * c)

    def body(qf_ref, kf_ref, vf_ref, dyf_ref, pstf_ref, qb_ref, kb_ref, vb_ref, dyb_ref, pstb_ref,
             m_ref, mw_ref, qd_ref, qdw_ref, kd_ref, kdw_ref, g_ref, gw_ref,
             dqf_ref, dkf_ref, dvf_ref, dqb_ref, dkb_ref, dvb_ref, dlam_ref, r_sc, acc_sc, e_sc, g_sc):
        n = pl.program_id(0)

        @pl.when(n == 0)
        def _():
            r_sc[...] = jnp.zeros_like(r_sc)
            acc_sc[...] = jnp.zeros_like(acc_sc)
            e_sc[...] = jnp.zeros_like(e_sc)
            g_sc[...] = jnp.zeros_like(g_sc)

        dirs = ((qf_ref, kf_ref, vf_ref, dyf_ref, pstf_ref, dqf_ref, dkf_ref, dvf_ref),
                (qb_ref, kb_ref, vb_ref, dyb_ref, pstb_ref, dqb_ref, dkb_ref, dvb_ref))
        heads = [slice(h * R_HD, (h + 1) * R_HD) for h in range(R_H)]

        def chunk(j, carry):
            first = []
            for d, (q_ref, k_ref, v_ref, dy_ref, pst_ref, _, _, _) in enumerate(dirs):
                cc = cb - 1 - j if d == 0 else j
                sl = pl.ds(pl.multiple_of(cc * c, c), c)
                for h, hs in enumerate(heads):
                    t = d * R_H + h
                    q, k, v, dy = q_ref[sl, hs], k_ref[sl, hs], v_ref[sl, hs], dy_ref[sl, hs]
                    r = r_sc[t]
                    first.append((_mm_nt(q, k), _mm_nt(dy, v), _mm_nt(dy, pst_ref[h, cc]), _mm_nt(v, r),
                                  _mm(k * kd_ref[t], r), _mm_tn(q * qd_ref[t], dy)))
            for d, (q_ref, k_ref, _, dy_ref, pst_ref, dq_ref, dk_ref, dv_ref) in enumerate(dirs):
                cc = cb - 1 - j if d == 0 else j
                sl = pl.ds(pl.multiple_of(cc * c, c), c)
                for h, hs in enumerate(heads):
                    t = d * R_H + h
                    qk, ds, dyp, vr, kr, qdy = first[t]
                    q, k, dy = q_ref[sl, hs], k_ref[sl, hs], dy_ref[sl, hs]
                    r = r_sc[t]
                    da = ds * m_ref[t]
                    dv_ref[sl, hs] = _mm_tn(qk * m_ref[t], dy) + kr
                    dq_ref[sl, hs] = _mm(da, k) + dyp * qd_ref[t]
                    dk_ref[sl, hs] = _mm_tn(da, q) + vr * kd_ref[t]
                    acc_sc[t] += dyp * q * qdw_ref[t] + vr * k * kdw_ref[t]
                    e_sc[t] += ds * qk * mw_ref[t]
                    g_sc[t] += r * pst_ref[h, cc]
                    r_sc[t] = r * g_ref[t] + qdy
            return carry

        lax.fori_loop(0, cb, chunk, 0)

        @pl.when(n == ns - 1)
        def _():
            for t in range(2 * R_H):
                tot = jnp.sum(jnp.sum(acc_sc[t] + e_sc[t] + g_sc[t] * gw_ref[t], axis=0, keepdims=True),
                              axis=1, keepdims=True)
                dlam_ref[t] = jnp.broadcast_to(tot, (1, LANES))

    asc = lambda off: pl.BlockSpec((cb * c, R_W), lambda n: (n, off))
    desc = lambda off: pl.BlockSpec((cb * c, R_W), lambda n: (ns - 1 - n, off))
    big = _full((2 * R_H, c, c))
    vec = _full((2 * R_H, 1, LANES))
    oshape = jax.ShapeDtypeStruct((seq, R_W), F32)
    sq = pltpu.VMEM((2 * R_H, R_HD, R_HD), F32)
    return pl.pallas_call(
        body, name="ret_bwd", grid=(ns,),
        in_specs=[desc(0), desc(0), desc(2), desc(0),
                  pl.BlockSpec((R_H, cb, R_HD, R_HD), lambda n: (0, ns - 1 - n, 0, 0)),
                  asc(0), asc(0), asc(2), asc(0),
                  pl.BlockSpec((R_H, cb, R_HD, R_HD), lambda n: (0, n, 0, 0)),
                  big, big, big, big, big, big, vec, vec],
        out_specs=(desc(0), desc(0), desc(0), asc(0), asc(0), asc(0), vec),
        out_shape=(oshape,) * 6 + (jax.ShapeDtypeStruct((2 * R_H, 1, LANES), F32),),
        scratch_shapes=[sq, sq, sq, sq],
    )(rq, rk, pr, dry, pstf, rq, rk, pr, dry, pstb, m, mw, qd, qdw, kd, kdw, gdec, gw)


def _group_norm(ry):
    yn, rs = [], []
    for h in range(R_H):
        s = ry[:, h * R_HD:(h + 1) * R_HD]
        mu = jnp.mean(s, axis=-1, keepdims=True)
        cen = s - mu
        r = lax.rsqrt(jnp.mean(cen * cen, axis=-1, keepdims=True) + GN_EPS)
        yn.append(cen * r)
        rs.append(r)
    return yn, rs


def _merge_fwd(x, o, yf, yb, pr, pg, gain_r, wao, wro, wout):
    seq = x.shape[0]
    tm = _tile(seq, 512)

    def body(x_ref, o_ref, yf_ref, yb_ref, rg_ref, ga_ref, gr_ref, gn_ref, wao_ref, wro_ref, wout_ref,
             x1_ref, mg_ref, ri_ref):
        yn, _ = _group_norm(yf_ref[...] + yb_ref[...])
        rg = rg_ref[...]
        ret_in = jnp.concatenate(yn, axis=1) * gn_ref[...] * (rg * _sigmoid(rg))
        ri_ref[...] = _bf(ret_in)
        attn_out = _mm(o_ref[...], wao_ref[...])
        ret_out = _mm(ret_in, wro_ref[...])
        merged = _sigmoid(ga_ref[...]) * attn_out + _sigmoid(gr_ref[...]) * ret_out
        mg_ref[...] = _bf(merged)
        x1_ref[...] = x_ref[...] + _mm(merged, wout_ref[...])

    row = lambda w_, j=0: pl.BlockSpec((tm, w_), lambda i: (i, j))
    return pl.pallas_call(
        body, name="merge_fwd", grid=(seq // tm,),
        in_specs=[row(D), row(AQ_W), row(R_W), row(R_W), row(R_W, 3), row(D, 0), row(D, 1),
                  _full((1, R_W)), _full((AQ_W, D)), _full((R_W, D)), _full((D, D))],
        out_specs=(row(D), row(D), row(R_W)),
        out_shape=(jax.ShapeDtypeStruct((seq, D), F32), jax.ShapeDtypeStruct((seq, D), _MXU),
                   jax.ShapeDtypeStruct((seq, R_W), _MXU)),
    )(x, o, yf, yb, pr, pg, pg, gain_r, wao, wro, wout)


def _mlp_fwd(x1, gain, wup, wdown):
    seq = x1.shape[0]
    tm = _tile(seq, 1024)
    fc = 2048
    nfc = FF // fc

    def body(x_ref, g_ref, wu_ref, wd_ref, x2_ref, hm_sc, acc_sc):
        c = pl.program_id(1)

        @pl.when(c == 0)
        def _():
            n, _ = _rms(x_ref[...])
            hm_sc[...] = _bf(n * g_ref[...])
            acc_sc[...] = jnp.zeros_like(acc_sc)

        halves = (slice(0, fc // 2), slice(fc // 2, fc))
        ups = [jnp.maximum(_mm(hm_sc[...], wu_ref[:, hs]), 0.0) for hs in halves]
        acc_sc[...] += _mm(ups[0] * ups[0], wd_ref[halves[0], :]) + _mm(ups[1] * ups[1], wd_ref[halves[1], :])

        @pl.when(c == nfc - 1)
        def _():
            x2_ref[...] = x_ref[...] + acc_sc[...]

    return pl.pallas_call(
        body, name="mlp_fwd", grid=(seq // tm, nfc),
        in_specs=[pl.BlockSpec((tm, D), lambda i, c: (i, 0)), pl.BlockSpec((1, D), lambda i, c: (0, 0)),
                  pl.BlockSpec((D, fc), lambda i, c: (0, c)), pl.BlockSpec((fc, D), lambda i, c: (c, 0))],
        out_specs=pl.BlockSpec((tm, D), lambda i, c: (i, 0)),
        out_shape=jax.ShapeDtypeStruct((seq, D), F32),
        scratch_shapes=[pltpu.VMEM((tm, D), _MXU), pltpu.VMEM((tm, D), F32)],
    )(x1, gain, wup, wdown)


def _ple_loss(x2, p, tgt, g_ple, g_fin, wpg, wpgt, wple):
    seq = x2.shape[0]
    tm = _tile(seq, 512)

    def body(x2_ref, p_ref, t_ref, gp_ref, gf_ref, wpg_ref, wpgt_ref, wple_ref,
             dx2_ref, de_ref, dz_ref, hp_ref, loss_ref, dgf_ref, dgp_ref):
        @pl.when(pl.program_id(0) == 0)
        def _():
            loss_ref[...] = jnp.zeros_like(loss_ref)
            dgf_ref[...] = jnp.zeros_like(dgf_ref)
            dgp_ref[...] = jnp.zeros_like(dgp_ref)

        x2 = x2_ref[...]
        gp, gf = gp_ref[...], gf_ref[...]
        n2, r2 = _rms(x2)
        hp = _bf(n2 * gp)
        hp_ref[...] = hp
        gate = _sigmoid(_mm(hp, wpg_ref[...]))
        e = _mm(p_ref[...], wple_ref[...])
        x3 = x2 + gate * e
        n3, r3 = _rms(x3)
        diff = n3 * gf - t_ref[...]
        row_loss = jnp.mean(diff * diff, axis=-1, keepdims=True)
        loss_ref[...] += 0.5 * jnp.sum(row_loss, axis=0, keepdims=True)
        dy = diff * (1.0 / D)
        dgf_ref[...] += jnp.sum(dy * n3, axis=0, keepdims=True)
        dx3 = _rms_bwd(n3, r3, gf, dy)
        de_ref[...] = _bf(dx3 * gate)
        dz = dx3 * e * gate * (1.0 - gate)
        dz_ref[...] = _bf(dz)
        dhp = _mm(dz, wpgt_ref[...])
        dgp_ref[...] += jnp.sum(dhp * n2, axis=0, keepdims=True)
        dx2_ref[...] = dx3 + _rms_bwd(n2, r2, gp, dhp)

    row = lambda w_: pl.BlockSpec((tm, w_), lambda i: (i, 0))
    act = lambda dt: jax.ShapeDtypeStruct((seq, D), dt)
    return pl.pallas_call(
        body, name="ple_loss", grid=(seq // tm,),
        in_specs=[row(D), row(PLE), row(D), _full((1, D)), _full((1, D)),
                  _full((D, D)), _full((D, D)), _full((PLE, D))],
        out_specs=(row(D), row(D), row(D), row(D), _full((1, LANES)), _full((1, D)), _full((1, D))),
        out_shape=(act(F32), act(_MXU), act(_MXU), act(_MXU), jax.ShapeDtypeStruct((1, LANES), F32),
                   jax.ShapeDtypeStruct((1, D), F32), jax.ShapeDtypeStruct((1, D), F32)),
    )(x2, p, tgt, g_ple, g_fin, wpg, wpgt, wple)


def _mlp_bwd(x1, dx2, gain, wup, wdownt, wupt):
    seq = x1.shape[0]
    tm = _tile(seq, 512)
    fc = 2048
    nfc = FF // fc

    def body(x_ref, dx2_ref, g_ref, wu_ref, wdt_ref, wut_ref,
             dx1_ref, a_ref, du_ref, hm_ref, dg_ref, dhm_sc):
        i = pl.program_id(0)
        c = pl.program_id(1)

        @pl.when((i == 0) & (c == 0))
        def _():
            dg_ref[...] = jnp.zeros_like(dg_ref)

        @pl.when(c == 0)
        def _():
            n, _ = _rms(x_ref[...])
            hm_ref[...] = _bf(n * g_ref[...])
            dhm_sc[...] = jnp.zeros_like(dhm_sc)

        halves = (slice(0, fc // 2), slice(fc // 2, fc))
        ups = [jnp.maximum(_mm(hm_ref[...], wu_ref[:, hs]), 0.0) for hs in halves]
        das = [_mm(dx2_ref[...], wdt_ref[:, hs]) for hs in halves]
        part = None
        for u, da, hs in zip(ups, das, halves):
            a_ref[:, hs] = _bf(u * u)
            du = _bf(da * (2.0 * u))
            du_ref[:, hs] = du
            t = _mm(du, wut_ref[hs, :])
            part = t if part is None else part + t
        dhm_sc[...] += part

        @pl.when(c == nfc - 1)
        def _():
            n, r = _rms(x_ref[...])
            dhm = dhm_sc[...]
            dg_ref[...] += jnp.sum(dhm * n, axis=0, keepdims=True)
            dx1_ref[...] = dx2_ref[...] + _rms_bwd(n, r, g_ref[...], dhm)

    rowd = pl.BlockSpec((tm, D), lambda i, c: (i, 0))
    rowf = pl.BlockSpec((tm, fc), lambda i, c: (i, c))
    return pl.pallas_call(
        body, name="mlp_bwd", grid=(seq // tm, nfc),
        in_specs=[rowd, rowd, pl.BlockSpec((1, D), lambda i, c: (0, 0)),
                  pl.BlockSpec((D, fc), lambda i, c: (0, c)), pl.BlockSpec((D, fc), lambda i, c: (0, c)),
                  pl.BlockSpec((fc, D), lambda i, c: (c, 0))],
        out_specs=(rowd, rowf, rowf, rowd, pl.BlockSpec((1, D), lambda i, c: (0, 0))),
        out_shape=(jax.ShapeDtypeStruct((seq, D), F32), jax.ShapeDtypeStruct((seq, FF), _MXU),
                   jax.ShapeDtypeStruct((seq, FF), _MXU), jax.ShapeDtypeStruct((seq, D), _MXU),
                   jax.ShapeDtypeStruct((1, D), F32)),
        scratch_shapes=[pltpu.VMEM((tm, D), F32)],
    )(x1, dx2, gain, wup, wdownt, wupt)


def _merge_bwd(dx1, o, yf, yb, pr, pg, gain_r, wao, wro, woutt, waot, wrot, gpack):
    seq = dx1.shape[0]
    tm = _tile(seq, 256)
    nsteps = seq // tm

    def body(dx1_ref, o_ref, yf_ref, yb_ref, rg_ref, ga_ref, gr_ref, gn_ref, wao_ref, wro_ref,
             woutt_ref, waot_ref, wrot_ref, gpack_ref,
             dpg_ref, dao_ref, dro_ref, do_ref, dry_ref, drg_ref, dgn_ref, land_ref,
             send_sems, recv_sems, local_sem):
        start, finish = _exchange_steps(gpack_ref, land_ref, send_sems, recv_sems, local_sem, True)
        pl.when(pl.program_id(0) == 0)(start)

        @pl.when(pl.program_id(0) == 0)
        def _():
            dgn_ref[...] = jnp.zeros_like(dgn_ref)

        yn_l, rs_l = _group_norm(yf_ref[...] + yb_ref[...])
        yn = jnp.concatenate(yn_l, axis=1)
        rg = rg_ref[...]
        gn = gn_ref[...]
        sg = _sigmoid(rg)
        sil = rg * sg
        ret_in = yn * gn * sil
        attn_out = _mm(o_ref[...], wao_ref[...])
        ret_out = _mm(ret_in, wro_ref[...])
        sa = _sigmoid(ga_ref[...])
        sr = _sigmoid(gr_ref[...])
        dm = _mm(dx1_ref[...], woutt_ref[...])
        dpg_ref[:, 0:D] = _bf(dm * attn_out * sa * (1.0 - sa))
        dpg_ref[:, D:2 * D] = _bf(dm * ret_out * sr * (1.0 - sr))
        dao = _bf(dm * sa)
        dro = _bf(dm * sr)
        dao_ref[...] = dao
        dro_ref[...] = dro
        do_ref[...] = _mm(dao, waot_ref[...])
        dri = _mm(dro, wrot_ref[...])
        dgn_ref[...] += jnp.sum(dri * yn * sil, axis=0, keepdims=True)
        drg_ref[...] = _bf(dri * yn * gn * (sg * (1.0 + rg * (1.0 - sg))))
        dyn = dri * gn * sil
        dry = []
        for h in range(R_H):
            dh = dyn[:, h * R_HD:(h + 1) * R_HD]
            dry.append(rs_l[h] * (dh - jnp.mean(dh, axis=-1, keepdims=True)
                                  - yn_l[h] * jnp.mean(dh * yn_l[h], axis=-1, keepdims=True)))
        dry_ref[...] = jnp.concatenate(dry, axis=1)
        pl.when(pl.program_id(0) == nsteps - 1)(finish)

    row = lambda w_, j=0: pl.BlockSpec((tm, w_), lambda i: (i, j))
    return pl.pallas_call(
        body, name="merge_bwd", grid=(nsteps,),
        in_specs=[row(D), row(AQ_W), row(R_W), row(R_W), row(R_W, 3), row(D, 0), row(D, 1),
                  _full((1, R_W)), _full((AQ_W, D)), _full((R_W, D)), _full((D, D)),
                  _full((D, AQ_W)), _full((D, R_W)), _ANY],
        out_specs=(row(PG_W), row(D), row(D), row(AQ_W), row(R_W), row(R_W), _full((1, R_W)), _ANY),
        out_shape=(jax.ShapeDtypeStruct((seq, PG_W), _MXU), jax.ShapeDtypeStruct((seq, D), _MXU),
                   jax.ShapeDtypeStruct((seq, D), _MXU), jax.ShapeDtypeStruct((seq, AQ_W), F32),
                   jax.ShapeDtypeStruct((seq, R_W), F32), jax.ShapeDtypeStruct((seq, R_W), _MXU),
                   jax.ShapeDtypeStruct((1, R_W), F32), jax.ShapeDtypeStruct(gpack.shape, gpack.dtype)),
        scratch_shapes=list(_COMM_SCRATCH),
    )(dx1, o, yf, yb, pr, pg, pg, gain_r, wao, wro, woutt, waot, wrot, gpack)


def _qk_prep_bwd(pa, dqh, dk2, dv2, rdf, rdb, drg, gq, gk, seg, ca, sa, cr, sr):
    seq = pa.shape[0]
    tm = _tile(seq, 512)

    def body(pa_ref, dqh_ref, dk2_ref, dv2_ref, rdqf_ref, rdqb_ref, rdkf_ref, rdkb_ref, rdvf_ref, rdvb_ref,
             drg_ref, gq_ref, gk_ref, seg_ref, ca_ref, sa_ref, cr_ref, sr_ref,
             dpa_ref, dpr_ref, dgq_ref, dgk_ref):
        @pl.when(pl.program_id(0) == 0)
        def _():
            dgq_ref[...] = jnp.zeros_like(dgq_ref)
            dgk_ref[...] = jnp.zeros_like(dgk_ref)

        ca_, sa_ = ca_ref[...], sa_ref[...]

        def norm_bwd(raw, gain, dy, segm, dg_ref):
            msq = _seg_mean(raw * raw, segm)
            r = lax.rsqrt(msq + EPS)
            n = raw * r
            dg_ref[...] += jnp.sum(dy * n, axis=0, keepdims=True)
            dn = dy * gain
            return r * (dn - n * _seg_mean(dn * n, segm))

        dqn = _rope(dqh_ref[...] * (A_HD ** -0.5), _cat(ca_, 4), -_cat(sa_, 4), A_HD // 2)
        dpa_ref[:, 0:AQ_W] = _bf(norm_bwd(pa_ref[:, 0:AQ_W], gq_ref[...], dqn, seg_ref[...], dgq_ref))
        dkn = _rope(dk2_ref[...], ca_, -sa_, A_HD // 2)
        dpa_ref[:, AQ_W:AQ_W + AKV_W] = _bf(norm_bwd(pa_ref[:, AQ_W:AQ_W + AKV_W], gk_ref[...], dkn,
                                                     seg_ref[0:AKV_W, 0:AKV_W], dgk_ref))
        dpa_ref[:, AQ_W + AKV_W:PA_W] = _bf(dv2_ref[...])
        cr_, sr_ = _cat(cr_ref[...], 4), -_cat(sr_ref[...], 4)
        dpr_ref[:, 0:R_W] = _bf(_rope((rdqf_ref[...] + rdqb_ref[...]) * (R_HD ** -0.5), cr_, sr_, R_HD // 2))
        dpr_ref[:, R_W:2 * R_W] = _bf(_rope(rdkf_ref[...] + rdkb_ref[...], cr_, sr_, R_HD // 2))
        dpr_ref[:, 2 * R_W:3 * R_W] = _bf(rdvf_ref[...] + rdvb_ref[...])
        dpr_ref[:, 3 * R_W:4 * R_W] = drg_ref[...]

    row = lambda w_: pl.BlockSpec((tm, w_), lambda i: (i, 0))
    return pl.pallas_call(
        body, name="qk_prep_bwd", grid=(seq // tm,),
        in_specs=[row(PA_W), row(AQ_W), row(AKV_W), row(AKV_W), row(R_W), row(R_W), row(R_W), row(R_W),
                  row(R_W), row(R_W), row(R_W), _full((1, AQ_W)), _full((1, AKV_W)), _full((AQ_W, AQ_W)),
                  row(LANES), row(LANES), row(LANES), row(LANES)],
        out_specs=(row(PA_W), row(PR_W), _full((1, AQ_W)), _full((1, AKV_W))),
        out_shape=(jax.ShapeDtypeStruct((seq, PA_W), _MXU), jax.ShapeDtypeStruct((seq, PR_W), _MXU),
                   jax.ShapeDtypeStruct((1, AQ_W), F32), jax.ShapeDtypeStruct((1, AKV_W), F32)),
    )(pa, dqh, dk2, dv2, rdf[0], rdb[0], rdf[1], rdb[1], rdf[2], rdb[2], drg, gq, gk, seg, ca, sa, cr, sr)


def _in_proj_bwd(x, dx1, gain, dpa, dpr, dpg, wint, packs):
    seq = x.shape[0]
    tm = _tile(seq, 256)
    nsteps = seq // tm
    npk = len(packs)

    def body(x_ref, dx1_ref, g_ref, dpa_ref, dpr_ref, dpg_ref, wt_ref, *rest):
        pack_refs, (dx_ref, dg_ref), land_refs = rest[:npk], rest[npk:npk + 2], rest[npk + 2:2 * npk + 2]
        send_sems, recv_sems, local_sems = rest[2 * npk + 2:]
        steps = [_exchange_steps(pack_refs[t], land_refs[t], send_sems, recv_sems, local_sems.at[t], True, base=7 * t)
                 for t in range(npk)]

        @pl.when(pl.program_id(0) == 0)
        def _():
            for start, _ in steps:
                start()
            dg_ref[...] = jnp.zeros_like(dg_ref)

        dh = (_mm(dpa_ref[...], wt_ref[0:PA_W, :]) + _mm(dpr_ref[...], wt_ref[PA_W:PA_W + PR_W, :])
              + _mm(dpg_ref[...], wt_ref[PA_W + PR_W:IN_W, :]))
        n, r = _rms(x_ref[...])
        dg_ref[...] += jnp.sum(dh * n, axis=0, keepdims=True)
        dx_ref[...] = dx1_ref[...] + _rms_bwd(n, r, g_ref[...], dh)

        @pl.when(pl.program_id(0) == nsteps - 1)
        def _():
            for _, finish in steps:
                finish()

    row = lambda w_: pl.BlockSpec((tm, w_), lambda i: (i, 0))
    return pl.pallas_call(
        body, name="in_proj_bwd", grid=(nsteps,),
        in_specs=[row(D), row(D), _full((1, D)), row(PA_W), row(PR_W), row(PG_W), _full((IN_W, D))] + [_ANY] * npk,
        out_specs=(row(D), _full((1, D))) + (_ANY,) * npk,
        out_shape=(jax.ShapeDtypeStruct((seq, D), F32), jax.ShapeDtypeStruct((1, D), F32))
        + tuple(jax.ShapeDtypeStruct(g.shape, g.dtype) for g in packs),
        scratch_shapes=[pltpu.SemaphoreType.DMA((7 * npk,)), pltpu.SemaphoreType.DMA((7 * npk,)),
                        pltpu.SemaphoreType.DMA((npk,))],
    )(x, dx1, gain, dpa, dpr, dpg, wint, *packs)


def _wgrad(a, b, name):
    seq, m = a.shape
    n = b.shape[1]
    tm, tn, ts = _tile(m, 1024), _tile(n, 1024), _tile(seq, 2048)
    ns = seq // ts

    def body(a_ref, b_ref, o_ref):
        @pl.when(pl.program_id(2) == 0)
        def _():
            o_ref[...] = jnp.zeros_like(o_ref)

        o_ref[...] += _mm_tn(a_ref[...], b_ref[...])

    return pl.pallas_call(
        body, name=name, grid=(m // tm, n // tn, ns),
        in_specs=[pl.BlockSpec((ts, tm), lambda i, j, s: (s, i)), pl.BlockSpec((ts, tn), lambda i, j, s: (s, j))],
        out_specs=pl.BlockSpec((tm, tn), lambda i, j, s: (i, j)),
        out_shape=jax.ShapeDtypeStruct((m, n), F32),
    )(a, b)


def _adamw_math(w, g, m, v):
    m = B1 * m + (1.0 - B1) * g
    v = B2 * v + (1.0 - B2) * (g * g)
    m_hat = m / (1.0 - B1 ** STEP)
    v_hat = v / (1.0 - B2 ** STEP)
    delta = -LR * (m_hat / (jnp.sqrt(v_hat) + ADAM_EPS) + WD * w)
    return delta, m, v


def _adamw_big(land, own, w, m, v, name):
    rws, cols = w.shape
    tr = next(t for t in range(min(rws, 288) // 16 * 16, 0, -16) if rws % t == 0)

    def body(l_ref, o_ref, w_ref, m_ref, v_ref, g_ref, d_ref, nm_ref, nv_ref):
        x, y, c = _mesh_pos()
        me = 4 * x + 2 * y + c
        g = o_ref[...]
        for j in range(N_DEV):
            g = g + jnp.where(me == j, 0.0, l_ref[j].astype(F32))
        g_ref[...] = g
        d_ref[...], nm_ref[...], nv_ref[...] = _adamw_math(w_ref[...], g, m_ref[...], v_ref[...])

    row = pl.BlockSpec((tr, cols), lambda i: (i, 0))
    shp = jax.ShapeDtypeStruct((rws, cols), F32)
    return pl.pallas_call(
        body, name=name, grid=(rws // tr,),
        in_specs=[pl.BlockSpec((N_DEV, tr, cols), lambda i: (0, i, 0)), row, row, row, row],
        out_specs=(row, row, row, row), out_shape=(shp, shp, shp, shp),
    )(land, own, w, m, v)


def _adamw_small(sland, w, m, v):
    def body(l_ref, w_ref, m_ref, v_ref, g_ref, d_ref, nm_ref, nv_ref, loss_ref):
        s = l_ref[0]
        for j in range(1, N_DEV):
            s = s + l_ref[j]
        w = w_ref[...]
        gq = s[8:9]
        for h in range(1, A_H):
            gq = gq + s[8 + h:9 + h]
        gk = s[16:17] + s[17:18]
        gdec = s[5:6] * _sigmoid(-w[5:6])
        g = jnp.concatenate([s[0:5], gdec, gq, gk], axis=0)
        g_ref[...] = g
        d_ref[...], nm_ref[...], nv_ref[...] = _adamw_math(w, g, m_ref[...], v_ref[...])
        loss_ref[...] = s[6:7, 0:LANES]

    shp = jax.ShapeDtypeStruct((8, PACK_COLS), F32)
    return pl.pallas_call(
        body, name="adamw_small",
        out_shape=(shp, shp, shp, shp, jax.ShapeDtypeStruct((1, LANES), F32)),
    )(sland, w, m, v)


_BIG = (("w_attn_o", AQ_W, D, 1), ("w_ret_o", R_W, D, 1), ("w_out", D, D, 0),
        ("w_up", D, FF, 1), ("w_down", FF, D, 0), ("w_ple_gate", D, D, 0), ("w_ple", PLE, D, 1))
_LATE = _BIG[:3]
_EARLY = _BIG[3:]
IN_SHARD = IN_W // N_DEV
_SMALL = ("mix_norm", "mlp_norm", "ple_norm", "final_norm", "ret_norm_gain", "ret_decay_logit",
          "attn_q_norm", "attn_k_norm")


def _shard_shape(rows, cols, axis):
    return (rows // N_DEV, cols) if axis == 0 else (rows, cols // N_DEV)


def _pack_shards(shards):
    flat = jnp.concatenate([s.reshape(-1) for s in shards])
    return flat.reshape(-1, PACK_COLS)


def _unpack_gathered(gathered):
    flat = gathered.reshape(N_DEV, -1)
    out, off = {}, 0
    for name, rows, cols, axis in _BIG:
        sr, sc = _shard_shape(rows, cols, axis)
        blk = flat[:, off:off + sr * sc].reshape(N_DEV, sr, sc)
        off += sr * sc
        out[name] = blk.reshape(rows, cols) if axis == 0 else blk.transpose(1, 0, 2).reshape(rows, cols)
    return out


def _pack_full_grads(grads, group):
    parts = []
    for name, rows, cols, axis in group:
        sr, sc = _shard_shape(rows, cols, axis)
        g = grads[name]
        blk = g.reshape(N_DEV, sr, sc) if axis == 0 else g.reshape(rows, N_DEV, sc).transpose(1, 0, 2)
        parts.append(blk.reshape(N_DEV, -1))
    flat = jnp.concatenate(parts, axis=1)
    return flat.reshape(N_DEV, -1, PACK_COLS)


def _unpack_shard(packed, group):
    flat = packed.reshape(-1)
    out, off = {}, 0
    for name, rows, cols, axis in group:
        sr, sc = _shard_shape(rows, cols, axis)
        out[name] = flat[off:off + sr * sc].reshape(1, sr, sc)
        off += sr * sc
    return out


def _pack_small(vals):
    rows = [jnp.pad(vals[n].reshape(-1), (0, PACK_COLS - vals[n].size)) for n in _SMALL]
    return jnp.stack(rows)


def _unpack_small(packed, like):
    return {n: packed[i, :like[n].size].reshape(like[n].shape) for i, n in enumerate(_SMALL)}


def _row(v):
    return jnp.pad(v.reshape(-1), (0, PACK_COLS - v.size))


def kernel(x, p, mix_norm, w_in, attn_q_norm, attn_k_norm, ret_decay_logit, ret_norm_gain, w_attn_o, w_ret_o, w_out, mlp_norm, w_up, w_down, ple_norm, w_ple_gate, w_ple, final_norm, loss_target, m_mix_norm, m_w_in, m_attn_q_norm, m_attn_k_norm, m_ret_decay_logit, m_ret_norm_gain, m_w_attn_o, m_w_ret_o, m_w_out, m_mlp_norm, m_w_up, m_w_down, m_ple_norm, m_w_ple_gate, m_w_ple, m_final_norm, v_mix_norm, v_w_in, v_attn_q_norm, v_attn_k_norm, v_ret_decay_logit, v_ret_norm_gain, v_w_attn_o, v_w_ret_o, v_w_out, v_mlp_norm, v_w_up, v_w_down, v_ple_norm, v_w_ple_gate, v_w_ple, v_final_norm):
    args = dict(locals())
    seq = x.shape[1]
    xs = x[0]
    ps = p[0, 0]
    tgt = loss_target[0]

    big_names = [b[0] for b in _BIG]
    wshard = _pack_shards([args[n] for n in big_names])
    win = _all_gather(w_in[0].astype(_MXU)).transpose(1, 0, 2).reshape(D, IN_W)

    g_mix, g_mlp, g_ple = mix_norm, mlp_norm, ple_norm
    g_fin = final_norm.reshape(1, D)
    gq = jnp.tile(attn_q_norm, (1, A_H))
    gk = jnp.tile(attn_k_norm, (1, A_KV))
    seg = _seg_mean_matrix()
    ca, sa, cr, sr = _rope_tables(seq)

    pa, pr, pg, h, rest_g = _in_proj(xs, g_mix, win, wshard.astype(_MXU))
    wfull = _unpack_gathered(rest_g)
    wao, wro, wout = wfull["w_attn_o"], wfull["w_ret_o"], wfull["w_out"]
    wup, wdown, wpg, wple = wfull["w_up"], wfull["w_down"], wfull["w_ple_gate"], wfull["w_ple"]
    qh, kh, vh, rqh, rkh = _qk_prep(pa, pr, gq, gk, seg, ca, sa, cr, sr)

    tq = _tile(seq, 128)
    tk = _tile(seq // 4, 2048)
    qt8 = qh.reshape(seq, A_H, A_HD).transpose(1, 2, 0)
    vta = jnp.stack([jnp.concatenate([_chunk_t(vh[:, g * A_HD:(g + 1) * A_HD], tk),
                                      jnp.ones((seq // tk, 16, tk), _MXU)], axis=1) for g in range(A_KV)])
    ot, lse = _attn_fwd(qt8, kh, vta, tq, tk)
    o = _heads_to_rows(ot)

    zb = jnp.broadcast_to(ret_decay_logit.reshape(2 * R_H, 1, 1), (2 * R_H, 1, LANES))
    tm_, tmw, tqd, tqdw, tkd, tkdw, tg, tgw = _ret_tables(zb)
    cb = _tile(seq // CHUNK, 4)
    yf, yb, pstf, pstb = _ret_fwd(rqh, rkh, pr, tm_, tqd, tkd, tg, cb)

    x1, merged, ret_in = _merge_fwd(xs, o, yf, yb, pr, pg, ret_norm_gain, wao, wro, wout)
    x2 = _mlp_fwd(x1, g_mlp, wup, wdown)

    dx2, de, dz, hp, loss_p, dg_fin, dg_ple = _ple_loss(x2, ps, tgt, g_ple, g_fin, wpg, wpg.T, wple)
    dx1, act, du, hm, dg_mlp = _mlp_bwd(x1, dx2, g_mlp, wup, wdown.T, wup.T)
    me = 4 * lax.axis_index("x") + 2 * lax.axis_index("y") + lax.axis_index("c")
    gpack_e = _pack_full_grads({"w_up": _wgrad(hm, du, "wgrad_up"), "w_down": _wgrad(act, dx2, "wgrad_down"),
                                "w_ple_gate": _wgrad(hp, dz, "wgrad_ple_gate"), "w_ple": _wgrad(ps, de, "wgrad_ple")},
                               _EARLY)
    dpg, dao, dro, do, dry, drg, dg_gn, land_e = _merge_bwd(dx1, o, yf, yb, pr, pg, ret_norm_gain, wao, wro,
                                                            wout.T, wao.T, wro.T, _bf(gpack_e))
    *rd, dlam = _ret_bwd(rqh, rkh, pr, dry, pstf, pstb, tm_, tmw, tqd, tqdw, tkd, tkdw, tg, tgw, cb)

    ksplit = 1
    tkb = _tile(seq // 4, 512)
    dot_ = do.reshape(seq, A_H, A_HD).transpose(1, 2, 0)
    dkt, dvt, dqt = _attn_bwd(qt8, ot, dot_, lse, kh, vh, _chunk_t(kh, tkb), tq, tkb, ksplit)
    dqh = _heads_to_rows(dqt[0] if ksplit == 1 else jnp.sum(dqt, axis=0))
    dpa, dpr, dg_q, dg_k = _qk_prep_bwd(pa, dqh, _chunks_to_rows(dkt), _chunks_to_rows(dvt), rd[0:3], rd[3:6], drg, gq, gk, seg, ca, sa, cr, sr)
    wg_in = jnp.concatenate([_wgrad(h, dpa, "wgrad_in_a"), _wgrad(h, dpr, "wgrad_in_r"),
                             _wgrad(h, dpg, "wgrad_in_g")], axis=1)
    gpack_l = _pack_full_grads({"w_attn_o": _wgrad(o, dao, "wgrad_attn_o"),
                                "w_ret_o": _wgrad(ret_in, dro, "wgrad_ret_o"),
                                "w_out": _wgrad(merged, dx1, "wgrad_out")}, _LATE)
    gpack_in = wg_in.reshape(D, N_DEV, IN_SHARD).transpose(1, 0, 2)
    grad_x, dg_mix, land_in, land_l = _in_proj_bwd(xs, dx1, g_mix, dpa, dpr, dpg, win.T,
                                                   [_bf(gpack_in), _bf(gpack_l)])
    small = jnp.stack(
        [_row(dg_mix), _row(dg_mlp), _row(dg_ple), _row(dg_fin), _row(dg_gn), _row(dlam[:, 0, 0]),
         _row(loss_p[0, 0:1]), jnp.zeros((PACK_COLS,), F32)]
        + [_row(dg_q[0, hh * A_HD:(hh + 1) * A_HD]) for hh in range(A_H)]
        + [_row(dg_k[0, hh * A_HD:(hh + 1) * A_HD]) for hh in range(A_KV)]
        + [jnp.zeros((PACK_COLS,), F32)] * (SMALL_ROWS - 18))

    own = lambda pack: lax.dynamic_index_in_dim(pack, me, axis=0, keepdims=False)
    (sland,) = _exchange_grads([small[None]])
    in_sh = _adamw_big(land_in, own(gpack_in), w_in[0], m_w_in[0], v_w_in[0], "adamw_w_in")
    group_sh = []
    for group, land, pack, name in ((_EARLY, land_e, gpack_e, "adamw_early"), (_LATE, land_l, gpack_l, "adamw_late")):
        packed = lambda pre: _pack_shards([args[pre + g[0]] for g in group])
        group_sh.append((group, _adamw_big(land, own(pack), packed(""), packed("m_"), packed("v_"), name)))
    g_sm, d_sm, m_sm, v_sm, loss_row = _adamw_small(
        sland, _pack_small({n: args[n] for n in _SMALL}), _pack_small({n: args["m_" + n] for n in _SMALL}),
        _pack_small({n: args["v_" + n] for n in _SMALL}))

    names = ["mix_norm", "w_in", "attn_q_norm", "attn_k_norm", "ret_decay_logit", "ret_norm_gain", "w_attn_o",
             "w_ret_o", "w_out", "mlp_norm", "w_up", "w_down", "ple_norm", "w_ple_gate", "w_ple", "final_norm"]
    like = {n: args[n] for n in _SMALL}
    outs = [loss_row[0, 0], grad_x[None]]
    for kind, sm in enumerate((g_sm, d_sm, m_sm, v_sm)):
        table = {**_unpack_small(sm, like), "w_in": in_sh[kind][None]}
        for group, res in group_sh:
            table.update(_unpack_shard(res[kind], group))
        outs += [table[n] for n in names]
    return tuple(outs)
```

```python
import functools

import jax
import jax.numpy as jnp
from jax import lax
from jax.experimental import pallas as pl
from jax.experimental.pallas import tpu as pltpu

F32 = jnp.float32
_MXU = jnp.bfloat16

D = 1024
PLE = 256
GRID_W = 64
A_HD = 64
A_H = 8
A_KV = 2
A_G = A_H // A_KV
AQ_W = A_H * A_HD
AKV_W = A_KV * A_HD
R_HD = 128
R_H = 4
R_W = R_H * R_HD
IN_W = AQ_W + 2 * AKV_W + 4 * R_W + 2 * D
PA_W = AQ_W + 2 * AKV_W
PR_W = 4 * R_W
PG_W = 2 * D
FF = 4 * D
CHUNK = 128
ROPE_THETA = 10000.0
EPS = 1e-6
GN_EPS = 1e-5
N_DEV = 8

LR, B1, B2, ADAM_EPS, WD, STEP = 0.001, 0.9, 0.999, 1e-08, 0.01, 10

LANES = 128
PACK_COLS = 1024
SMALL_ROWS = 24


def _tile(n, pref):
    t = min(n, pref)
    assert n % t == 0, (n, t)
    return t


def _bf(a):
    return a.astype(_MXU)


def _mm(a, b):
    return jnp.dot(_bf(a), _bf(b), preferred_element_type=F32)


def _mm_nt(a, b):
    return lax.dot_general(_bf(a), _bf(b), (((1,), (1,)), ((), ())), preferred_element_type=F32)


def _mm_tn(a, b):
    return lax.dot_general(_bf(a), _bf(b), (((0,), (0,)), ((), ())), preferred_element_type=F32)


def _seg_mean(v, segm):
    hi = _bf(v)
    lo = _bf(v - hi.astype(F32))
    return _mm(hi, segm) + _mm(lo, segm)


def _sigmoid(z):
    return 1.0 / (1.0 + jnp.exp(-z))


def _rms(x):
    r = lax.rsqrt(jnp.mean(x * x, axis=-1, keepdims=True) + EPS)
    return x * r, r


def _rms_bwd(n, r, gain, dy):
    dn = dy * gain
    return r * (dn - n * jnp.mean(dn * n, axis=-1, keepdims=True))


def _swap_halves(x, half):
    n = x.shape[-1]
    lane = lax.broadcasted_iota(jnp.int32, x.shape, x.ndim - 1)
    first = (lane % (2 * half)) < half
    return jnp.where(first, pltpu.roll(x, n - half, axis=1), pltpu.roll(x, half, axis=1))


def _rope(x, cos, sin, half):
    return x * cos + _swap_halves(x, half) * sin


def _cat(t, reps):
    return jnp.concatenate([t] * reps, axis=1)


def _full(shape):
    nd = len(shape)
    return pl.BlockSpec(shape, lambda *_: (0,) * nd)


def _rope_tables(seq):
    def tab(head_dim):
        n_axis = head_dim // 4
        freqs = ROPE_THETA ** (-jnp.arange(n_axis, dtype=F32) / n_axis)
        rows = seq // GRID_W
        row = jnp.repeat(jnp.arange(rows, dtype=F32), GRID_W)
        col = jnp.tile(jnp.arange(GRID_W, dtype=F32), rows)
        ang = jnp.concatenate([row[:, None] * freqs, col[:, None] * freqs], axis=-1)
        c, s = jnp.cos(ang), jnp.sin(ang)
        return jnp.concatenate([c, c], axis=-1), jnp.concatenate([-s, s], axis=-1)
    ca, sa = tab(A_HD)
    cr, sr = tab(R_HD)
    return jnp.tile(ca, (1, 2)), jnp.tile(sa, (1, 2)), cr, sr


def _seg_mean_matrix():
    i = jnp.arange(AQ_W) // A_HD
    return (i[:, None] == i[None, :]).astype(F32) / A_HD


def _mesh_pos():
    return lax.axis_index("x"), lax.axis_index("y"), lax.axis_index("c")


def _gather_steps(x_ref, out_ref, send_sems, recv_sems, local_sem, base=0):
    x, y, c = _mesh_pos()
    me, sibling = (x, y, c), (x, y, 1 - c)
    chips = [(1 - x, y), (x, 1 - y), (1 - x, 1 - y)]

    def slot(px, py, pc):
        return out_ref.at[4 * px + 2 * py + pc]

    def copy(k, block, to, src=None):
        return pltpu.make_async_remote_copy(
            src_ref=slot(*block) if src is None else src, dst_ref=slot(*block),
            send_sem=send_sems.at[base + k], recv_sem=recv_sems.at[base + k],
            device_id=to, device_id_type=pl.DeviceIdType.MESH)

    mine = pltpu.make_async_copy(x_ref, slot(*me), local_sem)
    first = [copy(0, me, sibling, src=x_ref)]
    first += [copy(1 + j, me, (*chip, c), src=x_ref) for j, chip in enumerate(chips)]
    passed = [copy(4 + j, (*chip, c), sibling) for j, chip in enumerate(chips)]

    def start():
        mine.start()
        for cp in first:
            cp.start()

    def finish():
        for j, chip in enumerate(chips):
            copy(1 + j, (*chip, c), me).wait_recv()
            passed[j].start()
        copy(0, sibling, me).wait_recv()
        for j, chip in enumerate(chips):
            copy(4 + j, (*chip, 1 - c), me).wait_recv()
        for cp in first + passed:
            cp.wait_send()
        mine.wait()

    return start, finish


def _exchange_steps(g_ref, land_ref, send_sems, recv_sems, local_sem, per_device, base=0):
    x, y, c = _mesh_pos()
    me = 4 * x + 2 * y + c

    def row(j):
        return g_ref.at[j if per_device else 0]

    def peer(k):
        p = (x ^ ((k >> 2) & 1), y ^ ((k >> 1) & 1), c ^ (k & 1))
        return p, 4 * p[0] + 2 * p[1] + p[2]

    def copy(k, src, dst):
        return pltpu.make_async_remote_copy(
            src_ref=src, dst_ref=dst, send_sem=send_sems.at[base + k - 1], recv_sem=recv_sems.at[base + k - 1],
            device_id=peer(k)[0], device_id_type=pl.DeviceIdType.MESH)

    own = pltpu.make_async_copy(row(me), land_ref.at[me], local_sem)
    sends = [copy(k, row(peer(k)[1]), land_ref.at[me]) for k in range(1, N_DEV)]

    def start():
        own.start()
        for cp in sends:
            cp.start()

    def finish():
        for k in range(1, N_DEV):
            copy(k, row(me), land_ref.at[peer(k)[1]]).wait_recv()
        for cp in sends:
            cp.wait_send()
        own.wait()

    return start, finish


_COMM_SCRATCH = [pltpu.SemaphoreType.DMA((7,)), pltpu.SemaphoreType.DMA((7,)), pltpu.SemaphoreType.DMA]
_ANY = pl.BlockSpec(memory_space=pl.ANY)


def _all_gather(shard):
    def body(x_ref, out_ref, send_sems, recv_sems, local_sem):
        start, finish = _gather_steps(x_ref, out_ref, send_sems, recv_sems, local_sem)
        start()
        finish()

    return pl.pallas_call(
        body, name="all_gather_weights", out_shape=jax.ShapeDtypeStruct((N_DEV,) + shard.shape, shard.dtype),
        in_specs=[_ANY], out_specs=_ANY, scratch_shapes=list(_COMM_SCRATCH),
    )(shard)


def _exchange_grads(packs):
    n = len(packs)

    def body(*refs):
        g_refs, land_refs = refs[:n], refs[n:2 * n]
        send_sems, recv_sems, local_sems = refs[2 * n:]
        steps = [_exchange_steps(g_refs[t], land_refs[t], send_sems, recv_sems, local_sems.at[t],
                                 packs[t].shape[0] == N_DEV, base=7 * t) for t in range(n)]
        for start, _ in steps:
            start()
        for _, finish in steps:
            finish()

    return pl.pallas_call(
        body, name="exchange_grads",
        out_shape=tuple(jax.ShapeDtypeStruct((N_DEV,) + g.shape[1:], g.dtype) for g in packs),
        in_specs=[_ANY] * n, out_specs=(_ANY,) * n,
        scratch_shapes=[pltpu.SemaphoreType.DMA((7 * n,)), pltpu.SemaphoreType.DMA((7 * n,)),
                        pltpu.SemaphoreType.DMA((n,))],
    )(*packs)


def _in_proj(x, gain, w, rest, gq, gk, seg, ca, sa, cr, sr):
    seq = x.shape[0]
    tm = _tile(seq, 256)
    nsteps = seq // tm

    def body(x_ref, g_ref, w_ref, gq_ref, gk_ref, seg_ref, ca_ref, sa_ref, cr_ref, sr_ref, rest_ref,
             pa_ref, pr_ref, pg_ref, h_ref, qh_ref, kh_ref, v_ref, rq_ref, rk_ref, gath_ref,
             send_sems, recv_sems, local_sem):
        start, finish = _gather_steps(rest_ref, gath_ref, send_sems, recv_sems, local_sem)
        pl.when(pl.program_id(0) == 0)(start)
        n, _ = _rms(x_ref[...])
        h = _bf(n * g_ref[...])
        h_ref[...] = h
        pa = _mm(h, w_ref[:, 0:PA_W])
        pr = _mm(h, w_ref[:, PA_W:PA_W + PR_W])
        pa_ref[...] = pa
        pr_ref[...] = pr
        pg_ref[...] = _mm(h, w_ref[:, PA_W + PR_W:IN_W])
        q = pa[:, 0:AQ_W]
        k = pa[:, AQ_W:AQ_W + AKV_W]
        v_ref[...] = _bf(pa[:, AQ_W + AKV_W:PA_W])
        ca_, sa_ = ca_ref[...], sa_ref[...]
        qn = q * lax.rsqrt(_seg_mean(q * q, seg_ref[...]) + EPS) * gq_ref[...]
        qh_ref[...] = _bf(_rope(qn, _cat(ca_, 4), _cat(sa_, 4), A_HD // 2) * (A_HD ** -0.5))
        kn = k * lax.rsqrt(_seg_mean(k * k, seg_ref[0:AKV_W, 0:AKV_W]) + EPS) * gk_ref[...]
        kh_ref[...] = _bf(_rope(kn, ca_, sa_, A_HD // 2))
        cr_, sr_ = _cat(cr_ref[...], 4), _cat(sr_ref[...], 4)
        rq_ref[...] = _rope(pr[:, 0:R_W], cr_, sr_, R_HD // 2) * (R_HD ** -0.5)
        rk_ref[...] = _rope(pr[:, R_W:2 * R_W], cr_, sr_, R_HD // 2)
        pl.when(pl.program_id(0) == nsteps - 1)(finish)

    row = lambda w_: pl.BlockSpec((tm, w_), lambda i: (i, 0))
    return pl.pallas_call(
        body, name="in_proj", grid=(nsteps,),
        in_specs=[row(D), _full((1, D)), _full((D, IN_W)), _full((1, AQ_W)), _full((1, AKV_W)), _full((AQ_W, AQ_W)),
                  row(LANES), row(LANES), row(LANES), row(LANES), _ANY],
        out_specs=(row(PA_W), row(PR_W), row(PG_W), row(D), row(AQ_W), row(AKV_W), row(AKV_W), row(R_W), row(R_W),
                   _ANY),
        out_shape=(jax.ShapeDtypeStruct((seq, PA_W), F32), jax.ShapeDtypeStruct((seq, PR_W), F32),
                   jax.ShapeDtypeStruct((seq, PG_W), F32), jax.ShapeDtypeStruct((seq, D), _MXU),
                   jax.ShapeDtypeStruct((seq, AQ_W), _MXU), jax.ShapeDtypeStruct((seq, AKV_W), _MXU),
                   jax.ShapeDtypeStruct((seq, AKV_W), _MXU), jax.ShapeDtypeStruct((seq, R_W), F32),
                   jax.ShapeDtypeStruct((seq, R_W), F32),
                   jax.ShapeDtypeStruct((N_DEV,) + rest.shape, rest.dtype)),
        scratch_shapes=list(_COMM_SCRATCH),
    )(x, gain, w, gq, gk, seg, ca, sa, cr, sr, rest)


def _qk_prep(pa, pr, gq, gk, seg, ca, sa, cr, sr):
    seq = pa.shape[0]
    tm = _tile(seq, 512)

    def body(pa_ref, pr_ref, gq_ref, gk_ref, seg_ref, ca_ref, sa_ref, cr_ref, sr_ref,
             qh_ref, kh_ref, v_ref, rq_ref, rk_ref):
        q = pa_ref[:, 0:AQ_W]
        k = pa_ref[:, AQ_W:AQ_W + AKV_W]
        v_ref[...] = _bf(pa_ref[:, AQ_W + AKV_W:PA_W])
        ca_, sa_ = ca_ref[...], sa_ref[...]
        msq = _seg_mean(q * q, seg_ref[...])
        qn = q * lax.rsqrt(msq + EPS) * gq_ref[...]
        qh_ref[...] = _bf(_rope(qn, _cat(ca_, 4), _cat(sa_, 4), A_HD // 2) * (A_HD ** -0.5))
        msk = _seg_mean(k * k, seg_ref[0:AKV_W, 0:AKV_W])
        kn = k * lax.rsqrt(msk + EPS) * gk_ref[...]
        kh_ref[...] = _bf(_rope(kn, ca_, sa_, A_HD // 2))
        cr_, sr_ = _cat(cr_ref[...], 4), _cat(sr_ref[...], 4)
        rq_ref[...] = _rope(pr_ref[:, 0:R_W], cr_, sr_, R_HD // 2) * (R_HD ** -0.5)
        rk_ref[...] = _rope(pr_ref[:, R_W:2 * R_W], cr_, sr_, R_HD // 2)

    row = lambda w_: pl.BlockSpec((tm, w_), lambda i: (i, 0))
    return pl.pallas_call(
        body, name="qk_prep", grid=(seq // tm,),
        in_specs=[row(PA_W), row(2 * R_W), _full((1, AQ_W)), _full((1, AKV_W)), _full((AQ_W, AQ_W)),
                  row(LANES), row(LANES), row(LANES), row(LANES)],
        out_specs=(row(AQ_W), row(AKV_W), row(AKV_W), row(R_W), row(R_W)),
        out_shape=(jax.ShapeDtypeStruct((seq, AQ_W), _MXU), jax.ShapeDtypeStruct((seq, AKV_W), _MXU),
                   jax.ShapeDtypeStruct((seq, AKV_W), _MXU), jax.ShapeDtypeStruct((seq, R_W), F32),
                   jax.ShapeDtypeStruct((seq, R_W), F32)),
    )(pa, pr, gq, gk, seg, ca, sa, cr, sr)


def _chunk_t(a, tk):
    seq = a.shape[0]
    return a.reshape(seq // tk, tk, a.shape[1]).transpose(0, 2, 1)


def _heads_to_rows(t):
    return t.transpose(2, 0, 1).reshape(t.shape[2], AQ_W)


def _attn_fwd(qt8, k2, vta, tq, tk):
    seq = k2.shape[0]
    nck = seq // tk
    rows = A_G * tq
    vrows = vta.shape[2]
    rb = _tile(tk, 256)
    assert nck % 2 == 0, nck

    def body(qt_ref, k_ref, vt_ref, o_ref, lse_ref, m_sc, acc_sc, qtp_sc, s_a, s_b, p_a, p_b, al_a, al_b):
        g = pl.program_id(0)
        qtp_sc[...] = jnp.zeros_like(qtp_sc)
        qtp_sc[pl.ds(pl.multiple_of(g * A_HD, A_HD), A_HD), :] = jnp.concatenate(
            [qt_ref[a] for a in range(A_G)], axis=1)
        m_sc[...] = jnp.full((1, rows), -jnp.inf, F32)
        acc_sc[...] = jnp.zeros_like(acc_sc)

        def scores(c):
            kc = k_ref[pl.ds(pl.multiple_of(c * tk, tk), tk), :]
            return _mm(kc, qtp_sc[...])

        def stage(c, s_cur, s_nxt, p_cur, p_prv, al_cur, al_prv, first=False, last=False):
            if not last:
                s_nxt[...] = scores(c + 1)
            if not first:
                acc_sc[...] = al_prv[...] * acc_sc[...] + _mm(vt_ref[0, c - 1], p_prv[...])
            m_old = m_sc[...]
            mx = None
            for r in range(0, tk, rb):
                bm = jnp.max(s_cur[r:r + rb, :].reshape(rb // 8, 8, rows), axis=0)
                mx = bm if mx is None else jnp.maximum(mx, bm)
            m_new = jnp.maximum(m_old, jnp.max(mx, axis=0, keepdims=True))
            for r in range(0, tk, rb):
                p_cur[r:r + rb, :] = _bf(jnp.exp(s_cur[r:r + rb, :] - m_new))
            al_cur[...] = jnp.exp(m_old - m_new)
            m_sc[...] = m_new

        s_a[...] = scores(0)
        stage(0, s_a, s_b, p_a, p_b, al_a, al_b, first=True)

        def pair(j, carry):
            stage(2 * j + 1, s_b, s_a, p_b, p_a, al_b, al_a)
            stage(2 * j + 2, s_a, s_b, p_a, p_b, al_a, al_b)
            return carry

        lax.fori_loop(0, nck // 2 - 1, pair, 0)
        stage(nck - 1, s_b, s_a, p_b, p_a, al_b, al_a, last=True)
        acc = al_b[...] * acc_sc[...] + _mm(vt_ref[0, nck - 1], p_b[...])
        l = acc[A_HD:A_HD + 1, :]
        lse = m_sc[...] + jnp.log(l)
        out = acc[0:A_HD, :] * (1.0 / l)
        for a in range(A_G):
            o_ref[a] = out[:, a * tq:(a + 1) * tq]
            lse_ref[a] = lse[:, a * tq:(a + 1) * tq]

    return pl.pallas_call(
        body, name="attn_fwd", grid=(A_KV, seq // tq),
        in_specs=[pl.BlockSpec((A_G, A_HD, tq), lambda g, i: (g, 0, i)),
                  _full((seq, LANES)), pl.BlockSpec((1, nck, vrows, tk), lambda g, i: (g, 0, 0, 0))],
        out_specs=(pl.BlockSpec((A_G, A_HD, tq), lambda g, i: (g, 0, i)),
                   pl.BlockSpec((A_G, 1, tq), lambda g, i: (g, 0, i))),
        out_shape=(jax.ShapeDtypeStruct((A_H, A_HD, seq), F32), jax.ShapeDtypeStruct((A_H, 1, seq), F32)),
        scratch_shapes=[pltpu.VMEM((1, rows), F32), pltpu.VMEM((vrows, rows), F32), pltpu.VMEM((LANES, rows), _MXU),
                        pltpu.VMEM((tk, rows), F32), pltpu.VMEM((tk, rows), F32),
                        pltpu.VMEM((tk, rows), _MXU), pltpu.VMEM((tk, rows), _MXU),
                        pltpu.VMEM((1, rows), F32), pltpu.VMEM((1, rows), F32)],
    )(qt8, k2, vta)


def _attn_bwd(qt8, ot, dot_, lse, k2, v2, k2t, tq, tk, ksplit):
    seq = k2.shape[0]
    sh = seq // ksplit
    nck = sh // tk
    rows = A_G * tq
    rb = _tile(tk, 16384 // rows)
    assert nck % 2 == 0, nck

    def body(qt_ref, ot_ref, dot_ref, lse_ref, k_ref, v_ref, kt_ref,
             dk_ref, dv_ref, dq_ref, dq_sc, qtp_sc, dotp_sc, pt_sc, dst_sc,
             s_a, s_b, dp_a, dp_b, p_a, p_b, ds_a, ds_b):
        g = pl.program_id(1)
        hrows = pl.ds(pl.multiple_of(g * A_HD, A_HD), A_HD)

        @pl.when(pl.program_id(2) == 0)
        def _():
            dk_ref[...] = jnp.zeros_like(dk_ref)
            dv_ref[...] = jnp.zeros_like(dv_ref)

        lse_row = jnp.concatenate([lse_ref[a] for a in range(A_G)], axis=1)
        dd = jnp.concatenate([jnp.sum(ot_ref[a] * dot_ref[a], axis=0, keepdims=True)
                              for a in range(A_G)], axis=1)
        qtp_sc[...] = jnp.zeros_like(qtp_sc)
        dotp_sc[...] = jnp.zeros_like(dotp_sc)
        qtp_sc[hrows, :] = jnp.concatenate([qt_ref[a] for a in range(A_G)], axis=1)
        dotp_sc[hrows, :] = _bf(jnp.concatenate([dot_ref[a] for a in range(A_G)], axis=1))
        dq_sc[...] = jnp.zeros_like(dq_sc)

        def products(c, s_ref, dp_ref):
            sl = pl.ds(pl.multiple_of(c * tk, tk), tk)
            s_ref[...] = _mm(k_ref[sl, :], qtp_sc[...])
            dp_ref[...] = _mm(v_ref[sl, :], dotp_sc[...])

        def accumulate(c, p_ref, ds_ref):
            pt_sc[...] = p_ref[...].T
            dst_sc[...] = ds_ref[...].T
            dq_sc[...] += _mm(kt_ref[c, hrows, :], ds_ref[...])
            dv_ref[0, c] += _mm(dotp_sc[hrows, :], pt_sc[...])
            dk_ref[0, c] += _mm(qtp_sc[hrows, :], dst_sc[...])

        def stage(c, s_cur, dp_cur, s_nxt, dp_nxt, p_cur, ds_cur, p_prv, ds_prv, first=False, last=False):
            if not last:
                products(c + 1, s_nxt, dp_nxt)
            if not first:
                accumulate(c - 1, p_prv, ds_prv)
            for r in range(0, tk, rb):
                p = jnp.exp(s_cur[r:r + rb, :] - lse_row)
                p_cur[r:r + rb, :] = _bf(p)
                ds_cur[r:r + rb, :] = _bf(p * (dp_cur[r:r + rb, :] - dd))

        products(0, s_a, dp_a)
        stage(0, s_a, dp_a, s_b, dp_b, p_a, ds_a, p_b, ds_b, first=True)

        def pair(j, carry):
            stage(2 * j + 1, s_b, dp_b, s_a, dp_a, p_b, ds_b, p_a, ds_a)
            stage(2 * j + 2, s_a, dp_a, s_b, dp_b, p_a, ds_a, p_b, ds_b)
            return carry

        lax.fori_loop(0, nck // 2 - 1, pair, 0)
        stage(nck - 1, s_b, dp_b, s_a, dp_a, p_b, ds_b, p_a, ds_a, last=True)
        accumulate(nck - 1, p_b, ds_b)
        for a in range(A_G):
            dq_ref[0, a] = dq_sc[:, a * tq:(a + 1) * tq]

    tspec = pl.BlockSpec((A_G, A_HD, tq), lambda s, g, i: (g, 0, i))
    kspec = pl.BlockSpec((sh, LANES), lambda s, g, i: (s, 0))
    gspec = pl.BlockSpec((1, nck, A_HD, tk), lambda s, g, i: (g, s, 0, 0))
    gshape = jax.ShapeDtypeStruct((A_KV, seq // tk, A_HD, tk), F32)
    big = lambda dt: pltpu.VMEM((tk, rows), dt)
    bigt = pltpu.VMEM((rows, tk), _MXU)
    return pl.pallas_call(
        body, name="attn_bwd", grid=(ksplit, A_KV, seq // tq),
        in_specs=[tspec, tspec, tspec, pl.BlockSpec((A_G, 1, tq), lambda s, g, i: (g, 0, i)),
                  kspec, kspec, pl.BlockSpec((nck, LANES, tk), lambda s, g, i: (s, 0, 0))],
        out_specs=(gspec, gspec, pl.BlockSpec((1, A_G, A_HD, tq), lambda s, g, i: (s, g, 0, i))),
        out_shape=(gshape, gshape, jax.ShapeDtypeStruct((ksplit, A_H, A_HD, seq), F32)),
        scratch_shapes=[pltpu.VMEM((A_HD, rows), F32), pltpu.VMEM((LANES, rows), _MXU), pltpu.VMEM((LANES, rows), _MXU),
                        bigt, bigt,
                        big(F32), big(F32), big(F32), big(F32), big(_MXU), big(_MXU), big(_MXU), big(_MXU)],
    )(qt8, ot, dot_, lse, k2, v2, k2t)


def _chunks_to_rows(t):
    return t.transpose(1, 3, 0, 2).reshape(t.shape[1] * t.shape[3], AKV_W)


def _ret_tables(zb):
    c = CHUNK

    def body(z_ref, m_ref, mw_ref, qd_ref, qdw_ref, kd_ref, kdw_ref, g_ref, gw_ref):
        fwd = pl.program_id(0) < R_H
        z = z_ref[0]
        lam = jnp.minimum(z, 0.0) - jnp.log(1.0 + jnp.exp(-jnp.abs(z)))
        i = lax.broadcasted_iota(jnp.int32, (c, c), 0).astype(F32)
        j = lax.broadcasted_iota(jnp.int32, (c, c), 1).astype(F32)
        diff = jnp.where(fwd, i - j, j - i)
        keep = diff >= jnp.where(fwd, 0.0, 1.0)
        dist = jnp.maximum(diff, 0.0)
        m = jnp.where(keep, jnp.exp(lam * dist), 0.0)
        m_ref[0] = m
        mw_ref[0] = m * dist
        fq = jnp.where(fwd, i + 1.0, c - i)
        qd = jnp.exp(lam * fq)
        qd_ref[0] = qd
        qdw_ref[0] = qd * fq
        fk = jnp.where(fwd, c - 1.0 - i, i)
        kd = jnp.exp(lam * fk)
        kd_ref[0] = kd
        kdw_ref[0] = kd * fk
        gdec = jnp.exp(lam * c)
        g_ref[0] = gdec
        gw_ref[0] = gdec * c

    big = pl.BlockSpec((1, c, c), lambda t: (t, 0, 0))
    vec = pl.BlockSpec((1, 1, LANES), lambda t: (t, 0, 0))
    bshape = jax.ShapeDtypeStruct((2 * R_H, c, c), F32)
    vshape = jax.ShapeDtypeStruct((2 * R_H, 1, LANES), F32)
    return pl.pallas_call(
        body, name="ret_tables", grid=(2 * R_H,), in_specs=[vec],
        out_specs=(big, big, big, big, big, big, vec, vec),
        out_shape=(bshape,) * 6 + (vshape, vshape),
    )(zb)


def _ret_fwd(rq, rk, pr, m, qd, kd, gdec, cb):
    seq = rq.shape[0]
    c = CHUNK
    ns = seq // (cb * c)

    def body(qf_ref, kf_ref, vf_ref, qb_ref, kb_ref, vb_ref, m_ref, qd_ref, kd_ref, g_ref,
             yf_ref, yb_ref, pstf_ref, pstb_ref, p_sc):
        @pl.when(pl.program_id(0) == 0)
        def _():
            p_sc[...] = jnp.zeros_like(p_sc)

        dirs = ((qf_ref, kf_ref, vf_ref, yf_ref, pstf_ref), (qb_ref, kb_ref, vb_ref, yb_ref, pstb_ref))
        heads = [slice(h * R_HD, (h + 1) * R_HD) for h in range(R_H)]

        def chunk(j, carry):
            early = []
            for d, (q_ref, k_ref, v_ref, _, _) in enumerate(dirs):
                cc = j if d == 0 else cb - 1 - j
                sl = pl.ds(pl.multiple_of(cc * c, c), c)
                for h, hs in enumerate(heads):
                    t = d * R_H + h
                    early.append((_mm_nt(q_ref[sl, hs], k_ref[sl, hs]), _mm(q_ref[sl, hs] * qd_ref[t], p_sc[t]),
                                  _mm_tn(k_ref[sl, hs] * kd_ref[t], v_ref[sl, hs])))
            for d, (_, _, v_ref, y_ref, pst_ref) in enumerate(dirs):
                cc = j if d == 0 else cb - 1 - j
                sl = pl.ds(pl.multiple_of(cc * c, c), c)
                for h, hs in enumerate(heads):
                    t = d * R_H + h
                    qk, qp, kv = early[t]
                    p = p_sc[t]
                    pst_ref[h, cc] = p
                    y_ref[sl, hs] = _mm(qk * m_ref[t], v_ref[sl, hs]) + qp
                    p_sc[t] = p * g_ref[t] + kv
            return carry

        lax.fori_loop(0, cb, chunk, 0)

    asc = lambda off: pl.BlockSpec((cb * c, R_W), lambda n: (n, off))
    desc = lambda off: pl.BlockSpec((cb * c, R_W), lambda n: (ns - 1 - n, off))
    return pl.pallas_call(
        body, name="ret_fwd", grid=(ns,),
        in_specs=[asc(0), asc(0), asc(2), desc(0), desc(0), desc(2),
                  _full((2 * R_H, c, c)), _full((2 * R_H, c, c)), _full((2 * R_H, c, c)), _full((2 * R_H, 1, LANES))],
        out_specs=(asc(0), desc(0),
                   pl.BlockSpec((R_H, cb, R_HD, R_HD), lambda n: (0, n, 0, 0)),
                   pl.BlockSpec((R_H, cb, R_HD, R_HD), lambda n: (0, ns - 1 - n, 0, 0))),
        out_shape=(jax.ShapeDtypeStruct((seq, R_W), F32), jax.ShapeDtypeStruct((seq, R_W), F32),
                   jax.ShapeDtypeStruct((R_H, seq // c, R_HD, R_HD), F32),
                   jax.ShapeDtypeStruct((R_H, seq // c, R_HD, R_HD), F32)),
        scratch_shapes=[pltpu.VMEM((2 * R_H, R_HD, R_HD), F32)],
    )(rq, rk, pr, rq, rk, pr, m, qd, kd, gdec)


def _ret_bwd(rq, rk, pr, dry, pstf, pstb, m, mw, qd, qdw, kd, kdw, gdec, gw, cb):
    seq = rq.shape[0]
    c = CHUNK
    ns = seq // (cb * c)

    def body(qf_ref, kf_ref, vf_ref, dyf_ref, pstf_ref, qb_ref, kb_ref, vb_ref, dyb_ref, pstb_ref,
             m_ref, mw_ref, qd_ref, qdw_ref, kd_ref, kdw_ref, g_ref, gw_ref,
             dqf_ref, dkf_ref, dvf_ref, dqb_ref, dkb_ref, dvb_ref, dlam_ref, r_sc, acc_sc, e_sc, g_sc):
        n = pl.program_id(0)

        @pl.when(n == 0)
        def _():
            r_sc[...] = jnp.zeros_like(r_sc)
            acc_sc[...] = jnp.zeros_like(acc_sc)
            e_sc[...] = jnp.zeros_like(e_sc)
            g_sc[...] = jnp.zeros_like(g_sc)

        dirs = ((qf_ref, kf_ref, vf_ref, dyf_ref, pstf_ref, dqf_ref, dkf_ref, dvf_ref),
                (qb_ref, kb_ref, vb_ref, dyb_ref, pstb_ref, dqb_ref, dkb_ref, dvb_ref))
        heads = [slice(h * R_HD, (h + 1) * R_HD) for h in range(R_H)]

        def chunk(j, carry):
            first = []
            for d, (q_ref, k_ref, v_ref, dy_ref, pst_ref, _, _, _) in enumerate(dirs):
                cc = cb - 1 - j if d == 0 else j
                sl = pl.ds(pl.multiple_of(cc * c, c), c)
                for h, hs in enumerate(heads):
                    t = d * R_H + h
                    q, k, v, dy = q_ref[sl, hs], k_ref[sl, hs], v_ref[sl, hs], dy_ref[sl, hs]
                    r = r_sc[t]
                    first.append((_mm_nt(q, k), _mm_nt(dy, v), _mm_nt(dy, pst_ref[h, cc]), _mm_nt(v, r),
                                  _mm(k * kd_ref[t], r), _mm_tn(q * qd_ref[t], dy)))
            for d, (q_ref, k_ref, _, dy_ref, pst_ref, dq_ref, dk_ref, dv_ref) in enumerate(dirs):
                cc = cb - 1 - j if d == 0 else j
                sl = pl.ds(pl.multiple_of(cc * c, c), c)
                for h, hs in enumerate(heads):
                    t = d * R_H + h
                    qk, ds, dyp, vr, kr, qdy = first[t]
                    q, k, dy = q_ref[sl, hs], k_ref[sl, hs], dy_ref[sl, hs]
                    r = r_sc[t]
                    da = ds * m_ref[t]
                    dv_ref[sl, hs] = _mm_tn(qk * m_ref[t], dy) + kr
                    dq_ref[sl, hs] = _mm(da, k) + dyp * qd_ref[t]
                    dk_ref[sl, hs] = _mm_tn(da, q) + vr * kd_ref[t]
                    acc_sc[t] += dyp * q * qdw_ref[t] + vr * k * kdw_ref[t]
                    e_sc[t] += ds * qk * mw_ref[t]
                    g_sc[t] += r * pst_ref[h, cc]
                    r_sc[t] = r * g_ref[t] + qdy
            return carry

        lax.fori_loop(0, cb, chunk, 0)

        @pl.when(n == ns - 1)
        def _():
            for t in range(2 * R_H):
                tot = jnp.sum(jnp.sum(acc_sc[t] + e_sc[t] + g_sc[t] * gw_ref[t], axis=0, keepdims=True),
                              axis=1, keepdims=True)
                dlam_ref[t] = jnp.broadcast_to(tot, (1, LANES))

    asc = lambda off: pl.BlockSpec((cb * c, R_W), lambda n: (n, off))
    desc = lambda off: pl.BlockSpec((cb * c, R_W), lambda n: (ns - 1 - n, off))
    big = _full((2 * R_H, c, c))
    vec = _full((2 * R_H, 1, LANES))
    oshape = jax.ShapeDtypeStruct((seq, R_W), F32)
    sq = pltpu.VMEM((2 * R_H, R_HD, R_HD), F32)
    return pl.pallas_call(
        body, name="ret_bwd", grid=(ns,),
        in_specs=[desc(0), desc(0), desc(2), desc(0),
                  pl.BlockSpec((R_H, cb, R_HD, R_HD), lambda n: (0, ns - 1 - n, 0, 0)),
                  asc(0), asc(0), asc(2), asc(0),
                  pl.BlockSpec((R_H, cb, R_HD, R_HD), lambda n: (0, n, 0, 0)),
                  big, big, big, big, big, big, vec, vec],
        out_specs=(desc(0), desc(0), desc(0), asc(0), asc(0), asc(0), vec),
        out_shape=(oshape,) * 6 + (jax.ShapeDtypeStruct((2 * R_H, 1, LANES), F32),),
        scratch_shapes=[sq, sq, sq, sq],
    )(rq, rk, pr, dry, pstf, rq, rk, pr, dry, pstb, m, mw, qd, qdw, kd, kdw, gdec, gw)


def _group_norm(ry):
    yn, rs = [], []
    for h in range(R_H):
        s = ry[:, h * R_HD:(h + 1) * R_HD]
        mu = jnp.mean(s, axis=-1, keepdims=True)
        cen = s - mu
        r = lax.rsqrt(jnp.mean(cen * cen, axis=-1, keepdims=True) + GN_EPS)
        yn.append(cen * r)
        rs.append(r)
    return yn, rs


def _merge_fwd(x, o, yf, yb, pr, pg, gain_r, wao, wro, wout):
    seq = x.shape[0]
    tm = _tile(seq, 512)

    def body(x_ref, o_ref, yf_ref, yb_ref, rg_ref, ga_ref, gr_ref, gn_ref, wao_ref, wro_ref, wout_ref,
             x1_ref, mg_ref, ri_ref):
        yn, _ = _group_norm(yf_ref[...] + yb_ref[...])
        rg = rg_ref[...]
        ret_in = jnp.concatenate(yn, axis=1) * gn_ref[...] * (rg * _sigmoid(rg))
        ri_ref[...] = _bf(ret_in)
        attn_out = _mm(o_ref[...], wao_ref[...])
        ret_out = _mm(ret_in, wro_ref[...])
        merged = _sigmoid(ga_ref[...]) * attn_out + _sigmoid(gr_ref[...]) * ret_out
        mg_ref[...] = _bf(merged)
        x1_ref[...] = x_ref[...] + _mm(merged, wout_ref[...])

    row = lambda w_, j=0: pl.BlockSpec((tm, w_), lambda i: (i, j))
    return pl.pallas_call(
        body, name="merge_fwd", grid=(seq // tm,),
        in_specs=[row(D), row(AQ_W), row(R_W), row(R_W), row(R_W, 3), row(D, 0), row(D, 1),
                  _full((1, R_W)), _full((AQ_W, D)), _full((R_W, D)), _full((D, D))],
        out_specs=(row(D), row(D), row(R_W)),
        out_shape=(jax.ShapeDtypeStruct((seq, D), F32), jax.ShapeDtypeStruct((seq, D), _MXU),
                   jax.ShapeDtypeStruct((seq, R_W), _MXU)),
    )(x, o, yf, yb, pr, pg, pg, gain_r, wao, wro, wout)


def _mlp_fwd(x1, gain, wup, wdown):
    seq = x1.shape[0]
    tm = _tile(seq, 1024)
    fc = 2048
    nfc = FF // fc

    def body(x_ref, g_ref, wu_ref, wd_ref, x2_ref, hm_sc, acc_sc):
        c = pl.program_id(1)

        @pl.when(c == 0)
        def _():
            n, _ = _rms(x_ref[...])
            hm_sc[...] = _bf(n * g_ref[...])
            acc_sc[...] = jnp.zeros_like(acc_sc)

        halves = (slice(0, fc // 2), slice(fc // 2, fc))
        ups = [jnp.maximum(_mm(hm_sc[...], wu_ref[:, hs]), 0.0) for hs in halves]
        acc_sc[...] += _mm(ups[0] * ups[0], wd_ref[halves[0], :]) + _mm(ups[1] * ups[1], wd_ref[halves[1], :])

        @pl.when(c == nfc - 1)
        def _():
            x2_ref[...] = x_ref[...] + acc_sc[...]

    return pl.pallas_call(
        body, name="mlp_fwd", grid=(seq // tm, nfc),
        in_specs=[pl.BlockSpec((tm, D), lambda i, c: (i, 0)), pl.BlockSpec((1, D), lambda i, c: (0, 0)),
                  pl.BlockSpec((D, fc), lambda i, c: (0, c)), pl.BlockSpec((fc, D), lambda i, c: (c, 0))],
        out_specs=pl.BlockSpec((tm, D), lambda i, c: (i, 0)),
        out_shape=jax.ShapeDtypeStruct((seq, D), F32),
        scratch_shapes=[pltpu.VMEM((tm, D), _MXU), pltpu.VMEM((tm, D), F32)],
    )(x1, gain, wup, wdown)


def _ple_loss(x2, p, tgt, g_ple, g_fin, wpg, wpgt, wple):
    seq = x2.shape[0]
    tm = _tile(seq, 512)

    def body(x2_ref, p_ref, t_ref, gp_ref, gf_ref, wpg_ref, wpgt_ref, wple_ref,
             dx2_ref, de_ref, dz_ref, hp_ref, loss_ref, dgf_ref, dgp_ref):
        @pl.when(pl.program_id(0) == 0)
        def _():
            loss_ref[...] = jnp.zeros_like(loss_ref)
            dgf_ref[...] = jnp.zeros_like(dgf_ref)
            dgp_ref[...] = jnp.zeros_like(dgp_ref)

        x2 = x2_ref[...]
        gp, gf = gp_ref[...], gf_ref[...]
        n2, r2 = _rms(x2)
        hp = _bf(n2 * gp)
        hp_ref[...] = hp
        gate = _sigmoid(_mm(hp, wpg_ref[...]))
        e = _mm(p_ref[...], wple_ref[...])
        x3 = x2 + gate * e
        n3, r3 = _rms(x3)
        diff = n3 * gf - t_ref[...]
        row_loss = jnp.mean(diff * diff, axis=-1, keepdims=True)
        loss_ref[...] += 0.5 * jnp.sum(row_loss, axis=0, keepdims=True)
        dy = diff * (1.0 / D)
        dgf_ref[...] += jnp.sum(dy * n3, axis=0, keepdims=True)
        dx3 = _rms_bwd(n3, r3, gf, dy)
        de_ref[...] = _bf(dx3 * gate)
        dz = dx3 * e * gate * (1.0 - gate)
        dz_ref[...] = _bf(dz)
        dhp = _mm(dz, wpgt_ref[...])
        dgp_ref[...] += jnp.sum(dhp * n2, axis=0, keepdims=True)
        dx2_ref[...] = dx3 + _rms_bwd(n2, r2, gp, dhp)

    row = lambda w_: pl.BlockSpec((tm, w_), lambda i: (i, 0))
    act = lambda dt: jax.ShapeDtypeStruct((seq, D), dt)
    return pl.pallas_call(
        body, name="ple_loss", grid=(seq // tm,),
        in_specs=[row(D), row(PLE), row(D), _full((1, D)), _full((1, D)),
                  _full((D, D)), _full((D, D)), _full((PLE, D))],
        out_specs=(row(D), row(D), row(D), row(D), _full((1, LANES)), _full((1, D)), _full((1, D))),
        out_shape=(act(F32), act(_MXU), act(_MXU), act(_MXU), jax.ShapeDtypeStruct((1, LANES), F32),
                   jax.ShapeDtypeStruct((1, D), F32), jax.ShapeDtypeStruct((1, D), F32)),
    )(x2, p, tgt, g_ple, g_fin, wpg, wpgt, wple)


def _mlp_bwd(x1, dx2, gain, wup, wdownt, wupt):
    seq = x1.shape[0]
    tm = _tile(seq, 512)
    fc = 2048
    nfc = FF // fc

    def body(x_ref, dx2_ref, g_ref, wu_ref, wdt_ref, wut_ref,
             dx1_ref, a_ref, du_ref, hm_ref, dg_ref, dhm_sc):
        i = pl.program_id(0)
        c = pl.program_id(1)

        @pl.when((i == 0) & (c == 0))
        def _():
            dg_ref[...] = jnp.zeros_like(dg_ref)

        @pl.when(c == 0)
        def _():
            n, _ = _rms(x_ref[...])
            hm_ref[...] = _bf(n * g_ref[...])
            dhm_sc[...] = jnp.zeros_like(dhm_sc)

        halves = (slice(0, fc // 2), slice(fc // 2, fc))
        ups = [jnp.maximum(_mm(hm_ref[...], wu_ref[:, hs]), 0.0) for hs in halves]
        das = [_mm(dx2_ref[...], wdt_ref[:, hs]) for hs in halves]
        part = None
        for u, da, hs in zip(ups, das, halves):
            a_ref[:, hs] = _bf(u * u)
            du = _bf(da * (2.0 * u))
            du_ref[:, hs] = du
            t = _mm(du, wut_ref[hs, :])
            part = t if part is None else part + t
        dhm_sc[...] += part

        @pl.when(c == nfc - 1)
        def _():
            n, r = _rms(x_ref[...])
            dhm = dhm_sc[...]
            dg_ref[...] += jnp.sum(dhm * n, axis=0, keepdims=True)
            dx1_ref[...] = dx2_ref[...] + _rms_bwd(n, r, g_ref[...], dhm)

    rowd = pl.BlockSpec((tm, D), lambda i, c: (i, 0))
    rowf = pl.BlockSpec((tm, fc), lambda i, c: (i, c))
    return pl.pallas_call(
        body, name="mlp_bwd", grid=(seq // tm, nfc),
        in_specs=[rowd, rowd, pl.BlockSpec((1, D), lambda i, c: (0, 0)),
                  pl.BlockSpec((D, fc), lambda i, c: (0, c)), pl.BlockSpec((D, fc), lambda i, c: (0, c)),
                  pl.BlockSpec((fc, D), lambda i, c: (c, 0))],
        out_specs=(rowd, rowf, rowf, rowd, pl.BlockSpec((1, D), lambda i, c: (0, 0))),
        out_shape=(jax.ShapeDtypeStruct((seq, D), F32), jax.ShapeDtypeStruct((seq, FF), _MXU),
                   jax.ShapeDtypeStruct((seq, FF), _MXU), jax.ShapeDtypeStruct((seq, D), _MXU),
                   jax.ShapeDtypeStruct((1, D), F32)),
        scratch_shapes=[pltpu.VMEM((tm, D), F32)],
    )(x1, dx2, gain, wup, wdownt, wupt)


def _merge_bwd(dx1, o, yf, yb, pr, pg, gain_r, wao, wro, woutt, waot, wrot, gpack):
    seq = dx1.shape[0]
    tm = _tile(seq, 256)
    nsteps = seq // tm

    def body(dx1_ref, o_ref, yf_ref, yb_ref, rg_ref, ga_ref, gr_ref, gn_ref, wao_ref, wro_ref,
             woutt_ref, waot_ref, wrot_ref, gpack_ref,
             dpg_ref, dao_ref, dro_ref, do_ref, dry_ref, drg_ref, dgn_ref, land_ref,
             send_sems, recv_sems, local_sem):
        start, finish = _exchange_steps(gpack_ref, land_ref, send_sems, recv_sems, local_sem, True)
        pl.when(pl.program_id(0) == 0)(start)

        @pl.when(pl.program_id(0) == 0)
        def _():
            dgn_ref[...] = jnp.zeros_like(dgn_ref)

        yn_l, rs_l = _group_norm(yf_ref[...] + yb_ref[...])
        yn = jnp.concatenate(yn_l, axis=1)
        rg = rg_ref[...]
        gn = gn_ref[...]
        sg = _sigmoid(rg)
        sil = rg * sg
        ret_in = yn * gn * sil
        attn_out = _mm(o_ref[...], wao_ref[...])
        ret_out = _mm(ret_in, wro_ref[...])
        sa = _sigmoid(ga_ref[...])
        sr = _sigmoid(gr_ref[...])
        dm = _mm(dx1_ref[...], woutt_ref[...])
        dpg_ref[:, 0:D] = _bf(dm * attn_out * sa * (1.0 - sa))
        dpg_ref[:, D:2 * D] = _bf(dm * ret_out * sr * (1.0 - sr))
        dao = _bf(dm * sa)
        dro = _bf(dm * sr)
        dao_ref[...] = dao
        dro_ref[...] = dro
        do_ref[...] = _mm(dao, waot_ref[...])
        dri = _mm(dro, wrot_ref[...])
        dgn_ref[...] += jnp.sum(dri * yn * sil, axis=0, keepdims=True)
        drg_ref[...] = _bf(dri * yn * gn * (sg * (1.0 + rg * (1.0 - sg))))
        dyn = dri * gn * sil
        dry = []
        for h in range(R_H):
            dh = dyn[:, h * R_HD:(h + 1) * R_HD]
            dry.append(rs_l[h] * (dh - jnp.mean(dh, axis=-1, keepdims=True)
                                  - yn_l[h] * jnp.mean(dh * yn_l[h], axis=-1, keepdims=True)))
        dry_ref[...] = jnp.concatenate(dry, axis=1)
        pl.when(pl.program_id(0) == nsteps - 1)(finish)

    row = lambda w_, j=0: pl.BlockSpec((tm, w_), lambda i: (i, j))
    return pl.pallas_call(
        body, name="merge_bwd", grid=(nsteps,),
        in_specs=[row(D), row(AQ_W), row(R_W), row(R_W), row(R_W, 3), row(D, 0), row(D, 1),
                  _full((1, R_W)), _full((AQ_W, D)), _full((R_W, D)), _full((D, D)),
                  _full((D, AQ_W)), _full((D, R_W)), _ANY],
        out_specs=(row(PG_W), row(D), row(D), row(AQ_W), row(R_W), row(R_W), _full((1, R_W)), _ANY),
        out_shape=(jax.ShapeDtypeStruct((seq, PG_W), _MXU), jax.ShapeDtypeStruct((seq, D), _MXU),
                   jax.ShapeDtypeStruct((seq, D), _MXU), jax.ShapeDtypeStruct((seq, AQ_W), F32),
                   jax.ShapeDtypeStruct((seq, R_W), F32), jax.ShapeDtypeStruct((seq, R_W), _MXU),
                   jax.ShapeDtypeStruct((1, R_W), F32), jax.ShapeDtypeStruct(gpack.shape, gpack.dtype)),
        scratch_shapes=list(_COMM_SCRATCH),
    )(dx1, o, yf, yb, pr, pg, pg, gain_r, wao, wro, woutt, waot, wrot, gpack)


def _qk_prep_bwd(pa, dqh, dk2, dv2, rdf, rdb, drg, gq, gk, seg, ca, sa, cr, sr):
    seq = pa.shape[0]
    tm = _tile(seq, 512)

    def body(pa_ref, dqh_ref, dk2_ref, dv2_ref, rdqf_ref, rdqb_ref, rdkf_ref, rdkb_ref, rdvf_ref, rdvb_ref,
             drg_ref, gq_ref, gk_ref, seg_ref, ca_ref, sa_ref, cr_ref, sr_ref,
             dpa_ref, dpr_ref, dgq_ref, dgk_ref):
        @pl.when(pl.program_id(0) == 0)
        def _():
            dgq_ref[...] = jnp.zeros_like(dgq_ref)
            dgk_ref[...] = jnp.zeros_like(dgk_ref)

        ca_, sa_ = ca_ref[...], sa_ref[...]

        def norm_bwd(raw, gain, dy, segm, dg_ref):
            msq = _seg_mean(raw * raw, segm)
            r = lax.rsqrt(msq + EPS)
            n = raw * r
            dg_ref[...] += jnp.sum(dy * n, axis=0, keepdims=True)
            dn = dy * gain
            return r * (dn - n * _seg_mean(dn * n, segm))

        dqn = _rope(dqh_ref[...] * (A_HD ** -0.5), _cat(ca_, 4), -_cat(sa_, 4), A_HD // 2)
        dpa_ref[:, 0:AQ_W] = _bf(norm_bwd(pa_ref[:, 0:AQ_W], gq_ref[...], dqn, seg_ref[...], dgq_ref))
        dkn = _rope(dk2_ref[...], ca_, -sa_, A_HD // 2)
        dpa_ref[:, AQ_W:AQ_W + AKV_W] = _bf(norm_bwd(pa_ref[:, AQ_W:AQ_W + AKV_W], gk_ref[...], dkn,
                                                     seg_ref[0:AKV_W, 0:AKV_W], dgk_ref))
        dpa_ref[:, AQ_W + AKV_W:PA_W] = _bf(dv2_ref[...])
        cr_, sr_ = _cat(cr_ref[...], 4), -_cat(sr_ref[...], 4)
        dpr_ref[:, 0:R_W] = _bf(_rope((rdqf_ref[...] + rdqb_ref[...]) * (R_HD ** -0.5), cr_, sr_, R_HD // 2))
        dpr_ref[:, R_W:2 * R_W] = _bf(_rope(rdkf_ref[...] + rdkb_ref[...], cr_, sr_, R_HD // 2))
        dpr_ref[:, 2 * R_W:3 * R_W] = _bf(rdvf_ref[...] + rdvb_ref[...])
        dpr_ref[:, 3 * R_W:4 * R_W] = drg_ref[...]

    row = lambda w_: pl.BlockSpec((tm, w_), lambda i: (i, 0))
    return pl.pallas_call(
        body, name="qk_prep_bwd", grid=(seq // tm,),
        in_specs=[row(PA_W), row(AQ_W), row(AKV_W), row(AKV_W), row(R_W), row(R_W), row(R_W), row(R_W),
                  row(R_W), row(R_W), row(R_W), _full((1, AQ_W)), _full((1, AKV_W)), _full((AQ_W, AQ_W)),
                  row(LANES), row(LANES), row(LANES), row(LANES)],
        out_specs=(row(PA_W), row(PR_W), _full((1, AQ_W)), _full((1, AKV_W))),
        out_shape=(jax.ShapeDtypeStruct((seq, PA_W), _MXU), jax.ShapeDtypeStruct((seq, PR_W), _MXU),
                   jax.ShapeDtypeStruct((1, AQ_W), F32), jax.ShapeDtypeStruct((1, AKV_W), F32)),
    )(pa, dqh, dk2, dv2, rdf[0], rdb[0], rdf[1], rdb[1], rdf[2], rdb[2], drg, gq, gk, seg, ca, sa, cr, sr)


def _in_proj_bwd(x, dx1, gain, dpa, dpr, dpg, wint, packs):
    seq = x.shape[0]
    tm = _tile(seq, 256)
    nsteps = seq // tm
    npk = len(packs)

    def body(x_ref, dx1_ref, g_ref, dpa_ref, dpr_ref, dpg_ref, wt_ref, *rest):
        pack_refs, (dx_ref, dg_ref), land_refs = rest[:npk], rest[npk:npk + 2], rest[npk + 2:2 * npk + 2]
        send_sems, recv_sems, local_sems = rest[2 * npk + 2:]
        steps = [_exchange_steps(pack_refs[t], land_refs[t], send_sems, recv_sems, local_sems.at[t], True, base=7 * t)
                 for t in range(npk)]

        @pl.when(pl.program_id(0) == 0)
        def _():
            for start, _ in steps:
                start()
            dg_ref[...] = jnp.zeros_like(dg_ref)

        dh = (_mm(dpa_ref[...], wt_ref[0:PA_W, :]) + _mm(dpr_ref[...], wt_ref[PA_W:PA_W + PR_W, :])
              + _mm(dpg_ref[...], wt_ref[PA_W + PR_W:IN_W, :]))
        n, r = _rms(x_ref[...])
        dg_ref[...] += jnp.sum(dh * n, axis=0, keepdims=True)
        dx_ref[...] = dx1_ref[...] + _rms_bwd(n, r, g_ref[...], dh)

        @pl.when(pl.program_id(0) == nsteps - 1)
        def _():
            for _, finish in steps:
                finish()

    row = lambda w_: pl.BlockSpec((tm, w_), lambda i: (i, 0))
    return pl.pallas_call(
        body, name="in_proj_bwd", grid=(nsteps,),
        in_specs=[row(D), row(D), _full((1, D)), row(PA_W), row(PR_W), row(PG_W), _full((IN_W, D))] + [_ANY] * npk,
        out_specs=(row(D), _full((1, D))) + (_ANY,) * npk,
        out_shape=(jax.ShapeDtypeStruct((seq, D), F32), jax.ShapeDtypeStruct((1, D), F32))
        + tuple(jax.ShapeDtypeStruct(g.shape, g.dtype) for g in packs),
        scratch_shapes=[pltpu.SemaphoreType.DMA((7 * npk,)), pltpu.SemaphoreType.DMA((7 * npk,)),
                        pltpu.SemaphoreType.DMA((npk,))],
    )(x, dx1, gain, dpa, dpr, dpg, wint, *packs)


def _wgrad(a, b, name):
    seq, m = a.shape
    n = b.shape[1]
    tm, tn, ts = _tile(m, 1024), _tile(n, 1024), _tile(seq, 2048)
    ns = seq // ts

    def body(a_ref, b_ref, o_ref):
        @pl.when(pl.program_id(2) == 0)
        def _():
            o_ref[...] = jnp.zeros_like(o_ref)

        o_ref[...] += _mm_tn(a_ref[...], b_ref[...])

    return pl.pallas_call(
        body, name=name, grid=(m // tm, n // tn, ns),
        in_specs=[pl.BlockSpec((ts, tm), lambda i, j, s: (s, i)), pl.BlockSpec((ts, tn), lambda i, j, s: (s, j))],
        out_specs=pl.BlockSpec((tm, tn), lambda i, j, s: (i, j)),
        out_shape=jax.ShapeDtypeStruct((m, n), F32),
    )(a, b)


def _adamw_math(w, g, m, v):
    m = B1 * m + (1.0 - B1) * g
    v = B2 * v + (1.0 - B2) * (g * g)
    m_hat = m / (1.0 - B1 ** STEP)
    v_hat = v / (1.0 - B2 ** STEP)
    delta = -LR * (m_hat / (jnp.sqrt(v_hat) + ADAM_EPS) + WD * w)
    return delta, m, v


def _adamw_big(land, own, w, m, v, name):
    rws, cols = w.shape
    tr = next(t for t in range(min(rws, 288) // 16 * 16, 0, -16) if rws % t == 0)

    def body(l_ref, o_ref, w_ref, m_ref, v_ref, g_ref, d_ref, nm_ref, nv_ref):
        x, y, c = _mesh_pos()
        me = 4 * x + 2 * y + c
        g = o_ref[...]
        for j in range(N_DEV):
            g = g + jnp.where(me == j, 0.0, l_ref[j].astype(F32))
        g_ref[...] = g
        d_ref[...], nm_ref[...], nv_ref[...] = _adamw_math(w_ref[...], g, m_ref[...], v_ref[...])

    row = pl.BlockSpec((tr, cols), lambda i: (i, 0))
    shp = jax.ShapeDtypeStruct((rws, cols), F32)
    return pl.pallas_call(
        body, name=name, grid=(rws // tr,),
        in_specs=[pl.BlockSpec((N_DEV, tr, cols), lambda i: (0, i, 0)), row, row, row, row],
        out_specs=(row, row, row, row), out_shape=(shp, shp, shp, shp),
    )(land, own, w, m, v)


def _adamw_small(sland, w, m, v):
    def body(l_ref, w_ref, m_ref, v_ref, g_ref, d_ref, nm_ref, nv_ref, loss_ref):
        s = l_ref[0]
        for j in range(1, N_DEV):
            s = s + l_ref[j]
        w = w_ref[...]
        gq = s[8:9]
        for h in range(1, A_H):
            gq = gq + s[8 + h:9 + h]
        gk = s[16:17] + s[17:18]
        gdec = s[5:6] * _sigmoid(-w[5:6])
        g = jnp.concatenate([s[0:5], gdec, gq, gk], axis=0)
        g_ref[...] = g
        d_ref[...], nm_ref[...], nv_ref[...] = _adamw_math(w, g, m_ref[...], v_ref[...])
        loss_ref[...] = s[6:7, 0:LANES]

    shp = jax.ShapeDtypeStruct((8, PACK_COLS), F32)
    return pl.pallas_call(
        body, name="adamw_small",
        out_shape=(shp, shp, shp, shp, jax.ShapeDtypeStruct((1, LANES), F32)),
    )(sland, w, m, v)


_BIG = (("w_attn_o", AQ_W, D, 1), ("w_ret_o", R_W, D, 1), ("w_out", D, D, 0),
        ("w_up", D, FF, 1), ("w_down", FF, D, 0), ("w_ple_gate", D, D, 0), ("w_ple", PLE, D, 1))
_LATE = _BIG[:3]
_EARLY = _BIG[3:]
IN_SHARD = IN_W // N_DEV
_SMALL = ("mix_norm", "mlp_norm", "ple_norm", "final_norm", "ret_norm_gain", "ret_decay_logit",
          "attn_q_norm", "attn_k_norm")


def _shard_shape(rows, cols, axis):
    return (rows // N_DEV, cols) if axis == 0 else (rows, cols // N_DEV)


def _pack_shards(shards):
    flat = jnp.concatenate([s.reshape(-1) for s in shards])
    return flat.reshape(-1, PACK_COLS)


def _unpack_gathered(gathered):
    flat = gathered.reshape(N_DEV, -1)
    out, off = {}, 0
    for name, rows, cols, axis in _BIG:
        sr, sc = _shard_shape(rows, cols, axis)
        blk = flat[:, off:off + sr * sc].reshape(N_DEV, sr, sc)
        off += sr * sc
        out[name] = blk.reshape(rows, cols) if axis == 0 else blk.transpose(1, 0, 2).reshape(rows, cols)
    return out


def _pack_full_grads(grads, group):
    parts = []
    for name, rows, cols, axis in group:
        sr, sc = _shard_shape(rows, cols, axis)
        g = grads[name]
        blk = g.reshape(N_DEV, sr, sc) if axis == 0 else g.reshape(rows, N_DEV, sc).transpose(1, 0, 2)
        parts.append(blk.reshape(N_DEV, -1))
    flat = jnp.concatenate(parts, axis=1)
    return flat.reshape(N_DEV, -1, PACK_COLS)


def _unpack_shard(packed, group):
    flat = packed.reshape(-1)
    out, off = {}, 0
    for name, rows, cols, axis in group:
        sr, sc = _shard_shape(rows, cols, axis)
        out[name] = flat[off:off + sr * sc].reshape(1, sr, sc)
        off += sr * sc
    return out


def _pack_small(vals):
    rows = [jnp.pad(vals[n].reshape(-1), (0, PACK_COLS - vals[n].size)) for n in _SMALL]
    return jnp.stack(rows)


def _unpack_small(packed, like):
    return {n: packed[i, :like[n].size].reshape(like[n].shape) for i, n in enumerate(_SMALL)}


def _row(v):
    return jnp.pad(v.reshape(-1), (0, PACK_COLS - v.size))


def kernel(x, p, mix_norm, w_in, attn_q_norm, attn_k_norm, ret_decay_logit, ret_norm_gain, w_attn_o, w_ret_o, w_out, mlp_norm, w_up, w_down, ple_norm, w_ple_gate, w_ple, final_norm, loss_target, m_mix_norm, m_w_in, m_attn_q_norm, m_attn_k_norm, m_ret_decay_logit, m_ret_norm_gain, m_w_attn_o, m_w_ret_o, m_w_out, m_mlp_norm, m_w_up, m_w_down, m_ple_norm, m_w_ple_gate, m_w_ple, m_final_norm, v_mix_norm, v_w_in, v_attn_q_norm, v_attn_k_norm, v_ret_decay_logit, v_ret_norm_gain, v_w_attn_o, v_w_ret_o, v_w_out, v_mlp_norm, v_w_up, v_w_down, v_ple_norm, v_w_ple_gate, v_w_ple, v_final_norm):
    args = dict(locals())
    seq = x.shape[1]
    xs = x[0]
    ps = p[0, 0]
    tgt = loss_target[0]

    big_names = [b[0] for b in _BIG]
    wshard = _pack_shards([args[n] for n in big_names])
    win = _all_gather(w_in[0].astype(_MXU)).transpose(1, 0, 2).reshape(D, IN_W)

    g_mix, g_mlp, g_ple = mix_norm, mlp_norm, ple_norm
    g_fin = final_norm.reshape(1, D)
    gq = jnp.tile(attn_q_norm, (1, A_H))
    gk = jnp.tile(attn_k_norm, (1, A_KV))
    seg = _seg_mean_matrix()
    ca, sa, cr, sr = _rope_tables(seq)

    pa, pr, pg, h, qh, kh, vh, rqh, rkh, rest_g = _in_proj(xs, g_mix, win, wshard.astype(_MXU),
                                                           gq, gk, seg, ca, sa, cr, sr)
    wfull = _unpack_gathered(rest_g)
    wao, wro, wout = wfull["w_attn_o"], wfull["w_ret_o"], wfull["w_out"]
    wup, wdown, wpg, wple = wfull["w_up"], wfull["w_down"], wfull["w_ple_gate"], wfull["w_ple"]

    tq = _tile(seq, 128)
    tk = _tile(seq // 4, 2048)
    qt8 = qh.reshape(seq, A_H, A_HD).transpose(1, 2, 0)
    vta = jnp.stack([jnp.concatenate([_chunk_t(vh[:, g * A_HD:(g + 1) * A_HD], tk),
                                      jnp.ones((seq // tk, 16, tk), _MXU)], axis=1) for g in range(A_KV)])
    ot, lse = _attn_fwd(qt8, kh, vta, tq, tk)
    o = _heads_to_rows(ot)

    zb = jnp.broadcast_to(ret_decay_logit.reshape(2 * R_H, 1, 1), (2 * R_H, 1, LANES))
    tm_, tmw, tqd, tqdw, tkd, tkdw, tg, tgw = _ret_tables(zb)
    cb = _tile(seq // CHUNK, 4)
    yf, yb, pstf, pstb = _ret_fwd(rqh, rkh, pr, tm_, tqd, tkd, tg, cb)

    x1, merged, ret_in = _merge_fwd(xs, o, yf, yb, pr, pg, ret_norm_gain, wao, wro, wout)
    x2 = _mlp_fwd(x1, g_mlp, wup, wdown)

    dx2, de, dz, hp, loss_p, dg_fin, dg_ple = _ple_loss(x2, ps, tgt, g_ple, g_fin, wpg, wpg.T, wple)
    dx1, act, du, hm, dg_mlp = _mlp_bwd(x1, dx2, g_mlp, wup, wdown.T, wup.T)
    me = 4 * lax.axis_index("x") + 2 * lax.axis_index("y") + lax.axis_index("c")
    gpack_e = _pack_full_grads({"w_up": _wgrad(hm, du, "wgrad_up"), "w_down": _wgrad(act, dx2, "wgrad_down"),
                                "w_ple_gate": _wgrad(hp, dz, "wgrad_ple_gate"), "w_ple": _wgrad(ps, de, "wgrad_ple")},
                               _EARLY)
    dpg, dao, dro, do, dry, drg, dg_gn, land_e = _merge_bwd(dx1, o, yf, yb, pr, pg, ret_norm_gain, wao, wro,
                                                            wout.T, wao.T, wro.T, _bf(gpack_e))
    *rd, dlam = _ret_bwd(rqh, rkh, pr, dry, pstf, pstb, tm_, tmw, tqd, tqdw, tkd, tkdw, tg, tgw, cb)

    ksplit = 1
    tkb = _tile(seq // 4, 512)
    dot_ = do.reshape(seq, A_H, A_HD).transpose(1, 2, 0)
    dkt, dvt, dqt = _attn_bwd(qt8, ot, dot_, lse, kh, vh, _chunk_t(kh, tkb), tq, tkb, ksplit)
    dqh = _heads_to_rows(dqt[0] if ksplit == 1 else jnp.sum(dqt, axis=0))
    dpa, dpr, dg_q, dg_k = _qk_prep_bwd(pa, dqh, _chunks_to_rows(dkt), _chunks_to_rows(dvt), rd[0:3], rd[3:6], drg, gq, gk, seg, ca, sa, cr, sr)
    wg_in = jnp.concatenate([_wgrad(h, dpa, "wgrad_in_a"), _wgrad(h, dpr, "wgrad_in_r"),
                             _wgrad(h, dpg, "wgrad_in_g")], axis=1)
    gpack_l = _pack_full_grads({"w_attn_o": _wgrad(o, dao, "wgrad_attn_o"),
                                "w_ret_o": _wgrad(ret_in, dro, "wgrad_ret_o"),
                                "w_out": _wgrad(merged, dx1, "wgrad_out")}, _LATE)
    gpack_in = wg_in.reshape(D, N_DEV, IN_SHARD).transpose(1, 0, 2)
    grad_x, dg_mix, land_in, land_l = _in_proj_bwd(xs, dx1, g_mix, dpa, dpr, dpg, win.T,
                                                   [_bf(gpack_in), _bf(gpack_l)])
    small = jnp.stack(
        [_row(dg_mix), _row(dg_mlp), _row(dg_ple), _row(dg_fin), _row(dg_gn), _row(dlam[:, 0, 0]),
         _row(loss_p[0, 0:1]), jnp.zeros((PACK_COLS,), F32)]
        + [_row(dg_q[0, hh * A_HD:(hh + 1) * A_HD]) for hh in range(A_H)]
        + [_row(dg_k[0, hh * A_HD:(hh + 1) * A_HD]) for hh in range(A_KV)]
        + [jnp.zeros((PACK_COLS,), F32)] * (SMALL_ROWS - 18))

    own = lambda pack: lax.dynamic_index_in_dim(pack, me, axis=0, keepdims=False)
    (sland,) = _exchange_grads([small[None]])
    in_sh = _adamw_big(land_in, own(gpack_in), w_in[0], m_w_in[0], v_w_in[0], "adamw_w_in")
    group_sh = []
    for group, land, pack, name in ((_EARLY, land_e, gpack_e, "adamw_early"), (_LATE, land_l, gpack_l, "adamw_late")):
        packed = lambda pre: _pack_shards([args[pre + g[0]] for g in group])
        group_sh.append((group, _adamw_big(land, own(pack), packed(""), packed("m_"), packed("v_"), name)))
    g_sm, d_sm, m_sm, v_sm, loss_row = _adamw_small(
        sland, _pack_small({n: args[n] for n in _SMALL}), _pack_small({n: args["m_" + n] for n in _SMALL}),
        _pack_small({n: args["v_" + n] for n in _SMALL}))

    names = ["mix_norm", "w_in", "attn_q_norm", "attn_k_norm", "ret_decay_logit", "ret_norm_gain", "w_attn_o",
             "w_ret_o", "w_out", "mlp_norm", "w_up", "w_down", "ple_norm", "w_ple_gate", "w_ple", "final_norm"]
    like = {n: args[n] for n in _SMALL}
    outs = [loss_row[0, 0], grad_x[None]]
    for kind, sm in enumerate((g_sm, d_sm, m_sm, v_sm)):
        table = {**_unpack_small(sm, like), "w_in": in_sh[kind][None]}
        for group, res in group_sh:
            table.update(_unpack_shard(res[kind], group))
        outs += [table[n] for n in names]
    return tuple(outs)
```
